```python
import jax, jax.numpy as jnp
from jax import lax
import numpy as np

D_MODEL = 1024
BATCH = 8
SEQ = 2048
DEPTH = 4
DEC_BATCH = 128
DEC_SEQ = 1
PAST_LEN = 16384
PAGE_SIZE = 128

N_MIXERS = 3
N_LAYERS_A = (DEPTH + 2) // 3
N_LAYERS_B = (DEPTH + 1) // 3
N_LAYERS_C = DEPTH // 3

CHUNK = 128
D_A = D_MODEL
A_HEADS = 8
A_HEAD_DIM = D_A // A_HEADS
POOL_WINDOWS = (2, 4, 8, 16)
B_GROUPS = len(POOL_WINDOWS)
D_B = D_MODEL
B_GROUP_DIM = D_B // B_GROUPS
POOL_BUF = max(POOL_WINDOWS) - 1
CONV_WIDTH = 31
D_C = D_MODEL
CONV_BUF = CONV_WIDTH - 1
N_GROUPS = 4
EXPERTS_PER_GROUP = 8
N_EXPERTS = N_GROUPS * EXPERTS_PER_GROUP
TOP_K = 2
D_EXPERT = D_MODEL // 2
MOE_BLOCK = 128
EPS = 1e-6

kernel_name = "hybrid_chunkmlp_pool_conformer_hmoe_step"


def rmsnorm(x, g):
    xf = x.astype(jnp.float32)
    y = xf * lax.rsqrt(jnp.mean(xf * xf, axis=-1, keepdims=True) + EPS)
    return y.astype(x.dtype) * g


def layernorm(x, g, b):
    xf = x.astype(jnp.float32)
    mu = jnp.mean(xf, axis=-1, keepdims=True)
    var = jnp.mean(jnp.square(xf - mu), axis=-1, keepdims=True)
    y = (xf - mu) * lax.rsqrt(var + EPS)
    return y.astype(x.dtype) * g + b


def chunk_mlp_mixer(h, w_in, ln_g, ln_b, w_s, b_s, w_out):
    n, t, _ = h.shape
    L = min(t, CHUNK)
    nc = t // L
    z = jax.nn.gelu(h @ w_in)
    u, v = jnp.split(z, 2, axis=-1)
    v = layernorm(v, ln_g, ln_b)
    mask = jnp.tril(jnp.ones((L, L), dtype=bool))
    ws = jnp.where(mask[None], w_s[:, :L, :L], 0.0).astype(v.dtype)
    vc = v.reshape(n, nc, L, A_HEADS, A_HEAD_DIM)
    mixed = jnp.einsum('hij,ncjhd->ncihd', ws, vc) + b_s[:, :L].T[None, None, :, :, None]
    y = (u * mixed.reshape(n, t, D_A)) @ w_out
    return y, v[:, t - L:]


def pool_mixer(h, buf, pos0, w_in, w_grp, b_grp, scale, w_out):
    p = h @ w_in
    n, t, _ = p.shape
    full = jnp.concatenate([buf.astype(p.dtype), p], axis=1)
    new_buf = full[:, -POOL_BUF:]
    cs = jnp.pad(jnp.cumsum(full.astype(jnp.float32), axis=1), ((0, 0), (1, 0), (0, 0)))
    pos = pos0 + jnp.arange(t)
    pf = p.astype(jnp.float32)
    groups = []
    for g, w in enumerate(POOL_WINDOWS):
        sl = slice(g * B_GROUP_DIM, (g + 1) * B_GROUP_DIM)
        end = cs[:, POOL_BUF + 1:POOL_BUF + 1 + t, sl]
        start = cs[:, POOL_BUF + 1 - w:POOL_BUF + 1 - w + t, sl]
        cnt = jnp.minimum(w, pos + 1).astype(jnp.float32)[None, :, None]
        groups.append((end - start) / cnt - pf[..., sl])
    pooled = jnp.stack(groups, axis=2).astype(h.dtype)
    mixed = jnp.einsum('ntgd,gde->ntge', pooled, w_grp) + b_grp
    y = (mixed.reshape(n, t, D_B) * scale) @ w_out
    return y, new_buf


def conv_mixer(h, buf, w_in, b_in, w_dw, b_dw, ln_g, ln_b, w_out, b_out):
    a, gt = jnp.split(h @ w_in + b_in, 2, axis=-1)
    glu = a * jax.nn.sigmoid(gt)
    full = jnp.concatenate([buf.astype(glu.dtype), glu], axis=1)
    new_buf = full[:, -CONV_BUF:]
    conv = lax.conv_general_dilated(full, w_dw[:, None, :], window_strides=(1,), padding='VALID',
                                    dimension_numbers=('NWC', 'WIO', 'NWC'),
                                    feature_group_count=D_C) + b_dw
    z = jax.nn.silu(layernorm(conv, ln_g, ln_b))
    return z @ w_out + b_out, new_buf


def routed_experts(xt, expert, weight, w_gate, w_up, w_down):
    T, d = xt.shape
    A = expert.shape[0]
    tok = (jnp.arange(A) // TOP_K).astype(jnp.int32)
    order = jnp.argsort(expert)
    e_sorted = expert[order]
    counts = jnp.bincount(expert, length=N_EXPERTS)
    padded = (counts + MOE_BLOCK - 1) // MOE_BLOCK * MOE_BLOCK
    starts = jnp.cumsum(counts) - counts
    pends = jnp.cumsum(padded)
    pstarts = pends - padded
    dest = pstarts[e_sorted] + (jnp.arange(A) - starts[e_sorted])
    n_blocks = -(-A // MOE_BLOCK) + N_EXPERTS
    P = n_blocks * MOE_BLOCK
    buf_tok = jnp.full((P,), T, jnp.int32).at[dest].set(tok[order])
    buf_w = jnp.zeros((P,), weight.dtype).at[dest].set(weight[order])
    block_expert = jnp.minimum(jnp.searchsorted(pends, jnp.arange(n_blocks) * MOE_BLOCK, side='right'),
                               N_EXPERTS - 1)
    x_pad = jnp.concatenate([xt, jnp.zeros((1, d), xt.dtype)], axis=0)

    def run_block(args):
        toks, e = args
        xb = x_pad[toks]
        hb = jax.nn.silu(xb @ w_gate[e]) * (xb @ w_up[e])
        return hb @ w_down[e]

    out = lax.map(run_block, (buf_tok.reshape(n_blocks, MOE_BLOCK), block_expert))
    out = out.reshape(P, d) * buf_w[:, None].astype(out.dtype)
    return jax.ops.segment_sum(out, buf_tok, num_segments=T + 1)[:T]


def hier_moe(h, w_grp, b_grp, w_exp, b_exp, w_gate, w_up, w_down):
    n, t, d = h.shape
    xt = h.reshape(n * t, d)
    T = xt.shape[0]
    g_logits = (xt @ w_grp + b_grp).astype(jnp.float32)
    g_prob = jax.nn.softmax(g_logits, axis=-1)
    g_sel = jnp.argmax(g_logits, axis=-1)
    g_w = jnp.take_along_axis(g_prob, g_sel[:, None], axis=-1)
    e_logits = (xt @ w_exp + b_exp).astype(jnp.float32).reshape(T, N_GROUPS, EXPERTS_PER_GROUP)
    e_in_group = jnp.take_along_axis(e_logits, g_sel[:, None, None], axis=1)[:, 0]
    top_v, top_i = lax.top_k(e_in_group, TOP_K)
    e_w = jax.nn.softmax(top_v, axis=-1) * g_w
    expert = (g_sel[:, None] * EXPERTS_PER_GROUP + top_i).astype(jnp.int32)
    y = routed_experts(xt, expert.reshape(-1), e_w.reshape(-1), w_gate, w_up, w_down)
    return y.reshape(n, t, d)


def trunk(x, c, pos0, pool_bufs, conv_bufs, ada_w, ada_b, norm_g, final_g,
          a_w_in, a_ln_g, a_ln_b, a_w_s, a_b_s, a_w_out,
          b_w_in, b_w_grp, b_b_grp, b_scale, b_w_out,
          c_w_in, c_b_in, c_w_dw, c_b_dw, c_ln_g, c_ln_b, c_w_out, c_b_out,
          moe_w_grp, moe_b_grp, moe_w_exp, moe_b_exp, moe_w_gate, moe_w_up, moe_w_down):
    new_a, new_b, new_c = [], [], []
    ia = ib = ic = 0
    for i in range(DEPTH):
        mod = (jax.nn.silu(c) @ ada_w[i] + ada_b[i])[:, None, :]
        sh1, sc1, g1, sh2, sc2, g2 = jnp.split(mod, 6, axis=-1)
        h = rmsnorm(x, norm_g[i, 0]) * (1.0 + sc1) + sh1
        kind = i % N_MIXERS
        if kind == 0:
            y, st = chunk_mlp_mixer(h, a_w_in[ia], a_ln_g[ia], a_ln_b[ia], a_w_s[ia], a_b_s[ia], a_w_out[ia])
            new_a.append(st)
            ia += 1
        elif kind == 1:
            y, st = pool_mixer(h, pool_bufs[ib], pos0, b_w_in[ib], b_w_grp[ib], b_b_grp[ib], b_scale[ib], b_w_out[ib])
            new_b.append(st)
            ib += 1
        else:
            y, st = conv_mixer(h, conv_bufs[ic], c_w_in[ic], c_b_in[ic], c_w_dw[ic], c_b_dw[ic],
                               c_ln_g[ic], c_ln_b[ic], c_w_out[ic], c_b_out[ic])
            new_c.append(st)
            ic += 1
        x = x + g1 * y
        h = rmsnorm(x, norm_g[i, 1]) * (1.0 + sc2) + sh2
        x = x + g2 * hier_moe(h, moe_w_grp[i], moe_b_grp[i], moe_w_exp[i], moe_b_exp[i],
                              moe_w_gate[i], moe_w_up[i], moe_w_down[i])
    return rmsnorm(x, final_g), jnp.stack(new_a), jnp.stack(new_b), jnp.stack(new_c)


def setup_inputs(seed: int = 0) -> dict:
    key = jax.random.key(seed)
    ks = jax.random.split(key, 40)
    cnt = [0]

    def nrm(shape, scale):
        k = ks[cnt[0]]
        cnt[0] += 1
        return jax.random.normal(k, shape, jnp.float32) * scale

    D = D_MODEL
    return {
        "x_prompt": nrm((BATCH, SEQ, D), 1.0),
        "x_sample": nrm((DEC_BATCH, DEC_SEQ, D), 1.0),
        "c_prompt": nrm((BATCH, D), 1.0),
        "c_sample": nrm((DEC_BATCH, D), 1.0),
        "state_pool": nrm((N_LAYERS_B, DEC_BATCH, POOL_BUF, D_B), 1.0),
        "state_conv": nrm((N_LAYERS_C, DEC_BATCH, CONV_BUF, D_C), 0.5),
        "ada_w": nrm((DEPTH, D, 6 * D), 0.5 * D ** -0.5),
        "ada_b": nrm((DEPTH, 6 * D), 0.02),
        "norm_g": 1.0 + nrm((DEPTH, 2, D), 0.05),
        "final_g": 1.0 + nrm((D,), 0.05),
        "a_w_in": nrm((N_LAYERS_A, D, 2 * D_A), D ** -0.5),
        "a_ln_g": 1.0 + nrm((N_LAYERS_A, D_A), 0.05),
        "a_ln_b": nrm((N_LAYERS_A, D_A), 0.02),
        "a_w_s": nrm((N_LAYERS_A, A_HEADS, CHUNK, CHUNK), 0.5 * CHUNK ** -0.5),
        "a_b_s": 1.0 + nrm((N_LAYERS_A, A_HEADS, CHUNK), 0.1),
        "a_w_out": nrm((N_LAYERS_A, D_A, D), D_A ** -0.5),
        "b_w_in": nrm((N_LAYERS_B, D, D_B), D ** -0.5),
        "b_w_grp": nrm((N_LAYERS_B, B_GROUPS, B_GROUP_DIM, B_GROUP_DIM), B_GROUP_DIM ** -0.5),
        "b_b_grp": nrm((N_LAYERS_B, B_GROUPS, B_GROUP_DIM), 0.02),
        "b_scale": 1.0 + nrm((N_LAYERS_B, D_B), 0.1),
        "b_w_out": nrm((N_LAYERS_B, D_B, D), D_B ** -0.5),
        "c_w_in": nrm((N_LAYERS_C, D, 2 * D_C), D ** -0.5),
        "c_b_in": nrm((N_LAYERS_C, 2 * D_C), 0.02),
        "c_w_dw": nrm((N_LAYERS_C, CONV_WIDTH, D_C), CONV_WIDTH ** -0.5),
        "c_b_dw": nrm((N_LAYERS_C, D_C), 0.02),
        "c_ln_g": 1.0 + nrm((N_LAYERS_C, D_C), 0.05),
        "c_ln_b": nrm((N_LAYERS_C, D_C), 0.02),
        "c_w_out": nrm((N_LAYERS_C, D_C, D), D_C ** -0.5),
        "c_b_out": nrm((N_LAYERS_C, D), 0.02),
        "moe_w_grp": nrm((DEPTH, D, N_GROUPS), D ** -0.5),
        "moe_b_grp": nrm((DEPTH, N_GROUPS), 0.01),
        "moe_w_exp": nrm((DEPTH, D, N_EXPERTS), D ** -0.5),
        "moe_b_exp": nrm((DEPTH, N_EXPERTS), 0.01),
        "moe_w_gate": nrm((DEPTH, N_EXPERTS, D, D_EXPERT), D ** -0.5),
        "moe_w_up": nrm((DEPTH, N_EXPERTS, D, D_EXPERT), D ** -0.5),
        "moe_w_down": nrm((DEPTH, N_EXPERTS, D_EXPERT, D), D_EXPERT ** -0.5),
    }


def reference(x_prompt, x_sample, c_prompt, c_sample, state_pool, state_conv,
              ada_w, ada_b, norm_g, final_g,
              a_w_in, a_ln_g, a_ln_b, a_w_s, a_b_s, a_w_out,
              b_w_in, b_w_grp, b_b_grp, b_scale, b_w_out,
              c_w_in, c_b_in, c_w_dw, c_b_dw, c_ln_g, c_ln_b, c_w_out, c_b_out,
              moe_w_grp, moe_b_grp, moe_w_exp, moe_b_exp, moe_w_gate, moe_w_up, moe_w_down):
    weights = dict(ada_w=ada_w, ada_b=ada_b, norm_g=norm_g, final_g=final_g,
                   a_w_in=a_w_in, a_ln_g=a_ln_g, a_ln_b=a_ln_b, a_w_s=a_w_s, a_b_s=a_b_s, a_w_out=a_w_out,
                   b_w_in=b_w_in, b_w_grp=b_w_grp, b_b_grp=b_b_grp, b_scale=b_scale, b_w_out=b_w_out,
                   c_w_in=c_w_in, c_b_in=c_b_in, c_w_dw=c_w_dw, c_b_dw=c_b_dw, c_ln_g=c_ln_g, c_ln_b=c_ln_b,
                   c_w_out=c_w_out, c_b_out=c_b_out,
                   moe_w_grp=moe_w_grp, moe_b_grp=moe_b_grp, moe_w_exp=moe_w_exp, moe_b_exp=moe_b_exp,
                   moe_w_gate=moe_w_gate, moe_w_up=moe_w_up, moe_w_down=moe_w_down)
    zero_pool = jnp.zeros((N_LAYERS_B, x_prompt.shape[0], POOL_BUF, D_B), x_prompt.dtype)
    zero_conv = jnp.zeros((N_LAYERS_C, x_prompt.shape[0], CONV_BUF, D_C), x_prompt.dtype)
    y_prompt, chunk_v_prompt, pool_state_prompt, conv_state_prompt = trunk(
        x_prompt, c_prompt, 0, zero_pool, zero_conv, **weights)
    y_sample, chunk_v_sample, pool_state_sample, conv_state_sample = trunk(
        x_sample, c_sample, PAST_LEN, state_pool, state_conv, **weights)
    return (y_prompt, y_sample, chunk_v_prompt, chunk_v_sample,
            pool_state_prompt, pool_state_sample, conv_state_prompt, conv_state_sample)
```

```python
import functools

import jax
import jax.numpy as jnp
from jax import lax
from jax.experimental import pallas as pl
from jax.experimental.pallas import tpu as pltpu
from jax.experimental.pallas import tpu_sc as plsc

F32 = jnp.float32
BF16 = jnp.bfloat16

D = 1024
BATCH = 8
SEQ = 2048
DEPTH = 4
DEC_BATCH = 128
PAST_LEN = 16384
CHUNK = 128
A_HEADS = 8
POOL_WINDOWS = (2, 4, 8, 16)
B_GROUP_DIM = D // len(POOL_WINDOWS)
POOL_BUF = max(POOL_WINDOWS) - 1
CONV_WIDTH = 31
CONV_BUF = CONV_WIDTH - 1
N_GROUPS = 4
EXPERTS_PER_GROUP = 8
N_EXPERTS = N_GROUPS * EXPERTS_PER_GROUP
D_EXPERT = D // 2
EPS = 1e-6

LANES = 128
SC_CORES = 2
SC_SUBCORES = 16
TM = 512
TILES_PER_SEQ = SEQ // TM
T_PROMPT = BATCH * SEQ
T_ALL = T_PROMPT + DEC_BATCH
N_TILES = -(-T_ALL // TM)
T_PAD = N_TILES * TM
N_PROMPT_TILES = T_PROMPT // TM
SAMPLE_ROWS = 32
BM = 256
N_BLOCKS = -(-(2 * T_ALL) // BM) + N_EXPERTS
P_ROWS = N_BLOCKS * BM
POOL_CARRY = 16
CONV_CARRY = 32
EXPERT_LANE0 = N_GROUPS
VMEM_LIMIT = 56 * 1024 * 1024


def _cparams(*sem):
    return pltpu.CompilerParams(dimension_semantics=sem, vmem_limit_bytes=VMEM_LIMIT)


def _rms(x, g):
    return x * lax.rsqrt(jnp.mean(x * x, axis=-1, keepdims=True) + EPS) * g


def _ln(x, g, b):
    mu = jnp.mean(x, axis=-1, keepdims=True)
    xc = x - mu
    var = jnp.mean(xc * xc, axis=-1, keepdims=True)
    return xc * lax.rsqrt(var + EPS) * g + b


def _dot(a, b):
    return jnp.dot(a, b, preferred_element_type=F32)


def _ada_kernel(c_ref, w_ref, b_ref, o_ref):
    c = c_ref[...]
    s = (c * jax.nn.sigmoid(c)).astype(BF16)
    o_ref[...] = _dot(s, w_ref[...].astype(BF16)) + b_ref[...]


def _ada_mods(c_all, ada_w, ada_b):
    n = c_all.shape[0]
    tn = 2048
    return pl.pallas_call(
        _ada_kernel,
        grid=(DEPTH, 6 * D // tn),
        in_specs=[
            pl.BlockSpec((n, D), lambda l, j: (0, 0)),
            pl.BlockSpec((None, D, tn), lambda l, j: (l, 0, j)),
            pl.BlockSpec((None, 1, tn), lambda l, j: (l, 0, j)),
        ],
        out_specs=pl.BlockSpec((None, n, tn), lambda l, j: (l, 0, j)),
        out_shape=jax.ShapeDtypeStruct((DEPTH, n, 6 * D), F32),
        compiler_params=_cparams("parallel", "parallel"),
        name="ada_mods",
    )(c_all, ada_w, ada_b.reshape(DEPTH, 1, 6 * D))


def _mod_specs_prompt(layer, chunks):
    return [
        pl.BlockSpec((None, None, 1, D), lambda i, c=c: (layer, i // TILES_PER_SEQ, 0, c))
        for c in chunks
    ]


def _mod_specs_sample(layer, chunks, rows, row_block0=0):
    return [
        pl.BlockSpec((None, rows, D), lambda i, c=c: (layer, i + row_block0, c)) for c in chunks
    ]


def _full(shape):
    nd = len(shape)
    return pl.BlockSpec(shape, lambda i: (0,) * nd)


def _layer_block(shape, layer):
    nd = len(shape)
    return pl.BlockSpec((None,) + shape, lambda i: (layer,) + (0,) * nd)


def _mix_a_front(x, sh, sc, ng, win_ref, lng_ref, lnb_ref):
    h = (_rms(x, ng) * (1.0 + sc) + sh).astype(BF16)
    z = jax.nn.gelu(_dot(h, win_ref[...]))
    u = z[:, :D]
    v = _ln(z[:, D:], lng_ref[...], lnb_ref[...])
    return u, v


def _mix_a_prompt_kernel(x_ref, sh_ref, sc_ref, gt_ref, ng_ref, win_ref, lng_ref, lnb_ref,
                         ws_ref, bs_ref, wout_ref, x1_ref, cv_ref, mixed_ref):
    i = pl.program_id(0)
    x = x_ref[...]
    u, v = _mix_a_front(x, sh_ref[...], sc_ref[...], ng_ref[...], win_ref, lng_ref, lnb_ref)
    vb = v.astype(BF16)
    n_chunks = TM // CHUNK
    row = lax.broadcasted_iota(jnp.int32, (CHUNK, CHUNK), 0)
    col = lax.broadcasted_iota(jnp.int32, (CHUNK, CHUNK), 1)
    tril = row >= col
    hd_dim = D // A_HEADS
    for hd in range(A_HEADS):
        ws = jnp.where(tril, ws_ref[hd], 0.0).astype(BF16)
        cols = slice(hd * hd_dim, (hd + 1) * hd_dim)
        vcat = jnp.concatenate([vb[c * CHUNK:(c + 1) * CHUNK, cols] for c in range(n_chunks)], axis=1)
        m = _dot(ws, vcat)
        for c in range(n_chunks):
            mixed_ref[c * CHUNK:(c + 1) * CHUNK, cols] = m[:, c * hd_dim:(c + 1) * hd_dim] + bs_ref[hd]
    y = _dot((u * mixed_ref[...]).astype(BF16), wout_ref[...])
    x1_ref[...] = x + gt_ref[...] * y

    @pl.when(i % TILES_PER_SEQ == TILES_PER_SEQ - 1)
    def _():
        cv_ref[...] = v[TM - CHUNK:, :]


def _mix_a_sample_kernel(x_ref, sh_ref, sc_ref, gt_ref, ng_ref, win_ref, lng_ref, lnb_ref,
                         wsd_ref, bsd_ref, wout_ref, x1_ref, cv_ref):
    x = x_ref[...]
    u, v = _mix_a_front(x, sh_ref[...], sc_ref[...], ng_ref[...], win_ref, lng_ref, lnb_ref)
    mixed = wsd_ref[...] * v + bsd_ref[...]
    y = _dot((u * mixed).astype(BF16), wout_ref[...])
    x1_ref[...] = x + gt_ref[...] * y
    cv_ref[...] = v


def _mixer_a(x_all, mod_p, mod_s, ng, layer, la, w_in, ln_g, ln_b, w_s, b_s, w_out):
    bs_b = jnp.broadcast_to(b_s[la][:, :, None], (A_HEADS, CHUNK, D // A_HEADS))
    common = [
        _layer_block((D, 2 * D), la), _layer_block((1, D), la), _layer_block((1, D), la),
    ]
    x_all, cv_p = pl.pallas_call(
        _mix_a_prompt_kernel,
        grid=(N_PROMPT_TILES,),
        in_specs=[pl.BlockSpec((TM, D), lambda i: (i, 0))]
        + _mod_specs_prompt(layer, (0, 1, 2))
        + [_full((1, D))] + common
        + [_layer_block((A_HEADS, CHUNK, CHUNK), la), _full((A_HEADS, CHUNK, D // A_HEADS)),
           _layer_block((D, D), la)],
        out_specs=[
            pl.BlockSpec((TM, D), lambda i: (i, 0)),
            pl.BlockSpec((None, CHUNK, D), lambda i: (i // TILES_PER_SEQ, 0, 0)),
        ],
        out_shape=[
            jax.ShapeDtypeStruct((T_PAD, D), F32),
            jax.ShapeDtypeStruct((BATCH, CHUNK, D), F32),
        ],
        scratch_shapes=[pltpu.VMEM((TM, D), F32)],
        input_output_aliases={0: 0},
        compiler_params=_cparams("arbitrary"),
        name="mixer_a_prompt",
    )(x_all, mod_p, mod_p, mod_p, ng, w_in, ln_g, ln_b, w_s, bs_b, w_out)

    wsd = jnp.repeat(w_s[la, :, 0, 0], D // A_HEADS).reshape(1, D)
    bsd = jnp.repeat(b_s[la, :, 0], D // A_HEADS).reshape(1, D)
    sblk = T_PROMPT // DEC_BATCH
    x_all, cv_s = pl.pallas_call(
        _mix_a_sample_kernel,
        grid=(1,),
        in_specs=[pl.BlockSpec((DEC_BATCH, D), lambda i: (sblk, 0))]
        + _mod_specs_sample(layer, (0, 1, 2), DEC_BATCH)
        + [_full((1, D))] + common
        + [_full((1, D)), _full((1, D)), _layer_block((D, D), la)],
        out_specs=[
            pl.BlockSpec((DEC_BATCH, D), lambda i: (sblk, 0)),
            pl.BlockSpec((DEC_BATCH, D), lambda i: (0, 0)),
        ],
        out_shape=[
            jax.ShapeDtypeStruct((T_PAD, D), F32),
            jax.ShapeDtypeStruct((DEC_BATCH, D), F32),
        ],
        input_output_aliases={0: 0},
        compiler_params=_cparams("arbitrary"),
        name="mixer_a_sample",
    )(x_all, mod_s, mod_s, mod_s, ng, w_in, ln_g, ln_b, wsd, bsd, w_out)
    return x_all, cv_p, cv_s.reshape(DEC_BATCH, 1, D)


def _mix_b_tail(pooled_groups, wgrp_ref, bgrp_ref, scale_ref, wout_ref):
    outs = [
        _dot(pg.astype(BF16), wgrp_ref[g]) + bgrp_ref[g]
        for g, pg in enumerate(pooled_groups)
    ]
    mixed = jnp.concatenate(outs, axis=1) * scale_ref[...]
    return _dot(mixed.astype(BF16), wout_ref[...])


def _mix_b_prompt_kernel(x_ref, sh_ref, sc_ref, gt_ref, ng_ref, win_ref, wgrp_ref, bgrp_ref,
                         scale_ref, wout_ref, x1_ref, st_ref, full_ref):
    i = pl.program_id(0)
    j = i % TILES_PER_SEQ
    x = x_ref[...]
    h = (_rms(x, ng_ref[...]) * (1.0 + sc_ref[...]) + sh_ref[...]).astype(BF16)
    p = _dot(h, win_ref[...])

    @pl.when(j == 0)
    def _():
        full_ref[0:POOL_CARRY, :] = jnp.zeros((POOL_CARRY, D), F32)

    full_ref[POOL_CARRY:POOL_CARRY + TM, :] = p
    pos = j * TM + lax.broadcasted_iota(jnp.int32, (TM, 1), 0)
    pooled = []
    for g, w in enumerate(POOL_WINDOWS):
        cols = slice(g * B_GROUP_DIM, (g + 1) * B_GROUP_DIM)
        s = p[:, cols]
        for k in range(1, w):
            s = s + full_ref[pl.ds(POOL_CARRY - k, TM), cols]
        cnt = jnp.minimum(w, pos + 1).astype(F32)
        pooled.append(s / cnt - p[:, cols])
    y = _mix_b_tail(pooled, wgrp_ref, bgrp_ref, scale_ref, wout_ref)
    x1_ref[...] = x + gt_ref[...] * y

    @pl.when(j == TILES_PER_SEQ - 1)
    def _():
        st_ref[...] = full_ref[pl.ds(POOL_CARRY + TM - POOL_BUF, POOL_BUF), :]

    full_ref[0:POOL_CARRY, :] = full_ref[pl.ds(TM, POOL_CARRY), :]


def _mix_b_sample_kernel(x_ref, sh_ref, sc_ref, gt_ref, ng_ref, st_ref, win_ref, wgrp_ref,
                         bgrp_ref, scale_ref, wout_ref, x1_ref, nst_ref):
    x = x_ref[...]
    h = (_rms(x, ng_ref[...]) * (1.0 + sc_ref[...]) + sh_ref[...]).astype(BF16)
    p = _dot(h, win_ref[...])
    pooled = []
    for g, w in enumerate(POOL_WINDOWS):
        s = p[:, g * B_GROUP_DIM:(g + 1) * B_GROUP_DIM]
        for k in range(1, w):
            c0 = (POOL_BUF - k) * D + g * B_GROUP_DIM
            s = s + st_ref[:, c0:c0 + B_GROUP_DIM]
        cnt = float(min(w, PAST_LEN + 1))
        pooled.append(s / cnt - p[:, g * B_GROUP_DIM:(g + 1) * B_GROUP_DIM])
    y = _mix_b_tail(pooled, wgrp_ref, bgrp_ref, scale_ref, wout_ref)
    x1_ref[...] = x + gt_ref[...] * y
    nst_ref[:, 0:(POOL_BUF - 1) * D] = st_ref[:, D:POOL_BUF * D]
    nst_ref[:, (POOL_BUF - 1) * D:POOL_BUF * D] = p


def _mixer_b(x_all, mod_p, mod_s, ng, layer, lb, state, w_in, w_grp, b_grp, scale, w_out):
    gd = B_GROUP_DIM
    common = [
        _layer_block((D, D), lb), _layer_block((len(POOL_WINDOWS), gd, gd), lb),
        _layer_block((len(POOL_WINDOWS), 1, gd), lb), _layer_block((1, D), lb),
        _layer_block((D, D), lb),
    ]
    x_all, st_p = pl.pallas_call(
        _mix_b_prompt_kernel,
        grid=(N_PROMPT_TILES,),
        in_specs=[pl.BlockSpec((TM, D), lambda i: (i, 0))]
        + _mod_specs_prompt(layer, (0, 1, 2)) + [_full((1, D))] + common,
        out_specs=[
            pl.BlockSpec((TM, D), lambda i: (i, 0)),
            pl.BlockSpec((None, POOL_BUF, D), lambda i: (i // TILES_PER_SEQ, 0, 0)),
        ],
        out_shape=[
            jax.ShapeDtypeStruct((T_PAD, D), F32),
            jax.ShapeDtypeStruct((BATCH, POOL_BUF, D), F32),
        ],
        scratch_shapes=[pltpu.VMEM((TM + POOL_CARRY, D), F32)],
        input_output_aliases={0: 0},
        compiler_params=_cparams("arbitrary"),
        name="mixer_b_prompt",
    )(x_all, mod_p, mod_p, mod_p, ng, w_in, w_grp, b_grp, scale, w_out)

    r = SAMPLE_ROWS
    sblk = T_PROMPT // r
    st2 = state[lb].reshape(DEC_BATCH, POOL_BUF * D)
    x_all, st_s = pl.pallas_call(
        _mix_b_sample_kernel,
        grid=(DEC_BATCH // r,),
        in_specs=[pl.BlockSpec((r, D), lambda i: (sblk + i, 0))]
        + _mod_specs_sample(layer, (0, 1, 2), r) + [_full((1, D))]
        + [pl.BlockSpec((r, POOL_BUF * D), lambda i: (i, 0))] + common,
        out_specs=[
            pl.BlockSpec((r, D), lambda i: (sblk + i, 0)),
            pl.BlockSpec((r, POOL_BUF * D), lambda i: (i, 0)),
        ],
        out_shape=[
            jax.ShapeDtypeStruct((T_PAD, D), F32),
            jax.ShapeDtypeStruct((DEC_BATCH, POOL_BUF * D), F32),
        ],
        input_output_aliases={0: 0},
        compiler_params=_cparams("arbitrary"),
        name="mixer_b_sample",
    )(x_all, mod_s, mod_s, mod_s, ng, st2, w_in, w_grp, b_grp, scale, w_out)
    return x_all, st_p, st_s.reshape(DEC_BATCH, POOL_BUF, D)


def _mix_c_glu(x, sh, sc, ng, win_ref, bin_ref):
    h = (_rms(x, ng) * (1.0 + sc) + sh).astype(BF16)
    ag = _dot(h, win_ref[...]) + bin_ref[...]
    return ag[:, :D] * jax.nn.sigmoid(ag[:, D:])


def _mix_c_tail(conv, lng_ref, lnb_ref, wout_ref, bout_ref):
    z = _ln(conv, lng_ref[...], lnb_ref[...])
    z = z * jax.nn.sigmoid(z)
    return _dot(z.astype(BF16), wout_ref[...]) + bout_ref[...]


def _mix_c_prompt_kernel(x_ref, sh_ref, sc_ref, gt_ref, ng_ref, win_ref, bin_ref, wdw_ref,
                         bdw_ref, lng_ref, lnb_ref, wout_ref, bout_ref, x1_ref, st_ref,
                         full_ref, conv_ref):
    i = pl.program_id(0)
    j = i % TILES_PER_SEQ
    x = x_ref[...]
    glu = _mix_c_glu(x, sh_ref[...], sc_ref[...], ng_ref[...], win_ref, bin_ref)

    @pl.when(j == 0)
    def _():
        full_ref[0:CONV_CARRY, :] = jnp.zeros((CONV_CARRY, D), F32)

    full_ref[CONV_CARRY:CONV_CARRY + TM, :] = glu
    off = CONV_CARRY - CONV_BUF
    for c in range(D // LANES):
        cols = slice(c * LANES, (c + 1) * LANES)
        acc = full_ref[pl.ds(off, TM), cols] * wdw_ref[0:1, cols]
        for k in range(1, CONV_WIDTH):
            acc = acc + full_ref[pl.ds(off + k, TM), cols] * wdw_ref[k:k + 1, cols]
        conv_ref[:, cols] = acc + bdw_ref[:, cols]
    y = _mix_c_tail(conv_ref[...], lng_ref, lnb_ref, wout_ref, bout_ref)
    x1_ref[...] = x + gt_ref[...] * y

    @pl.when(j == TILES_PER_SEQ - 1)
    def _():
        st_ref[...] = full_ref[pl.ds(CONV_CARRY + TM - CONV_BUF, CONV_BUF), :]

    full_ref[0:CONV_CARRY, :] = full_ref[pl.ds(TM, CONV_CARRY), :]


def _mix_c_sample_kernel(x_ref, sh_ref, sc_ref, gt_ref, ng_ref, st_ref, win_ref, bin_ref,
                         wdw_ref, bdw_ref, lng_ref, lnb_ref, wout_ref, bout_ref, x1_ref, nst_ref):
    x = x_ref[...]
    glu = _mix_c_glu(x, sh_ref[...], sc_ref[...], ng_ref[...], win_ref, bin_ref)
    acc = glu * wdw_ref[CONV_BUF:CONV_BUF + 1, :]
    for k in range(CONV_BUF):
        acc = acc + st_ref[:, k * D:(k + 1) * D] * wdw_ref[k:k + 1, :]
    y = _mix_c_tail(acc + bdw_ref[...], lng_ref, lnb_ref, wout_ref, bout_ref)
    x1_ref[...] = x + gt_ref[...] * y
    nst_ref[:, 0:(CONV_BUF - 1) * D] = st_ref[:, D:CONV_BUF * D]
    nst_ref[:, (CONV_BUF - 1) * D:CONV_BUF * D] = glu


def _mixer_c(x_all, mod_p, mod_s, ng, layer, lc, state, w_in, b_in, w_dw, b_dw, ln_g, ln_b,
             w_out, b_out):
    common = [
        _layer_block((D, 2 * D), lc), _layer_block((1, 2 * D), lc),
        _layer_block((CONV_WIDTH, D), lc), _layer_block((1, D), lc), _layer_block((1, D), lc),
        _layer_block((1, D), lc), _layer_block((D, D), lc), _layer_block((1, D), lc),
    ]
    x_all, st_p = pl.pallas_call(
        _mix_c_prompt_kernel,
        grid=(N_PROMPT_TILES,),
        in_specs=[pl.BlockSpec((TM, D), lambda i: (i, 0))]
        + _mod_specs_prompt(layer, (0, 1, 2)) + [_full((1, D))] + common,
        out_specs=[
            pl.BlockSpec((TM, D), lambda i: (i, 0)),
            pl.BlockSpec((None, CONV_BUF, D), lambda i: (i // TILES_PER_SEQ, 0, 0)),
        ],
        out_shape=[
            jax.ShapeDtypeStruct((T_PAD, D), F32),
            jax.ShapeDtypeStruct((BATCH, CONV_BUF, D), F32),
        ],
        scratch_shapes=[pltpu.VMEM((TM + CONV_CARRY, D), F32), pltpu.VMEM((TM, D), F32)],
        input_output_aliases={0: 0},
        compiler_params=_cparams("arbitrary"),
        name="mixer_c_prompt",
    )(x_all, mod_p, mod_p, mod_p, ng, w_in, b_in, w_dw, b_dw, ln_g, ln_b, w_out, b_out)

    r = SAMPLE_ROWS
    sblk = T_PROMPT // r
    st2 = state[lc].reshape(DEC_BATCH, CONV_BUF * D)
    x_all, st_s = pl.pallas_call(
        _mix_c_sample_kernel,
        grid=(DEC_BATCH // r,),
        in_specs=[pl.BlockSpec((r, D), lambda i: (sblk + i, 0))]
        + _mod_specs_sample(layer, (0, 1, 2), r) + [_full((1, D))]
        + [pl.BlockSpec((r, CONV_BUF * D), lambda i: (i, 0))] + common,
        out_specs=[
            pl.BlockSpec((r, D), lambda i: (sblk + i, 0)),
            pl.BlockSpec((r, CONV_BUF * D), lambda i: (i, 0)),
        ],
        out_shape=[
            jax.ShapeDtypeStruct((T_PAD, D), F32),
            jax.ShapeDtypeStruct((DEC_BATCH, CONV_BUF * D), F32),
        ],
        input_output_aliases={0: 0},
        compiler_params=_cparams("arbitrary"),
        name="mixer_c_sample",
    )(x_all, mod_s, mod_s, mod_s, ng, st2, w_in, b_in, w_dw, b_dw, ln_g, ln_b, w_out, b_out)
    return x_all, st_p, st_s.reshape(DEC_BATCH, CONV_BUF, D)


def _tile_mod(i, p_ref, s_ref):
    s_rows = jnp.concatenate([s_ref[...], jnp.zeros((TM - DEC_BATCH, D), F32)], axis=0)
    return jnp.where(i >= N_PROMPT_TILES, s_rows, p_ref[...])


def _unified_mod_specs(layer, chunks):
    specs = []
    for c in chunks:
        specs.append(pl.BlockSpec(
            (None, None, 1, D),
            lambda i, c=c: (layer, jnp.minimum(i // TILES_PER_SEQ, BATCH - 1), 0, c)))
        specs.append(pl.BlockSpec((None, DEC_BATCH, D), lambda i, c=c: (layer, 0, c)))
    return specs


def _router_kernel(x_ref, shp_ref, shs_ref, scp_ref, scs_ref, ng_ref, wr_ref, br_ref,
                   h2_ref, idx_ref, wts_ref, cnt_ref, carry_ref):
    i = pl.program_id(0)

    @pl.when(i == 0)
    def _():
        carry_ref[...] = jnp.zeros((1, LANES), F32)

    sh = _tile_mod(i, shp_ref, shs_ref)
    sc = _tile_mod(i, scp_ref, scs_ref)
    h2 = _rms(x_ref[...], ng_ref[...]) * (1.0 + sc) + sh
    h2_ref[...] = h2
    lg = jnp.dot(h2, wr_ref[...], preferred_element_type=F32,
                 precision=lax.Precision.HIGHEST) + br_ref[...]
    lane = lax.broadcasted_iota(jnp.int32, (TM, LANES), 1)
    neg = -jnp.inf
    is_g = lane < N_GROUPS
    gm = jnp.where(is_g, lg, neg)
    gmax = jnp.max(gm, axis=1, keepdims=True)
    lane_f = lane.astype(F32)
    far = float(LANES)
    gsel = jnp.min(jnp.where(gm == gmax, lane_f, far), axis=1, keepdims=True).astype(jnp.int32)
    gsum = jnp.sum(jnp.where(is_g, jnp.exp(lg - gmax), 0.0), axis=1, keepdims=True)
    g_w = 1.0 / gsum
    e_lane = lane - EXPERT_LANE0
    in_grp = (e_lane >= 0) & (e_lane < N_EXPERTS) & (
        lax.shift_right_arithmetic(e_lane, EXPERTS_PER_GROUP.bit_length() - 1) == gsel)
    em = jnp.where(in_grp, lg, neg)
    m1 = jnp.max(em, axis=1, keepdims=True)
    i1 = jnp.min(jnp.where(em == m1, lane_f, far), axis=1, keepdims=True).astype(jnp.int32)
    em2 = jnp.where(lane == i1, neg, em)
    m2 = jnp.max(em2, axis=1, keepdims=True)
    i2 = jnp.min(jnp.where(em2 == m2, lane_f, far), axis=1, keepdims=True).astype(jnp.int32)
    e2 = jnp.exp(m2 - m1)
    den = 1.0 + e2
    w1 = (1.0 / den) * g_w
    w2 = (e2 / den) * g_w

    valid = (i * TM + lax.broadcasted_iota(jnp.int32, (TM, 1), 0)) < T_ALL
    hit1 = (lane == i1) & valid
    hit2 = (lane == i2) & valid
    assign = jnp.where(hit1 | hit2, 1.0, 0.0).astype(BF16)
    r = lax.broadcasted_iota(jnp.int32, (TM, TM), 0)
    c = lax.broadcasted_iota(jnp.int32, (TM, TM), 1)
    before = (c < r).astype(BF16)
    seen = _dot(before, assign) + carry_ref[...]
    rank1 = jnp.sum(jnp.where(hit1, seen, 0.0), axis=1, keepdims=True).astype(jnp.int32)
    rank2 = jnp.sum(jnp.where(hit2, seen, 0.0), axis=1, keepdims=True).astype(jnp.int32)
    carry_ref[...] = carry_ref[...] + jnp.sum(assign.astype(F32), axis=0, keepdims=True)
    cnt_ref[...] = carry_ref[...]

    ex1 = jnp.where(valid, i1 - EXPERT_LANE0, 0)
    ex2 = jnp.where(valid, i2 - EXPERT_LANE0, 0)
    idx_ref[...] = jnp.where(lane == 0, ex1, jnp.where(lane == 1, ex2,
                             jnp.where(lane == 2, rank1, jnp.where(lane == 3, rank2, 0))))
    wts_ref[...] = jnp.where(lane == 0, jnp.where(valid, w1, 0.0),
                             jnp.where(lane == 1, jnp.where(valid, w2, 0.0), 0.0))


def _router(x_all, mod_p, mod_s, ng, layer, w_r, b_r):
    return pl.pallas_call(
        _router_kernel,
        grid=(N_TILES,),
        in_specs=[pl.BlockSpec((TM, D), lambda i: (i, 0))]
        + _unified_mod_specs(layer, (3, 4))
        + [_full((1, D)), _full((D, LANES)), _full((1, LANES))],
        out_specs=[
            pl.BlockSpec((TM, D), lambda i: (i, 0)),
            pl.BlockSpec((TM, LANES), lambda i: (i, 0)),
            pl.BlockSpec((TM, LANES), lambda i: (i, 0)),
            pl.BlockSpec((1, LANES), lambda i: (0, 0)),
        ],
        out_shape=[
            jax.ShapeDtypeStruct((T_PAD, D), F32),
            jax.ShapeDtypeStruct((T_PAD, LANES), jnp.int32),
            jax.ShapeDtypeStruct((T_PAD, LANES), F32),
            jax.ShapeDtypeStruct((1, LANES), F32),
        ],
        scratch_shapes=[pltpu.VMEM((1, LANES), F32)],
        compiler_params=_cparams("arbitrary"),
        name="router",
    )(x_all, mod_p, mod_s, mod_p, mod_s, ng, w_r, b_r)


def _gather_chunk(per_worker):
    for ch in range(64, 7, -8):
        if per_worker % ch == 0:
            return ch
    raise ValueError(per_worker)


def _sc_gather(table, idx):
    nc, nw = SC_CORES, SC_CORES * SC_SUBCORES
    m = idx.shape[0]
    per_w = m // nw
    assert per_w * nw == m
    ch = _gather_chunk(per_w)
    width = table.shape[1]
    mesh = plsc.VectorSubcoreMesh(core_axis_name="c", subcore_axis_name="s")

    @functools.partial(
        pl.kernel,
        out_type=jax.ShapeDtypeStruct((m, width), table.dtype),
        mesh=mesh,
        scratch_types=[
            pltpu.VMEM((ch,), jnp.int32),
            pltpu.VMEM((ch, width), table.dtype),
            pltpu.SemaphoreType.DMA,
        ],
    )
    def gather_kernel(t_hbm, i_hbm, o_hbm, idx_v, rows_v, sem):
        wid = lax.axis_index("s") * nc + lax.axis_index("c")
        base = wid * per_w

        @pl.loop(0, per_w // ch)
        def _(j):
            off = pl.multiple_of(base + j * ch, 8)
            pltpu.sync_copy(i_hbm.at[pl.ds(off, ch)], idx_v)
            pltpu.async_copy(t_hbm.at[idx_v], rows_v, sem).wait()
            pltpu.sync_copy(rows_v, o_hbm.at[pl.ds(off, ch)])

    return gather_kernel(table, idx)


def _expert_kernel(be_ref, first_ref, nused_ref, x_ref, wg_ref, wu_ref, wd_ref, y_ref,
                   wgb_ref, wub_ref, wdb_ref):
    b = pl.program_id(0)

    @pl.when(b < nused_ref[0])
    def _():
        @pl.when(first_ref[b] == 1)
        def _():
            wgb_ref[...] = wg_ref[...].astype(BF16)
            wub_ref[...] = wu_ref[...].astype(BF16)
            wdb_ref[...] = wd_ref[...].astype(BF16)

        x = x_ref[...].astype(BF16)
        g = _dot(x, wgb_ref[...])
        u = _dot(x, wub_ref[...])
        hmid = (g * jax.nn.sigmoid(g)) * u
        y_ref[...] = _dot(hmid.astype(BF16), wdb_ref[...])


def _experts(x_sorted, block_expert, first, nused, layer, w_gate, w_up, w_down):
    def row_map(b, be, fi, nu):
        return (jnp.minimum(b, nu[0] - 1), 0)

    def w_map(b, be, fi, nu):
        return (layer, be[b], 0, 0)

    grid_spec = pltpu.PrefetchScalarGridSpec(
        num_scalar_prefetch=3,
        grid=(N_BLOCKS,),
        in_specs=[
            pl.BlockSpec((BM, D), row_map),
            pl.BlockSpec((None, None, D, D_EXPERT), w_map),
            pl.BlockSpec((None, None, D, D_EXPERT), w_map),
            pl.BlockSpec((None, None, D_EXPERT, D), w_map),
        ],
        out_specs=pl.BlockSpec((BM, D), row_map),
        scratch_shapes=[
            pltpu.VMEM((D, D_EXPERT), BF16),
            pltpu.VMEM((D, D_EXPERT), BF16),
            pltpu.VMEM((D_EXPERT, D), BF16),
        ],
    )
    return pl.pallas_call(
        _expert_kernel,
        grid_spec=grid_spec,
        out_shape=jax.ShapeDtypeStruct((P_ROWS, D), F32),
        compiler_params=_cparams("arbitrary"),
        name="experts",
    )(block_expert, first, nused, x_sorted, w_gate, w_up, w_down)


def _combine_kernel(x_ref, g0_ref, g1_ref, wts_ref, gtp_ref, gts_ref, o_ref):
    i = pl.program_id(0)
    gt = _tile_mod(i, gtp_ref, gts_ref)
    w = wts_ref[...]
    moe = w[:, 0:1] * g0_ref[...] + w[:, 1:2] * g1_ref[...]
    o_ref[...] = x_ref[...] + gt * moe


def _combine(x_all, gathered, wts, mod_p, mod_s, layer):
    return pl.pallas_call(
        _combine_kernel,
        grid=(N_TILES,),
        in_specs=[
            pl.BlockSpec((TM, D), lambda i: (i, 0)),
            pl.BlockSpec((TM, D), lambda i: (i, 0)),
            pl.BlockSpec((TM, D), lambda i: (i + N_TILES, 0)),
            pl.BlockSpec((TM, LANES), lambda i: (i, 0)),
        ] + _unified_mod_specs(layer, (5,)),
        out_specs=pl.BlockSpec((TM, D), lambda i: (i, 0)),
        out_shape=jax.ShapeDtypeStruct((T_PAD, D), F32),
        input_output_aliases={0: 0},
        compiler_params=_cparams("arbitrary"),
        name="combine",
    )(x_all, gathered, gathered, wts, mod_p, mod_s)


def _moe(x_all, mod_p, mod_s, ng, layer, w_r, b_r, w_gate, w_up, w_down):
    h2, idx, wts, cnt = _router(x_all, mod_p, mod_s, ng, layer, w_r, b_r)
    counts = cnt[0, EXPERT_LANE0:EXPERT_LANE0 + N_EXPERTS].astype(jnp.int32)
    padded = (counts + BM - 1) // BM * BM
    pend = jnp.cumsum(padded)
    pstart = pend - padded
    expert = idx[:, 0:2]
    dest = pstart[expert] + idx[:, 2:4]
    valid = (jnp.arange(T_PAD) < T_ALL)[:, None]
    tok = jnp.broadcast_to(jnp.arange(T_PAD, dtype=jnp.int32)[:, None], (T_PAD, 2))
    slot_tok = jnp.zeros((P_ROWS,), jnp.int32).at[
        jnp.where(valid, dest, P_ROWS).reshape(-1)].set(tok.reshape(-1), mode="drop")
    dest_g = jnp.where(valid, dest, 0).astype(jnp.int32)
    nused = (pend[-1] // BM).astype(jnp.int32).reshape(1)
    block_expert = jnp.minimum(
        jnp.searchsorted(pend, jnp.arange(N_BLOCKS, dtype=jnp.int32) * BM, side="right"),
        N_EXPERTS - 1).astype(jnp.int32)
    first = jnp.concatenate(
        [jnp.ones((1,), jnp.int32), (block_expert[1:] != block_expert[:-1]).astype(jnp.int32)])

    x_sorted = _sc_gather(h2, slot_tok)
    y_sorted = _experts(x_sorted, block_expert, first, nused, layer, w_gate, w_up, w_down)
    gathered = _sc_gather(y_sorted, jnp.concatenate([dest_g[:, 0], dest_g[:, 1]]))
    return _combine(x_all, gathered, wts, mod_p, mod_s, layer)


def _final_kernel(x_ref, g_ref, yp_ref, ys_ref):
    i = pl.program_id(0)
    y = _rms(x_ref[...], g_ref[...])

    @pl.when(i < N_PROMPT_TILES)
    def _():
        yp_ref[...] = y

    @pl.when(i == N_PROMPT_TILES)
    def _():
        ys_ref[...] = y[0:DEC_BATCH, :]


def _final_norm(x_all, g):
    return pl.pallas_call(
        _final_kernel,
        grid=(N_PROMPT_TILES + 1,),
        in_specs=[pl.BlockSpec((TM, D), lambda i: (i, 0)), _full((1, D))],
        out_specs=[
            pl.BlockSpec((TM, D), lambda i: (jnp.minimum(i, N_PROMPT_TILES - 1), 0)),
            pl.BlockSpec((DEC_BATCH, D), lambda i: (0, 0)),
        ],
        out_shape=[
            jax.ShapeDtypeStruct((T_PROMPT, D), F32),
            jax.ShapeDtypeStruct((DEC_BATCH, D), F32),
        ],
        compiler_params=_cparams("arbitrary"),
        name="final_norm",
    )(x_all, g)


def kernel(x_prompt, x_sample, c_prompt, c_sample, state_pool, state_conv, ada_w, ada_b, norm_g, final_g, a_w_in, a_ln_g, a_ln_b, a_w_s, a_b_s, a_w_out, b_w_in, b_w_grp, b_b_grp, b_scale, b_w_out, c_w_in, c_b_in, c_w_dw, c_b_dw, c_ln_g, c_ln_b, c_w_out, c_b_out, moe_w_grp, moe_b_grp, moe_w_exp, moe_b_exp, moe_w_gate, moe_w_up, moe_w_down):
    x_all = jnp.concatenate([
        x_prompt.reshape(T_PROMPT, D), x_sample.reshape(DEC_BATCH, D),
        jnp.zeros((T_PAD - T_ALL, D), F32)], axis=0)
    mods = _ada_mods(jnp.concatenate([c_prompt, c_sample], axis=0), ada_w, ada_b)
    mod_p = mods[:, :BATCH].reshape(DEPTH, BATCH, 1, 6 * D)
    mod_s = mods[:, BATCH:]

    a_w_in_b, a_w_out_b = a_w_in.astype(BF16), a_w_out.astype(BF16)
    b_w_in_b, b_w_grp_b, b_w_out_b = b_w_in.astype(BF16), b_w_grp.astype(BF16), b_w_out.astype(BF16)
    c_w_in_b, c_w_out_b = c_w_in.astype(BF16), c_w_out.astype(BF16)
    w_r = jnp.pad(jnp.concatenate([moe_w_grp, moe_w_exp], axis=2),
                  ((0, 0), (0, 0), (0, LANES - N_GROUPS - N_EXPERTS)))
    b_r = jnp.pad(jnp.concatenate([moe_b_grp, moe_b_exp], axis=1),
                  ((0, 0), (0, LANES - N_GROUPS - N_EXPERTS)))

    new_a_p, new_a_s, new_b_p, new_b_s, new_c_p, new_c_s = [], [], [], [], [], []
    ia = ib = ic = 0
    for layer in range(DEPTH):
        ng1 = norm_g[layer, 0].reshape(1, D)
        ng2 = norm_g[layer, 1].reshape(1, D)
        kind = layer % 3
        if kind == 0:
            x_all, st_p, st_s = _mixer_a(
                x_all, mod_p, mod_s, ng1, layer, ia, a_w_in_b, a_ln_g.reshape(-1, 1, D),
                a_ln_b.reshape(-1, 1, D), a_w_s, a_b_s, a_w_out_b)
            new_a_p.append(st_p)
            new_a_s.append(st_s)
            ia += 1
        elif kind == 1:
            x_all, st_p, st_s = _mixer_b(
                x_all, mod_p, mod_s, ng1, layer, ib, state_pool, b_w_in_b, b_w_grp_b,
                b_b_grp.reshape(-1, len(POOL_WINDOWS), 1, B_GROUP_DIM),
                b_scale.reshape(-1, 1, D), b_w_out_b)
            new_b_p.append(st_p)
            new_b_s.append(st_s)
            ib += 1
        else:
            x_all, st_p, st_s = _mixer_c(
                x_all, mod_p, mod_s, ng1, layer, ic, state_conv, c_w_in_b,
                c_b_in.reshape(-1, 1, 2 * D), c_w_dw, c_b_dw.reshape(-1, 1, D),
                c_ln_g.reshape(-1, 1, D), c_ln_b.reshape(-1, 1, D), c_w_out_b,
                c_b_out.reshape(-1, 1, D))
            new_c_p.append(st_p)
            new_c_s.append(st_s)
            ic += 1
        x_all = _moe(x_all, mod_p, mod_s, ng2, layer, w_r[layer], b_r[layer].reshape(1, LANES),
                     moe_w_gate, moe_w_up, moe_w_down)

    y_p, y_s = _final_norm(x_all, final_g.reshape(1, D))
    return (y_p.reshape(BATCH, SEQ, D), y_s.reshape(DEC_BATCH, 1, D),
            jnp.stack(new_a_p), jnp.stack(new_a_s), jnp.stack(new_b_p), jnp.stack(new_b_s),
            jnp.stack(new_c_p), jnp.stack(new_c_s))
```

```python
import functools

import jax
import jax.numpy as jnp
from jax import lax
from jax.experimental import pallas as pl
from jax.experimental.pallas import tpu as pltpu
from jax.experimental.pallas import tpu_sc as plsc

F32 = jnp.float32
BF16 = jnp.bfloat16

D = 1024
BATCH = 8
SEQ = 2048
DEPTH = 4
DEC_BATCH = 128
PAST_LEN = 16384
CHUNK = 128
A_HEADS = 8
POOL_WINDOWS = (2, 4, 8, 16)
B_GROUP_DIM = D // len(POOL_WINDOWS)
POOL_BUF = max(POOL_WINDOWS) - 1
CONV_WIDTH = 31
CONV_BUF = CONV_WIDTH - 1
N_GROUPS = 4
EXPERTS_PER_GROUP = 8
N_EXPERTS = N_GROUPS * EXPERTS_PER_GROUP
D_EXPERT = D // 2
EPS = 1e-6

LANES = 128
SC_CORES = 2
SC_SUBCORES = 16
TM = 512
TILES_PER_SEQ = SEQ // TM
T_PROMPT = BATCH * SEQ
T_ALL = T_PROMPT + DEC_BATCH
N_TILES = -(-T_ALL // TM)
T_PAD = N_TILES * TM
N_PROMPT_TILES = T_PROMPT // TM
SAMPLE_ROWS = 32
BM = 256
N_BLOCKS = -(-(2 * T_PAD) // BM) + N_EXPERTS
P_ROWS = N_BLOCKS * BM
META_ROWS = 8
POOL_CARRY = 16
CONV_CARRY = 32
EXPERT_LANE0 = N_GROUPS
VMEM_LIMIT = 56 * 1024 * 1024


def _cparams(*sem):
    return pltpu.CompilerParams(dimension_semantics=sem, vmem_limit_bytes=VMEM_LIMIT)


def _rms(x, g):
    return x * lax.rsqrt(jnp.mean(x * x, axis=-1, keepdims=True) + EPS) * g


def _ln(x, g, b):
    mu = jnp.mean(x, axis=-1, keepdims=True)
    xc = x - mu
    var = jnp.mean(xc * xc, axis=-1, keepdims=True)
    return xc * lax.rsqrt(var + EPS) * g + b


def _dot(a, b):
    return jnp.dot(a, b, preferred_element_type=F32)


def _ada_kernel(c_ref, w_ref, b_ref, o_ref):
    c = c_ref[...]
    s = (c * jax.nn.sigmoid(c)).astype(BF16)
    o_ref[...] = _dot(s, w_ref[...].astype(BF16)) + b_ref[...]


def _ada_mods(c_all, ada_w, ada_b):
    n = c_all.shape[0]
    tn = 2048
    return pl.pallas_call(
        _ada_kernel,
        grid=(DEPTH, 6 * D // tn),
        in_specs=[
            pl.BlockSpec((n, D), lambda l, j: (0, 0)),
            pl.BlockSpec((None, D, tn), lambda l, j: (l, 0, j)),
            pl.BlockSpec((None, 1, tn), lambda l, j: (l, 0, j)),
        ],
        out_specs=pl.BlockSpec((None, n, tn), lambda l, j: (l, 0, j)),
        out_shape=jax.ShapeDtypeStruct((DEPTH, n, 6 * D), F32),
        compiler_params=_cparams("parallel", "parallel"),
        name="ada_mods",
    )(c_all, ada_w, ada_b.reshape(DEPTH, 1, 6 * D))


def _mod_specs_prompt(layer, chunks):
    return [
        pl.BlockSpec((None, None, 1, D), lambda i, c=c: (layer, i // TILES_PER_SEQ, 0, c))
        for c in chunks
    ]


def _mod_specs_sample(layer, chunks, rows, row_block0=0):
    return [
        pl.BlockSpec((None, rows, D), lambda i, c=c: (layer, i + row_block0, c)) for c in chunks
    ]


def _full(shape):
    nd = len(shape)
    return pl.BlockSpec(shape, lambda i: (0,) * nd)


def _layer_block(shape, layer):
    nd = len(shape)
    return pl.BlockSpec((None,) + shape, lambda i: (layer,) + (0,) * nd)


def _mix_a_front(x, sh, sc, ng, win_ref, lng_ref, lnb_ref):
    h = (_rms(x, ng) * (1.0 + sc) + sh).astype(BF16)
    z = jax.nn.gelu(_dot(h, win_ref[...]))
    u = z[:, :D]
    v = _ln(z[:, D:], lng_ref[...], lnb_ref[...])
    return u, v


def _mix_a_prompt_kernel(x_ref, sh_ref, sc_ref, gt_ref, ng_ref, win_ref, lng_ref, lnb_ref,
                         ws_ref, bs_ref, wout_ref, x1_ref, cv_ref, mixed_ref):
    i = pl.program_id(0)
    x = x_ref[...]
    u, v = _mix_a_front(x, sh_ref[...], sc_ref[...], ng_ref[...], win_ref, lng_ref, lnb_ref)
    vb = v.astype(BF16)
    n_chunks = TM // CHUNK
    row = lax.broadcasted_iota(jnp.int32, (CHUNK, CHUNK), 0)
    col = lax.broadcasted_iota(jnp.int32, (CHUNK, CHUNK), 1)
    tril = row >= col
    hd_dim = D // A_HEADS
    for hd in range(A_HEADS):
        ws = jnp.where(tril, ws_ref[hd], 0.0).astype(BF16)
        cols = slice(hd * hd_dim, (hd + 1) * hd_dim)
        vcat = jnp.concatenate([vb[c * CHUNK:(c + 1) * CHUNK, cols] for c in range(n_chunks)], axis=1)
        m = _dot(ws, vcat)
        for c in range(n_chunks):
            mixed_ref[c * CHUNK:(c + 1) * CHUNK, cols] = m[:, c * hd_dim:(c + 1) * hd_dim] + bs_ref[hd]
    y = _dot((u * mixed_ref[...]).astype(BF16), wout_ref[...])
    x1_ref[...] = x + gt_ref[...] * y

    @pl.when(i % TILES_PER_SEQ == TILES_PER_SEQ - 1)
    def _():
        cv_ref[...] = v[TM - CHUNK:, :]


def _mix_a_sample_kernel(x_ref, sh_ref, sc_ref, gt_ref, ng_ref, win_ref, lng_ref, lnb_ref,
                         wsd_ref, bsd_ref, wout_ref, x1_ref, cv_ref):
    x = x_ref[...]
    u, v = _mix_a_front(x, sh_ref[...], sc_ref[...], ng_ref[...], win_ref, lng_ref, lnb_ref)
    mixed = wsd_ref[...] * v + bsd_ref[...]
    y = _dot((u * mixed).astype(BF16), wout_ref[...])
    x1_ref[...] = x + gt_ref[...] * y
    cv_ref[...] = v


def _mixer_a(x_all, mod_p, mod_s, ng, layer, la, w_in, ln_g, ln_b, w_s, b_s, w_out):
    bs_b = jnp.broadcast_to(b_s[la][:, :, None], (A_HEADS, CHUNK, D // A_HEADS))
    common = [
        _layer_block((D, 2 * D), la), _layer_block((1, D), la), _layer_block((1, D), la),
    ]
    x_all, cv_p = pl.pallas_call(
        _mix_a_prompt_kernel,
        grid=(N_PROMPT_TILES,),
        in_specs=[pl.BlockSpec((TM, D), lambda i: (i, 0))]
        + _mod_specs_prompt(layer, (0, 1, 2))
        + [_full((1, D))] + common
        + [_layer_block((A_HEADS, CHUNK, CHUNK), la), _full((A_HEADS, CHUNK, D // A_HEADS)),
           _layer_block((D, D), la)],
        out_specs=[
            pl.BlockSpec((TM, D), lambda i: (i, 0)),
            pl.BlockSpec((None, CHUNK, D), lambda i: (i // TILES_PER_SEQ, 0, 0)),
        ],
        out_shape=[
            jax.ShapeDtypeStruct((T_PAD, D), F32),
            jax.ShapeDtypeStruct((BATCH, CHUNK, D), F32),
        ],
        scratch_shapes=[pltpu.VMEM((TM, D), F32)],
        input_output_aliases={0: 0},
        compiler_params=_cparams("arbitrary"),
        name="mixer_a_prompt",
    )(x_all, mod_p, mod_p, mod_p, ng, w_in, ln_g, ln_b, w_s, bs_b, w_out)

    wsd = jnp.repeat(w_s[la, :, 0, 0], D // A_HEADS).reshape(1, D)
    bsd = jnp.repeat(b_s[la, :, 0], D // A_HEADS).reshape(1, D)
    sblk = T_PROMPT // DEC_BATCH
    x_all, cv_s = pl.pallas_call(
        _mix_a_sample_kernel,
        grid=(1,),
        in_specs=[pl.BlockSpec((DEC_BATCH, D), lambda i: (sblk, 0))]
        + _mod_specs_sample(layer, (0, 1, 2), DEC_BATCH)
        + [_full((1, D))] + common
        + [_full((1, D)), _full((1, D)), _layer_block((D, D), la)],
        out_specs=[
            pl.BlockSpec((DEC_BATCH, D), lambda i: (sblk, 0)),
            pl.BlockSpec((DEC_BATCH, D), lambda i: (0, 0)),
        ],
        out_shape=[
            jax.ShapeDtypeStruct((T_PAD, D), F32),
            jax.ShapeDtypeStruct((DEC_BATCH, D), F32),
        ],
        input_output_aliases={0: 0},
        compiler_params=_cparams("arbitrary"),
        name="mixer_a_sample",
    )(x_all, mod_s, mod_s, mod_s, ng, w_in, ln_g, ln_b, wsd, bsd, w_out)
    return x_all, cv_p, cv_s.reshape(DEC_BATCH, 1, D)


def _mix_b_tail(pooled_groups, wgrp_ref, bgrp_ref, scale_ref, wout_ref):
    outs = [
        _dot(pg.astype(BF16), wgrp_ref[g]) + bgrp_ref[g]
        for g, pg in enumerate(pooled_groups)
    ]
    mixed = jnp.concatenate(outs, axis=1) * scale_ref[...]
    return _dot(mixed.astype(BF16), wout_ref[...])


def _mix_b_prompt_kernel(x_ref, sh_ref, sc_ref, gt_ref, ng_ref, win_ref, wgrp_ref, bgrp_ref,
                         scale_ref, wout_ref, x1_ref, st_ref, full_ref):
    i = pl.program_id(0)
    j = i % TILES_PER_SEQ
    x = x_ref[...]
    h = (_rms(x, ng_ref[...]) * (1.0 + sc_ref[...]) + sh_ref[...]).astype(BF16)
    p = _dot(h, win_ref[...])

    @pl.when(j == 0)
    def _():
        full_ref[0:POOL_CARRY, :] = jnp.zeros((POOL_CARRY, D), F32)

    full_ref[POOL_CARRY:POOL_CARRY + TM, :] = p
    pos = j * TM + lax.broadcasted_iota(jnp.int32, (TM, 1), 0)
    pooled = []
    for g, w in enumerate(POOL_WINDOWS):
        cols = slice(g * B_GROUP_DIM, (g + 1) * B_GROUP_DIM)
        s = p[:, cols]
        for k in range(1, w):
            s = s + full_ref[pl.ds(POOL_CARRY - k, TM), cols]
        cnt = jnp.minimum(w, pos + 1).astype(F32)
        pooled.append(s / cnt - p[:, cols])
    y = _mix_b_tail(pooled, wgrp_ref, bgrp_ref, scale_ref, wout_ref)
    x1_ref[...] = x + gt_ref[...] * y

    @pl.when(j == TILES_PER_SEQ - 1)
    def _():
        st_ref[...] = full_ref[pl.ds(POOL_CARRY + TM - POOL_BUF, POOL_BUF), :]

    full_ref[0:POOL_CARRY, :] = full_ref[pl.ds(TM, POOL_CARRY), :]


def _mix_b_sample_kernel(x_ref, sh_ref, sc_ref, gt_ref, ng_ref, st_ref, win_ref, wgrp_ref,
                         bgrp_ref, scale_ref, wout_ref, x1_ref, nst_ref):
    x = x_ref[...]
    h = (_rms(x, ng_ref[...]) * (1.0 + sc_ref[...]) + sh_ref[...]).astype(BF16)
    p = _dot(h, win_ref[...])
    pooled = []
    for g, w in enumerate(POOL_WINDOWS):
        s = p[:, g * B_GROUP_DIM:(g + 1) * B_GROUP_DIM]
        for k in range(1, w):
            c0 = (POOL_BUF - k) * D + g * B_GROUP_DIM
            s = s + st_ref[:, c0:c0 + B_GROUP_DIM]
        cnt = float(min(w, PAST_LEN + 1))
        pooled.append(s / cnt - p[:, g * B_GROUP_DIM:(g + 1) * B_GROUP_DIM])
    y = _mix_b_tail(pooled, wgrp_ref, bgrp_ref, scale_ref, wout_ref)
    x1_ref[...] = x + gt_ref[...] * y
    nst_ref[:, 0:(POOL_BUF - 1) * D] = st_ref[:, D:POOL_BUF * D]
    nst_ref[:, (POOL_BUF - 1) * D:POOL_BUF * D] = p


def _mixer_b(x_all, mod_p, mod_s, ng, layer, lb, state, w_in, w_grp, b_grp, scale, w_out):
    gd = B_GROUP_DIM
    common = [
        _layer_block((D, D), lb), _layer_block((len(POOL_WINDOWS), gd, gd), lb),
        _layer_block((len(POOL_WINDOWS), 1, gd), lb), _layer_block((1, D), lb),
        _layer_block((D, D), lb),
    ]
    x_all, st_p = pl.pallas_call(
        _mix_b_prompt_kernel,
        grid=(N_PROMPT_TILES,),
        in_specs=[pl.BlockSpec((TM, D), lambda i: (i, 0))]
        + _mod_specs_prompt(layer, (0, 1, 2)) + [_full((1, D))] + common,
        out_specs=[
            pl.BlockSpec((TM, D), lambda i: (i, 0)),
            pl.BlockSpec((None, POOL_BUF, D), lambda i: (i // TILES_PER_SEQ, 0, 0)),
        ],
        out_shape=[
            jax.ShapeDtypeStruct((T_PAD, D), F32),
            jax.ShapeDtypeStruct((BATCH, POOL_BUF, D), F32),
        ],
        scratch_shapes=[pltpu.VMEM((TM + POOL_CARRY, D), F32)],
        input_output_aliases={0: 0},
        compiler_params=_cparams("arbitrary"),
        name="mixer_b_prompt",
    )(x_all, mod_p, mod_p, mod_p, ng, w_in, w_grp, b_grp, scale, w_out)

    r = SAMPLE_ROWS
    sblk = T_PROMPT // r
    st2 = state[lb].reshape(DEC_BATCH, POOL_BUF * D)
    x_all, st_s = pl.pallas_call(
        _mix_b_sample_kernel,
        grid=(DEC_BATCH // r,),
        in_specs=[pl.BlockSpec((r, D), lambda i: (sblk + i, 0))]
        + _mod_specs_sample(layer, (0, 1, 2), r) + [_full((1, D))]
        + [pl.BlockSpec((r, POOL_BUF * D), lambda i: (i, 0))] + common,
        out_specs=[
            pl.BlockSpec((r, D), lambda i: (sblk + i, 0)),
            pl.BlockSpec((r, POOL_BUF * D), lambda i: (i, 0)),
        ],
        out_shape=[
            jax.ShapeDtypeStruct((T_PAD, D), F32),
            jax.ShapeDtypeStruct((DEC_BATCH, POOL_BUF * D), F32),
        ],
        input_output_aliases={0: 0},
        compiler_params=_cparams("arbitrary"),
        name="mixer_b_sample",
    )(x_all, mod_s, mod_s, mod_s, ng, st2, w_in, w_grp, b_grp, scale, w_out)
    return x_all, st_p, st_s.reshape(DEC_BATCH, POOL_BUF, D)


def _mix_c_glu(x, sh, sc, ng, win_ref, bin_ref):
    h = (_rms(x, ng) * (1.0 + sc) + sh).astype(BF16)
    ag = _dot(h, win_ref[...]) + bin_ref[...]
    return ag[:, :D] * jax.nn.sigmoid(ag[:, D:])


def _mix_c_tail(conv, lng_ref, lnb_ref, wout_ref, bout_ref):
    z = _ln(conv, lng_ref[...], lnb_ref[...])
    z = z * jax.nn.sigmoid(z)
    return _dot(z.astype(BF16), wout_ref[...]) + bout_ref[...]


def _mix_c_prompt_kernel(x_ref, sh_ref, sc_ref, gt_ref, ng_ref, win_ref, bin_ref, wdw_ref,
                         bdw_ref, lng_ref, lnb_ref, wout_ref, bout_ref, x1_ref, st_ref,
                         full_ref, conv_ref):
    i = pl.program_id(0)
    j = i % TILES_PER_SEQ
    x = x_ref[...]
    glu = _mix_c_glu(x, sh_ref[...], sc_ref[...], ng_ref[...], win_ref, bin_ref)

    @pl.when(j == 0)
    def _():
        full_ref[0:CONV_CARRY, :] = jnp.zeros((CONV_CARRY, D), F32)

    full_ref[CONV_CARRY:CONV_CARRY + TM, :] = glu
    off = CONV_CARRY - CONV_BUF
    for c in range(D // LANES):
        cols = slice(c * LANES, (c + 1) * LANES)
        acc = full_ref[pl.ds(off, TM), cols] * wdw_ref[0:1, cols]
        for k in range(1, CONV_WIDTH):
            acc = acc + full_ref[pl.ds(off + k, TM), cols] * wdw_ref[k:k + 1, cols]
        conv_ref[:, cols] = acc + bdw_ref[:, cols]
    y = _mix_c_tail(conv_ref[...], lng_ref, lnb_ref, wout_ref, bout_ref)
    x1_ref[...] = x + gt_ref[...] * y

    @pl.when(j == TILES_PER_SEQ - 1)
    def _():
        st_ref[...] = full_ref[pl.ds(CONV_CARRY + TM - CONV_BUF, CONV_BUF), :]

    full_ref[0:CONV_CARRY, :] = full_ref[pl.ds(TM, CONV_CARRY), :]


def _mix_c_sample_kernel(x_ref, sh_ref, sc_ref, gt_ref, ng_ref, st_ref, win_ref, bin_ref,
                         wdw_ref, bdw_ref, lng_ref, lnb_ref, wout_ref, bout_ref, x1_ref, nst_ref):
    x = x_ref[...]
    glu = _mix_c_glu(x, sh_ref[...], sc_ref[...], ng_ref[...], win_ref, bin_ref)
    acc = glu * wdw_ref[CONV_BUF:CONV_BUF + 1, :]
    for k in range(CONV_BUF):
        acc = acc + st_ref[:, k * D:(k + 1) * D] * wdw_ref[k:k + 1, :]
    y = _mix_c_tail(acc + bdw_ref[...], lng_ref, lnb_ref, wout_ref, bout_ref)
    x1_ref[...] = x + gt_ref[...] * y
    nst_ref[:, 0:(CONV_BUF - 1) * D] = st_ref[:, D:CONV_BUF * D]
    nst_ref[:, (CONV_BUF - 1) * D:CONV_BUF * D] = glu


def _mixer_c(x_all, mod_p, mod_s, ng, layer, lc, state, w_in, b_in, w_dw, b_dw, ln_g, ln_b,
             w_out, b_out):
    common = [
        _layer_block((D, 2 * D), lc), _layer_block((1, 2 * D), lc),
        _layer_block((CONV_WIDTH, D), lc), _layer_block((1, D), lc), _layer_block((1, D), lc),
        _layer_block((1, D), lc), _layer_block((D, D), lc), _layer_block((1, D), lc),
    ]
    x_all, st_p = pl.pallas_call(
        _mix_c_prompt_kernel,
        grid=(N_PROMPT_TILES,),
        in_specs=[pl.BlockSpec((TM, D), lambda i: (i, 0))]
        + _mod_specs_prompt(layer, (0, 1, 2)) + [_full((1, D))] + common,
        out_specs=[
            pl.BlockSpec((TM, D), lambda i: (i, 0)),
            pl.BlockSpec((None, CONV_BUF, D), lambda i: (i // TILES_PER_SEQ, 0, 0)),
        ],
        out_shape=[
            jax.ShapeDtypeStruct((T_PAD, D), F32),
            jax.ShapeDtypeStruct((BATCH, CONV_BUF, D), F32),
        ],
        scratch_shapes=[pltpu.VMEM((TM + CONV_CARRY, D), F32), pltpu.VMEM((TM, D), F32)],
        input_output_aliases={0: 0},
        compiler_params=_cparams("arbitrary"),
        name="mixer_c_prompt",
    )(x_all, mod_p, mod_p, mod_p, ng, w_in, b_in, w_dw, b_dw, ln_g, ln_b, w_out, b_out)

    r = SAMPLE_ROWS
    sblk = T_PROMPT // r
    st2 = state[lc].reshape(DEC_BATCH, CONV_BUF * D)
    x_all, st_s = pl.pallas_call(
        _mix_c_sample_kernel,
        grid=(DEC_BATCH // r,),
        in_specs=[pl.BlockSpec((r, D), lambda i: (sblk + i, 0))]
        + _mod_specs_sample(layer, (0, 1, 2), r) + [_full((1, D))]
        + [pl.BlockSpec((r, CONV_BUF * D), lambda i: (i, 0))] + common,
        out_specs=[
            pl.BlockSpec((r, D), lambda i: (sblk + i, 0)),
            pl.BlockSpec((r, CONV_BUF * D), lambda i: (i, 0)),
        ],
        out_shape=[
            jax.ShapeDtypeStruct((T_PAD, D), F32),
            jax.ShapeDtypeStruct((DEC_BATCH, CONV_BUF * D), F32),
        ],
        input_output_aliases={0: 0},
        compiler_params=_cparams("arbitrary"),
        name="mixer_c_sample",
    )(x_all, mod_s, mod_s, mod_s, ng, st2, w_in, b_in, w_dw, b_dw, ln_g, ln_b, w_out, b_out)
    return x_all, st_p, st_s.reshape(DEC_BATCH, CONV_BUF, D)


def _tile_mod(i, p_ref, s_ref):
    s_rows = jnp.concatenate([s_ref[...], jnp.zeros((TM - DEC_BATCH, D), F32)], axis=0)
    return jnp.where(i >= N_PROMPT_TILES, s_rows, p_ref[...])


def _unified_mod_specs(layer, chunks):
    specs = []
    for c in chunks:
        specs.append(pl.BlockSpec(
            (None, None, 1, D),
            lambda i, c=c: (layer, jnp.minimum(i // TILES_PER_SEQ, BATCH - 1), 0, c)))
        specs.append(pl.BlockSpec((None, DEC_BATCH, D), lambda i, c=c: (layer, 0, c)))
    return specs


def _router_kernel(x_ref, shp_ref, shs_ref, scp_ref, scs_ref, ng_ref, wr_ref, br_ref,
                   h2_ref, meta_ref, wts_ref, cnt_ref, carry_ref):
    i = pl.program_id(0)

    @pl.when(i == 0)
    def _():
        carry_ref[...] = jnp.zeros((1, LANES), F32)

    sh = _tile_mod(i, shp_ref, shs_ref)
    sc = _tile_mod(i, scp_ref, scs_ref)
    h2 = _rms(x_ref[...], ng_ref[...]) * (1.0 + sc) + sh
    h2_ref[...] = h2
    lg = jnp.dot(h2, wr_ref[...], preferred_element_type=F32,
                 precision=lax.Precision.HIGHEST) + br_ref[...]
    lane = lax.broadcasted_iota(jnp.int32, (TM, LANES), 1)
    neg = -jnp.inf
    is_g = lane < N_GROUPS
    gm = jnp.where(is_g, lg, neg)
    gmax = jnp.max(gm, axis=1, keepdims=True)
    lane_f = lane.astype(F32)
    far = float(LANES)
    gsel = jnp.min(jnp.where(gm == gmax, lane_f, far), axis=1, keepdims=True).astype(jnp.int32)
    gsum = jnp.sum(jnp.where(is_g, jnp.exp(lg - gmax), 0.0), axis=1, keepdims=True)
    g_w = 1.0 / gsum
    e_lane = lane - EXPERT_LANE0
    in_grp = (e_lane >= 0) & (e_lane < N_EXPERTS) & (
        lax.shift_right_arithmetic(e_lane, EXPERTS_PER_GROUP.bit_length() - 1) == gsel)
    em = jnp.where(in_grp, lg, neg)
    m1 = jnp.max(em, axis=1, keepdims=True)
    i1 = jnp.min(jnp.where(em == m1, lane_f, far), axis=1, keepdims=True).astype(jnp.int32)
    em2 = jnp.where(lane == i1, neg, em)
    m2 = jnp.max(em2, axis=1, keepdims=True)
    i2 = jnp.min(jnp.where(em2 == m2, lane_f, far), axis=1, keepdims=True).astype(jnp.int32)
    e2 = jnp.exp(m2 - m1)
    den = 1.0 + e2
    w1 = (1.0 / den) * g_w
    w2 = (e2 / den) * g_w

    hit1 = lane == i1
    hit2 = lane == i2
    assign = jnp.where(hit1 | hit2, 1.0, 0.0).astype(BF16)
    r = lax.broadcasted_iota(jnp.int32, (TM, TM), 0)
    c = lax.broadcasted_iota(jnp.int32, (TM, TM), 1)
    before = (c < r).astype(BF16)
    seen = _dot(before, assign) + carry_ref[...]
    rank1 = jnp.sum(jnp.where(hit1, seen, 0.0), axis=1, keepdims=True)
    rank2 = jnp.sum(jnp.where(hit2, seen, 0.0), axis=1, keepdims=True)
    carry_ref[...] = carry_ref[...] + jnp.sum(assign.astype(F32), axis=0, keepdims=True)
    cnt_ref[...] = carry_ref[...]

    ex1 = (i1 - EXPERT_LANE0).astype(F32)
    ex2 = (i2 - EXPERT_LANE0).astype(F32)
    fields = jnp.where(lane == 0, ex1, jnp.where(lane == 1, ex2,
                       jnp.where(lane == 2, rank1, jnp.where(lane == 3, rank2, 0.0))))
    meta_ref[...] = fields.T[0:META_ROWS, :]
    wts_ref[...] = jnp.where(lane == 0, w1, jnp.where(lane == 1, w2, 0.0))


def _router(x_all, mod_p, mod_s, ng, layer, w_r, b_r):
    return pl.pallas_call(
        _router_kernel,
        grid=(N_TILES,),
        in_specs=[pl.BlockSpec((TM, D), lambda i: (i, 0))]
        + _unified_mod_specs(layer, (3, 4))
        + [_full((1, D)), _full((D, LANES)), _full((1, LANES))],
        out_specs=[
            pl.BlockSpec((TM, D), lambda i: (i, 0)),
            pl.BlockSpec((META_ROWS, TM), lambda i: (0, i)),
            pl.BlockSpec((TM, LANES), lambda i: (i, 0)),
            pl.BlockSpec((1, LANES), lambda i: (0, 0)),
        ],
        out_shape=[
            jax.ShapeDtypeStruct((T_PAD, D), F32),
            jax.ShapeDtypeStruct((META_ROWS, T_PAD), F32),
            jax.ShapeDtypeStruct((T_PAD, LANES), F32),
            jax.ShapeDtypeStruct((1, LANES), F32),
        ],
        scratch_shapes=[pltpu.VMEM((1, LANES), F32)],
        compiler_params=_cparams("arbitrary"),
        name="router",
    )(x_all, mod_p, mod_s, mod_p, mod_s, ng, w_r, b_r)


def _gather_chunk(per_worker):
    for ch in range(64, 7, -8):
        if per_worker % ch == 0:
            return ch
    raise ValueError(per_worker)


def _sc_gather(table, idx):
    nc, nw = SC_CORES, SC_CORES * SC_SUBCORES
    m = idx.shape[0]
    per_w = m // nw
    assert per_w * nw == m
    ch = _gather_chunk(per_w)
    width = table.shape[1]
    mesh = plsc.VectorSubcoreMesh(core_axis_name="c", subcore_axis_name="s")

    @functools.partial(
        pl.kernel,
        out_type=jax.ShapeDtypeStruct((m, width), table.dtype),
        mesh=mesh,
        scratch_types=[
            pltpu.VMEM((ch,), jnp.int32),
            pltpu.VMEM((ch, width), table.dtype),
            pltpu.SemaphoreType.DMA,
        ],
    )
    def gather_kernel(t_hbm, i_hbm, o_hbm, idx_v, rows_v, sem):
        wid = lax.axis_index("s") * nc + lax.axis_index("c")
        base = wid * per_w

        @pl.loop(0, per_w // ch)
        def _(j):
            off = pl.multiple_of(base + j * ch, 8)
            pltpu.sync_copy(i_hbm.at[pl.ds(off, ch)], idx_v)
            pltpu.async_copy(t_hbm.at[idx_v], rows_v, sem).wait()
            pltpu.sync_copy(rows_v, o_hbm.at[pl.ds(off, ch)])

    return gather_kernel(table, idx)


def _sc_scatter2(rows, dest0, dest1, n_out):
    nc, nw = SC_CORES, SC_CORES * SC_SUBCORES
    n, width = rows.shape
    per_w = n // nw
    assert per_w * nw == n
    ch = _gather_chunk(per_w)
    mesh = plsc.VectorSubcoreMesh(core_axis_name="c", subcore_axis_name="s")

    @functools.partial(
        pl.kernel,
        out_type=jax.ShapeDtypeStruct((n_out, width), rows.dtype),
        mesh=mesh,
        scratch_types=[
            pltpu.VMEM((ch,), jnp.int32),
            pltpu.VMEM((ch,), jnp.int32),
            pltpu.VMEM((ch, width), rows.dtype),
            pltpu.SemaphoreType.DMA,
            pltpu.SemaphoreType.DMA,
        ],
    )
    def scatter_kernel(r_hbm, d0_hbm, d1_hbm, o_hbm, i0_v, i1_v, rows_v, sem0, sem1):
        wid = lax.axis_index("s") * nc + lax.axis_index("c")
        base = wid * per_w

        @pl.loop(0, per_w // ch)
        def _(j):
            off = pl.multiple_of(base + j * ch, 8)
            pltpu.sync_copy(d0_hbm.at[pl.ds(off, ch)], i0_v)
            pltpu.sync_copy(d1_hbm.at[pl.ds(off, ch)], i1_v)
            pltpu.sync_copy(r_hbm.at[pl.ds(off, ch)], rows_v)
            c0 = pltpu.async_copy(rows_v, o_hbm.at[i0_v], sem0)
            c1 = pltpu.async_copy(rows_v, o_hbm.at[i1_v], sem1)
            c0.wait()
            c1.wait()

    return scatter_kernel(rows, dest0, dest1)


def _expert_kernel(be_ref, first_ref, nvalid_ref, nused_ref, x_ref, wg_ref, wu_ref, wd_ref,
                   y_ref, wgb_ref, wub_ref, wdb_ref):
    b = pl.program_id(0)

    @pl.when(b < nused_ref[0])
    def _():
        @pl.when(first_ref[b] == 1)
        def _():
            wgb_ref[...] = wg_ref[...].astype(BF16)
            wub_ref[...] = wu_ref[...].astype(BF16)
            wdb_ref[...] = wd_ref[...].astype(BF16)

        live = lax.broadcasted_iota(jnp.int32, (BM, 1), 0) < nvalid_ref[b]
        x = jnp.where(live, x_ref[...], 0.0).astype(BF16)
        g = _dot(x, wgb_ref[...])
        u = _dot(x, wub_ref[...])
        hmid = (g * jax.nn.sigmoid(g)) * u
        y_ref[...] = _dot(hmid.astype(BF16), wdb_ref[...])


def _experts(x_sorted, block_expert, first, nvalid, nused, layer, w_gate, w_up, w_down):
    def row_map(b, be, fi, nv, nu):
        return (jnp.minimum(b, nu[0] - 1), 0)

    def w_map(b, be, fi, nv, nu):
        return (layer, be[b], 0, 0)

    grid_spec = pltpu.PrefetchScalarGridSpec(
        num_scalar_prefetch=4,
        grid=(N_BLOCKS,),
        in_specs=[
            pl.BlockSpec((BM, D), row_map),
            pl.BlockSpec((None, None, D, D_EXPERT), w_map),
            pl.BlockSpec((None, None, D, D_EXPERT), w_map),
            pl.BlockSpec((None, None, D_EXPERT, D), w_map),
        ],
        out_specs=pl.BlockSpec((BM, D), row_map),
        scratch_shapes=[
            pltpu.VMEM((D, D_EXPERT), BF16),
            pltpu.VMEM((D, D_EXPERT), BF16),
            pltpu.VMEM((D_EXPERT, D), BF16),
        ],
    )
    return pl.pallas_call(
        _expert_kernel,
        grid_spec=grid_spec,
        out_shape=jax.ShapeDtypeStruct((P_ROWS, D), F32),
        compiler_params=_cparams("arbitrary"),
        name="experts",
    )(block_expert, first, nvalid, nused, x_sorted, w_gate, w_up, w_down)


def _combine_kernel(x_ref, g0_ref, g1_ref, wts_ref, gtp_ref, gts_ref, o_ref):
    i = pl.program_id(0)
    gt = _tile_mod(i, gtp_ref, gts_ref)
    w = wts_ref[...]
    moe = w[:, 0:1] * g0_ref[...] + w[:, 1:2] * g1_ref[...]
    o_ref[...] = x_ref[...] + gt * moe


def _combine(x_all, gathered, wts, mod_p, mod_s, layer):
    return pl.pallas_call(
        _combine_kernel,
        grid=(N_TILES,),
        in_specs=[
            pl.BlockSpec((TM, D), lambda i: (i, 0)),
            pl.BlockSpec((TM, D), lambda i: (i, 0)),
            pl.BlockSpec((TM, D), lambda i: (i + N_TILES, 0)),
            pl.BlockSpec((TM, LANES), lambda i: (i, 0)),
        ] + _unified_mod_specs(layer, (5,)),
        out_specs=pl.BlockSpec((TM, D), lambda i: (i, 0)),
        out_shape=jax.ShapeDtypeStruct((T_PAD, D), F32),
        input_output_aliases={0: 0},
        compiler_params=_cparams("arbitrary"),
        name="combine",
    )(x_all, gathered, gathered, wts, mod_p, mod_s)


def _moe(x_all, mod_p, mod_s, ng, layer, w_r, b_r, w_gate, w_up, w_down):
    h2, meta, wts, cnt = _router(x_all, mod_p, mod_s, ng, layer, w_r, b_r)
    counts = cnt[0, EXPERT_LANE0:EXPERT_LANE0 + N_EXPERTS].astype(jnp.int32)
    padded = (counts + BM - 1) // BM * BM
    pend = jnp.cumsum(padded)
    pstart = pend - padded
    expert = meta[0:2].astype(jnp.int32)
    dest = meta[2:4].astype(jnp.int32)
    for e in range(N_EXPERTS):
        dest = dest + jnp.where(expert == e, pstart[e], 0)
    block0 = jnp.arange(N_BLOCKS, dtype=jnp.int32) * BM
    block_expert = jnp.minimum(
        jnp.sum((pend[None, :] <= block0[:, None]).astype(jnp.int32), axis=1), N_EXPERTS - 1)
    of_block = block_expert[:, None] == jnp.arange(N_EXPERTS, dtype=jnp.int32)[None, :]
    seg_end = jnp.sum(jnp.where(of_block, (pstart + counts)[None, :], 0), axis=1)
    nvalid = jnp.clip(seg_end - block0, 0, BM).astype(jnp.int32)
    nused = (pend[-1] // BM).astype(jnp.int32).reshape(1)
    first = jnp.concatenate(
        [jnp.ones((1,), jnp.int32), (block_expert[1:] != block_expert[:-1]).astype(jnp.int32)])

    x_sorted = _sc_scatter2(h2, dest[0], dest[1], P_ROWS)
    y_sorted = _experts(x_sorted, block_expert, first, nvalid, nused, layer, w_gate, w_up, w_down)
    gathered = _sc_gather(y_sorted, dest.reshape(-1))
    return _combine(x_all, gathered, wts, mod_p, mod_s, layer)


def _final_kernel(x_ref, g_ref, yp_ref, ys_ref):
    i = pl.program_id(0)
    y = _rms(x_ref[...], g_ref[...])

    @pl.when(i < N_PROMPT_TILES)
    def _():
        yp_ref[...] = y

    @pl.when(i == N_PROMPT_TILES)
    def _():
        ys_ref[...] = y[0:DEC_BATCH, :]


def _final_norm(x_all, g):
    return pl.pallas_call(
        _final_kernel,
        grid=(N_PROMPT_TILES + 1,),
        in_specs=[pl.BlockSpec((TM, D), lambda i: (i, 0)), _full((1, D))],
        out_specs=[
            pl.BlockSpec((TM, D), lambda i: (jnp.minimum(i, N_PROMPT_TILES - 1), 0)),
            pl.BlockSpec((DEC_BATCH, D), lambda i: (0, 0)),
        ],
        out_shape=[
            jax.ShapeDtypeStruct((T_PROMPT, D), F32),
            jax.ShapeDtypeStruct((DEC_BATCH, D), F32),
        ],
        compiler_params=_cparams("arbitrary"),
        name="final_norm",
    )(x_all, g)


def kernel(x_prompt, x_sample, c_prompt, c_sample, state_pool, state_conv, ada_w, ada_b, norm_g, final_g, a_w_in, a_ln_g, a_ln_b, a_w_s, a_b_s, a_w_out, b_w_in, b_w_grp, b_b_grp, b_scale, b_w_out, c_w_in, c_b_in, c_w_dw, c_b_dw, c_ln_g, c_ln_b, c_w_out, c_b_out, moe_w_grp, moe_b_grp, moe_w_exp, moe_b_exp, moe_w_gate, moe_w_up, moe_w_down):
    x_all = jnp.concatenate([
        x_prompt.reshape(T_PROMPT, D), x_sample.reshape(DEC_BATCH, D),
        jnp.zeros((T_PAD - T_ALL, D), F32)], axis=0)
    mods = _ada_mods(jnp.concatenate([c_prompt, c_sample], axis=0), ada_w, ada_b)
    mod_p = mods[:, :BATCH].reshape(DEPTH, BATCH, 1, 6 * D)
    mod_s = mods[:, BATCH:]

    a_w_in_b, a_w_out_b = a_w_in.astype(BF16), a_w_out.astype(BF16)
    b_w_in_b, b_w_grp_b, b_w_out_b = b_w_in.astype(BF16), b_w_grp.astype(BF16), b_w_out.astype(BF16)
    c_w_in_b, c_w_out_b = c_w_in.astype(BF16), c_w_out.astype(BF16)
    w_r = jnp.pad(jnp.concatenate([moe_w_grp, moe_w_exp], axis=2),
                  ((0, 0), (0, 0), (0, LANES - N_GROUPS - N_EXPERTS)))
    b_r = jnp.pad(jnp.concatenate([moe_b_grp, moe_b_exp], axis=1),
                  ((0, 0), (0, LANES - N_GROUPS - N_EXPERTS)))

    new_a_p, new_a_s, new_b_p, new_b_s, new_c_p, new_c_s = [], [], [], [], [], []
    ia = ib = ic = 0
    for layer in range(DEPTH):
        ng1 = norm_g[layer, 0].reshape(1, D)
        ng2 = norm_g[layer, 1].reshape(1, D)
        kind = layer % 3
        if kind == 0:
            x_all, st_p, st_s = _mixer_a(
                x_all, mod_p, mod_s, ng1, layer, ia, a_w_in_b, a_ln_g.reshape(-1, 1, D),
                a_ln_b.reshape(-1, 1, D), a_w_s, a_b_s, a_w_out_b)
            new_a_p.append(st_p)
            new_a_s.append(st_s)
            ia += 1
        elif kind == 1:
            x_all, st_p, st_s = _mixer_b(
                x_all, mod_p, mod_s, ng1, layer, ib, state_pool, b_w_in_b, b_w_grp_b,
                b_b_grp.reshape(-1, len(POOL_WINDOWS), 1, B_GROUP_DIM),
                b_scale.reshape(-1, 1, D), b_w_out_b)
            new_b_p.append(st_p)
            new_b_s.append(st_s)
            ib += 1
        else:
            x_all, st_p, st_s = _mixer_c(
                x_all, mod_p, mod_s, ng1, layer, ic, state_conv, c_w_in_b,
                c_b_in.reshape(-1, 1, 2 * D), c_w_dw, c_b_dw.reshape(-1, 1, D),
                c_ln_g.reshape(-1, 1, D), c_ln_b.reshape(-1, 1, D), c_w_out_b,
                c_b_out.reshape(-1, 1, D))
            new_c_p.append(st_p)
            new_c_s.append(st_s)
            ic += 1
        x_all = _moe(x_all, mod_p, mod_s, ng2, layer, w_r[layer], b_r[layer].reshape(1, LANES),
                     moe_w_gate, moe_w_up, moe_w_down)

    y_p, y_s = _final_norm(x_all, final_g.reshape(1, D))
    return (y_p.reshape(BATCH, SEQ, D), y_s.reshape(DEC_BATCH, 1, D),
            jnp.stack(new_a_p), jnp.stack(new_a_s), jnp.stack(new_b_p), jnp.stack(new_b_s),
            jnp.stack(new_c_p), jnp.stack(new_c_s))
```

```python
import functools

import jax
import jax.numpy as jnp
from jax import lax
from jax.experimental import pallas as pl
from jax.experimental.pallas import tpu as pltpu
from jax.experimental.pallas import tpu_sc as plsc

F32 = jnp.float32
BF16 = jnp.bfloat16

D = 1024
BATCH = 8
SEQ = 2048
DEPTH = 4
DEC_BATCH = 128
PAST_LEN = 16384
CHUNK = 128
A_HEADS = 8
POOL_WINDOWS = (2, 4, 8, 16)
B_GROUP_DIM = D // len(POOL_WINDOWS)
POOL_BUF = max(POOL_WINDOWS) - 1
CONV_WIDTH = 31
CONV_BUF = CONV_WIDTH - 1
N_GROUPS = 4
EXPERTS_PER_GROUP = 8
N_EXPERTS = N_GROUPS * EXPERTS_PER_GROUP
D_EXPERT = D // 2
EPS = 1e-6

LANES = 128
SUBLANES = 8
CONV_ROWS = 128
SC_CORES = 2
SC_SUBCORES = 16
TM = 512
TILES_PER_SEQ = SEQ // TM
T_PROMPT = BATCH * SEQ
T_ALL = T_PROMPT + DEC_BATCH
N_TILES = -(-T_ALL // TM)
T_PAD = N_TILES * TM
N_PROMPT_TILES = T_PROMPT // TM
SAMPLE_ROWS = 32
BM = 256
N_BLOCKS = -(-(2 * T_PAD) // BM) + N_EXPERTS
P_ROWS = N_BLOCKS * BM
META_ROWS = 8
POOL_CARRY = 16
CONV_CARRY = 32
EXPERT_LANE0 = N_GROUPS
VMEM_LIMIT = 56 * 1024 * 1024


def _cparams(*sem):
    return pltpu.CompilerParams(dimension_semantics=sem, vmem_limit_bytes=VMEM_LIMIT)


def _rms(x, g):
    return x * lax.rsqrt(jnp.mean(x * x, axis=-1, keepdims=True) + EPS) * g


def _ln(x, g, b):
    mu = jnp.mean(x, axis=-1, keepdims=True)
    xc = x - mu
    var = jnp.mean(xc * xc, axis=-1, keepdims=True)
    return xc * lax.rsqrt(var + EPS) * g + b


def _dot(a, b):
    return jnp.dot(a, b, preferred_element_type=F32)


def _pack_bf16_pairs(v):
    w = v.shape[1] // 2
    lo = lax.bitcast_convert_type(v[:, :w].astype(BF16).astype(F32), jnp.uint32)
    hi = lax.bitcast_convert_type(v[:, w:].astype(BF16).astype(F32), jnp.uint32)
    return lax.bitcast_convert_type(lax.shift_right_logical(lo, jnp.uint32(16)) | hi, jnp.int32)


def _unpack_bf16_pairs(words):
    u = lax.bitcast_convert_type(words, jnp.uint32)
    lo = lax.bitcast_convert_type(lax.shift_left(u, jnp.uint32(16)), F32)
    hi = lax.bitcast_convert_type(u & jnp.uint32(0xFFFF0000), F32)
    return jnp.concatenate([lo.astype(BF16), hi.astype(BF16)], axis=1)


def _ada_kernel(c_ref, w_ref, b_ref, o_ref):
    c = c_ref[...]
    s = (c * jax.nn.sigmoid(c)).astype(BF16)
    o_ref[...] = _dot(s, w_ref[...].astype(BF16)) + b_ref[...]


def _ada_mods(c_all, ada_w, ada_b):
    n = c_all.shape[0]
    tn = 2048
    return pl.pallas_call(
        _ada_kernel,
        grid=(DEPTH, 6 * D // tn),
        in_specs=[
            pl.BlockSpec((n, D), lambda l, j: (0, 0)),
            pl.BlockSpec((None, D, tn), lambda l, j: (l, 0, j)),
            pl.BlockSpec((None, 1, tn), lambda l, j: (l, 0, j)),
        ],
        out_specs=pl.BlockSpec((None, n, tn), lambda l, j: (l, 0, j)),
        out_shape=jax.ShapeDtypeStruct((DEPTH, n, 6 * D), F32),
        compiler_params=_cparams("parallel", "parallel"),
        name="ada_mods",
    )(c_all, ada_w, ada_b.reshape(DEPTH, 1, 6 * D))


def _mod_specs_prompt(layer, chunks):
    return [
        pl.BlockSpec((None, None, 1, D), lambda i, c=c: (layer, i // TILES_PER_SEQ, 0, c))
        for c in chunks
    ]


def _mod_specs_sample(layer, chunks, rows, row_block0=0):
    return [
        pl.BlockSpec((None, rows, D), lambda i, c=c: (layer, i + row_block0, c)) for c in chunks
    ]


def _full(shape):
    nd = len(shape)
    return pl.BlockSpec(shape, lambda i: (0,) * nd)


def _layer_block(shape, layer):
    nd = len(shape)
    return pl.BlockSpec((None,) + shape, lambda i: (layer,) + (0,) * nd)


def _mix_a_front(x, sh, sc, ng, win_ref, lng_ref, lnb_ref):
    h = (_rms(x, ng) * (1.0 + sc) + sh).astype(BF16)
    z = jax.nn.gelu(_dot(h, win_ref[...]))
    u = z[:, :D]
    v = _ln(z[:, D:], lng_ref[...], lnb_ref[...])
    return u, v


def _mix_a_prompt_kernel(x_ref, sh_ref, sc_ref, gt_ref, ng_ref, win_ref, lng_ref, lnb_ref,
                         ws_ref, bs_ref, wout_ref, x1_ref, cv_ref, mixed_ref):
    i = pl.program_id(0)
    x = x_ref[...]
    u, v = _mix_a_front(x, sh_ref[...], sc_ref[...], ng_ref[...], win_ref, lng_ref, lnb_ref)
    vb = v.astype(BF16)
    n_chunks = TM // CHUNK
    row = lax.broadcasted_iota(jnp.int32, (CHUNK, CHUNK), 0)
    col = lax.broadcasted_iota(jnp.int32, (CHUNK, CHUNK), 1)
    tril = row >= col
    hd_dim = D // A_HEADS
    for hd in range(A_HEADS):
        ws = jnp.where(tril, ws_ref[hd], 0.0).astype(BF16)
        cols = slice(hd * hd_dim, (hd + 1) * hd_dim)
        vcat = jnp.concatenate([vb[c * CHUNK:(c + 1) * CHUNK, cols] for c in range(n_chunks)], axis=1)
        m = _dot(ws, vcat)
        for c in range(n_chunks):
            mixed_ref[c * CHUNK:(c + 1) * CHUNK, cols] = m[:, c * hd_dim:(c + 1) * hd_dim] + bs_ref[hd]
    y = _dot((u * mixed_ref[...]).astype(BF16), wout_ref[...])
    x1_ref[...] = x + gt_ref[...] * y

    @pl.when(i % TILES_PER_SEQ == TILES_PER_SEQ - 1)
    def _():
        cv_ref[...] = v[TM - CHUNK:, :]


def _mix_a_sample_kernel(x_ref, sh_ref, sc_ref, gt_ref, ng_ref, win_ref, lng_ref, lnb_ref,
                         wsd_ref, bsd_ref, wout_ref, x1_ref, cv_ref):
    x = x_ref[...]
    u, v = _mix_a_front(x, sh_ref[...], sc_ref[...], ng_ref[...], win_ref, lng_ref, lnb_ref)
    mixed = wsd_ref[...] * v + bsd_ref[...]
    y = _dot((u * mixed).astype(BF16), wout_ref[...])
    x1_ref[...] = x + gt_ref[...] * y
    cv_ref[...] = v


def _mixer_a(x_all, mod_p, mod_s, ng, layer, la, w_in, ln_g, ln_b, w_s, b_s, w_out):
    bs_b = jnp.broadcast_to(b_s[la][:, :, None], (A_HEADS, CHUNK, D // A_HEADS))
    common = [
        _layer_block((D, 2 * D), la), _layer_block((1, D), la), _layer_block((1, D), la),
    ]
    x_all, cv_p = pl.pallas_call(
        _mix_a_prompt_kernel,
        grid=(N_PROMPT_TILES,),
        in_specs=[pl.BlockSpec((TM, D), lambda i: (i, 0))]
        + _mod_specs_prompt(layer, (0, 1, 2))
        + [_full((1, D))] + common
        + [_layer_block((A_HEADS, CHUNK, CHUNK), la), _full((A_HEADS, CHUNK, D // A_HEADS)),
           _layer_block((D, D), la)],
        out_specs=[
            pl.BlockSpec((TM, D), lambda i: (i, 0)),
            pl.BlockSpec((None, CHUNK, D), lambda i: (i // TILES_PER_SEQ, 0, 0)),
        ],
        out_shape=[
            jax.ShapeDtypeStruct((T_PAD, D), F32),
            jax.ShapeDtypeStruct((BATCH, CHUNK, D), F32),
        ],
        scratch_shapes=[pltpu.VMEM((TM, D), F32)],
        input_output_aliases={0: 0},
        compiler_params=_cparams("arbitrary"),
        name="mixer_a_prompt",
    )(x_all, mod_p, mod_p, mod_p, ng, w_in, ln_g, ln_b, w_s, bs_b, w_out)

    wsd = jnp.repeat(w_s[la, :, 0, 0], D // A_HEADS).reshape(1, D)
    bsd = jnp.repeat(b_s[la, :, 0], D // A_HEADS).reshape(1, D)
    sblk = T_PROMPT // DEC_BATCH
    x_all, cv_s = pl.pallas_call(
        _mix_a_sample_kernel,
        grid=(1,),
        in_specs=[pl.BlockSpec((DEC_BATCH, D), lambda i: (sblk, 0))]
        + _mod_specs_sample(layer, (0, 1, 2), DEC_BATCH)
        + [_full((1, D))] + common
        + [_full((1, D)), _full((1, D)), _layer_block((D, D), la)],
        out_specs=[
            pl.BlockSpec((DEC_BATCH, D), lambda i: (sblk, 0)),
            pl.BlockSpec((DEC_BATCH, D), lambda i: (0, 0)),
        ],
        out_shape=[
            jax.ShapeDtypeStruct((T_PAD, D), F32),
            jax.ShapeDtypeStruct((DEC_BATCH, D), F32),
        ],
        input_output_aliases={0: 0},
        compiler_params=_cparams("arbitrary"),
        name="mixer_a_sample",
    )(x_all, mod_s, mod_s, mod_s, ng, w_in, ln_g, ln_b, wsd, bsd, w_out)
    return x_all, cv_p, cv_s.reshape(DEC_BATCH, 1, D)


def _mix_b_tail(pooled_groups, wgrp_ref, bgrp_ref, scale_ref, wout_ref):
    outs = [
        _dot(pg.astype(BF16), wgrp_ref[g]) + bgrp_ref[g]
        for g, pg in enumerate(pooled_groups)
    ]
    mixed = jnp.concatenate(outs, axis=1) * scale_ref[...]
    return _dot(mixed.astype(BF16), wout_ref[...])


def _mix_b_prompt_kernel(x_ref, sh_ref, sc_ref, gt_ref, ng_ref, win_ref, wgrp_ref, bgrp_ref,
                         scale_ref, wout_ref, x1_ref, st_ref, full_ref):
    i = pl.program_id(0)
    j = i % TILES_PER_SEQ
    x = x_ref[...]
    h = (_rms(x, ng_ref[...]) * (1.0 + sc_ref[...]) + sh_ref[...]).astype(BF16)
    p = _dot(h, win_ref[...])

    @pl.when(j == 0)
    def _():
        full_ref[0:POOL_CARRY, :] = jnp.zeros((POOL_CARRY, D), F32)

    full_ref[POOL_CARRY:POOL_CARRY + TM, :] = p
    pos = j * TM + lax.broadcasted_iota(jnp.int32, (TM, 1), 0)
    pooled = []
    for g, w in enumerate(POOL_WINDOWS):
        cols = slice(g * B_GROUP_DIM, (g + 1) * B_GROUP_DIM)
        s = p[:, cols]
        for k in range(1, w):
            s = s + full_ref[pl.ds(POOL_CARRY - k, TM), cols]
        cnt = jnp.minimum(w, pos + 1).astype(F32)
        pooled.append(s / cnt - p[:, cols])
    y = _mix_b_tail(pooled, wgrp_ref, bgrp_ref, scale_ref, wout_ref)
    x1_ref[...] = x + gt_ref[...] * y

    @pl.when(j == TILES_PER_SEQ - 1)
    def _():
        st_ref[...] = full_ref[pl.ds(POOL_CARRY + TM - POOL_BUF, POOL_BUF), :]

    full_ref[0:POOL_CARRY, :] = full_ref[pl.ds(TM, POOL_CARRY), :]


def _mix_b_sample_kernel(x_ref, sh_ref, sc_ref, gt_ref, ng_ref, st_ref, win_ref, wgrp_ref,
                         bgrp_ref, scale_ref, wout_ref, x1_ref, nst_ref):
    x = x_ref[...]
    h = (_rms(x, ng_ref[...]) * (1.0 + sc_ref[...]) + sh_ref[...]).astype(BF16)
    p = _dot(h, win_ref[...])
    pooled = []
    for g, w in enumerate(POOL_WINDOWS):
        s = p[:, g * B_GROUP_DIM:(g + 1) * B_GROUP_DIM]
        for k in range(1, w):
            c0 = (POOL_BUF - k) * D + g * B_GROUP_DIM
            s = s + st_ref[:, c0:c0 + B_GROUP_DIM]
        cnt = float(min(w, PAST_LEN + 1))
        pooled.append(s / cnt - p[:, g * B_GROUP_DIM:(g + 1) * B_GROUP_DIM])
    y = _mix_b_tail(pooled, wgrp_ref, bgrp_ref, scale_ref, wout_ref)
    x1_ref[...] = x + gt_ref[...] * y
    nst_ref[:, 0:(POOL_BUF - 1) * D] = st_ref[:, D:POOL_BUF * D]
    nst_ref[:, (POOL_BUF - 1) * D:POOL_BUF * D] = p


def _mixer_b(x_all, mod_p, mod_s, ng, layer, lb, state, w_in, w_grp, b_grp, scale, w_out):
    gd = B_GROUP_DIM
    common = [
        _layer_block((D, D), lb), _layer_block((len(POOL_WINDOWS), gd, gd), lb),
        _layer_block((len(POOL_WINDOWS), 1, gd), lb), _layer_block((1, D), lb),
        _layer_block((D, D), lb),
    ]
    x_all, st_p = pl.pallas_call(
        _mix_b_prompt_kernel,
        grid=(N_PROMPT_TILES,),
        in_specs=[pl.BlockSpec((TM, D), lambda i: (i, 0))]
        + _mod_specs_prompt(layer, (0, 1, 2)) + [_full((1, D))] + common,
        out_specs=[
            pl.BlockSpec((TM, D), lambda i: (i, 0)),
            pl.BlockSpec((None, POOL_BUF, D), lambda i: (i // TILES_PER_SEQ, 0, 0)),
        ],
        out_shape=[
            jax.ShapeDtypeStruct((T_PAD, D), F32),
            jax.ShapeDtypeStruct((BATCH, POOL_BUF, D), F32),
        ],
        scratch_shapes=[pltpu.VMEM((TM + POOL_CARRY, D), F32)],
        input_output_aliases={0: 0},
        compiler_params=_cparams("arbitrary"),
        name="mixer_b_prompt",
    )(x_all, mod_p, mod_p, mod_p, ng, w_in, w_grp, b_grp, scale, w_out)

    r = SAMPLE_ROWS
    sblk = T_PROMPT // r
    st2 = state[lb].reshape(DEC_BATCH, POOL_BUF * D)
    x_all, st_s = pl.pallas_call(
        _mix_b_sample_kernel,
        grid=(DEC_BATCH // r,),
        in_specs=[pl.BlockSpec((r, D), lambda i: (sblk + i, 0))]
        + _mod_specs_sample(layer, (0, 1, 2), r) + [_full((1, D))]
        + [pl.BlockSpec((r, POOL_BUF * D), lambda i: (i, 0))] + common,
        out_specs=[
            pl.BlockSpec((r, D), lambda i: (sblk + i, 0)),
            pl.BlockSpec((r, POOL_BUF * D), lambda i: (i, 0)),
        ],
        out_shape=[
            jax.ShapeDtypeStruct((T_PAD, D), F32),
            jax.ShapeDtypeStruct((DEC_BATCH, POOL_BUF * D), F32),
        ],
        input_output_aliases={0: 0},
        compiler_params=_cparams("arbitrary"),
        name="mixer_b_sample",
    )(x_all, mod_s, mod_s, mod_s, ng, st2, w_in, w_grp, b_grp, scale, w_out)
    return x_all, st_p, st_s.reshape(DEC_BATCH, POOL_BUF, D)


def _mix_c_glu(x, sh, sc, ng, win_ref, bin_ref):
    h = (_rms(x, ng) * (1.0 + sc) + sh).astype(BF16)
    ag = _dot(h, win_ref[...]) + bin_ref[...]
    return ag[:, :D] * jax.nn.sigmoid(ag[:, D:])


def _mix_c_tail(conv, lng_ref, lnb_ref, wout_ref, bout_ref):
    z = _ln(conv, lng_ref[...], lnb_ref[...])
    z = z * jax.nn.sigmoid(z)
    return _dot(z.astype(BF16), wout_ref[...]) + bout_ref[...]


def _mix_c_prompt_kernel(x_ref, sh_ref, sc_ref, gt_ref, ng_ref, win_ref, bin_ref, wdw_ref,
                         bdw_ref, lng_ref, lnb_ref, wout_ref, bout_ref, x1_ref, st_ref,
                         full_ref, conv_ref, *shift_refs):
    i = pl.program_id(0)
    j = i % TILES_PER_SEQ
    x = x_ref[...]
    glu = _mix_c_glu(x, sh_ref[...], sc_ref[...], ng_ref[...], win_ref, bin_ref)

    @pl.when(j == 0)
    def _():
        full_ref[0:CONV_CARRY, :] = jnp.zeros((CONV_CARRY, D), F32)

    full_ref[CONV_CARRY:CONV_CARRY + TM, :] = glu
    off = CONV_CARRY - CONV_BUF
    span = TM + (off + CONV_BUF) // SUBLANES * SUBLANES - SUBLANES
    for c in range(D // LANES):
        cols = slice(c * LANES, (c + 1) * LANES)
        shift_ref = shift_refs[c % len(shift_refs)]
        for r in range(1, SUBLANES):
            shift_ref[r - 1] = full_ref[pl.ds(r, span), cols]
        for rb in range(TM // CONV_ROWS):
            acc = None
            for k in range(CONV_WIDTH):
                q, r = divmod(off + k, SUBLANES)
                start = q * SUBLANES + rb * CONV_ROWS
                if r == 0:
                    src = full_ref[pl.ds(start, CONV_ROWS), cols]
                else:
                    src = shift_ref[r - 1, pl.ds(start, CONV_ROWS), :]
                term = src * wdw_ref[k:k + 1, cols]
                acc = term if acc is None else acc + term
            conv_ref[pl.ds(rb * CONV_ROWS, CONV_ROWS), cols] = acc + bdw_ref[:, cols]
    y = _mix_c_tail(conv_ref[...], lng_ref, lnb_ref, wout_ref, bout_ref)
    x1_ref[...] = x + gt_ref[...] * y

    @pl.when(j == TILES_PER_SEQ - 1)
    def _():
        st_ref[...] = full_ref[pl.ds(CONV_CARRY + TM - CONV_BUF, CONV_BUF), :]

    full_ref[0:CONV_CARRY, :] = full_ref[pl.ds(TM, CONV_CARRY), :]


def _mix_c_sample_kernel(x_ref, sh_ref, sc_ref, gt_ref, ng_ref, st_ref, win_ref, bin_ref,
                         wdw_ref, bdw_ref, lng_ref, lnb_ref, wout_ref, bout_ref, x1_ref, nst_ref):
    x = x_ref[...]
    glu = _mix_c_glu(x, sh_ref[...], sc_ref[...], ng_ref[...], win_ref, bin_ref)
    acc = glu * wdw_ref[CONV_BUF:CONV_BUF + 1, :]
    for k in range(CONV_BUF):
        acc = acc + st_ref[:, k * D:(k + 1) * D] * wdw_ref[k:k + 1, :]
    y = _mix_c_tail(acc + bdw_ref[...], lng_ref, lnb_ref, wout_ref, bout_ref)
    x1_ref[...] = x + gt_ref[...] * y
    nst_ref[:, 0:(CONV_BUF - 1) * D] = st_ref[:, D:CONV_BUF * D]
    nst_ref[:, (CONV_BUF - 1) * D:CONV_BUF * D] = glu


def _mixer_c(x_all, mod_p, mod_s, ng, layer, lc, state, w_in, b_in, w_dw, b_dw, ln_g, ln_b,
             w_out, b_out):
    common = [
        _layer_block((D, 2 * D), lc), _layer_block((1, 2 * D), lc),
        _layer_block((CONV_WIDTH, D), lc), _layer_block((1, D), lc), _layer_block((1, D), lc),
        _layer_block((1, D), lc), _layer_block((D, D), lc), _layer_block((1, D), lc),
    ]
    x_all, st_p = pl.pallas_call(
        _mix_c_prompt_kernel,
        grid=(N_PROMPT_TILES,),
        in_specs=[pl.BlockSpec((TM, D), lambda i: (i, 0))]
        + _mod_specs_prompt(layer, (0, 1, 2)) + [_full((1, D))] + common,
        out_specs=[
            pl.BlockSpec((TM, D), lambda i: (i, 0)),
            pl.BlockSpec((None, CONV_BUF, D), lambda i: (i // TILES_PER_SEQ, 0, 0)),
        ],
        out_shape=[
            jax.ShapeDtypeStruct((T_PAD, D), F32),
            jax.ShapeDtypeStruct((BATCH, CONV_BUF, D), F32),
        ],
        scratch_shapes=[pltpu.VMEM((TM + CONV_CARRY, D), F32), pltpu.VMEM((TM, D), F32)]
        + [pltpu.VMEM((SUBLANES - 1, TM + CONV_CARRY - SUBLANES, LANES), F32)] * 2,
        input_output_aliases={0: 0},
        compiler_params=_cparams("arbitrary"),
        name="mixer_c_prompt",
    )(x_all, mod_p, mod_p, mod_p, ng, w_in, b_in, w_dw, b_dw, ln_g, ln_b, w_out, b_out)

    r = SAMPLE_ROWS
    sblk = T_PROMPT // r
    st2 = state[lc].reshape(DEC_BATCH, CONV_BUF * D)
    x_all, st_s = pl.pallas_call(
        _mix_c_sample_kernel,
        grid=(DEC_BATCH // r,),
        in_specs=[pl.BlockSpec((r, D), lambda i: (sblk + i, 0))]
        + _mod_specs_sample(layer, (0, 1, 2), r) + [_full((1, D))]
        + [pl.BlockSpec((r, CONV_BUF * D), lambda i: (i, 0))] + common,
        out_specs=[
            pl.BlockSpec((r, D), lambda i: (sblk + i, 0)),
            pl.BlockSpec((r, CONV_BUF * D), lambda i: (i, 0)),
        ],
        out_shape=[
            jax.ShapeDtypeStruct((T_PAD, D), F32),
            jax.ShapeDtypeStruct((DEC_BATCH, CONV_BUF * D), F32),
        ],
        input_output_aliases={0: 0},
        compiler_params=_cparams("arbitrary"),
        name="mixer_c_sample",
    )(x_all, mod_s, mod_s, mod_s, ng, st2, w_in, b_in, w_dw, b_dw, ln_g, ln_b, w_out, b_out)
    return x_all, st_p, st_s.reshape(DEC_BATCH, CONV_BUF, D)


def _tile_mod(i, p_ref, s_ref):
    s_rows = jnp.concatenate([s_ref[...], jnp.zeros((TM - DEC_BATCH, D), F32)], axis=0)
    return jnp.where(i >= N_PROMPT_TILES, s_rows, p_ref[...])


def _unified_mod_specs(layer, chunks):
    specs = []
    for c in chunks:
        specs.append(pl.BlockSpec(
            (None, None, 1, D),
            lambda i, c=c: (layer, jnp.minimum(i // TILES_PER_SEQ, BATCH - 1), 0, c)))
        specs.append(pl.BlockSpec((None, DEC_BATCH, D), lambda i, c=c: (layer, 0, c)))
    return specs


def _router_kernel(x_ref, shp_ref, shs_ref, scp_ref, scs_ref, ng_ref, wr_ref, br_ref,
                   h2_ref, meta_ref, wts_ref, cnt_ref, carry_ref):
    i = pl.program_id(0)

    @pl.when(i == 0)
    def _():
        carry_ref[...] = jnp.zeros((1, LANES), F32)

    sh = _tile_mod(i, shp_ref, shs_ref)
    sc = _tile_mod(i, scp_ref, scs_ref)
    h2 = _rms(x_ref[...], ng_ref[...]) * (1.0 + sc) + sh
    h2_ref[...] = _pack_bf16_pairs(h2)
    lg = jnp.dot(h2, wr_ref[...], preferred_element_type=F32,
                 precision=lax.Precision.HIGHEST) + br_ref[...]
    lane = lax.broadcasted_iota(jnp.int32, (TM, LANES), 1)
    neg = -jnp.inf
    is_g = lane < N_GROUPS
    gm = jnp.where(is_g, lg, neg)
    gmax = jnp.max(gm, axis=1, keepdims=True)
    lane_f = lane.astype(F32)
    far = float(LANES)
    gsel = jnp.min(jnp.where(gm == gmax, lane_f, far), axis=1, keepdims=True).astype(jnp.int32)
    gsum = jnp.sum(jnp.where(is_g, jnp.exp(lg - gmax), 0.0), axis=1, keepdims=True)
    g_w = 1.0 / gsum
    e_lane = lane - EXPERT_LANE0
    in_grp = (e_lane >= 0) & (e_lane < N_EXPERTS) & (
        lax.shift_right_arithmetic(e_lane, EXPERTS_PER_GROUP.bit_length() - 1) == gsel)
    em = jnp.where(in_grp, lg, neg)
    m1 = jnp.max(em, axis=1, keepdims=True)
    i1 = jnp.min(jnp.where(em == m1, lane_f, far), axis=1, keepdims=True).astype(jnp.int32)
    em2 = jnp.where(lane == i1, neg, em)
    m2 = jnp.max(em2, axis=1, keepdims=True)
    i2 = jnp.min(jnp.where(em2 == m2, lane_f, far), axis=1, keepdims=True).astype(jnp.int32)
    e2 = jnp.exp(m2 - m1)
    den = 1.0 + e2
    w1 = (1.0 / den) * g_w
    w2 = (e2 / den) * g_w

    hit1 = lane == i1
    hit2 = lane == i2
    assign = jnp.where(hit1 | hit2, 1.0, 0.0).astype(BF16)
    r = lax.broadcasted_iota(jnp.int32, (TM, TM), 0)
    c = lax.broadcasted_iota(jnp.int32, (TM, TM), 1)
    before = (c < r).astype(BF16)
    seen = _dot(before, assign) + carry_ref[...]
    rank1 = jnp.sum(jnp.where(hit1, seen, 0.0), axis=1, keepdims=True)
    rank2 = jnp.sum(jnp.where(hit2, seen, 0.0), axis=1, keepdims=True)
    carry_ref[...] = carry_ref[...] + jnp.sum(assign.astype(F32), axis=0, keepdims=True)
    cnt_ref[...] = carry_ref[...]

    ex1 = (i1 - EXPERT_LANE0).astype(F32)
    ex2 = (i2 - EXPERT_LANE0).astype(F32)
    fields = jnp.where(lane == 0, ex1, jnp.where(lane == 1, ex2,
                       jnp.where(lane == 2, rank1, jnp.where(lane == 3, rank2, 0.0))))
    meta_ref[...] = fields.T[0:META_ROWS, :]
    wts_ref[...] = jnp.where(lane == 0, w1, jnp.where(lane == 1, w2, 0.0))


def _router(x_all, mod_p, mod_s, ng, layer, w_r, b_r):
    return pl.pallas_call(
        _router_kernel,
        grid=(N_TILES,),
        in_specs=[pl.BlockSpec((TM, D), lambda i: (i, 0))]
        + _unified_mod_specs(layer, (3, 4))
        + [_full((1, D)), _full((D, LANES)), _full((1, LANES))],
        out_specs=[
            pl.BlockSpec((TM, D // 2), lambda i: (i, 0)),
            pl.BlockSpec((META_ROWS, TM), lambda i: (0, i)),
            pl.BlockSpec((TM, LANES), lambda i: (i, 0)),
            pl.BlockSpec((1, LANES), lambda i: (0, 0)),
        ],
        out_shape=[
            jax.ShapeDtypeStruct((T_PAD, D // 2), jnp.int32),
            jax.ShapeDtypeStruct((META_ROWS, T_PAD), F32),
            jax.ShapeDtypeStruct((T_PAD, LANES), F32),
            jax.ShapeDtypeStruct((1, LANES), F32),
        ],
        scratch_shapes=[pltpu.VMEM((1, LANES), F32)],
        compiler_params=_cparams("arbitrary"),
        name="router",
    )(x_all, mod_p, mod_s, mod_p, mod_s, ng, w_r, b_r)


SC_INDEX_MAX = 128
SC_ROWS_BYTES = 256 * 1024


def _gather_chunk(per_worker, row_bytes):
    top = min(SC_INDEX_MAX, SC_ROWS_BYTES // row_bytes) // SUBLANES * SUBLANES
    for ch in range(top, SUBLANES - 1, -SUBLANES):
        if per_worker % ch == 0:
            return ch
    raise ValueError(per_worker)


def _sc_gather(table, idx):
    nc, nw = SC_CORES, SC_CORES * SC_SUBCORES
    m = idx.shape[0]
    per_w = m // nw
    assert per_w * nw == m
    width = table.shape[1]
    ch = _gather_chunk(per_w, width * table.dtype.itemsize)
    mesh = plsc.VectorSubcoreMesh(core_axis_name="c", subcore_axis_name="s")

    @functools.partial(
        pl.kernel,
        out_type=jax.ShapeDtypeStruct((m, width), table.dtype),
        mesh=mesh,
        scratch_types=[
            pltpu.VMEM((ch,), jnp.int32),
            pltpu.VMEM((ch, width), table.dtype),
            pltpu.SemaphoreType.DMA,
        ],
    )
    def gather_kernel(t_hbm, i_hbm, o_hbm, idx_v, rows_v, sem):
        wid = lax.axis_index("s") * nc + lax.axis_index("c")
        base = wid * per_w

        @pl.loop(0, per_w // ch)
        def _(j):
            off = pl.multiple_of(base + j * ch, 8)
            pltpu.sync_copy(i_hbm.at[pl.ds(off, ch)], idx_v)
            pltpu.async_copy(t_hbm.at[idx_v], rows_v, sem).wait()
            pltpu.sync_copy(rows_v, o_hbm.at[pl.ds(off, ch)])

    return gather_kernel(table, idx)


def _sc_scatter2(rows, dest0, dest1, n_out):
    nc, nw = SC_CORES, SC_CORES * SC_SUBCORES
    n, width = rows.shape
    per_w = n // nw
    assert per_w * nw == n
    ch = _gather_chunk(per_w, width * rows.dtype.itemsize)
    mesh = plsc.VectorSubcoreMesh(core_axis_name="c", subcore_axis_name="s")

    @functools.partial(
        pl.kernel,
        out_type=jax.ShapeDtypeStruct((n_out, width), rows.dtype),
        mesh=mesh,
        scratch_types=[
            pltpu.VMEM((ch,), jnp.int32),
            pltpu.VMEM((ch,), jnp.int32),
            pltpu.VMEM((ch, width), rows.dtype),
            pltpu.SemaphoreType.DMA,
            pltpu.SemaphoreType.DMA,
        ],
    )
    def scatter_kernel(r_hbm, d0_hbm, d1_hbm, o_hbm, i0_v, i1_v, rows_v, sem0, sem1):
        wid = lax.axis_index("s") * nc + lax.axis_index("c")
        base = wid * per_w

        @pl.loop(0, per_w // ch)
        def _(j):
            off = pl.multiple_of(base + j * ch, 8)
            pltpu.sync_copy(d0_hbm.at[pl.ds(off, ch)], i0_v)
            pltpu.sync_copy(d1_hbm.at[pl.ds(off, ch)], i1_v)
            pltpu.sync_copy(r_hbm.at[pl.ds(off, ch)], rows_v)
            c0 = pltpu.async_copy(rows_v, o_hbm.at[i0_v], sem0)
            c1 = pltpu.async_copy(rows_v, o_hbm.at[i1_v], sem1)
            c0.wait()
            c1.wait()

    return scatter_kernel(rows, dest0, dest1)


def _expert_kernel(be_ref, first_ref, nvalid_ref, nused_ref, x_ref, wg_ref, wu_ref, wd_ref,
                   y_ref, wgb_ref, wub_ref, wdb_ref):
    b = pl.program_id(0)

    @pl.when(b < nused_ref[0])
    def _():
        @pl.when(first_ref[b] == 1)
        def _():
            wgb_ref[...] = wg_ref[...].astype(BF16)
            wub_ref[...] = wu_ref[...].astype(BF16)
            wdb_ref[...] = wd_ref[...].astype(BF16)

        live = lax.broadcasted_iota(jnp.int32, (BM, 1), 0) < nvalid_ref[b]
        x = _unpack_bf16_pairs(jnp.where(live, x_ref[...], 0))
        g = _dot(x, wgb_ref[...])
        u = _dot(x, wub_ref[...])
        hmid = (g * jax.nn.sigmoid(g)) * u
        y_ref[...] = _dot(hmid.astype(BF16), wdb_ref[...])


def _experts(x_sorted, block_expert, first, nvalid, nused, layer, w_gate, w_up, w_down):
    def row_map(b, be, fi, nv, nu):
        return (jnp.minimum(b, nu[0] - 1), 0)

    def w_map(b, be, fi, nv, nu):
        return (layer, be[b], 0, 0)

    grid_spec = pltpu.PrefetchScalarGridSpec(
        num_scalar_prefetch=4,
        grid=(N_BLOCKS,),
        in_specs=[
            pl.BlockSpec((BM, D // 2), row_map),
            pl.BlockSpec((None, None, D, D_EXPERT), w_map),
            pl.BlockSpec((None, None, D, D_EXPERT), w_map),
            pl.BlockSpec((None, None, D_EXPERT, D), w_map),
        ],
        out_specs=pl.BlockSpec((BM, D), row_map),
        scratch_shapes=[
            pltpu.VMEM((D, D_EXPERT), BF16),
            pltpu.VMEM((D, D_EXPERT), BF16),
            pltpu.VMEM((D_EXPERT, D), BF16),
        ],
    )
    return pl.pallas_call(
        _expert_kernel,
        grid_spec=grid_spec,
        out_shape=jax.ShapeDtypeStruct((P_ROWS, D), F32),
        compiler_params=_cparams("arbitrary"),
        name="experts",
    )(block_expert, first, nvalid, nused, x_sorted, w_gate, w_up, w_down)


def _combine_kernel(x_ref, g0_ref, g1_ref, wts_ref, gtp_ref, gts_ref, o_ref):
    i = pl.program_id(0)
    gt = _tile_mod(i, gtp_ref, gts_ref)
    w = wts_ref[...]
    moe = w[:, 0:1] * g0_ref[...] + w[:, 1:2] * g1_ref[...]
    o_ref[...] = x_ref[...] + gt * moe


def _combine(x_all, gathered, wts, mod_p, mod_s, layer):
    return pl.pallas_call(
        _combine_kernel,
        grid=(N_TILES,),
        in_specs=[
            pl.BlockSpec((TM, D), lambda i: (i, 0)),
            pl.BlockSpec((TM, D), lambda i: (i, 0)),
            pl.BlockSpec((TM, D), lambda i: (i + N_TILES, 0)),
            pl.BlockSpec((TM, LANES), lambda i: (i, 0)),
        ] + _unified_mod_specs(layer, (5,)),
        out_specs=pl.BlockSpec((TM, D), lambda i: (i, 0)),
        out_shape=jax.ShapeDtypeStruct((T_PAD, D), F32),
        input_output_aliases={0: 0},
        compiler_params=_cparams("arbitrary"),
        name="combine",
    )(x_all, gathered, gathered, wts, mod_p, mod_s)


def _moe(x_all, mod_p, mod_s, ng, layer, w_r, b_r, w_gate, w_up, w_down):
    h2, meta, wts, cnt = _router(x_all, mod_p, mod_s, ng, layer, w_r, b_r)
    counts = cnt[0, EXPERT_LANE0:EXPERT_LANE0 + N_EXPERTS].astype(jnp.int32)
    padded = (counts + BM - 1) // BM * BM
    pend = jnp.cumsum(padded)
    pstart = pend - padded
    expert = meta[0:2].astype(jnp.int32)
    dest = meta[2:4].astype(jnp.int32)
    for e in range(N_EXPERTS):
        dest = dest + jnp.where(expert == e, pstart[e], 0)
    block0 = jnp.arange(N_BLOCKS, dtype=jnp.int32) * BM
    block_expert = jnp.minimum(
        jnp.sum((pend[None, :] <= block0[:, None]).astype(jnp.int32), axis=1), N_EXPERTS - 1)
    of_block = block_expert[:, None] == jnp.arange(N_EXPERTS, dtype=jnp.int32)[None, :]
    seg_end = jnp.sum(jnp.where(of_block, (pstart + counts)[None, :], 0), axis=1)
    nvalid = jnp.clip(seg_end - block0, 0, BM).astype(jnp.int32)
    nused = (pend[-1] // BM).astype(jnp.int32).reshape(1)
    first = jnp.concatenate(
        [jnp.ones((1,), jnp.int32), (block_expert[1:] != block_expert[:-1]).astype(jnp.int32)])

    x_sorted = _sc_scatter2(h2, dest[0], dest[1], P_ROWS)
    y_sorted = _experts(x_sorted, block_expert, first, nvalid, nused, layer, w_gate, w_up, w_down)
    gathered = _sc_gather(y_sorted, dest.reshape(-1))
    return _combine(x_all, gathered, wts, mod_p, mod_s, layer)


def _final_kernel(x_ref, g_ref, yp_ref, ys_ref):
    i = pl.program_id(0)
    y = _rms(x_ref[...], g_ref[...])

    @pl.when(i < N_PROMPT_TILES)
    def _():
        yp_ref[...] = y

    @pl.when(i == N_PROMPT_TILES)
    def _():
        ys_ref[...] = y[0:DEC_BATCH, :]


def _final_norm(x_all, g):
    return pl.pallas_call(
        _final_kernel,
        grid=(N_PROMPT_TILES + 1,),
        in_specs=[pl.BlockSpec((TM, D), lambda i: (i, 0)), _full((1, D))],
        out_specs=[
            pl.BlockSpec((TM, D), lambda i: (jnp.minimum(i, N_PROMPT_TILES - 1), 0)),
            pl.BlockSpec((DEC_BATCH, D), lambda i: (0, 0)),
        ],
        out_shape=[
            jax.ShapeDtypeStruct((T_PROMPT, D), F32),
            jax.ShapeDtypeStruct((DEC_BATCH, D), F32),
        ],
        compiler_params=_cparams("arbitrary"),
        name="final_norm",
    )(x_all, g)


def kernel(x_prompt, x_sample, c_prompt, c_sample, state_pool, state_conv, ada_w, ada_b, norm_g, final_g, a_w_in, a_ln_g, a_ln_b, a_w_s, a_b_s, a_w_out, b_w_in, b_w_grp, b_b_grp, b_scale, b_w_out, c_w_in, c_b_in, c_w_dw, c_b_dw, c_ln_g, c_ln_b, c_w_out, c_b_out, moe_w_grp, moe_b_grp, moe_w_exp, moe_b_exp, moe_w_gate, moe_w_up, moe_w_down):
    x_all = jnp.concatenate([
        x_prompt.reshape(T_PROMPT, D), x_sample.reshape(DEC_BATCH, D),
        jnp.zeros((T_PAD - T_ALL, D), F32)], axis=0)
    mods = _ada_mods(jnp.concatenate([c_prompt, c_sample], axis=0), ada_w, ada_b)
    mod_p = mods[:, :BATCH].reshape(DEPTH, BATCH, 1, 6 * D)
    mod_s = mods[:, BATCH:]

    a_w_in_b, a_w_out_b = a_w_in.astype(BF16), a_w_out.astype(BF16)
    b_w_in_b, b_w_grp_b, b_w_out_b = b_w_in.astype(BF16), b_w_grp.astype(BF16), b_w_out.astype(BF16)
    c_w_in_b, c_w_out_b = c_w_in.astype(BF16), c_w_out.astype(BF16)
    w_r = jnp.pad(jnp.concatenate([moe_w_grp, moe_w_exp], axis=2),
                  ((0, 0), (0, 0), (0, LANES - N_GROUPS - N_EXPERTS)))
    b_r = jnp.pad(jnp.concatenate([moe_b_grp, moe_b_exp], axis=1),
                  ((0, 0), (0, LANES - N_GROUPS - N_EXPERTS)))

    new_a_p, new_a_s, new_b_p, new_b_s, new_c_p, new_c_s = [], [], [], [], [], []
    ia = ib = ic = 0
    for layer in range(DEPTH):
        ng1 = norm_g[layer, 0].reshape(1, D)
        ng2 = norm_g[layer, 1].reshape(1, D)
        kind = layer % 3
        if kind == 0:
            x_all, st_p, st_s = _mixer_a(
                x_all, mod_p, mod_s, ng1, layer, ia, a_w_in_b, a_ln_g.reshape(-1, 1, D),
                a_ln_b.reshape(-1, 1, D), a_w_s, a_b_s, a_w_out_b)
            new_a_p.append(st_p)
            new_a_s.append(st_s)
            ia += 1
        elif kind == 1:
            x_all, st_p, st_s = _mixer_b(
                x_all, mod_p, mod_s, ng1, layer, ib, state_pool, b_w_in_b, b_w_grp_b,
                b_b_grp.reshape(-1, len(POOL_WINDOWS), 1, B_GROUP_DIM),
                b_scale.reshape(-1, 1, D), b_w_out_b)
            new_b_p.append(st_p)
            new_b_s.append(st_s)
            ib += 1
        else:
            x_all, st_p, st_s = _mixer_c(
                x_all, mod_p, mod_s, ng1, layer, ic, state_conv, c_w_in_b,
                c_b_in.reshape(-1, 1, 2 * D), c_w_dw, c_b_dw.reshape(-1, 1, D),
                c_ln_g.reshape(-1, 1, D), c_ln_b.reshape(-1, 1, D), c_w_out_b,
                c_b_out.reshape(-1, 1, D))
            new_c_p.append(st_p)
            new_c_s.append(st_s)
            ic += 1
        x_all = _moe(x_all, mod_p, mod_s, ng2, layer, w_r[layer], b_r[layer].reshape(1, LANES),
                     moe_w_gate, moe_w_up, moe_w_down)

    y_p, y_s = _final_norm(x_all, final_g.reshape(1, D))
    return (y_p.reshape(BATCH, SEQ, D), y_s.reshape(DEC_BATCH, 1, D),
            jnp.stack(new_a_p), jnp.stack(new_a_s), jnp.stack(new_b_p), jnp.stack(new_b_s),
            jnp.stack(new_c_p), jnp.stack(new_c_s))
```

```python
import functools

import jax
import jax.numpy as jnp
from jax import lax
from jax.experimental import pallas as pl
from jax.experimental.pallas import tpu as pltpu
from jax.experimental.pallas import tpu_sc as plsc

F32 = jnp.float32
BF16 = jnp.bfloat16

D = 1024
BATCH = 8
SEQ = 2048
DEPTH = 4
DEC_BATCH = 128
PAST_LEN = 16384
CHUNK = 128
A_HEADS = 8
POOL_WINDOWS = (2, 4, 8, 16)
B_GROUP_DIM = D // len(POOL_WINDOWS)
POOL_BUF = max(POOL_WINDOWS) - 1
CONV_WIDTH = 31
CONV_BUF = CONV_WIDTH - 1
N_GROUPS = 4
EXPERTS_PER_GROUP = 8
N_EXPERTS = N_GROUPS * EXPERTS_PER_GROUP
D_EXPERT = D // 2
EPS = 1e-6

LANES = 128
SUBLANES = 8
CONV_ROWS = 128
SC_CORES = 2
SC_SUBCORES = 16
TM = 512
TILES_PER_SEQ = SEQ // TM
T_PROMPT = BATCH * SEQ
T_ALL = T_PROMPT + DEC_BATCH
N_TILES = -(-T_ALL // TM)
T_PAD = N_TILES * TM
N_PROMPT_TILES = T_PROMPT // TM
SAMPLE_ROWS = 32
BM = 256
N_BLOCKS = -(-(2 * T_PAD) // BM) + N_EXPERTS
P_ROWS = N_BLOCKS * BM
META_ROWS = 8
POOL_CARRY = 16
CONV_CARRY = 32
EXPERT_LANE0 = N_GROUPS
VMEM_LIMIT = 56 * 1024 * 1024


def _cparams(*sem):
    return pltpu.CompilerParams(dimension_semantics=sem, vmem_limit_bytes=VMEM_LIMIT)


def _rms(x, g):
    return x * lax.rsqrt(jnp.mean(x * x, axis=-1, keepdims=True) + EPS) * g


def _ln(x, g, b):
    mu = jnp.mean(x, axis=-1, keepdims=True)
    xc = x - mu
    var = jnp.mean(xc * xc, axis=-1, keepdims=True)
    return xc * lax.rsqrt(var + EPS) * g + b


def _dot(a, b):
    return jnp.dot(a, b, preferred_element_type=F32)


def _pack_bf16_pairs(v):
    w = v.shape[1] // 2
    lo = lax.bitcast_convert_type(v[:, :w].astype(BF16).astype(F32), jnp.uint32)
    hi = lax.bitcast_convert_type(v[:, w:].astype(BF16).astype(F32), jnp.uint32)
    return lax.bitcast_convert_type(lax.shift_right_logical(lo, jnp.uint32(16)) | hi, jnp.int32)


def _unpack_bf16_pairs(words, dtype):
    u = lax.bitcast_convert_type(words, jnp.uint32)
    lo = lax.bitcast_convert_type(lax.shift_left(u, jnp.uint32(16)), F32)
    hi = lax.bitcast_convert_type(u & jnp.uint32(0xFFFF0000), F32)
    return jnp.concatenate([lo.astype(dtype), hi.astype(dtype)], axis=1)


def _ada_kernel(c_ref, w_ref, b_ref, o_ref):
    c = c_ref[...]
    s = (c * jax.nn.sigmoid(c)).astype(BF16)
    o_ref[...] = _dot(s, w_ref[...].astype(BF16)) + b_ref[...]


def _ada_mods(c_all, ada_w, ada_b):
    n = c_all.shape[0]
    tn = 2048
    return pl.pallas_call(
        _ada_kernel,
        grid=(DEPTH, 6 * D // tn),
        in_specs=[
            pl.BlockSpec((n, D), lambda l, j: (0, 0)),
            pl.BlockSpec((None, D, tn), lambda l, j: (l, 0, j)),
            pl.BlockSpec((None, 1, tn), lambda l, j: (l, 0, j)),
        ],
        out_specs=pl.BlockSpec((None, n, tn), lambda l, j: (l, 0, j)),
        out_shape=jax.ShapeDtypeStruct((DEPTH, n, 6 * D), F32),
        compiler_params=_cparams("parallel", "parallel"),
        name="ada_mods",
    )(c_all, ada_w, ada_b.reshape(DEPTH, 1, 6 * D))


def _mod_specs_prompt(layer, chunks):
    return [
        pl.BlockSpec((None, None, 1, D), lambda i, c=c: (layer, i // TILES_PER_SEQ, 0, c))
        for c in chunks
    ]


def _mod_specs_sample(layer, chunks, rows, row_block0=0):
    return [
        pl.BlockSpec((None, rows, D), lambda i, c=c: (layer, i + row_block0, c)) for c in chunks
    ]


def _full(shape):
    nd = len(shape)
    return pl.BlockSpec(shape, lambda i: (0,) * nd)


def _layer_block(shape, layer):
    nd = len(shape)
    return pl.BlockSpec((None,) + shape, lambda i: (layer,) + (0,) * nd)


def _mix_a_front(x, sh, sc, ng, win_ref, lng_ref, lnb_ref):
    h = (_rms(x, ng) * (1.0 + sc) + sh).astype(BF16)
    z = jax.nn.gelu(_dot(h, win_ref[...]))
    u = z[:, :D]
    v = _ln(z[:, D:], lng_ref[...], lnb_ref[...])
    return u, v


def _mix_a_prompt_kernel(x_ref, sh_ref, sc_ref, gt_ref, ng_ref, win_ref, lng_ref, lnb_ref,
                         ws_ref, bs_ref, wout_ref, x1_ref, cv_ref, mixed_ref):
    i = pl.program_id(0)
    x = x_ref[...]
    u, v = _mix_a_front(x, sh_ref[...], sc_ref[...], ng_ref[...], win_ref, lng_ref, lnb_ref)
    vb = v.astype(BF16)
    n_chunks = TM // CHUNK
    row = lax.broadcasted_iota(jnp.int32, (CHUNK, CHUNK), 0)
    col = lax.broadcasted_iota(jnp.int32, (CHUNK, CHUNK), 1)
    tril = row >= col
    hd_dim = D // A_HEADS
    for hd in range(A_HEADS):
        ws = jnp.where(tril, ws_ref[hd], 0.0).astype(BF16)
        cols = slice(hd * hd_dim, (hd + 1) * hd_dim)
        vcat = jnp.concatenate([vb[c * CHUNK:(c + 1) * CHUNK, cols] for c in range(n_chunks)], axis=1)
        m = _dot(ws, vcat)
        for c in range(n_chunks):
            mixed_ref[c * CHUNK:(c + 1) * CHUNK, cols] = m[:, c * hd_dim:(c + 1) * hd_dim] + bs_ref[hd]
    y = _dot((u * mixed_ref[...]).astype(BF16), wout_ref[...])
    x1_ref[...] = x + gt_ref[...] * y

    @pl.when(i % TILES_PER_SEQ == TILES_PER_SEQ - 1)
    def _():
        cv_ref[...] = v[TM - CHUNK:, :]


def _mix_a_sample_kernel(x_ref, sh_ref, sc_ref, gt_ref, ng_ref, win_ref, lng_ref, lnb_ref,
                         wsd_ref, bsd_ref, wout_ref, x1_ref, cv_ref):
    x = x_ref[...]
    u, v = _mix_a_front(x, sh_ref[...], sc_ref[...], ng_ref[...], win_ref, lng_ref, lnb_ref)
    mixed = wsd_ref[...] * v + bsd_ref[...]
    y = _dot((u * mixed).astype(BF16), wout_ref[...])
    x1_ref[...] = x + gt_ref[...] * y
    cv_ref[...] = v


def _mixer_a(x_all, mod_p, mod_s, ng, layer, la, w_in, ln_g, ln_b, w_s, b_s, w_out):
    bs_b = jnp.broadcast_to(b_s[la][:, :, None], (A_HEADS, CHUNK, D // A_HEADS))
    common = [
        _layer_block((D, 2 * D), la), _layer_block((1, D), la), _layer_block((1, D), la),
    ]
    x_all, cv_p = pl.pallas_call(
        _mix_a_prompt_kernel,
        grid=(N_PROMPT_TILES,),
        in_specs=[pl.BlockSpec((TM, D), lambda i: (i, 0))]
        + _mod_specs_prompt(layer, (0, 1, 2))
        + [_full((1, D))] + common
        + [_layer_block((A_HEADS, CHUNK, CHUNK), la), _full((A_HEADS, CHUNK, D // A_HEADS)),
           _layer_block((D, D), la)],
        out_specs=[
            pl.BlockSpec((TM, D), lambda i: (i, 0)),
            pl.BlockSpec((None, CHUNK, D), lambda i: (i // TILES_PER_SEQ, 0, 0)),
        ],
        out_shape=[
            jax.ShapeDtypeStruct((T_PAD, D), F32),
            jax.ShapeDtypeStruct((BATCH, CHUNK, D), F32),
        ],
        scratch_shapes=[pltpu.VMEM((TM, D), F32)],
        input_output_aliases={0: 0},
        compiler_params=_cparams("arbitrary"),
        name="mixer_a_prompt",
    )(x_all, mod_p, mod_p, mod_p, ng, w_in, ln_g, ln_b, w_s, bs_b, w_out)

    wsd = jnp.repeat(w_s[la, :, 0, 0], D // A_HEADS).reshape(1, D)
    bsd = jnp.repeat(b_s[la, :, 0], D // A_HEADS).reshape(1, D)
    sblk = T_PROMPT // DEC_BATCH
    x_all, cv_s = pl.pallas_call(
        _mix_a_sample_kernel,
        grid=(1,),
        in_specs=[pl.BlockSpec((DEC_BATCH, D), lambda i: (sblk, 0))]
        + _mod_specs_sample(layer, (0, 1, 2), DEC_BATCH)
        + [_full((1, D))] + common
        + [_full((1, D)), _full((1, D)), _layer_block((D, D), la)],
        out_specs=[
            pl.BlockSpec((DEC_BATCH, D), lambda i: (sblk, 0)),
            pl.BlockSpec((DEC_BATCH, D), lambda i: (0, 0)),
        ],
        out_shape=[
            jax.ShapeDtypeStruct((T_PAD, D), F32),
            jax.ShapeDtypeStruct((DEC_BATCH, D), F32),
        ],
        input_output_aliases={0: 0},
        compiler_params=_cparams("arbitrary"),
        name="mixer_a_sample",
    )(x_all, mod_s, mod_s, mod_s, ng, w_in, ln_g, ln_b, wsd, bsd, w_out)
    return x_all, cv_p, cv_s.reshape(DEC_BATCH, 1, D)


def _mix_b_tail(pooled_groups, wgrp_ref, bgrp_ref, scale_ref, wout_ref):
    outs = [
        _dot(pg.astype(BF16), wgrp_ref[g]) + bgrp_ref[g]
        for g, pg in enumerate(pooled_groups)
    ]
    mixed = jnp.concatenate(outs, axis=1) * scale_ref[...]
    return _dot(mixed.astype(BF16), wout_ref[...])


def _mix_b_prompt_kernel(x_ref, sh_ref, sc_ref, gt_ref, ng_ref, win_ref, wgrp_ref, bgrp_ref,
                         scale_ref, wout_ref, x1_ref, st_ref, full_ref):
    i = pl.program_id(0)
    j = i % TILES_PER_SEQ
    x = x_ref[...]
    h = (_rms(x, ng_ref[...]) * (1.0 + sc_ref[...]) + sh_ref[...]).astype(BF16)
    p = _dot(h, win_ref[...])

    @pl.when(j == 0)
    def _():
        full_ref[0:POOL_CARRY, :] = jnp.zeros((POOL_CARRY, D), F32)

    full_ref[POOL_CARRY:POOL_CARRY + TM, :] = p
    pos = j * TM + lax.broadcasted_iota(jnp.int32, (TM, 1), 0)
    pooled = []
    for g, w in enumerate(POOL_WINDOWS):
        cols = slice(g * B_GROUP_DIM, (g + 1) * B_GROUP_DIM)
        s = p[:, cols]
        for k in range(1, w):
            s = s + full_ref[pl.ds(POOL_CARRY - k, TM), cols]
        cnt = jnp.minimum(w, pos + 1).astype(F32)
        pooled.append(s / cnt - p[:, cols])
    y = _mix_b_tail(pooled, wgrp_ref, bgrp_ref, scale_ref, wout_ref)
    x1_ref[...] = x + gt_ref[...] * y

    @pl.when(j == TILES_PER_SEQ - 1)
    def _():
        st_ref[...] = full_ref[pl.ds(POOL_CARRY + TM - POOL_BUF, POOL_BUF), :]

    full_ref[0:POOL_CARRY, :] = full_ref[pl.ds(TM, POOL_CARRY), :]


def _mix_b_sample_kernel(x_ref, sh_ref, sc_ref, gt_ref, ng_ref, st_ref, win_ref, wgrp_ref,
                         bgrp_ref, scale_ref, wout_ref, x1_ref, nst_ref):
    x = x_ref[...]
    h = (_rms(x, ng_ref[...]) * (1.0 + sc_ref[...]) + sh_ref[...]).astype(BF16)
    p = _dot(h, win_ref[...])
    pooled = []
    for g, w in enumerate(POOL_WINDOWS):
        s = p[:, g * B_GROUP_DIM:(g + 1) * B_GROUP_DIM]
        for k in range(1, w):
            c0 = (POOL_BUF - k) * D + g * B_GROUP_DIM
            s = s + st_ref[:, c0:c0 + B_GROUP_DIM]
        cnt = float(min(w, PAST_LEN + 1))
        pooled.append(s / cnt - p[:, g * B_GROUP_DIM:(g + 1) * B_GROUP_DIM])
    y = _mix_b_tail(pooled, wgrp_ref, bgrp_ref, scale_ref, wout_ref)
    x1_ref[...] = x + gt_ref[...] * y
    nst_ref[:, 0:(POOL_BUF - 1) * D] = st_ref[:, D:POOL_BUF * D]
    nst_ref[:, (POOL_BUF - 1) * D:POOL_BUF * D] = p


def _mixer_b(x_all, mod_p, mod_s, ng, layer, lb, state, w_in, w_grp, b_grp, scale, w_out):
    gd = B_GROUP_DIM
    common = [
        _layer_block((D, D), lb), _layer_block((len(POOL_WINDOWS), gd, gd), lb),
        _layer_block((len(POOL_WINDOWS), 1, gd), lb), _layer_block((1, D), lb),
        _layer_block((D, D), lb),
    ]
    x_all, st_p = pl.pallas_call(
        _mix_b_prompt_kernel,
        grid=(N_PROMPT_TILES,),
        in_specs=[pl.BlockSpec((TM, D), lambda i: (i, 0))]
        + _mod_specs_prompt(layer, (0, 1, 2)) + [_full((1, D))] + common,
        out_specs=[
            pl.BlockSpec((TM, D), lambda i: (i, 0)),
            pl.BlockSpec((None, POOL_BUF, D), lambda i: (i // TILES_PER_SEQ, 0, 0)),
        ],
        out_shape=[
            jax.ShapeDtypeStruct((T_PAD, D), F32),
            jax.ShapeDtypeStruct((BATCH, POOL_BUF, D), F32),
        ],
        scratch_shapes=[pltpu.VMEM((TM + POOL_CARRY, D), F32)],
        input_output_aliases={0: 0},
        compiler_params=_cparams("arbitrary"),
        name="mixer_b_prompt",
    )(x_all, mod_p, mod_p, mod_p, ng, w_in, w_grp, b_grp, scale, w_out)

    r = SAMPLE_ROWS
    sblk = T_PROMPT // r
    st2 = state[lb].reshape(DEC_BATCH, POOL_BUF * D)
    x_all, st_s = pl.pallas_call(
        _mix_b_sample_kernel,
        grid=(DEC_BATCH // r,),
        in_specs=[pl.BlockSpec((r, D), lambda i: (sblk + i, 0))]
        + _mod_specs_sample(layer, (0, 1, 2), r) + [_full((1, D))]
        + [pl.BlockSpec((r, POOL_BUF * D), lambda i: (i, 0))] + common,
        out_specs=[
            pl.BlockSpec((r, D), lambda i: (sblk + i, 0)),
            pl.BlockSpec((r, POOL_BUF * D), lambda i: (i, 0)),
        ],
        out_shape=[
            jax.ShapeDtypeStruct((T_PAD, D), F32),
            jax.ShapeDtypeStruct((DEC_BATCH, POOL_BUF * D), F32),
        ],
        input_output_aliases={0: 0},
        compiler_params=_cparams("arbitrary"),
        name="mixer_b_sample",
    )(x_all, mod_s, mod_s, mod_s, ng, st2, w_in, w_grp, b_grp, scale, w_out)
    return x_all, st_p, st_s.reshape(DEC_BATCH, POOL_BUF, D)


def _mix_c_glu(x, sh, sc, ng, win_ref, bin_ref):
    h = (_rms(x, ng) * (1.0 + sc) + sh).astype(BF16)
    ag = _dot(h, win_ref[...]) + bin_ref[...]
    return ag[:, :D] * jax.nn.sigmoid(ag[:, D:])


def _mix_c_tail(conv, lng_ref, lnb_ref, wout_ref, bout_ref):
    z = _ln(conv, lng_ref[...], lnb_ref[...])
    z = z * jax.nn.sigmoid(z)
    return _dot(z.astype(BF16), wout_ref[...]) + bout_ref[...]


def _mix_c_prompt_kernel(x_ref, sh_ref, sc_ref, gt_ref, ng_ref, win_ref, bin_ref, wdw_ref,
                         bdw_ref, lng_ref, lnb_ref, wout_ref, bout_ref, x1_ref, st_ref,
                         full_ref, conv_ref, *shift_refs):
    i = pl.program_id(0)
    j = i % TILES_PER_SEQ
    x = x_ref[...]
    glu = _mix_c_glu(x, sh_ref[...], sc_ref[...], ng_ref[...], win_ref, bin_ref)

    @pl.when(j == 0)
    def _():
        full_ref[0:CONV_CARRY, :] = jnp.zeros((CONV_CARRY, D), F32)

    full_ref[CONV_CARRY:CONV_CARRY + TM, :] = glu
    off = CONV_CARRY - CONV_BUF
    span = TM + (off + CONV_BUF) // SUBLANES * SUBLANES - SUBLANES
    for c in range(D // LANES):
        cols = slice(c * LANES, (c + 1) * LANES)
        shift_ref = shift_refs[c % len(shift_refs)]
        for r in range(1, SUBLANES):
            shift_ref[r - 1] = full_ref[pl.ds(r, span), cols]
        for rb in range(TM // CONV_ROWS):
            acc = None
            for k in range(CONV_WIDTH):
                q, r = divmod(off + k, SUBLANES)
                start = q * SUBLANES + rb * CONV_ROWS
                if r == 0:
                    src = full_ref[pl.ds(start, CONV_ROWS), cols]
                else:
                    src = shift_ref[r - 1, pl.ds(start, CONV_ROWS), :]
                term = src * wdw_ref[k:k + 1, cols]
                acc = term if acc is None else acc + term
            conv_ref[pl.ds(rb * CONV_ROWS, CONV_ROWS), cols] = acc + bdw_ref[:, cols]
    y = _mix_c_tail(conv_ref[...], lng_ref, lnb_ref, wout_ref, bout_ref)
    x1_ref[...] = x + gt_ref[...] * y

    @pl.when(j == TILES_PER_SEQ - 1)
    def _():
        st_ref[...] = full_ref[pl.ds(CONV_CARRY + TM - CONV_BUF, CONV_BUF), :]

    full_ref[0:CONV_CARRY, :] = full_ref[pl.ds(TM, CONV_CARRY), :]


def _mix_c_sample_kernel(x_ref, sh_ref, sc_ref, gt_ref, ng_ref, st_ref, win_ref, bin_ref,
                         wdw_ref, bdw_ref, lng_ref, lnb_ref, wout_ref, bout_ref, x1_ref, nst_ref):
    x = x_ref[...]
    glu = _mix_c_glu(x, sh_ref[...], sc_ref[...], ng_ref[...], win_ref, bin_ref)
    acc = glu * wdw_ref[CONV_BUF:CONV_BUF + 1, :]
    for k in range(CONV_BUF):
        acc = acc + st_ref[:, k * D:(k + 1) * D] * wdw_ref[k:k + 1, :]
    y = _mix_c_tail(acc + bdw_ref[...], lng_ref, lnb_ref, wout_ref, bout_ref)
    x1_ref[...] = x + gt_ref[...] * y
    nst_ref[:, 0:(CONV_BUF - 1) * D] = st_ref[:, D:CONV_BUF * D]
    nst_ref[:, (CONV_BUF - 1) * D:CONV_BUF * D] = glu


def _mixer_c(x_all, mod_p, mod_s, ng, layer, lc, state, w_in, b_in, w_dw, b_dw, ln_g, ln_b,
             w_out, b_out):
    common = [
        _layer_block((D, 2 * D), lc), _layer_block((1, 2 * D), lc),
        _layer_block((CONV_WIDTH, D), lc), _layer_block((1, D), lc), _layer_block((1, D), lc),
        _layer_block((1, D), lc), _layer_block((D, D), lc), _layer_block((1, D), lc),
    ]
    x_all, st_p = pl.pallas_call(
        _mix_c_prompt_kernel,
        grid=(N_PROMPT_TILES,),
        in_specs=[pl.BlockSpec((TM, D), lambda i: (i, 0))]
        + _mod_specs_prompt(layer, (0, 1, 2)) + [_full((1, D))] + common,
        out_specs=[
            pl.BlockSpec((TM, D), lambda i: (i, 0)),
            pl.BlockSpec((None, CONV_BUF, D), lambda i: (i // TILES_PER_SEQ, 0, 0)),
        ],
        out_shape=[
            jax.ShapeDtypeStruct((T_PAD, D), F32),
            jax.ShapeDtypeStruct((BATCH, CONV_BUF, D), F32),
        ],
        scratch_shapes=[pltpu.VMEM((TM + CONV_CARRY, D), F32), pltpu.VMEM((TM, D), F32)]
        + [pltpu.VMEM((SUBLANES - 1, TM + CONV_CARRY - SUBLANES, LANES), F32)] * 2,
        input_output_aliases={0: 0},
        compiler_params=_cparams("arbitrary"),
        name="mixer_c_prompt",
    )(x_all, mod_p, mod_p, mod_p, ng, w_in, b_in, w_dw, b_dw, ln_g, ln_b, w_out, b_out)

    r = SAMPLE_ROWS
    sblk = T_PROMPT // r
    st2 = state[lc].reshape(DEC_BATCH, CONV_BUF * D)
    x_all, st_s = pl.pallas_call(
        _mix_c_sample_kernel,
        grid=(DEC_BATCH // r,),
        in_specs=[pl.BlockSpec((r, D), lambda i: (sblk + i, 0))]
        + _mod_specs_sample(layer, (0, 1, 2), r) + [_full((1, D))]
        + [pl.BlockSpec((r, CONV_BUF * D), lambda i: (i, 0))] + common,
        out_specs=[
            pl.BlockSpec((r, D), lambda i: (sblk + i, 0)),
            pl.BlockSpec((r, CONV_BUF * D), lambda i: (i, 0)),
        ],
        out_shape=[
            jax.ShapeDtypeStruct((T_PAD, D), F32),
            jax.ShapeDtypeStruct((DEC_BATCH, CONV_BUF * D), F32),
        ],
        input_output_aliases={0: 0},
        compiler_params=_cparams("arbitrary"),
        name="mixer_c_sample",
    )(x_all, mod_s, mod_s, mod_s, ng, st2, w_in, b_in, w_dw, b_dw, ln_g, ln_b, w_out, b_out)
    return x_all, st_p, st_s.reshape(DEC_BATCH, CONV_BUF, D)


def _tile_mod(i, p_ref, s_ref):
    s_rows = jnp.concatenate([s_ref[...], jnp.zeros((TM - DEC_BATCH, D), F32)], axis=0)
    return jnp.where(i >= N_PROMPT_TILES, s_rows, p_ref[...])


def _unified_mod_specs(layer, chunks):
    specs = []
    for c in chunks:
        specs.append(pl.BlockSpec(
            (None, None, 1, D),
            lambda i, c=c: (layer, jnp.minimum(i // TILES_PER_SEQ, BATCH - 1), 0, c)))
        specs.append(pl.BlockSpec((None, DEC_BATCH, D), lambda i, c=c: (layer, 0, c)))
    return specs


def _router_kernel(x_ref, shp_ref, shs_ref, scp_ref, scs_ref, ng_ref, wr_ref, br_ref,
                   h2_ref, meta_ref, wts_ref, cnt_ref, carry_ref):
    i = pl.program_id(0)

    @pl.when(i == 0)
    def _():
        carry_ref[...] = jnp.zeros((1, LANES), F32)

    sh = _tile_mod(i, shp_ref, shs_ref)
    sc = _tile_mod(i, scp_ref, scs_ref)
    h2 = _rms(x_ref[...], ng_ref[...]) * (1.0 + sc) + sh
    h2_ref[...] = _pack_bf16_pairs(h2)
    lg = _dot(h2.astype(BF16), wr_ref[...]) + br_ref[...]
    lane = lax.broadcasted_iota(jnp.int32, (TM, LANES), 1)
    neg = -jnp.inf
    is_g = lane < N_GROUPS
    gm = jnp.where(is_g, lg, neg)
    gmax = jnp.max(gm, axis=1, keepdims=True)
    lane_f = lane.astype(F32)
    far = float(LANES)
    gsel = jnp.min(jnp.where(gm == gmax, lane_f, far), axis=1, keepdims=True).astype(jnp.int32)
    gsum = jnp.sum(jnp.where(is_g, jnp.exp(lg - gmax), 0.0), axis=1, keepdims=True)
    g_w = 1.0 / gsum
    e_lane = lane - EXPERT_LANE0
    in_grp = (e_lane >= 0) & (e_lane < N_EXPERTS) & (
        lax.shift_right_arithmetic(e_lane, EXPERTS_PER_GROUP.bit_length() - 1) == gsel)
    em = jnp.where(in_grp, lg, neg)
    m1 = jnp.max(em, axis=1, keepdims=True)
    i1 = jnp.min(jnp.where(em == m1, lane_f, far), axis=1, keepdims=True).astype(jnp.int32)
    em2 = jnp.where(lane == i1, neg, em)
    m2 = jnp.max(em2, axis=1, keepdims=True)
    i2 = jnp.min(jnp.where(em2 == m2, lane_f, far), axis=1, keepdims=True).astype(jnp.int32)
    e2 = jnp.exp(m2 - m1)
    den = 1.0 + e2
    w1 = (1.0 / den) * g_w
    w2 = (e2 / den) * g_w

    hit1 = lane == i1
    hit2 = lane == i2
    assign = jnp.where(hit1 | hit2, 1.0, 0.0).astype(BF16)
    r = lax.broadcasted_iota(jnp.int32, (TM, TM), 0)
    c = lax.broadcasted_iota(jnp.int32, (TM, TM), 1)
    before = (c < r).astype(BF16)
    seen = _dot(before, assign) + carry_ref[...]
    rank1 = jnp.sum(jnp.where(hit1, seen, 0.0), axis=1, keepdims=True)
    rank2 = jnp.sum(jnp.where(hit2, seen, 0.0), axis=1, keepdims=True)
    carry_ref[...] = carry_ref[...] + jnp.sum(assign.astype(F32), axis=0, keepdims=True)
    cnt_ref[...] = carry_ref[...]

    ex1 = (i1 - EXPERT_LANE0).astype(F32)
    ex2 = (i2 - EXPERT_LANE0).astype(F32)
    fields = jnp.where(lane == 0, ex1, jnp.where(lane == 1, ex2,
                       jnp.where(lane == 2, rank1, jnp.where(lane == 3, rank2, 0.0))))
    meta_ref[...] = fields.T[0:META_ROWS, :]
    wts_ref[...] = jnp.where(lane == 0, w1, jnp.where(lane == 1, w2, 0.0))


def _router(x_all, mod_p, mod_s, ng, layer, w_r, b_r):
    return pl.pallas_call(
        _router_kernel,
        grid=(N_TILES,),
        in_specs=[pl.BlockSpec((TM, D), lambda i: (i, 0))]
        + _unified_mod_specs(layer, (3, 4))
        + [_full((1, D)), _full((D, LANES)), _full((1, LANES))],
        out_specs=[
            pl.BlockSpec((TM, D // 2), lambda i: (i, 0)),
            pl.BlockSpec((META_ROWS, TM), lambda i: (0, i)),
            pl.BlockSpec((TM, LANES), lambda i: (i, 0)),
            pl.BlockSpec((1, LANES), lambda i: (0, 0)),
        ],
        out_shape=[
            jax.ShapeDtypeStruct((T_PAD, D // 2), jnp.int32),
            jax.ShapeDtypeStruct((META_ROWS, T_PAD), F32),
            jax.ShapeDtypeStruct((T_PAD, LANES), F32),
            jax.ShapeDtypeStruct((1, LANES), F32),
        ],
        scratch_shapes=[pltpu.VMEM((1, LANES), F32)],
        compiler_params=_cparams("arbitrary"),
        name="router",
    )(x_all, mod_p, mod_s, mod_p, mod_s, ng, w_r, b_r)


SC_INDEX_MAX = 128
SC_ROWS_BYTES = 256 * 1024


def _gather_chunk(per_worker, row_bytes):
    top = min(SC_INDEX_MAX, SC_ROWS_BYTES // row_bytes) // SUBLANES * SUBLANES
    for ch in range(top, SUBLANES - 1, -SUBLANES):
        if per_worker % ch == 0:
            return ch
    raise ValueError(per_worker)


def _sc_gather(table, idx):
    nc, nw = SC_CORES, SC_CORES * SC_SUBCORES
    m = idx.shape[0]
    per_w = m // nw
    assert per_w * nw == m
    width = table.shape[1]
    ch = _gather_chunk(per_w, width * table.dtype.itemsize)
    mesh = plsc.VectorSubcoreMesh(core_axis_name="c", subcore_axis_name="s")

    @functools.partial(
        pl.kernel,
        out_type=jax.ShapeDtypeStruct((m, width), table.dtype),
        mesh=mesh,
        scratch_types=[
            pltpu.VMEM((ch,), jnp.int32),
            pltpu.VMEM((ch, width), table.dtype),
            pltpu.SemaphoreType.DMA,
        ],
    )
    def gather_kernel(t_hbm, i_hbm, o_hbm, idx_v, rows_v, sem):
        wid = lax.axis_index("s") * nc + lax.axis_index("c")
        base = wid * per_w

        @pl.loop(0, per_w // ch)
        def _(j):
            off = pl.multiple_of(base + j * ch, 8)
            pltpu.sync_copy(i_hbm.at[pl.ds(off, ch)], idx_v)
            pltpu.async_copy(t_hbm.at[idx_v], rows_v, sem).wait()
            pltpu.sync_copy(rows_v, o_hbm.at[pl.ds(off, ch)])

    return gather_kernel(table, idx)


def _sc_scatter2(rows, dest0, dest1, n_out):
    nc, nw = SC_CORES, SC_CORES * SC_SUBCORES
    n, width = rows.shape
    per_w = n // nw
    assert per_w * nw == n
    ch = _gather_chunk(per_w, width * rows.dtype.itemsize)
    mesh = plsc.VectorSubcoreMesh(core_axis_name="c", subcore_axis_name="s")

    @functools.partial(
        pl.kernel,
        out_type=jax.ShapeDtypeStruct((n_out, width), rows.dtype),
        mesh=mesh,
        scratch_types=[
            pltpu.VMEM((ch,), jnp.int32),
            pltpu.VMEM((ch,), jnp.int32),
            pltpu.VMEM((ch, width), rows.dtype),
            pltpu.SemaphoreType.DMA,
            pltpu.SemaphoreType.DMA,
        ],
    )
    def scatter_kernel(r_hbm, d0_hbm, d1_hbm, o_hbm, i0_v, i1_v, rows_v, sem0, sem1):
        wid = lax.axis_index("s") * nc + lax.axis_index("c")
        base = wid * per_w

        @pl.loop(0, per_w // ch)
        def _(j):
            off = pl.multiple_of(base + j * ch, 8)
            pltpu.sync_copy(d0_hbm.at[pl.ds(off, ch)], i0_v)
            pltpu.sync_copy(d1_hbm.at[pl.ds(off, ch)], i1_v)
            pltpu.sync_copy(r_hbm.at[pl.ds(off, ch)], rows_v)
            c0 = pltpu.async_copy(rows_v, o_hbm.at[i0_v], sem0)
            c1 = pltpu.async_copy(rows_v, o_hbm.at[i1_v], sem1)
            c0.wait()
            c1.wait()

    return scatter_kernel(rows, dest0, dest1)


def _expert_kernel(be_ref, first_ref, nvalid_ref, nused_ref, x_ref, wg_ref, wu_ref, wd_ref,
                   y_ref, wgb_ref, wub_ref, wdb_ref):
    b = pl.program_id(0)

    @pl.when(b < nused_ref[0])
    def _():
        @pl.when(first_ref[b] == 1)
        def _():
            wgb_ref[...] = wg_ref[...].astype(BF16)
            wub_ref[...] = wu_ref[...].astype(BF16)
            wdb_ref[...] = wd_ref[...].astype(BF16)

        live = lax.broadcasted_iota(jnp.int32, (BM, 1), 0) < nvalid_ref[b]
        x = _unpack_bf16_pairs(jnp.where(live, x_ref[...], 0), BF16)
        g = _dot(x, wgb_ref[...])
        u = _dot(x, wub_ref[...])
        hmid = (g * jax.nn.sigmoid(g)) * u
        y_ref[...] = _pack_bf16_pairs(_dot(hmid.astype(BF16), wdb_ref[...]))


def _experts(x_sorted, block_expert, first, nvalid, nused, layer, w_gate, w_up, w_down):
    def row_map(b, be, fi, nv, nu):
        return (jnp.minimum(b, nu[0] - 1), 0)

    def w_map(b, be, fi, nv, nu):
        return (layer, be[b], 0, 0)

    grid_spec = pltpu.PrefetchScalarGridSpec(
        num_scalar_prefetch=4,
        grid=(N_BLOCKS,),
        in_specs=[
            pl.BlockSpec((BM, D // 2), row_map),
            pl.BlockSpec((None, None, D, D_EXPERT), w_map),
            pl.BlockSpec((None, None, D, D_EXPERT), w_map),
            pl.BlockSpec((None, None, D_EXPERT, D), w_map),
        ],
        out_specs=pl.BlockSpec((BM, D // 2), row_map),
        scratch_shapes=[
            pltpu.VMEM((D, D_EXPERT), BF16),
            pltpu.VMEM((D, D_EXPERT), BF16),
            pltpu.VMEM((D_EXPERT, D), BF16),
        ],
    )
    return pl.pallas_call(
        _expert_kernel,
        grid_spec=grid_spec,
        out_shape=jax.ShapeDtypeStruct((P_ROWS, D // 2), jnp.int32),
        compiler_params=_cparams("arbitrary"),
        name="experts",
    )(block_expert, first, nvalid, nused, x_sorted, w_gate, w_up, w_down)


def _combine_kernel(x_ref, g0_ref, g1_ref, wts_ref, gtp_ref, gts_ref, o_ref):
    i = pl.program_id(0)
    gt = _tile_mod(i, gtp_ref, gts_ref)
    w = wts_ref[...]
    y0 = _unpack_bf16_pairs(g0_ref[...], F32)
    y1 = _unpack_bf16_pairs(g1_ref[...], F32)
    o_ref[...] = x_ref[...] + gt * (w[:, 0:1] * y0 + w[:, 1:2] * y1)


def _combine(x_all, gathered, wts, mod_p, mod_s, layer):
    return pl.pallas_call(
        _combine_kernel,
        grid=(N_TILES,),
        in_specs=[
            pl.BlockSpec((TM, D), lambda i: (i, 0)),
            pl.BlockSpec((TM, D // 2), lambda i: (i, 0)),
            pl.BlockSpec((TM, D // 2), lambda i: (i + N_TILES, 0)),
            pl.BlockSpec((TM, LANES), lambda i: (i, 0)),
        ] + _unified_mod_specs(layer, (5,)),
        out_specs=pl.BlockSpec((TM, D), lambda i: (i, 0)),
        out_shape=jax.ShapeDtypeStruct((T_PAD, D), F32),
        input_output_aliases={0: 0},
        compiler_params=_cparams("arbitrary"),
        name="combine",
    )(x_all, gathered, gathered, wts, mod_p, mod_s)


def _moe(x_all, mod_p, mod_s, ng, layer, w_r, b_r, w_gate, w_up, w_down):
    h2, meta, wts, cnt = _router(x_all, mod_p, mod_s, ng, layer, w_r, b_r)
    counts = cnt[0, EXPERT_LANE0:EXPERT_LANE0 + N_EXPERTS].astype(jnp.int32)
    padded = (counts + BM - 1) // BM * BM
    pend = jnp.cumsum(padded)
    pstart = pend - padded
    expert = meta[0:2].astype(jnp.int32)
    dest = meta[2:4].astype(jnp.int32)
    for e in range(N_EXPERTS):
        dest = dest + jnp.where(expert == e, pstart[e], 0)
    block0 = jnp.arange(N_BLOCKS, dtype=jnp.int32) * BM
    block_expert = jnp.minimum(
        jnp.sum((pend[None, :] <= block0[:, None]).astype(jnp.int32), axis=1), N_EXPERTS - 1)
    of_block = block_expert[:, None] == jnp.arange(N_EXPERTS, dtype=jnp.int32)[None, :]
    seg_end = jnp.sum(jnp.where(of_block, (pstart + counts)[None, :], 0), axis=1)
    nvalid = jnp.clip(seg_end - block0, 0, BM).astype(jnp.int32)
    nused = (pend[-1] // BM).astype(jnp.int32).reshape(1)
    first = jnp.concatenate(
        [jnp.ones((1,), jnp.int32), (block_expert[1:] != block_expert[:-1]).astype(jnp.int32)])

    x_sorted = _sc_scatter2(h2, dest[0], dest[1], P_ROWS)
    y_sorted = _experts(x_sorted, block_expert, first, nvalid, nused, layer, w_gate, w_up, w_down)
    gathered = _sc_gather(y_sorted, dest.reshape(-1))
    return _combine(x_all, gathered, wts, mod_p, mod_s, layer)


def _final_kernel(x_ref, g_ref, yp_ref, ys_ref):
    i = pl.program_id(0)
    y = _rms(x_ref[...], g_ref[...])

    @pl.when(i < N_PROMPT_TILES)
    def _():
        yp_ref[...] = y

    @pl.when(i == N_PROMPT_TILES)
    def _():
        ys_ref[...] = y[0:DEC_BATCH, :]


def _final_norm(x_all, g):
    return pl.pallas_call(
        _final_kernel,
        grid=(N_PROMPT_TILES + 1,),
        in_specs=[pl.BlockSpec((TM, D), lambda i: (i, 0)), _full((1, D))],
        out_specs=[
            pl.BlockSpec((TM, D), lambda i: (jnp.minimum(i, N_PROMPT_TILES - 1), 0)),
            pl.BlockSpec((DEC_BATCH, D), lambda i: (0, 0)),
        ],
        out_shape=[
            jax.ShapeDtypeStruct((T_PROMPT, D), F32),
            jax.ShapeDtypeStruct((DEC_BATCH, D), F32),
        ],
        compiler_params=_cparams("arbitrary"),
        name="final_norm",
    )(x_all, g)


def kernel(x_prompt, x_sample, c_prompt, c_sample, state_pool, state_conv, ada_w, ada_b, norm_g, final_g, a_w_in, a_ln_g, a_ln_b, a_w_s, a_b_s, a_w_out, b_w_in, b_w_grp, b_b_grp, b_scale, b_w_out, c_w_in, c_b_in, c_w_dw, c_b_dw, c_ln_g, c_ln_b, c_w_out, c_b_out, moe_w_grp, moe_b_grp, moe_w_exp, moe_b_exp, moe_w_gate, moe_w_up, moe_w_down):
    x_all = jnp.concatenate([
        x_prompt.reshape(T_PROMPT, D), x_sample.reshape(DEC_BATCH, D),
        jnp.zeros((T_PAD - T_ALL, D), F32)], axis=0)
    mods = _ada_mods(jnp.concatenate([c_prompt, c_sample], axis=0), ada_w, ada_b)
    mod_p = mods[:, :BATCH].reshape(DEPTH, BATCH, 1, 6 * D)
    mod_s = mods[:, BATCH:]

    a_w_in_b, a_w_out_b = a_w_in.astype(BF16), a_w_out.astype(BF16)
    b_w_in_b, b_w_grp_b, b_w_out_b = b_w_in.astype(BF16), b_w_grp.astype(BF16), b_w_out.astype(BF16)
    c_w_in_b, c_w_out_b = c_w_in.astype(BF16), c_w_out.astype(BF16)
    w_r = jnp.pad(jnp.concatenate([moe_w_grp, moe_w_exp], axis=2),
                  ((0, 0), (0, 0), (0, LANES - N_GROUPS - N_EXPERTS))).astype(BF16)
    b_r = jnp.pad(jnp.concatenate([moe_b_grp, moe_b_exp], axis=1),
                  ((0, 0), (0, LANES - N_GROUPS - N_EXPERTS)))

    new_a_p, new_a_s, new_b_p, new_b_s, new_c_p, new_c_s = [], [], [], [], [], []
    ia = ib = ic = 0
    for layer in range(DEPTH):
        ng1 = norm_g[layer, 0].reshape(1, D)
        ng2 = norm_g[layer, 1].reshape(1, D)
        kind = layer % 3
        if kind == 0:
            x_all, st_p, st_s = _mixer_a(
                x_all, mod_p, mod_s, ng1, layer, ia, a_w_in_b, a_ln_g.reshape(-1, 1, D),
                a_ln_b.reshape(-1, 1, D), a_w_s, a_b_s, a_w_out_b)
            new_a_p.append(st_p)
            new_a_s.append(st_s)
            ia += 1
        elif kind == 1:
            x_all, st_p, st_s = _mixer_b(
                x_all, mod_p, mod_s, ng1, layer, ib, state_pool, b_w_in_b, b_w_grp_b,
                b_b_grp.reshape(-1, len(POOL_WINDOWS), 1, B_GROUP_DIM),
                b_scale.reshape(-1, 1, D), b_w_out_b)
            new_b_p.append(st_p)
            new_b_s.append(st_s)
            ib += 1
        else:
            x_all, st_p, st_s = _mixer_c(
                x_all, mod_p, mod_s, ng1, layer, ic, state_conv, c_w_in_b,
                c_b_in.reshape(-1, 1, 2 * D), c_w_dw, c_b_dw.reshape(-1, 1, D),
                c_ln_g.reshape(-1, 1, D), c_ln_b.reshape(-1, 1, D), c_w_out_b,
                c_b_out.reshape(-1, 1, D))
            new_c_p.append(st_p)
            new_c_s.append(st_s)
            ic += 1
        x_all = _moe(x_all, mod_p, mod_s, ng2, layer, w_r[layer], b_r[layer].reshape(1, LANES),
                     moe_w_gate, moe_w_up, moe_w_down)

    y_p, y_s = _final_norm(x_all, final_g.reshape(1, D))
    return (y_p.reshape(BATCH, SEQ, D), y_s.reshape(DEC_BATCH, 1, D),
            jnp.stack(new_a_p), jnp.stack(new_a_s), jnp.stack(new_b_p), jnp.stack(new_b_s),
            jnp.stack(new_c_p), jnp.stack(new_c_s))
```

```python
import functools

import jax
import jax.numpy as jnp
from jax import lax
from jax.experimental import pallas as pl
from jax.experimental.pallas import tpu as pltpu
from jax.experimental.pallas import tpu_sc as plsc

F32 = jnp.float32
BF16 = jnp.bfloat16

D = 1024
BATCH = 8
SEQ = 2048
DEPTH = 4
DEC_BATCH = 128
PAST_LEN = 16384
CHUNK = 128
A_HEADS = 8
POOL_WINDOWS = (2, 4, 8, 16)
B_GROUP_DIM = D // len(POOL_WINDOWS)
POOL_BUF = max(POOL_WINDOWS) - 1
CONV_WIDTH = 31
CONV_BUF = CONV_WIDTH - 1
N_GROUPS = 4
EXPERTS_PER_GROUP = 8
N_EXPERTS = N_GROUPS * EXPERTS_PER_GROUP
D_EXPERT = D // 2
EPS = 1e-6

LANES = 128
SUBLANES = 8
CONV_ROWS = 128
SC_CORES = 2
SC_SUBCORES = 16
TM = 512
TILES_PER_SEQ = SEQ // TM
T_PROMPT = BATCH * SEQ
T_ALL = T_PROMPT + DEC_BATCH
N_TILES = -(-T_ALL // TM)
T_PAD = N_TILES * TM
N_PROMPT_TILES = T_PROMPT // TM
SAMPLE_ROWS = 32
BM = 512
N_BLOCKS = -(-(2 * T_PAD) // BM) + N_EXPERTS
P_ROWS = N_BLOCKS * BM
META_ROWS = 8
POOL_CARRY = 16
CONV_CARRY = 32
EXPERT_LANE0 = N_GROUPS
VMEM_LIMIT = 56 * 1024 * 1024


def _cparams(*sem):
    return pltpu.CompilerParams(dimension_semantics=sem, vmem_limit_bytes=VMEM_LIMIT)


def _rms(x, g):
    return x * lax.rsqrt(jnp.mean(x * x, axis=-1, keepdims=True) + EPS) * g


def _ln(x, g, b):
    mu = jnp.mean(x, axis=-1, keepdims=True)
    xc = x - mu
    var = jnp.mean(xc * xc, axis=-1, keepdims=True)
    return xc * lax.rsqrt(var + EPS) * g + b


def _dot(a, b):
    return jnp.dot(a, b, preferred_element_type=F32)


def _pack_bf16_pairs(v):
    w = v.shape[1] // 2
    lo = lax.bitcast_convert_type(v[:, :w].astype(BF16).astype(F32), jnp.uint32)
    hi = lax.bitcast_convert_type(v[:, w:].astype(BF16).astype(F32), jnp.uint32)
    return lax.bitcast_convert_type(lax.shift_right_logical(lo, jnp.uint32(16)) | hi, jnp.int32)


def _unpack_bf16_pairs(words, dtype):
    u = lax.bitcast_convert_type(words, jnp.uint32)
    lo = lax.bitcast_convert_type(lax.shift_left(u, jnp.uint32(16)), F32)
    hi = lax.bitcast_convert_type(u & jnp.uint32(0xFFFF0000), F32)
    return jnp.concatenate([lo.astype(dtype), hi.astype(dtype)], axis=1)


def _ada_kernel(c_ref, w_ref, b_ref, o_ref):
    c = c_ref[...]
    s = (c * jax.nn.sigmoid(c)).astype(BF16)
    o_ref[...] = _dot(s, w_ref[...].astype(BF16)) + b_ref[...]


def _ada_mods(c_all, ada_w, ada_b):
    n = c_all.shape[0]
    tn = 2048
    return pl.pallas_call(
        _ada_kernel,
        grid=(DEPTH, 6 * D // tn),
        in_specs=[
            pl.BlockSpec((n, D), lambda l, j: (0, 0)),
            pl.BlockSpec((None, D, tn), lambda l, j: (l, 0, j)),
            pl.BlockSpec((None, 1, tn), lambda l, j: (l, 0, j)),
        ],
        out_specs=pl.BlockSpec((None, n, tn), lambda l, j: (l, 0, j)),
        out_shape=jax.ShapeDtypeStruct((DEPTH, n, 6 * D), F32),
        compiler_params=_cparams("parallel", "parallel"),
        name="ada_mods",
    )(c_all, ada_w, ada_b.reshape(DEPTH, 1, 6 * D))


def _mod_specs_prompt(layer, chunks):
    return [
        pl.BlockSpec((None, None, 1, D), lambda i, c=c: (layer, i // TILES_PER_SEQ, 0, c))
        for c in chunks
    ]


def _mod_specs_sample(layer, chunks, rows, row_block0=0):
    return [
        pl.BlockSpec((None, rows, D), lambda i, c=c: (layer, i + row_block0, c)) for c in chunks
    ]


def _full(shape):
    nd = len(shape)
    return pl.BlockSpec(shape, lambda i: (0,) * nd)


def _layer_block(shape, layer):
    nd = len(shape)
    return pl.BlockSpec((None,) + shape, lambda i: (layer,) + (0,) * nd)


def _mix_a_front(x, sh, sc, ng, win_ref, lng_ref, lnb_ref):
    h = (_rms(x, ng) * (1.0 + sc) + sh).astype(BF16)
    z = jax.nn.gelu(_dot(h, win_ref[...]))
    u = z[:, :D]
    v = _ln(z[:, D:], lng_ref[...], lnb_ref[...])
    return u, v


def _mix_a_prompt_kernel(x_ref, sh_ref, sc_ref, gt_ref, ng_ref, win_ref, lng_ref, lnb_ref,
                         ws_ref, bs_ref, wout_ref, x1_ref, cv_ref, mixed_ref):
    i = pl.program_id(0)
    x = x_ref[...]
    u, v = _mix_a_front(x, sh_ref[...], sc_ref[...], ng_ref[...], win_ref, lng_ref, lnb_ref)
    vb = v.astype(BF16)
    n_chunks = TM // CHUNK
    row = lax.broadcasted_iota(jnp.int32, (CHUNK, CHUNK), 0)
    col = lax.broadcasted_iota(jnp.int32, (CHUNK, CHUNK), 1)
    tril = row >= col
    hd_dim = D // A_HEADS
    for hd in range(A_HEADS):
        ws = jnp.where(tril, ws_ref[hd], 0.0).astype(BF16)
        cols = slice(hd * hd_dim, (hd + 1) * hd_dim)
        vcat = jnp.concatenate([vb[c * CHUNK:(c + 1) * CHUNK, cols] for c in range(n_chunks)], axis=1)
        m = _dot(ws, vcat)
        for c in range(n_chunks):
            mixed_ref[c * CHUNK:(c + 1) * CHUNK, cols] = m[:, c * hd_dim:(c + 1) * hd_dim] + bs_ref[hd]
    y = _dot((u * mixed_ref[...]).astype(BF16), wout_ref[...])
    x1_ref[...] = x + gt_ref[...] * y

    @pl.when(i % TILES_PER_SEQ == TILES_PER_SEQ - 1)
    def _():
        cv_ref[...] = v[TM - CHUNK:, :]


def _mix_a_sample_kernel(x_ref, sh_ref, sc_ref, gt_ref, ng_ref, win_ref, lng_ref, lnb_ref,
                         wsd_ref, bsd_ref, wout_ref, x1_ref, cv_ref):
    x = x_ref[...]
    u, v = _mix_a_front(x, sh_ref[...], sc_ref[...], ng_ref[...], win_ref, lng_ref, lnb_ref)
    mixed = wsd_ref[...] * v + bsd_ref[...]
    y = _dot((u * mixed).astype(BF16), wout_ref[...])
    x1_ref[...] = x + gt_ref[...] * y
    cv_ref[...] = v


def _mixer_a(x_all, mod_p, mod_s, ng, layer, la, w_in, ln_g, ln_b, w_s, b_s, w_out):
    bs_b = jnp.broadcast_to(b_s[la][:, :, None], (A_HEADS, CHUNK, D // A_HEADS))
    common = [
        _layer_block((D, 2 * D), la), _layer_block((1, D), la), _layer_block((1, D), la),
    ]
    x_all, cv_p = pl.pallas_call(
        _mix_a_prompt_kernel,
        grid=(N_PROMPT_TILES,),
        in_specs=[pl.BlockSpec((TM, D), lambda i: (i, 0))]
        + _mod_specs_prompt(layer, (0, 1, 2))
        + [_full((1, D))] + common
        + [_layer_block((A_HEADS, CHUNK, CHUNK), la), _full((A_HEADS, CHUNK, D // A_HEADS)),
           _layer_block((D, D), la)],
        out_specs=[
            pl.BlockSpec((TM, D), lambda i: (i, 0)),
            pl.BlockSpec((None, CHUNK, D), lambda i: (i // TILES_PER_SEQ, 0, 0)),
        ],
        out_shape=[
            jax.ShapeDtypeStruct((T_PAD, D), F32),
            jax.ShapeDtypeStruct((BATCH, CHUNK, D), F32),
        ],
        scratch_shapes=[pltpu.VMEM((TM, D), F32)],
        input_output_aliases={0: 0},
        compiler_params=_cparams("arbitrary"),
        name="mixer_a_prompt",
    )(x_all, mod_p, mod_p, mod_p, ng, w_in, ln_g, ln_b, w_s, bs_b, w_out)

    wsd = jnp.repeat(w_s[la, :, 0, 0], D // A_HEADS).reshape(1, D)
    bsd = jnp.repeat(b_s[la, :, 0], D // A_HEADS).reshape(1, D)
    sblk = T_PROMPT // DEC_BATCH
    x_all, cv_s = pl.pallas_call(
        _mix_a_sample_kernel,
        grid=(1,),
        in_specs=[pl.BlockSpec((DEC_BATCH, D), lambda i: (sblk, 0))]
        + _mod_specs_sample(layer, (0, 1, 2), DEC_BATCH)
        + [_full((1, D))] + common
        + [_full((1, D)), _full((1, D)), _layer_block((D, D), la)],
        out_specs=[
            pl.BlockSpec((DEC_BATCH, D), lambda i: (sblk, 0)),
            pl.BlockSpec((DEC_BATCH, D), lambda i: (0, 0)),
        ],
        out_shape=[
            jax.ShapeDtypeStruct((T_PAD, D), F32),
            jax.ShapeDtypeStruct((DEC_BATCH, D), F32),
        ],
        input_output_aliases={0: 0},
        compiler_params=_cparams("arbitrary"),
        name="mixer_a_sample",
    )(x_all, mod_s, mod_s, mod_s, ng, w_in, ln_g, ln_b, wsd, bsd, w_out)
    return x_all, cv_p, cv_s.reshape(DEC_BATCH, 1, D)


def _mix_b_tail(pooled_groups, wgrp_ref, bgrp_ref, scale_ref, wout_ref):
    outs = [
        _dot(pg.astype(BF16), wgrp_ref[g]) + bgrp_ref[g]
        for g, pg in enumerate(pooled_groups)
    ]
    mixed = jnp.concatenate(outs, axis=1) * scale_ref[...]
    return _dot(mixed.astype(BF16), wout_ref[...])


def _mix_b_prompt_kernel(x_ref, sh_ref, sc_ref, gt_ref, ng_ref, win_ref, wgrp_ref, bgrp_ref,
                         scale_ref, wout_ref, x1_ref, st_ref, full_ref):
    i = pl.program_id(0)
    j = i % TILES_PER_SEQ
    x = x_ref[...]
    h = (_rms(x, ng_ref[...]) * (1.0 + sc_ref[...]) + sh_ref[...]).astype(BF16)
    p = _dot(h, win_ref[...])

    @pl.when(j == 0)
    def _():
        full_ref[0:POOL_CARRY, :] = jnp.zeros((POOL_CARRY, D), F32)

    full_ref[POOL_CARRY:POOL_CARRY + TM, :] = p
    pos = j * TM + lax.broadcasted_iota(jnp.int32, (TM, 1), 0)
    pooled = []
    for g, w in enumerate(POOL_WINDOWS):
        cols = slice(g * B_GROUP_DIM, (g + 1) * B_GROUP_DIM)
        s = p[:, cols]
        for k in range(1, w):
            s = s + full_ref[pl.ds(POOL_CARRY - k, TM), cols]
        cnt = jnp.minimum(w, pos + 1).astype(F32)
        pooled.append(s / cnt - p[:, cols])
    y = _mix_b_tail(pooled, wgrp_ref, bgrp_ref, scale_ref, wout_ref)
    x1_ref[...] = x + gt_ref[...] * y

    @pl.when(j == TILES_PER_SEQ - 1)
    def _():
        st_ref[...] = full_ref[pl.ds(POOL_CARRY + TM - POOL_BUF, POOL_BUF), :]

    full_ref[0:POOL_CARRY, :] = full_ref[pl.ds(TM, POOL_CARRY), :]


def _mix_b_sample_kernel(x_ref, sh_ref, sc_ref, gt_ref, ng_ref, st_ref, win_ref, wgrp_ref,
                         bgrp_ref, scale_ref, wout_ref, x1_ref, nst_ref):
    x = x_ref[...]
    h = (_rms(x, ng_ref[...]) * (1.0 + sc_ref[...]) + sh_ref[...]).astype(BF16)
    p = _dot(h, win_ref[...])
    pooled = []
    for g, w in enumerate(POOL_WINDOWS):
        s = p[:, g * B_GROUP_DIM:(g + 1) * B_GROUP_DIM]
        for k in range(1, w):
            c0 = (POOL_BUF - k) * D + g * B_GROUP_DIM
            s = s + st_ref[:, c0:c0 + B_GROUP_DIM]
        cnt = float(min(w, PAST_LEN + 1))
        pooled.append(s / cnt - p[:, g * B_GROUP_DIM:(g + 1) * B_GROUP_DIM])
    y = _mix_b_tail(pooled, wgrp_ref, bgrp_ref, scale_ref, wout_ref)
    x1_ref[...] = x + gt_ref[...] * y
    nst_ref[:, 0:(POOL_BUF - 1) * D] = st_ref[:, D:POOL_BUF * D]
    nst_ref[:, (POOL_BUF - 1) * D:POOL_BUF * D] = p


def _mixer_b(x_all, mod_p, mod_s, ng, layer, lb, state, w_in, w_grp, b_grp, scale, w_out):
    gd = B_GROUP_DIM
    common = [
        _layer_block((D, D), lb), _layer_block((len(POOL_WINDOWS), gd, gd), lb),
        _layer_block((len(POOL_WINDOWS), 1, gd), lb), _layer_block((1, D), lb),
        _layer_block((D, D), lb),
    ]
    x_all, st_p = pl.pallas_call(
        _mix_b_prompt_kernel,
        grid=(N_PROMPT_TILES,),
        in_specs=[pl.BlockSpec((TM, D), lambda i: (i, 0))]
        + _mod_specs_prompt(layer, (0, 1, 2)) + [_full((1, D))] + common,
        out_specs=[
            pl.BlockSpec((TM, D), lambda i: (i, 0)),
            pl.BlockSpec((None, POOL_BUF, D), lambda i: (i // TILES_PER_SEQ, 0, 0)),
        ],
        out_shape=[
            jax.ShapeDtypeStruct((T_PAD, D), F32),
            jax.ShapeDtypeStruct((BATCH, POOL_BUF, D), F32),
        ],
        scratch_shapes=[pltpu.VMEM((TM + POOL_CARRY, D), F32)],
        input_output_aliases={0: 0},
        compiler_params=_cparams("arbitrary"),
        name="mixer_b_prompt",
    )(x_all, mod_p, mod_p, mod_p, ng, w_in, w_grp, b_grp, scale, w_out)

    r = SAMPLE_ROWS
    sblk = T_PROMPT // r
    st2 = state[lb].reshape(DEC_BATCH, POOL_BUF * D)
    x_all, st_s = pl.pallas_call(
        _mix_b_sample_kernel,
        grid=(DEC_BATCH // r,),
        in_specs=[pl.BlockSpec((r, D), lambda i: (sblk + i, 0))]
        + _mod_specs_sample(layer, (0, 1, 2), r) + [_full((1, D))]
        + [pl.BlockSpec((r, POOL_BUF * D), lambda i: (i, 0))] + common,
        out_specs=[
            pl.BlockSpec((r, D), lambda i: (sblk + i, 0)),
            pl.BlockSpec((r, POOL_BUF * D), lambda i: (i, 0)),
        ],
        out_shape=[
            jax.ShapeDtypeStruct((T_PAD, D), F32),
            jax.ShapeDtypeStruct((DEC_BATCH, POOL_BUF * D), F32),
        ],
        input_output_aliases={0: 0},
        compiler_params=_cparams("arbitrary"),
        name="mixer_b_sample",
    )(x_all, mod_s, mod_s, mod_s, ng, st2, w_in, w_grp, b_grp, scale, w_out)
    return x_all, st_p, st_s.reshape(DEC_BATCH, POOL_BUF, D)


def _mix_c_glu(x, sh, sc, ng, win_ref, bin_ref):
    h = (_rms(x, ng) * (1.0 + sc) + sh).astype(BF16)
    ag = _dot(h, win_ref[...]) + bin_ref[...]
    return ag[:, :D] * jax.nn.sigmoid(ag[:, D:])


def _mix_c_tail(conv, lng_ref, lnb_ref, wout_ref, bout_ref):
    z = _ln(conv, lng_ref[...], lnb_ref[...])
    z = z * jax.nn.sigmoid(z)
    return _dot(z.astype(BF16), wout_ref[...]) + bout_ref[...]


def _mix_c_prompt_kernel(x_ref, sh_ref, sc_ref, gt_ref, ng_ref, win_ref, bin_ref, wdw_ref,
                         bdw_ref, lng_ref, lnb_ref, wout_ref, bout_ref, x1_ref, st_ref,
                         full_ref, conv_ref, *shift_refs):
    i = pl.program_id(0)
    j = i % TILES_PER_SEQ
    x = x_ref[...]
    glu = _mix_c_glu(x, sh_ref[...], sc_ref[...], ng_ref[...], win_ref, bin_ref)

    @pl.when(j == 0)
    def _():
        full_ref[0:CONV_CARRY, :] = jnp.zeros((CONV_CARRY, D), F32)

    full_ref[CONV_CARRY:CONV_CARRY + TM, :] = glu
    off = CONV_CARRY - CONV_BUF
    span = TM + (off + CONV_BUF) // SUBLANES * SUBLANES - SUBLANES
    for c in range(D // LANES):
        cols = slice(c * LANES, (c + 1) * LANES)
        shift_ref = shift_refs[c % len(shift_refs)]
        for r in range(1, SUBLANES):
            shift_ref[r - 1] = full_ref[pl.ds(r, span), cols]
        for rb in range(TM // CONV_ROWS):
            acc = None
            for k in range(CONV_WIDTH):
                q, r = divmod(off + k, SUBLANES)
                start = q * SUBLANES + rb * CONV_ROWS
                if r == 0:
                    src = full_ref[pl.ds(start, CONV_ROWS), cols]
                else:
                    src = shift_ref[r - 1, pl.ds(start, CONV_ROWS), :]
                term = src * wdw_ref[k:k + 1, cols]
                acc = term if acc is None else acc + term
            conv_ref[pl.ds(rb * CONV_ROWS, CONV_ROWS), cols] = acc + bdw_ref[:, cols]
    y = _mix_c_tail(conv_ref[...], lng_ref, lnb_ref, wout_ref, bout_ref)
    x1_ref[...] = x + gt_ref[...] * y

    @pl.when(j == TILES_PER_SEQ - 1)
    def _():
        st_ref[...] = full_ref[pl.ds(CONV_CARRY + TM - CONV_BUF, CONV_BUF), :]

    full_ref[0:CONV_CARRY, :] = full_ref[pl.ds(TM, CONV_CARRY), :]


def _mix_c_sample_kernel(x_ref, sh_ref, sc_ref, gt_ref, ng_ref, st_ref, win_ref, bin_ref,
                         wdw_ref, bdw_ref, lng_ref, lnb_ref, wout_ref, bout_ref, x1_ref, nst_ref):
    x = x_ref[...]
    glu = _mix_c_glu(x, sh_ref[...], sc_ref[...], ng_ref[...], win_ref, bin_ref)
    acc = glu * wdw_ref[CONV_BUF:CONV_BUF + 1, :]
    for k in range(CONV_BUF):
        acc = acc + st_ref[:, k * D:(k + 1) * D] * wdw_ref[k:k + 1, :]
    y = _mix_c_tail(acc + bdw_ref[...], lng_ref, lnb_ref, wout_ref, bout_ref)
    x1_ref[...] = x + gt_ref[...] * y
    nst_ref[:, 0:(CONV_BUF - 1) * D] = st_ref[:, D:CONV_BUF * D]
    nst_ref[:, (CONV_BUF - 1) * D:CONV_BUF * D] = glu


def _mixer_c(x_all, mod_p, mod_s, ng, layer, lc, state, w_in, b_in, w_dw, b_dw, ln_g, ln_b,
             w_out, b_out):
    common = [
        _layer_block((D, 2 * D), lc), _layer_block((1, 2 * D), lc),
        _layer_block((CONV_WIDTH, D), lc), _layer_block((1, D), lc), _layer_block((1, D), lc),
        _layer_block((1, D), lc), _layer_block((D, D), lc), _layer_block((1, D), lc),
    ]
    x_all, st_p = pl.pallas_call(
        _mix_c_prompt_kernel,
        grid=(N_PROMPT_TILES,),
        in_specs=[pl.BlockSpec((TM, D), lambda i: (i, 0))]
        + _mod_specs_prompt(layer, (0, 1, 2)) + [_full((1, D))] + common,
        out_specs=[
            pl.BlockSpec((TM, D), lambda i: (i, 0)),
            pl.BlockSpec((None, CONV_BUF, D), lambda i: (i // TILES_PER_SEQ, 0, 0)),
        ],
        out_shape=[
            jax.ShapeDtypeStruct((T_PAD, D), F32),
            jax.ShapeDtypeStruct((BATCH, CONV_BUF, D), F32),
        ],
        scratch_shapes=[pltpu.VMEM((TM + CONV_CARRY, D), F32), pltpu.VMEM((TM, D), F32)]
        + [pltpu.VMEM((SUBLANES - 1, TM + CONV_CARRY - SUBLANES, LANES), F32)] * 2,
        input_output_aliases={0: 0},
        compiler_params=_cparams("arbitrary"),
        name="mixer_c_prompt",
    )(x_all, mod_p, mod_p, mod_p, ng, w_in, b_in, w_dw, b_dw, ln_g, ln_b, w_out, b_out)

    r = SAMPLE_ROWS
    sblk = T_PROMPT // r
    st2 = state[lc].reshape(DEC_BATCH, CONV_BUF * D)
    x_all, st_s = pl.pallas_call(
        _mix_c_sample_kernel,
        grid=(DEC_BATCH // r,),
        in_specs=[pl.BlockSpec((r, D), lambda i: (sblk + i, 0))]
        + _mod_specs_sample(layer, (0, 1, 2), r) + [_full((1, D))]
        + [pl.BlockSpec((r, CONV_BUF * D), lambda i: (i, 0))] + common,
        out_specs=[
            pl.BlockSpec((r, D), lambda i: (sblk + i, 0)),
            pl.BlockSpec((r, CONV_BUF * D), lambda i: (i, 0)),
        ],
        out_shape=[
            jax.ShapeDtypeStruct((T_PAD, D), F32),
            jax.ShapeDtypeStruct((DEC_BATCH, CONV_BUF * D), F32),
        ],
        input_output_aliases={0: 0},
        compiler_params=_cparams("arbitrary"),
        name="mixer_c_sample",
    )(x_all, mod_s, mod_s, mod_s, ng, st2, w_in, b_in, w_dw, b_dw, ln_g, ln_b, w_out, b_out)
    return x_all, st_p, st_s.reshape(DEC_BATCH, CONV_BUF, D)


def _tile_mod(i, p_ref, s_ref):
    s_rows = jnp.concatenate([s_ref[...], jnp.zeros((TM - DEC_BATCH, D), F32)], axis=0)
    return jnp.where(i >= N_PROMPT_TILES, s_rows, p_ref[...])


def _unified_mod_specs(layer, chunks):
    specs = []
    for c in chunks:
        specs.append(pl.BlockSpec(
            (None, None, 1, D),
            lambda i, c=c: (layer, jnp.minimum(i // TILES_PER_SEQ, BATCH - 1), 0, c)))
        specs.append(pl.BlockSpec((None, DEC_BATCH, D), lambda i, c=c: (layer, 0, c)))
    return specs


def _router_kernel(x_ref, shp_ref, shs_ref, scp_ref, scs_ref, ng_ref, wr_ref, br_ref,
                   h2_ref, meta_ref, wts_ref, cnt_ref, carry_ref):
    i = pl.program_id(0)

    @pl.when(i == 0)
    def _():
        carry_ref[...] = jnp.zeros((1, LANES), F32)

    sh = _tile_mod(i, shp_ref, shs_ref)
    sc = _tile_mod(i, scp_ref, scs_ref)
    h2 = _rms(x_ref[...], ng_ref[...]) * (1.0 + sc) + sh
    h2_ref[...] = _pack_bf16_pairs(h2)
    lg = _dot(h2.astype(BF16), wr_ref[...]) + br_ref[...]
    lane = lax.broadcasted_iota(jnp.int32, (TM, LANES), 1)
    neg = -jnp.inf
    is_g = lane < N_GROUPS
    gm = jnp.where(is_g, lg, neg)
    gmax = jnp.max(gm, axis=1, keepdims=True)
    lane_f = lane.astype(F32)
    far = float(LANES)
    gsel = jnp.min(jnp.where(gm == gmax, lane_f, far), axis=1, keepdims=True).astype(jnp.int32)
    gsum = jnp.sum(jnp.where(is_g, jnp.exp(lg - gmax), 0.0), axis=1, keepdims=True)
    g_w = 1.0 / gsum
    e_lane = lane - EXPERT_LANE0
    in_grp = (e_lane >= 0) & (e_lane < N_EXPERTS) & (
        lax.shift_right_arithmetic(e_lane, EXPERTS_PER_GROUP.bit_length() - 1) == gsel)
    em = jnp.where(in_grp, lg, neg)
    m1 = jnp.max(em, axis=1, keepdims=True)
    i1 = jnp.min(jnp.where(em == m1, lane_f, far), axis=1, keepdims=True).astype(jnp.int32)
    em2 = jnp.where(lane == i1, neg, em)
    m2 = jnp.max(em2, axis=1, keepdims=True)
    i2 = jnp.min(jnp.where(em2 == m2, lane_f, far), axis=1, keepdims=True).astype(jnp.int32)
    e2 = jnp.exp(m2 - m1)
    den = 1.0 + e2
    w1 = (1.0 / den) * g_w
    w2 = (e2 / den) * g_w

    hit1 = lane == i1
    hit2 = lane == i2
    assign = jnp.where(hit1 | hit2, 1.0, 0.0).astype(BF16)
    r = lax.broadcasted_iota(jnp.int32, (TM, TM), 0)
    c = lax.broadcasted_iota(jnp.int32, (TM, TM), 1)
    before = (c < r).astype(BF16)
    seen = _dot(before, assign) + carry_ref[...]
    rank1 = jnp.sum(jnp.where(hit1, seen, 0.0), axis=1, keepdims=True)
    rank2 = jnp.sum(jnp.where(hit2, seen, 0.0), axis=1, keepdims=True)
    carry_ref[...] = carry_ref[...] + jnp.sum(assign.astype(F32), axis=0, keepdims=True)
    cnt_ref[...] = carry_ref[...]

    ex1 = (i1 - EXPERT_LANE0).astype(F32)
    ex2 = (i2 - EXPERT_LANE0).astype(F32)
    fields = jnp.where(lane == 0, ex1, jnp.where(lane == 1, ex2,
                       jnp.where(lane == 2, rank1, jnp.where(lane == 3, rank2, 0.0))))
    meta_ref[...] = fields.T[0:META_ROWS, :]
    wts_ref[...] = jnp.where(lane == 0, w1, jnp.where(lane == 1, w2, 0.0))


def _router(x_all, mod_p, mod_s, ng, layer, w_r, b_r):
    return pl.pallas_call(
        _router_kernel,
        grid=(N_TILES,),
        in_specs=[pl.BlockSpec((TM, D), lambda i: (i, 0))]
        + _unified_mod_specs(layer, (3, 4))
        + [_full((1, D)), _full((D, LANES)), _full((1, LANES))],
        out_specs=[
            pl.BlockSpec((TM, D // 2), lambda i: (i, 0)),
            pl.BlockSpec((META_ROWS, TM), lambda i: (0, i)),
            pl.BlockSpec((TM, LANES), lambda i: (i, 0)),
            pl.BlockSpec((1, LANES), lambda i: (0, 0)),
        ],
        out_shape=[
            jax.ShapeDtypeStruct((T_PAD, D // 2), jnp.int32),
            jax.ShapeDtypeStruct((META_ROWS, T_PAD), F32),
            jax.ShapeDtypeStruct((T_PAD, LANES), F32),
            jax.ShapeDtypeStruct((1, LANES), F32),
        ],
        scratch_shapes=[pltpu.VMEM((1, LANES), F32)],
        compiler_params=_cparams("arbitrary"),
        name="router",
    )(x_all, mod_p, mod_s, mod_p, mod_s, ng, w_r, b_r)


SC_INDEX_MAX = 128
SC_ROWS_BYTES = 256 * 1024


def _gather_chunk(per_worker, row_bytes):
    top = min(SC_INDEX_MAX, SC_ROWS_BYTES // row_bytes) // SUBLANES * SUBLANES
    for ch in range(top, SUBLANES - 1, -SUBLANES):
        if per_worker % ch == 0:
            return ch
    raise ValueError(per_worker)


def _sc_gather(table, idx):
    nc, nw = SC_CORES, SC_CORES * SC_SUBCORES
    m = idx.shape[0]
    per_w = m // nw
    assert per_w * nw == m
    width = table.shape[1]
    ch = _gather_chunk(per_w, width * table.dtype.itemsize)
    mesh = plsc.VectorSubcoreMesh(core_axis_name="c", subcore_axis_name="s")

    @functools.partial(
        pl.kernel,
        out_type=jax.ShapeDtypeStruct((m, width), table.dtype),
        mesh=mesh,
        scratch_types=[
            pltpu.VMEM((ch,), jnp.int32),
            pltpu.VMEM((ch, width), table.dtype),
            pltpu.SemaphoreType.DMA,
        ],
    )
    def gather_kernel(t_hbm, i_hbm, o_hbm, idx_v, rows_v, sem):
        wid = lax.axis_index("s") * nc + lax.axis_index("c")
        base = wid * per_w

        @pl.loop(0, per_w // ch)
        def _(j):
            off = pl.multiple_of(base + j * ch, 8)
            pltpu.sync_copy(i_hbm.at[pl.ds(off, ch)], idx_v)
            pltpu.async_copy(t_hbm.at[idx_v], rows_v, sem).wait()
            pltpu.sync_copy(rows_v, o_hbm.at[pl.ds(off, ch)])

    return gather_kernel(table, idx)


def _sc_scatter2(rows, dest0, dest1, n_out):
    nc, nw = SC_CORES, SC_CORES * SC_SUBCORES
    n, width = rows.shape
    per_w = n // nw
    assert per_w * nw == n
    ch = _gather_chunk(per_w, width * rows.dtype.itemsize)
    mesh = plsc.VectorSubcoreMesh(core_axis_name="c", subcore_axis_name="s")

    @functools.partial(
        pl.kernel,
        out_type=jax.ShapeDtypeStruct((n_out, width), rows.dtype),
        mesh=mesh,
        scratch_types=[
            pltpu.VMEM((ch,), jnp.int32),
            pltpu.VMEM((ch,), jnp.int32),
            pltpu.VMEM((ch, width), rows.dtype),
            pltpu.SemaphoreType.DMA,
            pltpu.SemaphoreType.DMA,
        ],
    )
    def scatter_kernel(r_hbm, d0_hbm, d1_hbm, o_hbm, i0_v, i1_v, rows_v, sem0, sem1):
        wid = lax.axis_index("s") * nc + lax.axis_index("c")
        base = wid * per_w

        @pl.loop(0, per_w // ch)
        def _(j):
            off = pl.multiple_of(base + j * ch, 8)
            pltpu.sync_copy(d0_hbm.at[pl.ds(off, ch)], i0_v)
            pltpu.sync_copy(d1_hbm.at[pl.ds(off, ch)], i1_v)
            pltpu.sync_copy(r_hbm.at[pl.ds(off, ch)], rows_v)
            c0 = pltpu.async_copy(rows_v, o_hbm.at[i0_v], sem0)
            c1 = pltpu.async_copy(rows_v, o_hbm.at[i1_v], sem1)
            c0.wait()
            c1.wait()

    return scatter_kernel(rows, dest0, dest1)


def _expert_kernel(be_ref, first_ref, nvalid_ref, nused_ref, x_ref, wg_ref, wu_ref, wd_ref,
                   y_ref, wgb_ref, wub_ref, wdb_ref):
    b = pl.program_id(0)

    @pl.when(b < nused_ref[0])
    def _():
        @pl.when(first_ref[b] == 1)
        def _():
            wgb_ref[...] = wg_ref[...].astype(BF16)
            wub_ref[...] = wu_ref[...].astype(BF16)
            wdb_ref[...] = wd_ref[...].astype(BF16)

        live = lax.broadcasted_iota(jnp.int32, (BM, 1), 0) < nvalid_ref[b]
        x = _unpack_bf16_pairs(jnp.where(live, x_ref[...], 0), BF16)
        g = _dot(x, wgb_ref[...])
        u = _dot(x, wub_ref[...])
        hmid = (g * jax.nn.sigmoid(g)) * u
        y_ref[...] = _pack_bf16_pairs(_dot(hmid.astype(BF16), wdb_ref[...]))


def _experts(x_sorted, block_expert, first, nvalid, nused, layer, w_gate, w_up, w_down):
    def row_map(b, be, fi, nv, nu):
        return (jnp.minimum(b, nu[0] - 1), 0)

    def w_map(b, be, fi, nv, nu):
        return (layer, be[b], 0, 0)

    grid_spec = pltpu.PrefetchScalarGridSpec(
        num_scalar_prefetch=4,
        grid=(N_BLOCKS,),
        in_specs=[
            pl.BlockSpec((BM, D // 2), row_map),
            pl.BlockSpec((None, None, D, D_EXPERT), w_map),
            pl.BlockSpec((None, None, D, D_EXPERT), w_map),
            pl.BlockSpec((None, None, D_EXPERT, D), w_map),
        ],
        out_specs=pl.BlockSpec((BM, D // 2), row_map),
        scratch_shapes=[
            pltpu.VMEM((D, D_EXPERT), BF16),
            pltpu.VMEM((D, D_EXPERT), BF16),
            pltpu.VMEM((D_EXPERT, D), BF16),
        ],
    )
    return pl.pallas_call(
        _expert_kernel,
        grid_spec=grid_spec,
        out_shape=jax.ShapeDtypeStruct((P_ROWS, D // 2), jnp.int32),
        compiler_params=_cparams("arbitrary"),
        name="experts",
    )(block_expert, first, nvalid, nused, x_sorted, w_gate, w_up, w_down)


def _combine_kernel(x_ref, g0_ref, g1_ref, wts_ref, gtp_ref, gts_ref, o_ref):
    i = pl.program_id(0)
    gt = _tile_mod(i, gtp_ref, gts_ref)
    w = wts_ref[...]
    y0 = _unpack_bf16_pairs(g0_ref[...], F32)
    y1 = _unpack_bf16_pairs(g1_ref[...], F32)
    o_ref[...] = x_ref[...] + gt * (w[:, 0:1] * y0 + w[:, 1:2] * y1)


def _combine(x_all, gathered, wts, mod_p, mod_s, layer):
    return pl.pallas_call(
        _combine_kernel,
        grid=(N_TILES,),
        in_specs=[
            pl.BlockSpec((TM, D), lambda i: (i, 0)),
            pl.BlockSpec((TM, D // 2), lambda i: (i, 0)),
            pl.BlockSpec((TM, D // 2), lambda i: (i + N_TILES, 0)),
            pl.BlockSpec((TM, LANES), lambda i: (i, 0)),
        ] + _unified_mod_specs(layer, (5,)),
        out_specs=pl.BlockSpec((TM, D), lambda i: (i, 0)),
        out_shape=jax.ShapeDtypeStruct((T_PAD, D), F32),
        input_output_aliases={0: 0},
        compiler_params=_cparams("arbitrary"),
        name="combine",
    )(x_all, gathered, gathered, wts, mod_p, mod_s)


def _moe(x_all, mod_p, mod_s, ng, layer, w_r, b_r, w_gate, w_up, w_down):
    h2, meta, wts, cnt = _router(x_all, mod_p, mod_s, ng, layer, w_r, b_r)
    counts = cnt[0, EXPERT_LANE0:EXPERT_LANE0 + N_EXPERTS].astype(jnp.int32)
    padded = (counts + BM - 1) // BM * BM
    pend = jnp.cumsum(padded)
    pstart = pend - padded
    expert = meta[0:2].astype(jnp.int32)
    dest = meta[2:4].astype(jnp.int32)
    for e in range(N_EXPERTS):
        dest = dest + jnp.where(expert == e, pstart[e], 0)
    block0 = jnp.arange(N_BLOCKS, dtype=jnp.int32) * BM
    block_expert = jnp.minimum(
        jnp.sum((pend[None, :] <= block0[:, None]).astype(jnp.int32), axis=1), N_EXPERTS - 1)
    of_block = block_expert[:, None] == jnp.arange(N_EXPERTS, dtype=jnp.int32)[None, :]
    seg_end = jnp.sum(jnp.where(of_block, (pstart + counts)[None, :], 0), axis=1)
    nvalid = jnp.clip(seg_end - block0, 0, BM).astype(jnp.int32)
    nused = (pend[-1] // BM).astype(jnp.int32).reshape(1)
    first = jnp.concatenate(
        [jnp.ones((1,), jnp.int32), (block_expert[1:] != block_expert[:-1]).astype(jnp.int32)])

    x_sorted = _sc_scatter2(h2, dest[0], dest[1], P_ROWS)
    y_sorted = _experts(x_sorted, block_expert, first, nvalid, nused, layer, w_gate, w_up, w_down)
    gathered = _sc_gather(y_sorted, dest.reshape(-1))
    return _combine(x_all, gathered, wts, mod_p, mod_s, layer)


def _final_kernel(x_ref, g_ref, yp_ref, ys_ref):
    i = pl.program_id(0)
    y = _rms(x_ref[...], g_ref[...])

    @pl.when(i < N_PROMPT_TILES)
    def _():
        yp_ref[...] = y

    @pl.when(i == N_PROMPT_TILES)
    def _():
        ys_ref[...] = y[0:DEC_BATCH, :]


def _final_norm(x_all, g):
    return pl.pallas_call(
        _final_kernel,
        grid=(N_PROMPT_TILES + 1,),
        in_specs=[pl.BlockSpec((TM, D), lambda i: (i, 0)), _full((1, D))],
        out_specs=[
            pl.BlockSpec((TM, D), lambda i: (jnp.minimum(i, N_PROMPT_TILES - 1), 0)),
            pl.BlockSpec((DEC_BATCH, D), lambda i: (0, 0)),
        ],
        out_shape=[
            jax.ShapeDtypeStruct((T_PROMPT, D), F32),
            jax.ShapeDtypeStruct((DEC_BATCH, D), F32),
        ],
        compiler_params=_cparams("arbitrary"),
        name="final_norm",
    )(x_all, g)


def kernel(x_prompt, x_sample, c_prompt, c_sample, state_pool, state_conv, ada_w, ada_b, norm_g, final_g, a_w_in, a_ln_g, a_ln_b, a_w_s, a_b_s, a_w_out, b_w_in, b_w_grp, b_b_grp, b_scale, b_w_out, c_w_in, c_b_in, c_w_dw, c_b_dw, c_ln_g, c_ln_b, c_w_out, c_b_out, moe_w_grp, moe_b_grp, moe_w_exp, moe_b_exp, moe_w_gate, moe_w_up, moe_w_down):
    x_all = jnp.concatenate([
        x_prompt.reshape(T_PROMPT, D), x_sample.reshape(DEC_BATCH, D),
        jnp.zeros((T_PAD - T_ALL, D), F32)], axis=0)
    mods = _ada_mods(jnp.concatenate([c_prompt, c_sample], axis=0), ada_w, ada_b)
    mod_p = mods[:, :BATCH].reshape(DEPTH, BATCH, 1, 6 * D)
    mod_s = mods[:, BATCH:]

    a_w_in_b, a_w_out_b = a_w_in.astype(BF16), a_w_out.astype(BF16)
    b_w_in_b, b_w_grp_b, b_w_out_b = b_w_in.astype(BF16), b_w_grp.astype(BF16), b_w_out.astype(BF16)
    c_w_in_b, c_w_out_b = c_w_in.astype(BF16), c_w_out.astype(BF16)
    w_r = jnp.pad(jnp.concatenate([moe_w_grp, moe_w_exp], axis=2),
                  ((0, 0), (0, 0), (0, LANES - N_GROUPS - N_EXPERTS))).astype(BF16)
    b_r = jnp.pad(jnp.concatenate([moe_b_grp, moe_b_exp], axis=1),
                  ((0, 0), (0, LANES - N_GROUPS - N_EXPERTS)))

    new_a_p, new_a_s, new_b_p, new_b_s, new_c_p, new_c_s = [], [], [], [], [], []
    ia = ib = ic = 0
    for layer in range(DEPTH):
        ng1 = norm_g[layer, 0].reshape(1, D)
        ng2 = norm_g[layer, 1].reshape(1, D)
        kind = layer % 3
        if kind == 0:
            x_all, st_p, st_s = _mixer_a(
                x_all, mod_p, mod_s, ng1, layer, ia, a_w_in_b, a_ln_g.reshape(-1, 1, D),
                a_ln_b.reshape(-1, 1, D), a_w_s, a_b_s, a_w_out_b)
            new_a_p.append(st_p)
            new_a_s.append(st_s)
            ia += 1
        elif kind == 1:
            x_all, st_p, st_s = _mixer_b(
                x_all, mod_p, mod_s, ng1, layer, ib, state_pool, b_w_in_b, b_w_grp_b,
                b_b_grp.reshape(-1, len(POOL_WINDOWS), 1, B_GROUP_DIM),
                b_scale.reshape(-1, 1, D), b_w_out_b)
            new_b_p.append(st_p)
            new_b_s.append(st_s)
            ib += 1
        else:
            x_all, st_p, st_s = _mixer_c(
                x_all, mod_p, mod_s, ng1, layer, ic, state_conv, c_w_in_b,
                c_b_in.reshape(-1, 1, 2 * D), c_w_dw, c_b_dw.reshape(-1, 1, D),
                c_ln_g.reshape(-1, 1, D), c_ln_b.reshape(-1, 1, D), c_w_out_b,
                c_b_out.reshape(-1, 1, D))
            new_c_p.append(st_p)
            new_c_s.append(st_s)
            ic += 1
        x_all = _moe(x_all, mod_p, mod_s, ng2, layer, w_r[layer], b_r[layer].reshape(1, LANES),
                     moe_w_gate, moe_w_up, moe_w_down)

    y_p, y_s = _final_norm(x_all, final_g.reshape(1, D))
    return (y_p.reshape(BATCH, SEQ, D), y_s.reshape(DEC_BATCH, 1, D),
            jnp.stack(new_a_p), jnp.stack(new_a_s), jnp.stack(new_b_p), jnp.stack(new_b_s),
            jnp.stack(new_c_p), jnp.stack(new_c_s))
```

```python
import functools

import jax
import jax.numpy as jnp
from jax import lax
from jax.experimental import pallas as pl
from jax.experimental.pallas import tpu as pltpu
from jax.experimental.pallas import tpu_sc as plsc

F32 = jnp.float32
BF16 = jnp.bfloat16

D = 1024
BATCH = 8
SEQ = 2048
DEPTH = 4
DEC_BATCH = 128
PAST_LEN = 16384
CHUNK = 128
A_HEADS = 8
POOL_WINDOWS = (2, 4, 8, 16)
B_GROUP_DIM = D // len(POOL_WINDOWS)
POOL_BUF = max(POOL_WINDOWS) - 1
CONV_WIDTH = 31
CONV_BUF = CONV_WIDTH - 1
N_GROUPS = 4
EXPERTS_PER_GROUP = 8
N_EXPERTS = N_GROUPS * EXPERTS_PER_GROUP
D_EXPERT = D // 2
EPS = 1e-6

LANES = 128
SUBLANES = 8
CONV_ROWS = 128
SC_CORES = 2
SC_SUBCORES = 16
TM = 512
TILES_PER_SEQ = SEQ // TM
T_PROMPT = BATCH * SEQ
T_ALL = T_PROMPT + DEC_BATCH
N_TILES = -(-T_ALL // TM)
T_PAD = N_TILES * TM
N_PROMPT_TILES = T_PROMPT // TM
SAMPLE_ROWS = 32
BM = 512
N_BLOCKS = -(-(2 * T_PAD) // BM) + N_EXPERTS
P_ROWS = N_BLOCKS * BM
META_ROWS = 8
POOL_CARRY = 16
CONV_CARRY = 32
EXPERT_LANE0 = N_GROUPS
VMEM_LIMIT = 56 * 1024 * 1024


def _cparams(*sem):
    return pltpu.CompilerParams(dimension_semantics=sem, vmem_limit_bytes=VMEM_LIMIT)


def _rms(x, g):
    return x * lax.rsqrt(jnp.mean(x * x, axis=-1, keepdims=True) + EPS) * g


def _ln(x, g, b):
    mu = jnp.mean(x, axis=-1, keepdims=True)
    xc = x - mu
    var = jnp.mean(xc * xc, axis=-1, keepdims=True)
    return xc * lax.rsqrt(var + EPS) * g + b


def _dot(a, b):
    return jnp.dot(a, b, preferred_element_type=F32)


def _pack_bf16_pairs(v):
    w = v.shape[1] // 2
    lo = lax.bitcast_convert_type(v[:, :w].astype(BF16).astype(F32), jnp.uint32)
    hi = lax.bitcast_convert_type(v[:, w:].astype(BF16).astype(F32), jnp.uint32)
    return lax.bitcast_convert_type(lax.shift_right_logical(lo, jnp.uint32(16)) | hi, jnp.int32)


def _unpack_bf16_pairs(words, dtype):
    u = lax.bitcast_convert_type(words, jnp.uint32)
    lo = lax.bitcast_convert_type(lax.shift_left(u, jnp.uint32(16)), F32)
    hi = lax.bitcast_convert_type(u & jnp.uint32(0xFFFF0000), F32)
    return jnp.concatenate([lo.astype(dtype), hi.astype(dtype)], axis=1)


def _ada_kernel(c_ref, w_ref, b_ref, o_ref):
    c = c_ref[...]
    s = (c * jax.nn.sigmoid(c)).astype(BF16)
    o_ref[...] = _dot(s, w_ref[...].astype(BF16)) + b_ref[...]


def _ada_mods(c_all, ada_w, ada_b):
    n = c_all.shape[0]
    tn = 2048
    return pl.pallas_call(
        _ada_kernel,
        grid=(DEPTH, 6 * D // tn),
        in_specs=[
            pl.BlockSpec((n, D), lambda l, j: (0, 0)),
            pl.BlockSpec((None, D, tn), lambda l, j: (l, 0, j)),
            pl.BlockSpec((None, 1, tn), lambda l, j: (l, 0, j)),
        ],
        out_specs=pl.BlockSpec((None, n, tn), lambda l, j: (l, 0, j)),
        out_shape=jax.ShapeDtypeStruct((DEPTH, n, 6 * D), F32),
        compiler_params=_cparams("parallel", "parallel"),
        name="ada_mods",
    )(c_all, ada_w, ada_b.reshape(DEPTH, 1, 6 * D))


def _mod_specs_prompt(layer, chunks):
    return [
        pl.BlockSpec((None, None, 1, D), lambda i, c=c: (layer, i // TILES_PER_SEQ, 0, c))
        for c in chunks
    ]


def _mod_specs_sample(layer, chunks, rows, row_block0=0):
    return [
        pl.BlockSpec((None, rows, D), lambda i, c=c: (layer, i + row_block0, c)) for c in chunks
    ]


def _full(shape):
    nd = len(shape)
    return pl.BlockSpec(shape, lambda i: (0,) * nd)


def _layer_block(shape, layer):
    nd = len(shape)
    return pl.BlockSpec((None,) + shape, lambda i: (layer,) + (0,) * nd)


def _mix_a_front(x, sh, sc, ng, win_ref, lng_ref, lnb_ref):
    h = (_rms(x, ng) * (1.0 + sc) + sh).astype(BF16)
    z = jax.nn.gelu(_dot(h, win_ref[...]))
    u = z[:, :D]
    v = _ln(z[:, D:], lng_ref[...], lnb_ref[...])
    return u, v


def _mix_a_prompt_kernel(x_ref, sh_ref, sc_ref, gt_ref, ng_ref, win_ref, lng_ref, lnb_ref,
                         ws_ref, bs_ref, wout_ref, x1_ref, cv_ref, mixed_ref):
    i = pl.program_id(0)
    x = x_ref[...]
    u, v = _mix_a_front(x, sh_ref[...], sc_ref[...], ng_ref[...], win_ref, lng_ref, lnb_ref)
    vb = v.astype(BF16)
    n_chunks = TM // CHUNK
    row = lax.broadcasted_iota(jnp.int32, (CHUNK, CHUNK), 0)
    col = lax.broadcasted_iota(jnp.int32, (CHUNK, CHUNK), 1)
    tril = row >= col
    hd_dim = D // A_HEADS
    for hd in range(A_HEADS):
        ws = jnp.where(tril, ws_ref[hd], 0.0).astype(BF16)
        cols = slice(hd * hd_dim, (hd + 1) * hd_dim)
        vcat = jnp.concatenate([vb[c * CHUNK:(c + 1) * CHUNK, cols] for c in range(n_chunks)], axis=1)
        m = _dot(ws, vcat)
        for c in range(n_chunks):
            mixed_ref[c * CHUNK:(c + 1) * CHUNK, cols] = m[:, c * hd_dim:(c + 1) * hd_dim] + bs_ref[hd]
    y = _dot((u * mixed_ref[...]).astype(BF16), wout_ref[...])
    x1_ref[...] = x + gt_ref[...] * y

    @pl.when(i % TILES_PER_SEQ == TILES_PER_SEQ - 1)
    def _():
        cv_ref[...] = v[TM - CHUNK:, :]


def _mix_a_sample_kernel(x_ref, sh_ref, sc_ref, gt_ref, ng_ref, win_ref, lng_ref, lnb_ref,
                         wsd_ref, bsd_ref, wout_ref, x1_ref, cv_ref):
    x = x_ref[...]
    u, v = _mix_a_front(x, sh_ref[...], sc_ref[...], ng_ref[...], win_ref, lng_ref, lnb_ref)
    mixed = wsd_ref[...] * v + bsd_ref[...]
    y = _dot((u * mixed).astype(BF16), wout_ref[...])
    x1_ref[...] = x + gt_ref[...] * y
    cv_ref[...] = v


def _mixer_a(x_all, mod_p, mod_s, ng, layer, la, w_in, ln_g, ln_b, w_s, b_s, w_out):
    bs_b = jnp.broadcast_to(b_s[la][:, :, None], (A_HEADS, CHUNK, D // A_HEADS))
    common = [
        _layer_block((D, 2 * D), la), _layer_block((1, D), la), _layer_block((1, D), la),
    ]
    x_all, cv_p = pl.pallas_call(
        _mix_a_prompt_kernel,
        grid=(N_PROMPT_TILES,),
        in_specs=[pl.BlockSpec((TM, D), lambda i: (i, 0))]
        + _mod_specs_prompt(layer, (0, 1, 2))
        + [_full((1, D))] + common
        + [_layer_block((A_HEADS, CHUNK, CHUNK), la), _full((A_HEADS, CHUNK, D // A_HEADS)),
           _layer_block((D, D), la)],
        out_specs=[
            pl.BlockSpec((TM, D), lambda i: (i, 0)),
            pl.BlockSpec((None, CHUNK, D), lambda i: (i // TILES_PER_SEQ, 0, 0)),
        ],
        out_shape=[
            jax.ShapeDtypeStruct((T_PAD, D), F32),
            jax.ShapeDtypeStruct((BATCH, CHUNK, D), F32),
        ],
        scratch_shapes=[pltpu.VMEM((TM, D), F32)],
        input_output_aliases={0: 0},
        compiler_params=_cparams("arbitrary"),
        name="mixer_a_prompt",
    )(x_all, mod_p, mod_p, mod_p, ng, w_in, ln_g, ln_b, w_s, bs_b, w_out)

    wsd = jnp.repeat(w_s[la, :, 0, 0], D // A_HEADS).reshape(1, D)
    bsd = jnp.repeat(b_s[la, :, 0], D // A_HEADS).reshape(1, D)
    sblk = T_PROMPT // DEC_BATCH
    x_all, cv_s = pl.pallas_call(
        _mix_a_sample_kernel,
        grid=(1,),
        in_specs=[pl.BlockSpec((DEC_BATCH, D), lambda i: (sblk, 0))]
        + _mod_specs_sample(layer, (0, 1, 2), DEC_BATCH)
        + [_full((1, D))] + common
        + [_full((1, D)), _full((1, D)), _layer_block((D, D), la)],
        out_specs=[
            pl.BlockSpec((DEC_BATCH, D), lambda i: (sblk, 0)),
            pl.BlockSpec((DEC_BATCH, D), lambda i: (0, 0)),
        ],
        out_shape=[
            jax.ShapeDtypeStruct((T_PAD, D), F32),
            jax.ShapeDtypeStruct((DEC_BATCH, D), F32),
        ],
        input_output_aliases={0: 0},
        compiler_params=_cparams("arbitrary"),
        name="mixer_a_sample",
    )(x_all, mod_s, mod_s, mod_s, ng, w_in, ln_g, ln_b, wsd, bsd, w_out)
    return x_all, cv_p, cv_s.reshape(DEC_BATCH, 1, D)


def _mix_b_tail(pooled_groups, wgrp_ref, bgrp_ref, scale_ref, wout_ref):
    outs = [
        _dot(pg.astype(BF16), wgrp_ref[g]) + bgrp_ref[g]
        for g, pg in enumerate(pooled_groups)
    ]
    mixed = jnp.concatenate(outs, axis=1) * scale_ref[...]
    return _dot(mixed.astype(BF16), wout_ref[...])


def _mix_b_prompt_kernel(x_ref, sh_ref, sc_ref, gt_ref, ng_ref, win_ref, wgrp_ref, bgrp_ref,
                         scale_ref, wout_ref, x1_ref, st_ref, full_ref):
    i = pl.program_id(0)
    j = i % TILES_PER_SEQ
    x = x_ref[...]
    h = (_rms(x, ng_ref[...]) * (1.0 + sc_ref[...]) + sh_ref[...]).astype(BF16)
    p = _dot(h, win_ref[...])

    @pl.when(j == 0)
    def _():
        full_ref[0:POOL_CARRY, :] = jnp.zeros((POOL_CARRY, D), F32)

    full_ref[POOL_CARRY:POOL_CARRY + TM, :] = p
    pos = j * TM + lax.broadcasted_iota(jnp.int32, (TM, 1), 0)
    pooled = []
    for g, w in enumerate(POOL_WINDOWS):
        cols = slice(g * B_GROUP_DIM, (g + 1) * B_GROUP_DIM)
        s = p[:, cols]
        for k in range(1, w):
            s = s + full_ref[pl.ds(POOL_CARRY - k, TM), cols]
        cnt = jnp.minimum(w, pos + 1).astype(F32)
        pooled.append(s / cnt - p[:, cols])
    y = _mix_b_tail(pooled, wgrp_ref, bgrp_ref, scale_ref, wout_ref)
    x1_ref[...] = x + gt_ref[...] * y

    @pl.when(j == TILES_PER_SEQ - 1)
    def _():
        st_ref[...] = full_ref[pl.ds(POOL_CARRY + TM - POOL_BUF, POOL_BUF), :]

    full_ref[0:POOL_CARRY, :] = full_ref[pl.ds(TM, POOL_CARRY), :]


def _mix_b_sample_kernel(x_ref, sh_ref, sc_ref, gt_ref, ng_ref, st_ref, win_ref, wgrp_ref,
                         bgrp_ref, scale_ref, wout_ref, x1_ref, nst_ref):
    x = x_ref[...]
    h = (_rms(x, ng_ref[...]) * (1.0 + sc_ref[...]) + sh_ref[...]).astype(BF16)
    p = _dot(h, win_ref[...])
    pooled = []
    for g, w in enumerate(POOL_WINDOWS):
        s = p[:, g * B_GROUP_DIM:(g + 1) * B_GROUP_DIM]
        for k in range(1, w):
            c0 = (POOL_BUF - k) * D + g * B_GROUP_DIM
            s = s + st_ref[:, c0:c0 + B_GROUP_DIM]
        cnt = float(min(w, PAST_LEN + 1))
        pooled.append(s / cnt - p[:, g * B_GROUP_DIM:(g + 1) * B_GROUP_DIM])
    y = _mix_b_tail(pooled, wgrp_ref, bgrp_ref, scale_ref, wout_ref)
    x1_ref[...] = x + gt_ref[...] * y
    nst_ref[:, 0:(POOL_BUF - 1) * D] = st_ref[:, D:POOL_BUF * D]
    nst_ref[:, (POOL_BUF - 1) * D:POOL_BUF * D] = p


def _mixer_b(x_all, mod_p, mod_s, ng, layer, lb, state, w_in, w_grp, b_grp, scale, w_out):
    gd = B_GROUP_DIM
    common = [
        _layer_block((D, D), lb), _layer_block((len(POOL_WINDOWS), gd, gd), lb),
        _layer_block((len(POOL_WINDOWS), 1, gd), lb), _layer_block((1, D), lb),
        _layer_block((D, D), lb),
    ]
    x_all, st_p = pl.pallas_call(
        _mix_b_prompt_kernel,
        grid=(N_PROMPT_TILES,),
        in_specs=[pl.BlockSpec((TM, D), lambda i: (i, 0))]
        + _mod_specs_prompt(layer, (0, 1, 2)) + [_full((1, D))] + common,
        out_specs=[
            pl.BlockSpec((TM, D), lambda i: (i, 0)),
            pl.BlockSpec((None, POOL_BUF, D), lambda i: (i // TILES_PER_SEQ, 0, 0)),
        ],
        out_shape=[
            jax.ShapeDtypeStruct((T_PAD, D), F32),
            jax.ShapeDtypeStruct((BATCH, POOL_BUF, D), F32),
        ],
        scratch_shapes=[pltpu.VMEM((TM + POOL_CARRY, D), F32)],
        input_output_aliases={0: 0},
        compiler_params=_cparams("arbitrary"),
        name="mixer_b_prompt",
    )(x_all, mod_p, mod_p, mod_p, ng, w_in, w_grp, b_grp, scale, w_out)

    r = SAMPLE_ROWS
    sblk = T_PROMPT // r
    st2 = state[lb].reshape(DEC_BATCH, POOL_BUF * D)
    x_all, st_s = pl.pallas_call(
        _mix_b_sample_kernel,
        grid=(DEC_BATCH // r,),
        in_specs=[pl.BlockSpec((r, D), lambda i: (sblk + i, 0))]
        + _mod_specs_sample(layer, (0, 1, 2), r) + [_full((1, D))]
        + [pl.BlockSpec((r, POOL_BUF * D), lambda i: (i, 0))] + common,
        out_specs=[
            pl.BlockSpec((r, D), lambda i: (sblk + i, 0)),
            pl.BlockSpec((r, POOL_BUF * D), lambda i: (i, 0)),
        ],
        out_shape=[
            jax.ShapeDtypeStruct((T_PAD, D), F32),
            jax.ShapeDtypeStruct((DEC_BATCH, POOL_BUF * D), F32),
        ],
        input_output_aliases={0: 0},
        compiler_params=_cparams("arbitrary"),
        name="mixer_b_sample",
    )(x_all, mod_s, mod_s, mod_s, ng, st2, w_in, w_grp, b_grp, scale, w_out)
    return x_all, st_p, st_s.reshape(DEC_BATCH, POOL_BUF, D)


def _mix_c_glu(x, sh, sc, ng, win_ref, bin_ref):
    h = (_rms(x, ng) * (1.0 + sc) + sh).astype(BF16)
    ag = _dot(h, win_ref[...]) + bin_ref[...]
    return ag[:, :D] * jax.nn.sigmoid(ag[:, D:])


def _mix_c_tail(conv, lng_ref, lnb_ref, wout_ref, bout_ref):
    z = _ln(conv, lng_ref[...], lnb_ref[...])
    z = z * jax.nn.sigmoid(z)
    return _dot(z.astype(BF16), wout_ref[...]) + bout_ref[...]


def _mix_c_prompt_kernel(x_ref, sh_ref, sc_ref, gt_ref, ng_ref, win_ref, bin_ref, wdw_ref,
                         bdw_ref, lng_ref, lnb_ref, wout_ref, bout_ref, x1_ref, st_ref,
                         full_ref, conv_ref, *shift_refs):
    i = pl.program_id(0)
    j = i % TILES_PER_SEQ
    x = x_ref[...]
    glu = _mix_c_glu(x, sh_ref[...], sc_ref[...], ng_ref[...], win_ref, bin_ref)

    @pl.when(j == 0)
    def _():
        full_ref[0:CONV_CARRY, :] = jnp.zeros((CONV_CARRY, D), F32)

    full_ref[CONV_CARRY:CONV_CARRY + TM, :] = glu
    off = CONV_CARRY - CONV_BUF
    span = TM + (off + CONV_BUF) // SUBLANES * SUBLANES - SUBLANES
    for c in range(D // LANES):
        cols = slice(c * LANES, (c + 1) * LANES)
        shift_ref = shift_refs[c % len(shift_refs)]
        for r in range(1, SUBLANES):
            shift_ref[r - 1] = full_ref[pl.ds(r, span), cols]
        for rb in range(TM // CONV_ROWS):
            acc = None
            for k in range(CONV_WIDTH):
                q, r = divmod(off + k, SUBLANES)
                start = q * SUBLANES + rb * CONV_ROWS
                if r == 0:
                    src = full_ref[pl.ds(start, CONV_ROWS), cols]
                else:
                    src = shift_ref[r - 1, pl.ds(start, CONV_ROWS), :]
                term = src * wdw_ref[k:k + 1, cols]
                acc = term if acc is None else acc + term
            conv_ref[pl.ds(rb * CONV_ROWS, CONV_ROWS), cols] = acc + bdw_ref[:, cols]
    y = _mix_c_tail(conv_ref[...], lng_ref, lnb_ref, wout_ref, bout_ref)
    x1_ref[...] = x + gt_ref[...] * y

    @pl.when(j == TILES_PER_SEQ - 1)
    def _():
        st_ref[...] = full_ref[pl.ds(CONV_CARRY + TM - CONV_BUF, CONV_BUF), :]

    full_ref[0:CONV_CARRY, :] = full_ref[pl.ds(TM, CONV_CARRY), :]


def _mix_c_sample_kernel(x_ref, sh_ref, sc_ref, gt_ref, ng_ref, st_ref, win_ref, bin_ref,
                         wdw_ref, bdw_ref, lng_ref, lnb_ref, wout_ref, bout_ref, x1_ref, nst_ref):
    x = x_ref[...]
    glu = _mix_c_glu(x, sh_ref[...], sc_ref[...], ng_ref[...], win_ref, bin_ref)
    acc = glu * wdw_ref[CONV_BUF:CONV_BUF + 1, :]
    for k in range(CONV_BUF):
        acc = acc + st_ref[:, k * D:(k + 1) * D] * wdw_ref[k:k + 1, :]
    y = _mix_c_tail(acc + bdw_ref[...], lng_ref, lnb_ref, wout_ref, bout_ref)
    x1_ref[...] = x + gt_ref[...] * y
    nst_ref[:, 0:(CONV_BUF - 1) * D] = st_ref[:, D:CONV_BUF * D]
    nst_ref[:, (CONV_BUF - 1) * D:CONV_BUF * D] = glu


def _mixer_c(x_all, mod_p, mod_s, ng, layer, lc, state, w_in, b_in, w_dw, b_dw, ln_g, ln_b,
             w_out, b_out):
    common = [
        _layer_block((D, 2 * D), lc), _layer_block((1, 2 * D), lc),
        _layer_block((CONV_WIDTH, D), lc), _layer_block((1, D), lc), _layer_block((1, D), lc),
        _layer_block((1, D), lc), _layer_block((D, D), lc), _layer_block((1, D), lc),
    ]
    x_all, st_p = pl.pallas_call(
        _mix_c_prompt_kernel,
        grid=(N_PROMPT_TILES,),
        in_specs=[pl.BlockSpec((TM, D), lambda i: (i, 0))]
        + _mod_specs_prompt(layer, (0, 1, 2)) + [_full((1, D))] + common,
        out_specs=[
            pl.BlockSpec((TM, D), lambda i: (i, 0)),
            pl.BlockSpec((None, CONV_BUF, D), lambda i: (i // TILES_PER_SEQ, 0, 0)),
        ],
        out_shape=[
            jax.ShapeDtypeStruct((T_PAD, D), F32),
            jax.ShapeDtypeStruct((BATCH, CONV_BUF, D), F32),
        ],
        scratch_shapes=[pltpu.VMEM((TM + CONV_CARRY, D), F32), pltpu.VMEM((TM, D), F32)]
        + [pltpu.VMEM((SUBLANES - 1, TM + CONV_CARRY - SUBLANES, LANES), F32)] * 2,
        input_output_aliases={0: 0},
        compiler_params=_cparams("arbitrary"),
        name="mixer_c_prompt",
    )(x_all, mod_p, mod_p, mod_p, ng, w_in, b_in, w_dw, b_dw, ln_g, ln_b, w_out, b_out)

    r = SAMPLE_ROWS
    sblk = T_PROMPT // r
    st2 = state[lc].reshape(DEC_BATCH, CONV_BUF * D)
    x_all, st_s = pl.pallas_call(
        _mix_c_sample_kernel,
        grid=(DEC_BATCH // r,),
        in_specs=[pl.BlockSpec((r, D), lambda i: (sblk + i, 0))]
        + _mod_specs_sample(layer, (0, 1, 2), r) + [_full((1, D))]
        + [pl.BlockSpec((r, CONV_BUF * D), lambda i: (i, 0))] + common,
        out_specs=[
            pl.BlockSpec((r, D), lambda i: (sblk + i, 0)),
            pl.BlockSpec((r, CONV_BUF * D), lambda i: (i, 0)),
        ],
        out_shape=[
            jax.ShapeDtypeStruct((T_PAD, D), F32),
            jax.ShapeDtypeStruct((DEC_BATCH, CONV_BUF * D), F32),
        ],
        input_output_aliases={0: 0},
        compiler_params=_cparams("arbitrary"),
        name="mixer_c_sample",
    )(x_all, mod_s, mod_s, mod_s, ng, st2, w_in, b_in, w_dw, b_dw, ln_g, ln_b, w_out, b_out)
    return x_all, st_p, st_s.reshape(DEC_BATCH, CONV_BUF, D)


def _tile_mod(i, p_ref, s_ref):
    s_rows = jnp.concatenate([s_ref[...], jnp.zeros((TM - DEC_BATCH, D), F32)], axis=0)
    return jnp.where(i >= N_PROMPT_TILES, s_rows, p_ref[...])


def _unified_mod_specs(layer, chunks):
    specs = []
    for c in chunks:
        specs.append(pl.BlockSpec(
            (None, None, 1, D),
            lambda i, c=c: (layer, jnp.minimum(i // TILES_PER_SEQ, BATCH - 1), 0, c)))
        specs.append(pl.BlockSpec((None, DEC_BATCH, D), lambda i, c=c: (layer, 0, c)))
    return specs


def _router_kernel(x_ref, shp_ref, shs_ref, scp_ref, scs_ref, ng_ref, wr_ref, br_ref,
                   h2_ref, meta_ref, wts_ref, cnt_ref, carry_ref):
    i = pl.program_id(0)

    @pl.when(i == 0)
    def _():
        carry_ref[...] = jnp.zeros((1, LANES), F32)

    sh = _tile_mod(i, shp_ref, shs_ref)
    sc = _tile_mod(i, scp_ref, scs_ref)
    h2 = _rms(x_ref[...], ng_ref[...]) * (1.0 + sc) + sh
    h2_ref[...] = _pack_bf16_pairs(h2)
    lg = _dot(h2.astype(BF16), wr_ref[...]) + br_ref[...]
    lane = lax.broadcasted_iota(jnp.int32, (TM, LANES), 1)
    neg = -jnp.inf
    is_g = lane < N_GROUPS
    gm = jnp.where(is_g, lg, neg)
    gmax = jnp.max(gm, axis=1, keepdims=True)
    lane_f = lane.astype(F32)
    far = float(LANES)
    gsel = jnp.min(jnp.where(gm == gmax, lane_f, far), axis=1, keepdims=True).astype(jnp.int32)
    gsum = jnp.sum(jnp.where(is_g, jnp.exp(lg - gmax), 0.0), axis=1, keepdims=True)
    g_w = 1.0 / gsum
    e_lane = lane - EXPERT_LANE0
    in_grp = (e_lane >= 0) & (e_lane < N_EXPERTS) & (
        lax.shift_right_arithmetic(e_lane, EXPERTS_PER_GROUP.bit_length() - 1) == gsel)
    em = jnp.where(in_grp, lg, neg)
    m1 = jnp.max(em, axis=1, keepdims=True)
    i1 = jnp.min(jnp.where(em == m1, lane_f, far), axis=1, keepdims=True).astype(jnp.int32)
    em2 = jnp.where(lane == i1, neg, em)
    m2 = jnp.max(em2, axis=1, keepdims=True)
    i2 = jnp.min(jnp.where(em2 == m2, lane_f, far), axis=1, keepdims=True).astype(jnp.int32)
    e2 = jnp.exp(m2 - m1)
    den = 1.0 + e2
    w1 = (1.0 / den) * g_w
    w2 = (e2 / den) * g_w

    hit1 = lane == i1
    hit2 = lane == i2
    assign = jnp.where(hit1 | hit2, 1.0, 0.0).astype(BF16)
    r = lax.broadcasted_iota(jnp.int32, (TM, TM), 0)
    c = lax.broadcasted_iota(jnp.int32, (TM, TM), 1)
    before = (c < r).astype(BF16)
    seen = _dot(before, assign) + carry_ref[...]
    rank1 = jnp.sum(jnp.where(hit1, seen, 0.0), axis=1, keepdims=True)
    rank2 = jnp.sum(jnp.where(hit2, seen, 0.0), axis=1, keepdims=True)
    carry_ref[...] = carry_ref[...] + jnp.sum(assign.astype(F32), axis=0, keepdims=True)
    cnt_ref[...] = carry_ref[...]

    ex1 = (i1 - EXPERT_LANE0).astype(F32)
    ex2 = (i2 - EXPERT_LANE0).astype(F32)
    fields = jnp.where(lane == 0, ex1, jnp.where(lane == 1, ex2,
                       jnp.where(lane == 2, rank1, jnp.where(lane == 3, rank2, 0.0))))
    meta_ref[...] = fields.T[0:META_ROWS, :]
    wts_ref[...] = jnp.where(lane == 0, w1, jnp.where(lane == 1, w2, 0.0))


def _router(x_all, mod_p, mod_s, ng, layer, w_r, b_r):
    return pl.pallas_call(
        _router_kernel,
        grid=(N_TILES,),
        in_specs=[pl.BlockSpec((TM, D), lambda i: (i, 0))]
        + _unified_mod_specs(layer, (3, 4))
        + [_full((1, D)), _full((D, LANES)), _full((1, LANES))],
        out_specs=[
            pl.BlockSpec((TM, D // 2), lambda i: (i, 0)),
            pl.BlockSpec((META_ROWS, TM), lambda i: (0, i)),
            pl.BlockSpec((TM, LANES), lambda i: (i, 0)),
            pl.BlockSpec((1, LANES), lambda i: (0, 0)),
        ],
        out_shape=[
            jax.ShapeDtypeStruct((T_PAD, D // 2), jnp.int32),
            jax.ShapeDtypeStruct((META_ROWS, T_PAD), F32),
            jax.ShapeDtypeStruct((T_PAD, LANES), F32),
            jax.ShapeDtypeStruct((1, LANES), F32),
        ],
        scratch_shapes=[pltpu.VMEM((1, LANES), F32)],
        compiler_params=_cparams("arbitrary"),
        name="router",
    )(x_all, mod_p, mod_s, mod_p, mod_s, ng, w_r, b_r)


SC_INDEX_MAX = 128
SC_ROWS_BYTES = 256 * 1024


def _gather_chunk(per_worker, row_bytes):
    top = min(SC_INDEX_MAX, SC_ROWS_BYTES // row_bytes) // SUBLANES * SUBLANES
    for ch in range(top, SUBLANES - 1, -SUBLANES):
        if per_worker % ch == 0:
            return ch
    raise ValueError(per_worker)


def _sc_gather(table, idx):
    nc, nw = SC_CORES, SC_CORES * SC_SUBCORES
    m = idx.shape[0]
    per_w = m // nw
    assert per_w * nw == m
    width = table.shape[1]
    ch = _gather_chunk(per_w, width * table.dtype.itemsize)
    mesh = plsc.VectorSubcoreMesh(core_axis_name="c", subcore_axis_name="s")

    @functools.partial(
        pl.kernel,
        out_type=jax.ShapeDtypeStruct((m, width), table.dtype),
        mesh=mesh,
        scratch_types=[
            pltpu.VMEM((ch,), jnp.int32),
            pltpu.VMEM((ch, width), table.dtype),
            pltpu.SemaphoreType.DMA,
        ],
    )
    def gather_kernel(t_hbm, i_hbm, o_hbm, idx_v, rows_v, sem):
        wid = lax.axis_index("s") * nc + lax.axis_index("c")
        base = wid * per_w

        @pl.loop(0, per_w // ch)
        def _(j):
            off = pl.multiple_of(base + j * ch, 8)
            pltpu.sync_copy(i_hbm.at[pl.ds(off, ch)], idx_v)
            pltpu.async_copy(t_hbm.at[idx_v], rows_v, sem).wait()
            pltpu.sync_copy(rows_v, o_hbm.at[pl.ds(off, ch)])

    return gather_kernel(table, idx)


def _sc_scatter2(rows, dest0, dest1, n_out):
    nc, nw = SC_CORES, SC_CORES * SC_SUBCORES
    n, width = rows.shape
    per_w = n // nw
    assert per_w * nw == n
    ch = _gather_chunk(per_w, width * rows.dtype.itemsize)
    mesh = plsc.VectorSubcoreMesh(core_axis_name="c", subcore_axis_name="s")

    @functools.partial(
        pl.kernel,
        out_type=jax.ShapeDtypeStruct((n_out, width), rows.dtype),
        mesh=mesh,
        scratch_types=[
            pltpu.VMEM((ch,), jnp.int32),
            pltpu.VMEM((ch,), jnp.int32),
            pltpu.VMEM((ch, width), rows.dtype),
            pltpu.SemaphoreType.DMA,
            pltpu.SemaphoreType.DMA,
        ],
    )
    def scatter_kernel(r_hbm, d0_hbm, d1_hbm, o_hbm, i0_v, i1_v, rows_v, sem0, sem1):
        wid = lax.axis_index("s") * nc + lax.axis_index("c")
        base = wid * per_w

        @pl.loop(0, per_w // ch)
        def _(j):
            off = pl.multiple_of(base + j * ch, 8)
            pltpu.sync_copy(d0_hbm.at[pl.ds(off, ch)], i0_v)
            pltpu.sync_copy(d1_hbm.at[pl.ds(off, ch)], i1_v)
            pltpu.sync_copy(r_hbm.at[pl.ds(off, ch)], rows_v)
            c0 = pltpu.async_copy(rows_v, o_hbm.at[i0_v], sem0)
            c1 = pltpu.async_copy(rows_v, o_hbm.at[i1_v], sem1)
            c0.wait()
            c1.wait()

    return scatter_kernel(rows, dest0, dest1)


def _expert_kernel(be_ref, first_ref, nvalid_ref, next_ref, nused_ref, x_ref, wg_hbm, wu_hbm,
                   wd_hbm, y_ref, wgf_ref, wuf_ref, wdf_ref, wgb_ref, wub_ref, wdb_ref, sems,
                   *, layer):
    b = pl.program_id(0)

    def weight_copies(e):
        return (
            pltpu.make_async_copy(wg_hbm.at[layer, e], wgf_ref, sems.at[0]),
            pltpu.make_async_copy(wu_hbm.at[layer, e], wuf_ref, sems.at[1]),
            pltpu.make_async_copy(wd_hbm.at[layer, e], wdf_ref, sems.at[2]),
        )

    @pl.when(b == 0)
    def _():
        for cp in weight_copies(be_ref[0]):
            cp.start()

    @pl.when(b < nused_ref[0])
    def _():
        @pl.when(first_ref[b] == 1)
        def _():
            for cp in weight_copies(be_ref[b]):
                cp.wait()
            wgb_ref[...] = wgf_ref[...].astype(BF16)
            wub_ref[...] = wuf_ref[...].astype(BF16)
            wdb_ref[...] = wdf_ref[...].astype(BF16)

            @pl.when(next_ref[b] >= 0)
            def _():
                for cp in weight_copies(next_ref[b]):
                    cp.start()

        live = lax.broadcasted_iota(jnp.int32, (BM, 1), 0) < nvalid_ref[b]
        x = _unpack_bf16_pairs(jnp.where(live, x_ref[...], 0), BF16)
        g = _dot(x, wgb_ref[...])
        u = _dot(x, wub_ref[...])
        hmid = (g * jax.nn.sigmoid(g)) * u
        y_ref[...] = _pack_bf16_pairs(_dot(hmid.astype(BF16), wdb_ref[...]))

    @pl.when(b >= nused_ref[0])
    def _():
        y_ref[...] = jnp.zeros((BM, D // 2), jnp.int32)


def _experts(x_sorted, block_expert, first, nvalid, next_expert, nused, layer, w_gate, w_up,
             w_down):
    def row_map(b, be, fi, nv, nx, nu):
        return (jnp.minimum(b, nu[0] - 1), 0)

    def out_map(b, be, fi, nv, nx, nu):
        return (b, 0)

    grid_spec = pltpu.PrefetchScalarGridSpec(
        num_scalar_prefetch=5,
        grid=(N_BLOCKS,),
        in_specs=[
            pl.BlockSpec((BM, D // 2), row_map),
            pl.BlockSpec(memory_space=pl.ANY),
            pl.BlockSpec(memory_space=pl.ANY),
            pl.BlockSpec(memory_space=pl.ANY),
        ],
        out_specs=pl.BlockSpec((BM, D // 2), out_map),
        scratch_shapes=[
            pltpu.VMEM((D, D_EXPERT), F32),
            pltpu.VMEM((D, D_EXPERT), F32),
            pltpu.VMEM((D_EXPERT, D), F32),
            pltpu.VMEM((D, D_EXPERT), BF16),
            pltpu.VMEM((D, D_EXPERT), BF16),
            pltpu.VMEM((D_EXPERT, D), BF16),
            pltpu.SemaphoreType.DMA((3,)),
        ],
    )
    return pl.pallas_call(
        functools.partial(_expert_kernel, layer=layer),
        grid_spec=grid_spec,
        out_shape=jax.ShapeDtypeStruct((P_ROWS, D // 2), jnp.int32),
        compiler_params=_cparams("arbitrary"),
        name="experts",
    )(block_expert, first, nvalid, next_expert, nused, x_sorted, w_gate, w_up, w_down)


def _combine_kernel(x_ref, g0_ref, g1_ref, wts_ref, gtp_ref, gts_ref, o_ref):
    i = pl.program_id(0)
    gt = _tile_mod(i, gtp_ref, gts_ref)
    w = wts_ref[...]
    y0 = _unpack_bf16_pairs(g0_ref[...], F32)
    y1 = _unpack_bf16_pairs(g1_ref[...], F32)
    o_ref[...] = x_ref[...] + gt * (w[:, 0:1] * y0 + w[:, 1:2] * y1)


def _combine(x_all, gathered, wts, mod_p, mod_s, layer):
    return pl.pallas_call(
        _combine_kernel,
        grid=(N_TILES,),
        in_specs=[
            pl.BlockSpec((TM, D), lambda i: (i, 0)),
            pl.BlockSpec((TM, D // 2), lambda i: (i, 0)),
            pl.BlockSpec((TM, D // 2), lambda i: (i + N_TILES, 0)),
            pl.BlockSpec((TM, LANES), lambda i: (i, 0)),
        ] + _unified_mod_specs(layer, (5,)),
        out_specs=pl.BlockSpec((TM, D), lambda i: (i, 0)),
        out_shape=jax.ShapeDtypeStruct((T_PAD, D), F32),
        input_output_aliases={0: 0},
        compiler_params=_cparams("arbitrary"),
        name="combine",
    )(x_all, gathered, gathered, wts, mod_p, mod_s)


def _moe(x_all, mod_p, mod_s, ng, layer, w_r, b_r, w_gate, w_up, w_down):
    h2, meta, wts, cnt = _router(x_all, mod_p, mod_s, ng, layer, w_r, b_r)
    counts = cnt[0, EXPERT_LANE0:EXPERT_LANE0 + N_EXPERTS].astype(jnp.int32)
    padded = (counts + BM - 1) // BM * BM
    pend = jnp.cumsum(padded)
    pstart = pend - padded
    expert = meta[0:2].astype(jnp.int32)
    dest = meta[2:4].astype(jnp.int32)
    for e in range(N_EXPERTS):
        dest = dest + jnp.where(expert == e, pstart[e], 0)
    block0 = jnp.arange(N_BLOCKS, dtype=jnp.int32) * BM
    block_expert = jnp.minimum(
        jnp.sum((pend[None, :] <= block0[:, None]).astype(jnp.int32), axis=1), N_EXPERTS - 1)
    of_block = block_expert[:, None] == jnp.arange(N_EXPERTS, dtype=jnp.int32)[None, :]
    seg_end = jnp.sum(jnp.where(of_block, (pstart + counts)[None, :], 0), axis=1)
    nvalid = jnp.clip(seg_end - block0, 0, BM).astype(jnp.int32)
    nused = (pend[-1] // BM).astype(jnp.int32).reshape(1)
    first = jnp.concatenate(
        [jnp.ones((1,), jnp.int32), (block_expert[1:] != block_expert[:-1]).astype(jnp.int32)])
    ids = jnp.arange(N_EXPERTS, dtype=jnp.int32)
    later_used = (ids[None, :] > ids[:, None]) & (counts[None, :] > 0)
    nxt = jnp.min(jnp.where(later_used, ids[None, :], N_EXPERTS), axis=1)
    nxt = jnp.where(nxt == N_EXPERTS, -1, nxt)
    next_expert = jnp.sum(jnp.where(of_block, nxt[None, :], 0), axis=1).astype(jnp.int32)

    x_sorted = _sc_scatter2(h2, dest[0], dest[1], P_ROWS)
    y_sorted = _experts(x_sorted, block_expert, first, nvalid, next_expert, nused, layer,
                        w_gate, w_up, w_down)
    gathered = _sc_gather(y_sorted, dest.reshape(-1))
    return _combine(x_all, gathered, wts, mod_p, mod_s, layer)


def _final_kernel(x_ref, g_ref, yp_ref, ys_ref):
    i = pl.program_id(0)
    y = _rms(x_ref[...], g_ref[...])

    @pl.when(i < N_PROMPT_TILES)
    def _():
        yp_ref[...] = y

    @pl.when(i == N_PROMPT_TILES)
    def _():
        ys_ref[...] = y[0:DEC_BATCH, :]


def _final_norm(x_all, g):
    return pl.pallas_call(
        _final_kernel,
        grid=(N_PROMPT_TILES + 1,),
        in_specs=[pl.BlockSpec((TM, D), lambda i: (i, 0)), _full((1, D))],
        out_specs=[
            pl.BlockSpec((TM, D), lambda i: (jnp.minimum(i, N_PROMPT_TILES - 1), 0)),
            pl.BlockSpec((DEC_BATCH, D), lambda i: (0, 0)),
        ],
        out_shape=[
            jax.ShapeDtypeStruct((T_PROMPT, D), F32),
            jax.ShapeDtypeStruct((DEC_BATCH, D), F32),
        ],
        compiler_params=_cparams("arbitrary"),
        name="final_norm",
    )(x_all, g)


def kernel(x_prompt, x_sample, c_prompt, c_sample, state_pool, state_conv, ada_w, ada_b, norm_g, final_g, a_w_in, a_ln_g, a_ln_b, a_w_s, a_b_s, a_w_out, b_w_in, b_w_grp, b_b_grp, b_scale, b_w_out, c_w_in, c_b_in, c_w_dw, c_b_dw, c_ln_g, c_ln_b, c_w_out, c_b_out, moe_w_grp, moe_b_grp, moe_w_exp, moe_b_exp, moe_w_gate, moe_w_up, moe_w_down):
    x_all = jnp.concatenate([
        x_prompt.reshape(T_PROMPT, D), x_sample.reshape(DEC_BATCH, D),
        jnp.zeros((T_PAD - T_ALL, D), F32)], axis=0)
    mods = _ada_mods(jnp.concatenate([c_prompt, c_sample], axis=0), ada_w, ada_b)
    mod_p = mods[:, :BATCH].reshape(DEPTH, BATCH, 1, 6 * D)
    mod_s = mods[:, BATCH:]

    a_w_in_b, a_w_out_b = a_w_in.astype(BF16), a_w_out.astype(BF16)
    b_w_in_b, b_w_grp_b, b_w_out_b = b_w_in.astype(BF16), b_w_grp.astype(BF16), b_w_out.astype(BF16)
    c_w_in_b, c_w_out_b = c_w_in.astype(BF16), c_w_out.astype(BF16)
    w_r = jnp.pad(jnp.concatenate([moe_w_grp, moe_w_exp], axis=2),
                  ((0, 0), (0, 0), (0, LANES - N_GROUPS - N_EXPERTS))).astype(BF16)
    b_r = jnp.pad(jnp.concatenate([moe_b_grp, moe_b_exp], axis=1),
                  ((0, 0), (0, LANES - N_GROUPS - N_EXPERTS)))

    new_a_p, new_a_s, new_b_p, new_b_s, new_c_p, new_c_s = [], [], [], [], [], []
    ia = ib = ic = 0
    for layer in range(DEPTH):
        ng1 = norm_g[layer, 0].reshape(1, D)
        ng2 = norm_g[layer, 1].reshape(1, D)
        kind = layer % 3
        if kind == 0:
            x_all, st_p, st_s = _mixer_a(
                x_all, mod_p, mod_s, ng1, layer, ia, a_w_in_b, a_ln_g.reshape(-1, 1, D),
                a_ln_b.reshape(-1, 1, D), a_w_s, a_b_s, a_w_out_b)
            new_a_p.append(st_p)
            new_a_s.append(st_s)
            ia += 1
        elif kind == 1:
            x_all, st_p, st_s = _mixer_b(
                x_all, mod_p, mod_s, ng1, layer, ib, state_pool, b_w_in_b, b_w_grp_b,
                b_b_grp.reshape(-1, len(POOL_WINDOWS), 1, B_GROUP_DIM),
                b_scale.reshape(-1, 1, D), b_w_out_b)
            new_b_p.append(st_p)
            new_b_s.append(st_s)
            ib += 1
        else:
            x_all, st_p, st_s = _mixer_c(
                x_all, mod_p, mod_s, ng1, layer, ic, state_conv, c_w_in_b,
                c_b_in.reshape(-1, 1, 2 * D), c_w_dw, c_b_dw.reshape(-1, 1, D),
                c_ln_g.reshape(-1, 1, D), c_ln_b.reshape(-1, 1, D), c_w_out_b,
                c_b_out.reshape(-1, 1, D))
            new_c_p.append(st_p)
            new_c_s.append(st_s)
            ic += 1
        x_all = _moe(x_all, mod_p, mod_s, ng2, layer, w_r[layer], b_r[layer].reshape(1, LANES),
                     moe_w_gate, moe_w_up, moe_w_down)

    y_p, y_s = _final_norm(x_all, final_g.reshape(1, D))
    return (y_p.reshape(BATCH, SEQ, D), y_s.reshape(DEC_BATCH, 1, D),
            jnp.stack(new_a_p), jnp.stack(new_a_s), jnp.stack(new_b_p), jnp.stack(new_b_s),
            jnp.stack(new_c_p), jnp.stack(new_c_s))
```

```python
import functools

import jax
import jax.numpy as jnp
from jax import lax
from jax.experimental import pallas as pl
from jax.experimental.pallas import tpu as pltpu
from jax.experimental.pallas import tpu_sc as plsc

F32 = jnp.float32
BF16 = jnp.bfloat16

D = 1024
BATCH = 8
SEQ = 2048
DEPTH = 4
DEC_BATCH = 128
PAST_LEN = 16384
CHUNK = 128
A_HEADS = 8
POOL_WINDOWS = (2, 4, 8, 16)
B_GROUP_DIM = D // len(POOL_WINDOWS)
POOL_BUF = max(POOL_WINDOWS) - 1
CONV_WIDTH = 31
CONV_BUF = CONV_WIDTH - 1
N_GROUPS = 4
EXPERTS_PER_GROUP = 8
N_EXPERTS = N_GROUPS * EXPERTS_PER_GROUP
D_EXPERT = D // 2
EPS = 1e-6

LANES = 128
SUBLANES = 8
CONV_ROWS = 128
SC_CORES = 2
SC_SUBCORES = 16
TM = 512
TILES_PER_SEQ = SEQ // TM
T_PROMPT = BATCH * SEQ
T_ALL = T_PROMPT + DEC_BATCH
N_TILES = -(-T_ALL // TM)
T_PAD = N_TILES * TM
N_PROMPT_TILES = T_PROMPT // TM
SAMPLE_ROWS = 32
BM = 512
N_BLOCKS = -(-(2 * T_PAD) // BM) + N_EXPERTS
P_ROWS = N_BLOCKS * BM
META_ROWS = 8
POOL_CARRY = 16
CONV_CARRY = 32
EXPERT_LANE0 = N_GROUPS
VMEM_LIMIT = 56 * 1024 * 1024


def _cparams(*sem):
    return pltpu.CompilerParams(dimension_semantics=sem, vmem_limit_bytes=VMEM_LIMIT)


def _rms(x, g):
    return x * lax.rsqrt(jnp.mean(x * x, axis=-1, keepdims=True) + EPS) * g


def _ln(x, g, b):
    mu = jnp.mean(x, axis=-1, keepdims=True)
    xc = x - mu
    var = jnp.mean(xc * xc, axis=-1, keepdims=True)
    return xc * lax.rsqrt(var + EPS) * g + b


def _dot(a, b):
    return jnp.dot(a, b, preferred_element_type=F32)


def _pack_bf16_pairs(v):
    w = v.shape[1] // 2
    lo = lax.bitcast_convert_type(v[:, :w].astype(BF16).astype(F32), jnp.uint32)
    hi = lax.bitcast_convert_type(v[:, w:].astype(BF16).astype(F32), jnp.uint32)
    return lax.bitcast_convert_type(lax.shift_right_logical(lo, jnp.uint32(16)) | hi, jnp.int32)


def _unpack_bf16_pairs(words, dtype):
    u = lax.bitcast_convert_type(words, jnp.uint32)
    lo = lax.bitcast_convert_type(lax.shift_left(u, jnp.uint32(16)), F32)
    hi = lax.bitcast_convert_type(u & jnp.uint32(0xFFFF0000), F32)
    return jnp.concatenate([lo.astype(dtype), hi.astype(dtype)], axis=1)


def _ada_kernel(c_ref, w_ref, b_ref, o_ref):
    c = c_ref[...]
    s = (c * jax.nn.sigmoid(c)).astype(BF16)
    o_ref[...] = _dot(s, w_ref[...].astype(BF16)) + b_ref[...]


def _ada_mods(c_all, ada_w, ada_b):
    n = c_all.shape[0]
    tn = 2048
    return pl.pallas_call(
        _ada_kernel,
        grid=(DEPTH, 6 * D // tn),
        in_specs=[
            pl.BlockSpec((n, D), lambda l, j: (0, 0)),
            pl.BlockSpec((None, D, tn), lambda l, j: (l, 0, j)),
            pl.BlockSpec((None, 1, tn), lambda l, j: (l, 0, j)),
        ],
        out_specs=pl.BlockSpec((None, n, tn), lambda l, j: (l, 0, j)),
        out_shape=jax.ShapeDtypeStruct((DEPTH, n, 6 * D), F32),
        compiler_params=_cparams("parallel", "parallel"),
        name="ada_mods",
    )(c_all, ada_w, ada_b.reshape(DEPTH, 1, 6 * D))


def _mod_specs_prompt(layer, chunks):
    return [
        pl.BlockSpec((None, None, 1, D), lambda i, c=c: (layer, i // TILES_PER_SEQ, 0, c))
        for c in chunks
    ]


def _mod_specs_sample(layer, chunks, rows, row_block0=0):
    return [
        pl.BlockSpec((None, rows, D), lambda i, c=c: (layer, i + row_block0, c)) for c in chunks
    ]


def _full(shape):
    nd = len(shape)
    return pl.BlockSpec(shape, lambda i: (0,) * nd)


def _layer_block(shape, layer):
    nd = len(shape)
    return pl.BlockSpec((None,) + shape, lambda i: (layer,) + (0,) * nd)


def _mix_a_front(x, sh, sc, ng, win_ref, lng_ref, lnb_ref):
    h = (_rms(x, ng) * (1.0 + sc) + sh).astype(BF16)
    z = jax.nn.gelu(_dot(h, win_ref[...]))
    u = z[:, :D]
    v = _ln(z[:, D:], lng_ref[...], lnb_ref[...])
    return u, v


def _mix_a_prompt_kernel(x_ref, sh_ref, sc_ref, gt_ref, ng_ref, win_ref, lng_ref, lnb_ref,
                         ws_ref, bs_ref, wout_ref, x1_ref, cv_ref, mixed_ref):
    i = pl.program_id(0)
    x = x_ref[...]
    u, v = _mix_a_front(x, sh_ref[...], sc_ref[...], ng_ref[...], win_ref, lng_ref, lnb_ref)
    vb = v.astype(BF16)
    n_chunks = TM // CHUNK
    row = lax.broadcasted_iota(jnp.int32, (CHUNK, CHUNK), 0)
    col = lax.broadcasted_iota(jnp.int32, (CHUNK, CHUNK), 1)
    tril = row >= col
    hd_dim = D // A_HEADS
    for hd in range(A_HEADS):
        ws = jnp.where(tril, ws_ref[hd], 0.0).astype(BF16)
        cols = slice(hd * hd_dim, (hd + 1) * hd_dim)
        vcat = jnp.concatenate([vb[c * CHUNK:(c + 1) * CHUNK, cols] for c in range(n_chunks)], axis=1)
        m = _dot(ws, vcat)
        for c in range(n_chunks):
            mixed_ref[c * CHUNK:(c + 1) * CHUNK, cols] = m[:, c * hd_dim:(c + 1) * hd_dim] + bs_ref[hd]
    y = _dot((u * mixed_ref[...]).astype(BF16), wout_ref[...])
    x1_ref[...] = x + gt_ref[...] * y

    @pl.when(i % TILES_PER_SEQ == TILES_PER_SEQ - 1)
    def _():
        cv_ref[...] = v[TM - CHUNK:, :]


def _mix_a_sample_kernel(x_ref, sh_ref, sc_ref, gt_ref, ng_ref, win_ref, lng_ref, lnb_ref,
                         wsd_ref, bsd_ref, wout_ref, x1_ref, cv_ref):
    x = x_ref[...]
    u, v = _mix_a_front(x, sh_ref[...], sc_ref[...], ng_ref[...], win_ref, lng_ref, lnb_ref)
    mixed = wsd_ref[...] * v + bsd_ref[...]
    y = _dot((u * mixed).astype(BF16), wout_ref[...])
    x1_ref[...] = x + gt_ref[...] * y
    cv_ref[...] = v


def _mixer_a(x_all, mod_p, mod_s, ng, layer, la, w_in, ln_g, ln_b, w_s, b_s, w_out):
    bs_b = jnp.broadcast_to(b_s[la][:, :, None], (A_HEADS, CHUNK, D // A_HEADS))
    common = [
        _layer_block((D, 2 * D), la), _layer_block((1, D), la), _layer_block((1, D), la),
    ]
    x_all, cv_p = pl.pallas_call(
        _mix_a_prompt_kernel,
        grid=(N_PROMPT_TILES,),
        in_specs=[pl.BlockSpec((TM, D), lambda i: (i, 0))]
        + _mod_specs_prompt(layer, (0, 1, 2))
        + [_full((1, D))] + common
        + [_layer_block((A_HEADS, CHUNK, CHUNK), la), _full((A_HEADS, CHUNK, D // A_HEADS)),
           _layer_block((D, D), la)],
        out_specs=[
            pl.BlockSpec((TM, D), lambda i: (i, 0)),
            pl.BlockSpec((None, CHUNK, D), lambda i: (i // TILES_PER_SEQ, 0, 0)),
        ],
        out_shape=[
            jax.ShapeDtypeStruct((T_PAD, D), F32),
            jax.ShapeDtypeStruct((BATCH, CHUNK, D), F32),
        ],
        scratch_shapes=[pltpu.VMEM((TM, D), F32)],
        input_output_aliases={0: 0},
        compiler_params=_cparams("arbitrary"),
        name="mixer_a_prompt",
    )(x_all, mod_p, mod_p, mod_p, ng, w_in, ln_g, ln_b, w_s, bs_b, w_out)

    wsd = jnp.repeat(w_s[la, :, 0, 0], D // A_HEADS).reshape(1, D)
    bsd = jnp.repeat(b_s[la, :, 0], D // A_HEADS).reshape(1, D)
    sblk = T_PROMPT // DEC_BATCH
    x_all, cv_s = pl.pallas_call(
        _mix_a_sample_kernel,
        grid=(1,),
        in_specs=[pl.BlockSpec((DEC_BATCH, D), lambda i: (sblk, 0))]
        + _mod_specs_sample(layer, (0, 1, 2), DEC_BATCH)
        + [_full((1, D))] + common
        + [_full((1, D)), _full((1, D)), _layer_block((D, D), la)],
        out_specs=[
            pl.BlockSpec((DEC_BATCH, D), lambda i: (sblk, 0)),
            pl.BlockSpec((DEC_BATCH, D), lambda i: (0, 0)),
        ],
        out_shape=[
            jax.ShapeDtypeStruct((T_PAD, D), F32),
            jax.ShapeDtypeStruct((DEC_BATCH, D), F32),
        ],
        input_output_aliases={0: 0},
        compiler_params=_cparams("arbitrary"),
        name="mixer_a_sample",
    )(x_all, mod_s, mod_s, mod_s, ng, w_in, ln_g, ln_b, wsd, bsd, w_out)
    return x_all, cv_p, cv_s.reshape(DEC_BATCH, 1, D)


def _mix_b_tail(pooled_groups, wgrp_ref, bgrp_ref, scale_ref, wout_ref):
    outs = [
        _dot(pg.astype(BF16), wgrp_ref[g]) + bgrp_ref[g]
        for g, pg in enumerate(pooled_groups)
    ]
    mixed = jnp.concatenate(outs, axis=1) * scale_ref[...]
    return _dot(mixed.astype(BF16), wout_ref[...])


def _mix_b_prompt_kernel(x_ref, sh_ref, sc_ref, gt_ref, ng_ref, win_ref, wgrp_ref, bgrp_ref,
                         scale_ref, wout_ref, x1_ref, st_ref, full_ref):
    i = pl.program_id(0)
    j = i % TILES_PER_SEQ
    x = x_ref[...]
    h = (_rms(x, ng_ref[...]) * (1.0 + sc_ref[...]) + sh_ref[...]).astype(BF16)
    p = _dot(h, win_ref[...])

    @pl.when(j == 0)
    def _():
        full_ref[0:POOL_CARRY, :] = jnp.zeros((POOL_CARRY, D), F32)

    full_ref[POOL_CARRY:POOL_CARRY + TM, :] = p
    pos = j * TM + lax.broadcasted_iota(jnp.int32, (TM, 1), 0)
    pooled = []
    for g, w in enumerate(POOL_WINDOWS):
        cols = slice(g * B_GROUP_DIM, (g + 1) * B_GROUP_DIM)
        s = p[:, cols]
        for k in range(1, w):
            s = s + full_ref[pl.ds(POOL_CARRY - k, TM), cols]
        cnt = jnp.minimum(w, pos + 1).astype(F32)
        pooled.append(s / cnt - p[:, cols])
    y = _mix_b_tail(pooled, wgrp_ref, bgrp_ref, scale_ref, wout_ref)
    x1_ref[...] = x + gt_ref[...] * y

    @pl.when(j == TILES_PER_SEQ - 1)
    def _():
        st_ref[...] = full_ref[pl.ds(POOL_CARRY + TM - POOL_BUF, POOL_BUF), :]

    full_ref[0:POOL_CARRY, :] = full_ref[pl.ds(TM, POOL_CARRY), :]


def _mix_b_sample_kernel(x_ref, sh_ref, sc_ref, gt_ref, ng_ref, st_ref, win_ref, wgrp_ref,
                         bgrp_ref, scale_ref, wout_ref, x1_ref, nst_ref):
    x = x_ref[...]
    h = (_rms(x, ng_ref[...]) * (1.0 + sc_ref[...]) + sh_ref[...]).astype(BF16)
    p = _dot(h, win_ref[...])
    pooled = []
    for g, w in enumerate(POOL_WINDOWS):
        s = p[:, g * B_GROUP_DIM:(g + 1) * B_GROUP_DIM]
        for k in range(1, w):
            c0 = (POOL_BUF - k) * D + g * B_GROUP_DIM
            s = s + st_ref[:, c0:c0 + B_GROUP_DIM]
        cnt = float(min(w, PAST_LEN + 1))
        pooled.append(s / cnt - p[:, g * B_GROUP_DIM:(g + 1) * B_GROUP_DIM])
    y = _mix_b_tail(pooled, wgrp_ref, bgrp_ref, scale_ref, wout_ref)
    x1_ref[...] = x + gt_ref[...] * y
    nst_ref[:, 0:(POOL_BUF - 1) * D] = st_ref[:, D:POOL_BUF * D]
    nst_ref[:, (POOL_BUF - 1) * D:POOL_BUF * D] = p


def _mixer_b(x_all, mod_p, mod_s, ng, layer, lb, state, w_in, w_grp, b_grp, scale, w_out):
    gd = B_GROUP_DIM
    common = [
        _layer_block((D, D), lb), _layer_block((len(POOL_WINDOWS), gd, gd), lb),
        _layer_block((len(POOL_WINDOWS), 1, gd), lb), _layer_block((1, D), lb),
        _layer_block((D, D), lb),
    ]
    x_all, st_p = pl.pallas_call(
        _mix_b_prompt_kernel,
        grid=(N_PROMPT_TILES,),
        in_specs=[pl.BlockSpec((TM, D), lambda i: (i, 0))]
        + _mod_specs_prompt(layer, (0, 1, 2)) + [_full((1, D))] + common,
        out_specs=[
            pl.BlockSpec((TM, D), lambda i: (i, 0)),
            pl.BlockSpec((None, POOL_BUF, D), lambda i: (i // TILES_PER_SEQ, 0, 0)),
        ],
        out_shape=[
            jax.ShapeDtypeStruct((T_PAD, D), F32),
            jax.ShapeDtypeStruct((BATCH, POOL_BUF, D), F32),
        ],
        scratch_shapes=[pltpu.VMEM((TM + POOL_CARRY, D), F32)],
        input_output_aliases={0: 0},
        compiler_params=_cparams("arbitrary"),
        name="mixer_b_prompt",
    )(x_all, mod_p, mod_p, mod_p, ng, w_in, w_grp, b_grp, scale, w_out)

    r = SAMPLE_ROWS
    sblk = T_PROMPT // r
    st2 = state[lb].reshape(DEC_BATCH, POOL_BUF * D)
    x_all, st_s = pl.pallas_call(
        _mix_b_sample_kernel,
        grid=(DEC_BATCH // r,),
        in_specs=[pl.BlockSpec((r, D), lambda i: (sblk + i, 0))]
        + _mod_specs_sample(layer, (0, 1, 2), r) + [_full((1, D))]
        + [pl.BlockSpec((r, POOL_BUF * D), lambda i: (i, 0))] + common,
        out_specs=[
            pl.BlockSpec((r, D), lambda i: (sblk + i, 0)),
            pl.BlockSpec((r, POOL_BUF * D), lambda i: (i, 0)),
        ],
        out_shape=[
            jax.ShapeDtypeStruct((T_PAD, D), F32),
            jax.ShapeDtypeStruct((DEC_BATCH, POOL_BUF * D), F32),
        ],
        input_output_aliases={0: 0},
        compiler_params=_cparams("arbitrary"),
        name="mixer_b_sample",
    )(x_all, mod_s, mod_s, mod_s, ng, st2, w_in, w_grp, b_grp, scale, w_out)
    return x_all, st_p, st_s.reshape(DEC_BATCH, POOL_BUF, D)


def _mix_c_glu(x, sh, sc, ng, win_ref, bin_ref):
    h = (_rms(x, ng) * (1.0 + sc) + sh).astype(BF16)
    ag = _dot(h, win_ref[...]) + bin_ref[...]
    return ag[:, :D] * jax.nn.sigmoid(ag[:, D:])


def _mix_c_tail(conv, lng_ref, lnb_ref, wout_ref, bout_ref):
    z = _ln(conv, lng_ref[...], lnb_ref[...])
    z = z * jax.nn.sigmoid(z)
    return _dot(z.astype(BF16), wout_ref[...]) + bout_ref[...]


def _mix_c_prompt_kernel(x_ref, sh_ref, sc_ref, gt_ref, ng_ref, win_ref, bin_ref, wdw_ref,
                         bdw_ref, lng_ref, lnb_ref, wout_ref, bout_ref, x1_ref, st_ref,
                         full_ref, conv_ref, *shift_refs):
    i = pl.program_id(0)
    j = i % TILES_PER_SEQ
    x = x_ref[...]
    glu = _mix_c_glu(x, sh_ref[...], sc_ref[...], ng_ref[...], win_ref, bin_ref)

    @pl.when(j == 0)
    def _():
        full_ref[0:CONV_CARRY, :] = jnp.zeros((CONV_CARRY, D), F32)

    full_ref[CONV_CARRY:CONV_CARRY + TM, :] = glu
    off = CONV_CARRY - CONV_BUF
    span = TM + (off + CONV_BUF) // SUBLANES * SUBLANES - SUBLANES
    for c in range(D // LANES):
        cols = slice(c * LANES, (c + 1) * LANES)
        shift_ref = shift_refs[c % len(shift_refs)]
        for r in range(1, SUBLANES):
            shift_ref[r - 1] = full_ref[pl.ds(r, span), cols]
        for rb in range(TM // CONV_ROWS):
            acc = None
            for k in range(CONV_WIDTH):
                q, r = divmod(off + k, SUBLANES)
                start = q * SUBLANES + rb * CONV_ROWS
                if r == 0:
                    src = full_ref[pl.ds(start, CONV_ROWS), cols]
                else:
                    src = shift_ref[r - 1, pl.ds(start, CONV_ROWS), :]
                term = src * wdw_ref[k:k + 1, cols]
                acc = term if acc is None else acc + term
            conv_ref[pl.ds(rb * CONV_ROWS, CONV_ROWS), cols] = acc + bdw_ref[:, cols]
    y = _mix_c_tail(conv_ref[...], lng_ref, lnb_ref, wout_ref, bout_ref)
    x1_ref[...] = x + gt_ref[...] * y

    @pl.when(j == TILES_PER_SEQ - 1)
    def _():
        st_ref[...] = full_ref[pl.ds(CONV_CARRY + TM - CONV_BUF, CONV_BUF), :]

    full_ref[0:CONV_CARRY, :] = full_ref[pl.ds(TM, CONV_CARRY), :]


def _mix_c_sample_kernel(x_ref, sh_ref, sc_ref, gt_ref, ng_ref, st_ref, win_ref, bin_ref,
                         wdw_ref, bdw_ref, lng_ref, lnb_ref, wout_ref, bout_ref, x1_ref, nst_ref):
    x = x_ref[...]
    glu = _mix_c_glu(x, sh_ref[...], sc_ref[...], ng_ref[...], win_ref, bin_ref)
    acc = glu * wdw_ref[CONV_BUF:CONV_BUF + 1, :]
    for k in range(CONV_BUF):
        acc = acc + st_ref[:, k * D:(k + 1) * D] * wdw_ref[k:k + 1, :]
    y = _mix_c_tail(acc + bdw_ref[...], lng_ref, lnb_ref, wout_ref, bout_ref)
    x1_ref[...] = x + gt_ref[...] * y
    nst_ref[:, 0:(CONV_BUF - 1) * D] = st_ref[:, D:CONV_BUF * D]
    nst_ref[:, (CONV_BUF - 1) * D:CONV_BUF * D] = glu


def _mixer_c(x_all, mod_p, mod_s, ng, layer, lc, state, w_in, b_in, w_dw, b_dw, ln_g, ln_b,
             w_out, b_out):
    common = [
        _layer_block((D, 2 * D), lc), _layer_block((1, 2 * D), lc),
        _layer_block((CONV_WIDTH, D), lc), _layer_block((1, D), lc), _layer_block((1, D), lc),
        _layer_block((1, D), lc), _layer_block((D, D), lc), _layer_block((1, D), lc),
    ]
    x_all, st_p = pl.pallas_call(
        _mix_c_prompt_kernel,
        grid=(N_PROMPT_TILES,),
        in_specs=[pl.BlockSpec((TM, D), lambda i: (i, 0))]
        + _mod_specs_prompt(layer, (0, 1, 2)) + [_full((1, D))] + common,
        out_specs=[
            pl.BlockSpec((TM, D), lambda i: (i, 0)),
            pl.BlockSpec((None, CONV_BUF, D), lambda i: (i // TILES_PER_SEQ, 0, 0)),
        ],
        out_shape=[
            jax.ShapeDtypeStruct((T_PAD, D), F32),
            jax.ShapeDtypeStruct((BATCH, CONV_BUF, D), F32),
        ],
        scratch_shapes=[pltpu.VMEM((TM + CONV_CARRY, D), F32), pltpu.VMEM((TM, D), F32)]
        + [pltpu.VMEM((SUBLANES - 1, TM + CONV_CARRY - SUBLANES, LANES), F32)] * 2,
        input_output_aliases={0: 0},
        compiler_params=_cparams("arbitrary"),
        name="mixer_c_prompt",
    )(x_all, mod_p, mod_p, mod_p, ng, w_in, b_in, w_dw, b_dw, ln_g, ln_b, w_out, b_out)

    r = SAMPLE_ROWS
    sblk = T_PROMPT // r
    st2 = state[lc].reshape(DEC_BATCH, CONV_BUF * D)
    x_all, st_s = pl.pallas_call(
        _mix_c_sample_kernel,
        grid=(DEC_BATCH // r,),
        in_specs=[pl.BlockSpec((r, D), lambda i: (sblk + i, 0))]
        + _mod_specs_sample(layer, (0, 1, 2), r) + [_full((1, D))]
        + [pl.BlockSpec((r, CONV_BUF * D), lambda i: (i, 0))] + common,
        out_specs=[
            pl.BlockSpec((r, D), lambda i: (sblk + i, 0)),
            pl.BlockSpec((r, CONV_BUF * D), lambda i: (i, 0)),
        ],
        out_shape=[
            jax.ShapeDtypeStruct((T_PAD, D), F32),
            jax.ShapeDtypeStruct((DEC_BATCH, CONV_BUF * D), F32),
        ],
        input_output_aliases={0: 0},
        compiler_params=_cparams("arbitrary"),
        name="mixer_c_sample",
    )(x_all, mod_s, mod_s, mod_s, ng, st2, w_in, b_in, w_dw, b_dw, ln_g, ln_b, w_out, b_out)
    return x_all, st_p, st_s.reshape(DEC_BATCH, CONV_BUF, D)


def _tile_mod(i, p_ref, s_ref):
    s_rows = jnp.concatenate([s_ref[...], jnp.zeros((TM - DEC_BATCH, D), F32)], axis=0)
    return jnp.where(i >= N_PROMPT_TILES, s_rows, p_ref[...])


def _unified_mod_specs(layer, chunks):
    specs = []
    for c in chunks:
        specs.append(pl.BlockSpec(
            (None, None, 1, D),
            lambda i, c=c: (layer, jnp.minimum(i // TILES_PER_SEQ, BATCH - 1), 0, c)))
        specs.append(pl.BlockSpec((None, DEC_BATCH, D), lambda i, c=c: (layer, 0, c)))
    return specs


def _router_kernel(x_ref, shp_ref, shs_ref, scp_ref, scs_ref, ng_ref, wr_ref, br_ref,
                   h2_ref, meta_ref, wts_ref, cnt_ref, carry_ref):
    i = pl.program_id(0)

    @pl.when(i == 0)
    def _():
        carry_ref[...] = jnp.zeros((1, LANES), F32)

    sh = _tile_mod(i, shp_ref, shs_ref)
    sc = _tile_mod(i, scp_ref, scs_ref)
    h2 = _rms(x_ref[...], ng_ref[...]) * (1.0 + sc) + sh
    h2_ref[...] = _pack_bf16_pairs(h2)
    lg = _dot(h2.astype(BF16), wr_ref[...]) + br_ref[...]
    lane = lax.broadcasted_iota(jnp.int32, (TM, LANES), 1)
    neg = -jnp.inf
    is_g = lane < N_GROUPS
    gm = jnp.where(is_g, lg, neg)
    gmax = jnp.max(gm, axis=1, keepdims=True)
    lane_f = lane.astype(F32)
    far = float(LANES)
    gsel = jnp.min(jnp.where(gm == gmax, lane_f, far), axis=1, keepdims=True).astype(jnp.int32)
    gsum = jnp.sum(jnp.where(is_g, jnp.exp(lg - gmax), 0.0), axis=1, keepdims=True)
    g_w = 1.0 / gsum
    e_lane = lane - EXPERT_LANE0
    in_grp = (e_lane >= 0) & (e_lane < N_EXPERTS) & (
        lax.shift_right_arithmetic(e_lane, EXPERTS_PER_GROUP.bit_length() - 1) == gsel)
    em = jnp.where(in_grp, lg, neg)
    m1 = jnp.max(em, axis=1, keepdims=True)
    i1 = jnp.min(jnp.where(em == m1, lane_f, far), axis=1, keepdims=True).astype(jnp.int32)
    em2 = jnp.where(lane == i1, neg, em)
    m2 = jnp.max(em2, axis=1, keepdims=True)
    i2 = jnp.min(jnp.where(em2 == m2, lane_f, far), axis=1, keepdims=True).astype(jnp.int32)
    e2 = jnp.exp(m2 - m1)
    den = 1.0 + e2
    w1 = (1.0 / den) * g_w
    w2 = (e2 / den) * g_w

    hit1 = lane == i1
    hit2 = lane == i2
    assign = jnp.where(hit1 | hit2, 1.0, 0.0).astype(BF16)
    r = lax.broadcasted_iota(jnp.int32, (TM, TM), 0)
    c = lax.broadcasted_iota(jnp.int32, (TM, TM), 1)
    before = (c < r).astype(BF16)
    seen = _dot(before, assign) + carry_ref[...]
    rank1 = jnp.sum(jnp.where(hit1, seen, 0.0), axis=1, keepdims=True)
    rank2 = jnp.sum(jnp.where(hit2, seen, 0.0), axis=1, keepdims=True)
    carry_ref[...] = carry_ref[...] + jnp.sum(assign.astype(F32), axis=0, keepdims=True)
    cnt_ref[...] = carry_ref[...].astype(jnp.int32)

    ex1 = (i1 - EXPERT_LANE0).astype(F32)
    ex2 = (i2 - EXPERT_LANE0).astype(F32)
    fields = jnp.where(lane == 0, ex1, jnp.where(lane == 1, ex2,
                       jnp.where(lane == 2, rank1, jnp.where(lane == 3, rank2, 0.0))))
    meta_ref[...] = fields.T[0:META_ROWS, :]
    wts_ref[...] = jnp.where(lane == 0, w1, jnp.where(lane == 1, w2, 0.0))


def _router(x_all, mod_p, mod_s, ng, layer, w_r, b_r):
    return pl.pallas_call(
        _router_kernel,
        grid=(N_TILES,),
        in_specs=[pl.BlockSpec((TM, D), lambda i: (i, 0))]
        + _unified_mod_specs(layer, (3, 4))
        + [_full((1, D)), _full((D, LANES)), _full((1, LANES))],
        out_specs=[
            pl.BlockSpec((TM, D // 2), lambda i: (i, 0)),
            pl.BlockSpec((META_ROWS, TM), lambda i: (0, i)),
            pl.BlockSpec((TM, LANES), lambda i: (i, 0)),
            pl.BlockSpec((1, LANES), lambda i: (0, 0)),
        ],
        out_shape=[
            jax.ShapeDtypeStruct((T_PAD, D // 2), jnp.int32),
            jax.ShapeDtypeStruct((META_ROWS, T_PAD), F32),
            jax.ShapeDtypeStruct((T_PAD, LANES), F32),
            jax.ShapeDtypeStruct((1, LANES), jnp.int32),
        ],
        scratch_shapes=[pltpu.VMEM((1, LANES), F32)],
        compiler_params=_cparams("arbitrary"),
        name="router",
    )(x_all, mod_p, mod_s, mod_p, mod_s, ng, w_r, b_r)


SC_INDEX_MAX = 128
SC_ROWS_BYTES = 256 * 1024


def _gather_chunk(per_worker, row_bytes):
    top = min(SC_INDEX_MAX, SC_ROWS_BYTES // row_bytes) // SUBLANES * SUBLANES
    for ch in range(top, SUBLANES - 1, -SUBLANES):
        if per_worker % ch == 0:
            return ch
    raise ValueError(per_worker)


def _sc_gather(table, idx):
    nc, nw = SC_CORES, SC_CORES * SC_SUBCORES
    m = idx.shape[0]
    per_w = m // nw
    assert per_w * nw == m
    width = table.shape[1]
    ch = _gather_chunk(per_w, width * table.dtype.itemsize)
    mesh = plsc.VectorSubcoreMesh(core_axis_name="c", subcore_axis_name="s")

    @functools.partial(
        pl.kernel,
        out_type=jax.ShapeDtypeStruct((m, width), table.dtype),
        mesh=mesh,
        scratch_types=[
            pltpu.VMEM((ch,), jnp.int32),
            pltpu.VMEM((ch, width), table.dtype),
            pltpu.SemaphoreType.DMA,
        ],
    )
    def gather_kernel(t_hbm, i_hbm, o_hbm, idx_v, rows_v, sem):
        wid = lax.axis_index("s") * nc + lax.axis_index("c")
        base = wid * per_w

        @pl.loop(0, per_w // ch)
        def _(j):
            off = pl.multiple_of(base + j * ch, 8)
            pltpu.sync_copy(i_hbm.at[pl.ds(off, ch)], idx_v)
            pltpu.async_copy(t_hbm.at[idx_v], rows_v, sem).wait()
            pltpu.sync_copy(rows_v, o_hbm.at[pl.ds(off, ch)])

    return gather_kernel(table, idx)


def _sc_scatter2(rows, dest0, dest1, n_out):
    nc, nw = SC_CORES, SC_CORES * SC_SUBCORES
    n, width = rows.shape
    per_w = n // nw
    assert per_w * nw == n
    ch = _gather_chunk(per_w, width * rows.dtype.itemsize)
    mesh = plsc.VectorSubcoreMesh(core_axis_name="c", subcore_axis_name="s")

    @functools.partial(
        pl.kernel,
        out_type=jax.ShapeDtypeStruct((n_out, width), rows.dtype),
        mesh=mesh,
        scratch_types=[
            pltpu.VMEM((ch,), jnp.int32),
            pltpu.VMEM((ch,), jnp.int32),
            pltpu.VMEM((ch, width), rows.dtype),
            pltpu.SemaphoreType.DMA,
            pltpu.SemaphoreType.DMA,
        ],
    )
    def scatter_kernel(r_hbm, d0_hbm, d1_hbm, o_hbm, i0_v, i1_v, rows_v, sem0, sem1):
        wid = lax.axis_index("s") * nc + lax.axis_index("c")
        base = wid * per_w

        @pl.loop(0, per_w // ch)
        def _(j):
            off = pl.multiple_of(base + j * ch, 8)
            pltpu.sync_copy(d0_hbm.at[pl.ds(off, ch)], i0_v)
            pltpu.sync_copy(d1_hbm.at[pl.ds(off, ch)], i1_v)
            pltpu.sync_copy(r_hbm.at[pl.ds(off, ch)], rows_v)
            c0 = pltpu.async_copy(rows_v, o_hbm.at[i0_v], sem0)
            c1 = pltpu.async_copy(rows_v, o_hbm.at[i1_v], sem1)
            c0.wait()
            c1.wait()

    return scatter_kernel(rows, dest0, dest1)


def _expert_kernel(be_ref, first_ref, nvalid_ref, next_ref, nused_ref, x_ref, wg_hbm, wu_hbm,
                   wd_hbm, y_ref, wgf_ref, wuf_ref, wdf_ref, wgb_ref, wub_ref, wdb_ref, sems,
                   *, layer):
    b = pl.program_id(0)

    def weight_copies(e):
        return (
            pltpu.make_async_copy(wg_hbm.at[layer, e], wgf_ref, sems.at[0]),
            pltpu.make_async_copy(wu_hbm.at[layer, e], wuf_ref, sems.at[1]),
            pltpu.make_async_copy(wd_hbm.at[layer, e], wdf_ref, sems.at[2]),
        )

    @pl.when(b == 0)
    def _():
        for cp in weight_copies(be_ref[0]):
            cp.start()

    @pl.when(b < nused_ref[0])
    def _():
        @pl.when(first_ref[b] == 1)
        def _():
            for cp in weight_copies(be_ref[b]):
                cp.wait()
            wgb_ref[...] = wgf_ref[...].astype(BF16)
            wub_ref[...] = wuf_ref[...].astype(BF16)
            wdb_ref[...] = wdf_ref[...].astype(BF16)

            @pl.when(next_ref[b] >= 0)
            def _():
                for cp in weight_copies(next_ref[b]):
                    cp.start()

        live = lax.broadcasted_iota(jnp.int32, (BM, 1), 0) < nvalid_ref[b]
        x = _unpack_bf16_pairs(jnp.where(live, x_ref[...], 0), BF16)
        g = _dot(x, wgb_ref[...])
        u = _dot(x, wub_ref[...])
        hmid = (g * jax.nn.sigmoid(g)) * u
        y_ref[...] = _pack_bf16_pairs(_dot(hmid.astype(BF16), wdb_ref[...]))

    @pl.when(b >= nused_ref[0])
    def _():
        y_ref[...] = jnp.zeros((BM, D // 2), jnp.int32)


def _experts(x_sorted, block_expert, first, nvalid, next_expert, nused, layer, w_gate, w_up,
             w_down):
    def row_map(b, be, fi, nv, nx, nu):
        return (jnp.minimum(b, nu[0] - 1), 0)

    def out_map(b, be, fi, nv, nx, nu):
        return (b, 0)

    grid_spec = pltpu.PrefetchScalarGridSpec(
        num_scalar_prefetch=5,
        grid=(N_BLOCKS,),
        in_specs=[
            pl.BlockSpec((BM, D // 2), row_map),
            pl.BlockSpec(memory_space=pl.ANY),
            pl.BlockSpec(memory_space=pl.ANY),
            pl.BlockSpec(memory_space=pl.ANY),
        ],
        out_specs=pl.BlockSpec((BM, D // 2), out_map),
        scratch_shapes=[
            pltpu.VMEM((D, D_EXPERT), F32),
            pltpu.VMEM((D, D_EXPERT), F32),
            pltpu.VMEM((D_EXPERT, D), F32),
            pltpu.VMEM((D, D_EXPERT), BF16),
            pltpu.VMEM((D, D_EXPERT), BF16),
            pltpu.VMEM((D_EXPERT, D), BF16),
            pltpu.SemaphoreType.DMA((3,)),
        ],
    )
    return pl.pallas_call(
        functools.partial(_expert_kernel, layer=layer),
        grid_spec=grid_spec,
        out_shape=jax.ShapeDtypeStruct((P_ROWS, D // 2), jnp.int32),
        compiler_params=_cparams("arbitrary"),
        name="experts",
    )(block_expert, first, nvalid, next_expert, nused, x_sorted, w_gate, w_up, w_down)


def _combined_rows(i, x_ref, g0_ref, g1_ref, wts_ref, gtp_ref, gts_ref):
    gt = _tile_mod(i, gtp_ref, gts_ref)
    w = wts_ref[...]
    y0 = _unpack_bf16_pairs(g0_ref[...], F32)
    y1 = _unpack_bf16_pairs(g1_ref[...], F32)
    return x_ref[...] + gt * (w[:, 0:1] * y0 + w[:, 1:2] * y1)


def _combine_kernel(x_ref, g0_ref, g1_ref, wts_ref, gtp_ref, gts_ref, o_ref):
    i = pl.program_id(0)
    o_ref[...] = _combined_rows(i, x_ref, g0_ref, g1_ref, wts_ref, gtp_ref, gts_ref)


def _combine_final_kernel(x_ref, g0_ref, g1_ref, wts_ref, gtp_ref, gts_ref, fg_ref, yp_ref,
                          ys_ref):
    i = pl.program_id(0)
    y = _rms(_combined_rows(i, x_ref, g0_ref, g1_ref, wts_ref, gtp_ref, gts_ref), fg_ref[...])

    @pl.when(i < N_PROMPT_TILES)
    def _():
        yp_ref[...] = y

    @pl.when(i == N_PROMPT_TILES)
    def _():
        ys_ref[...] = y[0:DEC_BATCH, :]


def _combine(x_all, gathered, wts, mod_p, mod_s, layer, final_g=None):
    in_specs = [
        pl.BlockSpec((TM, D), lambda i: (i, 0)),
        pl.BlockSpec((TM, D // 2), lambda i: (i, 0)),
        pl.BlockSpec((TM, D // 2), lambda i: (i + N_TILES, 0)),
        pl.BlockSpec((TM, LANES), lambda i: (i, 0)),
    ] + _unified_mod_specs(layer, (5,))
    if final_g is None:
        return pl.pallas_call(
            _combine_kernel,
            grid=(N_TILES,),
            in_specs=in_specs,
            out_specs=pl.BlockSpec((TM, D), lambda i: (i, 0)),
            out_shape=jax.ShapeDtypeStruct((T_PAD, D), F32),
            input_output_aliases={0: 0},
            compiler_params=_cparams("arbitrary"),
            name="combine",
        )(x_all, gathered, gathered, wts, mod_p, mod_s)
    return pl.pallas_call(
        _combine_final_kernel,
        grid=(N_PROMPT_TILES + 1,),
        in_specs=in_specs + [_full((1, D))],
        out_specs=[
            pl.BlockSpec((TM, D), lambda i: (jnp.minimum(i, N_PROMPT_TILES - 1), 0)),
            pl.BlockSpec((DEC_BATCH, D), lambda i: (0, 0)),
        ],
        out_shape=[
            jax.ShapeDtypeStruct((T_PROMPT, D), F32),
            jax.ShapeDtypeStruct((DEC_BATCH, D), F32),
        ],
        compiler_params=_cparams("arbitrary"),
        name="combine_final",
    )(x_all, gathered, gathered, wts, mod_p, mod_s, final_g)


def _plan_kernel(cnt_ref, meta_ref, d0_ref, d1_ref, be_ref, first_ref, nvalid_ref, next_ref,
                 nused_ref, pstart_ref, nxt_ref):
    shift = BM.bit_length() - 1
    assert 1 << shift == BM

    def count(e):
        return cnt_ref[0, EXPERT_LANE0 + e]

    def scan_back(k, nxt):
        e = N_EXPERTS - 1 - k
        nxt_ref[e] = nxt
        return jnp.where(count(e) > 0, e, nxt)

    lax.fori_loop(0, N_EXPERTS, scan_back, jnp.int32(-1))

    def fill(e, blk):
        c = count(e)
        pstart_ref[e] = blk << shift

        def one(j, carry):
            be_ref[blk + j] = e
            first_ref[blk + j] = jnp.where(j == 0, 1, 0)
            nvalid_ref[blk + j] = jnp.minimum(c - (j << shift), BM)
            next_ref[blk + j] = nxt_ref[e]
            return carry

        n_blk = (c + (BM - 1)) >> shift
        lax.fori_loop(0, n_blk, one, 0)
        return blk + n_blk

    used = lax.fori_loop(0, N_EXPERTS, fill, jnp.int32(0))
    nused_ref[0] = used

    def tail(b, carry):
        be_ref[b] = 0
        first_ref[b] = 0
        nvalid_ref[b] = 0
        next_ref[b] = -1
        return carry

    lax.fori_loop(used, N_BLOCKS, tail, 0)

    m = meta_ref[...]
    expert = m[0:2, :]
    dest = m[2:4, :]
    for e in range(N_EXPERTS):
        dest = dest + jnp.where(expert == float(e), pstart_ref[e].astype(F32), 0.0)
    dest = dest.astype(jnp.int32)
    d0_ref[...] = dest[0:1, :]
    d1_ref[...] = dest[1:2, :]


def _plan(cnt, meta):
    smem = pl.BlockSpec(memory_space=pltpu.SMEM)
    blocks = jax.ShapeDtypeStruct((N_BLOCKS,), jnp.int32)
    return pl.pallas_call(
        _plan_kernel,
        grid=(1,),
        in_specs=[smem, _full((META_ROWS, T_PAD))],
        out_specs=[_full((1, T_PAD)), _full((1, T_PAD)), smem, smem, smem, smem, smem],
        out_shape=[
            jax.ShapeDtypeStruct((1, T_PAD), jnp.int32),
            jax.ShapeDtypeStruct((1, T_PAD), jnp.int32),
            blocks, blocks, blocks, blocks,
            jax.ShapeDtypeStruct((1,), jnp.int32),
        ],
        scratch_shapes=[pltpu.SMEM((N_EXPERTS,), jnp.int32), pltpu.SMEM((N_EXPERTS,), jnp.int32)],
        compiler_params=_cparams("arbitrary"),
        name="plan",
    )(cnt, meta)


def _moe(x_all, mod_p, mod_s, ng, layer, w_r, b_r, w_gate, w_up, w_down, final_g=None):
    h2, meta, wts, cnt = _router(x_all, mod_p, mod_s, ng, layer, w_r, b_r)
    dest0, dest1, block_expert, first, nvalid, next_expert, nused = _plan(cnt, meta)
    dest0, dest1 = dest0.reshape(T_PAD), dest1.reshape(T_PAD)
    x_sorted = _sc_scatter2(h2, dest0, dest1, P_ROWS)
    y_sorted = _experts(x_sorted, block_expert, first, nvalid, next_expert, nused, layer,
                        w_gate, w_up, w_down)
    gathered = _sc_gather(y_sorted, jnp.concatenate([dest0, dest1]))
    return _combine(x_all, gathered, wts, mod_p, mod_s, layer, final_g)


def kernel(x_prompt, x_sample, c_prompt, c_sample, state_pool, state_conv, ada_w, ada_b, norm_g, final_g, a_w_in, a_ln_g, a_ln_b, a_w_s, a_b_s, a_w_out, b_w_in, b_w_grp, b_b_grp, b_scale, b_w_out, c_w_in, c_b_in, c_w_dw, c_b_dw, c_ln_g, c_ln_b, c_w_out, c_b_out, moe_w_grp, moe_b_grp, moe_w_exp, moe_b_exp, moe_w_gate, moe_w_up, moe_w_down):
    x_all = jnp.concatenate([
        x_prompt.reshape(T_PROMPT, D), x_sample.reshape(DEC_BATCH, D),
        jnp.zeros((T_PAD - T_ALL, D), F32)], axis=0)
    mods = _ada_mods(jnp.concatenate([c_prompt, c_sample], axis=0), ada_w, ada_b)
    mod_p = mods[:, :BATCH].reshape(DEPTH, BATCH, 1, 6 * D)
    mod_s = mods[:, BATCH:]

    a_w_in_b, a_w_out_b = a_w_in.astype(BF16), a_w_out.astype(BF16)
    b_w_in_b, b_w_grp_b, b_w_out_b = b_w_in.astype(BF16), b_w_grp.astype(BF16), b_w_out.astype(BF16)
    c_w_in_b, c_w_out_b = c_w_in.astype(BF16), c_w_out.astype(BF16)
    w_r = jnp.pad(jnp.concatenate([moe_w_grp, moe_w_exp], axis=2),
                  ((0, 0), (0, 0), (0, LANES - N_GROUPS - N_EXPERTS))).astype(BF16)
    b_r = jnp.pad(jnp.concatenate([moe_b_grp, moe_b_exp], axis=1),
                  ((0, 0), (0, LANES - N_GROUPS - N_EXPERTS)))

    new_a_p, new_a_s, new_b_p, new_b_s, new_c_p, new_c_s = [], [], [], [], [], []
    ia = ib = ic = 0
    for layer in range(DEPTH):
        ng1 = norm_g[layer, 0].reshape(1, D)
        ng2 = norm_g[layer, 1].reshape(1, D)
        kind = layer % 3
        if kind == 0:
            x_all, st_p, st_s = _mixer_a(
                x_all, mod_p, mod_s, ng1, layer, ia, a_w_in_b, a_ln_g.reshape(-1, 1, D),
                a_ln_b.reshape(-1, 1, D), a_w_s, a_b_s, a_w_out_b)
            new_a_p.append(st_p)
            new_a_s.append(st_s)
            ia += 1
        elif kind == 1:
            x_all, st_p, st_s = _mixer_b(
                x_all, mod_p, mod_s, ng1, layer, ib, state_pool, b_w_in_b, b_w_grp_b,
                b_b_grp.reshape(-1, len(POOL_WINDOWS), 1, B_GROUP_DIM),
                b_scale.reshape(-1, 1, D), b_w_out_b)
            new_b_p.append(st_p)
            new_b_s.append(st_s)
            ib += 1
        else:
            x_all, st_p, st_s = _mixer_c(
                x_all, mod_p, mod_s, ng1, layer, ic, state_conv, c_w_in_b,
                c_b_in.reshape(-1, 1, 2 * D), c_w_dw, c_b_dw.reshape(-1, 1, D),
                c_ln_g.reshape(-1, 1, D), c_ln_b.reshape(-1, 1, D), c_w_out_b,
                c_b_out.reshape(-1, 1, D))
            new_c_p.append(st_p)
            new_c_s.append(st_s)
            ic += 1
        last = layer == DEPTH - 1
        x_all = _moe(x_all, mod_p, mod_s, ng2, layer, w_r[layer], b_r[layer].reshape(1, LANES),
                     moe_w_gate, moe_w_up, moe_w_down,
                     final_g=final_g.reshape(1, D) if last else None)

    y_p, y_s = x_all
    return (y_p.reshape(BATCH, SEQ, D), y_s.reshape(DEC_BATCH, 1, D),
            jnp.stack(new_a_p), jnp.stack(new_a_s), jnp.stack(new_b_p), jnp.stack(new_b_s),
            jnp.stack(new_c_p), jnp.stack(new_c_s))
```

```python
import functools

import jax
import jax.numpy as jnp
from jax import lax
from jax.experimental import pallas as pl
from jax.experimental.pallas import tpu as pltpu
from jax.experimental.pallas import tpu_sc as plsc

F32 = jnp.float32
BF16 = jnp.bfloat16

D = 1024
BATCH = 8
SEQ = 2048
DEPTH = 4
DEC_BATCH = 128
PAST_LEN = 16384
CHUNK = 128
A_HEADS = 8
POOL_WINDOWS = (2, 4, 8, 16)
B_GROUP_DIM = D // len(POOL_WINDOWS)
POOL_BUF = max(POOL_WINDOWS) - 1
CONV_WIDTH = 31
CONV_BUF = CONV_WIDTH - 1
N_GROUPS = 4
EXPERTS_PER_GROUP = 8
N_EXPERTS = N_GROUPS * EXPERTS_PER_GROUP
D_EXPERT = D // 2
EPS = 1e-6

LANES = 128
SUBLANES = 8
CONV_ROWS = 128
SC_CORES = 2
SC_SUBCORES = 16
TM = 512
TILES_PER_SEQ = SEQ // TM
T_PROMPT = BATCH * SEQ
T_ALL = T_PROMPT + DEC_BATCH
N_TILES = -(-T_ALL // TM)
T_PAD = N_TILES * TM
N_PROMPT_TILES = T_PROMPT // TM
SAMPLE_ROWS = 32
BM = 512
N_BLOCKS = -(-(2 * T_PAD) // BM) + N_EXPERTS
P_ROWS = N_BLOCKS * BM
META_ROWS = 8
POOL_CARRY = 16
CONV_CARRY = 32
EXPERT_LANE0 = N_GROUPS
VMEM_LIMIT = 56 * 1024 * 1024


def _cparams(*sem):
    return pltpu.CompilerParams(dimension_semantics=sem, vmem_limit_bytes=VMEM_LIMIT)


def _rms(x, g):
    return x * lax.rsqrt(jnp.mean(x * x, axis=-1, keepdims=True) + EPS) * g


def _ln(x, g, b):
    mu = jnp.mean(x, axis=-1, keepdims=True)
    xc = x - mu
    var = jnp.mean(xc * xc, axis=-1, keepdims=True)
    return xc * lax.rsqrt(var + EPS) * g + b


def _dot(a, b):
    return jnp.dot(a, b, preferred_element_type=F32)


def _pack_bf16_pairs(v):
    w = v.shape[1] // 2
    lo = lax.bitcast_convert_type(v[:, :w].astype(BF16).astype(F32), jnp.uint32)
    hi = lax.bitcast_convert_type(v[:, w:].astype(BF16).astype(F32), jnp.uint32)
    return lax.bitcast_convert_type(lax.shift_right_logical(lo, jnp.uint32(16)) | hi, jnp.int32)


def _unpack_bf16_pairs(words, dtype):
    u = lax.bitcast_convert_type(words, jnp.uint32)
    lo = lax.bitcast_convert_type(lax.shift_left(u, jnp.uint32(16)), F32)
    hi = lax.bitcast_convert_type(u & jnp.uint32(0xFFFF0000), F32)
    return jnp.concatenate([lo.astype(dtype), hi.astype(dtype)], axis=1)


def _ada_kernel(c_ref, w_ref, b_ref, o_ref):
    c = c_ref[...]
    s = (c * jax.nn.sigmoid(c)).astype(BF16)
    o_ref[...] = _dot(s, w_ref[...].astype(BF16)) + b_ref[...]


def _ada_mods(c_all, ada_w, ada_b):
    n = c_all.shape[0]
    tn = 2048
    return pl.pallas_call(
        _ada_kernel,
        grid=(DEPTH, 6 * D // tn),
        in_specs=[
            pl.BlockSpec((n, D), lambda l, j: (0, 0)),
            pl.BlockSpec((None, D, tn), lambda l, j: (l, 0, j)),
            pl.BlockSpec((None, 1, tn), lambda l, j: (l, 0, j)),
        ],
        out_specs=pl.BlockSpec((None, n, tn), lambda l, j: (l, 0, j)),
        out_shape=jax.ShapeDtypeStruct((DEPTH, n, 6 * D), F32),
        compiler_params=_cparams("parallel", "parallel"),
        name="ada_mods",
    )(c_all, ada_w, ada_b.reshape(DEPTH, 1, 6 * D))


def _mod_specs_prompt(layer, chunks):
    return [
        pl.BlockSpec((None, None, 1, D), lambda i, c=c: (layer, i // TILES_PER_SEQ, 0, c))
        for c in chunks
    ]


def _mod_specs_sample(layer, chunks, rows, row_block0=0):
    return [
        pl.BlockSpec((None, rows, D), lambda i, c=c: (layer, i + row_block0, c)) for c in chunks
    ]


def _full(shape):
    nd = len(shape)
    return pl.BlockSpec(shape, lambda i: (0,) * nd)


def _layer_block(shape, layer):
    nd = len(shape)
    return pl.BlockSpec((None,) + shape, lambda i: (layer,) + (0,) * nd)


def _plus_moe(x, g0_ref, g1_ref, wts_ref, gate):
    w = wts_ref[...]
    y0 = _unpack_bf16_pairs(g0_ref[...], F32)
    y1 = _unpack_bf16_pairs(g1_ref[...], F32)
    return x + gate * (w[:, 0:1] * y0 + w[:, 1:2] * y1)


def _residual_in(refs, pending):
    if not pending:
        return refs[0][...], refs[1:]
    x_ref, g0_ref, g1_ref, wts_ref, gate_ref = refs[:5]
    return _plus_moe(x_ref[...], g0_ref, g1_ref, wts_ref, gate_ref[...]), refs[5:]


def _pending_prompt(pending, mod_p, layer):
    if pending is None:
        return [], []
    gathered, wts = pending
    specs = [
        pl.BlockSpec((TM, D // 2), lambda i: (i, 0)),
        pl.BlockSpec((TM, D // 2), lambda i: (i + N_TILES, 0)),
        pl.BlockSpec((TM, LANES), lambda i: (i, 0)),
    ] + _mod_specs_prompt(layer - 1, (5,))
    return specs, [gathered, gathered, wts, mod_p]


def _pending_sample(pending, mod_s, layer, rows):
    if pending is None:
        return [], []
    gathered, wts = pending
    first = T_PROMPT // rows
    second = (T_PAD + T_PROMPT) // rows
    specs = [
        pl.BlockSpec((rows, D // 2), lambda i: (first + i, 0)),
        pl.BlockSpec((rows, D // 2), lambda i: (second + i, 0)),
        pl.BlockSpec((rows, LANES), lambda i: (first + i, 0)),
    ] + _mod_specs_sample(layer - 1, (5,), rows)
    return specs, [gathered, gathered, wts, mod_s]


def _mix_a_front(x, sh, sc, ng, win_ref, lng_ref, lnb_ref):
    h = (_rms(x, ng) * (1.0 + sc) + sh).astype(BF16)
    z = jax.nn.gelu(_dot(h, win_ref[...]))
    u = z[:, :D]
    v = _ln(z[:, D:], lng_ref[...], lnb_ref[...])
    return u, v


def _mix_a_prompt_kernel(*refs, pending):
    x, refs = _residual_in(refs, pending)
    (sh_ref, sc_ref, gt_ref, ng_ref, win_ref, lng_ref, lnb_ref, ws_ref, bs_ref, wout_ref,
     x1_ref, cv_ref, mixed_ref) = refs
    i = pl.program_id(0)
    u, v = _mix_a_front(x, sh_ref[...], sc_ref[...], ng_ref[...], win_ref, lng_ref, lnb_ref)
    vb = v.astype(BF16)
    n_chunks = TM // CHUNK
    row = lax.broadcasted_iota(jnp.int32, (CHUNK, CHUNK), 0)
    col = lax.broadcasted_iota(jnp.int32, (CHUNK, CHUNK), 1)
    tril = row >= col
    hd_dim = D // A_HEADS
    for hd in range(A_HEADS):
        ws = jnp.where(tril, ws_ref[hd], 0.0).astype(BF16)
        cols = slice(hd * hd_dim, (hd + 1) * hd_dim)
        vcat = jnp.concatenate([vb[c * CHUNK:(c + 1) * CHUNK, cols] for c in range(n_chunks)], axis=1)
        m = _dot(ws, vcat)
        for c in range(n_chunks):
            mixed_ref[c * CHUNK:(c + 1) * CHUNK, cols] = m[:, c * hd_dim:(c + 1) * hd_dim] + bs_ref[hd]
    y = _dot((u * mixed_ref[...]).astype(BF16), wout_ref[...])
    x1_ref[...] = x + gt_ref[...] * y

    @pl.when(i % TILES_PER_SEQ == TILES_PER_SEQ - 1)
    def _():
        cv_ref[...] = v[TM - CHUNK:, :]


def _mix_a_sample_kernel(*refs, pending):
    x, refs = _residual_in(refs, pending)
    (sh_ref, sc_ref, gt_ref, ng_ref, win_ref, lng_ref, lnb_ref, wsd_ref, bsd_ref, wout_ref,
     x1_ref, cv_ref) = refs
    u, v = _mix_a_front(x, sh_ref[...], sc_ref[...], ng_ref[...], win_ref, lng_ref, lnb_ref)
    mixed = wsd_ref[...] * v + bsd_ref[...]
    y = _dot((u * mixed).astype(BF16), wout_ref[...])
    x1_ref[...] = x + gt_ref[...] * y
    cv_ref[...] = v


def _mixer_a(x_all, pending, mod_p, mod_s, ng, layer, la, w_in, ln_g, ln_b, w_s, b_s, w_out):
    bs_b = jnp.broadcast_to(b_s[la][:, :, None], (A_HEADS, CHUNK, D // A_HEADS))
    common = [
        _layer_block((D, 2 * D), la), _layer_block((1, D), la), _layer_block((1, D), la),
    ]
    has_pending = pending is not None
    pend_specs, pend_args = _pending_prompt(pending, mod_p, layer)
    x_all, cv_p = pl.pallas_call(
        functools.partial(_mix_a_prompt_kernel, pending=has_pending),
        grid=(N_PROMPT_TILES,),
        in_specs=[pl.BlockSpec((TM, D), lambda i: (i, 0))] + pend_specs
        + _mod_specs_prompt(layer, (0, 1, 2))
        + [_full((1, D))] + common
        + [_layer_block((A_HEADS, CHUNK, CHUNK), la), _full((A_HEADS, CHUNK, D // A_HEADS)),
           _layer_block((D, D), la)],
        out_specs=[
            pl.BlockSpec((TM, D), lambda i: (i, 0)),
            pl.BlockSpec((None, CHUNK, D), lambda i: (i // TILES_PER_SEQ, 0, 0)),
        ],
        out_shape=[
            jax.ShapeDtypeStruct((T_PAD, D), F32),
            jax.ShapeDtypeStruct((BATCH, CHUNK, D), F32),
        ],
        scratch_shapes=[pltpu.VMEM((TM, D), F32)],
        input_output_aliases={0: 0},
        compiler_params=_cparams("arbitrary"),
        name="mixer_a_prompt",
    )(x_all, *pend_args, mod_p, mod_p, mod_p, ng, w_in, ln_g, ln_b, w_s, bs_b, w_out)

    wsd = jnp.repeat(w_s[la, :, 0, 0], D // A_HEADS).reshape(1, D)
    bsd = jnp.repeat(b_s[la, :, 0], D // A_HEADS).reshape(1, D)
    sblk = T_PROMPT // DEC_BATCH
    pend_specs, pend_args = _pending_sample(pending, mod_s, layer, DEC_BATCH)
    x_all, cv_s = pl.pallas_call(
        functools.partial(_mix_a_sample_kernel, pending=has_pending),
        grid=(1,),
        in_specs=[pl.BlockSpec((DEC_BATCH, D), lambda i: (sblk, 0))] + pend_specs
        + _mod_specs_sample(layer, (0, 1, 2), DEC_BATCH)
        + [_full((1, D))] + common
        + [_full((1, D)), _full((1, D)), _layer_block((D, D), la)],
        out_specs=[
            pl.BlockSpec((DEC_BATCH, D), lambda i: (sblk, 0)),
            pl.BlockSpec((DEC_BATCH, D), lambda i: (0, 0)),
        ],
        out_shape=[
            jax.ShapeDtypeStruct((T_PAD, D), F32),
            jax.ShapeDtypeStruct((DEC_BATCH, D), F32),
        ],
        input_output_aliases={0: 0},
        compiler_params=_cparams("arbitrary"),
        name="mixer_a_sample",
    )(x_all, *pend_args, mod_s, mod_s, mod_s, ng, w_in, ln_g, ln_b, wsd, bsd, w_out)
    return x_all, cv_p, cv_s.reshape(DEC_BATCH, 1, D)


def _mix_b_tail(pooled_groups, wgrp_ref, bgrp_ref, scale_ref, wout_ref):
    outs = [
        _dot(pg.astype(BF16), wgrp_ref[g]) + bgrp_ref[g]
        for g, pg in enumerate(pooled_groups)
    ]
    mixed = jnp.concatenate(outs, axis=1) * scale_ref[...]
    return _dot(mixed.astype(BF16), wout_ref[...])


def _mix_b_prompt_kernel(*refs, pending):
    x, refs = _residual_in(refs, pending)
    (sh_ref, sc_ref, gt_ref, ng_ref, win_ref, wgrp_ref, bgrp_ref, scale_ref, wout_ref,
     x1_ref, st_ref, full_ref) = refs
    i = pl.program_id(0)
    j = i % TILES_PER_SEQ
    h = (_rms(x, ng_ref[...]) * (1.0 + sc_ref[...]) + sh_ref[...]).astype(BF16)
    p = _dot(h, win_ref[...])

    @pl.when(j == 0)
    def _():
        full_ref[0:POOL_CARRY, :] = jnp.zeros((POOL_CARRY, D), F32)

    full_ref[POOL_CARRY:POOL_CARRY + TM, :] = p
    pos = j * TM + lax.broadcasted_iota(jnp.int32, (TM, 1), 0)
    pooled = []
    for g, w in enumerate(POOL_WINDOWS):
        cols = slice(g * B_GROUP_DIM, (g + 1) * B_GROUP_DIM)
        s = p[:, cols]
        for k in range(1, w):
            s = s + full_ref[pl.ds(POOL_CARRY - k, TM), cols]
        cnt = jnp.minimum(w, pos + 1).astype(F32)
        pooled.append(s / cnt - p[:, cols])
    y = _mix_b_tail(pooled, wgrp_ref, bgrp_ref, scale_ref, wout_ref)
    x1_ref[...] = x + gt_ref[...] * y

    @pl.when(j == TILES_PER_SEQ - 1)
    def _():
        st_ref[...] = full_ref[pl.ds(POOL_CARRY + TM - POOL_BUF, POOL_BUF), :]

    full_ref[0:POOL_CARRY, :] = full_ref[pl.ds(TM, POOL_CARRY), :]


def _mix_b_sample_kernel(*refs, pending):
    x, refs = _residual_in(refs, pending)
    (sh_ref, sc_ref, gt_ref, ng_ref, st_ref, win_ref, wgrp_ref, bgrp_ref, scale_ref, wout_ref,
     x1_ref, nst_ref) = refs
    h = (_rms(x, ng_ref[...]) * (1.0 + sc_ref[...]) + sh_ref[...]).astype(BF16)
    p = _dot(h, win_ref[...])
    pooled = []
    for g, w in enumerate(POOL_WINDOWS):
        s = p[:, g * B_GROUP_DIM:(g + 1) * B_GROUP_DIM]
        for k in range(1, w):
            c0 = (POOL_BUF - k) * D + g * B_GROUP_DIM
            s = s + st_ref[:, c0:c0 + B_GROUP_DIM]
        cnt = float(min(w, PAST_LEN + 1))
        pooled.append(s / cnt - p[:, g * B_GROUP_DIM:(g + 1) * B_GROUP_DIM])
    y = _mix_b_tail(pooled, wgrp_ref, bgrp_ref, scale_ref, wout_ref)
    x1_ref[...] = x + gt_ref[...] * y
    nst_ref[:, 0:(POOL_BUF - 1) * D] = st_ref[:, D:POOL_BUF * D]
    nst_ref[:, (POOL_BUF - 1) * D:POOL_BUF * D] = p


def _mixer_b(x_all, pending, mod_p, mod_s, ng, layer, lb, state, w_in, w_grp, b_grp, scale,
             w_out):
    gd = B_GROUP_DIM
    common = [
        _layer_block((D, D), lb), _layer_block((len(POOL_WINDOWS), gd, gd), lb),
        _layer_block((len(POOL_WINDOWS), 1, gd), lb), _layer_block((1, D), lb),
        _layer_block((D, D), lb),
    ]
    has_pending = pending is not None
    pend_specs, pend_args = _pending_prompt(pending, mod_p, layer)
    x_all, st_p = pl.pallas_call(
        functools.partial(_mix_b_prompt_kernel, pending=has_pending),
        grid=(N_PROMPT_TILES,),
        in_specs=[pl.BlockSpec((TM, D), lambda i: (i, 0))] + pend_specs
        + _mod_specs_prompt(layer, (0, 1, 2)) + [_full((1, D))] + common,
        out_specs=[
            pl.BlockSpec((TM, D), lambda i: (i, 0)),
            pl.BlockSpec((None, POOL_BUF, D), lambda i: (i // TILES_PER_SEQ, 0, 0)),
        ],
        out_shape=[
            jax.ShapeDtypeStruct((T_PAD, D), F32),
            jax.ShapeDtypeStruct((BATCH, POOL_BUF, D), F32),
        ],
        scratch_shapes=[pltpu.VMEM((TM + POOL_CARRY, D), F32)],
        input_output_aliases={0: 0},
        compiler_params=_cparams("arbitrary"),
        name="mixer_b_prompt",
    )(x_all, *pend_args, mod_p, mod_p, mod_p, ng, w_in, w_grp, b_grp, scale, w_out)

    r = SAMPLE_ROWS
    sblk = T_PROMPT // r
    st2 = state[lb].reshape(DEC_BATCH, POOL_BUF * D)
    pend_specs, pend_args = _pending_sample(pending, mod_s, layer, r)
    x_all, st_s = pl.pallas_call(
        functools.partial(_mix_b_sample_kernel, pending=has_pending),
        grid=(DEC_BATCH // r,),
        in_specs=[pl.BlockSpec((r, D), lambda i: (sblk + i, 0))] + pend_specs
        + _mod_specs_sample(layer, (0, 1, 2), r) + [_full((1, D))]
        + [pl.BlockSpec((r, POOL_BUF * D), lambda i: (i, 0))] + common,
        out_specs=[
            pl.BlockSpec((r, D), lambda i: (sblk + i, 0)),
            pl.BlockSpec((r, POOL_BUF * D), lambda i: (i, 0)),
        ],
        out_shape=[
            jax.ShapeDtypeStruct((T_PAD, D), F32),
            jax.ShapeDtypeStruct((DEC_BATCH, POOL_BUF * D), F32),
        ],
        input_output_aliases={0: 0},
        compiler_params=_cparams("arbitrary"),
        name="mixer_b_sample",
    )(x_all, *pend_args, mod_s, mod_s, mod_s, ng, st2, w_in, w_grp, b_grp, scale, w_out)
    return x_all, st_p, st_s.reshape(DEC_BATCH, POOL_BUF, D)


def _mix_c_glu(x, sh, sc, ng, win_ref, bin_ref):
    h = (_rms(x, ng) * (1.0 + sc) + sh).astype(BF16)
    ag = _dot(h, win_ref[...]) + bin_ref[...]
    return ag[:, :D] * jax.nn.sigmoid(ag[:, D:])


def _mix_c_tail(conv, lng_ref, lnb_ref, wout_ref, bout_ref):
    z = _ln(conv, lng_ref[...], lnb_ref[...])
    z = z * jax.nn.sigmoid(z)
    return _dot(z.astype(BF16), wout_ref[...]) + bout_ref[...]


def _mix_c_prompt_kernel(*refs, pending):
    x, refs = _residual_in(refs, pending)
    (sh_ref, sc_ref, gt_ref, ng_ref, win_ref, bin_ref, wdw_ref, bdw_ref, lng_ref, lnb_ref,
     wout_ref, bout_ref, x1_ref, st_ref, full_ref, conv_ref, *shift_refs) = refs
    i = pl.program_id(0)
    j = i % TILES_PER_SEQ
    glu = _mix_c_glu(x, sh_ref[...], sc_ref[...], ng_ref[...], win_ref, bin_ref)

    @pl.when(j == 0)
    def _():
        full_ref[0:CONV_CARRY, :] = jnp.zeros((CONV_CARRY, D), F32)

    full_ref[CONV_CARRY:CONV_CARRY + TM, :] = glu
    off = CONV_CARRY - CONV_BUF
    span = TM + (off + CONV_BUF) // SUBLANES * SUBLANES - SUBLANES
    for c in range(D // LANES):
        cols = slice(c * LANES, (c + 1) * LANES)
        shift_ref = shift_refs[c % len(shift_refs)]
        for r in range(1, SUBLANES):
            shift_ref[r - 1] = full_ref[pl.ds(r, span), cols]
        for rb in range(TM // CONV_ROWS):
            acc = None
            for k in range(CONV_WIDTH):
                q, r = divmod(off + k, SUBLANES)
                start = q * SUBLANES + rb * CONV_ROWS
                if r == 0:
                    src = full_ref[pl.ds(start, CONV_ROWS), cols]
                else:
                    src = shift_ref[r - 1, pl.ds(start, CONV_ROWS), :]
                term = src * wdw_ref[k:k + 1, cols]
                acc = term if acc is None else acc + term
            conv_ref[pl.ds(rb * CONV_ROWS, CONV_ROWS), cols] = acc + bdw_ref[:, cols]
    y = _mix_c_tail(conv_ref[...], lng_ref, lnb_ref, wout_ref, bout_ref)
    x1_ref[...] = x + gt_ref[...] * y

    @pl.when(j == TILES_PER_SEQ - 1)
    def _():
        st_ref[...] = full_ref[pl.ds(CONV_CARRY + TM - CONV_BUF, CONV_BUF), :]

    full_ref[0:CONV_CARRY, :] = full_ref[pl.ds(TM, CONV_CARRY), :]


def _mix_c_sample_kernel(*refs, pending):
    x, refs = _residual_in(refs, pending)
    (sh_ref, sc_ref, gt_ref, ng_ref, st_ref, win_ref, bin_ref, wdw_ref, bdw_ref, lng_ref,
     lnb_ref, wout_ref, bout_ref, x1_ref, nst_ref) = refs
    glu = _mix_c_glu(x, sh_ref[...], sc_ref[...], ng_ref[...], win_ref, bin_ref)
    acc = glu * wdw_ref[CONV_BUF:CONV_BUF + 1, :]
    for k in range(CONV_BUF):
        acc = acc + st_ref[:, k * D:(k + 1) * D] * wdw_ref[k:k + 1, :]
    y = _mix_c_tail(acc + bdw_ref[...], lng_ref, lnb_ref, wout_ref, bout_ref)
    x1_ref[...] = x + gt_ref[...] * y
    nst_ref[:, 0:(CONV_BUF - 1) * D] = st_ref[:, D:CONV_BUF * D]
    nst_ref[:, (CONV_BUF - 1) * D:CONV_BUF * D] = glu


def _mixer_c(x_all, pending, mod_p, mod_s, ng, layer, lc, state, w_in, b_in, w_dw, b_dw, ln_g,
             ln_b, w_out, b_out):
    common = [
        _layer_block((D, 2 * D), lc), _layer_block((1, 2 * D), lc),
        _layer_block((CONV_WIDTH, D), lc), _layer_block((1, D), lc), _layer_block((1, D), lc),
        _layer_block((1, D), lc), _layer_block((D, D), lc), _layer_block((1, D), lc),
    ]
    has_pending = pending is not None
    pend_specs, pend_args = _pending_prompt(pending, mod_p, layer)
    x_all, st_p = pl.pallas_call(
        functools.partial(_mix_c_prompt_kernel, pending=has_pending),
        grid=(N_PROMPT_TILES,),
        in_specs=[pl.BlockSpec((TM, D), lambda i: (i, 0))] + pend_specs
        + _mod_specs_prompt(layer, (0, 1, 2)) + [_full((1, D))] + common,
        out_specs=[
            pl.BlockSpec((TM, D), lambda i: (i, 0)),
            pl.BlockSpec((None, CONV_BUF, D), lambda i: (i // TILES_PER_SEQ, 0, 0)),
        ],
        out_shape=[
            jax.ShapeDtypeStruct((T_PAD, D), F32),
            jax.ShapeDtypeStruct((BATCH, CONV_BUF, D), F32),
        ],
        scratch_shapes=[pltpu.VMEM((TM + CONV_CARRY, D), F32), pltpu.VMEM((TM, D), F32)]
        + [pltpu.VMEM((SUBLANES - 1, TM + CONV_CARRY - SUBLANES, LANES), F32)] * 2,
        input_output_aliases={0: 0},
        compiler_params=_cparams("arbitrary"),
        name="mixer_c_prompt",
    )(x_all, *pend_args, mod_p, mod_p, mod_p, ng, w_in, b_in, w_dw, b_dw, ln_g, ln_b, w_out,
      b_out)

    r = SAMPLE_ROWS
    sblk = T_PROMPT // r
    st2 = state[lc].reshape(DEC_BATCH, CONV_BUF * D)
    pend_specs, pend_args = _pending_sample(pending, mod_s, layer, r)
    x_all, st_s = pl.pallas_call(
        functools.partial(_mix_c_sample_kernel, pending=has_pending),
        grid=(DEC_BATCH // r,),
        in_specs=[pl.BlockSpec((r, D), lambda i: (sblk + i, 0))] + pend_specs
        + _mod_specs_sample(layer, (0, 1, 2), r) + [_full((1, D))]
        + [pl.BlockSpec((r, CONV_BUF * D), lambda i: (i, 0))] + common,
        out_specs=[
            pl.BlockSpec((r, D), lambda i: (sblk + i, 0)),
            pl.BlockSpec((r, CONV_BUF * D), lambda i: (i, 0)),
        ],
        out_shape=[
            jax.ShapeDtypeStruct((T_PAD, D), F32),
            jax.ShapeDtypeStruct((DEC_BATCH, CONV_BUF * D), F32),
        ],
        input_output_aliases={0: 0},
        compiler_params=_cparams("arbitrary"),
        name="mixer_c_sample",
    )(x_all, *pend_args, mod_s, mod_s, mod_s, ng, st2, w_in, b_in, w_dw, b_dw, ln_g, ln_b,
      w_out, b_out)
    return x_all, st_p, st_s.reshape(DEC_BATCH, CONV_BUF, D)


def _tile_mod(i, p_ref, s_ref):
    s_rows = jnp.concatenate([s_ref[...], jnp.zeros((TM - DEC_BATCH, D), F32)], axis=0)
    return jnp.where(i >= N_PROMPT_TILES, s_rows, p_ref[...])


def _unified_mod_specs(layer, chunks):
    specs = []
    for c in chunks:
        specs.append(pl.BlockSpec(
            (None, None, 1, D),
            lambda i, c=c: (layer, jnp.minimum(i // TILES_PER_SEQ, BATCH - 1), 0, c)))
        specs.append(pl.BlockSpec((None, DEC_BATCH, D), lambda i, c=c: (layer, 0, c)))
    return specs


def _router_kernel(x_ref, shp_ref, shs_ref, scp_ref, scs_ref, ng_ref, wr_ref, br_ref,
                   h2_ref, meta_ref, wts_ref, cnt_ref, carry_ref):
    i = pl.program_id(0)

    @pl.when(i == 0)
    def _():
        carry_ref[...] = jnp.zeros((1, LANES), F32)

    sh = _tile_mod(i, shp_ref, shs_ref)
    sc = _tile_mod(i, scp_ref, scs_ref)
    h2 = _rms(x_ref[...], ng_ref[...]) * (1.0 + sc) + sh
    h2_ref[...] = _pack_bf16_pairs(h2)
    lg = _dot(h2.astype(BF16), wr_ref[...]) + br_ref[...]
    lane = lax.broadcasted_iota(jnp.int32, (TM, LANES), 1)
    neg = -jnp.inf
    is_g = lane < N_GROUPS
    gm = jnp.where(is_g, lg, neg)
    gmax = jnp.max(gm, axis=1, keepdims=True)
    lane_f = lane.astype(F32)
    far = float(LANES)
    gsel = jnp.min(jnp.where(gm == gmax, lane_f, far), axis=1, keepdims=True).astype(jnp.int32)
    gsum = jnp.sum(jnp.where(is_g, jnp.exp(lg - gmax), 0.0), axis=1, keepdims=True)
    g_w = 1.0 / gsum
    e_lane = lane - EXPERT_LANE0
    in_grp = (e_lane >= 0) & (e_lane < N_EXPERTS) & (
        lax.shift_right_arithmetic(e_lane, EXPERTS_PER_GROUP.bit_length() - 1) == gsel)
    em = jnp.where(in_grp, lg, neg)
    m1 = jnp.max(em, axis=1, keepdims=True)
    i1 = jnp.min(jnp.where(em == m1, lane_f, far), axis=1, keepdims=True).astype(jnp.int32)
    em2 = jnp.where(lane == i1, neg, em)
    m2 = jnp.max(em2, axis=1, keepdims=True)
    i2 = jnp.min(jnp.where(em2 == m2, lane_f, far), axis=1, keepdims=True).astype(jnp.int32)
    e2 = jnp.exp(m2 - m1)
    den = 1.0 + e2
    w1 = (1.0 / den) * g_w
    w2 = (e2 / den) * g_w

    hit1 = lane == i1
    hit2 = lane == i2
    assign = jnp.where(hit1 | hit2, 1.0, 0.0).astype(BF16)
    r = lax.broadcasted_iota(jnp.int32, (TM, TM), 0)
    c = lax.broadcasted_iota(jnp.int32, (TM, TM), 1)
    before = (c < r).astype(BF16)
    seen = _dot(before, assign) + carry_ref[...]
    rank1 = jnp.sum(jnp.where(hit1, seen, 0.0), axis=1, keepdims=True)
    rank2 = jnp.sum(jnp.where(hit2, seen, 0.0), axis=1, keepdims=True)
    carry_ref[...] = carry_ref[...] + jnp.sum(assign.astype(F32), axis=0, keepdims=True)
    cnt_ref[...] = carry_ref[...].astype(jnp.int32)

    ex1 = (i1 - EXPERT_LANE0).astype(F32)
    ex2 = (i2 - EXPERT_LANE0).astype(F32)
    fields = jnp.where(lane == 0, ex1, jnp.where(lane == 1, ex2,
                       jnp.where(lane == 2, rank1, jnp.where(lane == 3, rank2, 0.0))))
    meta_ref[...] = fields.T[0:META_ROWS, :]
    wts_ref[...] = jnp.where(lane == 0, w1, jnp.where(lane == 1, w2, 0.0))


def _router(x_all, mod_p, mod_s, ng, layer, w_r, b_r):
    return pl.pallas_call(
        _router_kernel,
        grid=(N_TILES,),
        in_specs=[pl.BlockSpec((TM, D), lambda i: (i, 0))]
        + _unified_mod_specs(layer, (3, 4))
        + [_full((1, D)), _full((D, LANES)), _full((1, LANES))],
        out_specs=[
            pl.BlockSpec((TM, D // 2), lambda i: (i, 0)),
            pl.BlockSpec((META_ROWS, TM), lambda i: (0, i)),
            pl.BlockSpec((TM, LANES), lambda i: (i, 0)),
            pl.BlockSpec((1, LANES), lambda i: (0, 0)),
        ],
        out_shape=[
            jax.ShapeDtypeStruct((T_PAD, D // 2), jnp.int32),
            jax.ShapeDtypeStruct((META_ROWS, T_PAD), F32),
            jax.ShapeDtypeStruct((T_PAD, LANES), F32),
            jax.ShapeDtypeStruct((1, LANES), jnp.int32),
        ],
        scratch_shapes=[pltpu.VMEM((1, LANES), F32)],
        compiler_params=_cparams("arbitrary"),
        name="router",
    )(x_all, mod_p, mod_s, mod_p, mod_s, ng, w_r, b_r)


SC_INDEX_MAX = 128
SC_ROWS_BYTES = 256 * 1024


def _gather_chunk(per_worker, row_bytes):
    top = min(SC_INDEX_MAX, SC_ROWS_BYTES // row_bytes) // SUBLANES * SUBLANES
    for ch in range(top, SUBLANES - 1, -SUBLANES):
        if per_worker % ch == 0:
            return ch
    raise ValueError(per_worker)


def _sc_gather(table, idx):
    nc, nw = SC_CORES, SC_CORES * SC_SUBCORES
    m = idx.shape[0]
    per_w = m // nw
    assert per_w * nw == m
    width = table.shape[1]
    ch = _gather_chunk(per_w, width * table.dtype.itemsize)
    mesh = plsc.VectorSubcoreMesh(core_axis_name="c", subcore_axis_name="s")

    @functools.partial(
        pl.kernel,
        out_type=jax.ShapeDtypeStruct((m, width), table.dtype),
        mesh=mesh,
        scratch_types=[
            pltpu.VMEM((ch,), jnp.int32),
            pltpu.VMEM((ch, width), table.dtype),
            pltpu.SemaphoreType.DMA,
        ],
    )
    def gather_kernel(t_hbm, i_hbm, o_hbm, idx_v, rows_v, sem):
        wid = lax.axis_index("s") * nc + lax.axis_index("c")
        base = wid * per_w

        @pl.loop(0, per_w // ch)
        def _(j):
            off = pl.multiple_of(base + j * ch, 8)
            pltpu.sync_copy(i_hbm.at[pl.ds(off, ch)], idx_v)
            pltpu.async_copy(t_hbm.at[idx_v], rows_v, sem).wait()
            pltpu.sync_copy(rows_v, o_hbm.at[pl.ds(off, ch)])

    return gather_kernel(table, idx)


def _sc_scatter2(rows, dest0, dest1, n_out):
    nc, nw = SC_CORES, SC_CORES * SC_SUBCORES
    n, width = rows.shape
    per_w = n // nw
    assert per_w * nw == n
    ch = _gather_chunk(per_w, width * rows.dtype.itemsize)
    mesh = plsc.VectorSubcoreMesh(core_axis_name="c", subcore_axis_name="s")

    @functools.partial(
        pl.kernel,
        out_type=jax.ShapeDtypeStruct((n_out, width), rows.dtype),
        mesh=mesh,
        scratch_types=[
            pltpu.VMEM((ch,), jnp.int32),
            pltpu.VMEM((ch,), jnp.int32),
            pltpu.VMEM((ch, width), rows.dtype),
            pltpu.SemaphoreType.DMA,
            pltpu.SemaphoreType.DMA,
        ],
    )
    def scatter_kernel(r_hbm, d0_hbm, d1_hbm, o_hbm, i0_v, i1_v, rows_v, sem0, sem1):
        wid = lax.axis_index("s") * nc + lax.axis_index("c")
        base = wid * per_w

        @pl.loop(0, per_w // ch)
        def _(j):
            off = pl.multiple_of(base + j * ch, 8)
            pltpu.sync_copy(d0_hbm.at[pl.ds(off, ch)], i0_v)
            pltpu.sync_copy(d1_hbm.at[pl.ds(off, ch)], i1_v)
            pltpu.sync_copy(r_hbm.at[pl.ds(off, ch)], rows_v)
            c0 = pltpu.async_copy(rows_v, o_hbm.at[i0_v], sem0)
            c1 = pltpu.async_copy(rows_v, o_hbm.at[i1_v], sem1)
            c0.wait()
            c1.wait()

    return scatter_kernel(rows, dest0, dest1)


def _expert_kernel(be_ref, first_ref, nvalid_ref, next_ref, nused_ref, x_ref, wg_hbm, wu_hbm,
                   wd_hbm, y_ref, wgf_ref, wuf_ref, wdf_ref, wgb_ref, wub_ref, wdb_ref, sems,
                   *, layer):
    b = pl.program_id(0)

    def weight_copies(e):
        return (
            pltpu.make_async_copy(wg_hbm.at[layer, e], wgf_ref, sems.at[0]),
            pltpu.make_async_copy(wu_hbm.at[layer, e], wuf_ref, sems.at[1]),
            pltpu.make_async_copy(wd_hbm.at[layer, e], wdf_ref, sems.at[2]),
        )

    @pl.when(b == 0)
    def _():
        for cp in weight_copies(be_ref[0]):
            cp.start()

    @pl.when(b < nused_ref[0])
    def _():
        @pl.when(first_ref[b] == 1)
        def _():
            for cp in weight_copies(be_ref[b]):
                cp.wait()
            wgb_ref[...] = wgf_ref[...].astype(BF16)
            wub_ref[...] = wuf_ref[...].astype(BF16)
            wdb_ref[...] = wdf_ref[...].astype(BF16)

            @pl.when(next_ref[b] >= 0)
            def _():
                for cp in weight_copies(next_ref[b]):
                    cp.start()

        live = lax.broadcasted_iota(jnp.int32, (BM, 1), 0) < nvalid_ref[b]
        x = _unpack_bf16_pairs(jnp.where(live, x_ref[...], 0), BF16)
        g = _dot(x, wgb_ref[...])
        u = _dot(x, wub_ref[...])
        hmid = (g * jax.nn.sigmoid(g)) * u
        y_ref[...] = _pack_bf16_pairs(_dot(hmid.astype(BF16), wdb_ref[...]))

    @pl.when(b >= nused_ref[0])
    def _():
        y_ref[...] = jnp.zeros((BM, D // 2), jnp.int32)


def _experts(x_sorted, block_expert, first, nvalid, next_expert, nused, layer, w_gate, w_up,
             w_down):
    def row_map(b, be, fi, nv, nx, nu):
        return (jnp.minimum(b, nu[0] - 1), 0)

    def out_map(b, be, fi, nv, nx, nu):
        return (b, 0)

    grid_spec = pltpu.PrefetchScalarGridSpec(
        num_scalar_prefetch=5,
        grid=(N_BLOCKS,),
        in_specs=[
            pl.BlockSpec((BM, D // 2), row_map),
            pl.BlockSpec(memory_space=pl.ANY),
            pl.BlockSpec(memory_space=pl.ANY),
            pl.BlockSpec(memory_space=pl.ANY),
        ],
        out_specs=pl.BlockSpec((BM, D // 2), out_map),
        scratch_shapes=[
            pltpu.VMEM((D, D_EXPERT), F32),
            pltpu.VMEM((D, D_EXPERT), F32),
            pltpu.VMEM((D_EXPERT, D), F32),
            pltpu.VMEM((D, D_EXPERT), BF16),
            pltpu.VMEM((D, D_EXPERT), BF16),
            pltpu.VMEM((D_EXPERT, D), BF16),
            pltpu.SemaphoreType.DMA((3,)),
        ],
    )
    return pl.pallas_call(
        functools.partial(_expert_kernel, layer=layer),
        grid_spec=grid_spec,
        out_shape=jax.ShapeDtypeStruct((P_ROWS, D // 2), jnp.int32),
        compiler_params=_cparams("arbitrary"),
        name="experts",
    )(block_expert, first, nvalid, next_expert, nused, x_sorted, w_gate, w_up, w_down)


def _combine_final_kernel(x_ref, g0_ref, g1_ref, wts_ref, gtp_ref, gts_ref, fg_ref, yp_ref,
                          ys_ref):
    i = pl.program_id(0)
    gate = _tile_mod(i, gtp_ref, gts_ref)
    y = _rms(_plus_moe(x_ref[...], g0_ref, g1_ref, wts_ref, gate), fg_ref[...])

    @pl.when(i < N_PROMPT_TILES)
    def _():
        yp_ref[...] = y

    @pl.when(i == N_PROMPT_TILES)
    def _():
        ys_ref[...] = y[0:DEC_BATCH, :]


def _combine_final(x_all, gathered, wts, mod_p, mod_s, layer, final_g):
    in_specs = [
        pl.BlockSpec((TM, D), lambda i: (i, 0)),
        pl.BlockSpec((TM, D // 2), lambda i: (i, 0)),
        pl.BlockSpec((TM, D // 2), lambda i: (i + N_TILES, 0)),
        pl.BlockSpec((TM, LANES), lambda i: (i, 0)),
    ] + _unified_mod_specs(layer, (5,))
    return pl.pallas_call(
        _combine_final_kernel,
        grid=(N_PROMPT_TILES + 1,),
        in_specs=in_specs + [_full((1, D))],
        out_specs=[
            pl.BlockSpec((TM, D), lambda i: (jnp.minimum(i, N_PROMPT_TILES - 1), 0)),
            pl.BlockSpec((DEC_BATCH, D), lambda i: (0, 0)),
        ],
        out_shape=[
            jax.ShapeDtypeStruct((T_PROMPT, D), F32),
            jax.ShapeDtypeStruct((DEC_BATCH, D), F32),
        ],
        compiler_params=_cparams("arbitrary"),
        name="combine_final",
    )(x_all, gathered, gathered, wts, mod_p, mod_s, final_g)


def _plan_kernel(cnt_ref, meta_ref, d0_ref, d1_ref, be_ref, first_ref, nvalid_ref, next_ref,
                 nused_ref, pstart_ref, nxt_ref):
    shift = BM.bit_length() - 1
    assert 1 << shift == BM

    def count(e):
        return cnt_ref[0, EXPERT_LANE0 + e]

    def scan_back(k, nxt):
        e = N_EXPERTS - 1 - k
        nxt_ref[e] = nxt
        return jnp.where(count(e) > 0, e, nxt)

    lax.fori_loop(0, N_EXPERTS, scan_back, jnp.int32(-1))

    def fill(e, blk):
        c = count(e)
        pstart_ref[e] = blk << shift

        def one(j, carry):
            be_ref[blk + j] = e
            first_ref[blk + j] = jnp.where(j == 0, 1, 0)
            nvalid_ref[blk + j] = jnp.minimum(c - (j << shift), BM)
            next_ref[blk + j] = nxt_ref[e]
            return carry

        n_blk = (c + (BM - 1)) >> shift
        lax.fori_loop(0, n_blk, one, 0)
        return blk + n_blk

    used = lax.fori_loop(0, N_EXPERTS, fill, jnp.int32(0))
    nused_ref[0] = used

    def tail(b, carry):
        be_ref[b] = 0
        first_ref[b] = 0
        nvalid_ref[b] = 0
        next_ref[b] = -1
        return carry

    lax.fori_loop(used, N_BLOCKS, tail, 0)

    m = meta_ref[...]
    expert = m[0:2, :]
    dest = m[2:4, :]
    for e in range(N_EXPERTS):
        dest = dest + jnp.where(expert == float(e), pstart_ref[e].astype(F32), 0.0)
    dest = dest.astype(jnp.int32)
    d0_ref[...] = dest[0:1, :]
    d1_ref[...] = dest[1:2, :]


def _plan(cnt, meta):
    smem = pl.BlockSpec(memory_space=pltpu.SMEM)
    blocks = jax.ShapeDtypeStruct((N_BLOCKS,), jnp.int32)
    return pl.pallas_call(
        _plan_kernel,
        grid=(1,),
        in_specs=[smem, _full((META_ROWS, T_PAD))],
        out_specs=[_full((1, T_PAD)), _full((1, T_PAD)), smem, smem, smem, smem, smem],
        out_shape=[
            jax.ShapeDtypeStruct((1, T_PAD), jnp.int32),
            jax.ShapeDtypeStruct((1, T_PAD), jnp.int32),
            blocks, blocks, blocks, blocks,
            jax.ShapeDtypeStruct((1,), jnp.int32),
        ],
        scratch_shapes=[pltpu.SMEM((N_EXPERTS,), jnp.int32), pltpu.SMEM((N_EXPERTS,), jnp.int32)],
        compiler_params=_cparams("arbitrary"),
        name="plan",
    )(cnt, meta)


def _moe(x_all, mod_p, mod_s, ng, layer, w_r, b_r, w_gate, w_up, w_down):
    h2, meta, wts, cnt = _router(x_all, mod_p, mod_s, ng, layer, w_r, b_r)
    dest0, dest1, block_expert, first, nvalid, next_expert, nused = _plan(cnt, meta)
    dest0, dest1 = dest0.reshape(T_PAD), dest1.reshape(T_PAD)
    x_sorted = _sc_scatter2(h2, dest0, dest1, P_ROWS)
    y_sorted = _experts(x_sorted, block_expert, first, nvalid, next_expert, nused, layer,
                        w_gate, w_up, w_down)
    gathered = _sc_gather(y_sorted, jnp.concatenate([dest0, dest1]))
    return gathered, wts


def kernel(x_prompt, x_sample, c_prompt, c_sample, state_pool, state_conv, ada_w, ada_b, norm_g, final_g, a_w_in, a_ln_g, a_ln_b, a_w_s, a_b_s, a_w_out, b_w_in, b_w_grp, b_b_grp, b_scale, b_w_out, c_w_in, c_b_in, c_w_dw, c_b_dw, c_ln_g, c_ln_b, c_w_out, c_b_out, moe_w_grp, moe_b_grp, moe_w_exp, moe_b_exp, moe_w_gate, moe_w_up, moe_w_down):
    x_all = jnp.concatenate([
        x_prompt.reshape(T_PROMPT, D), x_sample.reshape(DEC_BATCH, D),
        jnp.zeros((T_PAD - T_ALL, D), F32)], axis=0)
    mods = _ada_mods(jnp.concatenate([c_prompt, c_sample], axis=0), ada_w, ada_b)
    mod_p = mods[:, :BATCH].reshape(DEPTH, BATCH, 1, 6 * D)
    mod_s = mods[:, BATCH:]

    a_w_in_b, a_w_out_b = a_w_in.astype(BF16), a_w_out.astype(BF16)
    b_w_in_b, b_w_grp_b, b_w_out_b = b_w_in.astype(BF16), b_w_grp.astype(BF16), b_w_out.astype(BF16)
    c_w_in_b, c_w_out_b = c_w_in.astype(BF16), c_w_out.astype(BF16)
    w_r = jnp.pad(jnp.concatenate([moe_w_grp, moe_w_exp], axis=2),
                  ((0, 0), (0, 0), (0, LANES - N_GROUPS - N_EXPERTS))).astype(BF16)
    b_r = jnp.pad(jnp.concatenate([moe_b_grp, moe_b_exp], axis=1),
                  ((0, 0), (0, LANES - N_GROUPS - N_EXPERTS)))

    new_a_p, new_a_s, new_b_p, new_b_s, new_c_p, new_c_s = [], [], [], [], [], []
    ia = ib = ic = 0
    pending = None
    for layer in range(DEPTH):
        ng1 = norm_g[layer, 0].reshape(1, D)
        ng2 = norm_g[layer, 1].reshape(1, D)
        kind = layer % 3
        if kind == 0:
            x_all, st_p, st_s = _mixer_a(
                x_all, pending, mod_p, mod_s, ng1, layer, ia, a_w_in_b, a_ln_g.reshape(-1, 1, D),
                a_ln_b.reshape(-1, 1, D), a_w_s, a_b_s, a_w_out_b)
            new_a_p.append(st_p)
            new_a_s.append(st_s)
            ia += 1
        elif kind == 1:
            x_all, st_p, st_s = _mixer_b(
                x_all, pending, mod_p, mod_s, ng1, layer, ib, state_pool, b_w_in_b, b_w_grp_b,
                b_b_grp.reshape(-1, len(POOL_WINDOWS), 1, B_GROUP_DIM),
                b_scale.reshape(-1, 1, D), b_w_out_b)
            new_b_p.append(st_p)
            new_b_s.append(st_s)
            ib += 1
        else:
            x_all, st_p, st_s = _mixer_c(
                x_all, pending, mod_p, mod_s, ng1, layer, ic, state_conv, c_w_in_b,
                c_b_in.reshape(-1, 1, 2 * D), c_w_dw, c_b_dw.reshape(-1, 1, D),
                c_ln_g.reshape(-1, 1, D), c_ln_b.reshape(-1, 1, D), c_w_out_b,
                c_b_out.reshape(-1, 1, D))
            new_c_p.append(st_p)
            new_c_s.append(st_s)
            ic += 1
        pending = _moe(x_all, mod_p, mod_s, ng2, layer, w_r[layer],
                       b_r[layer].reshape(1, LANES), moe_w_gate, moe_w_up, moe_w_down)

    y_p, y_s = _combine_final(x_all, *pending, mod_p, mod_s, DEPTH - 1, final_g.reshape(1, D))
    return (y_p.reshape(BATCH, SEQ, D), y_s.reshape(DEC_BATCH, 1, D),
            jnp.stack(new_a_p), jnp.stack(new_a_s), jnp.stack(new_b_p), jnp.stack(new_b_s),
            jnp.stack(new_c_p), jnp.stack(new_c_s))
```

```python
import functools

import jax
import jax.numpy as jnp
from jax import lax
from jax.experimental import pallas as pl
from jax.experimental.pallas import tpu as pltpu
from jax.experimental.pallas import tpu_sc as plsc

F32 = jnp.float32
BF16 = jnp.bfloat16

D = 1024
BATCH = 8
SEQ = 2048
DEPTH = 4
DEC_BATCH = 128
PAST_LEN = 16384
CHUNK = 128
A_HEADS = 8
POOL_WINDOWS = (2, 4, 8, 16)
B_GROUP_DIM = D // len(POOL_WINDOWS)
POOL_BUF = max(POOL_WINDOWS) - 1
CONV_WIDTH = 31
CONV_BUF = CONV_WIDTH - 1
N_GROUPS = 4
EXPERTS_PER_GROUP = 8
N_EXPERTS = N_GROUPS * EXPERTS_PER_GROUP
D_EXPERT = D // 2
EPS = 1e-6

LANES = 128
SUBLANES = 8
CONV_ROWS = 128
SC_CORES = 2
SC_SUBCORES = 16
TM = 512
TILES_PER_SEQ = SEQ // TM
T_PROMPT = BATCH * SEQ
T_ALL = T_PROMPT + DEC_BATCH
N_TILES = -(-T_ALL // TM)
T_PAD = N_TILES * TM
N_PROMPT_TILES = T_PROMPT // TM
SAMPLE_ROWS = 32
BM = 512
N_BLOCKS = -(-(2 * T_PAD) // BM) + N_EXPERTS
P_ROWS = N_BLOCKS * BM
META_ROWS = 8
POOL_CARRY = 16
CONV_CARRY = 32
EXPERT_LANE0 = N_GROUPS
VMEM_LIMIT = 56 * 1024 * 1024


def _cparams(*sem):
    return pltpu.CompilerParams(dimension_semantics=sem, vmem_limit_bytes=VMEM_LIMIT)


def _rms(x, g):
    return x * lax.rsqrt(jnp.mean(x * x, axis=-1, keepdims=True) + EPS) * g


def _ln(x, g, b):
    mu = jnp.mean(x, axis=-1, keepdims=True)
    xc = x - mu
    var = jnp.mean(xc * xc, axis=-1, keepdims=True)
    return xc * lax.rsqrt(var + EPS) * g + b


def _dot(a, b):
    return jnp.dot(a, b, preferred_element_type=F32)


def _pack_bf16_pairs(v):
    w = v.shape[1] // 2
    lo = lax.bitcast_convert_type(v[:, :w].astype(BF16).astype(F32), jnp.uint32)
    hi = lax.bitcast_convert_type(v[:, w:].astype(BF16).astype(F32), jnp.uint32)
    return lax.bitcast_convert_type(lax.shift_right_logical(lo, jnp.uint32(16)) | hi, jnp.int32)


def _unpack_bf16_pairs(words, dtype):
    u = lax.bitcast_convert_type(words, jnp.uint32)
    lo = lax.bitcast_convert_type(lax.shift_left(u, jnp.uint32(16)), F32)
    hi = lax.bitcast_convert_type(u & jnp.uint32(0xFFFF0000), F32)
    return jnp.concatenate([lo.astype(dtype), hi.astype(dtype)], axis=1)


def _ada_kernel(c_ref, w_ref, b_ref, o_ref):
    c = c_ref[...]
    s = (c * jax.nn.sigmoid(c)).astype(BF16)
    o_ref[...] = _dot(s, w_ref[...].astype(BF16)) + b_ref[...]


def _ada_mods(c_all, ada_w, ada_b):
    n = c_all.shape[0]
    tn = 2048
    return pl.pallas_call(
        _ada_kernel,
        grid=(DEPTH, 6 * D // tn),
        in_specs=[
            pl.BlockSpec((n, D), lambda l, j: (0, 0)),
            pl.BlockSpec((None, D, tn), lambda l, j: (l, 0, j)),
            pl.BlockSpec((None, 1, tn), lambda l, j: (l, 0, j)),
        ],
        out_specs=pl.BlockSpec((None, n, tn), lambda l, j: (l, 0, j)),
        out_shape=jax.ShapeDtypeStruct((DEPTH, n, 6 * D), F32),
        compiler_params=_cparams("parallel", "parallel"),
        name="ada_mods",
    )(c_all, ada_w, ada_b.reshape(DEPTH, 1, 6 * D))


def _mod_specs_prompt(layer, chunks):
    return [
        pl.BlockSpec((None, None, 1, D),
                     lambda i, c=c: (layer, jnp.minimum(i // TILES_PER_SEQ, BATCH - 1), 0, c))
        for c in chunks
    ]


def _mod_specs_sample(layer, chunks, rows, row_block0=0):
    return [
        pl.BlockSpec((None, rows, D), lambda i, c=c: (layer, i + row_block0, c)) for c in chunks
    ]


def _full(shape):
    nd = len(shape)
    return pl.BlockSpec(shape, lambda i: (0,) * nd)


def _layer_block(shape, layer):
    nd = len(shape)
    return pl.BlockSpec((None,) + shape, lambda i: (layer,) + (0,) * nd)


def _plus_moe(x, g0_ref, g1_ref, wts_ref, gate):
    w = wts_ref[...]
    y0 = _unpack_bf16_pairs(g0_ref[...], F32)
    y1 = _unpack_bf16_pairs(g1_ref[...], F32)
    return x + gate * (w[:, 0:1] * y0 + w[:, 1:2] * y1)


def _residual_in(refs, pending):
    if not pending:
        return refs[0][...], refs[1:]
    x_ref, g0_ref, g1_ref, wts_ref, gate_ref = refs[:5]
    return _plus_moe(x_ref[...], g0_ref, g1_ref, wts_ref, gate_ref[...]), refs[5:]


def _pending_prompt(pending, mod_p, layer):
    if pending is None:
        return [], []
    gathered, wts = pending
    specs = [
        pl.BlockSpec((TM, D // 2), lambda i: (i, 0)),
        pl.BlockSpec((TM, D // 2), lambda i: (i + N_TILES, 0)),
        pl.BlockSpec((TM, LANES), lambda i: (i, 0)),
    ] + _mod_specs_prompt(layer - 1, (5,))
    return specs, [gathered, gathered, wts, mod_p]


def _pending_sample(pending, mod_s, layer, rows):
    if pending is None:
        return [], []
    gathered, wts = pending
    first = T_PROMPT // rows
    second = (T_PAD + T_PROMPT) // rows
    specs = [
        pl.BlockSpec((rows, D // 2), lambda i: (first + i, 0)),
        pl.BlockSpec((rows, D // 2), lambda i: (second + i, 0)),
        pl.BlockSpec((rows, LANES), lambda i: (first + i, 0)),
    ] + _mod_specs_sample(layer - 1, (5,), rows)
    return specs, [gathered, gathered, wts, mod_s]


def _mix_a_front(x, sh, sc, ng, win_ref, lng_ref, lnb_ref):
    h = (_rms(x, ng) * (1.0 + sc) + sh).astype(BF16)
    z = jax.nn.gelu(_dot(h, win_ref[...]))
    u = z[:, :D]
    v = _ln(z[:, D:], lng_ref[...], lnb_ref[...])
    return u, v


def _mix_a_prompt_kernel(*refs, pending, first):
    x, refs = _residual_in(refs, pending)
    (sh_ref, sc_ref, gt_ref, ng_ref, win_ref, lng_ref, lnb_ref, ws_ref, bs_ref, wout_ref,
     x1_ref, cv_ref, mixed_ref) = refs
    i = pl.program_id(0)
    u, v = _mix_a_front(x, sh_ref[...], sc_ref[...], ng_ref[...], win_ref, lng_ref, lnb_ref)
    vb = v.astype(BF16)
    n_chunks = TM // CHUNK
    row = lax.broadcasted_iota(jnp.int32, (CHUNK, CHUNK), 0)
    col = lax.broadcasted_iota(jnp.int32, (CHUNK, CHUNK), 1)
    tril = row >= col
    hd_dim = D // A_HEADS
    for hd in range(A_HEADS):
        ws = jnp.where(tril, ws_ref[hd], 0.0).astype(BF16)
        cols = slice(hd * hd_dim, (hd + 1) * hd_dim)
        vcat = jnp.concatenate([vb[c * CHUNK:(c + 1) * CHUNK, cols] for c in range(n_chunks)], axis=1)
        m = _dot(ws, vcat)
        for c in range(n_chunks):
            mixed_ref[c * CHUNK:(c + 1) * CHUNK, cols] = m[:, c * hd_dim:(c + 1) * hd_dim] + bs_ref[hd]
    y = _dot((u * mixed_ref[...]).astype(BF16), wout_ref[...])
    out = x + gt_ref[...] * y
    if first:
        out = jnp.where(i < N_PROMPT_TILES, out, 0.0)
    x1_ref[...] = out

    @pl.when(i % TILES_PER_SEQ == TILES_PER_SEQ - 1)
    def _():
        cv_ref[...] = v[TM - CHUNK:, :]


def _mix_a_sample_kernel(*refs, pending, first):
    if first:
        refs = refs[:1] + refs[2:]
    x, refs = _residual_in(refs, pending)
    (sh_ref, sc_ref, gt_ref, ng_ref, win_ref, lng_ref, lnb_ref, wsd_ref, bsd_ref, wout_ref,
     x1_ref, cv_ref) = refs
    u, v = _mix_a_front(x, sh_ref[...], sc_ref[...], ng_ref[...], win_ref, lng_ref, lnb_ref)
    mixed = wsd_ref[...] * v + bsd_ref[...]
    y = _dot((u * mixed).astype(BF16), wout_ref[...])
    x1_ref[...] = x + gt_ref[...] * y
    cv_ref[...] = v


def _mixer_a(x_in, pending, mod_p, mod_s, ng, layer, la, w_in, ln_g, ln_b, w_s, b_s, w_out):
    first = isinstance(x_in, tuple)
    x_all = x_in[0] if first else x_in
    bs_b = jnp.broadcast_to(b_s[la][:, :, None], (A_HEADS, CHUNK, D // A_HEADS))
    common = [
        _layer_block((D, 2 * D), la), _layer_block((1, D), la), _layer_block((1, D), la),
    ]
    has_pending = pending is not None
    pend_specs, pend_args = _pending_prompt(pending, mod_p, layer)
    last_tile = N_PROMPT_TILES - 1
    x_all, cv_p = pl.pallas_call(
        functools.partial(_mix_a_prompt_kernel, pending=has_pending, first=first),
        grid=(N_TILES if first else N_PROMPT_TILES,),
        in_specs=[pl.BlockSpec((TM, D), lambda i: (jnp.minimum(i, last_tile), 0))] + pend_specs
        + _mod_specs_prompt(layer, (0, 1, 2))
        + [_full((1, D))] + common
        + [_layer_block((A_HEADS, CHUNK, CHUNK), la), _full((A_HEADS, CHUNK, D // A_HEADS)),
           _layer_block((D, D), la)],
        out_specs=[
            pl.BlockSpec((TM, D), lambda i: (i, 0)),
            pl.BlockSpec((None, CHUNK, D), lambda i: (jnp.minimum(i, last_tile) // TILES_PER_SEQ, 0, 0)),
        ],
        out_shape=[
            jax.ShapeDtypeStruct((T_PAD, D), F32),
            jax.ShapeDtypeStruct((BATCH, CHUNK, D), F32),
        ],
        scratch_shapes=[pltpu.VMEM((TM, D), F32)],
        input_output_aliases={} if first else {0: 0},
        compiler_params=_cparams("arbitrary"),
        name="mixer_a_prompt",
    )(x_all, *pend_args, mod_p, mod_p, mod_p, ng, w_in, ln_g, ln_b, w_s, bs_b, w_out)

    wsd = jnp.repeat(w_s[la, :, 0, 0], D // A_HEADS).reshape(1, D)
    bsd = jnp.repeat(b_s[la, :, 0], D // A_HEADS).reshape(1, D)
    sblk = T_PROMPT // DEC_BATCH
    pend_specs, pend_args = _pending_sample(pending, mod_s, layer, DEC_BATCH)
    if first:
        x_specs = [pl.BlockSpec((DEC_BATCH, D), lambda i: (0, 0)), pl.BlockSpec(memory_space=pl.ANY)]
        x_args = [x_in[1], x_all]
    else:
        x_specs = [pl.BlockSpec((DEC_BATCH, D), lambda i: (sblk, 0))]
        x_args = [x_all]
    x_all, cv_s = pl.pallas_call(
        functools.partial(_mix_a_sample_kernel, pending=has_pending, first=first),
        grid=(1,),
        in_specs=x_specs + pend_specs
        + _mod_specs_sample(layer, (0, 1, 2), DEC_BATCH)
        + [_full((1, D))] + common
        + [_full((1, D)), _full((1, D)), _layer_block((D, D), la)],
        out_specs=[
            pl.BlockSpec((DEC_BATCH, D), lambda i: (sblk, 0)),
            pl.BlockSpec((DEC_BATCH, D), lambda i: (0, 0)),
        ],
        out_shape=[
            jax.ShapeDtypeStruct((T_PAD, D), F32),
            jax.ShapeDtypeStruct((DEC_BATCH, D), F32),
        ],
        input_output_aliases={len(x_args) - 1: 0},
        compiler_params=_cparams("arbitrary"),
        name="mixer_a_sample",
    )(*x_args, *pend_args, mod_s, mod_s, mod_s, ng, w_in, ln_g, ln_b, wsd, bsd, w_out)
    return x_all, cv_p, cv_s.reshape(DEC_BATCH, 1, D)


def _mix_b_tail(pooled_groups, wgrp_ref, bgrp_ref, scale_ref, wout_ref):
    outs = [
        _dot(pg.astype(BF16), wgrp_ref[g]) + bgrp_ref[g]
        for g, pg in enumerate(pooled_groups)
    ]
    mixed = jnp.concatenate(outs, axis=1) * scale_ref[...]
    return _dot(mixed.astype(BF16), wout_ref[...])


def _mix_b_prompt_kernel(*refs, pending):
    x, refs = _residual_in(refs, pending)
    (sh_ref, sc_ref, gt_ref, ng_ref, win_ref, wgrp_ref, bgrp_ref, scale_ref, wout_ref,
     x1_ref, st_ref, full_ref) = refs
    i = pl.program_id(0)
    j = i % TILES_PER_SEQ
    h = (_rms(x, ng_ref[...]) * (1.0 + sc_ref[...]) + sh_ref[...]).astype(BF16)
    p = _dot(h, win_ref[...])

    @pl.when(j == 0)
    def _():
        full_ref[0:POOL_CARRY, :] = jnp.zeros((POOL_CARRY, D), F32)

    full_ref[POOL_CARRY:POOL_CARRY + TM, :] = p
    pos = j * TM + lax.broadcasted_iota(jnp.int32, (TM, 1), 0)
    pooled = []
    for g, w in enumerate(POOL_WINDOWS):
        cols = slice(g * B_GROUP_DIM, (g + 1) * B_GROUP_DIM)
        s = p[:, cols]
        for k in range(1, w):
            s = s + full_ref[pl.ds(POOL_CARRY - k, TM), cols]
        cnt = jnp.minimum(w, pos + 1).astype(F32)
        pooled.append(s / cnt - p[:, cols])
    y = _mix_b_tail(pooled, wgrp_ref, bgrp_ref, scale_ref, wout_ref)
    x1_ref[...] = x + gt_ref[...] * y

    @pl.when(j == TILES_PER_SEQ - 1)
    def _():
        st_ref[...] = full_ref[pl.ds(POOL_CARRY + TM - POOL_BUF, POOL_BUF), :]

    full_ref[0:POOL_CARRY, :] = full_ref[pl.ds(TM, POOL_CARRY), :]


def _mix_b_sample_kernel(*refs, pending):
    x, refs = _residual_in(refs, pending)
    (sh_ref, sc_ref, gt_ref, ng_ref, st_ref, win_ref, wgrp_ref, bgrp_ref, scale_ref, wout_ref,
     x1_ref, nst_ref) = refs
    h = (_rms(x, ng_ref[...]) * (1.0 + sc_ref[...]) + sh_ref[...]).astype(BF16)
    p = _dot(h, win_ref[...])
    pooled = []
    for g, w in enumerate(POOL_WINDOWS):
        s = p[:, g * B_GROUP_DIM:(g + 1) * B_GROUP_DIM]
        for k in range(1, w):
            s = s + st_ref[:, POOL_BUF - k, g * B_GROUP_DIM:(g + 1) * B_GROUP_DIM]
        cnt = float(min(w, PAST_LEN + 1))
        pooled.append(s / cnt - p[:, g * B_GROUP_DIM:(g + 1) * B_GROUP_DIM])
    y = _mix_b_tail(pooled, wgrp_ref, bgrp_ref, scale_ref, wout_ref)
    x1_ref[...] = x + gt_ref[...] * y
    nst_ref[:, 0:POOL_BUF - 1, :] = st_ref[:, 1:POOL_BUF, :]
    nst_ref[:, POOL_BUF - 1, :] = p


def _mixer_b(x_all, pending, mod_p, mod_s, ng, layer, lb, state, w_in, w_grp, b_grp, scale,
             w_out):
    gd = B_GROUP_DIM
    common = [
        _layer_block((D, D), lb), _layer_block((len(POOL_WINDOWS), gd, gd), lb),
        _layer_block((len(POOL_WINDOWS), 1, gd), lb), _layer_block((1, D), lb),
        _layer_block((D, D), lb),
    ]
    has_pending = pending is not None
    pend_specs, pend_args = _pending_prompt(pending, mod_p, layer)
    x_all, st_p = pl.pallas_call(
        functools.partial(_mix_b_prompt_kernel, pending=has_pending),
        grid=(N_PROMPT_TILES,),
        in_specs=[pl.BlockSpec((TM, D), lambda i: (i, 0))] + pend_specs
        + _mod_specs_prompt(layer, (0, 1, 2)) + [_full((1, D))] + common,
        out_specs=[
            pl.BlockSpec((TM, D), lambda i: (i, 0)),
            pl.BlockSpec((None, POOL_BUF, D), lambda i: (i // TILES_PER_SEQ, 0, 0)),
        ],
        out_shape=[
            jax.ShapeDtypeStruct((T_PAD, D), F32),
            jax.ShapeDtypeStruct((BATCH, POOL_BUF, D), F32),
        ],
        scratch_shapes=[pltpu.VMEM((TM + POOL_CARRY, D), F32)],
        input_output_aliases={0: 0},
        compiler_params=_cparams("arbitrary"),
        name="mixer_b_prompt",
    )(x_all, *pend_args, mod_p, mod_p, mod_p, ng, w_in, w_grp, b_grp, scale, w_out)

    r = SAMPLE_ROWS
    sblk = T_PROMPT // r
    pend_specs, pend_args = _pending_sample(pending, mod_s, layer, r)
    x_all, st_s = pl.pallas_call(
        functools.partial(_mix_b_sample_kernel, pending=has_pending),
        grid=(DEC_BATCH // r,),
        in_specs=[pl.BlockSpec((r, D), lambda i: (sblk + i, 0))] + pend_specs
        + _mod_specs_sample(layer, (0, 1, 2), r) + [_full((1, D))]
        + [pl.BlockSpec((None, r, POOL_BUF, D), lambda i: (lb, i, 0, 0))] + common,
        out_specs=[
            pl.BlockSpec((r, D), lambda i: (sblk + i, 0)),
            pl.BlockSpec((r, POOL_BUF, D), lambda i: (i, 0, 0)),
        ],
        out_shape=[
            jax.ShapeDtypeStruct((T_PAD, D), F32),
            jax.ShapeDtypeStruct((DEC_BATCH, POOL_BUF, D), F32),
        ],
        input_output_aliases={0: 0},
        compiler_params=_cparams("arbitrary"),
        name="mixer_b_sample",
    )(x_all, *pend_args, mod_s, mod_s, mod_s, ng, state, w_in, w_grp, b_grp, scale, w_out)
    return x_all, st_p, st_s


def _mix_c_glu(x, sh, sc, ng, win_ref, bin_ref):
    h = (_rms(x, ng) * (1.0 + sc) + sh).astype(BF16)
    ag = _dot(h, win_ref[...]) + bin_ref[...]
    return ag[:, :D] * jax.nn.sigmoid(ag[:, D:])


def _mix_c_tail(conv, lng_ref, lnb_ref, wout_ref, bout_ref):
    z = _ln(conv, lng_ref[...], lnb_ref[...])
    z = z * jax.nn.sigmoid(z)
    return _dot(z.astype(BF16), wout_ref[...]) + bout_ref[...]


def _mix_c_prompt_kernel(*refs, pending):
    x, refs = _residual_in(refs, pending)
    (sh_ref, sc_ref, gt_ref, ng_ref, win_ref, bin_ref, wdw_ref, bdw_ref, lng_ref, lnb_ref,
     wout_ref, bout_ref, x1_ref, st_ref, full_ref, conv_ref, *shift_refs) = refs
    i = pl.program_id(0)
    j = i % TILES_PER_SEQ
    glu = _mix_c_glu(x, sh_ref[...], sc_ref[...], ng_ref[...], win_ref, bin_ref)

    @pl.when(j == 0)
    def _():
        full_ref[0:CONV_CARRY, :] = jnp.zeros((CONV_CARRY, D), F32)

    full_ref[CONV_CARRY:CONV_CARRY + TM, :] = glu
    off = CONV_CARRY - CONV_BUF
    span = TM + (off + CONV_BUF) // SUBLANES * SUBLANES - SUBLANES
    for c in range(D // LANES):
        cols = slice(c * LANES, (c + 1) * LANES)
        shift_ref = shift_refs[c % len(shift_refs)]
        for r in range(1, SUBLANES):
            shift_ref[r - 1] = full_ref[pl.ds(r, span), cols]
        for rb in range(TM // CONV_ROWS):
            acc = None
            for k in range(CONV_WIDTH):
                q, r = divmod(off + k, SUBLANES)
                start = q * SUBLANES + rb * CONV_ROWS
                if r == 0:
                    src = full_ref[pl.ds(start, CONV_ROWS), cols]
                else:
                    src = shift_ref[r - 1, pl.ds(start, CONV_ROWS), :]
                term = src * wdw_ref[k:k + 1, cols]
                acc = term if acc is None else acc + term
            conv_ref[pl.ds(rb * CONV_ROWS, CONV_ROWS), cols] = acc + bdw_ref[:, cols]
    y = _mix_c_tail(conv_ref[...], lng_ref, lnb_ref, wout_ref, bout_ref)
    x1_ref[...] = x + gt_ref[...] * y

    @pl.when(j == TILES_PER_SEQ - 1)
    def _():
        st_ref[...] = full_ref[pl.ds(CONV_CARRY + TM - CONV_BUF, CONV_BUF), :]

    full_ref[0:CONV_CARRY, :] = full_ref[pl.ds(TM, CONV_CARRY), :]


def _mix_c_sample_kernel(*refs, pending):
    x, refs = _residual_in(refs, pending)
    (sh_ref, sc_ref, gt_ref, ng_ref, st_ref, win_ref, bin_ref, wdw_ref, bdw_ref, lng_ref,
     lnb_ref, wout_ref, bout_ref, x1_ref, nst_ref) = refs
    glu = _mix_c_glu(x, sh_ref[...], sc_ref[...], ng_ref[...], win_ref, bin_ref)
    acc = glu * wdw_ref[CONV_BUF:CONV_BUF + 1, :]
    for k in range(CONV_BUF):
        acc = acc + st_ref[:, k, :] * wdw_ref[k:k + 1, :]
    y = _mix_c_tail(acc + bdw_ref[...], lng_ref, lnb_ref, wout_ref, bout_ref)
    x1_ref[...] = x + gt_ref[...] * y
    nst_ref[:, 0:CONV_BUF - 1, :] = st_ref[:, 1:CONV_BUF, :]
    nst_ref[:, CONV_BUF - 1, :] = glu


def _mixer_c(x_all, pending, mod_p, mod_s, ng, layer, lc, state, w_in, b_in, w_dw, b_dw, ln_g,
             ln_b, w_out, b_out):
    common = [
        _layer_block((D, 2 * D), lc), _layer_block((1, 2 * D), lc),
        _layer_block((CONV_WIDTH, D), lc), _layer_block((1, D), lc), _layer_block((1, D), lc),
        _layer_block((1, D), lc), _layer_block((D, D), lc), _layer_block((1, D), lc),
    ]
    has_pending = pending is not None
    pend_specs, pend_args = _pending_prompt(pending, mod_p, layer)
    x_all, st_p = pl.pallas_call(
        functools.partial(_mix_c_prompt_kernel, pending=has_pending),
        grid=(N_PROMPT_TILES,),
        in_specs=[pl.BlockSpec((TM, D), lambda i: (i, 0))] + pend_specs
        + _mod_specs_prompt(layer, (0, 1, 2)) + [_full((1, D))] + common,
        out_specs=[
            pl.BlockSpec((TM, D), lambda i: (i, 0)),
            pl.BlockSpec((None, CONV_BUF, D), lambda i: (i // TILES_PER_SEQ, 0, 0)),
        ],
        out_shape=[
            jax.ShapeDtypeStruct((T_PAD, D), F32),
            jax.ShapeDtypeStruct((BATCH, CONV_BUF, D), F32),
        ],
        scratch_shapes=[pltpu.VMEM((TM + CONV_CARRY, D), F32), pltpu.VMEM((TM, D), F32)]
        + [pltpu.VMEM((SUBLANES - 1, TM + CONV_CARRY - SUBLANES, LANES), F32)] * 2,
        input_output_aliases={0: 0},
        compiler_params=_cparams("arbitrary"),
        name="mixer_c_prompt",
    )(x_all, *pend_args, mod_p, mod_p, mod_p, ng, w_in, b_in, w_dw, b_dw, ln_g, ln_b, w_out,
      b_out)

    r = SAMPLE_ROWS
    sblk = T_PROMPT // r
    pend_specs, pend_args = _pending_sample(pending, mod_s, layer, r)
    x_all, st_s = pl.pallas_call(
        functools.partial(_mix_c_sample_kernel, pending=has_pending),
        grid=(DEC_BATCH // r,),
        in_specs=[pl.BlockSpec((r, D), lambda i: (sblk + i, 0))] + pend_specs
        + _mod_specs_sample(layer, (0, 1, 2), r) + [_full((1, D))]
        + [pl.BlockSpec((None, r, CONV_BUF, D), lambda i: (lc, i, 0, 0))] + common,
        out_specs=[
            pl.BlockSpec((r, D), lambda i: (sblk + i, 0)),
            pl.BlockSpec((r, CONV_BUF, D), lambda i: (i, 0, 0)),
        ],
        out_shape=[
            jax.ShapeDtypeStruct((T_PAD, D), F32),
            jax.ShapeDtypeStruct((DEC_BATCH, CONV_BUF, D), F32),
        ],
        input_output_aliases={0: 0},
        compiler_params=_cparams("arbitrary"),
        name="mixer_c_sample",
    )(x_all, *pend_args, mod_s, mod_s, mod_s, ng, state, w_in, b_in, w_dw, b_dw, ln_g, ln_b,
      w_out, b_out)
    return x_all, st_p, st_s


def _tile_mod(i, p_ref, s_ref):
    s_rows = jnp.concatenate([s_ref[...], jnp.zeros((TM - DEC_BATCH, D), F32)], axis=0)
    return jnp.where(i >= N_PROMPT_TILES, s_rows, p_ref[...])


def _unified_mod_specs(layer, chunks):
    specs = []
    for c in chunks:
        specs.append(pl.BlockSpec(
            (None, None, 1, D),
            lambda i, c=c: (layer, jnp.minimum(i // TILES_PER_SEQ, BATCH - 1), 0, c)))
        specs.append(pl.BlockSpec((None, DEC_BATCH, D), lambda i, c=c: (layer, 0, c)))
    return specs


def _router_kernel(x_ref, shp_ref, shs_ref, scp_ref, scs_ref, ng_ref, wr_ref, br_ref,
                   h2_ref, meta_ref, wts_ref, cnt_ref, carry_ref):
    i = pl.program_id(0)

    @pl.when(i == 0)
    def _():
        carry_ref[...] = jnp.zeros((1, LANES), F32)

    sh = _tile_mod(i, shp_ref, shs_ref)
    sc = _tile_mod(i, scp_ref, scs_ref)
    h2 = _rms(x_ref[...], ng_ref[...]) * (1.0 + sc) + sh
    h2_ref[...] = _pack_bf16_pairs(h2)
    lg = _dot(h2.astype(BF16), wr_ref[...]) + br_ref[...]
    lane = lax.broadcasted_iota(jnp.int32, (TM, LANES), 1)
    neg = -jnp.inf
    is_g = lane < N_GROUPS
    gm = jnp.where(is_g, lg, neg)
    gmax = jnp.max(gm, axis=1, keepdims=True)
    lane_f = lane.astype(F32)
    far = float(LANES)
    gsel = jnp.min(jnp.where(gm == gmax, lane_f, far), axis=1, keepdims=True).astype(jnp.int32)
    gsum = jnp.sum(jnp.where(is_g, jnp.exp(lg - gmax), 0.0), axis=1, keepdims=True)
    g_w = 1.0 / gsum
    e_lane = lane - EXPERT_LANE0
    in_grp = (e_lane >= 0) & (e_lane < N_EXPERTS) & (
        lax.shift_right_arithmetic(e_lane, EXPERTS_PER_GROUP.bit_length() - 1) == gsel)
    em = jnp.where(in_grp, lg, neg)
    m1 = jnp.max(em, axis=1, keepdims=True)
    i1 = jnp.min(jnp.where(em == m1, lane_f, far), axis=1, keepdims=True).astype(jnp.int32)
    em2 = jnp.where(lane == i1, neg, em)
    m2 = jnp.max(em2, axis=1, keepdims=True)
    i2 = jnp.min(jnp.where(em2 == m2, lane_f, far), axis=1, keepdims=True).astype(jnp.int32)
    e2 = jnp.exp(m2 - m1)
    den = 1.0 + e2
    w1 = (1.0 / den) * g_w
    w2 = (e2 / den) * g_w

    hit1 = lane == i1
    hit2 = lane == i2
    assign = jnp.where(hit1 | hit2, 1.0, 0.0).astype(BF16)
    r = lax.broadcasted_iota(jnp.int32, (TM, TM), 0)
    c = lax.broadcasted_iota(jnp.int32, (TM, TM), 1)
    before = (c < r).astype(BF16)
    seen = _dot(before, assign) + carry_ref[...]
    rank1 = jnp.sum(jnp.where(hit1, seen, 0.0), axis=1, keepdims=True)
    rank2 = jnp.sum(jnp.where(hit2, seen, 0.0), axis=1, keepdims=True)
    carry_ref[...] = carry_ref[...] + jnp.sum(assign.astype(F32), axis=0, keepdims=True)
    cnt_ref[...] = carry_ref[...].astype(jnp.int32)

    ex1 = (i1 - EXPERT_LANE0).astype(F32)
    ex2 = (i2 - EXPERT_LANE0).astype(F32)
    fields = jnp.where(lane == 0, ex1, jnp.where(lane == 1, ex2,
                       jnp.where(lane == 2, rank1, jnp.where(lane == 3, rank2, 0.0))))
    meta_ref[...] = fields.T[0:META_ROWS, :]
    wts_ref[...] = jnp.where(lane == 0, w1, jnp.where(lane == 1, w2, 0.0))


def _router(x_all, mod_p, mod_s, ng, layer, w_r, b_r):
    return pl.pallas_call(
        _router_kernel,
        grid=(N_TILES,),
        in_specs=[pl.BlockSpec((TM, D), lambda i: (i, 0))]
        + _unified_mod_specs(layer, (3, 4))
        + [_full((1, D)), _full((D, LANES)), _full((1, LANES))],
        out_specs=[
            pl.BlockSpec((TM, D // 2), lambda i: (i, 0)),
            pl.BlockSpec((META_ROWS, TM), lambda i: (0, i)),
            pl.BlockSpec((TM, LANES), lambda i: (i, 0)),
            pl.BlockSpec((1, LANES), lambda i: (0, 0)),
        ],
        out_shape=[
            jax.ShapeDtypeStruct((T_PAD, D // 2), jnp.int32),
            jax.ShapeDtypeStruct((META_ROWS, T_PAD), F32),
            jax.ShapeDtypeStruct((T_PAD, LANES), F32),
            jax.ShapeDtypeStruct((1, LANES), jnp.int32),
        ],
        scratch_shapes=[pltpu.VMEM((1, LANES), F32)],
        compiler_params=_cparams("arbitrary"),
        name="router",
    )(x_all, mod_p, mod_s, mod_p, mod_s, ng, w_r, b_r)


SC_INDEX_MAX = 128
SC_ROWS_BYTES = 256 * 1024


def _gather_chunk(per_worker, row_bytes):
    top = min(SC_INDEX_MAX, SC_ROWS_BYTES // row_bytes) // SUBLANES * SUBLANES
    for ch in range(top, SUBLANES - 1, -SUBLANES):
        if per_worker % ch == 0:
            return ch
    raise ValueError(per_worker)


def _sc_gather(table, idx):
    nc, nw = SC_CORES, SC_CORES * SC_SUBCORES
    m = idx.shape[0]
    per_w = m // nw
    assert per_w * nw == m
    width = table.shape[1]
    ch = _gather_chunk(per_w, width * table.dtype.itemsize)
    mesh = plsc.VectorSubcoreMesh(core_axis_name="c", subcore_axis_name="s")

    @functools.partial(
        pl.kernel,
        out_type=jax.ShapeDtypeStruct((m, width), table.dtype),
        mesh=mesh,
        scratch_types=[
            pltpu.VMEM((ch,), jnp.int32),
            pltpu.VMEM((ch, width), table.dtype),
            pltpu.SemaphoreType.DMA,
        ],
    )
    def gather_kernel(t_hbm, i_hbm, o_hbm, idx_v, rows_v, sem):
        wid = lax.axis_index("s") * nc + lax.axis_index("c")
        base = wid * per_w

        @pl.loop(0, per_w // ch)
        def _(j):
            off = pl.multiple_of(base + j * ch, 8)
            pltpu.sync_copy(i_hbm.at[pl.ds(off, ch)], idx_v)
            pltpu.async_copy(t_hbm.at[idx_v], rows_v, sem).wait()
            pltpu.sync_copy(rows_v, o_hbm.at[pl.ds(off, ch)])

    return gather_kernel(table, idx)


def _sc_scatter2(rows, dest0, dest1, n_out):
    nc, nw = SC_CORES, SC_CORES * SC_SUBCORES
    n, width = rows.shape
    per_w = n // nw
    assert per_w * nw == n
    ch = _gather_chunk(per_w, width * rows.dtype.itemsize)
    mesh = plsc.VectorSubcoreMesh(core_axis_name="c", subcore_axis_name="s")

    @functools.partial(
        pl.kernel,
        out_type=jax.ShapeDtypeStruct((n_out, width), rows.dtype),
        mesh=mesh,
        scratch_types=[
            pltpu.VMEM((ch,), jnp.int32),
            pltpu.VMEM((ch,), jnp.int32),
            pltpu.VMEM((ch, width), rows.dtype),
            pltpu.SemaphoreType.DMA,
            pltpu.SemaphoreType.DMA,
        ],
    )
    def scatter_kernel(r_hbm, d0_hbm, d1_hbm, o_hbm, i0_v, i1_v, rows_v, sem0, sem1):
        wid = lax.axis_index("s") * nc + lax.axis_index("c")
        base = wid * per_w

        @pl.loop(0, per_w // ch)
        def _(j):
            off = pl.multiple_of(base + j * ch, 8)
            pltpu.sync_copy(d0_hbm.at[pl.ds(off, ch)], i0_v)
            pltpu.sync_copy(d1_hbm.at[pl.ds(off, ch)], i1_v)
            pltpu.sync_copy(r_hbm.at[pl.ds(off, ch)], rows_v)
            c0 = pltpu.async_copy(rows_v, o_hbm.at[i0_v], sem0)
            c1 = pltpu.async_copy(rows_v, o_hbm.at[i1_v], sem1)
            c0.wait()
            c1.wait()

    return scatter_kernel(rows, dest0, dest1)


def _expert_kernel(be_ref, first_ref, nvalid_ref, next_ref, nused_ref, x_ref, wg_hbm, wu_hbm,
                   wd_hbm, y_ref, wgf_ref, wuf_ref, wdf_ref, wgb_ref, wub_ref, wdb_ref, sems,
                   *, layer):
    b = pl.program_id(0)

    def weight_copies(e):
        return (
            pltpu.make_async_copy(wg_hbm.at[layer, e], wgf_ref, sems.at[0]),
            pltpu.make_async_copy(wu_hbm.at[layer, e], wuf_ref, sems.at[1]),
            pltpu.make_async_copy(wd_hbm.at[layer, e], wdf_ref, sems.at[2]),
        )

    @pl.when(b == 0)
    def _():
        for cp in weight_copies(be_ref[0]):
            cp.start()

    @pl.when(b < nused_ref[0])
    def _():
        @pl.when(first_ref[b] == 1)
        def _():
            for cp in weight_copies(be_ref[b]):
                cp.wait()
            wgb_ref[...] = wgf_ref[...].astype(BF16)
            wub_ref[...] = wuf_ref[...].astype(BF16)
            wdb_ref[...] = wdf_ref[...].astype(BF16)

            @pl.when(next_ref[b] >= 0)
            def _():
                for cp in weight_copies(next_ref[b]):
                    cp.start()

        def run(rows):
            live = lax.broadcasted_iota(jnp.int32, (rows, 1), 0) < nvalid_ref[b]
            x = _unpack_bf16_pairs(jnp.where(live, x_ref[0:rows, :], 0), BF16)
            g = _dot(x, wgb_ref[...])
            u = _dot(x, wub_ref[...])
            hmid = (g * jax.nn.sigmoid(g)) * u
            y_ref[0:rows, :] = _pack_bf16_pairs(_dot(hmid.astype(BF16), wdb_ref[...]))

        half = BM // 2

        @pl.when(nvalid_ref[b] > half)
        def _():
            run(BM)

        @pl.when(nvalid_ref[b] <= half)
        def _():
            run(half)
            y_ref[half:BM, :] = jnp.zeros((BM - half, D // 2), jnp.int32)

    @pl.when(b >= nused_ref[0])
    def _():
        y_ref[...] = jnp.zeros((BM, D // 2), jnp.int32)


def _experts(x_sorted, block_expert, first, nvalid, next_expert, nused, layer, w_gate, w_up,
             w_down):
    def row_map(b, be, fi, nv, nx, nu):
        return (jnp.minimum(b, nu[0] - 1), 0)

    def out_map(b, be, fi, nv, nx, nu):
        return (b, 0)

    grid_spec = pltpu.PrefetchScalarGridSpec(
        num_scalar_prefetch=5,
        grid=(N_BLOCKS,),
        in_specs=[
            pl.BlockSpec((BM, D // 2), row_map),
            pl.BlockSpec(memory_space=pl.ANY),
            pl.BlockSpec(memory_space=pl.ANY),
            pl.BlockSpec(memory_space=pl.ANY),
        ],
        out_specs=pl.BlockSpec((BM, D // 2), out_map),
        scratch_shapes=[
            pltpu.VMEM((D, D_EXPERT), F32),
            pltpu.VMEM((D, D_EXPERT), F32),
            pltpu.VMEM((D_EXPERT, D), F32),
            pltpu.VMEM((D, D_EXPERT), BF16),
            pltpu.VMEM((D, D_EXPERT), BF16),
            pltpu.VMEM((D_EXPERT, D), BF16),
            pltpu.SemaphoreType.DMA((3,)),
        ],
    )
    return pl.pallas_call(
        functools.partial(_expert_kernel, layer=layer),
        grid_spec=grid_spec,
        out_shape=jax.ShapeDtypeStruct((P_ROWS, D // 2), jnp.int32),
        compiler_params=_cparams("arbitrary"),
        name="experts",
    )(block_expert, first, nvalid, next_expert, nused, x_sorted, w_gate, w_up, w_down)


def _combine_final_kernel(x_ref, g0_ref, g1_ref, wts_ref, gtp_ref, gts_ref, fg_ref, yp_ref,
                          ys_ref):
    i = pl.program_id(0)
    gate = _tile_mod(i, gtp_ref, gts_ref)
    y = _rms(_plus_moe(x_ref[...], g0_ref, g1_ref, wts_ref, gate), fg_ref[...])

    @pl.when(i < N_PROMPT_TILES)
    def _():
        yp_ref[...] = y

    @pl.when(i == N_PROMPT_TILES)
    def _():
        ys_ref[...] = y[0:DEC_BATCH, :]


def _combine_final(x_all, gathered, wts, mod_p, mod_s, layer, final_g):
    in_specs = [
        pl.BlockSpec((TM, D), lambda i: (i, 0)),
        pl.BlockSpec((TM, D // 2), lambda i: (i, 0)),
        pl.BlockSpec((TM, D // 2), lambda i: (i + N_TILES, 0)),
        pl.BlockSpec((TM, LANES), lambda i: (i, 0)),
    ] + _unified_mod_specs(layer, (5,))
    return pl.pallas_call(
        _combine_final_kernel,
        grid=(N_PROMPT_TILES + 1,),
        in_specs=in_specs + [_full((1, D))],
        out_specs=[
            pl.BlockSpec((TM, D), lambda i: (jnp.minimum(i, N_PROMPT_TILES - 1), 0)),
            pl.BlockSpec((DEC_BATCH, D), lambda i: (0, 0)),
        ],
        out_shape=[
            jax.ShapeDtypeStruct((T_PROMPT, D), F32),
            jax.ShapeDtypeStruct((DEC_BATCH, D), F32),
        ],
        compiler_params=_cparams("arbitrary"),
        name="combine_final",
    )(x_all, gathered, gathered, wts, mod_p, mod_s, final_g)


def _plan_kernel(cnt_ref, meta_ref, d0_ref, d1_ref, be_ref, first_ref, nvalid_ref, next_ref,
                 nused_ref, pstart_ref, nxt_ref):
    shift = BM.bit_length() - 1
    assert 1 << shift == BM

    def count(e):
        return cnt_ref[0, EXPERT_LANE0 + e]

    def scan_back(k, nxt):
        e = N_EXPERTS - 1 - k
        nxt_ref[e] = nxt
        return jnp.where(count(e) > 0, e, nxt)

    lax.fori_loop(0, N_EXPERTS, scan_back, jnp.int32(-1))

    def fill(e, blk):
        c = count(e)
        pstart_ref[e] = blk << shift

        def one(j, carry):
            be_ref[blk + j] = e
            first_ref[blk + j] = jnp.where(j == 0, 1, 0)
            nvalid_ref[blk + j] = jnp.minimum(c - (j << shift), BM)
            next_ref[blk + j] = nxt_ref[e]
            return carry

        n_blk = (c + (BM - 1)) >> shift
        lax.fori_loop(0, n_blk, one, 0)
        return blk + n_blk

    used = lax.fori_loop(0, N_EXPERTS, fill, jnp.int32(0))
    nused_ref[0] = used

    def tail(b, carry):
        be_ref[b] = 0
        first_ref[b] = 0
        nvalid_ref[b] = 0
        next_ref[b] = -1
        return carry

    lax.fori_loop(used, N_BLOCKS, tail, 0)

    m = meta_ref[...]
    expert = m[0:2, :]
    dest = m[2:4, :]
    for e in range(N_EXPERTS):
        dest = dest + jnp.where(expert == float(e), pstart_ref[e].astype(F32), 0.0)
    dest = dest.astype(jnp.int32)
    d0_ref[...] = dest[0:1, :]
    d1_ref[...] = dest[1:2, :]


def _plan(cnt, meta):
    smem = pl.BlockSpec(memory_space=pltpu.SMEM)
    blocks = jax.ShapeDtypeStruct((N_BLOCKS,), jnp.int32)
    return pl.pallas_call(
        _plan_kernel,
        grid=(1,),
        in_specs=[smem, _full((META_ROWS, T_PAD))],
        out_specs=[_full((1, T_PAD)), _full((1, T_PAD)), smem, smem, smem, smem, smem],
        out_shape=[
            jax.ShapeDtypeStruct((1, T_PAD), jnp.int32),
            jax.ShapeDtypeStruct((1, T_PAD), jnp.int32),
            blocks, blocks, blocks, blocks,
            jax.ShapeDtypeStruct((1,), jnp.int32),
        ],
        scratch_shapes=[pltpu.SMEM((N_EXPERTS,), jnp.int32), pltpu.SMEM((N_EXPERTS,), jnp.int32)],
        compiler_params=_cparams("arbitrary"),
        name="plan",
    )(cnt, meta)


def _moe(x_all, mod_p, mod_s, ng, layer, w_r, b_r, w_gate, w_up, w_down):
    h2, meta, wts, cnt = _router(x_all, mod_p, mod_s, ng, layer, w_r, b_r)
    dest0, dest1, block_expert, first, nvalid, next_expert, nused = _plan(cnt, meta)
    dest0, dest1 = dest0.reshape(T_PAD), dest1.reshape(T_PAD)
    x_sorted = _sc_scatter2(h2, dest0, dest1, P_ROWS)
    y_sorted = _experts(x_sorted, block_expert, first, nvalid, next_expert, nused, layer,
                        w_gate, w_up, w_down)
    gathered = _sc_gather(y_sorted, jnp.concatenate([dest0, dest1]))
    return gathered, wts


def kernel(x_prompt, x_sample, c_prompt, c_sample, state_pool, state_conv, ada_w, ada_b, norm_g, final_g, a_w_in, a_ln_g, a_ln_b, a_w_s, a_b_s, a_w_out, b_w_in, b_w_grp, b_b_grp, b_scale, b_w_out, c_w_in, c_b_in, c_w_dw, c_b_dw, c_ln_g, c_ln_b, c_w_out, c_b_out, moe_w_grp, moe_b_grp, moe_w_exp, moe_b_exp, moe_w_gate, moe_w_up, moe_w_down):
    x_all = (x_prompt.reshape(T_PROMPT, D), x_sample.reshape(DEC_BATCH, D))
    mods = _ada_mods(jnp.concatenate([c_prompt, c_sample], axis=0), ada_w, ada_b)
    mod_p = mods[:, :BATCH].reshape(DEPTH, BATCH, 1, 6 * D)
    mod_s = mods[:, BATCH:]

    a_w_in_b, a_w_out_b = a_w_in.astype(BF16), a_w_out.astype(BF16)
    b_w_in_b, b_w_grp_b, b_w_out_b = b_w_in.astype(BF16), b_w_grp.astype(BF16), b_w_out.astype(BF16)
    c_w_in_b, c_w_out_b = c_w_in.astype(BF16), c_w_out.astype(BF16)
    w_r = jnp.pad(jnp.concatenate([moe_w_grp, moe_w_exp], axis=2),
                  ((0, 0), (0, 0), (0, LANES - N_GROUPS - N_EXPERTS))).astype(BF16)
    b_r = jnp.pad(jnp.concatenate([moe_b_grp, moe_b_exp], axis=1),
                  ((0, 0), (0, LANES - N_GROUPS - N_EXPERTS)))

    new_a_p, new_a_s, new_b_p, new_b_s, new_c_p, new_c_s = [], [], [], [], [], []
    ia = ib = ic = 0
    pending = None
    for layer in range(DEPTH):
        ng1 = norm_g[layer, 0].reshape(1, D)
        ng2 = norm_g[layer, 1].reshape(1, D)
        kind = layer % 3
        if kind == 0:
            x_all, st_p, st_s = _mixer_a(
                x_all, pending, mod_p, mod_s, ng1, layer, ia, a_w_in_b, a_ln_g.reshape(-1, 1, D),
                a_ln_b.reshape(-1, 1, D), a_w_s, a_b_s, a_w_out_b)
            new_a_p.append(st_p)
            new_a_s.append(st_s)
            ia += 1
        elif kind == 1:
            x_all, st_p, st_s = _mixer_b(
                x_all, pending, mod_p, mod_s, ng1, layer, ib, state_pool, b_w_in_b, b_w_grp_b,
                b_b_grp.reshape(-1, len(POOL_WINDOWS), 1, B_GROUP_DIM),
                b_scale.reshape(-1, 1, D), b_w_out_b)
            new_b_p.append(st_p)
            new_b_s.append(st_s)
            ib += 1
        else:
            x_all, st_p, st_s = _mixer_c(
                x_all, pending, mod_p, mod_s, ng1, layer, ic, state_conv, c_w_in_b,
                c_b_in.reshape(-1, 1, 2 * D), c_w_dw, c_b_dw.reshape(-1, 1, D),
                c_ln_g.reshape(-1, 1, D), c_ln_b.reshape(-1, 1, D), c_w_out_b,
                c_b_out.reshape(-1, 1, D))
            new_c_p.append(st_p)
            new_c_s.append(st_s)
            ic += 1
        pending = _moe(x_all, mod_p, mod_s, ng2, layer, w_r[layer],
                       b_r[layer].reshape(1, LANES), moe_w_gate, moe_w_up, moe_w_down)

    y_p, y_s = _combine_final(x_all, *pending, mod_p, mod_s, DEPTH - 1, final_g.reshape(1, D))
    return (y_p.reshape(BATCH, SEQ, D), y_s.reshape(DEC_BATCH, 1, D),
            jnp.stack(new_a_p), jnp.stack(new_a_s), jnp.stack(new_b_p), jnp.stack(new_b_s),
            jnp.stack(new_c_p), jnp.stack(new_c_s))
```

```python
import functools

import jax
import jax.numpy as jnp
from jax import lax
from jax.experimental import pallas as pl
from jax.experimental.pallas import tpu as pltpu
from jax.experimental.pallas import tpu_sc as plsc

F32 = jnp.float32
BF16 = jnp.bfloat16

D = 1024
BATCH = 8
SEQ = 2048
DEPTH = 4
DEC_BATCH = 128
PAST_LEN = 16384
CHUNK = 128
A_HEADS = 8
POOL_WINDOWS = (2, 4, 8, 16)
B_GROUP_DIM = D // len(POOL_WINDOWS)
POOL_BUF = max(POOL_WINDOWS) - 1
CONV_WIDTH = 31
CONV_BUF = CONV_WIDTH - 1
N_GROUPS = 4
EXPERTS_PER_GROUP = 8
N_EXPERTS = N_GROUPS * EXPERTS_PER_GROUP
D_EXPERT = D // 2
EPS = 1e-6

LANES = 128
SUBLANES = 8
CONV_ROWS = 128
SC_CORES = 2
SC_SUBCORES = 16
TM = 512
TILES_PER_SEQ = SEQ // TM
T_PROMPT = BATCH * SEQ
T_ALL = T_PROMPT + DEC_BATCH
N_TILES = -(-T_ALL // TM)
T_PAD = N_TILES * TM
N_PROMPT_TILES = T_PROMPT // TM
SAMPLE_ROWS = 32
BM = 512
N_BLOCKS = -(-(2 * T_PAD) // BM) + N_EXPERTS
P_ROWS = N_BLOCKS * BM
META_ROWS = 8
POOL_CARRY = 16
CONV_CARRY = 32
EXPERT_LANE0 = N_GROUPS
ROUTER_ROWS = 48
VMEM_LIMIT = 56 * 1024 * 1024


def _cparams(*sem):
    return pltpu.CompilerParams(dimension_semantics=sem, vmem_limit_bytes=VMEM_LIMIT)


def _rms(x, g):
    return x * lax.rsqrt(jnp.mean(x * x, axis=-1, keepdims=True) + EPS) * g


def _ln(x, g, b):
    mu = jnp.mean(x, axis=-1, keepdims=True)
    xc = x - mu
    var = jnp.mean(xc * xc, axis=-1, keepdims=True)
    return xc * lax.rsqrt(var + EPS) * g + b


def _dot(a, b):
    return jnp.dot(a, b, preferred_element_type=F32)


def _pack_bf16_pairs(v):
    w = v.shape[1] // 2
    lo = lax.bitcast_convert_type(v[:, :w].astype(BF16).astype(F32), jnp.uint32)
    hi = lax.bitcast_convert_type(v[:, w:].astype(BF16).astype(F32), jnp.uint32)
    return lax.bitcast_convert_type(lax.shift_right_logical(lo, jnp.uint32(16)) | hi, jnp.int32)


def _unpack_bf16_pairs(words, dtype):
    u = lax.bitcast_convert_type(words, jnp.uint32)
    lo = lax.bitcast_convert_type(lax.shift_left(u, jnp.uint32(16)), F32)
    hi = lax.bitcast_convert_type(u & jnp.uint32(0xFFFF0000), F32)
    return jnp.concatenate([lo.astype(dtype), hi.astype(dtype)], axis=1)


def _ada_kernel(c_ref, w_ref, b_ref, o_ref):
    c = c_ref[...]
    s = (c * jax.nn.sigmoid(c)).astype(BF16)
    o_ref[...] = _dot(s, w_ref[...].astype(BF16)) + b_ref[...]


def _ada_mods(c_all, ada_w, ada_b):
    n = c_all.shape[0]
    tn = 2048
    return pl.pallas_call(
        _ada_kernel,
        grid=(DEPTH, 6 * D // tn),
        in_specs=[
            pl.BlockSpec((n, D), lambda l, j: (0, 0)),
            pl.BlockSpec((None, D, tn), lambda l, j: (l, 0, j)),
            pl.BlockSpec((None, 1, tn), lambda l, j: (l, 0, j)),
        ],
        out_specs=pl.BlockSpec((None, n, tn), lambda l, j: (l, 0, j)),
        out_shape=jax.ShapeDtypeStruct((DEPTH, n, 6 * D), F32),
        compiler_params=_cparams("parallel", "parallel"),
        name="ada_mods",
    )(c_all, ada_w, ada_b.reshape(DEPTH, 1, 6 * D))


def _mod_specs_prompt(layer, chunks):
    return [
        pl.BlockSpec((None, None, 1, D),
                     lambda i, c=c: (layer, jnp.minimum(i // TILES_PER_SEQ, BATCH - 1), 0, c))
        for c in chunks
    ]


def _mod_specs_sample(layer, chunks, rows, row_block0=0):
    return [
        pl.BlockSpec((None, rows, D), lambda i, c=c: (layer, i + row_block0, c)) for c in chunks
    ]


def _full(shape):
    nd = len(shape)
    return pl.BlockSpec(shape, lambda i: (0,) * nd)


def _layer_block(shape, layer):
    nd = len(shape)
    return pl.BlockSpec((None,) + shape, lambda i: (layer,) + (0,) * nd)


def _plus_moe(x, g0_ref, g1_ref, wts_ref, gate):
    w = wts_ref[...]
    y0 = _unpack_bf16_pairs(g0_ref[...], F32)
    y1 = _unpack_bf16_pairs(g1_ref[...], F32)
    return x + gate * (w[:, 0:1] * y0 + w[:, 1:2] * y1)


def _residual_in(refs, pending):
    if not pending:
        return refs[0][...], refs[1:]
    x_ref, g0_ref, g1_ref, wts_ref, gate_ref = refs[:5]
    return _plus_moe(x_ref[...], g0_ref, g1_ref, wts_ref, gate_ref[...]), refs[5:]


def _pending_prompt(pending, mod_p, layer):
    if pending is None:
        return [], []
    gathered, wts = pending
    specs = [
        pl.BlockSpec((TM, D // 2), lambda i: (i, 0)),
        pl.BlockSpec((TM, D // 2), lambda i: (i + N_TILES, 0)),
        pl.BlockSpec((TM, LANES), lambda i: (i, 0)),
    ] + _mod_specs_prompt(layer - 1, (5,))
    return specs, [gathered, gathered, wts, mod_p]


def _pending_sample(pending, mod_s, layer, rows):
    if pending is None:
        return [], []
    gathered, wts = pending
    first = T_PROMPT // rows
    second = (T_PAD + T_PROMPT) // rows
    specs = [
        pl.BlockSpec((rows, D // 2), lambda i: (first + i, 0)),
        pl.BlockSpec((rows, D // 2), lambda i: (second + i, 0)),
        pl.BlockSpec((rows, LANES), lambda i: (first + i, 0)),
    ] + _mod_specs_sample(layer - 1, (5,), rows)
    return specs, [gathered, gathered, wts, mod_s]


def _mix_a_front(x, sh, sc, ng, win_ref, lng_ref, lnb_ref):
    h = (_rms(x, ng) * (1.0 + sc) + sh).astype(BF16)
    z = jax.nn.gelu(_dot(h, win_ref[...]))
    u = z[:, :D]
    v = _ln(z[:, D:], lng_ref[...], lnb_ref[...])
    return u, v


def _mix_a_prompt_kernel(*refs, pending, first):
    x, refs = _residual_in(refs, pending)
    (sh_ref, sc_ref, gt_ref, ng_ref, win_ref, lng_ref, lnb_ref, ws_ref, bs_ref, wout_ref,
     x1_ref, cv_ref, mixed_ref) = refs
    i = pl.program_id(0)
    u, v = _mix_a_front(x, sh_ref[...], sc_ref[...], ng_ref[...], win_ref, lng_ref, lnb_ref)
    vb = v.astype(BF16)
    n_chunks = TM // CHUNK
    row = lax.broadcasted_iota(jnp.int32, (CHUNK, CHUNK), 0)
    col = lax.broadcasted_iota(jnp.int32, (CHUNK, CHUNK), 1)
    tril = row >= col
    hd_dim = D // A_HEADS
    for hd in range(A_HEADS):
        ws = jnp.where(tril, ws_ref[hd], 0.0).astype(BF16)
        cols = slice(hd * hd_dim, (hd + 1) * hd_dim)
        vcat = jnp.concatenate([vb[c * CHUNK:(c + 1) * CHUNK, cols] for c in range(n_chunks)], axis=1)
        m = _dot(ws, vcat)
        for c in range(n_chunks):
            mixed_ref[c * CHUNK:(c + 1) * CHUNK, cols] = m[:, c * hd_dim:(c + 1) * hd_dim] + bs_ref[hd]
    y = _dot((u * mixed_ref[...]).astype(BF16), wout_ref[...])
    out = x + gt_ref[...] * y
    if first:
        out = jnp.where(i < N_PROMPT_TILES, out, 0.0)
    x1_ref[...] = out

    @pl.when(i % TILES_PER_SEQ == TILES_PER_SEQ - 1)
    def _():
        cv_ref[...] = v[TM - CHUNK:, :]


def _mix_a_sample_kernel(*refs, pending, first):
    if first:
        refs = refs[:1] + refs[2:]
    x, refs = _residual_in(refs, pending)
    (sh_ref, sc_ref, gt_ref, ng_ref, win_ref, lng_ref, lnb_ref, wsd_ref, bsd_ref, wout_ref,
     x1_ref, cv_ref) = refs
    u, v = _mix_a_front(x, sh_ref[...], sc_ref[...], ng_ref[...], win_ref, lng_ref, lnb_ref)
    mixed = wsd_ref[...] * v + bsd_ref[...]
    y = _dot((u * mixed).astype(BF16), wout_ref[...])
    x1_ref[...] = x + gt_ref[...] * y
    cv_ref[...] = v


def _mixer_a(x_in, pending, mod_p, mod_s, ng, layer, la, w_in, ln_g, ln_b, w_s, b_s, w_out):
    first = isinstance(x_in, tuple)
    x_all = x_in[0] if first else x_in
    bs_b = jnp.broadcast_to(b_s[la][:, :, None], (A_HEADS, CHUNK, D // A_HEADS))
    common = [
        _layer_block((D, 2 * D), la), _layer_block((1, D), la), _layer_block((1, D), la),
    ]
    has_pending = pending is not None
    pend_specs, pend_args = _pending_prompt(pending, mod_p, layer)
    last_tile = N_PROMPT_TILES - 1
    x_all, cv_p = pl.pallas_call(
        functools.partial(_mix_a_prompt_kernel, pending=has_pending, first=first),
        grid=(N_TILES if first else N_PROMPT_TILES,),
        in_specs=[pl.BlockSpec((TM, D), lambda i: (jnp.minimum(i, last_tile), 0))] + pend_specs
        + _mod_specs_prompt(layer, (0, 1, 2))
        + [_full((1, D))] + common
        + [_layer_block((A_HEADS, CHUNK, CHUNK), la), _full((A_HEADS, CHUNK, D // A_HEADS)),
           _layer_block((D, D), la)],
        out_specs=[
            pl.BlockSpec((TM, D), lambda i: (i, 0)),
            pl.BlockSpec((None, CHUNK, D), lambda i: (jnp.minimum(i, last_tile) // TILES_PER_SEQ, 0, 0)),
        ],
        out_shape=[
            jax.ShapeDtypeStruct((T_PAD, D), F32),
            jax.ShapeDtypeStruct((BATCH, CHUNK, D), F32),
        ],
        scratch_shapes=[pltpu.VMEM((TM, D), F32)],
        input_output_aliases={} if first else {0: 0},
        compiler_params=_cparams("arbitrary"),
        name="mixer_a_prompt",
    )(x_all, *pend_args, mod_p, mod_p, mod_p, ng, w_in, ln_g, ln_b, w_s, bs_b, w_out)

    wsd = jnp.repeat(w_s[la, :, 0, 0], D // A_HEADS).reshape(1, D)
    bsd = jnp.repeat(b_s[la, :, 0], D // A_HEADS).reshape(1, D)
    sblk = T_PROMPT // DEC_BATCH
    pend_specs, pend_args = _pending_sample(pending, mod_s, layer, DEC_BATCH)
    if first:
        x_specs = [pl.BlockSpec((DEC_BATCH, D), lambda i: (0, 0)), pl.BlockSpec(memory_space=pl.ANY)]
        x_args = [x_in[1], x_all]
    else:
        x_specs = [pl.BlockSpec((DEC_BATCH, D), lambda i: (sblk, 0))]
        x_args = [x_all]
    x_all, cv_s = pl.pallas_call(
        functools.partial(_mix_a_sample_kernel, pending=has_pending, first=first),
        grid=(1,),
        in_specs=x_specs + pend_specs
        + _mod_specs_sample(layer, (0, 1, 2), DEC_BATCH)
        + [_full((1, D))] + common
        + [_full((1, D)), _full((1, D)), _layer_block((D, D), la)],
        out_specs=[
            pl.BlockSpec((DEC_BATCH, D), lambda i: (sblk, 0)),
            pl.BlockSpec((DEC_BATCH, D), lambda i: (0, 0)),
        ],
        out_shape=[
            jax.ShapeDtypeStruct((T_PAD, D), F32),
            jax.ShapeDtypeStruct((DEC_BATCH, D), F32),
        ],
        input_output_aliases={len(x_args) - 1: 0},
        compiler_params=_cparams("arbitrary"),
        name="mixer_a_sample",
    )(*x_args, *pend_args, mod_s, mod_s, mod_s, ng, w_in, ln_g, ln_b, wsd, bsd, w_out)
    return x_all, cv_p, cv_s.reshape(DEC_BATCH, 1, D)


def _mix_b_tail(pooled_groups, wgrp_ref, bgrp_ref, scale_ref, wout_ref):
    outs = [
        _dot(pg.astype(BF16), wgrp_ref[g]) + bgrp_ref[g]
        for g, pg in enumerate(pooled_groups)
    ]
    mixed = jnp.concatenate(outs, axis=1) * scale_ref[...]
    return _dot(mixed.astype(BF16), wout_ref[...])


def _mix_b_prompt_kernel(*refs, pending):
    x, refs = _residual_in(refs, pending)
    (sh_ref, sc_ref, gt_ref, ng_ref, win_ref, wgrp_ref, bgrp_ref, scale_ref, wout_ref,
     x1_ref, st_ref, full_ref) = refs
    i = pl.program_id(0)
    j = i % TILES_PER_SEQ
    h = (_rms(x, ng_ref[...]) * (1.0 + sc_ref[...]) + sh_ref[...]).astype(BF16)
    p = _dot(h, win_ref[...])

    @pl.when(j == 0)
    def _():
        full_ref[0:POOL_CARRY, :] = jnp.zeros((POOL_CARRY, D), F32)

    full_ref[POOL_CARRY:POOL_CARRY + TM, :] = p
    pos = j * TM + lax.broadcasted_iota(jnp.int32, (TM, 1), 0)
    pooled = []
    for g, w in enumerate(POOL_WINDOWS):
        cols = slice(g * B_GROUP_DIM, (g + 1) * B_GROUP_DIM)
        s = p[:, cols]
        for k in range(1, w):
            s = s + full_ref[pl.ds(POOL_CARRY - k, TM), cols]
        cnt = jnp.minimum(w, pos + 1).astype(F32)
        pooled.append(s / cnt - p[:, cols])
    y = _mix_b_tail(pooled, wgrp_ref, bgrp_ref, scale_ref, wout_ref)
    x1_ref[...] = x + gt_ref[...] * y

    @pl.when(j == TILES_PER_SEQ - 1)
    def _():
        st_ref[...] = full_ref[pl.ds(POOL_CARRY + TM - POOL_BUF, POOL_BUF), :]

    full_ref[0:POOL_CARRY, :] = full_ref[pl.ds(TM, POOL_CARRY), :]


def _mix_b_sample_kernel(*refs, pending):
    x, refs = _residual_in(refs, pending)
    (sh_ref, sc_ref, gt_ref, ng_ref, st_ref, win_ref, wgrp_ref, bgrp_ref, scale_ref, wout_ref,
     x1_ref, nst_ref) = refs
    h = (_rms(x, ng_ref[...]) * (1.0 + sc_ref[...]) + sh_ref[...]).astype(BF16)
    p = _dot(h, win_ref[...])
    pooled = []
    for g, w in enumerate(POOL_WINDOWS):
        s = p[:, g * B_GROUP_DIM:(g + 1) * B_GROUP_DIM]
        for k in range(1, w):
            s = s + st_ref[:, POOL_BUF - k, g * B_GROUP_DIM:(g + 1) * B_GROUP_DIM]
        cnt = float(min(w, PAST_LEN + 1))
        pooled.append(s / cnt - p[:, g * B_GROUP_DIM:(g + 1) * B_GROUP_DIM])
    y = _mix_b_tail(pooled, wgrp_ref, bgrp_ref, scale_ref, wout_ref)
    x1_ref[...] = x + gt_ref[...] * y
    for k in range(1, POOL_BUF):
        nst_ref[k - 1] = st_ref[:, k, :]
    nst_ref[POOL_BUF - 1] = p


def _mixer_b(x_all, pending, mod_p, mod_s, ng, layer, lb, state, w_in, w_grp, b_grp, scale,
             w_out):
    gd = B_GROUP_DIM
    common = [
        _layer_block((D, D), lb), _layer_block((len(POOL_WINDOWS), gd, gd), lb),
        _layer_block((len(POOL_WINDOWS), 1, gd), lb), _layer_block((1, D), lb),
        _layer_block((D, D), lb),
    ]
    has_pending = pending is not None
    pend_specs, pend_args = _pending_prompt(pending, mod_p, layer)
    x_all, st_p = pl.pallas_call(
        functools.partial(_mix_b_prompt_kernel, pending=has_pending),
        grid=(N_PROMPT_TILES,),
        in_specs=[pl.BlockSpec((TM, D), lambda i: (i, 0))] + pend_specs
        + _mod_specs_prompt(layer, (0, 1, 2)) + [_full((1, D))] + common,
        out_specs=[
            pl.BlockSpec((TM, D), lambda i: (i, 0)),
            pl.BlockSpec((None, POOL_BUF, D), lambda i: (i // TILES_PER_SEQ, 0, 0)),
        ],
        out_shape=[
            jax.ShapeDtypeStruct((T_PAD, D), F32),
            jax.ShapeDtypeStruct((BATCH, POOL_BUF, D), F32),
        ],
        scratch_shapes=[pltpu.VMEM((TM + POOL_CARRY, D), F32)],
        input_output_aliases={0: 0},
        compiler_params=_cparams("arbitrary"),
        name="mixer_b_prompt",
    )(x_all, *pend_args, mod_p, mod_p, mod_p, ng, w_in, w_grp, b_grp, scale, w_out)

    r = SAMPLE_ROWS
    sblk = T_PROMPT // r
    pend_specs, pend_args = _pending_sample(pending, mod_s, layer, r)
    x_all, st_s = pl.pallas_call(
        functools.partial(_mix_b_sample_kernel, pending=has_pending),
        grid=(DEC_BATCH // r,),
        in_specs=[pl.BlockSpec((r, D), lambda i: (sblk + i, 0))] + pend_specs
        + _mod_specs_sample(layer, (0, 1, 2), r) + [_full((1, D))]
        + [pl.BlockSpec((None, r, POOL_BUF, D), lambda i: (lb, i, 0, 0))] + common,
        out_specs=[
            pl.BlockSpec((r, D), lambda i: (sblk + i, 0)),
            pl.BlockSpec((POOL_BUF, r, D), lambda i: (0, i, 0)),
        ],
        out_shape=[
            jax.ShapeDtypeStruct((T_PAD, D), F32),
            jax.ShapeDtypeStruct((POOL_BUF, DEC_BATCH, D), F32),
        ],
        input_output_aliases={0: 0},
        compiler_params=_cparams("arbitrary"),
        name="mixer_b_sample",
    )(x_all, *pend_args, mod_s, mod_s, mod_s, ng, state, w_in, w_grp, b_grp, scale, w_out)
    return x_all, st_p, jnp.transpose(st_s, (1, 0, 2))


def _mix_c_glu(x, sh, sc, ng, win_ref, bin_ref):
    h = (_rms(x, ng) * (1.0 + sc) + sh).astype(BF16)
    ag = _dot(h, win_ref[...]) + bin_ref[...]
    return ag[:, :D] * jax.nn.sigmoid(ag[:, D:])


def _mix_c_tail(conv, lng_ref, lnb_ref, wout_ref, bout_ref):
    z = _ln(conv, lng_ref[...], lnb_ref[...])
    z = z * jax.nn.sigmoid(z)
    return _dot(z.astype(BF16), wout_ref[...]) + bout_ref[...]


def _mix_c_prompt_kernel(*refs, pending):
    x, refs = _residual_in(refs, pending)
    (sh_ref, sc_ref, gt_ref, ng_ref, win_ref, bin_ref, wdw_ref, bdw_ref, lng_ref, lnb_ref,
     wout_ref, bout_ref, x1_ref, st_ref, full_ref, conv_ref, *shift_refs) = refs
    i = pl.program_id(0)
    j = i % TILES_PER_SEQ
    glu = _mix_c_glu(x, sh_ref[...], sc_ref[...], ng_ref[...], win_ref, bin_ref)

    @pl.when(j == 0)
    def _():
        full_ref[0:CONV_CARRY, :] = jnp.zeros((CONV_CARRY, D), F32)

    full_ref[CONV_CARRY:CONV_CARRY + TM, :] = glu
    off = CONV_CARRY - CONV_BUF
    span = TM + (off + CONV_BUF) // SUBLANES * SUBLANES - SUBLANES
    for c in range(D // LANES):
        cols = slice(c * LANES, (c + 1) * LANES)
        shift_ref = shift_refs[c % len(shift_refs)]
        for r in range(1, SUBLANES):
            shift_ref[r - 1] = full_ref[pl.ds(r, span), cols]
        for rb in range(TM // CONV_ROWS):
            acc = None
            for k in range(CONV_WIDTH):
                q, r = divmod(off + k, SUBLANES)
                start = q * SUBLANES + rb * CONV_ROWS
                if r == 0:
                    src = full_ref[pl.ds(start, CONV_ROWS), cols]
                else:
                    src = shift_ref[r - 1, pl.ds(start, CONV_ROWS), :]
                term = src * wdw_ref[k:k + 1, cols]
                acc = term if acc is None else acc + term
            conv_ref[pl.ds(rb * CONV_ROWS, CONV_ROWS), cols] = acc + bdw_ref[:, cols]
    y = _mix_c_tail(conv_ref[...], lng_ref, lnb_ref, wout_ref, bout_ref)
    x1_ref[...] = x + gt_ref[...] * y

    @pl.when(j == TILES_PER_SEQ - 1)
    def _():
        st_ref[...] = full_ref[pl.ds(CONV_CARRY + TM - CONV_BUF, CONV_BUF), :]

    full_ref[0:CONV_CARRY, :] = full_ref[pl.ds(TM, CONV_CARRY), :]


def _mix_c_sample_kernel(*refs, pending):
    x, refs = _residual_in(refs, pending)
    (sh_ref, sc_ref, gt_ref, ng_ref, st_ref, win_ref, bin_ref, wdw_ref, bdw_ref, lng_ref,
     lnb_ref, wout_ref, bout_ref, x1_ref, nst_ref) = refs
    glu = _mix_c_glu(x, sh_ref[...], sc_ref[...], ng_ref[...], win_ref, bin_ref)
    acc = glu * wdw_ref[CONV_BUF:CONV_BUF + 1, :]
    for k in range(CONV_BUF):
        acc = acc + st_ref[:, k, :] * wdw_ref[k:k + 1, :]
    y = _mix_c_tail(acc + bdw_ref[...], lng_ref, lnb_ref, wout_ref, bout_ref)
    x1_ref[...] = x + gt_ref[...] * y
    for k in range(1, CONV_BUF):
        nst_ref[k - 1] = st_ref[:, k, :]
    nst_ref[CONV_BUF - 1] = glu


def _mixer_c(x_all, pending, mod_p, mod_s, ng, layer, lc, state, w_in, b_in, w_dw, b_dw, ln_g,
             ln_b, w_out, b_out):
    common = [
        _layer_block((D, 2 * D), lc), _layer_block((1, 2 * D), lc),
        _layer_block((CONV_WIDTH, D), lc), _layer_block((1, D), lc), _layer_block((1, D), lc),
        _layer_block((1, D), lc), _layer_block((D, D), lc), _layer_block((1, D), lc),
    ]
    has_pending = pending is not None
    pend_specs, pend_args = _pending_prompt(pending, mod_p, layer)
    x_all, st_p = pl.pallas_call(
        functools.partial(_mix_c_prompt_kernel, pending=has_pending),
        grid=(N_PROMPT_TILES,),
        in_specs=[pl.BlockSpec((TM, D), lambda i: (i, 0))] + pend_specs
        + _mod_specs_prompt(layer, (0, 1, 2)) + [_full((1, D))] + common,
        out_specs=[
            pl.BlockSpec((TM, D), lambda i: (i, 0)),
            pl.BlockSpec((None, CONV_BUF, D), lambda i: (i // TILES_PER_SEQ, 0, 0)),
        ],
        out_shape=[
            jax.ShapeDtypeStruct((T_PAD, D), F32),
            jax.ShapeDtypeStruct((BATCH, CONV_BUF, D), F32),
        ],
        scratch_shapes=[pltpu.VMEM((TM + CONV_CARRY, D), F32), pltpu.VMEM((TM, D), F32)]
        + [pltpu.VMEM((SUBLANES - 1, TM + CONV_CARRY - SUBLANES, LANES), F32)] * 2,
        input_output_aliases={0: 0},
        compiler_params=_cparams("arbitrary"),
        name="mixer_c_prompt",
    )(x_all, *pend_args, mod_p, mod_p, mod_p, ng, w_in, b_in, w_dw, b_dw, ln_g, ln_b, w_out,
      b_out)

    r = SAMPLE_ROWS
    sblk = T_PROMPT // r
    pend_specs, pend_args = _pending_sample(pending, mod_s, layer, r)
    x_all, st_s = pl.pallas_call(
        functools.partial(_mix_c_sample_kernel, pending=has_pending),
        grid=(DEC_BATCH // r,),
        in_specs=[pl.BlockSpec((r, D), lambda i: (sblk + i, 0))] + pend_specs
        + _mod_specs_sample(layer, (0, 1, 2), r) + [_full((1, D))]
        + [pl.BlockSpec((None, r, CONV_BUF, D), lambda i: (lc, i, 0, 0))] + common,
        out_specs=[
            pl.BlockSpec((r, D), lambda i: (sblk + i, 0)),
            pl.BlockSpec((CONV_BUF, r, D), lambda i: (0, i, 0)),
        ],
        out_shape=[
            jax.ShapeDtypeStruct((T_PAD, D), F32),
            jax.ShapeDtypeStruct((CONV_BUF, DEC_BATCH, D), F32),
        ],
        input_output_aliases={0: 0},
        compiler_params=_cparams("arbitrary"),
        name="mixer_c_sample",
    )(x_all, *pend_args, mod_s, mod_s, mod_s, ng, state, w_in, b_in, w_dw, b_dw, ln_g, ln_b,
      w_out, b_out)
    return x_all, st_p, jnp.transpose(st_s, (1, 0, 2))


def _tile_mod(i, p_ref, s_ref):
    s_rows = jnp.concatenate([s_ref[...], jnp.zeros((TM - DEC_BATCH, D), F32)], axis=0)
    return jnp.where(i >= N_PROMPT_TILES, s_rows, p_ref[...])


def _unified_mod_specs(layer, chunks):
    specs = []
    for c in chunks:
        specs.append(pl.BlockSpec(
            (None, None, 1, D),
            lambda i, c=c: (layer, jnp.minimum(i // TILES_PER_SEQ, BATCH - 1), 0, c)))
        specs.append(pl.BlockSpec((None, DEC_BATCH, D), lambda i, c=c: (layer, 0, c)))
    return specs


def _router_kernel(x_ref, shp_ref, shs_ref, scp_ref, scs_ref, ng_ref, wr_ref, br_ref,
                   h2_ref, meta_ref, wts_ref, cnt_ref, carry_ref, before_ref):
    i = pl.program_id(0)

    @pl.when(i == 0)
    def _():
        carry_ref[...] = jnp.zeros((ROUTER_ROWS, 1), F32)
        earlier = lax.broadcasted_iota(jnp.int32, (TM, TM), 0)
        token = lax.broadcasted_iota(jnp.int32, (TM, TM), 1)
        before_ref[...] = (earlier < token).astype(BF16)

    sh = _tile_mod(i, shp_ref, shs_ref)
    sc = _tile_mod(i, scp_ref, scs_ref)
    h2 = _rms(x_ref[...], ng_ref[...]) * (1.0 + sc) + sh
    h2_ref[...] = _pack_bf16_pairs(h2)
    lg = lax.dot_general(wr_ref[...], h2.astype(BF16), (((1,), (1,)), ((), ())),
                         preferred_element_type=F32) + br_ref[...]
    row = lax.broadcasted_iota(jnp.int32, (ROUTER_ROWS, TM), 0)
    row_f = row.astype(F32)
    far = float(ROUTER_ROWS)
    neg = -jnp.inf
    is_g = row < N_GROUPS
    gm = jnp.where(is_g, lg, neg)
    gmax = jnp.max(gm, axis=0, keepdims=True)
    gsel = jnp.min(jnp.where(gm == gmax, row_f, far), axis=0, keepdims=True).astype(jnp.int32)
    gsum = jnp.sum(jnp.where(is_g, jnp.exp(lg - gmax), 0.0), axis=0, keepdims=True)
    g_w = 1.0 / gsum
    e_row = row - EXPERT_LANE0
    in_grp = (e_row >= 0) & (e_row < N_EXPERTS) & (
        lax.shift_right_arithmetic(e_row, EXPERTS_PER_GROUP.bit_length() - 1) == gsel)
    em = jnp.where(in_grp, lg, neg)
    m1 = jnp.max(em, axis=0, keepdims=True)
    i1 = jnp.min(jnp.where(em == m1, row_f, far), axis=0, keepdims=True).astype(jnp.int32)
    em2 = jnp.where(row == i1, neg, em)
    m2 = jnp.max(em2, axis=0, keepdims=True)
    i2 = jnp.min(jnp.where(em2 == m2, row_f, far), axis=0, keepdims=True).astype(jnp.int32)
    e2 = jnp.exp(m2 - m1)
    den = 1.0 + e2
    w1 = (1.0 / den) * g_w
    w2 = (e2 / den) * g_w

    hit1 = row == i1
    hit2 = row == i2
    assign = jnp.where(hit1 | hit2, 1.0, 0.0).astype(BF16)
    seen = _dot(assign, before_ref[...]) + carry_ref[...]
    rank1 = jnp.sum(jnp.where(hit1, seen, 0.0), axis=0, keepdims=True)
    rank2 = jnp.sum(jnp.where(hit2, seen, 0.0), axis=0, keepdims=True)
    carry_ref[...] = carry_ref[...] + jnp.sum(assign.astype(F32), axis=1, keepdims=True)
    cnt_ref[...] = carry_ref[...].astype(jnp.int32)

    ex1 = (i1 - EXPERT_LANE0).astype(F32)
    ex2 = (i2 - EXPERT_LANE0).astype(F32)
    field = lax.broadcasted_iota(jnp.int32, (META_ROWS, TM), 0)
    meta_ref[...] = jnp.where(field == 0, ex1, jnp.where(field == 1, ex2,
                              jnp.where(field == 2, rank1, jnp.where(field == 3, rank2, 0.0))))
    slot = lax.broadcasted_iota(jnp.int32, (LANES, TM), 0)
    wts_ref[...] = jnp.where(slot == 0, w1, jnp.where(slot == 1, w2, 0.0)).T


def _router(x_all, mod_p, mod_s, ng, layer, w_r, b_r):
    return pl.pallas_call(
        _router_kernel,
        grid=(N_TILES,),
        in_specs=[pl.BlockSpec((TM, D), lambda i: (i, 0))]
        + _unified_mod_specs(layer, (3, 4))
        + [_full((1, D)), _full((ROUTER_ROWS, D)), _full((ROUTER_ROWS, 1))],
        out_specs=[
            pl.BlockSpec((TM, D // 2), lambda i: (i, 0)),
            pl.BlockSpec((META_ROWS, TM), lambda i: (0, i)),
            pl.BlockSpec((TM, LANES), lambda i: (i, 0)),
            pl.BlockSpec((ROUTER_ROWS, 1), lambda i: (0, 0)),
        ],
        out_shape=[
            jax.ShapeDtypeStruct((T_PAD, D // 2), jnp.int32),
            jax.ShapeDtypeStruct((META_ROWS, T_PAD), F32),
            jax.ShapeDtypeStruct((T_PAD, LANES), F32),
            jax.ShapeDtypeStruct((ROUTER_ROWS, 1), jnp.int32),
        ],
        scratch_shapes=[pltpu.VMEM((ROUTER_ROWS, 1), F32), pltpu.VMEM((TM, TM), BF16)],
        compiler_params=_cparams("arbitrary"),
        name="router",
    )(x_all, mod_p, mod_s, mod_p, mod_s, ng, w_r, b_r)


SC_INDEX_MAX = 128
SC_ROWS_BYTES = 256 * 1024


def _gather_chunk(per_worker, row_bytes):
    top = min(SC_INDEX_MAX, SC_ROWS_BYTES // row_bytes) // SUBLANES * SUBLANES
    for ch in range(top, SUBLANES - 1, -SUBLANES):
        if per_worker % ch == 0:
            return ch
    raise ValueError(per_worker)


def _sc_gather(table, idx):
    nc, nw = SC_CORES, SC_CORES * SC_SUBCORES
    m = idx.shape[0]
    per_w = m // nw
    assert per_w * nw == m
    width = table.shape[1]
    ch = _gather_chunk(per_w, width * table.dtype.itemsize)
    mesh = plsc.VectorSubcoreMesh(core_axis_name="c", subcore_axis_name="s")

    @functools.partial(
        pl.kernel,
        out_type=jax.ShapeDtypeStruct((m, width), table.dtype),
        mesh=mesh,
        scratch_types=[
            pltpu.VMEM((ch,), jnp.int32),
            pltpu.VMEM((ch, width), table.dtype),
            pltpu.SemaphoreType.DMA,
        ],
    )
    def gather_kernel(t_hbm, i_hbm, o_hbm, idx_v, rows_v, sem):
        wid = lax.axis_index("s") * nc + lax.axis_index("c")
        base = wid * per_w

        @pl.loop(0, per_w // ch)
        def _(j):
            off = pl.multiple_of(base + j * ch, 8)
            pltpu.sync_copy(i_hbm.at[pl.ds(off, ch)], idx_v)
            pltpu.async_copy(t_hbm.at[idx_v], rows_v, sem).wait()
            pltpu.sync_copy(rows_v, o_hbm.at[pl.ds(off, ch)])

    return gather_kernel(table, idx)


def _sc_scatter2(rows, dest0, dest1, n_out):
    nc, nw = SC_CORES, SC_CORES * SC_SUBCORES
    n, width = rows.shape
    per_w = n // nw
    assert per_w * nw == n
    ch = _gather_chunk(per_w, width * rows.dtype.itemsize)
    mesh = plsc.VectorSubcoreMesh(core_axis_name="c", subcore_axis_name="s")

    @functools.partial(
        pl.kernel,
        out_type=jax.ShapeDtypeStruct((n_out, width), rows.dtype),
        mesh=mesh,
        scratch_types=[
            pltpu.VMEM((ch,), jnp.int32),
            pltpu.VMEM((ch,), jnp.int32),
            pltpu.VMEM((ch, width), rows.dtype),
            pltpu.SemaphoreType.DMA,
            pltpu.SemaphoreType.DMA,
        ],
    )
    def scatter_kernel(r_hbm, d0_hbm, d1_hbm, o_hbm, i0_v, i1_v, rows_v, sem0, sem1):
        wid = lax.axis_index("s") * nc + lax.axis_index("c")
        base = wid * per_w

        @pl.loop(0, per_w // ch)
        def _(j):
            off = pl.multiple_of(base + j * ch, 8)
            pltpu.sync_copy(d0_hbm.at[pl.ds(off, ch)], i0_v)
            pltpu.sync_copy(d1_hbm.at[pl.ds(off, ch)], i1_v)
            pltpu.sync_copy(r_hbm.at[pl.ds(off, ch)], rows_v)
            c0 = pltpu.async_copy(rows_v, o_hbm.at[i0_v], sem0)
            c1 = pltpu.async_copy(rows_v, o_hbm.at[i1_v], sem1)
            c0.wait()
            c1.wait()

    return scatter_kernel(rows, dest0, dest1)


def _expert_kernel(be_ref, first_ref, nvalid_ref, next_ref, nused_ref, x_ref, wg_hbm, wu_hbm,
                   wd_hbm, y_ref, wgf_ref, wuf_ref, wdf_ref, wgb_ref, wub_ref, wdb_ref, sems,
                   *, layer):
    b = pl.program_id(0)

    def weight_copies(e):
        return (
            pltpu.make_async_copy(wg_hbm.at[layer, e], wgf_ref, sems.at[0]),
            pltpu.make_async_copy(wu_hbm.at[layer, e], wuf_ref, sems.at[1]),
            pltpu.make_async_copy(wd_hbm.at[layer, e], wdf_ref, sems.at[2]),
        )

    @pl.when(b == 0)
    def _():
        for cp in weight_copies(be_ref[0]):
            cp.start()

    @pl.when(b < nused_ref[0])
    def _():
        @pl.when(first_ref[b] == 1)
        def _():
            for cp in weight_copies(be_ref[b]):
                cp.wait()
            wgb_ref[...] = wgf_ref[...].astype(BF16)
            wub_ref[...] = wuf_ref[...].astype(BF16)
            wdb_ref[...] = wdf_ref[...].astype(BF16)

            @pl.when(next_ref[b] >= 0)
            def _():
                for cp in weight_copies(next_ref[b]):
                    cp.start()

        def run(rows):
            live = lax.broadcasted_iota(jnp.int32, (rows, 1), 0) < nvalid_ref[b]
            x = _unpack_bf16_pairs(jnp.where(live, x_ref[0:rows, :], 0), BF16)
            g = _dot(x, wgb_ref[...])
            u = _dot(x, wub_ref[...])
            hmid = (g * jax.nn.sigmoid(g)) * u
            y_ref[0:rows, :] = _pack_bf16_pairs(_dot(hmid.astype(BF16), wdb_ref[...]))

        half = BM // 2

        @pl.when(nvalid_ref[b] > half)
        def _():
            run(BM)

        @pl.when(nvalid_ref[b] <= half)
        def _():
            run(half)
            y_ref[half:BM, :] = jnp.zeros((BM - half, D // 2), jnp.int32)

    @pl.when(b >= nused_ref[0])
    def _():
        y_ref[...] = jnp.zeros((BM, D // 2), jnp.int32)


def _experts(x_sorted, block_expert, first, nvalid, next_expert, nused, layer, w_gate, w_up,
             w_down):
    def row_map(b, be, fi, nv, nx, nu):
        return (jnp.minimum(b, nu[0] - 1), 0)

    def out_map(b, be, fi, nv, nx, nu):
        return (b, 0)

    grid_spec = pltpu.PrefetchScalarGridSpec(
        num_scalar_prefetch=5,
        grid=(N_BLOCKS,),
        in_specs=[
            pl.BlockSpec((BM, D // 2), row_map),
            pl.BlockSpec(memory_space=pl.ANY),
            pl.BlockSpec(memory_space=pl.ANY),
            pl.BlockSpec(memory_space=pl.ANY),
        ],
        out_specs=pl.BlockSpec((BM, D // 2), out_map),
        scratch_shapes=[
            pltpu.VMEM((D, D_EXPERT), F32),
            pltpu.VMEM((D, D_EXPERT), F32),
            pltpu.VMEM((D_EXPERT, D), F32),
            pltpu.VMEM((D, D_EXPERT), BF16),
            pltpu.VMEM((D, D_EXPERT), BF16),
            pltpu.VMEM((D_EXPERT, D), BF16),
            pltpu.SemaphoreType.DMA((3,)),
        ],
    )
    return pl.pallas_call(
        functools.partial(_expert_kernel, layer=layer),
        grid_spec=grid_spec,
        out_shape=jax.ShapeDtypeStruct((P_ROWS, D // 2), jnp.int32),
        compiler_params=_cparams("arbitrary"),
        name="experts",
    )(block_expert, first, nvalid, next_expert, nused, x_sorted, w_gate, w_up, w_down)


def _combine_final_kernel(x_ref, g0_ref, g1_ref, wts_ref, gtp_ref, gts_ref, fg_ref, yp_ref,
                          ys_ref):
    i = pl.program_id(0)
    gate = _tile_mod(i, gtp_ref, gts_ref)
    y = _rms(_plus_moe(x_ref[...], g0_ref, g1_ref, wts_ref, gate), fg_ref[...])

    @pl.when(i < N_PROMPT_TILES)
    def _():
        yp_ref[...] = y

    @pl.when(i == N_PROMPT_TILES)
    def _():
        ys_ref[...] = y[0:DEC_BATCH, :]


def _combine_final(x_all, gathered, wts, mod_p, mod_s, layer, final_g):
    in_specs = [
        pl.BlockSpec((TM, D), lambda i: (i, 0)),
        pl.BlockSpec((TM, D // 2), lambda i: (i, 0)),
        pl.BlockSpec((TM, D // 2), lambda i: (i + N_TILES, 0)),
        pl.BlockSpec((TM, LANES), lambda i: (i, 0)),
    ] + _unified_mod_specs(layer, (5,))
    return pl.pallas_call(
        _combine_final_kernel,
        grid=(N_PROMPT_TILES + 1,),
        in_specs=in_specs + [_full((1, D))],
        out_specs=[
            pl.BlockSpec((TM, D), lambda i: (jnp.minimum(i, N_PROMPT_TILES - 1), 0)),
            pl.BlockSpec((DEC_BATCH, D), lambda i: (0, 0)),
        ],
        out_shape=[
            jax.ShapeDtypeStruct((T_PROMPT, D), F32),
            jax.ShapeDtypeStruct((DEC_BATCH, D), F32),
        ],
        compiler_params=_cparams("arbitrary"),
        name="combine_final",
    )(x_all, gathered, gathered, wts, mod_p, mod_s, final_g)


def _plan_kernel(cnt_ref, meta_ref, d0_ref, d1_ref, be_ref, first_ref, nvalid_ref, next_ref,
                 nused_ref, pstart_ref, nxt_ref):
    shift = BM.bit_length() - 1
    assert 1 << shift == BM

    def count(e):
        return cnt_ref[EXPERT_LANE0 + e, 0]

    def scan_back(k, nxt):
        e = N_EXPERTS - 1 - k
        nxt_ref[e] = nxt
        return jnp.where(count(e) > 0, e, nxt)

    lax.fori_loop(0, N_EXPERTS, scan_back, jnp.int32(-1))

    def fill(e, blk):
        c = count(e)
        pstart_ref[e] = blk << shift

        def one(j, carry):
            be_ref[blk + j] = e
            first_ref[blk + j] = jnp.where(j == 0, 1, 0)
            nvalid_ref[blk + j] = jnp.minimum(c - (j << shift), BM)
            next_ref[blk + j] = nxt_ref[e]
            return carry

        n_blk = (c + (BM - 1)) >> shift
        lax.fori_loop(0, n_blk, one, 0)
        return blk + n_blk

    used = lax.fori_loop(0, N_EXPERTS, fill, jnp.int32(0))
    nused_ref[0] = used

    def tail(b, carry):
        be_ref[b] = 0
        first_ref[b] = 0
        nvalid_ref[b] = 0
        next_ref[b] = -1
        return carry

    lax.fori_loop(used, N_BLOCKS, tail, 0)

    m = meta_ref[...]
    expert = m[0:2, :]
    dest = m[2:4, :]
    for e in range(N_EXPERTS):
        dest = dest + jnp.where(expert == float(e), pstart_ref[e].astype(F32), 0.0)
    dest = dest.astype(jnp.int32)
    d0_ref[...] = dest[0:1, :]
    d1_ref[...] = dest[1:2, :]


def _plan(cnt, meta):
    smem = pl.BlockSpec(memory_space=pltpu.SMEM)
    blocks = jax.ShapeDtypeStruct((N_BLOCKS,), jnp.int32)
    return pl.pallas_call(
        _plan_kernel,
        grid=(1,),
        in_specs=[smem, _full((META_ROWS, T_PAD))],
        out_specs=[_full((1, T_PAD)), _full((1, T_PAD)), smem, smem, smem, smem, smem],
        out_shape=[
            jax.ShapeDtypeStruct((1, T_PAD), jnp.int32),
            jax.ShapeDtypeStruct((1, T_PAD), jnp.int32),
            blocks, blocks, blocks, blocks,
            jax.ShapeDtypeStruct((1,), jnp.int32),
        ],
        scratch_shapes=[pltpu.SMEM((N_EXPERTS,), jnp.int32), pltpu.SMEM((N_EXPERTS,), jnp.int32)],
        compiler_params=_cparams("arbitrary"),
        name="plan",
    )(cnt, meta)


def _moe(x_all, mod_p, mod_s, ng, layer, w_r, b_r, w_gate, w_up, w_down):
    h2, meta, wts, cnt = _router(x_all, mod_p, mod_s, ng, layer, w_r, b_r)
    dest0, dest1, block_expert, first, nvalid, next_expert, nused = _plan(cnt, meta)
    dest0, dest1 = dest0.reshape(T_PAD), dest1.reshape(T_PAD)
    x_sorted = _sc_scatter2(h2, dest0, dest1, P_ROWS)
    y_sorted = _experts(x_sorted, block_expert, first, nvalid, next_expert, nused, layer,
                        w_gate, w_up, w_down)
    gathered = _sc_gather(y_sorted, jnp.concatenate([dest0, dest1]))
    return gathered, wts


def kernel(x_prompt, x_sample, c_prompt, c_sample, state_pool, state_conv, ada_w, ada_b, norm_g, final_g, a_w_in, a_ln_g, a_ln_b, a_w_s, a_b_s, a_w_out, b_w_in, b_w_grp, b_b_grp, b_scale, b_w_out, c_w_in, c_b_in, c_w_dw, c_b_dw, c_ln_g, c_ln_b, c_w_out, c_b_out, moe_w_grp, moe_b_grp, moe_w_exp, moe_b_exp, moe_w_gate, moe_w_up, moe_w_down):
    x_all = (x_prompt.reshape(T_PROMPT, D), x_sample.reshape(DEC_BATCH, D))
    mods = _ada_mods(jnp.concatenate([c_prompt, c_sample], axis=0), ada_w, ada_b)
    mod_p = mods[:, :BATCH].reshape(DEPTH, BATCH, 1, 6 * D)
    mod_s = mods[:, BATCH:]

    a_w_in_b, a_w_out_b = a_w_in.astype(BF16), a_w_out.astype(BF16)
    b_w_in_b, b_w_grp_b, b_w_out_b = b_w_in.astype(BF16), b_w_grp.astype(BF16), b_w_out.astype(BF16)
    c_w_in_b, c_w_out_b = c_w_in.astype(BF16), c_w_out.astype(BF16)
    pad_rows = ROUTER_ROWS - N_GROUPS - N_EXPERTS
    w_r = jnp.pad(jnp.swapaxes(jnp.concatenate([moe_w_grp, moe_w_exp], axis=2), 1, 2),
                  ((0, 0), (0, pad_rows), (0, 0))).astype(BF16)
    b_r = jnp.pad(jnp.concatenate([moe_b_grp, moe_b_exp], axis=1), ((0, 0), (0, pad_rows)))

    new_a_p, new_a_s, new_b_p, new_b_s, new_c_p, new_c_s = [], [], [], [], [], []
    ia = ib = ic = 0
    pending = None
    for layer in range(DEPTH):
        ng1 = norm_g[layer, 0].reshape(1, D)
        ng2 = norm_g[layer, 1].reshape(1, D)
        kind = layer % 3
        if kind == 0:
            x_all, st_p, st_s = _mixer_a(
                x_all, pending, mod_p, mod_s, ng1, layer, ia, a_w_in_b, a_ln_g.reshape(-1, 1, D),
                a_ln_b.reshape(-1, 1, D), a_w_s, a_b_s, a_w_out_b)
            new_a_p.append(st_p)
            new_a_s.append(st_s)
            ia += 1
        elif kind == 1:
            x_all, st_p, st_s = _mixer_b(
                x_all, pending, mod_p, mod_s, ng1, layer, ib, state_pool, b_w_in_b, b_w_grp_b,
                b_b_grp.reshape(-1, len(POOL_WINDOWS), 1, B_GROUP_DIM),
                b_scale.reshape(-1, 1, D), b_w_out_b)
            new_b_p.append(st_p)
            new_b_s.append(st_s)
            ib += 1
        else:
            x_all, st_p, st_s = _mixer_c(
                x_all, pending, mod_p, mod_s, ng1, layer, ic, state_conv, c_w_in_b,
                c_b_in.reshape(-1, 1, 2 * D), c_w_dw, c_b_dw.reshape(-1, 1, D),
                c_ln_g.reshape(-1, 1, D), c_ln_b.reshape(-1, 1, D), c_w_out_b,
                c_b_out.reshape(-1, 1, D))
            new_c_p.append(st_p)
            new_c_s.append(st_s)
            ic += 1
        pending = _moe(x_all, mod_p, mod_s, ng2, layer, w_r[layer],
                       b_r[layer].reshape(ROUTER_ROWS, 1), moe_w_gate, moe_w_up, moe_w_down)

    y_p, y_s = _combine_final(x_all, *pending, mod_p, mod_s, DEPTH - 1, final_g.reshape(1, D))
    return (y_p.reshape(BATCH, SEQ, D), y_s.reshape(DEC_BATCH, 1, D),
            jnp.stack(new_a_p), jnp.stack(new_a_s), jnp.stack(new_b_p), jnp.stack(new_b_s),
            jnp.stack(new_c_p), jnp.stack(new_c_s))
```

```python
import functools

import jax
import jax.numpy as jnp
from jax import lax
from jax.experimental import pallas as pl
from jax.experimental.pallas import tpu as pltpu
from jax.experimental.pallas import tpu_sc as plsc

F32 = jnp.float32
BF16 = jnp.bfloat16

D = 1024
BATCH = 8
SEQ = 2048
DEPTH = 4
DEC_BATCH = 128
PAST_LEN = 16384
CHUNK = 128
A_HEADS = 8
POOL_WINDOWS = (2, 4, 8, 16)
B_GROUP_DIM = D // len(POOL_WINDOWS)
POOL_BUF = max(POOL_WINDOWS) - 1
CONV_WIDTH = 31
CONV_BUF = CONV_WIDTH - 1
N_GROUPS = 4
EXPERTS_PER_GROUP = 8
N_EXPERTS = N_GROUPS * EXPERTS_PER_GROUP
D_EXPERT = D // 2
EPS = 1e-6

LANES = 128
SUBLANES = 8
CONV_ROWS = 128
SC_CORES = 2
SC_SUBCORES = 16
TM = 512
TILES_PER_SEQ = SEQ // TM
T_PROMPT = BATCH * SEQ
T_ALL = T_PROMPT + DEC_BATCH
N_TILES = -(-T_ALL // TM)
T_PAD = N_TILES * TM
N_PROMPT_TILES = T_PROMPT // TM
SAMPLE_ROWS = 32
BM = 512
N_BLOCKS = -(-(2 * T_PAD) // BM) + N_EXPERTS
P_ROWS = N_BLOCKS * BM
META_ROWS = 8
POOL_CARRY = 16
CONV_CARRY = 32
EXPERT_LANE0 = N_GROUPS
ROUTER_ROWS = 48
VMEM_LIMIT = 56 * 1024 * 1024


def _cparams(*sem):
    return pltpu.CompilerParams(dimension_semantics=sem, vmem_limit_bytes=VMEM_LIMIT)


def _rms(x, g):
    return x * lax.rsqrt(jnp.mean(x * x, axis=-1, keepdims=True) + EPS) * g


def _ln(x, g, b):
    mu = jnp.mean(x, axis=-1, keepdims=True)
    xc = x - mu
    var = jnp.mean(xc * xc, axis=-1, keepdims=True)
    return xc * lax.rsqrt(var + EPS) * g + b


def _dot(a, b):
    return jnp.dot(a, b, preferred_element_type=F32)


def _pack_bf16_pairs(v):
    w = v.shape[1] // 2
    lo = lax.bitcast_convert_type(v[:, :w].astype(BF16).astype(F32), jnp.uint32)
    hi = lax.bitcast_convert_type(v[:, w:].astype(BF16).astype(F32), jnp.uint32)
    return lax.bitcast_convert_type(lax.shift_right_logical(lo, jnp.uint32(16)) | hi, jnp.int32)


def _unpack_bf16_pairs(words, dtype):
    u = lax.bitcast_convert_type(words, jnp.uint32)
    lo = lax.bitcast_convert_type(lax.shift_left(u, jnp.uint32(16)), F32)
    hi = lax.bitcast_convert_type(u & jnp.uint32(0xFFFF0000), F32)
    return jnp.concatenate([lo.astype(dtype), hi.astype(dtype)], axis=1)


def _ada_kernel(c_ref, w_ref, b_ref, o_ref):
    c = c_ref[...]
    s = (c * jax.nn.sigmoid(c)).astype(BF16)
    o_ref[...] = _dot(s, w_ref[...].astype(BF16)) + b_ref[...]


def _ada_mods(c_all, ada_w, ada_b):
    n = c_all.shape[0]
    tn = 2048
    return pl.pallas_call(
        _ada_kernel,
        grid=(DEPTH, 6 * D // tn),
        in_specs=[
            pl.BlockSpec((n, D), lambda l, j: (0, 0)),
            pl.BlockSpec((None, D, tn), lambda l, j: (l, 0, j)),
            pl.BlockSpec((None, 1, tn), lambda l, j: (l, 0, j)),
        ],
        out_specs=pl.BlockSpec((None, n, tn), lambda l, j: (l, 0, j)),
        out_shape=jax.ShapeDtypeStruct((DEPTH, n, 6 * D), F32),
        compiler_params=_cparams("parallel", "parallel"),
        name="ada_mods",
    )(c_all, ada_w, ada_b.reshape(DEPTH, 1, 6 * D))


def _mod_specs_prompt(layer, chunks):
    return [
        pl.BlockSpec((None, None, 1, D),
                     lambda i, c=c: (layer, jnp.minimum(i // TILES_PER_SEQ, BATCH - 1), 0, c))
        for c in chunks
    ]


def _mod_specs_sample(layer, chunks, rows, row_block0=0):
    return [
        pl.BlockSpec((None, rows, D), lambda i, c=c: (layer, i + row_block0, c)) for c in chunks
    ]


def _full(shape):
    nd = len(shape)
    return pl.BlockSpec(shape, lambda i: (0,) * nd)


def _layer_block(shape, layer):
    nd = len(shape)
    return pl.BlockSpec((None,) + shape, lambda i: (layer,) + (0,) * nd)


def _plus_moe(x, g0_ref, g1_ref, wts_ref, gate):
    w = wts_ref[...]
    y0 = _unpack_bf16_pairs(g0_ref[...], F32)
    y1 = _unpack_bf16_pairs(g1_ref[...], F32)
    return x + gate * (w[:, 0:1] * y0 + w[:, 1:2] * y1)


def _residual_in(refs, pending):
    if not pending:
        return refs[0][...], refs[1:]
    x_ref, g0_ref, g1_ref, wts_ref, gate_ref = refs[:5]
    return _plus_moe(x_ref[...], g0_ref, g1_ref, wts_ref, gate_ref[...]), refs[5:]


def _pending_prompt(pending, mod_p, layer):
    if pending is None:
        return [], []
    gathered, wts = pending
    specs = [
        pl.BlockSpec((TM, D // 2), lambda i: (i, 0)),
        pl.BlockSpec((TM, D // 2), lambda i: (i + N_TILES, 0)),
        pl.BlockSpec((TM, LANES), lambda i: (i, 0)),
    ] + _mod_specs_prompt(layer - 1, (5,))
    return specs, [gathered, gathered, wts, mod_p]


def _pending_sample(pending, mod_s, layer, rows):
    if pending is None:
        return [], []
    gathered, wts = pending
    first = T_PROMPT // rows
    second = (T_PAD + T_PROMPT) // rows
    specs = [
        pl.BlockSpec((rows, D // 2), lambda i: (first + i, 0)),
        pl.BlockSpec((rows, D // 2), lambda i: (second + i, 0)),
        pl.BlockSpec((rows, LANES), lambda i: (first + i, 0)),
    ] + _mod_specs_sample(layer - 1, (5,), rows)
    return specs, [gathered, gathered, wts, mod_s]


def _mix_a_front(x, sh, sc, ng, win_ref, lng_ref, lnb_ref):
    h = (_rms(x, ng) * (1.0 + sc) + sh).astype(BF16)
    z = jax.nn.gelu(_dot(h, win_ref[...]))
    u = z[:, :D]
    v = _ln(z[:, D:], lng_ref[...], lnb_ref[...])
    return u, v


def _mix_a_prompt_kernel(*refs, pending, first):
    x, refs = _residual_in(refs, pending)
    (sh_ref, sc_ref, gt_ref, ng_ref, win_ref, lng_ref, lnb_ref, ws_ref, bs_ref, wout_ref,
     x1_ref, cv_ref, mixed_ref) = refs
    i = pl.program_id(0)
    u, v = _mix_a_front(x, sh_ref[...], sc_ref[...], ng_ref[...], win_ref, lng_ref, lnb_ref)
    vb = v.astype(BF16)
    n_chunks = TM // CHUNK
    row = lax.broadcasted_iota(jnp.int32, (CHUNK, CHUNK), 0)
    col = lax.broadcasted_iota(jnp.int32, (CHUNK, CHUNK), 1)
    tril = row >= col
    hd_dim = D // A_HEADS
    for hd in range(A_HEADS):
        ws = jnp.where(tril, ws_ref[hd], 0.0).astype(BF16)
        cols = slice(hd * hd_dim, (hd + 1) * hd_dim)
        vcat = jnp.concatenate([vb[c * CHUNK:(c + 1) * CHUNK, cols] for c in range(n_chunks)], axis=1)
        m = _dot(ws, vcat)
        for c in range(n_chunks):
            mixed_ref[c * CHUNK:(c + 1) * CHUNK, cols] = m[:, c * hd_dim:(c + 1) * hd_dim] + bs_ref[hd]
    y = _dot((u * mixed_ref[...]).astype(BF16), wout_ref[...])
    out = x + gt_ref[...] * y
    if first:
        out = jnp.where(i < N_PROMPT_TILES, out, 0.0)
    x1_ref[...] = out

    @pl.when(i % TILES_PER_SEQ == TILES_PER_SEQ - 1)
    def _():
        cv_ref[...] = v[TM - CHUNK:, :]


def _mix_a_sample_kernel(*refs, pending, first):
    if first:
        refs = refs[:1] + refs[2:]
    x, refs = _residual_in(refs, pending)
    (sh_ref, sc_ref, gt_ref, ng_ref, win_ref, lng_ref, lnb_ref, wsd_ref, bsd_ref, wout_ref,
     x1_ref, cv_ref) = refs
    u, v = _mix_a_front(x, sh_ref[...], sc_ref[...], ng_ref[...], win_ref, lng_ref, lnb_ref)
    mixed = wsd_ref[...] * v + bsd_ref[...]
    y = _dot((u * mixed).astype(BF16), wout_ref[...])
    x1_ref[...] = x + gt_ref[...] * y
    cv_ref[...] = v


def _mixer_a(x_in, pending, mod_p, mod_s, ng, layer, la, w_in, ln_g, ln_b, w_s, b_s, w_out):
    first = isinstance(x_in, tuple)
    x_all = x_in[0] if first else x_in
    bs_b = jnp.broadcast_to(b_s[la][:, :, None], (A_HEADS, CHUNK, D // A_HEADS))
    common = [
        _layer_block((D, 2 * D), la), _layer_block((1, D), la), _layer_block((1, D), la),
    ]
    has_pending = pending is not None
    pend_specs, pend_args = _pending_prompt(pending, mod_p, layer)
    last_tile = N_PROMPT_TILES - 1
    x_all, cv_p = pl.pallas_call(
        functools.partial(_mix_a_prompt_kernel, pending=has_pending, first=first),
        grid=(N_TILES if first else N_PROMPT_TILES,),
        in_specs=[pl.BlockSpec((TM, D), lambda i: (jnp.minimum(i, last_tile), 0))] + pend_specs
        + _mod_specs_prompt(layer, (0, 1, 2))
        + [_full((1, D))] + common
        + [_layer_block((A_HEADS, CHUNK, CHUNK), la), _full((A_HEADS, CHUNK, D // A_HEADS)),
           _layer_block((D, D), la)],
        out_specs=[
            pl.BlockSpec((TM, D), lambda i: (i, 0)),
            pl.BlockSpec((None, CHUNK, D), lambda i: (jnp.minimum(i, last_tile) // TILES_PER_SEQ, 0, 0)),
        ],
        out_shape=[
            jax.ShapeDtypeStruct((T_PAD, D), F32),
            jax.ShapeDtypeStruct((BATCH, CHUNK, D), F32),
        ],
        scratch_shapes=[pltpu.VMEM((TM, D), F32)],
        input_output_aliases={} if first else {0: 0},
        compiler_params=_cparams("arbitrary"),
        name="mixer_a_prompt",
    )(x_all, *pend_args, mod_p, mod_p, mod_p, ng, w_in, ln_g, ln_b, w_s, bs_b, w_out)

    wsd = jnp.repeat(w_s[la, :, 0, 0], D // A_HEADS).reshape(1, D)
    bsd = jnp.repeat(b_s[la, :, 0], D // A_HEADS).reshape(1, D)
    sblk = T_PROMPT // DEC_BATCH
    pend_specs, pend_args = _pending_sample(pending, mod_s, layer, DEC_BATCH)
    if first:
        x_specs = [pl.BlockSpec((DEC_BATCH, D), lambda i: (0, 0)), pl.BlockSpec(memory_space=pl.ANY)]
        x_args = [x_in[1], x_all]
    else:
        x_specs = [pl.BlockSpec((DEC_BATCH, D), lambda i: (sblk, 0))]
        x_args = [x_all]
    x_all, cv_s = pl.pallas_call(
        functools.partial(_mix_a_sample_kernel, pending=has_pending, first=first),
        grid=(1,),
        in_specs=x_specs + pend_specs
        + _mod_specs_sample(layer, (0, 1, 2), DEC_BATCH)
        + [_full((1, D))] + common
        + [_full((1, D)), _full((1, D)), _layer_block((D, D), la)],
        out_specs=[
            pl.BlockSpec((DEC_BATCH, D), lambda i: (sblk, 0)),
            pl.BlockSpec((DEC_BATCH, D), lambda i: (0, 0)),
        ],
        out_shape=[
            jax.ShapeDtypeStruct((T_PAD, D), F32),
            jax.ShapeDtypeStruct((DEC_BATCH, D), F32),
        ],
        input_output_aliases={len(x_args) - 1: 0},
        compiler_params=_cparams("arbitrary"),
        name="mixer_a_sample",
    )(*x_args, *pend_args, mod_s, mod_s, mod_s, ng, w_in, ln_g, ln_b, wsd, bsd, w_out)
    return x_all, cv_p, cv_s.reshape(DEC_BATCH, 1, D)


def _mix_b_tail(pooled_groups, wgrp_ref, bgrp_ref, scale_ref, wout_ref):
    outs = [
        _dot(pg.astype(BF16), wgrp_ref[g]) + bgrp_ref[g]
        for g, pg in enumerate(pooled_groups)
    ]
    mixed = jnp.concatenate(outs, axis=1) * scale_ref[...]
    return _dot(mixed.astype(BF16), wout_ref[...])


def _mix_b_prompt_kernel(*refs, pending):
    x, refs = _residual_in(refs, pending)
    (sh_ref, sc_ref, gt_ref, ng_ref, win_ref, wgrp_ref, bgrp_ref, scale_ref, wout_ref,
     x1_ref, st_ref, full_ref, s2_ref, s4_ref, s8_ref) = refs
    assert POOL_WINDOWS == (2, 4, 8, 16)
    i = pl.program_id(0)
    j = i % TILES_PER_SEQ
    h = (_rms(x, ng_ref[...]) * (1.0 + sc_ref[...]) + sh_ref[...]).astype(BF16)
    p = _dot(h, win_ref[...])
    gd = B_GROUP_DIM
    top = SUBLANES
    cur = top + POOL_CARRY
    rows = TM + POOL_CARRY

    @pl.when(i == 0)
    def _():
        for ref in (full_ref, s2_ref, s4_ref, s8_ref):
            ref[0:top, :] = jnp.zeros((top, ref.shape[1]), F32)

    full_ref[top:cur, :] = jnp.where(j == 0, 0.0, full_ref[top:cur, :])
    full_ref[cur:cur + TM, :] = p
    s2_ref[top:top + rows, :] = full_ref[top:top + rows, :] + full_ref[pl.ds(top - 1, rows), :]
    s4_ref[top:top + rows, :] = s2_ref[top:top + rows, gd:] + s2_ref[pl.ds(top - 2, rows), gd:]
    s8_ref[top:top + rows, :] = s4_ref[top:top + rows, gd:] + s4_ref[pl.ds(top - 4, rows), gd:]
    s16 = s8_ref[cur:cur + TM, gd:] + s8_ref[pl.ds(cur - 8, TM), gd:]
    sums = [s2_ref[cur:cur + TM, 0:gd], s4_ref[cur:cur + TM, 0:gd], s8_ref[cur:cur + TM, 0:gd], s16]
    pos = j * TM + lax.broadcasted_iota(jnp.int32, (TM, 1), 0)
    pooled = []
    for g, w in enumerate(POOL_WINDOWS):
        cnt = jnp.minimum(w, pos + 1).astype(F32)
        pooled.append(sums[g] / cnt - p[:, g * gd:(g + 1) * gd])
    y = _mix_b_tail(pooled, wgrp_ref, bgrp_ref, scale_ref, wout_ref)
    x1_ref[...] = x + gt_ref[...] * y

    @pl.when(j == TILES_PER_SEQ - 1)
    def _():
        st_ref[...] = full_ref[pl.ds(cur + TM - POOL_BUF, POOL_BUF), :]

    full_ref[top:cur, :] = full_ref[pl.ds(TM + top, POOL_CARRY), :]


def _mix_b_sample_kernel(*refs, pending):
    x, refs = _residual_in(refs, pending)
    (sh_ref, sc_ref, gt_ref, ng_ref, st_ref, win_ref, wgrp_ref, bgrp_ref, scale_ref, wout_ref,
     x1_ref, nst_ref) = refs
    h = (_rms(x, ng_ref[...]) * (1.0 + sc_ref[...]) + sh_ref[...]).astype(BF16)
    p = _dot(h, win_ref[...])
    pooled = []
    for g, w in enumerate(POOL_WINDOWS):
        s = p[:, g * B_GROUP_DIM:(g + 1) * B_GROUP_DIM]
        for k in range(1, w):
            s = s + st_ref[:, POOL_BUF - k, g * B_GROUP_DIM:(g + 1) * B_GROUP_DIM]
        cnt = float(min(w, PAST_LEN + 1))
        pooled.append(s / cnt - p[:, g * B_GROUP_DIM:(g + 1) * B_GROUP_DIM])
    y = _mix_b_tail(pooled, wgrp_ref, bgrp_ref, scale_ref, wout_ref)
    x1_ref[...] = x + gt_ref[...] * y
    for k in range(1, POOL_BUF):
        nst_ref[k - 1] = st_ref[:, k, :]
    nst_ref[POOL_BUF - 1] = p


def _mixer_b(x_all, pending, mod_p, mod_s, ng, layer, lb, state, w_in, w_grp, b_grp, scale,
             w_out):
    gd = B_GROUP_DIM
    common = [
        _layer_block((D, D), lb), _layer_block((len(POOL_WINDOWS), gd, gd), lb),
        _layer_block((len(POOL_WINDOWS), 1, gd), lb), _layer_block((1, D), lb),
        _layer_block((D, D), lb),
    ]
    has_pending = pending is not None
    pend_specs, pend_args = _pending_prompt(pending, mod_p, layer)
    x_all, st_p = pl.pallas_call(
        functools.partial(_mix_b_prompt_kernel, pending=has_pending),
        grid=(N_PROMPT_TILES,),
        in_specs=[pl.BlockSpec((TM, D), lambda i: (i, 0))] + pend_specs
        + _mod_specs_prompt(layer, (0, 1, 2)) + [_full((1, D))] + common,
        out_specs=[
            pl.BlockSpec((TM, D), lambda i: (i, 0)),
            pl.BlockSpec((None, POOL_BUF, D), lambda i: (i // TILES_PER_SEQ, 0, 0)),
        ],
        out_shape=[
            jax.ShapeDtypeStruct((T_PAD, D), F32),
            jax.ShapeDtypeStruct((BATCH, POOL_BUF, D), F32),
        ],
        scratch_shapes=[
            pltpu.VMEM((SUBLANES + POOL_CARRY + TM, D - g * gd), F32) for g in (0, 0, 1, 2)],
        input_output_aliases={0: 0},
        compiler_params=_cparams("arbitrary"),
        name="mixer_b_prompt",
    )(x_all, *pend_args, mod_p, mod_p, mod_p, ng, w_in, w_grp, b_grp, scale, w_out)

    r = SAMPLE_ROWS
    sblk = T_PROMPT // r
    pend_specs, pend_args = _pending_sample(pending, mod_s, layer, r)
    x_all, st_s = pl.pallas_call(
        functools.partial(_mix_b_sample_kernel, pending=has_pending),
        grid=(DEC_BATCH // r,),
        in_specs=[pl.BlockSpec((r, D), lambda i: (sblk + i, 0))] + pend_specs
        + _mod_specs_sample(layer, (0, 1, 2), r) + [_full((1, D))]
        + [pl.BlockSpec((None, r, POOL_BUF, D), lambda i: (lb, i, 0, 0))] + common,
        out_specs=[
            pl.BlockSpec((r, D), lambda i: (sblk + i, 0)),
            pl.BlockSpec((POOL_BUF, r, D), lambda i: (0, i, 0)),
        ],
        out_shape=[
            jax.ShapeDtypeStruct((T_PAD, D), F32),
            jax.ShapeDtypeStruct((POOL_BUF, DEC_BATCH, D), F32),
        ],
        input_output_aliases={0: 0},
        compiler_params=_cparams("arbitrary"),
        name="mixer_b_sample",
    )(x_all, *pend_args, mod_s, mod_s, mod_s, ng, state, w_in, w_grp, b_grp, scale, w_out)
    return x_all, st_p, jnp.transpose(st_s, (1, 0, 2))


def _mix_c_glu(x, sh, sc, ng, win_ref, bin_ref):
    h = (_rms(x, ng) * (1.0 + sc) + sh).astype(BF16)
    ag = _dot(h, win_ref[...]) + bin_ref[...]
    return ag[:, :D] * jax.nn.sigmoid(ag[:, D:])


def _mix_c_tail(conv, lng_ref, lnb_ref, wout_ref, bout_ref):
    z = _ln(conv, lng_ref[...], lnb_ref[...])
    z = z * jax.nn.sigmoid(z)
    return _dot(z.astype(BF16), wout_ref[...]) + bout_ref[...]


def _mix_c_prompt_kernel(*refs, pending):
    x, refs = _residual_in(refs, pending)
    (sh_ref, sc_ref, gt_ref, ng_ref, win_ref, bin_ref, wdw_ref, bdw_ref, lng_ref, lnb_ref,
     wout_ref, bout_ref, x1_ref, st_ref, full_ref, conv_ref, *shift_refs) = refs
    i = pl.program_id(0)
    j = i % TILES_PER_SEQ
    glu = _mix_c_glu(x, sh_ref[...], sc_ref[...], ng_ref[...], win_ref, bin_ref)

    @pl.when(j == 0)
    def _():
        full_ref[0:CONV_CARRY, :] = jnp.zeros((CONV_CARRY, D), F32)

    full_ref[CONV_CARRY:CONV_CARRY + TM, :] = glu
    off = CONV_CARRY - CONV_BUF
    span = TM + (off + CONV_BUF) // SUBLANES * SUBLANES - SUBLANES
    for c in range(D // LANES):
        cols = slice(c * LANES, (c + 1) * LANES)
        shift_ref = shift_refs[c % len(shift_refs)]
        for r in range(1, SUBLANES):
            shift_ref[r - 1] = full_ref[pl.ds(r, span), cols]
        for rb in range(TM // CONV_ROWS):
            acc = None
            for k in range(CONV_WIDTH):
                q, r = divmod(off + k, SUBLANES)
                start = q * SUBLANES + rb * CONV_ROWS
                if r == 0:
                    src = full_ref[pl.ds(start, CONV_ROWS), cols]
                else:
                    src = shift_ref[r - 1, pl.ds(start, CONV_ROWS), :]
                term = src * wdw_ref[k:k + 1, cols]
                acc = term if acc is None else acc + term
            conv_ref[pl.ds(rb * CONV_ROWS, CONV_ROWS), cols] = acc + bdw_ref[:, cols]
    y = _mix_c_tail(conv_ref[...], lng_ref, lnb_ref, wout_ref, bout_ref)
    x1_ref[...] = x + gt_ref[...] * y

    @pl.when(j == TILES_PER_SEQ - 1)
    def _():
        st_ref[...] = full_ref[pl.ds(CONV_CARRY + TM - CONV_BUF, CONV_BUF), :]

    full_ref[0:CONV_CARRY, :] = full_ref[pl.ds(TM, CONV_CARRY), :]


def _mix_c_sample_kernel(*refs, pending):
    x, refs = _residual_in(refs, pending)
    (sh_ref, sc_ref, gt_ref, ng_ref, st_ref, win_ref, bin_ref, wdw_ref, bdw_ref, lng_ref,
     lnb_ref, wout_ref, bout_ref, x1_ref, nst_ref) = refs
    glu = _mix_c_glu(x, sh_ref[...], sc_ref[...], ng_ref[...], win_ref, bin_ref)
    acc = glu * wdw_ref[CONV_BUF:CONV_BUF + 1, :]
    for k in range(CONV_BUF):
        acc = acc + st_ref[:, k, :] * wdw_ref[k:k + 1, :]
    y = _mix_c_tail(acc + bdw_ref[...], lng_ref, lnb_ref, wout_ref, bout_ref)
    x1_ref[...] = x + gt_ref[...] * y
    for k in range(1, CONV_BUF):
        nst_ref[k - 1] = st_ref[:, k, :]
    nst_ref[CONV_BUF - 1] = glu


def _mixer_c(x_all, pending, mod_p, mod_s, ng, layer, lc, state, w_in, b_in, w_dw, b_dw, ln_g,
             ln_b, w_out, b_out):
    common = [
        _layer_block((D, 2 * D), lc), _layer_block((1, 2 * D), lc),
        _layer_block((CONV_WIDTH, D), lc), _layer_block((1, D), lc), _layer_block((1, D), lc),
        _layer_block((1, D), lc), _layer_block((D, D), lc), _layer_block((1, D), lc),
    ]
    has_pending = pending is not None
    pend_specs, pend_args = _pending_prompt(pending, mod_p, layer)
    x_all, st_p = pl.pallas_call(
        functools.partial(_mix_c_prompt_kernel, pending=has_pending),
        grid=(N_PROMPT_TILES,),
        in_specs=[pl.BlockSpec((TM, D), lambda i: (i, 0))] + pend_specs
        + _mod_specs_prompt(layer, (0, 1, 2)) + [_full((1, D))] + common,
        out_specs=[
            pl.BlockSpec((TM, D), lambda i: (i, 0)),
            pl.BlockSpec((None, CONV_BUF, D), lambda i: (i // TILES_PER_SEQ, 0, 0)),
        ],
        out_shape=[
            jax.ShapeDtypeStruct((T_PAD, D), F32),
            jax.ShapeDtypeStruct((BATCH, CONV_BUF, D), F32),
        ],
        scratch_shapes=[pltpu.VMEM((TM + CONV_CARRY, D), F32), pltpu.VMEM((TM, D), F32)]
        + [pltpu.VMEM((SUBLANES - 1, TM + CONV_CARRY - SUBLANES, LANES), F32)] * 2,
        input_output_aliases={0: 0},
        compiler_params=_cparams("arbitrary"),
        name="mixer_c_prompt",
    )(x_all, *pend_args, mod_p, mod_p, mod_p, ng, w_in, b_in, w_dw, b_dw, ln_g, ln_b, w_out,
      b_out)

    r = SAMPLE_ROWS
    sblk = T_PROMPT // r
    pend_specs, pend_args = _pending_sample(pending, mod_s, layer, r)
    x_all, st_s = pl.pallas_call(
        functools.partial(_mix_c_sample_kernel, pending=has_pending),
        grid=(DEC_BATCH // r,),
        in_specs=[pl.BlockSpec((r, D), lambda i: (sblk + i, 0))] + pend_specs
        + _mod_specs_sample(layer, (0, 1, 2), r) + [_full((1, D))]
        + [pl.BlockSpec((None, r, CONV_BUF, D), lambda i: (lc, i, 0, 0))] + common,
        out_specs=[
            pl.BlockSpec((r, D), lambda i: (sblk + i, 0)),
            pl.BlockSpec((CONV_BUF, r, D), lambda i: (0, i, 0)),
        ],
        out_shape=[
            jax.ShapeDtypeStruct((T_PAD, D), F32),
            jax.ShapeDtypeStruct((CONV_BUF, DEC_BATCH, D), F32),
        ],
        input_output_aliases={0: 0},
        compiler_params=_cparams("arbitrary"),
        name="mixer_c_sample",
    )(x_all, *pend_args, mod_s, mod_s, mod_s, ng, state, w_in, b_in, w_dw, b_dw, ln_g, ln_b,
      w_out, b_out)
    return x_all, st_p, jnp.transpose(st_s, (1, 0, 2))


def _tile_mod(i, p_ref, s_ref):
    s_rows = jnp.concatenate([s_ref[...], jnp.zeros((TM - DEC_BATCH, D), F32)], axis=0)
    return jnp.where(i >= N_PROMPT_TILES, s_rows, p_ref[...])


def _unified_mod_specs(layer, chunks):
    specs = []
    for c in chunks:
        specs.append(pl.BlockSpec(
            (None, None, 1, D),
            lambda i, c=c: (layer, jnp.minimum(i // TILES_PER_SEQ, BATCH - 1), 0, c)))
        specs.append(pl.BlockSpec((None, DEC_BATCH, D), lambda i, c=c: (layer, 0, c)))
    return specs


def _router_kernel(x_ref, shp_ref, shs_ref, scp_ref, scs_ref, ng_ref, wr_ref, br_ref,
                   h2_ref, meta_ref, wts_ref, cnt_ref, carry_ref, before_ref):
    i = pl.program_id(0)

    @pl.when(i == 0)
    def _():
        carry_ref[...] = jnp.zeros((ROUTER_ROWS, 1), F32)
        earlier = lax.broadcasted_iota(jnp.int32, (TM, TM), 0)
        token = lax.broadcasted_iota(jnp.int32, (TM, TM), 1)
        before_ref[...] = (earlier < token).astype(BF16)

    sh = _tile_mod(i, shp_ref, shs_ref)
    sc = _tile_mod(i, scp_ref, scs_ref)
    h2 = _rms(x_ref[...], ng_ref[...]) * (1.0 + sc) + sh
    h2_ref[...] = _pack_bf16_pairs(h2)
    lg = lax.dot_general(wr_ref[...], h2.astype(BF16), (((1,), (1,)), ((), ())),
                         preferred_element_type=F32) + br_ref[...]
    row = lax.broadcasted_iota(jnp.int32, (ROUTER_ROWS, TM), 0)
    row_f = row.astype(F32)
    far = float(ROUTER_ROWS)
    neg = -jnp.inf
    is_g = row < N_GROUPS
    gm = jnp.where(is_g, lg, neg)
    gmax = jnp.max(gm, axis=0, keepdims=True)
    gsel = jnp.min(jnp.where(gm == gmax, row_f, far), axis=0, keepdims=True).astype(jnp.int32)
    gsum = jnp.sum(jnp.where(is_g, jnp.exp(lg - gmax), 0.0), axis=0, keepdims=True)
    g_w = 1.0 / gsum
    e_row = row - EXPERT_LANE0
    in_grp = (e_row >= 0) & (e_row < N_EXPERTS) & (
        lax.shift_right_arithmetic(e_row, EXPERTS_PER_GROUP.bit_length() - 1) == gsel)
    em = jnp.where(in_grp, lg, neg)
    m1 = jnp.max(em, axis=0, keepdims=True)
    i1 = jnp.min(jnp.where(em == m1, row_f, far), axis=0, keepdims=True).astype(jnp.int32)
    em2 = jnp.where(row == i1, neg, em)
    m2 = jnp.max(em2, axis=0, keepdims=True)
    i2 = jnp.min(jnp.where(em2 == m2, row_f, far), axis=0, keepdims=True).astype(jnp.int32)
    e2 = jnp.exp(m2 - m1)
    den = 1.0 + e2
    w1 = (1.0 / den) * g_w
    w2 = (e2 / den) * g_w

    hit1 = row == i1
    hit2 = row == i2
    assign = jnp.where(hit1 | hit2, 1.0, 0.0).astype(BF16)
    seen = _dot(assign, before_ref[...]) + carry_ref[...]
    rank1 = jnp.sum(jnp.where(hit1, seen, 0.0), axis=0, keepdims=True)
    rank2 = jnp.sum(jnp.where(hit2, seen, 0.0), axis=0, keepdims=True)
    carry_ref[...] = carry_ref[...] + jnp.sum(assign.astype(F32), axis=1, keepdims=True)
    cnt_ref[...] = carry_ref[...].astype(jnp.int32)

    ex1 = (i1 - EXPERT_LANE0).astype(F32)
    ex2 = (i2 - EXPERT_LANE0).astype(F32)
    field = lax.broadcasted_iota(jnp.int32, (META_ROWS, TM), 0)
    meta_ref[...] = jnp.where(field == 0, ex1, jnp.where(field == 1, ex2,
                              jnp.where(field == 2, rank1, jnp.where(field == 3, rank2, 0.0))))
    slot = lax.broadcasted_iota(jnp.int32, (LANES, TM), 0)
    wts_ref[...] = jnp.where(slot == 0, w1, jnp.where(slot == 1, w2, 0.0)).T


def _router(x_all, mod_p, mod_s, ng, layer, w_r, b_r):
    return pl.pallas_call(
        _router_kernel,
        grid=(N_TILES,),
        in_specs=[pl.BlockSpec((TM, D), lambda i: (i, 0))]
        + _unified_mod_specs(layer, (3, 4))
        + [_full((1, D)), _full((ROUTER_ROWS, D)), _full((ROUTER_ROWS, 1))],
        out_specs=[
            pl.BlockSpec((TM, D // 2), lambda i: (i, 0)),
            pl.BlockSpec((META_ROWS, TM), lambda i: (0, i)),
            pl.BlockSpec((TM, LANES), lambda i: (i, 0)),
            pl.BlockSpec((ROUTER_ROWS, 1), lambda i: (0, 0)),
        ],
        out_shape=[
            jax.ShapeDtypeStruct((T_PAD, D // 2), jnp.int32),
            jax.ShapeDtypeStruct((META_ROWS, T_PAD), F32),
            jax.ShapeDtypeStruct((T_PAD, LANES), F32),
            jax.ShapeDtypeStruct((ROUTER_ROWS, 1), jnp.int32),
        ],
        scratch_shapes=[pltpu.VMEM((ROUTER_ROWS, 1), F32), pltpu.VMEM((TM, TM), BF16)],
        compiler_params=_cparams("arbitrary"),
        name="router",
    )(x_all, mod_p, mod_s, mod_p, mod_s, ng, w_r, b_r)


SC_INDEX_MAX = 128
SC_ROWS_BYTES = 256 * 1024


def _gather_chunk(per_worker, row_bytes):
    top = min(SC_INDEX_MAX, SC_ROWS_BYTES // row_bytes) // SUBLANES * SUBLANES
    for ch in range(top, SUBLANES - 1, -SUBLANES):
        if per_worker % ch == 0:
            return ch
    raise ValueError(per_worker)


def _sc_gather(table, idx):
    nc, nw = SC_CORES, SC_CORES * SC_SUBCORES
    m = idx.shape[0]
    per_w = m // nw
    assert per_w * nw == m
    width = table.shape[1]
    ch = _gather_chunk(per_w, width * table.dtype.itemsize)
    mesh = plsc.VectorSubcoreMesh(core_axis_name="c", subcore_axis_name="s")

    @functools.partial(
        pl.kernel,
        out_type=jax.ShapeDtypeStruct((m, width), table.dtype),
        mesh=mesh,
        scratch_types=[
            pltpu.VMEM((ch,), jnp.int32),
            pltpu.VMEM((ch, width), table.dtype),
            pltpu.SemaphoreType.DMA,
        ],
    )
    def gather_kernel(t_hbm, i_hbm, o_hbm, idx_v, rows_v, sem):
        wid = lax.axis_index("s") * nc + lax.axis_index("c")
        base = wid * per_w

        @pl.loop(0, per_w // ch)
        def _(j):
            off = pl.multiple_of(base + j * ch, 8)
            pltpu.sync_copy(i_hbm.at[pl.ds(off, ch)], idx_v)
            pltpu.async_copy(t_hbm.at[idx_v], rows_v, sem).wait()
            pltpu.sync_copy(rows_v, o_hbm.at[pl.ds(off, ch)])

    return gather_kernel(table, idx)


def _sc_scatter2(rows, dest0, dest1, n_out):
    nc, nw = SC_CORES, SC_CORES * SC_SUBCORES
    n, width = rows.shape
    per_w = n // nw
    assert per_w * nw == n
    ch = _gather_chunk(per_w, width * rows.dtype.itemsize)
    mesh = plsc.VectorSubcoreMesh(core_axis_name="c", subcore_axis_name="s")

    @functools.partial(
        pl.kernel,
        out_type=jax.ShapeDtypeStruct((n_out, width), rows.dtype),
        mesh=mesh,
        scratch_types=[
            pltpu.VMEM((ch,), jnp.int32),
            pltpu.VMEM((ch,), jnp.int32),
            pltpu.VMEM((ch, width), rows.dtype),
            pltpu.SemaphoreType.DMA,
            pltpu.SemaphoreType.DMA,
        ],
    )
    def scatter_kernel(r_hbm, d0_hbm, d1_hbm, o_hbm, i0_v, i1_v, rows_v, sem0, sem1):
        wid = lax.axis_index("s") * nc + lax.axis_index("c")
        base = wid * per_w

        @pl.loop(0, per_w // ch)
        def _(j):
            off = pl.multiple_of(base + j * ch, 8)
            pltpu.sync_copy(d0_hbm.at[pl.ds(off, ch)], i0_v)
            pltpu.sync_copy(d1_hbm.at[pl.ds(off, ch)], i1_v)
            pltpu.sync_copy(r_hbm.at[pl.ds(off, ch)], rows_v)
            c0 = pltpu.async_copy(rows_v, o_hbm.at[i0_v], sem0)
            c1 = pltpu.async_copy(rows_v, o_hbm.at[i1_v], sem1)
            c0.wait()
            c1.wait()

    return scatter_kernel(rows, dest0, dest1)


def _expert_kernel(be_ref, first_ref, nvalid_ref, next_ref, nused_ref, x_ref, wg_hbm, wu_hbm,
                   wd_hbm, y_ref, wgf_ref, wuf_ref, wdf_ref, wgb_ref, wub_ref, wdb_ref, sems,
                   *, layer):
    b = pl.program_id(0)

    def weight_copies(e):
        return (
            pltpu.make_async_copy(wg_hbm.at[layer, e], wgf_ref, sems.at[0]),
            pltpu.make_async_copy(wu_hbm.at[layer, e], wuf_ref, sems.at[1]),
            pltpu.make_async_copy(wd_hbm.at[layer, e], wdf_ref, sems.at[2]),
        )

    @pl.when(b == 0)
    def _():
        for cp in weight_copies(be_ref[0]):
            cp.start()

    @pl.when(b < nused_ref[0])
    def _():
        @pl.when(first_ref[b] == 1)
        def _():
            for cp in weight_copies(be_ref[b]):
                cp.wait()
            wgb_ref[...] = wgf_ref[...].astype(BF16)
            wub_ref[...] = wuf_ref[...].astype(BF16)
            wdb_ref[...] = wdf_ref[...].astype(BF16)

            @pl.when(next_ref[b] >= 0)
            def _():
                for cp in weight_copies(next_ref[b]):
                    cp.start()

        def run(rows):
            live = lax.broadcasted_iota(jnp.int32, (rows, 1), 0) < nvalid_ref[b]
            x = _unpack_bf16_pairs(jnp.where(live, x_ref[0:rows, :], 0), BF16)
            g = _dot(x, wgb_ref[...])
            u = _dot(x, wub_ref[...])
            hmid = (g * jax.nn.sigmoid(g)) * u
            y_ref[0:rows, :] = _pack_bf16_pairs(_dot(hmid.astype(BF16), wdb_ref[...]))

        half = BM // 2

        @pl.when(nvalid_ref[b] > half)
        def _():
            run(BM)

        @pl.when(nvalid_ref[b] <= half)
        def _():
            run(half)
            y_ref[half:BM, :] = jnp.zeros((BM - half, D // 2), jnp.int32)

    @pl.when(b >= nused_ref[0])
    def _():
        y_ref[...] = jnp.zeros((BM, D // 2), jnp.int32)


def _experts(x_sorted, block_expert, first, nvalid, next_expert, nused, layer, w_gate, w_up,
             w_down):
    def row_map(b, be, fi, nv, nx, nu):
        return (jnp.minimum(b, nu[0] - 1), 0)

    def out_map(b, be, fi, nv, nx, nu):
        return (b, 0)

    grid_spec = pltpu.PrefetchScalarGridSpec(
        num_scalar_prefetch=5,
        grid=(N_BLOCKS,),
        in_specs=[
            pl.BlockSpec((BM, D // 2), row_map),
            pl.BlockSpec(memory_space=pl.ANY),
            pl.BlockSpec(memory_space=pl.ANY),
            pl.BlockSpec(memory_space=pl.ANY),
        ],
        out_specs=pl.BlockSpec((BM, D // 2), out_map),
        scratch_shapes=[
            pltpu.VMEM((D, D_EXPERT), F32),
            pltpu.VMEM((D, D_EXPERT), F32),
            pltpu.VMEM((D_EXPERT, D), F32),
            pltpu.VMEM((D, D_EXPERT), BF16),
            pltpu.VMEM((D, D_EXPERT), BF16),
            pltpu.VMEM((D_EXPERT, D), BF16),
            pltpu.SemaphoreType.DMA((3,)),
        ],
    )
    return pl.pallas_call(
        functools.partial(_expert_kernel, layer=layer),
        grid_spec=grid_spec,
        out_shape=jax.ShapeDtypeStruct((P_ROWS, D // 2), jnp.int32),
        compiler_params=_cparams("arbitrary"),
        name="experts",
    )(block_expert, first, nvalid, next_expert, nused, x_sorted, w_gate, w_up, w_down)


def _combine_final_kernel(x_ref, g0_ref, g1_ref, wts_ref, gtp_ref, gts_ref, fg_ref, yp_ref,
                          ys_ref):
    i = pl.program_id(0)
    gate = _tile_mod(i, gtp_ref, gts_ref)
    y = _rms(_plus_moe(x_ref[...], g0_ref, g1_ref, wts_ref, gate), fg_ref[...])

    @pl.when(i < N_PROMPT_TILES)
    def _():
        yp_ref[...] = y

    @pl.when(i == N_PROMPT_TILES)
    def _():
        ys_ref[...] = y[0:DEC_BATCH, :]


def _combine_final(x_all, gathered, wts, mod_p, mod_s, layer, final_g):
    in_specs = [
        pl.BlockSpec((TM, D), lambda i: (i, 0)),
        pl.BlockSpec((TM, D // 2), lambda i: (i, 0)),
        pl.BlockSpec((TM, D // 2), lambda i: (i + N_TILES, 0)),
        pl.BlockSpec((TM, LANES), lambda i: (i, 0)),
    ] + _unified_mod_specs(layer, (5,))
    return pl.pallas_call(
        _combine_final_kernel,
        grid=(N_PROMPT_TILES + 1,),
        in_specs=in_specs + [_full((1, D))],
        out_specs=[
            pl.BlockSpec((TM, D), lambda i: (jnp.minimum(i, N_PROMPT_TILES - 1), 0)),
            pl.BlockSpec((DEC_BATCH, D), lambda i: (0, 0)),
        ],
        out_shape=[
            jax.ShapeDtypeStruct((T_PROMPT, D), F32),
            jax.ShapeDtypeStruct((DEC_BATCH, D), F32),
        ],
        compiler_params=_cparams("arbitrary"),
        name="combine_final",
    )(x_all, gathered, gathered, wts, mod_p, mod_s, final_g)


def _plan_kernel(cnt_ref, meta_ref, d0_ref, d1_ref, be_ref, first_ref, nvalid_ref, next_ref,
                 nused_ref, pstart_ref, nxt_ref):
    shift = BM.bit_length() - 1
    assert 1 << shift == BM

    def count(e):
        return cnt_ref[EXPERT_LANE0 + e, 0]

    def scan_back(k, nxt):
        e = N_EXPERTS - 1 - k
        nxt_ref[e] = nxt
        return jnp.where(count(e) > 0, e, nxt)

    lax.fori_loop(0, N_EXPERTS, scan_back, jnp.int32(-1))

    def fill(e, blk):
        c = count(e)
        pstart_ref[e] = blk << shift

        def one(j, carry):
            be_ref[blk + j] = e
            first_ref[blk + j] = jnp.where(j == 0, 1, 0)
            nvalid_ref[blk + j] = jnp.minimum(c - (j << shift), BM)
            next_ref[blk + j] = nxt_ref[e]
            return carry

        n_blk = (c + (BM - 1)) >> shift
        lax.fori_loop(0, n_blk, one, 0)
        return blk + n_blk

    used = lax.fori_loop(0, N_EXPERTS, fill, jnp.int32(0))
    nused_ref[0] = used

    def tail(b, carry):
        be_ref[b] = 0
        first_ref[b] = 0
        nvalid_ref[b] = 0
        next_ref[b] = -1
        return carry

    lax.fori_loop(used, N_BLOCKS, tail, 0)

    m = meta_ref[...]
    expert = m[0:2, :]
    dest = m[2:4, :]
    for e in range(N_EXPERTS):
        dest = dest + jnp.where(expert == float(e), pstart_ref[e].astype(F32), 0.0)
    dest = dest.astype(jnp.int32)
    d0_ref[...] = dest[0:1, :]
    d1_ref[...] = dest[1:2, :]


def _plan(cnt, meta):
    smem = pl.BlockSpec(memory_space=pltpu.SMEM)
    blocks = jax.ShapeDtypeStruct((N_BLOCKS,), jnp.int32)
    return pl.pallas_call(
        _plan_kernel,
        grid=(1,),
        in_specs=[smem, _full((META_ROWS, T_PAD))],
        out_specs=[_full((1, T_PAD)), _full((1, T_PAD)), smem, smem, smem, smem, smem],
        out_shape=[
            jax.ShapeDtypeStruct((1, T_PAD), jnp.int32),
            jax.ShapeDtypeStruct((1, T_PAD), jnp.int32),
            blocks, blocks, blocks, blocks,
            jax.ShapeDtypeStruct((1,), jnp.int32),
        ],
        scratch_shapes=[pltpu.SMEM((N_EXPERTS,), jnp.int32), pltpu.SMEM((N_EXPERTS,), jnp.int32)],
        compiler_params=_cparams("arbitrary"),
        name="plan",
    )(cnt, meta)


def _moe(x_all, mod_p, mod_s, ng, layer, w_r, b_r, w_gate, w_up, w_down):
    h2, meta, wts, cnt = _router(x_all, mod_p, mod_s, ng, layer, w_r, b_r)
    dest0, dest1, block_expert, first, nvalid, next_expert, nused = _plan(cnt, meta)
    dest0, dest1 = dest0.reshape(T_PAD), dest1.reshape(T_PAD)
    x_sorted = _sc_scatter2(h2, dest0, dest1, P_ROWS)
    y_sorted = _experts(x_sorted, block_expert, first, nvalid, next_expert, nused, layer,
                        w_gate, w_up, w_down)
    gathered = _sc_gather(y_sorted, jnp.concatenate([dest0, dest1]))
    return gathered, wts


def kernel(x_prompt, x_sample, c_prompt, c_sample, state_pool, state_conv, ada_w, ada_b, norm_g, final_g, a_w_in, a_ln_g, a_ln_b, a_w_s, a_b_s, a_w_out, b_w_in, b_w_grp, b_b_grp, b_scale, b_w_out, c_w_in, c_b_in, c_w_dw, c_b_dw, c_ln_g, c_ln_b, c_w_out, c_b_out, moe_w_grp, moe_b_grp, moe_w_exp, moe_b_exp, moe_w_gate, moe_w_up, moe_w_down):
    x_all = (x_prompt.reshape(T_PROMPT, D), x_sample.reshape(DEC_BATCH, D))
    mods = _ada_mods(jnp.concatenate([c_prompt, c_sample], axis=0), ada_w, ada_b)
    mod_p = mods[:, :BATCH].reshape(DEPTH, BATCH, 1, 6 * D)
    mod_s = mods[:, BATCH:]

    a_w_in_b, a_w_out_b = a_w_in.astype(BF16), a_w_out.astype(BF16)
    b_w_in_b, b_w_grp_b, b_w_out_b = b_w_in.astype(BF16), b_w_grp.astype(BF16), b_w_out.astype(BF16)
    c_w_in_b, c_w_out_b = c_w_in.astype(BF16), c_w_out.astype(BF16)
    pad_rows = ROUTER_ROWS - N_GROUPS - N_EXPERTS
    w_r = jnp.pad(jnp.swapaxes(jnp.concatenate([moe_w_grp, moe_w_exp], axis=2), 1, 2),
                  ((0, 0), (0, pad_rows), (0, 0))).astype(BF16)
    b_r = jnp.pad(jnp.concatenate([moe_b_grp, moe_b_exp], axis=1), ((0, 0), (0, pad_rows)))

    new_a_p, new_a_s, new_b_p, new_b_s, new_c_p, new_c_s = [], [], [], [], [], []
    ia = ib = ic = 0
    pending = None
    for layer in range(DEPTH):
        ng1 = norm_g[layer, 0].reshape(1, D)
        ng2 = norm_g[layer, 1].reshape(1, D)
        kind = layer % 3
        if kind == 0:
            x_all, st_p, st_s = _mixer_a(
                x_all, pending, mod_p, mod_s, ng1, layer, ia, a_w_in_b, a_ln_g.reshape(-1, 1, D),
                a_ln_b.reshape(-1, 1, D), a_w_s, a_b_s, a_w_out_b)
            new_a_p.append(st_p)
            new_a_s.append(st_s)
            ia += 1
        elif kind == 1:
            x_all, st_p, st_s = _mixer_b(
                x_all, pending, mod_p, mod_s, ng1, layer, ib, state_pool, b_w_in_b, b_w_grp_b,
                b_b_grp.reshape(-1, len(POOL_WINDOWS), 1, B_GROUP_DIM),
                b_scale.reshape(-1, 1, D), b_w_out_b)
            new_b_p.append(st_p)
            new_b_s.append(st_s)
            ib += 1
        else:
            x_all, st_p, st_s = _mixer_c(
                x_all, pending, mod_p, mod_s, ng1, layer, ic, state_conv, c_w_in_b,
                c_b_in.reshape(-1, 1, 2 * D), c_w_dw, c_b_dw.reshape(-1, 1, D),
                c_ln_g.reshape(-1, 1, D), c_ln_b.reshape(-1, 1, D), c_w_out_b,
                c_b_out.reshape(-1, 1, D))
            new_c_p.append(st_p)
            new_c_s.append(st_s)
            ic += 1
        pending = _moe(x_all, mod_p, mod_s, ng2, layer, w_r[layer],
                       b_r[layer].reshape(ROUTER_ROWS, 1), moe_w_gate, moe_w_up, moe_w_down)

    y_p, y_s = _combine_final(x_all, *pending, mod_p, mod_s, DEPTH - 1, final_g.reshape(1, D))
    return (y_p.reshape(BATCH, SEQ, D), y_s.reshape(DEC_BATCH, 1, D),
            jnp.stack(new_a_p), jnp.stack(new_a_s), jnp.stack(new_b_p), jnp.stack(new_b_s),
            jnp.stack(new_c_p), jnp.stack(new_c_s))
```

```python
import functools

import jax
import jax.numpy as jnp
from jax import lax
from jax.experimental import pallas as pl
from jax.experimental.pallas import tpu as pltpu
from jax.experimental.pallas import tpu_sc as plsc

F32 = jnp.float32
BF16 = jnp.bfloat16

D = 1024
BATCH = 8
SEQ = 2048
DEPTH = 4
DEC_BATCH = 128
PAST_LEN = 16384
CHUNK = 128
A_HEADS = 8
POOL_WINDOWS = (2, 4, 8, 16)
B_GROUP_DIM = D // len(POOL_WINDOWS)
POOL_BUF = max(POOL_WINDOWS) - 1
CONV_WIDTH = 31
CONV_BUF = CONV_WIDTH - 1
N_GROUPS = 4
EXPERTS_PER_GROUP = 8
N_EXPERTS = N_GROUPS * EXPERTS_PER_GROUP
D_EXPERT = D // 2
EPS = 1e-6

LANES = 128
SUBLANES = 8
CONV_ROWS = 128
SC_CORES = 2
SC_SUBCORES = 16
TM = 512
TILES_PER_SEQ = SEQ // TM
T_PROMPT = BATCH * SEQ
T_ALL = T_PROMPT + DEC_BATCH
N_TILES = -(-T_ALL // TM)
T_PAD = N_TILES * TM
N_PROMPT_TILES = T_PROMPT // TM
SAMPLE_ROWS = 32
BM = 1024
EXPERT_ROW_GROUP = 256
N_BLOCKS = -(-(2 * T_PAD) // BM) + N_EXPERTS
P_ROWS = N_BLOCKS * BM
META_ROWS = 8
POOL_CARRY = 16
CONV_CARRY = 32
EXPERT_LANE0 = N_GROUPS
ROUTER_ROWS = 48
VMEM_LIMIT = 56 * 1024 * 1024


def _cparams(*sem):
    return pltpu.CompilerParams(dimension_semantics=sem, vmem_limit_bytes=VMEM_LIMIT)


def _rms(x, g):
    return x * lax.rsqrt(jnp.mean(x * x, axis=-1, keepdims=True) + EPS) * g


def _ln(x, g, b):
    mu = jnp.mean(x, axis=-1, keepdims=True)
    xc = x - mu
    var = jnp.mean(xc * xc, axis=-1, keepdims=True)
    return xc * lax.rsqrt(var + EPS) * g + b


def _dot(a, b):
    return jnp.dot(a, b, preferred_element_type=F32)


def _pack_bf16_pairs(v):
    w = v.shape[1] // 2
    lo = lax.bitcast_convert_type(v[:, :w].astype(BF16).astype(F32), jnp.uint32)
    hi = lax.bitcast_convert_type(v[:, w:].astype(BF16).astype(F32), jnp.uint32)
    return lax.bitcast_convert_type(lax.shift_right_logical(lo, jnp.uint32(16)) | hi, jnp.int32)


def _unpack_bf16_pairs(words, dtype):
    u = lax.bitcast_convert_type(words, jnp.uint32)
    lo = lax.bitcast_convert_type(lax.shift_left(u, jnp.uint32(16)), F32)
    hi = lax.bitcast_convert_type(u & jnp.uint32(0xFFFF0000), F32)
    return jnp.concatenate([lo.astype(dtype), hi.astype(dtype)], axis=1)


def _ada_kernel(c_ref, w_ref, b_ref, o_ref):
    c = c_ref[...]
    s = (c * jax.nn.sigmoid(c)).astype(BF16)
    o_ref[...] = _dot(s, w_ref[...].astype(BF16)) + b_ref[...]


def _ada_mods(c_all, ada_w, ada_b):
    n = c_all.shape[0]
    tn = 2048
    return pl.pallas_call(
        _ada_kernel,
        grid=(DEPTH, 6 * D // tn),
        in_specs=[
            pl.BlockSpec((n, D), lambda l, j: (0, 0)),
            pl.BlockSpec((None, D, tn), lambda l, j: (l, 0, j)),
            pl.BlockSpec((None, 1, tn), lambda l, j: (l, 0, j)),
        ],
        out_specs=pl.BlockSpec((None, n, tn), lambda l, j: (l, 0, j)),
        out_shape=jax.ShapeDtypeStruct((DEPTH, n, 6 * D), F32),
        compiler_params=_cparams("parallel", "parallel"),
        name="ada_mods",
    )(c_all, ada_w, ada_b.reshape(DEPTH, 1, 6 * D))


def _mod_specs_prompt(layer, chunks):
    return [
        pl.BlockSpec((None, None, 1, D),
                     lambda i, c=c: (layer, jnp.minimum(i // TILES_PER_SEQ, BATCH - 1), 0, c))
        for c in chunks
    ]


def _mod_specs_sample(layer, chunks, rows, row_block0=0):
    return [
        pl.BlockSpec((None, rows, D), lambda i, c=c: (layer, i + row_block0, c)) for c in chunks
    ]


def _full(shape):
    nd = len(shape)
    return pl.BlockSpec(shape, lambda i: (0,) * nd)


def _layer_block(shape, layer):
    nd = len(shape)
    return pl.BlockSpec((None,) + shape, lambda i: (layer,) + (0,) * nd)


def _plus_moe(x, g0_ref, g1_ref, wts_ref, gate):
    w = wts_ref[...]
    y0 = _unpack_bf16_pairs(g0_ref[...], F32)
    y1 = _unpack_bf16_pairs(g1_ref[...], F32)
    return x + gate * (w[:, 0:1] * y0 + w[:, 1:2] * y1)


def _residual_in(refs, pending):
    if not pending:
        return refs[0][...], refs[1:]
    x_ref, g0_ref, g1_ref, wts_ref, gate_ref = refs[:5]
    return _plus_moe(x_ref[...], g0_ref, g1_ref, wts_ref, gate_ref[...]), refs[5:]


def _pending_prompt(pending, mod_p, layer):
    if pending is None:
        return [], []
    gathered, wts = pending
    specs = [
        pl.BlockSpec((TM, D // 2), lambda i: (i, 0)),
        pl.BlockSpec((TM, D // 2), lambda i: (i + N_TILES, 0)),
        pl.BlockSpec((TM, LANES), lambda i: (i, 0)),
    ] + _mod_specs_prompt(layer - 1, (5,))
    return specs, [gathered, gathered, wts, mod_p]


def _pending_sample(pending, mod_s, layer, rows):
    if pending is None:
        return [], []
    gathered, wts = pending
    first = T_PROMPT // rows
    second = (T_PAD + T_PROMPT) // rows
    specs = [
        pl.BlockSpec((rows, D // 2), lambda i: (first + i, 0)),
        pl.BlockSpec((rows, D // 2), lambda i: (second + i, 0)),
        pl.BlockSpec((rows, LANES), lambda i: (first + i, 0)),
    ] + _mod_specs_sample(layer - 1, (5,), rows)
    return specs, [gathered, gathered, wts, mod_s]


def _mix_a_front(x, sh, sc, ng, win_ref, lng_ref, lnb_ref):
    h = (_rms(x, ng) * (1.0 + sc) + sh).astype(BF16)
    z = jax.nn.gelu(_dot(h, win_ref[...]))
    u = z[:, :D]
    v = _ln(z[:, D:], lng_ref[...], lnb_ref[...])
    return u, v


def _mix_a_prompt_kernel(*refs, pending, first):
    x, refs = _residual_in(refs, pending)
    (sh_ref, sc_ref, gt_ref, ng_ref, win_ref, lng_ref, lnb_ref, ws_ref, bs_ref, wout_ref,
     x1_ref, cv_ref, mixed_ref) = refs
    i = pl.program_id(0)
    u, v = _mix_a_front(x, sh_ref[...], sc_ref[...], ng_ref[...], win_ref, lng_ref, lnb_ref)
    vb = v.astype(BF16)
    n_chunks = TM // CHUNK
    row = lax.broadcasted_iota(jnp.int32, (CHUNK, CHUNK), 0)
    col = lax.broadcasted_iota(jnp.int32, (CHUNK, CHUNK), 1)
    tril = row >= col
    hd_dim = D // A_HEADS
    for hd in range(A_HEADS):
        ws = jnp.where(tril, ws_ref[hd], 0.0).astype(BF16)
        cols = slice(hd * hd_dim, (hd + 1) * hd_dim)
        vcat = jnp.concatenate([vb[c * CHUNK:(c + 1) * CHUNK, cols] for c in range(n_chunks)], axis=1)
        m = _dot(ws, vcat)
        for c in range(n_chunks):
            mixed_ref[c * CHUNK:(c + 1) * CHUNK, cols] = m[:, c * hd_dim:(c + 1) * hd_dim] + bs_ref[hd]
    y = _dot((u * mixed_ref[...]).astype(BF16), wout_ref[...])
    out = x + gt_ref[...] * y
    if first:
        out = jnp.where(i < N_PROMPT_TILES, out, 0.0)
    x1_ref[...] = out

    @pl.when(i % TILES_PER_SEQ == TILES_PER_SEQ - 1)
    def _():
        cv_ref[...] = v[TM - CHUNK:, :]


def _mix_a_sample_kernel(*refs, pending, first):
    if first:
        refs = refs[:1] + refs[2:]
    x, refs = _residual_in(refs, pending)
    (sh_ref, sc_ref, gt_ref, ng_ref, win_ref, lng_ref, lnb_ref, wsd_ref, bsd_ref, wout_ref,
     x1_ref, cv_ref) = refs
    u, v = _mix_a_front(x, sh_ref[...], sc_ref[...], ng_ref[...], win_ref, lng_ref, lnb_ref)
    mixed = wsd_ref[...] * v + bsd_ref[...]
    y = _dot((u * mixed).astype(BF16), wout_ref[...])
    x1_ref[...] = x + gt_ref[...] * y
    cv_ref[...] = v


def _mixer_a(x_in, pending, mod_p, mod_s, ng, layer, la, w_in, ln_g, ln_b, w_s, b_s, w_out):
    first = isinstance(x_in, tuple)
    x_all = x_in[0] if first else x_in
    bs_b = jnp.broadcast_to(b_s[la][:, :, None], (A_HEADS, CHUNK, D // A_HEADS))
    common = [
        _layer_block((D, 2 * D), la), _layer_block((1, D), la), _layer_block((1, D), la),
    ]
    has_pending = pending is not None
    pend_specs, pend_args = _pending_prompt(pending, mod_p, layer)
    last_tile = N_PROMPT_TILES - 1
    x_all, cv_p = pl.pallas_call(
        functools.partial(_mix_a_prompt_kernel, pending=has_pending, first=first),
        grid=(N_TILES if first else N_PROMPT_TILES,),
        in_specs=[pl.BlockSpec((TM, D), lambda i: (jnp.minimum(i, last_tile), 0))] + pend_specs
        + _mod_specs_prompt(layer, (0, 1, 2))
        + [_full((1, D))] + common
        + [_layer_block((A_HEADS, CHUNK, CHUNK), la), _full((A_HEADS, CHUNK, D // A_HEADS)),
           _layer_block((D, D), la)],
        out_specs=[
            pl.BlockSpec((TM, D), lambda i: (i, 0)),
            pl.BlockSpec((None, CHUNK, D), lambda i: (jnp.minimum(i, last_tile) // TILES_PER_SEQ, 0, 0)),
        ],
        out_shape=[
            jax.ShapeDtypeStruct((T_PAD, D), F32),
            jax.ShapeDtypeStruct((BATCH, CHUNK, D), F32),
        ],
        scratch_shapes=[pltpu.VMEM((TM, D), F32)],
        input_output_aliases={} if first else {0: 0},
        compiler_params=_cparams("arbitrary"),
        name="mixer_a_prompt",
    )(x_all, *pend_args, mod_p, mod_p, mod_p, ng, w_in, ln_g, ln_b, w_s, bs_b, w_out)

    wsd = jnp.repeat(w_s[la, :, 0, 0], D // A_HEADS).reshape(1, D)
    bsd = jnp.repeat(b_s[la, :, 0], D // A_HEADS).reshape(1, D)
    sblk = T_PROMPT // DEC_BATCH
    pend_specs, pend_args = _pending_sample(pending, mod_s, layer, DEC_BATCH)
    if first:
        x_specs = [pl.BlockSpec((DEC_BATCH, D), lambda i: (0, 0)), pl.BlockSpec(memory_space=pl.ANY)]
        x_args = [x_in[1], x_all]
    else:
        x_specs = [pl.BlockSpec((DEC_BATCH, D), lambda i: (sblk, 0))]
        x_args = [x_all]
    x_all, cv_s = pl.pallas_call(
        functools.partial(_mix_a_sample_kernel, pending=has_pending, first=first),
        grid=(1,),
        in_specs=x_specs + pend_specs
        + _mod_specs_sample(layer, (0, 1, 2), DEC_BATCH)
        + [_full((1, D))] + common
        + [_full((1, D)), _full((1, D)), _layer_block((D, D), la)],
        out_specs=[
            pl.BlockSpec((DEC_BATCH, D), lambda i: (sblk, 0)),
            pl.BlockSpec((DEC_BATCH, D), lambda i: (0, 0)),
        ],
        out_shape=[
            jax.ShapeDtypeStruct((T_PAD, D), F32),
            jax.ShapeDtypeStruct((DEC_BATCH, D), F32),
        ],
        input_output_aliases={len(x_args) - 1: 0},
        compiler_params=_cparams("arbitrary"),
        name="mixer_a_sample",
    )(*x_args, *pend_args, mod_s, mod_s, mod_s, ng, w_in, ln_g, ln_b, wsd, bsd, w_out)
    return x_all, cv_p, cv_s.reshape(DEC_BATCH, 1, D)


def _mix_b_tail(pooled_groups, wgrp_ref, bgrp_ref, scale_ref, wout_ref):
    outs = [
        _dot(pg.astype(BF16), wgrp_ref[g]) + bgrp_ref[g]
        for g, pg in enumerate(pooled_groups)
    ]
    mixed = jnp.concatenate(outs, axis=1) * scale_ref[...]
    return _dot(mixed.astype(BF16), wout_ref[...])


def _mix_b_prompt_kernel(*refs, pending):
    x, refs = _residual_in(refs, pending)
    (sh_ref, sc_ref, gt_ref, ng_ref, win_ref, wgrp_ref, bgrp_ref, scale_ref, wout_ref,
     x1_ref, st_ref, full_ref, s2_ref, s4_ref, s8_ref) = refs
    assert POOL_WINDOWS == (2, 4, 8, 16)
    i = pl.program_id(0)
    j = i % TILES_PER_SEQ
    h = (_rms(x, ng_ref[...]) * (1.0 + sc_ref[...]) + sh_ref[...]).astype(BF16)
    p = _dot(h, win_ref[...])
    gd = B_GROUP_DIM
    top = SUBLANES
    cur = top + POOL_CARRY
    rows = TM + POOL_CARRY

    @pl.when(i == 0)
    def _():
        for ref in (full_ref, s2_ref, s4_ref, s8_ref):
            ref[0:top, :] = jnp.zeros((top, ref.shape[1]), F32)

    full_ref[top:cur, :] = jnp.where(j == 0, 0.0, full_ref[top:cur, :])
    full_ref[cur:cur + TM, :] = p
    s2_ref[top:top + rows, :] = full_ref[top:top + rows, :] + full_ref[pl.ds(top - 1, rows), :]
    s4_ref[top:top + rows, :] = s2_ref[top:top + rows, gd:] + s2_ref[pl.ds(top - 2, rows), gd:]
    s8_ref[top:top + rows, :] = s4_ref[top:top + rows, gd:] + s4_ref[pl.ds(top - 4, rows), gd:]
    s16 = s8_ref[cur:cur + TM, gd:] + s8_ref[pl.ds(cur - 8, TM), gd:]
    sums = [s2_ref[cur:cur + TM, 0:gd], s4_ref[cur:cur + TM, 0:gd], s8_ref[cur:cur + TM, 0:gd], s16]
    pos = j * TM + lax.broadcasted_iota(jnp.int32, (TM, 1), 0)
    pooled = []
    for g, w in enumerate(POOL_WINDOWS):
        cnt = jnp.minimum(w, pos + 1).astype(F32)
        pooled.append(sums[g] / cnt - p[:, g * gd:(g + 1) * gd])
    y = _mix_b_tail(pooled, wgrp_ref, bgrp_ref, scale_ref, wout_ref)
    x1_ref[...] = x + gt_ref[...] * y

    @pl.when(j == TILES_PER_SEQ - 1)
    def _():
        st_ref[...] = full_ref[pl.ds(cur + TM - POOL_BUF, POOL_BUF), :]

    full_ref[top:cur, :] = full_ref[pl.ds(TM + top, POOL_CARRY), :]


def _mix_b_sample_kernel(*refs, pending):
    x, refs = _residual_in(refs, pending)
    (sh_ref, sc_ref, gt_ref, ng_ref, st_ref, win_ref, wgrp_ref, bgrp_ref, scale_ref, wout_ref,
     x1_ref, nst_ref) = refs
    h = (_rms(x, ng_ref[...]) * (1.0 + sc_ref[...]) + sh_ref[...]).astype(BF16)
    p = _dot(h, win_ref[...])
    pooled = []
    for g, w in enumerate(POOL_WINDOWS):
        s = p[:, g * B_GROUP_DIM:(g + 1) * B_GROUP_DIM]
        for k in range(1, w):
            s = s + st_ref[:, POOL_BUF - k, g * B_GROUP_DIM:(g + 1) * B_GROUP_DIM]
        cnt = float(min(w, PAST_LEN + 1))
        pooled.append(s / cnt - p[:, g * B_GROUP_DIM:(g + 1) * B_GROUP_DIM])
    y = _mix_b_tail(pooled, wgrp_ref, bgrp_ref, scale_ref, wout_ref)
    x1_ref[...] = x + gt_ref[...] * y
    for k in range(1, POOL_BUF):
        nst_ref[k - 1] = st_ref[:, k, :]
    nst_ref[POOL_BUF - 1] = p


def _mixer_b(x_all, pending, mod_p, mod_s, ng, layer, lb, state, w_in, w_grp, b_grp, scale,
             w_out):
    gd = B_GROUP_DIM
    common = [
        _layer_block((D, D), lb), _layer_block((len(POOL_WINDOWS), gd, gd), lb),
        _layer_block((len(POOL_WINDOWS), 1, gd), lb), _layer_block((1, D), lb),
        _layer_block((D, D), lb),
    ]
    has_pending = pending is not None
    pend_specs, pend_args = _pending_prompt(pending, mod_p, layer)
    x_all, st_p = pl.pallas_call(
        functools.partial(_mix_b_prompt_kernel, pending=has_pending),
        grid=(N_PROMPT_TILES,),
        in_specs=[pl.BlockSpec((TM, D), lambda i: (i, 0))] + pend_specs
        + _mod_specs_prompt(layer, (0, 1, 2)) + [_full((1, D))] + common,
        out_specs=[
            pl.BlockSpec((TM, D), lambda i: (i, 0)),
            pl.BlockSpec((None, POOL_BUF, D), lambda i: (i // TILES_PER_SEQ, 0, 0)),
        ],
        out_shape=[
            jax.ShapeDtypeStruct((T_PAD, D), F32),
            jax.ShapeDtypeStruct((BATCH, POOL_BUF, D), F32),
        ],
        scratch_shapes=[
            pltpu.VMEM((SUBLANES + POOL_CARRY + TM, D - g * gd), F32) for g in (0, 0, 1, 2)],
        input_output_aliases={0: 0},
        compiler_params=_cparams("arbitrary"),
        name="mixer_b_prompt",
    )(x_all, *pend_args, mod_p, mod_p, mod_p, ng, w_in, w_grp, b_grp, scale, w_out)

    r = SAMPLE_ROWS
    sblk = T_PROMPT // r
    pend_specs, pend_args = _pending_sample(pending, mod_s, layer, r)
    x_all, st_s = pl.pallas_call(
        functools.partial(_mix_b_sample_kernel, pending=has_pending),
        grid=(DEC_BATCH // r,),
        in_specs=[pl.BlockSpec((r, D), lambda i: (sblk + i, 0))] + pend_specs
        + _mod_specs_sample(layer, (0, 1, 2), r) + [_full((1, D))]
        + [pl.BlockSpec((None, r, POOL_BUF, D), lambda i: (lb, i, 0, 0))] + common,
        out_specs=[
            pl.BlockSpec((r, D), lambda i: (sblk + i, 0)),
            pl.BlockSpec((POOL_BUF, r, D), lambda i: (0, i, 0)),
        ],
        out_shape=[
            jax.ShapeDtypeStruct((T_PAD, D), F32),
            jax.ShapeDtypeStruct((POOL_BUF, DEC_BATCH, D), F32),
        ],
        input_output_aliases={0: 0},
        compiler_params=_cparams("arbitrary"),
        name="mixer_b_sample",
    )(x_all, *pend_args, mod_s, mod_s, mod_s, ng, state, w_in, w_grp, b_grp, scale, w_out)
    return x_all, st_p, jnp.transpose(st_s, (1, 0, 2))


def _mix_c_glu(x, sh, sc, ng, win_ref, bin_ref):
    h = (_rms(x, ng) * (1.0 + sc) + sh).astype(BF16)
    ag = _dot(h, win_ref[...]) + bin_ref[...]
    return ag[:, :D] * jax.nn.sigmoid(ag[:, D:])


def _mix_c_tail(conv, lng_ref, lnb_ref, wout_ref, bout_ref):
    z = _ln(conv, lng_ref[...], lnb_ref[...])
    z = z * jax.nn.sigmoid(z)
    return _dot(z.astype(BF16), wout_ref[...]) + bout_ref[...]


def _mix_c_prompt_kernel(*refs, pending):
    x, refs = _residual_in(refs, pending)
    (sh_ref, sc_ref, gt_ref, ng_ref, win_ref, bin_ref, wdw_ref, bdw_ref, lng_ref, lnb_ref,
     wout_ref, bout_ref, x1_ref, st_ref, full_ref, conv_ref, *shift_refs) = refs
    i = pl.program_id(0)
    j = i % TILES_PER_SEQ
    glu = _mix_c_glu(x, sh_ref[...], sc_ref[...], ng_ref[...], win_ref, bin_ref)

    @pl.when(j == 0)
    def _():
        full_ref[0:CONV_CARRY, :] = jnp.zeros((CONV_CARRY, D), F32)

    full_ref[CONV_CARRY:CONV_CARRY + TM, :] = glu
    off = CONV_CARRY - CONV_BUF
    span = TM + (off + CONV_BUF) // SUBLANES * SUBLANES - SUBLANES
    for c in range(D // LANES):
        cols = slice(c * LANES, (c + 1) * LANES)
        shift_ref = shift_refs[c % len(shift_refs)]
        for r in range(1, SUBLANES):
            shift_ref[r - 1] = full_ref[pl.ds(r, span), cols]
        for rb in range(TM // CONV_ROWS):
            acc = None
            for k in range(CONV_WIDTH):
                q, r = divmod(off + k, SUBLANES)
                start = q * SUBLANES + rb * CONV_ROWS
                if r == 0:
                    src = full_ref[pl.ds(start, CONV_ROWS), cols]
                else:
                    src = shift_ref[r - 1, pl.ds(start, CONV_ROWS), :]
                term = src * wdw_ref[k:k + 1, cols]
                acc = term if acc is None else acc + term
            conv_ref[pl.ds(rb * CONV_ROWS, CONV_ROWS), cols] = acc + bdw_ref[:, cols]
    y = _mix_c_tail(conv_ref[...], lng_ref, lnb_ref, wout_ref, bout_ref)
    x1_ref[...] = x + gt_ref[...] * y

    @pl.when(j == TILES_PER_SEQ - 1)
    def _():
        st_ref[...] = full_ref[pl.ds(CONV_CARRY + TM - CONV_BUF, CONV_BUF), :]

    full_ref[0:CONV_CARRY, :] = full_ref[pl.ds(TM, CONV_CARRY), :]


def _mix_c_sample_kernel(*refs, pending):
    x, refs = _residual_in(refs, pending)
    (sh_ref, sc_ref, gt_ref, ng_ref, st_ref, win_ref, bin_ref, wdw_ref, bdw_ref, lng_ref,
     lnb_ref, wout_ref, bout_ref, x1_ref, nst_ref) = refs
    glu = _mix_c_glu(x, sh_ref[...], sc_ref[...], ng_ref[...], win_ref, bin_ref)
    acc = glu * wdw_ref[CONV_BUF:CONV_BUF + 1, :]
    for k in range(CONV_BUF):
        acc = acc + st_ref[:, k, :] * wdw_ref[k:k + 1, :]
    y = _mix_c_tail(acc + bdw_ref[...], lng_ref, lnb_ref, wout_ref, bout_ref)
    x1_ref[...] = x + gt_ref[...] * y
    for k in range(1, CONV_BUF):
        nst_ref[k - 1] = st_ref[:, k, :]
    nst_ref[CONV_BUF - 1] = glu


def _mixer_c(x_all, pending, mod_p, mod_s, ng, layer, lc, state, w_in, b_in, w_dw, b_dw, ln_g,
             ln_b, w_out, b_out):
    common = [
        _layer_block((D, 2 * D), lc), _layer_block((1, 2 * D), lc),
        _layer_block((CONV_WIDTH, D), lc), _layer_block((1, D), lc), _layer_block((1, D), lc),
        _layer_block((1, D), lc), _layer_block((D, D), lc), _layer_block((1, D), lc),
    ]
    has_pending = pending is not None
    pend_specs, pend_args = _pending_prompt(pending, mod_p, layer)
    x_all, st_p = pl.pallas_call(
        functools.partial(_mix_c_prompt_kernel, pending=has_pending),
        grid=(N_PROMPT_TILES,),
        in_specs=[pl.BlockSpec((TM, D), lambda i: (i, 0))] + pend_specs
        + _mod_specs_prompt(layer, (0, 1, 2)) + [_full((1, D))] + common,
        out_specs=[
            pl.BlockSpec((TM, D), lambda i: (i, 0)),
            pl.BlockSpec((None, CONV_BUF, D), lambda i: (i // TILES_PER_SEQ, 0, 0)),
        ],
        out_shape=[
            jax.ShapeDtypeStruct((T_PAD, D), F32),
            jax.ShapeDtypeStruct((BATCH, CONV_BUF, D), F32),
        ],
        scratch_shapes=[pltpu.VMEM((TM + CONV_CARRY, D), F32), pltpu.VMEM((TM, D), F32)]
        + [pltpu.VMEM((SUBLANES - 1, TM + CONV_CARRY - SUBLANES, LANES), F32)] * 2,
        input_output_aliases={0: 0},
        compiler_params=_cparams("arbitrary"),
        name="mixer_c_prompt",
    )(x_all, *pend_args, mod_p, mod_p, mod_p, ng, w_in, b_in, w_dw, b_dw, ln_g, ln_b, w_out,
      b_out)

    r = SAMPLE_ROWS
    sblk = T_PROMPT // r
    pend_specs, pend_args = _pending_sample(pending, mod_s, layer, r)
    x_all, st_s = pl.pallas_call(
        functools.partial(_mix_c_sample_kernel, pending=has_pending),
        grid=(DEC_BATCH // r,),
        in_specs=[pl.BlockSpec((r, D), lambda i: (sblk + i, 0))] + pend_specs
        + _mod_specs_sample(layer, (0, 1, 2), r) + [_full((1, D))]
        + [pl.BlockSpec((None, r, CONV_BUF, D), lambda i: (lc, i, 0, 0))] + common,
        out_specs=[
            pl.BlockSpec((r, D), lambda i: (sblk + i, 0)),
            pl.BlockSpec((CONV_BUF, r, D), lambda i: (0, i, 0)),
        ],
        out_shape=[
            jax.ShapeDtypeStruct((T_PAD, D), F32),
            jax.ShapeDtypeStruct((CONV_BUF, DEC_BATCH, D), F32),
        ],
        input_output_aliases={0: 0},
        compiler_params=_cparams("arbitrary"),
        name="mixer_c_sample",
    )(x_all, *pend_args, mod_s, mod_s, mod_s, ng, state, w_in, b_in, w_dw, b_dw, ln_g, ln_b,
      w_out, b_out)
    return x_all, st_p, jnp.transpose(st_s, (1, 0, 2))


def _tile_mod(i, p_ref, s_ref):
    s_rows = jnp.concatenate([s_ref[...], jnp.zeros((TM - DEC_BATCH, D), F32)], axis=0)
    return jnp.where(i >= N_PROMPT_TILES, s_rows, p_ref[...])


def _unified_mod_specs(layer, chunks):
    specs = []
    for c in chunks:
        specs.append(pl.BlockSpec(
            (None, None, 1, D),
            lambda i, c=c: (layer, jnp.minimum(i // TILES_PER_SEQ, BATCH - 1), 0, c)))
        specs.append(pl.BlockSpec((None, DEC_BATCH, D), lambda i, c=c: (layer, 0, c)))
    return specs


def _router_kernel(x_ref, shp_ref, shs_ref, scp_ref, scs_ref, ng_ref, wr_ref, br_ref,
                   h2_ref, meta_ref, wts_ref, cnt_ref, carry_ref, before_ref):
    i = pl.program_id(0)

    @pl.when(i == 0)
    def _():
        carry_ref[...] = jnp.zeros((ROUTER_ROWS, 1), F32)
        earlier = lax.broadcasted_iota(jnp.int32, (TM, TM), 0)
        token = lax.broadcasted_iota(jnp.int32, (TM, TM), 1)
        before_ref[...] = (earlier < token).astype(BF16)

    sh = _tile_mod(i, shp_ref, shs_ref)
    sc = _tile_mod(i, scp_ref, scs_ref)
    h2 = _rms(x_ref[...], ng_ref[...]) * (1.0 + sc) + sh
    h2_ref[...] = _pack_bf16_pairs(h2)
    lg = lax.dot_general(wr_ref[...], h2.astype(BF16), (((1,), (1,)), ((), ())),
                         preferred_element_type=F32) + br_ref[...]
    row = lax.broadcasted_iota(jnp.int32, (ROUTER_ROWS, TM), 0)
    row_f = row.astype(F32)
    far = float(ROUTER_ROWS)
    neg = -jnp.inf
    is_g = row < N_GROUPS
    gm = jnp.where(is_g, lg, neg)
    gmax = jnp.max(gm, axis=0, keepdims=True)
    gsel = jnp.min(jnp.where(gm == gmax, row_f, far), axis=0, keepdims=True).astype(jnp.int32)
    gsum = jnp.sum(jnp.where(is_g, jnp.exp(lg - gmax), 0.0), axis=0, keepdims=True)
    g_w = 1.0 / gsum
    e_row = row - EXPERT_LANE0
    in_grp = (e_row >= 0) & (e_row < N_EXPERTS) & (
        lax.shift_right_arithmetic(e_row, EXPERTS_PER_GROUP.bit_length() - 1) == gsel)
    em = jnp.where(in_grp, lg, neg)
    m1 = jnp.max(em, axis=0, keepdims=True)
    i1 = jnp.min(jnp.where(em == m1, row_f, far), axis=0, keepdims=True).astype(jnp.int32)
    em2 = jnp.where(row == i1, neg, em)
    m2 = jnp.max(em2, axis=0, keepdims=True)
    i2 = jnp.min(jnp.where(em2 == m2, row_f, far), axis=0, keepdims=True).astype(jnp.int32)
    e2 = jnp.exp(m2 - m1)
    den = 1.0 + e2
    w1 = (1.0 / den) * g_w
    w2 = (e2 / den) * g_w

    hit1 = row == i1
    hit2 = row == i2
    assign = jnp.where(hit1 | hit2, 1.0, 0.0).astype(BF16)
    seen = _dot(assign, before_ref[...]) + carry_ref[...]
    rank1 = jnp.sum(jnp.where(hit1, seen, 0.0), axis=0, keepdims=True)
    rank2 = jnp.sum(jnp.where(hit2, seen, 0.0), axis=0, keepdims=True)
    carry_ref[...] = carry_ref[...] + jnp.sum(assign.astype(F32), axis=1, keepdims=True)
    cnt_ref[...] = carry_ref[...].astype(jnp.int32)

    ex1 = (i1 - EXPERT_LANE0).astype(F32)
    ex2 = (i2 - EXPERT_LANE0).astype(F32)
    field = lax.broadcasted_iota(jnp.int32, (META_ROWS, TM), 0)
    meta_ref[...] = jnp.where(field == 0, ex1, jnp.where(field == 1, ex2,
                              jnp.where(field == 2, rank1, jnp.where(field == 3, rank2, 0.0))))
    slot = lax.broadcasted_iota(jnp.int32, (LANES, TM), 0)
    wts_ref[...] = jnp.where(slot == 0, w1, jnp.where(slot == 1, w2, 0.0)).T


def _router(x_all, mod_p, mod_s, ng, layer, w_r, b_r):
    return pl.pallas_call(
        _router_kernel,
        grid=(N_TILES,),
        in_specs=[pl.BlockSpec((TM, D), lambda i: (i, 0))]
        + _unified_mod_specs(layer, (3, 4))
        + [_full((1, D)), _full((ROUTER_ROWS, D)), _full((ROUTER_ROWS, 1))],
        out_specs=[
            pl.BlockSpec((TM, D // 2), lambda i: (i, 0)),
            pl.BlockSpec((META_ROWS, TM), lambda i: (0, i)),
            pl.BlockSpec((TM, LANES), lambda i: (i, 0)),
            pl.BlockSpec((ROUTER_ROWS, 1), lambda i: (0, 0)),
        ],
        out_shape=[
            jax.ShapeDtypeStruct((T_PAD, D // 2), jnp.int32),
            jax.ShapeDtypeStruct((META_ROWS, T_PAD), F32),
            jax.ShapeDtypeStruct((T_PAD, LANES), F32),
            jax.ShapeDtypeStruct((ROUTER_ROWS, 1), jnp.int32),
        ],
        scratch_shapes=[pltpu.VMEM((ROUTER_ROWS, 1), F32), pltpu.VMEM((TM, TM), BF16)],
        compiler_params=_cparams("arbitrary"),
        name="router",
    )(x_all, mod_p, mod_s, mod_p, mod_s, ng, w_r, b_r)


SC_INDEX_MAX = 128
SC_ROWS_BYTES = 256 * 1024


def _gather_chunk(per_worker, row_bytes):
    top = min(SC_INDEX_MAX, SC_ROWS_BYTES // row_bytes) // SUBLANES * SUBLANES
    for ch in range(top, SUBLANES - 1, -SUBLANES):
        if per_worker % ch == 0:
            return ch
    raise ValueError(per_worker)


def _sc_gather(table, idx):
    nc, nw = SC_CORES, SC_CORES * SC_SUBCORES
    m = idx.shape[0]
    per_w = m // nw
    assert per_w * nw == m
    width = table.shape[1]
    ch = _gather_chunk(per_w, width * table.dtype.itemsize)
    mesh = plsc.VectorSubcoreMesh(core_axis_name="c", subcore_axis_name="s")

    @functools.partial(
        pl.kernel,
        out_type=jax.ShapeDtypeStruct((m, width), table.dtype),
        mesh=mesh,
        scratch_types=[
            pltpu.VMEM((ch,), jnp.int32),
            pltpu.VMEM((ch, width), table.dtype),
            pltpu.SemaphoreType.DMA,
        ],
    )
    def gather_kernel(t_hbm, i_hbm, o_hbm, idx_v, rows_v, sem):
        wid = lax.axis_index("s") * nc + lax.axis_index("c")
        base = wid * per_w

        @pl.loop(0, per_w // ch)
        def _(j):
            off = pl.multiple_of(base + j * ch, 8)
            pltpu.sync_copy(i_hbm.at[pl.ds(off, ch)], idx_v)
            pltpu.async_copy(t_hbm.at[idx_v], rows_v, sem).wait()
            pltpu.sync_copy(rows_v, o_hbm.at[pl.ds(off, ch)])

    return gather_kernel(table, idx)


def _sc_scatter2(rows, dest0, dest1, n_out):
    nc, nw = SC_CORES, SC_CORES * SC_SUBCORES
    n, width = rows.shape
    per_w = n // nw
    assert per_w * nw == n
    ch = _gather_chunk(per_w, width * rows.dtype.itemsize)
    mesh = plsc.VectorSubcoreMesh(core_axis_name="c", subcore_axis_name="s")

    @functools.partial(
        pl.kernel,
        out_type=jax.ShapeDtypeStruct((n_out, width), rows.dtype),
        mesh=mesh,
        scratch_types=[
            pltpu.VMEM((ch,), jnp.int32),
            pltpu.VMEM((ch,), jnp.int32),
            pltpu.VMEM((ch, width), rows.dtype),
            pltpu.SemaphoreType.DMA,
            pltpu.SemaphoreType.DMA,
        ],
    )
    def scatter_kernel(r_hbm, d0_hbm, d1_hbm, o_hbm, i0_v, i1_v, rows_v, sem0, sem1):
        wid = lax.axis_index("s") * nc + lax.axis_index("c")
        base = wid * per_w

        @pl.loop(0, per_w // ch)
        def _(j):
            off = pl.multiple_of(base + j * ch, 8)
            pltpu.sync_copy(d0_hbm.at[pl.ds(off, ch)], i0_v)
            pltpu.sync_copy(d1_hbm.at[pl.ds(off, ch)], i1_v)
            pltpu.sync_copy(r_hbm.at[pl.ds(off, ch)], rows_v)
            c0 = pltpu.async_copy(rows_v, o_hbm.at[i0_v], sem0)
            c1 = pltpu.async_copy(rows_v, o_hbm.at[i1_v], sem1)
            c0.wait()
            c1.wait()

    return scatter_kernel(rows, dest0, dest1)


def _expert_kernel(be_ref, first_ref, nvalid_ref, next_ref, nused_ref, x_ref, wg_hbm, wu_hbm,
                   wd_hbm, y_ref, wgf_ref, wuf_ref, wdf_ref, wgb_ref, wub_ref, wdb_ref, sems,
                   *, layer):
    b = pl.program_id(0)

    def weight_copies(e):
        return (
            pltpu.make_async_copy(wg_hbm.at[layer, e], wgf_ref, sems.at[0]),
            pltpu.make_async_copy(wu_hbm.at[layer, e], wuf_ref, sems.at[1]),
            pltpu.make_async_copy(wd_hbm.at[layer, e], wdf_ref, sems.at[2]),
        )

    @pl.when(b == 0)
    def _():
        for cp in weight_copies(be_ref[0]):
            cp.start()

    @pl.when(b < nused_ref[0])
    def _():
        @pl.when(first_ref[b] == 1)
        def _():
            for cp in weight_copies(be_ref[b]):
                cp.wait()
            wgb_ref[...] = wgf_ref[...].astype(BF16)
            wub_ref[...] = wuf_ref[...].astype(BF16)
            wdb_ref[...] = wdf_ref[...].astype(BF16)

            @pl.when(next_ref[b] >= 0)
            def _():
                for cp in weight_copies(next_ref[b]):
                    cp.start()

        def run(rows):
            live = lax.broadcasted_iota(jnp.int32, (rows, 1), 0) < nvalid_ref[b]
            x = _unpack_bf16_pairs(jnp.where(live, x_ref[0:rows, :], 0), BF16)
            g = _dot(x, wgb_ref[...])
            u = _dot(x, wub_ref[...])
            hmid = (g * jax.nn.sigmoid(g)) * u
            y_ref[0:rows, :] = _pack_bf16_pairs(_dot(hmid.astype(BF16), wdb_ref[...]))

        for rows in range(EXPERT_ROW_GROUP, BM + 1, EXPERT_ROW_GROUP):
            @pl.when((nvalid_ref[b] > rows - EXPERT_ROW_GROUP) & (nvalid_ref[b] <= rows))
            def _(rows=rows):
                run(rows)
                if rows < BM:
                    y_ref[rows:BM, :] = jnp.zeros((BM - rows, D // 2), jnp.int32)

    @pl.when(b >= nused_ref[0])
    def _():
        y_ref[...] = jnp.zeros((BM, D // 2), jnp.int32)


def _experts(x_sorted, block_expert, first, nvalid, next_expert, nused, layer, w_gate, w_up,
             w_down):
    def row_map(b, be, fi, nv, nx, nu):
        return (jnp.minimum(b, nu[0] - 1), 0)

    def out_map(b, be, fi, nv, nx, nu):
        return (b, 0)

    grid_spec = pltpu.PrefetchScalarGridSpec(
        num_scalar_prefetch=5,
        grid=(N_BLOCKS,),
        in_specs=[
            pl.BlockSpec((BM, D // 2), row_map),
            pl.BlockSpec(memory_space=pl.ANY),
            pl.BlockSpec(memory_space=pl.ANY),
            pl.BlockSpec(memory_space=pl.ANY),
        ],
        out_specs=pl.BlockSpec((BM, D // 2), out_map),
        scratch_shapes=[
            pltpu.VMEM((D, D_EXPERT), F32),
            pltpu.VMEM((D, D_EXPERT), F32),
            pltpu.VMEM((D_EXPERT, D), F32),
            pltpu.VMEM((D, D_EXPERT), BF16),
            pltpu.VMEM((D, D_EXPERT), BF16),
            pltpu.VMEM((D_EXPERT, D), BF16),
            pltpu.SemaphoreType.DMA((3,)),
        ],
    )
    return pl.pallas_call(
        functools.partial(_expert_kernel, layer=layer),
        grid_spec=grid_spec,
        out_shape=jax.ShapeDtypeStruct((P_ROWS, D // 2), jnp.int32),
        compiler_params=_cparams("arbitrary"),
        name="experts",
    )(block_expert, first, nvalid, next_expert, nused, x_sorted, w_gate, w_up, w_down)


def _combine_final_kernel(x_ref, g0_ref, g1_ref, wts_ref, gtp_ref, gts_ref, fg_ref, yp_ref,
                          ys_ref):
    i = pl.program_id(0)
    gate = _tile_mod(i, gtp_ref, gts_ref)
    y = _rms(_plus_moe(x_ref[...], g0_ref, g1_ref, wts_ref, gate), fg_ref[...])

    @pl.when(i < N_PROMPT_TILES)
    def _():
        yp_ref[...] = y

    @pl.when(i == N_PROMPT_TILES)
    def _():
        ys_ref[...] = y[0:DEC_BATCH, :]


def _combine_final(x_all, gathered, wts, mod_p, mod_s, layer, final_g):
    in_specs = [
        pl.BlockSpec((TM, D), lambda i: (i, 0)),
        pl.BlockSpec((TM, D // 2), lambda i: (i, 0)),
        pl.BlockSpec((TM, D // 2), lambda i: (i + N_TILES, 0)),
        pl.BlockSpec((TM, LANES), lambda i: (i, 0)),
    ] + _unified_mod_specs(layer, (5,))
    return pl.pallas_call(
        _combine_final_kernel,
        grid=(N_PROMPT_TILES + 1,),
        in_specs=in_specs + [_full((1, D))],
        out_specs=[
            pl.BlockSpec((TM, D), lambda i: (jnp.minimum(i, N_PROMPT_TILES - 1), 0)),
            pl.BlockSpec((DEC_BATCH, D), lambda i: (0, 0)),
        ],
        out_shape=[
            jax.ShapeDtypeStruct((T_PROMPT, D), F32),
            jax.ShapeDtypeStruct((DEC_BATCH, D), F32),
        ],
        compiler_params=_cparams("arbitrary"),
        name="combine_final",
    )(x_all, gathered, gathered, wts, mod_p, mod_s, final_g)


def _plan_kernel(cnt_ref, meta_ref, d0_ref, d1_ref, be_ref, first_ref, nvalid_ref, next_ref,
                 nused_ref, pstart_ref, nxt_ref):
    shift = BM.bit_length() - 1
    assert 1 << shift == BM

    def count(e):
        return cnt_ref[EXPERT_LANE0 + e, 0]

    def scan_back(k, nxt):
        e = N_EXPERTS - 1 - k
        nxt_ref[e] = nxt
        return jnp.where(count(e) > 0, e, nxt)

    lax.fori_loop(0, N_EXPERTS, scan_back, jnp.int32(-1))

    def fill(e, blk):
        c = count(e)
        pstart_ref[e] = blk << shift

        def one(j, carry):
            be_ref[blk + j] = e
            first_ref[blk + j] = jnp.where(j == 0, 1, 0)
            nvalid_ref[blk + j] = jnp.minimum(c - (j << shift), BM)
            next_ref[blk + j] = nxt_ref[e]
            return carry

        n_blk = (c + (BM - 1)) >> shift
        lax.fori_loop(0, n_blk, one, 0)
        return blk + n_blk

    used = lax.fori_loop(0, N_EXPERTS, fill, jnp.int32(0))
    nused_ref[0] = used

    def tail(b, carry):
        be_ref[b] = 0
        first_ref[b] = 0
        nvalid_ref[b] = 0
        next_ref[b] = -1
        return carry

    lax.fori_loop(used, N_BLOCKS, tail, 0)

    m = meta_ref[...]
    expert = m[0:2, :]
    dest = m[2:4, :]
    for e in range(N_EXPERTS):
        dest = dest + jnp.where(expert == float(e), pstart_ref[e].astype(F32), 0.0)
    dest = dest.astype(jnp.int32)
    d0_ref[...] = dest[0:1, :]
    d1_ref[...] = dest[1:2, :]


def _plan(cnt, meta):
    smem = pl.BlockSpec(memory_space=pltpu.SMEM)
    blocks = jax.ShapeDtypeStruct((N_BLOCKS,), jnp.int32)
    return pl.pallas_call(
        _plan_kernel,
        grid=(1,),
        in_specs=[smem, _full((META_ROWS, T_PAD))],
        out_specs=[_full((1, T_PAD)), _full((1, T_PAD)), smem, smem, smem, smem, smem],
        out_shape=[
            jax.ShapeDtypeStruct((1, T_PAD), jnp.int32),
            jax.ShapeDtypeStruct((1, T_PAD), jnp.int32),
            blocks, blocks, blocks, blocks,
            jax.ShapeDtypeStruct((1,), jnp.int32),
        ],
        scratch_shapes=[pltpu.SMEM((N_EXPERTS,), jnp.int32), pltpu.SMEM((N_EXPERTS,), jnp.int32)],
        compiler_params=_cparams("arbitrary"),
        name="plan",
    )(cnt, meta)


def _moe(x_all, mod_p, mod_s, ng, layer, w_r, b_r, w_gate, w_up, w_down):
    h2, meta, wts, cnt = _router(x_all, mod_p, mod_s, ng, layer, w_r, b_r)
    dest0, dest1, block_expert, first, nvalid, next_expert, nused = _plan(cnt, meta)
    dest0, dest1 = dest0.reshape(T_PAD), dest1.reshape(T_PAD)
    x_sorted = _sc_scatter2(h2, dest0, dest1, P_ROWS)
    y_sorted = _experts(x_sorted, block_expert, first, nvalid, next_expert, nused, layer,
                        w_gate, w_up, w_down)
    gathered = _sc_gather(y_sorted, jnp.concatenate([dest0, dest1]))
    return gathered, wts


def kernel(x_prompt, x_sample, c_prompt, c_sample, state_pool, state_conv, ada_w, ada_b, norm_g, final_g, a_w_in, a_ln_g, a_ln_b, a_w_s, a_b_s, a_w_out, b_w_in, b_w_grp, b_b_grp, b_scale, b_w_out, c_w_in, c_b_in, c_w_dw, c_b_dw, c_ln_g, c_ln_b, c_w_out, c_b_out, moe_w_grp, moe_b_grp, moe_w_exp, moe_b_exp, moe_w_gate, moe_w_up, moe_w_down):
    x_all = (x_prompt.reshape(T_PROMPT, D), x_sample.reshape(DEC_BATCH, D))
    mods = _ada_mods(jnp.concatenate([c_prompt, c_sample], axis=0), ada_w, ada_b)
    mod_p = mods[:, :BATCH].reshape(DEPTH, BATCH, 1, 6 * D)
    mod_s = mods[:, BATCH:]

    a_w_in_b, a_w_out_b = a_w_in.astype(BF16), a_w_out.astype(BF16)
    b_w_in_b, b_w_grp_b, b_w_out_b = b_w_in.astype(BF16), b_w_grp.astype(BF16), b_w_out.astype(BF16)
    c_w_in_b, c_w_out_b = c_w_in.astype(BF16), c_w_out.astype(BF16)
    pad_rows = ROUTER_ROWS - N_GROUPS - N_EXPERTS
    w_r = jnp.pad(jnp.swapaxes(jnp.concatenate([moe_w_grp, moe_w_exp], axis=2), 1, 2),
                  ((0, 0), (0, pad_rows), (0, 0))).astype(BF16)
    b_r = jnp.pad(jnp.concatenate([moe_b_grp, moe_b_exp], axis=1), ((0, 0), (0, pad_rows)))

    new_a_p, new_a_s, new_b_p, new_b_s, new_c_p, new_c_s = [], [], [], [], [], []
    ia = ib = ic = 0
    pending = None
    for layer in range(DEPTH):
        ng1 = norm_g[layer, 0].reshape(1, D)
        ng2 = norm_g[layer, 1].reshape(1, D)
        kind = layer % 3
        if kind == 0:
            x_all, st_p, st_s = _mixer_a(
                x_all, pending, mod_p, mod_s, ng1, layer, ia, a_w_in_b, a_ln_g.reshape(-1, 1, D),
                a_ln_b.reshape(-1, 1, D), a_w_s, a_b_s, a_w_out_b)
            new_a_p.append(st_p)
            new_a_s.append(st_s)
            ia += 1
        elif kind == 1:
            x_all, st_p, st_s = _mixer_b(
                x_all, pending, mod_p, mod_s, ng1, layer, ib, state_pool, b_w_in_b, b_w_grp_b,
                b_b_grp.reshape(-1, len(POOL_WINDOWS), 1, B_GROUP_DIM),
                b_scale.reshape(-1, 1, D), b_w_out_b)
            new_b_p.append(st_p)
            new_b_s.append(st_s)
            ib += 1
        else:
            x_all, st_p, st_s = _mixer_c(
                x_all, pending, mod_p, mod_s, ng1, layer, ic, state_conv, c_w_in_b,
                c_b_in.reshape(-1, 1, 2 * D), c_w_dw, c_b_dw.reshape(-1, 1, D),
                c_ln_g.reshape(-1, 1, D), c_ln_b.reshape(-1, 1, D), c_w_out_b,
                c_b_out.reshape(-1, 1, D))
            new_c_p.append(st_p)
            new_c_s.append(st_s)
            ic += 1
        pending = _moe(x_all, mod_p, mod_s, ng2, layer, w_r[layer],
                       b_r[layer].reshape(ROUTER_ROWS, 1), moe_w_gate, moe_w_up, moe_w_down)

    y_p, y_s = _combine_final(x_all, *pending, mod_p, mod_s, DEPTH - 1, final_g.reshape(1, D))
    return (y_p.reshape(BATCH, SEQ, D), y_s.reshape(DEC_BATCH, 1, D),
            jnp.stack(new_a_p), jnp.stack(new_a_s), jnp.stack(new_b_p), jnp.stack(new_b_s),
            jnp.stack(new_c_p), jnp.stack(new_c_s))
```

```python
import functools

import jax
import jax.numpy as jnp
from jax import lax
from jax.experimental import pallas as pl
from jax.experimental.pallas import tpu as pltpu
from jax.experimental.pallas import tpu_sc as plsc

F32 = jnp.float32
BF16 = jnp.bfloat16

D = 1024
BATCH = 8
SEQ = 2048
DEPTH = 4
DEC_BATCH = 128
PAST_LEN = 16384
CHUNK = 128
A_HEADS = 8
POOL_WINDOWS = (2, 4, 8, 16)
B_GROUP_DIM = D // len(POOL_WINDOWS)
POOL_BUF = max(POOL_WINDOWS) - 1
CONV_WIDTH = 31
CONV_BUF = CONV_WIDTH - 1
N_GROUPS = 4
EXPERTS_PER_GROUP = 8
N_EXPERTS = N_GROUPS * EXPERTS_PER_GROUP
D_EXPERT = D // 2
EPS = 1e-6

LANES = 128
SUBLANES = 8
CONV_ROWS = 128
SC_CORES = 2
SC_SUBCORES = 16
TM = 512
TILES_PER_SEQ = SEQ // TM
T_PROMPT = BATCH * SEQ
T_ALL = T_PROMPT + DEC_BATCH
N_TILES = -(-T_ALL // TM)
T_PAD = N_TILES * TM
N_PROMPT_TILES = T_PROMPT // TM
SAMPLE_ROWS = 32
BM = 512
N_BLOCKS = -(-(2 * T_PAD) // BM) + N_EXPERTS
P_ROWS = N_BLOCKS * BM
META_ROWS = 8
POOL_CARRY = 16
CONV_CARRY = 32
EXPERT_LANE0 = N_GROUPS
ROUTER_ROWS = 48
VMEM_LIMIT = 56 * 1024 * 1024


def _cparams(*sem):
    return pltpu.CompilerParams(dimension_semantics=sem, vmem_limit_bytes=VMEM_LIMIT)


def _rms(x, g):
    return x * lax.rsqrt(jnp.mean(x * x, axis=-1, keepdims=True) + EPS) * g


def _ln(x, g, b):
    mu = jnp.mean(x, axis=-1, keepdims=True)
    xc = x - mu
    var = jnp.mean(xc * xc, axis=-1, keepdims=True)
    return xc * lax.rsqrt(var + EPS) * g + b


def _dot(a, b):
    return jnp.dot(a, b, preferred_element_type=F32)


def _pack_bf16_pairs(v):
    w = v.shape[1] // 2
    lo = lax.bitcast_convert_type(v[:, :w].astype(BF16).astype(F32), jnp.uint32)
    hi = lax.bitcast_convert_type(v[:, w:].astype(BF16).astype(F32), jnp.uint32)
    return lax.bitcast_convert_type(lax.shift_right_logical(lo, jnp.uint32(16)) | hi, jnp.int32)


def _unpack_bf16_pairs(words, dtype):
    u = lax.bitcast_convert_type(words, jnp.uint32)
    lo = lax.bitcast_convert_type(lax.shift_left(u, jnp.uint32(16)), F32)
    hi = lax.bitcast_convert_type(u & jnp.uint32(0xFFFF0000), F32)
    return jnp.concatenate([lo.astype(dtype), hi.astype(dtype)], axis=1)


def _ada_kernel(c_ref, w_ref, b_ref, o_ref):
    c = c_ref[...]
    s = (c * jax.nn.sigmoid(c)).astype(BF16)
    o_ref[...] = _dot(s, w_ref[...].astype(BF16)) + b_ref[...]


def _ada_mods(c_all, ada_w, ada_b):
    n = c_all.shape[0]
    tn = 2048
    return pl.pallas_call(
        _ada_kernel,
        grid=(DEPTH, 6 * D // tn),
        in_specs=[
            pl.BlockSpec((n, D), lambda l, j: (0, 0)),
            pl.BlockSpec((None, D, tn), lambda l, j: (l, 0, j)),
            pl.BlockSpec((None, 1, tn), lambda l, j: (l, 0, j)),
        ],
        out_specs=pl.BlockSpec((None, n, tn), lambda l, j: (l, 0, j)),
        out_shape=jax.ShapeDtypeStruct((DEPTH, n, 6 * D), F32),
        compiler_params=_cparams("parallel", "parallel"),
        name="ada_mods",
    )(c_all, ada_w, ada_b.reshape(DEPTH, 1, 6 * D))


def _mod_specs_prompt(layer, chunks):
    return [
        pl.BlockSpec((None, None, 1, D),
                     lambda i, c=c: (layer, jnp.minimum(i // TILES_PER_SEQ, BATCH - 1), 0, c))
        for c in chunks
    ]


def _mod_specs_sample(layer, chunks, rows, row_block0=0):
    return [
        pl.BlockSpec((None, rows, D), lambda i, c=c: (layer, i + row_block0, c)) for c in chunks
    ]


def _full(shape):
    nd = len(shape)
    return pl.BlockSpec(shape, lambda i: (0,) * nd)


def _layer_block(shape, layer):
    nd = len(shape)
    return pl.BlockSpec((None,) + shape, lambda i: (layer,) + (0,) * nd)


def _plus_moe(x, g0_ref, g1_ref, wts_ref, gate):
    w = wts_ref[...]
    y0 = _unpack_bf16_pairs(g0_ref[...], F32)
    y1 = _unpack_bf16_pairs(g1_ref[...], F32)
    return x + gate * (w[:, 0:1] * y0 + w[:, 1:2] * y1)


def _residual_in(refs, pending):
    if not pending:
        return refs[0][...], refs[1:]
    x_ref, g0_ref, g1_ref, wts_ref, gate_ref = refs[:5]
    return _plus_moe(x_ref[...], g0_ref, g1_ref, wts_ref, gate_ref[...]), refs[5:]


def _pending_prompt(pending, mod_p, layer):
    if pending is None:
        return [], []
    gathered, wts = pending
    specs = [
        pl.BlockSpec((TM, D // 2), lambda i: (i, 0)),
        pl.BlockSpec((TM, D // 2), lambda i: (i + N_TILES, 0)),
        pl.BlockSpec((TM, LANES), lambda i: (i, 0)),
    ] + _mod_specs_prompt(layer - 1, (5,))
    return specs, [gathered, gathered, wts, mod_p]


def _pending_sample(pending, mod_s, layer, rows):
    if pending is None:
        return [], []
    gathered, wts = pending
    first = T_PROMPT // rows
    second = (T_PAD + T_PROMPT) // rows
    specs = [
        pl.BlockSpec((rows, D // 2), lambda i: (first + i, 0)),
        pl.BlockSpec((rows, D // 2), lambda i: (second + i, 0)),
        pl.BlockSpec((rows, LANES), lambda i: (first + i, 0)),
    ] + _mod_specs_sample(layer - 1, (5,), rows)
    return specs, [gathered, gathered, wts, mod_s]


def _mix_a_front(x, sh, sc, ng, win_ref, lng_ref, lnb_ref):
    h = (_rms(x, ng) * (1.0 + sc) + sh).astype(BF16)
    z = jax.nn.gelu(_dot(h, win_ref[...]))
    u = z[:, :D]
    v = _ln(z[:, D:], lng_ref[...], lnb_ref[...])
    return u, v


def _mix_a_prompt_kernel(*refs, pending, first):
    x, refs = _residual_in(refs, pending)
    (sh_ref, sc_ref, gt_ref, ng_ref, win_ref, lng_ref, lnb_ref, ws_ref, bs_ref, wout_ref,
     x1_ref, cv_ref, mixed_ref) = refs
    i = pl.program_id(0)
    u, v = _mix_a_front(x, sh_ref[...], sc_ref[...], ng_ref[...], win_ref, lng_ref, lnb_ref)
    vb = v.astype(BF16)
    n_chunks = TM // CHUNK
    row = lax.broadcasted_iota(jnp.int32, (CHUNK, CHUNK), 0)
    col = lax.broadcasted_iota(jnp.int32, (CHUNK, CHUNK), 1)
    tril = row >= col
    hd_dim = D // A_HEADS
    for hd in range(A_HEADS):
        ws = jnp.where(tril, ws_ref[hd], 0.0).astype(BF16)
        cols = slice(hd * hd_dim, (hd + 1) * hd_dim)
        vcat = jnp.concatenate([vb[c * CHUNK:(c + 1) * CHUNK, cols] for c in range(n_chunks)], axis=1)
        m = _dot(ws, vcat)
        for c in range(n_chunks):
            mixed_ref[c * CHUNK:(c + 1) * CHUNK, cols] = m[:, c * hd_dim:(c + 1) * hd_dim] + bs_ref[hd]
    y = _dot((u * mixed_ref[...]).astype(BF16), wout_ref[...])
    out = x + gt_ref[...] * y
    if first:
        out = jnp.where(i < N_PROMPT_TILES, out, 0.0)
    x1_ref[...] = out

    @pl.when(i % TILES_PER_SEQ == TILES_PER_SEQ - 1)
    def _():
        cv_ref[...] = v[TM - CHUNK:, :]


def _mix_a_sample_kernel(*refs, pending, first):
    if first:
        refs = refs[:1] + refs[2:]
    x, refs = _residual_in(refs, pending)
    (sh_ref, sc_ref, gt_ref, ng_ref, win_ref, lng_ref, lnb_ref, wsd_ref, bsd_ref, wout_ref,
     x1_ref, cv_ref) = refs
    u, v = _mix_a_front(x, sh_ref[...], sc_ref[...], ng_ref[...], win_ref, lng_ref, lnb_ref)
    mixed = wsd_ref[...] * v + bsd_ref[...]
    y = _dot((u * mixed).astype(BF16), wout_ref[...])
    x1_ref[...] = x + gt_ref[...] * y
    cv_ref[...] = v


def _mixer_a(x_in, pending, mod_p, mod_s, ng, layer, la, w_in, ln_g, ln_b, w_s, b_s, w_out):
    first = isinstance(x_in, tuple)
    x_all = x_in[0] if first else x_in
    bs_b = jnp.broadcast_to(b_s[la][:, :, None], (A_HEADS, CHUNK, D // A_HEADS))
    common = [
        _layer_block((D, 2 * D), la), _layer_block((1, D), la), _layer_block((1, D), la),
    ]
    has_pending = pending is not None
    pend_specs, pend_args = _pending_prompt(pending, mod_p, layer)
    last_tile = N_PROMPT_TILES - 1
    x_all, cv_p = pl.pallas_call(
        functools.partial(_mix_a_prompt_kernel, pending=has_pending, first=first),
        grid=(N_TILES if first else N_PROMPT_TILES,),
        in_specs=[pl.BlockSpec((TM, D), lambda i: (jnp.minimum(i, last_tile), 0))] + pend_specs
        + _mod_specs_prompt(layer, (0, 1, 2))
        + [_full((1, D))] + common
        + [_layer_block((A_HEADS, CHUNK, CHUNK), la), _full((A_HEADS, CHUNK, D // A_HEADS)),
           _layer_block((D, D), la)],
        out_specs=[
            pl.BlockSpec((TM, D), lambda i: (i, 0)),
            pl.BlockSpec((None, CHUNK, D), lambda i: (jnp.minimum(i, last_tile) // TILES_PER_SEQ, 0, 0)),
        ],
        out_shape=[
            jax.ShapeDtypeStruct((T_PAD, D), F32),
            jax.ShapeDtypeStruct((BATCH, CHUNK, D), F32),
        ],
        scratch_shapes=[pltpu.VMEM((TM, D), F32)],
        input_output_aliases={} if first else {0: 0},
        compiler_params=_cparams("arbitrary"),
        name="mixer_a_prompt",
    )(x_all, *pend_args, mod_p, mod_p, mod_p, ng, w_in, ln_g, ln_b, w_s, bs_b, w_out)

    wsd = jnp.repeat(w_s[la, :, 0, 0], D // A_HEADS).reshape(1, D)
    bsd = jnp.repeat(b_s[la, :, 0], D // A_HEADS).reshape(1, D)
    sblk = T_PROMPT // DEC_BATCH
    pend_specs, pend_args = _pending_sample(pending, mod_s, layer, DEC_BATCH)
    if first:
        x_specs = [pl.BlockSpec((DEC_BATCH, D), lambda i: (0, 0)), pl.BlockSpec(memory_space=pl.ANY)]
        x_args = [x_in[1], x_all]
    else:
        x_specs = [pl.BlockSpec((DEC_BATCH, D), lambda i: (sblk, 0))]
        x_args = [x_all]
    x_all, cv_s = pl.pallas_call(
        functools.partial(_mix_a_sample_kernel, pending=has_pending, first=first),
        grid=(1,),
        in_specs=x_specs + pend_specs
        + _mod_specs_sample(layer, (0, 1, 2), DEC_BATCH)
        + [_full((1, D))] + common
        + [_full((1, D)), _full((1, D)), _layer_block((D, D), la)],
        out_specs=[
            pl.BlockSpec((DEC_BATCH, D), lambda i: (sblk, 0)),
            pl.BlockSpec((DEC_BATCH, D), lambda i: (0, 0)),
        ],
        out_shape=[
            jax.ShapeDtypeStruct((T_PAD, D), F32),
            jax.ShapeDtypeStruct((DEC_BATCH, D), F32),
        ],
        input_output_aliases={len(x_args) - 1: 0},
        compiler_params=_cparams("arbitrary"),
        name="mixer_a_sample",
    )(*x_args, *pend_args, mod_s, mod_s, mod_s, ng, w_in, ln_g, ln_b, wsd, bsd, w_out)
    return x_all, cv_p, cv_s.reshape(DEC_BATCH, 1, D)


def _mix_b_tail(pooled_groups, wgrp_ref, bgrp_ref, scale_ref, wout_ref):
    outs = [
        _dot(pg.astype(BF16), wgrp_ref[g]) + bgrp_ref[g]
        for g, pg in enumerate(pooled_groups)
    ]
    mixed = jnp.concatenate(outs, axis=1) * scale_ref[...]
    return _dot(mixed.astype(BF16), wout_ref[...])


def _mix_b_prompt_kernel(*refs, pending):
    x, refs = _residual_in(refs, pending)
    (sh_ref, sc_ref, gt_ref, ng_ref, win_ref, wgrp_ref, bgrp_ref, scale_ref, wout_ref,
     x1_ref, st_ref, full_ref, s2_ref, s4_ref, s8_ref) = refs
    assert POOL_WINDOWS == (2, 4, 8, 16)
    i = pl.program_id(0)
    j = i % TILES_PER_SEQ
    h = (_rms(x, ng_ref[...]) * (1.0 + sc_ref[...]) + sh_ref[...]).astype(BF16)
    p = _dot(h, win_ref[...])
    gd = B_GROUP_DIM
    top = SUBLANES
    cur = top + POOL_CARRY
    rows = TM + POOL_CARRY

    @pl.when(i == 0)
    def _():
        for ref in (full_ref, s2_ref, s4_ref, s8_ref):
            ref[0:top, :] = jnp.zeros((top, ref.shape[1]), F32)

    full_ref[top:cur, :] = jnp.where(j == 0, 0.0, full_ref[top:cur, :])
    full_ref[cur:cur + TM, :] = p
    s2_ref[top:top + rows, :] = full_ref[top:top + rows, :] + full_ref[pl.ds(top - 1, rows), :]
    s4_ref[top:top + rows, :] = s2_ref[top:top + rows, gd:] + s2_ref[pl.ds(top - 2, rows), gd:]
    s8_ref[top:top + rows, :] = s4_ref[top:top + rows, gd:] + s4_ref[pl.ds(top - 4, rows), gd:]
    s16 = s8_ref[cur:cur + TM, gd:] + s8_ref[pl.ds(cur - 8, TM), gd:]
    sums = [s2_ref[cur:cur + TM, 0:gd], s4_ref[cur:cur + TM, 0:gd], s8_ref[cur:cur + TM, 0:gd], s16]
    pos = j * TM + lax.broadcasted_iota(jnp.int32, (TM, 1), 0)
    pooled = []
    for g, w in enumerate(POOL_WINDOWS):
        cnt = jnp.minimum(w, pos + 1).astype(F32)
        pooled.append(sums[g] / cnt - p[:, g * gd:(g + 1) * gd])
    y = _mix_b_tail(pooled, wgrp_ref, bgrp_ref, scale_ref, wout_ref)
    x1_ref[...] = x + gt_ref[...] * y

    @pl.when(j == TILES_PER_SEQ - 1)
    def _():
        st_ref[...] = full_ref[pl.ds(cur + TM - POOL_BUF, POOL_BUF), :]

    full_ref[top:cur, :] = full_ref[pl.ds(TM + top, POOL_CARRY), :]


def _mix_b_sample_kernel(*refs, pending):
    x, refs = _residual_in(refs, pending)
    (sh_ref, sc_ref, gt_ref, ng_ref, st_ref, win_ref, wgrp_ref, bgrp_ref, scale_ref, wout_ref,
     x1_ref, nst_ref) = refs
    h = (_rms(x, ng_ref[...]) * (1.0 + sc_ref[...]) + sh_ref[...]).astype(BF16)
    p = _dot(h, win_ref[...])
    pooled = []
    for g, w in enumerate(POOL_WINDOWS):
        s = p[:, g * B_GROUP_DIM:(g + 1) * B_GROUP_DIM]
        for k in range(1, w):
            s = s + st_ref[:, POOL_BUF - k, g * B_GROUP_DIM:(g + 1) * B_GROUP_DIM]
        cnt = float(min(w, PAST_LEN + 1))
        pooled.append(s / cnt - p[:, g * B_GROUP_DIM:(g + 1) * B_GROUP_DIM])
    y = _mix_b_tail(pooled, wgrp_ref, bgrp_ref, scale_ref, wout_ref)
    x1_ref[...] = x + gt_ref[...] * y
    for k in range(1, POOL_BUF):
        nst_ref[k - 1] = st_ref[:, k, :]
    nst_ref[POOL_BUF - 1] = p


def _mixer_b(x_all, pending, mod_p, mod_s, ng, layer, lb, state, w_in, w_grp, b_grp, scale,
             w_out):
    gd = B_GROUP_DIM
    common = [
        _layer_block((D, D), lb), _layer_block((len(POOL_WINDOWS), gd, gd), lb),
        _layer_block((len(POOL_WINDOWS), 1, gd), lb), _layer_block((1, D), lb),
        _layer_block((D, D), lb),
    ]
    has_pending = pending is not None
    pend_specs, pend_args = _pending_prompt(pending, mod_p, layer)
    x_all, st_p = pl.pallas_call(
        functools.partial(_mix_b_prompt_kernel, pending=has_pending),
        grid=(N_PROMPT_TILES,),
        in_specs=[pl.BlockSpec((TM, D), lambda i: (i, 0))] + pend_specs
        + _mod_specs_prompt(layer, (0, 1, 2)) + [_full((1, D))] + common,
        out_specs=[
            pl.BlockSpec((TM, D), lambda i: (i, 0)),
            pl.BlockSpec((None, POOL_BUF, D), lambda i: (i // TILES_PER_SEQ, 0, 0)),
        ],
        out_shape=[
            jax.ShapeDtypeStruct((T_PAD, D), F32),
            jax.ShapeDtypeStruct((BATCH, POOL_BUF, D), F32),
        ],
        scratch_shapes=[
            pltpu.VMEM((SUBLANES + POOL_CARRY + TM, D - g * gd), F32) for g in (0, 0, 1, 2)],
        input_output_aliases={0: 0},
        compiler_params=_cparams("arbitrary"),
        name="mixer_b_prompt",
    )(x_all, *pend_args, mod_p, mod_p, mod_p, ng, w_in, w_grp, b_grp, scale, w_out)

    r = SAMPLE_ROWS
    sblk = T_PROMPT // r
    pend_specs, pend_args = _pending_sample(pending, mod_s, layer, r)
    x_all, st_s = pl.pallas_call(
        functools.partial(_mix_b_sample_kernel, pending=has_pending),
        grid=(DEC_BATCH // r,),
        in_specs=[pl.BlockSpec((r, D), lambda i: (sblk + i, 0))] + pend_specs
        + _mod_specs_sample(layer, (0, 1, 2), r) + [_full((1, D))]
        + [pl.BlockSpec((None, r, POOL_BUF, D), lambda i: (lb, i, 0, 0))] + common,
        out_specs=[
            pl.BlockSpec((r, D), lambda i: (sblk + i, 0)),
            pl.BlockSpec((POOL_BUF, r, D), lambda i: (0, i, 0)),
        ],
        out_shape=[
            jax.ShapeDtypeStruct((T_PAD, D), F32),
            jax.ShapeDtypeStruct((POOL_BUF, DEC_BATCH, D), F32),
        ],
        input_output_aliases={0: 0},
        compiler_params=_cparams("arbitrary"),
        name="mixer_b_sample",
    )(x_all, *pend_args, mod_s, mod_s, mod_s, ng, state, w_in, w_grp, b_grp, scale, w_out)
    return x_all, st_p, jnp.transpose(st_s, (1, 0, 2))


def _mix_c_glu(x, sh, sc, ng, win_ref, bin_ref):
    h = (_rms(x, ng) * (1.0 + sc) + sh).astype(BF16)
    ag = _dot(h, win_ref[...]) + bin_ref[...]
    return ag[:, :D] * jax.nn.sigmoid(ag[:, D:])


def _mix_c_tail(conv, lng_ref, lnb_ref, wout_ref, bout_ref):
    z = _ln(conv, lng_ref[...], lnb_ref[...])
    z = z * jax.nn.sigmoid(z)
    return _dot(z.astype(BF16), wout_ref[...]) + bout_ref[...]


def _mix_c_prompt_kernel(*refs, pending):
    x, refs = _residual_in(refs, pending)
    (sh_ref, sc_ref, gt_ref, ng_ref, win_ref, bin_ref, wdw_ref, bdw_ref, lng_ref, lnb_ref,
     wout_ref, bout_ref, x1_ref, st_ref, full_ref, conv_ref, *shift_refs) = refs
    i = pl.program_id(0)
    j = i % TILES_PER_SEQ
    glu = _mix_c_glu(x, sh_ref[...], sc_ref[...], ng_ref[...], win_ref, bin_ref)

    @pl.when(j == 0)
    def _():
        full_ref[0:CONV_CARRY, :] = jnp.zeros((CONV_CARRY, D), F32)

    full_ref[CONV_CARRY:CONV_CARRY + TM, :] = glu
    off = CONV_CARRY - CONV_BUF
    span = TM + (off + CONV_BUF) // SUBLANES * SUBLANES - SUBLANES
    for c in range(D // LANES):
        cols = slice(c * LANES, (c + 1) * LANES)
        shift_ref = shift_refs[c % len(shift_refs)]
        for r in range(1, SUBLANES):
            shift_ref[r - 1] = full_ref[pl.ds(r, span), cols]
        for rb in range(TM // CONV_ROWS):
            acc = None
            for k in range(CONV_WIDTH):
                q, r = divmod(off + k, SUBLANES)
                start = q * SUBLANES + rb * CONV_ROWS
                if r == 0:
                    src = full_ref[pl.ds(start, CONV_ROWS), cols]
                else:
                    src = shift_ref[r - 1, pl.ds(start, CONV_ROWS), :]
                term = src * wdw_ref[k:k + 1, cols]
                acc = term if acc is None else acc + term
            conv_ref[pl.ds(rb * CONV_ROWS, CONV_ROWS), cols] = acc + bdw_ref[:, cols]
    y = _mix_c_tail(conv_ref[...], lng_ref, lnb_ref, wout_ref, bout_ref)
    x1_ref[...] = x + gt_ref[...] * y

    @pl.when(j == TILES_PER_SEQ - 1)
    def _():
        st_ref[...] = full_ref[pl.ds(CONV_CARRY + TM - CONV_BUF, CONV_BUF), :]

    full_ref[0:CONV_CARRY, :] = full_ref[pl.ds(TM, CONV_CARRY), :]


def _mix_c_sample_kernel(*refs, pending):
    x, refs = _residual_in(refs, pending)
    (sh_ref, sc_ref, gt_ref, ng_ref, st_ref, win_ref, bin_ref, wdw_ref, bdw_ref, lng_ref,
     lnb_ref, wout_ref, bout_ref, x1_ref, nst_ref) = refs
    glu = _mix_c_glu(x, sh_ref[...], sc_ref[...], ng_ref[...], win_ref, bin_ref)
    acc = glu * wdw_ref[CONV_BUF:CONV_BUF + 1, :]
    for k in range(CONV_BUF):
        acc = acc + st_ref[:, k, :] * wdw_ref[k:k + 1, :]
    y = _mix_c_tail(acc + bdw_ref[...], lng_ref, lnb_ref, wout_ref, bout_ref)
    x1_ref[...] = x + gt_ref[...] * y
    for k in range(1, CONV_BUF):
        nst_ref[k - 1] = st_ref[:, k, :]
    nst_ref[CONV_BUF - 1] = glu


def _mixer_c(x_all, pending, mod_p, mod_s, ng, layer, lc, state, w_in, b_in, w_dw, b_dw, ln_g,
             ln_b, w_out, b_out):
    common = [
        _layer_block((D, 2 * D), lc), _layer_block((1, 2 * D), lc),
        _layer_block((CONV_WIDTH, D), lc), _layer_block((1, D), lc), _layer_block((1, D), lc),
        _layer_block((1, D), lc), _layer_block((D, D), lc), _layer_block((1, D), lc),
    ]
    has_pending = pending is not None
    pend_specs, pend_args = _pending_prompt(pending, mod_p, layer)
    x_all, st_p = pl.pallas_call(
        functools.partial(_mix_c_prompt_kernel, pending=has_pending),
        grid=(N_PROMPT_TILES,),
        in_specs=[pl.BlockSpec((TM, D), lambda i: (i, 0))] + pend_specs
        + _mod_specs_prompt(layer, (0, 1, 2)) + [_full((1, D))] + common,
        out_specs=[
            pl.BlockSpec((TM, D), lambda i: (i, 0)),
            pl.BlockSpec((None, CONV_BUF, D), lambda i: (i // TILES_PER_SEQ, 0, 0)),
        ],
        out_shape=[
            jax.ShapeDtypeStruct((T_PAD, D), F32),
            jax.ShapeDtypeStruct((BATCH, CONV_BUF, D), F32),
        ],
        scratch_shapes=[pltpu.VMEM((TM + CONV_CARRY, D), F32), pltpu.VMEM((TM, D), F32)]
        + [pltpu.VMEM((SUBLANES - 1, TM + CONV_CARRY - SUBLANES, LANES), F32)] * 2,
        input_output_aliases={0: 0},
        compiler_params=_cparams("arbitrary"),
        name="mixer_c_prompt",
    )(x_all, *pend_args, mod_p, mod_p, mod_p, ng, w_in, b_in, w_dw, b_dw, ln_g, ln_b, w_out,
      b_out)

    r = SAMPLE_ROWS
    sblk = T_PROMPT // r
    pend_specs, pend_args = _pending_sample(pending, mod_s, layer, r)
    x_all, st_s = pl.pallas_call(
        functools.partial(_mix_c_sample_kernel, pending=has_pending),
        grid=(DEC_BATCH // r,),
        in_specs=[pl.BlockSpec((r, D), lambda i: (sblk + i, 0))] + pend_specs
        + _mod_specs_sample(layer, (0, 1, 2), r) + [_full((1, D))]
        + [pl.BlockSpec((None, r, CONV_BUF, D), lambda i: (lc, i, 0, 0))] + common,
        out_specs=[
            pl.BlockSpec((r, D), lambda i: (sblk + i, 0)),
            pl.BlockSpec((CONV_BUF, r, D), lambda i: (0, i, 0)),
        ],
        out_shape=[
            jax.ShapeDtypeStruct((T_PAD, D), F32),
            jax.ShapeDtypeStruct((CONV_BUF, DEC_BATCH, D), F32),
        ],
        input_output_aliases={0: 0},
        compiler_params=_cparams("arbitrary"),
        name="mixer_c_sample",
    )(x_all, *pend_args, mod_s, mod_s, mod_s, ng, state, w_in, b_in, w_dw, b_dw, ln_g, ln_b,
      w_out, b_out)
    return x_all, st_p, jnp.transpose(st_s, (1, 0, 2))


def _tile_mod(i, p_ref, s_ref):
    s_rows = jnp.concatenate([s_ref[...], jnp.zeros((TM - DEC_BATCH, D), F32)], axis=0)
    return jnp.where(i >= N_PROMPT_TILES, s_rows, p_ref[...])


def _unified_mod_specs(layer, chunks):
    specs = []
    for c in chunks:
        specs.append(pl.BlockSpec(
            (None, None, 1, D),
            lambda i, c=c: (layer, jnp.minimum(i // TILES_PER_SEQ, BATCH - 1), 0, c)))
        specs.append(pl.BlockSpec((None, DEC_BATCH, D), lambda i, c=c: (layer, 0, c)))
    return specs


def _router_kernel(x_ref, shp_ref, shs_ref, scp_ref, scs_ref, ng_ref, wr_ref, br_ref,
                   h2_ref, meta_ref, wts_ref, cnt_ref, carry_ref, before_ref):
    i = pl.program_id(0)

    @pl.when(i == 0)
    def _():
        carry_ref[...] = jnp.zeros((ROUTER_ROWS, 1), F32)
        earlier = lax.broadcasted_iota(jnp.int32, (TM, TM), 0)
        token = lax.broadcasted_iota(jnp.int32, (TM, TM), 1)
        before_ref[...] = (earlier < token).astype(BF16)

    sh = _tile_mod(i, shp_ref, shs_ref)
    sc = _tile_mod(i, scp_ref, scs_ref)
    h2 = _rms(x_ref[...], ng_ref[...]) * (1.0 + sc) + sh
    h2_ref[...] = _pack_bf16_pairs(h2)
    lg = lax.dot_general(wr_ref[...], h2.astype(BF16), (((1,), (1,)), ((), ())),
                         preferred_element_type=F32) + br_ref[...]
    row = lax.broadcasted_iota(jnp.int32, (ROUTER_ROWS, TM), 0)
    row_f = row.astype(F32)
    far = float(ROUTER_ROWS)
    neg = -jnp.inf
    is_g = row < N_GROUPS
    gm = jnp.where(is_g, lg, neg)
    gmax = jnp.max(gm, axis=0, keepdims=True)
    gsel = jnp.min(jnp.where(gm == gmax, row_f, far), axis=0, keepdims=True).astype(jnp.int32)
    gsum = jnp.sum(jnp.where(is_g, jnp.exp(lg - gmax), 0.0), axis=0, keepdims=True)
    g_w = 1.0 / gsum
    e_row = row - EXPERT_LANE0
    in_grp = (e_row >= 0) & (e_row < N_EXPERTS) & (
        lax.shift_right_arithmetic(e_row, EXPERTS_PER_GROUP.bit_length() - 1) == gsel)
    em = jnp.where(in_grp, lg, neg)
    m1 = jnp.max(em, axis=0, keepdims=True)
    i1 = jnp.min(jnp.where(em == m1, row_f, far), axis=0, keepdims=True).astype(jnp.int32)
    em2 = jnp.where(row == i1, neg, em)
    m2 = jnp.max(em2, axis=0, keepdims=True)
    i2 = jnp.min(jnp.where(em2 == m2, row_f, far), axis=0, keepdims=True).astype(jnp.int32)
    e2 = jnp.exp(m2 - m1)
    den = 1.0 + e2
    w1 = (1.0 / den) * g_w
    w2 = (e2 / den) * g_w

    hit1 = row == i1
    hit2 = row == i2
    assign = jnp.where(hit1 | hit2, 1.0, 0.0).astype(BF16)
    seen = _dot(assign, before_ref[...]) + carry_ref[...]
    rank1 = jnp.sum(jnp.where(hit1, seen, 0.0), axis=0, keepdims=True)
    rank2 = jnp.sum(jnp.where(hit2, seen, 0.0), axis=0, keepdims=True)
    carry_ref[...] = carry_ref[...] + jnp.sum(assign.astype(F32), axis=1, keepdims=True)
    cnt_ref[...] = carry_ref[...].astype(jnp.int32)

    ex1 = (i1 - EXPERT_LANE0).astype(F32)
    ex2 = (i2 - EXPERT_LANE0).astype(F32)
    field = lax.broadcasted_iota(jnp.int32, (META_ROWS, TM), 0)
    meta_ref[...] = jnp.where(field == 0, ex1, jnp.where(field == 1, ex2,
                              jnp.where(field == 2, rank1, jnp.where(field == 3, rank2, 0.0))))
    slot = lax.broadcasted_iota(jnp.int32, (LANES, TM), 0)
    wts_ref[...] = jnp.where(slot == 0, w1, jnp.where(slot == 1, w2, 0.0)).T


def _router(x_all, mod_p, mod_s, ng, layer, w_r, b_r):
    return pl.pallas_call(
        _router_kernel,
        grid=(N_TILES,),
        in_specs=[pl.BlockSpec((TM, D), lambda i: (i, 0))]
        + _unified_mod_specs(layer, (3, 4))
        + [_full((1, D)), _full((ROUTER_ROWS, D)), _full((ROUTER_ROWS, 1))],
        out_specs=[
            pl.BlockSpec((TM, D // 2), lambda i: (i, 0)),
            pl.BlockSpec((META_ROWS, TM), lambda i: (0, i)),
            pl.BlockSpec((TM, LANES), lambda i: (i, 0)),
            pl.BlockSpec((ROUTER_ROWS, 1), lambda i: (0, 0)),
        ],
        out_shape=[
            jax.ShapeDtypeStruct((T_PAD, D // 2), jnp.int32),
            jax.ShapeDtypeStruct((META_ROWS, T_PAD), F32),
            jax.ShapeDtypeStruct((T_PAD, LANES), F32),
            jax.ShapeDtypeStruct((ROUTER_ROWS, 1), jnp.int32),
        ],
        scratch_shapes=[pltpu.VMEM((ROUTER_ROWS, 1), F32), pltpu.VMEM((TM, TM), BF16)],
        compiler_params=_cparams("arbitrary"),
        name="router",
    )(x_all, mod_p, mod_s, mod_p, mod_s, ng, w_r, b_r)


SC_INDEX_MAX = 128
SC_ROWS_BYTES = 400 * 1024


def _gather_chunk(per_worker, row_bytes, buffers=1):
    top = min(SC_INDEX_MAX, SC_ROWS_BYTES // (row_bytes * buffers)) // SUBLANES * SUBLANES
    for ch in range(top, SUBLANES - 1, -SUBLANES):
        if per_worker % ch == 0:
            return ch
    raise ValueError(per_worker)


def _sc_gather(table, idx):
    nc, nw = SC_CORES, SC_CORES * SC_SUBCORES
    m = idx.shape[0]
    per_w = m // nw
    assert per_w * nw == m
    width = table.shape[1]
    ch = _gather_chunk(per_w, width * table.dtype.itemsize, buffers=2)
    mesh = plsc.VectorSubcoreMesh(core_axis_name="c", subcore_axis_name="s")

    n_chunks = per_w // ch

    @functools.partial(
        pl.kernel,
        out_type=jax.ShapeDtypeStruct((m, width), table.dtype),
        mesh=mesh,
        scratch_types=[pltpu.VMEM((ch,), jnp.int32)] * 2
        + [pltpu.VMEM((ch, width), table.dtype)] * 2
        + [pltpu.SemaphoreType.DMA] * 4,
    )
    def gather_kernel(t_hbm, i_hbm, o_hbm, idx_a, idx_b, rows_a, rows_b, sg_a, sg_b, sw_a, sw_b):
        wid = lax.axis_index("s") * nc + lax.axis_index("c")
        base = wid * per_w
        idx_v, rows_v, sem_g, sem_w = (idx_a, idx_b), (rows_a, rows_b), (sg_a, sg_b), (sw_a, sw_b)

        def out_rows(j):
            return o_hbm.at[pl.ds(pl.multiple_of(base + j * ch, 8), ch)]

        gathers, writes = {}, {}
        for j in range(n_chunks):
            b = j % 2
            if j >= 2:
                writes[j - 2].wait()
            pltpu.sync_copy(i_hbm.at[pl.ds(pl.multiple_of(base + j * ch, 8), ch)], idx_v[b])
            gathers[j] = pltpu.async_copy(t_hbm.at[idx_v[b]], rows_v[b], sem_g[b])
            if j >= 1:
                gathers[j - 1].wait()
                writes[j - 1] = pltpu.async_copy(rows_v[1 - b], out_rows(j - 1), sem_w[1 - b])
        last = n_chunks - 1
        gathers[last].wait()
        writes[last] = pltpu.async_copy(rows_v[last % 2], out_rows(last), sem_w[last % 2])
        for j in range(max(last - 1, 0), n_chunks):
            writes[j].wait()

    return gather_kernel(table, idx)


def _sc_scatter2(rows, dest0, dest1, n_out):
    nc, nw = SC_CORES, SC_CORES * SC_SUBCORES
    n, width = rows.shape
    per_w = n // nw
    assert per_w * nw == n
    ch = _gather_chunk(per_w, width * rows.dtype.itemsize)
    mesh = plsc.VectorSubcoreMesh(core_axis_name="c", subcore_axis_name="s")

    @functools.partial(
        pl.kernel,
        out_type=jax.ShapeDtypeStruct((n_out, width), rows.dtype),
        mesh=mesh,
        scratch_types=[
            pltpu.VMEM((ch,), jnp.int32),
            pltpu.VMEM((ch,), jnp.int32),
            pltpu.VMEM((ch, width), rows.dtype),
            pltpu.SemaphoreType.DMA,
            pltpu.SemaphoreType.DMA,
        ],
    )
    def scatter_kernel(r_hbm, d0_hbm, d1_hbm, o_hbm, i0_v, i1_v, rows_v, sem0, sem1):
        wid = lax.axis_index("s") * nc + lax.axis_index("c")
        base = wid * per_w

        @pl.loop(0, per_w // ch)
        def _(j):
            off = pl.multiple_of(base + j * ch, 8)
            pltpu.sync_copy(d0_hbm.at[pl.ds(off, ch)], i0_v)
            pltpu.sync_copy(d1_hbm.at[pl.ds(off, ch)], i1_v)
            pltpu.sync_copy(r_hbm.at[pl.ds(off, ch)], rows_v)
            c0 = pltpu.async_copy(rows_v, o_hbm.at[i0_v], sem0)
            c1 = pltpu.async_copy(rows_v, o_hbm.at[i1_v], sem1)
            c0.wait()
            c1.wait()

    return scatter_kernel(rows, dest0, dest1)


def _expert_kernel(be_ref, first_ref, nvalid_ref, next_ref, nused_ref, x_ref, wg_hbm, wu_hbm,
                   wd_hbm, y_ref, wgf_ref, wuf_ref, wdf_ref, wgb_ref, wub_ref, wdb_ref, sems,
                   *, layer):
    b = pl.program_id(0)

    def weight_copies(e):
        return (
            pltpu.make_async_copy(wg_hbm.at[layer, e], wgf_ref, sems.at[0]),
            pltpu.make_async_copy(wu_hbm.at[layer, e], wuf_ref, sems.at[1]),
            pltpu.make_async_copy(wd_hbm.at[layer, e], wdf_ref, sems.at[2]),
        )

    @pl.when(b == 0)
    def _():
        for cp in weight_copies(be_ref[0]):
            cp.start()

    @pl.when(b < nused_ref[0])
    def _():
        @pl.when(first_ref[b] == 1)
        def _():
            for cp in weight_copies(be_ref[b]):
                cp.wait()
            wgb_ref[...] = wgf_ref[...].astype(BF16)
            wub_ref[...] = wuf_ref[...].astype(BF16)
            wdb_ref[...] = wdf_ref[...].astype(BF16)

            @pl.when(next_ref[b] >= 0)
            def _():
                for cp in weight_copies(next_ref[b]):
                    cp.start()

        def run(rows):
            live = lax.broadcasted_iota(jnp.int32, (rows, 1), 0) < nvalid_ref[b]
            x = _unpack_bf16_pairs(jnp.where(live, x_ref[0:rows, :], 0), BF16)
            g = _dot(x, wgb_ref[...])
            u = _dot(x, wub_ref[...])
            hmid = (g * jax.nn.sigmoid(g)) * u
            y_ref[0:rows, :] = _pack_bf16_pairs(_dot(hmid.astype(BF16), wdb_ref[...]))

        half = BM // 2

        @pl.when(nvalid_ref[b] > half)
        def _():
            run(BM)

        @pl.when(nvalid_ref[b] <= half)
        def _():
            run(half)
            y_ref[half:BM, :] = jnp.zeros((BM - half, D // 2), jnp.int32)

    @pl.when(b >= nused_ref[0])
    def _():
        y_ref[...] = jnp.zeros((BM, D // 2), jnp.int32)


def _experts(x_sorted, block_expert, first, nvalid, next_expert, nused, layer, w_gate, w_up,
             w_down):
    def row_map(b, be, fi, nv, nx, nu):
        return (jnp.minimum(b, nu[0] - 1), 0)

    def out_map(b, be, fi, nv, nx, nu):
        return (b, 0)

    grid_spec = pltpu.PrefetchScalarGridSpec(
        num_scalar_prefetch=5,
        grid=(N_BLOCKS,),
        in_specs=[
            pl.BlockSpec((BM, D // 2), row_map),
            pl.BlockSpec(memory_space=pl.ANY),
            pl.BlockSpec(memory_space=pl.ANY),
            pl.BlockSpec(memory_space=pl.ANY),
        ],
        out_specs=pl.BlockSpec((BM, D // 2), out_map),
        scratch_shapes=[
            pltpu.VMEM((D, D_EXPERT), F32),
            pltpu.VMEM((D, D_EXPERT), F32),
            pltpu.VMEM((D_EXPERT, D), F32),
            pltpu.VMEM((D, D_EXPERT), BF16),
            pltpu.VMEM((D, D_EXPERT), BF16),
            pltpu.VMEM((D_EXPERT, D), BF16),
            pltpu.SemaphoreType.DMA((3,)),
        ],
    )
    return pl.pallas_call(
        functools.partial(_expert_kernel, layer=layer),
        grid_spec=grid_spec,
        out_shape=jax.ShapeDtypeStruct((P_ROWS, D // 2), jnp.int32),
        compiler_params=_cparams("arbitrary"),
        name="experts",
    )(block_expert, first, nvalid, next_expert, nused, x_sorted, w_gate, w_up, w_down)


def _combine_final_kernel(x_ref, g0_ref, g1_ref, wts_ref, gtp_ref, gts_ref, fg_ref, yp_ref,
                          ys_ref):
    i = pl.program_id(0)
    gate = _tile_mod(i, gtp_ref, gts_ref)
    y = _rms(_plus_moe(x_ref[...], g0_ref, g1_ref, wts_ref, gate), fg_ref[...])

    @pl.when(i < N_PROMPT_TILES)
    def _():
        yp_ref[...] = y

    @pl.when(i == N_PROMPT_TILES)
    def _():
        ys_ref[...] = y[0:DEC_BATCH, :]


def _combine_final(x_all, gathered, wts, mod_p, mod_s, layer, final_g):
    in_specs = [
        pl.BlockSpec((TM, D), lambda i: (i, 0)),
        pl.BlockSpec((TM, D // 2), lambda i: (i, 0)),
        pl.BlockSpec((TM, D // 2), lambda i: (i + N_TILES, 0)),
        pl.BlockSpec((TM, LANES), lambda i: (i, 0)),
    ] + _unified_mod_specs(layer, (5,))
    return pl.pallas_call(
        _combine_final_kernel,
        grid=(N_PROMPT_TILES + 1,),
        in_specs=in_specs + [_full((1, D))],
        out_specs=[
            pl.BlockSpec((TM, D), lambda i: (jnp.minimum(i, N_PROMPT_TILES - 1), 0)),
            pl.BlockSpec((DEC_BATCH, D), lambda i: (0, 0)),
        ],
        out_shape=[
            jax.ShapeDtypeStruct((T_PROMPT, D), F32),
            jax.ShapeDtypeStruct((DEC_BATCH, D), F32),
        ],
        compiler_params=_cparams("arbitrary"),
        name="combine_final",
    )(x_all, gathered, gathered, wts, mod_p, mod_s, final_g)


def _plan_kernel(cnt_ref, meta_ref, d0_ref, d1_ref, be_ref, first_ref, nvalid_ref, next_ref,
                 nused_ref, pstart_ref, nxt_ref):
    shift = BM.bit_length() - 1
    assert 1 << shift == BM

    def count(e):
        return cnt_ref[EXPERT_LANE0 + e, 0]

    def scan_back(k, nxt):
        e = N_EXPERTS - 1 - k
        nxt_ref[e] = nxt
        return jnp.where(count(e) > 0, e, nxt)

    lax.fori_loop(0, N_EXPERTS, scan_back, jnp.int32(-1))

    def fill(e, blk):
        c = count(e)
        pstart_ref[e] = blk << shift

        def one(j, carry):
            be_ref[blk + j] = e
            first_ref[blk + j] = jnp.where(j == 0, 1, 0)
            nvalid_ref[blk + j] = jnp.minimum(c - (j << shift), BM)
            next_ref[blk + j] = nxt_ref[e]
            return carry

        n_blk = (c + (BM - 1)) >> shift
        lax.fori_loop(0, n_blk, one, 0)
        return blk + n_blk

    used = lax.fori_loop(0, N_EXPERTS, fill, jnp.int32(0))
    nused_ref[0] = used

    def tail(b, carry):
        be_ref[b] = 0
        first_ref[b] = 0
        nvalid_ref[b] = 0
        next_ref[b] = -1
        return carry

    lax.fori_loop(used, N_BLOCKS, tail, 0)

    m = meta_ref[...]
    expert = m[0:2, :]
    dest = m[2:4, :]
    for e in range(N_EXPERTS):
        dest = dest + jnp.where(expert == float(e), pstart_ref[e].astype(F32), 0.0)
    dest = dest.astype(jnp.int32)
    d0_ref[...] = dest[0:1, :]
    d1_ref[...] = dest[1:2, :]


def _plan(cnt, meta):
    smem = pl.BlockSpec(memory_space=pltpu.SMEM)
    blocks = jax.ShapeDtypeStruct((N_BLOCKS,), jnp.int32)
    return pl.pallas_call(
        _plan_kernel,
        grid=(1,),
        in_specs=[smem, _full((META_ROWS, T_PAD))],
        out_specs=[_full((1, T_PAD)), _full((1, T_PAD)), smem, smem, smem, smem, smem],
        out_shape=[
            jax.ShapeDtypeStruct((1, T_PAD), jnp.int32),
            jax.ShapeDtypeStruct((1, T_PAD), jnp.int32),
            blocks, blocks, blocks, blocks,
            jax.ShapeDtypeStruct((1,), jnp.int32),
        ],
        scratch_shapes=[pltpu.SMEM((N_EXPERTS,), jnp.int32), pltpu.SMEM((N_EXPERTS,), jnp.int32)],
        compiler_params=_cparams("arbitrary"),
        name="plan",
    )(cnt, meta)


def _moe(x_all, mod_p, mod_s, ng, layer, w_r, b_r, w_gate, w_up, w_down):
    h2, meta, wts, cnt = _router(x_all, mod_p, mod_s, ng, layer, w_r, b_r)
    dest0, dest1, block_expert, first, nvalid, next_expert, nused = _plan(cnt, meta)
    dest0, dest1 = dest0.reshape(T_PAD), dest1.reshape(T_PAD)
    x_sorted = _sc_scatter2(h2, dest0, dest1, P_ROWS)
    y_sorted = _experts(x_sorted, block_expert, first, nvalid, next_expert, nused, layer,
                        w_gate, w_up, w_down)
    gathered = _sc_gather(y_sorted, jnp.concatenate([dest0, dest1]))
    return gathered, wts


def kernel(x_prompt, x_sample, c_prompt, c_sample, state_pool, state_conv, ada_w, ada_b, norm_g, final_g, a_w_in, a_ln_g, a_ln_b, a_w_s, a_b_s, a_w_out, b_w_in, b_w_grp, b_b_grp, b_scale, b_w_out, c_w_in, c_b_in, c_w_dw, c_b_dw, c_ln_g, c_ln_b, c_w_out, c_b_out, moe_w_grp, moe_b_grp, moe_w_exp, moe_b_exp, moe_w_gate, moe_w_up, moe_w_down):
    x_all = (x_prompt.reshape(T_PROMPT, D), x_sample.reshape(DEC_BATCH, D))
    mods = _ada_mods(jnp.concatenate([c_prompt, c_sample], axis=0), ada_w, ada_b)
    mod_p = mods[:, :BATCH].reshape(DEPTH, BATCH, 1, 6 * D)
    mod_s = mods[:, BATCH:]

    a_w_in_b, a_w_out_b = a_w_in.astype(BF16), a_w_out.astype(BF16)
    b_w_in_b, b_w_grp_b, b_w_out_b = b_w_in.astype(BF16), b_w_grp.astype(BF16), b_w_out.astype(BF16)
    c_w_in_b, c_w_out_b = c_w_in.astype(BF16), c_w_out.astype(BF16)
    pad_rows = ROUTER_ROWS - N_GROUPS - N_EXPERTS
    w_r = jnp.pad(jnp.swapaxes(jnp.concatenate([moe_w_grp, moe_w_exp], axis=2), 1, 2),
                  ((0, 0), (0, pad_rows), (0, 0))).astype(BF16)
    b_r = jnp.pad(jnp.concatenate([moe_b_grp, moe_b_exp], axis=1), ((0, 0), (0, pad_rows)))

    new_a_p, new_a_s, new_b_p, new_b_s, new_c_p, new_c_s = [], [], [], [], [], []
    ia = ib = ic = 0
    pending = None
    for layer in range(DEPTH):
        ng1 = norm_g[layer, 0].reshape(1, D)
        ng2 = norm_g[layer, 1].reshape(1, D)
        kind = layer % 3
        if kind == 0:
            x_all, st_p, st_s = _mixer_a(
                x_all, pending, mod_p, mod_s, ng1, layer, ia, a_w_in_b, a_ln_g.reshape(-1, 1, D),
                a_ln_b.reshape(-1, 1, D), a_w_s, a_b_s, a_w_out_b)
            new_a_p.append(st_p)
            new_a_s.append(st_s)
            ia += 1
        elif kind == 1:
            x_all, st_p, st_s = _mixer_b(
                x_all, pending, mod_p, mod_s, ng1, layer, ib, state_pool, b_w_in_b, b_w_grp_b,
                b_b_grp.reshape(-1, len(POOL_WINDOWS), 1, B_GROUP_DIM),
                b_scale.reshape(-1, 1, D), b_w_out_b)
            new_b_p.append(st_p)
            new_b_s.append(st_s)
            ib += 1
        else:
            x_all, st_p, st_s = _mixer_c(
                x_all, pending, mod_p, mod_s, ng1, layer, ic, state_conv, c_w_in_b,
                c_b_in.reshape(-1, 1, 2 * D), c_w_dw, c_b_dw.reshape(-1, 1, D),
                c_ln_g.reshape(-1, 1, D), c_ln_b.reshape(-1, 1, D), c_w_out_b,
                c_b_out.reshape(-1, 1, D))
            new_c_p.append(st_p)
            new_c_s.append(st_s)
            ic += 1
        pending = _moe(x_all, mod_p, mod_s, ng2, layer, w_r[layer],
                       b_r[layer].reshape(ROUTER_ROWS, 1), moe_w_gate, moe_w_up, moe_w_down)

    y_p, y_s = _combine_final(x_all, *pending, mod_p, mod_s, DEPTH - 1, final_g.reshape(1, D))
    return (y_p.reshape(BATCH, SEQ, D), y_s.reshape(DEC_BATCH, 1, D),
            jnp.stack(new_a_p), jnp.stack(new_a_s), jnp.stack(new_b_p), jnp.stack(new_b_s),
            jnp.stack(new_c_p), jnp.stack(new_c_s))
```

```python
import functools

import jax
import jax.numpy as jnp
from jax import lax
from jax.experimental import pallas as pl
from jax.experimental.pallas import tpu as pltpu
from jax.experimental.pallas import tpu_sc as plsc

F32 = jnp.float32
BF16 = jnp.bfloat16

D = 1024
BATCH = 8
SEQ = 2048
DEPTH = 4
DEC_BATCH = 128
PAST_LEN = 16384
CHUNK = 128
A_HEADS = 8
POOL_WINDOWS = (2, 4, 8, 16)
B_GROUP_DIM = D // len(POOL_WINDOWS)
POOL_BUF = max(POOL_WINDOWS) - 1
CONV_WIDTH = 31
CONV_BUF = CONV_WIDTH - 1
N_GROUPS = 4
EXPERTS_PER_GROUP = 8
N_EXPERTS = N_GROUPS * EXPERTS_PER_GROUP
D_EXPERT = D // 2
EPS = 1e-6

LANES = 128
SUBLANES = 8
CONV_ROWS = 128
SC_CORES = 2
SC_SUBCORES = 16
TM = 512
TILES_PER_SEQ = SEQ // TM
T_PROMPT = BATCH * SEQ
T_ALL = T_PROMPT + DEC_BATCH
N_TILES = -(-T_ALL // TM)
T_PAD = N_TILES * TM
N_PROMPT_TILES = T_PROMPT // TM
SAMPLE_ROWS = 32
BM = 512
N_BLOCKS = -(-(2 * T_PAD) // BM) + N_EXPERTS
P_ROWS = N_BLOCKS * BM
META_ROWS = 8
POOL_CARRY = 16
CONV_CARRY = 32
EXPERT_LANE0 = N_GROUPS
ROUTER_ROWS = 48
VMEM_LIMIT = 56 * 1024 * 1024


def _cparams(*sem):
    return pltpu.CompilerParams(dimension_semantics=sem, vmem_limit_bytes=VMEM_LIMIT)


def _rms(x, g):
    return x * lax.rsqrt(jnp.mean(x * x, axis=-1, keepdims=True) + EPS) * g


def _ln(x, g, b):
    mu = jnp.mean(x, axis=-1, keepdims=True)
    xc = x - mu
    var = jnp.mean(xc * xc, axis=-1, keepdims=True)
    return xc * lax.rsqrt(var + EPS) * g + b


def _dot(a, b):
    return jnp.dot(a, b, preferred_element_type=F32)


def _pack_bf16_pairs(v):
    w = v.shape[1] // 2
    lo = lax.bitcast_convert_type(v[:, :w].astype(BF16).astype(F32), jnp.uint32)
    hi = lax.bitcast_convert_type(v[:, w:].astype(BF16).astype(F32), jnp.uint32)
    return lax.bitcast_convert_type(lax.shift_right_logical(lo, jnp.uint32(16)) | hi, jnp.int32)


def _unpack_bf16_pairs(words, dtype):
    u = lax.bitcast_convert_type(words, jnp.uint32)
    lo = lax.bitcast_convert_type(lax.shift_left(u, jnp.uint32(16)), F32)
    hi = lax.bitcast_convert_type(u & jnp.uint32(0xFFFF0000), F32)
    return jnp.concatenate([lo.astype(dtype), hi.astype(dtype)], axis=1)


def _ada_kernel(c_ref, w_ref, b_ref, o_ref):
    c = c_ref[...]
    s = (c * jax.nn.sigmoid(c)).astype(BF16)
    o_ref[...] = _dot(s, w_ref[...].astype(BF16)) + b_ref[...]


def _ada_mods(c_all, ada_w, ada_b):
    n = c_all.shape[0]
    tn = 2048
    return pl.pallas_call(
        _ada_kernel,
        grid=(DEPTH, 6 * D // tn),
        in_specs=[
            pl.BlockSpec((n, D), lambda l, j: (0, 0)),
            pl.BlockSpec((None, D, tn), lambda l, j: (l, 0, j)),
            pl.BlockSpec((None, 1, tn), lambda l, j: (l, 0, j)),
        ],
        out_specs=pl.BlockSpec((None, n, tn), lambda l, j: (l, 0, j)),
        out_shape=jax.ShapeDtypeStruct((DEPTH, n, 6 * D), F32),
        compiler_params=_cparams("parallel", "parallel"),
        name="ada_mods",
    )(c_all, ada_w, ada_b.reshape(DEPTH, 1, 6 * D))


def _mod_specs_prompt(layer, chunks):
    return [
        pl.BlockSpec((None, None, 1, D),
                     lambda i, c=c: (layer, jnp.minimum(i // TILES_PER_SEQ, BATCH - 1), 0, c))
        for c in chunks
    ]


def _mod_specs_sample(layer, chunks, rows, row_block0=0):
    return [
        pl.BlockSpec((None, rows, D), lambda i, c=c: (layer, i + row_block0, c)) for c in chunks
    ]


def _full(shape):
    nd = len(shape)
    return pl.BlockSpec(shape, lambda i: (0,) * nd)


def _layer_block(shape, layer):
    nd = len(shape)
    return pl.BlockSpec((None,) + shape, lambda i: (layer,) + (0,) * nd)


def _plus_moe(x, g0_ref, g1_ref, wts_ref, gate):
    w = wts_ref[...]
    y0 = _unpack_bf16_pairs(g0_ref[...], F32)
    y1 = _unpack_bf16_pairs(g1_ref[...], F32)
    return x + gate * (w[:, 0:1] * y0 + w[:, 1:2] * y1)


def _residual_in(refs, pending):
    if not pending:
        return refs[0][...], refs[1:]
    x_ref, g0_ref, g1_ref, wts_ref, gate_ref = refs[:5]
    return _plus_moe(x_ref[...], g0_ref, g1_ref, wts_ref, gate_ref[...]), refs[5:]


def _pending_prompt(pending, mod_p, layer):
    if pending is None:
        return [], []
    gathered, wts = pending
    specs = [
        pl.BlockSpec((TM, D // 2), lambda i: (i, 0)),
        pl.BlockSpec((TM, D // 2), lambda i: (i + N_TILES, 0)),
        pl.BlockSpec((TM, LANES), lambda i: (i, 0)),
    ] + _mod_specs_prompt(layer - 1, (5,))
    return specs, [gathered, gathered, wts, mod_p]


def _pending_sample(pending, mod_s, layer, rows):
    if pending is None:
        return [], []
    gathered, wts = pending
    first = T_PROMPT // rows
    second = (T_PAD + T_PROMPT) // rows
    specs = [
        pl.BlockSpec((rows, D // 2), lambda i: (first + i, 0)),
        pl.BlockSpec((rows, D // 2), lambda i: (second + i, 0)),
        pl.BlockSpec((rows, LANES), lambda i: (first + i, 0)),
    ] + _mod_specs_sample(layer - 1, (5,), rows)
    return specs, [gathered, gathered, wts, mod_s]


def _mix_a_front(x, sh, sc, ng, win_ref, lng_ref, lnb_ref):
    h = (_rms(x, ng) * (1.0 + sc) + sh).astype(BF16)
    z = jax.nn.gelu(_dot(h, win_ref[...]))
    u = z[:, :D]
    v = _ln(z[:, D:], lng_ref[...], lnb_ref[...])
    return u, v


def _mix_a_prompt_kernel(*refs, pending, first):
    x, refs = _residual_in(refs, pending)
    (sh_ref, sc_ref, gt_ref, ng_ref, win_ref, lng_ref, lnb_ref, ws_ref, bs_ref, wout_ref,
     x1_ref, cv_ref, mixed_ref) = refs
    i = pl.program_id(0)
    u, v = _mix_a_front(x, sh_ref[...], sc_ref[...], ng_ref[...], win_ref, lng_ref, lnb_ref)
    vb = v.astype(BF16)
    n_chunks = TM // CHUNK
    row = lax.broadcasted_iota(jnp.int32, (CHUNK, CHUNK), 0)
    col = lax.broadcasted_iota(jnp.int32, (CHUNK, CHUNK), 1)
    tril = row >= col
    hd_dim = D // A_HEADS
    for hd in range(A_HEADS):
        ws = jnp.where(tril, ws_ref[hd], 0.0).astype(BF16)
        cols = slice(hd * hd_dim, (hd + 1) * hd_dim)
        vcat = jnp.concatenate([vb[c * CHUNK:(c + 1) * CHUNK, cols] for c in range(n_chunks)], axis=1)
        m = _dot(ws, vcat)
        for c in range(n_chunks):
            mixed_ref[c * CHUNK:(c + 1) * CHUNK, cols] = m[:, c * hd_dim:(c + 1) * hd_dim] + bs_ref[hd]
    y = _dot((u * mixed_ref[...]).astype(BF16), wout_ref[...])
    out = x + gt_ref[...] * y
    if first:
        out = jnp.where(i < N_PROMPT_TILES, out, 0.0)
    x1_ref[...] = out

    @pl.when(i % TILES_PER_SEQ == TILES_PER_SEQ - 1)
    def _():
        cv_ref[...] = v[TM - CHUNK:, :]


def _mix_a_sample_kernel(*refs, pending, first):
    if first:
        refs = refs[:1] + refs[2:]
    x, refs = _residual_in(refs, pending)
    (sh_ref, sc_ref, gt_ref, ng_ref, win_ref, lng_ref, lnb_ref, wsd_ref, bsd_ref, wout_ref,
     x1_ref, cv_ref) = refs
    u, v = _mix_a_front(x, sh_ref[...], sc_ref[...], ng_ref[...], win_ref, lng_ref, lnb_ref)
    mixed = wsd_ref[...] * v + bsd_ref[...]
    y = _dot((u * mixed).astype(BF16), wout_ref[...])
    x1_ref[...] = x + gt_ref[...] * y
    cv_ref[...] = v


def _mixer_a(x_in, pending, mod_p, mod_s, ng, layer, la, w_in, ln_g, ln_b, w_s, b_s, w_out):
    first = isinstance(x_in, tuple)
    x_all = x_in[0] if first else x_in
    bs_b = jnp.broadcast_to(b_s[la][:, :, None], (A_HEADS, CHUNK, D // A_HEADS))
    common = [
        _layer_block((D, 2 * D), la), _layer_block((1, D), la), _layer_block((1, D), la),
    ]
    has_pending = pending is not None
    pend_specs, pend_args = _pending_prompt(pending, mod_p, layer)
    last_tile = N_PROMPT_TILES - 1
    x_all, cv_p = pl.pallas_call(
        functools.partial(_mix_a_prompt_kernel, pending=has_pending, first=first),
        grid=(N_TILES if first else N_PROMPT_TILES,),
        in_specs=[pl.BlockSpec((TM, D), lambda i: (jnp.minimum(i, last_tile), 0))] + pend_specs
        + _mod_specs_prompt(layer, (0, 1, 2))
        + [_full((1, D))] + common
        + [_layer_block((A_HEADS, CHUNK, CHUNK), la), _full((A_HEADS, CHUNK, D // A_HEADS)),
           _layer_block((D, D), la)],
        out_specs=[
            pl.BlockSpec((TM, D), lambda i: (i, 0)),
            pl.BlockSpec((None, CHUNK, D), lambda i: (jnp.minimum(i, last_tile) // TILES_PER_SEQ, 0, 0)),
        ],
        out_shape=[
            jax.ShapeDtypeStruct((T_PAD, D), F32),
            jax.ShapeDtypeStruct((BATCH, CHUNK, D), F32),
        ],
        scratch_shapes=[pltpu.VMEM((TM, D), F32)],
        input_output_aliases={} if first else {0: 0},
        compiler_params=_cparams("arbitrary"),
        name="mixer_a_prompt",
    )(x_all, *pend_args, mod_p, mod_p, mod_p, ng, w_in, ln_g, ln_b, w_s, bs_b, w_out)

    wsd = jnp.repeat(w_s[la, :, 0, 0], D // A_HEADS).reshape(1, D)
    bsd = jnp.repeat(b_s[la, :, 0], D // A_HEADS).reshape(1, D)
    sblk = T_PROMPT // DEC_BATCH
    pend_specs, pend_args = _pending_sample(pending, mod_s, layer, DEC_BATCH)
    if first:
        x_specs = [pl.BlockSpec((DEC_BATCH, D), lambda i: (0, 0)), pl.BlockSpec(memory_space=pl.ANY)]
        x_args = [x_in[1], x_all]
    else:
        x_specs = [pl.BlockSpec((DEC_BATCH, D), lambda i: (sblk, 0))]
        x_args = [x_all]
    x_all, cv_s = pl.pallas_call(
        functools.partial(_mix_a_sample_kernel, pending=has_pending, first=first),
        grid=(1,),
        in_specs=x_specs + pend_specs
        + _mod_specs_sample(layer, (0, 1, 2), DEC_BATCH)
        + [_full((1, D))] + common
        + [_full((1, D)), _full((1, D)), _layer_block((D, D), la)],
        out_specs=[
            pl.BlockSpec((DEC_BATCH, D), lambda i: (sblk, 0)),
            pl.BlockSpec((DEC_BATCH, D), lambda i: (0, 0)),
        ],
        out_shape=[
            jax.ShapeDtypeStruct((T_PAD, D), F32),
            jax.ShapeDtypeStruct((DEC_BATCH, D), F32),
        ],
        input_output_aliases={len(x_args) - 1: 0},
        compiler_params=_cparams("arbitrary"),
        name="mixer_a_sample",
    )(*x_args, *pend_args, mod_s, mod_s, mod_s, ng, w_in, ln_g, ln_b, wsd, bsd, w_out)
    return x_all, cv_p, cv_s.reshape(DEC_BATCH, 1, D)


def _mix_b_tail(pooled_groups, wgrp_ref, bgrp_ref, scale_ref, wout_ref):
    outs = [
        _dot(pg.astype(BF16), wgrp_ref[g]) + bgrp_ref[g]
        for g, pg in enumerate(pooled_groups)
    ]
    mixed = jnp.concatenate(outs, axis=1) * scale_ref[...]
    return _dot(mixed.astype(BF16), wout_ref[...])


def _mix_b_prompt_kernel(*refs, pending):
    x, refs = _residual_in(refs, pending)
    (sh_ref, sc_ref, gt_ref, ng_ref, win_ref, wgrp_ref, bgrp_ref, scale_ref, wout_ref,
     x1_ref, st_ref, full_ref, s2_ref, s4_ref, s8_ref) = refs
    assert POOL_WINDOWS == (2, 4, 8, 16)
    i = pl.program_id(0)
    j = i % TILES_PER_SEQ
    h = (_rms(x, ng_ref[...]) * (1.0 + sc_ref[...]) + sh_ref[...]).astype(BF16)
    p = _dot(h, win_ref[...])
    gd = B_GROUP_DIM
    top = SUBLANES
    cur = top + POOL_CARRY
    rows = TM + POOL_CARRY

    @pl.when(i == 0)
    def _():
        for ref in (full_ref, s2_ref, s4_ref, s8_ref):
            ref[0:top, :] = jnp.zeros((top, ref.shape[1]), F32)

    full_ref[top:cur, :] = jnp.where(j == 0, 0.0, full_ref[top:cur, :])
    full_ref[cur:cur + TM, :] = p
    s2_ref[top:top + rows, :] = full_ref[top:top + rows, :] + full_ref[pl.ds(top - 1, rows), :]
    s4_ref[top:top + rows, :] = s2_ref[top:top + rows, gd:] + s2_ref[pl.ds(top - 2, rows), gd:]
    s8_ref[top:top + rows, :] = s4_ref[top:top + rows, gd:] + s4_ref[pl.ds(top - 4, rows), gd:]
    s16 = s8_ref[cur:cur + TM, gd:] + s8_ref[pl.ds(cur - 8, TM), gd:]
    sums = [s2_ref[cur:cur + TM, 0:gd], s4_ref[cur:cur + TM, 0:gd], s8_ref[cur:cur + TM, 0:gd], s16]
    pos = j * TM + lax.broadcasted_iota(jnp.int32, (TM, 1), 0)
    pooled = []
    for g, w in enumerate(POOL_WINDOWS):
        cnt = jnp.minimum(w, pos + 1).astype(F32)
        pooled.append(sums[g] / cnt - p[:, g * gd:(g + 1) * gd])
    y = _mix_b_tail(pooled, wgrp_ref, bgrp_ref, scale_ref, wout_ref)
    x1_ref[...] = x + gt_ref[...] * y

    @pl.when(j == TILES_PER_SEQ - 1)
    def _():
        st_ref[...] = full_ref[pl.ds(cur + TM - POOL_BUF, POOL_BUF), :]

    full_ref[top:cur, :] = full_ref[pl.ds(TM + top, POOL_CARRY), :]


def _mix_b_sample_kernel(*refs, pending):
    x, refs = _residual_in(refs, pending)
    (sh_ref, sc_ref, gt_ref, ng_ref, st_ref, win_ref, wgrp_ref, bgrp_ref, scale_ref, wout_ref,
     x1_ref, nst_ref) = refs
    h = (_rms(x, ng_ref[...]) * (1.0 + sc_ref[...]) + sh_ref[...]).astype(BF16)
    p = _dot(h, win_ref[...])
    pooled = []
    for g, w in enumerate(POOL_WINDOWS):
        s = p[:, g * B_GROUP_DIM:(g + 1) * B_GROUP_DIM]
        for k in range(1, w):
            s = s + st_ref[:, POOL_BUF - k, g * B_GROUP_DIM:(g + 1) * B_GROUP_DIM]
        cnt = float(min(w, PAST_LEN + 1))
        pooled.append(s / cnt - p[:, g * B_GROUP_DIM:(g + 1) * B_GROUP_DIM])
    y = _mix_b_tail(pooled, wgrp_ref, bgrp_ref, scale_ref, wout_ref)
    x1_ref[...] = x + gt_ref[...] * y
    for k in range(1, POOL_BUF):
        nst_ref[k - 1] = st_ref[:, k, :]
    nst_ref[POOL_BUF - 1] = p


def _mixer_b(x_all, pending, mod_p, mod_s, ng, layer, lb, state, w_in, w_grp, b_grp, scale,
             w_out):
    gd = B_GROUP_DIM
    common = [
        _layer_block((D, D), lb), _layer_block((len(POOL_WINDOWS), gd, gd), lb),
        _layer_block((len(POOL_WINDOWS), 1, gd), lb), _layer_block((1, D), lb),
        _layer_block((D, D), lb),
    ]
    has_pending = pending is not None
    pend_specs, pend_args = _pending_prompt(pending, mod_p, layer)
    x_all, st_p = pl.pallas_call(
        functools.partial(_mix_b_prompt_kernel, pending=has_pending),
        grid=(N_PROMPT_TILES,),
        in_specs=[pl.BlockSpec((TM, D), lambda i: (i, 0))] + pend_specs
        + _mod_specs_prompt(layer, (0, 1, 2)) + [_full((1, D))] + common,
        out_specs=[
            pl.BlockSpec((TM, D), lambda i: (i, 0)),
            pl.BlockSpec((None, POOL_BUF, D), lambda i: (i // TILES_PER_SEQ, 0, 0)),
        ],
        out_shape=[
            jax.ShapeDtypeStruct((T_PAD, D), F32),
            jax.ShapeDtypeStruct((BATCH, POOL_BUF, D), F32),
        ],
        scratch_shapes=[
            pltpu.VMEM((SUBLANES + POOL_CARRY + TM, D - g * gd), F32) for g in (0, 0, 1, 2)],
        input_output_aliases={0: 0},
        compiler_params=_cparams("arbitrary"),
        name="mixer_b_prompt",
    )(x_all, *pend_args, mod_p, mod_p, mod_p, ng, w_in, w_grp, b_grp, scale, w_out)

    r = SAMPLE_ROWS
    sblk = T_PROMPT // r
    pend_specs, pend_args = _pending_sample(pending, mod_s, layer, r)
    x_all, st_s = pl.pallas_call(
        functools.partial(_mix_b_sample_kernel, pending=has_pending),
        grid=(DEC_BATCH // r,),
        in_specs=[pl.BlockSpec((r, D), lambda i: (sblk + i, 0))] + pend_specs
        + _mod_specs_sample(layer, (0, 1, 2), r) + [_full((1, D))]
        + [pl.BlockSpec((None, r, POOL_BUF, D), lambda i: (lb, i, 0, 0))] + common,
        out_specs=[
            pl.BlockSpec((r, D), lambda i: (sblk + i, 0)),
            pl.BlockSpec((POOL_BUF, r, D), lambda i: (0, i, 0)),
        ],
        out_shape=[
            jax.ShapeDtypeStruct((T_PAD, D), F32),
            jax.ShapeDtypeStruct((POOL_BUF, DEC_BATCH, D), F32),
        ],
        input_output_aliases={0: 0},
        compiler_params=_cparams("arbitrary"),
        name="mixer_b_sample",
    )(x_all, *pend_args, mod_s, mod_s, mod_s, ng, state, w_in, w_grp, b_grp, scale, w_out)
    return x_all, st_p, jnp.transpose(st_s, (1, 0, 2))


def _mix_c_glu(x, sh, sc, ng, win_ref, bin_ref):
    h = (_rms(x, ng) * (1.0 + sc) + sh).astype(BF16)
    ag = _dot(h, win_ref[...]) + bin_ref[...]
    return ag[:, :D] * jax.nn.sigmoid(ag[:, D:])


def _mix_c_tail(conv, lng_ref, lnb_ref, wout_ref, bout_ref):
    z = _ln(conv, lng_ref[...], lnb_ref[...])
    z = z * jax.nn.sigmoid(z)
    return _dot(z.astype(BF16), wout_ref[...]) + bout_ref[...]


def _mix_c_prompt_kernel(*refs, pending):
    x, refs = _residual_in(refs, pending)
    (sh_ref, sc_ref, gt_ref, ng_ref, win_ref, bin_ref, wdw_ref, bdw_ref, lng_ref, lnb_ref,
     wout_ref, bout_ref, x1_ref, st_ref, full_ref, conv_ref, *shift_refs) = refs
    i = pl.program_id(0)
    j = i % TILES_PER_SEQ
    glu = _mix_c_glu(x, sh_ref[...], sc_ref[...], ng_ref[...], win_ref, bin_ref)

    @pl.when(j == 0)
    def _():
        full_ref[0:CONV_CARRY, :] = jnp.zeros((CONV_CARRY, D), F32)

    full_ref[CONV_CARRY:CONV_CARRY + TM, :] = glu
    off = CONV_CARRY - CONV_BUF
    span = TM + (off + CONV_BUF) // SUBLANES * SUBLANES - SUBLANES
    for c in range(D // LANES):
        cols = slice(c * LANES, (c + 1) * LANES)
        shift_ref = shift_refs[c % len(shift_refs)]
        for r in range(1, SUBLANES):
            shift_ref[r - 1] = full_ref[pl.ds(r, span), cols]
        for rb in range(TM // CONV_ROWS):
            acc = None
            for k in range(CONV_WIDTH):
                q, r = divmod(off + k, SUBLANES)
                start = q * SUBLANES + rb * CONV_ROWS
                if r == 0:
                    src = full_ref[pl.ds(start, CONV_ROWS), cols]
                else:
                    src = shift_ref[r - 1, pl.ds(start, CONV_ROWS), :]
                term = src * wdw_ref[k:k + 1, cols]
                acc = term if acc is None else acc + term
            conv_ref[pl.ds(rb * CONV_ROWS, CONV_ROWS), cols] = acc + bdw_ref[:, cols]
    y = _mix_c_tail(conv_ref[...], lng_ref, lnb_ref, wout_ref, bout_ref)
    x1_ref[...] = x + gt_ref[...] * y

    @pl.when(j == TILES_PER_SEQ - 1)
    def _():
        st_ref[...] = full_ref[pl.ds(CONV_CARRY + TM - CONV_BUF, CONV_BUF), :]

    full_ref[0:CONV_CARRY, :] = full_ref[pl.ds(TM, CONV_CARRY), :]


def _mix_c_sample_kernel(*refs, pending):
    x, refs = _residual_in(refs, pending)
    (sh_ref, sc_ref, gt_ref, ng_ref, st_ref, win_ref, bin_ref, wdw_ref, bdw_ref, lng_ref,
     lnb_ref, wout_ref, bout_ref, x1_ref, nst_ref) = refs
    glu = _mix_c_glu(x, sh_ref[...], sc_ref[...], ng_ref[...], win_ref, bin_ref)
    acc = glu * wdw_ref[CONV_BUF:CONV_BUF + 1, :]
    for k in range(CONV_BUF):
        acc = acc + st_ref[:, k, :] * wdw_ref[k:k + 1, :]
    y = _mix_c_tail(acc + bdw_ref[...], lng_ref, lnb_ref, wout_ref, bout_ref)
    x1_ref[...] = x + gt_ref[...] * y
    for k in range(1, CONV_BUF):
        nst_ref[k - 1] = st_ref[:, k, :]
    nst_ref[CONV_BUF - 1] = glu


def _mixer_c(x_all, pending, mod_p, mod_s, ng, layer, lc, state, w_in, b_in, w_dw, b_dw, ln_g,
             ln_b, w_out, b_out):
    common = [
        _layer_block((D, 2 * D), lc), _layer_block((1, 2 * D), lc),
        _layer_block((CONV_WIDTH, D), lc), _layer_block((1, D), lc), _layer_block((1, D), lc),
        _layer_block((1, D), lc), _layer_block((D, D), lc), _layer_block((1, D), lc),
    ]
    has_pending = pending is not None
    pend_specs, pend_args = _pending_prompt(pending, mod_p, layer)
    x_all, st_p = pl.pallas_call(
        functools.partial(_mix_c_prompt_kernel, pending=has_pending),
        grid=(N_PROMPT_TILES,),
        in_specs=[pl.BlockSpec((TM, D), lambda i: (i, 0))] + pend_specs
        + _mod_specs_prompt(layer, (0, 1, 2)) + [_full((1, D))] + common,
        out_specs=[
            pl.BlockSpec((TM, D), lambda i: (i, 0)),
            pl.BlockSpec((None, CONV_BUF, D), lambda i: (i // TILES_PER_SEQ, 0, 0)),
        ],
        out_shape=[
            jax.ShapeDtypeStruct((T_PAD, D), F32),
            jax.ShapeDtypeStruct((BATCH, CONV_BUF, D), F32),
        ],
        scratch_shapes=[pltpu.VMEM((TM + CONV_CARRY, D), F32), pltpu.VMEM((TM, D), F32)]
        + [pltpu.VMEM((SUBLANES - 1, TM + CONV_CARRY - SUBLANES, LANES), F32)] * 2,
        input_output_aliases={0: 0},
        compiler_params=_cparams("arbitrary"),
        name="mixer_c_prompt",
    )(x_all, *pend_args, mod_p, mod_p, mod_p, ng, w_in, b_in, w_dw, b_dw, ln_g, ln_b, w_out,
      b_out)

    r = SAMPLE_ROWS
    sblk = T_PROMPT // r
    pend_specs, pend_args = _pending_sample(pending, mod_s, layer, r)
    x_all, st_s = pl.pallas_call(
        functools.partial(_mix_c_sample_kernel, pending=has_pending),
        grid=(DEC_BATCH // r,),
        in_specs=[pl.BlockSpec((r, D), lambda i: (sblk + i, 0))] + pend_specs
        + _mod_specs_sample(layer, (0, 1, 2), r) + [_full((1, D))]
        + [pl.BlockSpec((None, r, CONV_BUF, D), lambda i: (lc, i, 0, 0))] + common,
        out_specs=[
            pl.BlockSpec((r, D), lambda i: (sblk + i, 0)),
            pl.BlockSpec((CONV_BUF, r, D), lambda i: (0, i, 0)),
        ],
        out_shape=[
            jax.ShapeDtypeStruct((T_PAD, D), F32),
            jax.ShapeDtypeStruct((CONV_BUF, DEC_BATCH, D), F32),
        ],
        input_output_aliases={0: 0},
        compiler_params=_cparams("arbitrary"),
        name="mixer_c_sample",
    )(x_all, *pend_args, mod_s, mod_s, mod_s, ng, state, w_in, b_in, w_dw, b_dw, ln_g, ln_b,
      w_out, b_out)
    return x_all, st_p, jnp.transpose(st_s, (1, 0, 2))


def _tile_mod(i, p_ref, s_ref):
    s_rows = jnp.concatenate([s_ref[...], jnp.zeros((TM - DEC_BATCH, D), F32)], axis=0)
    return jnp.where(i >= N_PROMPT_TILES, s_rows, p_ref[...])


def _unified_mod_specs(layer, chunks):
    specs = []
    for c in chunks:
        specs.append(pl.BlockSpec(
            (None, None, 1, D),
            lambda i, c=c: (layer, jnp.minimum(i // TILES_PER_SEQ, BATCH - 1), 0, c)))
        specs.append(pl.BlockSpec((None, DEC_BATCH, D), lambda i, c=c: (layer, 0, c)))
    return specs


def _router_kernel(x_ref, shp_ref, shs_ref, scp_ref, scs_ref, ng_ref, wr_ref, br_ref,
                   h2_ref, meta_ref, wts_ref, cnt_ref, carry_ref, before_ref):
    i = pl.program_id(0)

    @pl.when(i == 0)
    def _():
        carry_ref[...] = jnp.zeros((ROUTER_ROWS, 1), F32)
        earlier = lax.broadcasted_iota(jnp.int32, (TM, TM), 0)
        token = lax.broadcasted_iota(jnp.int32, (TM, TM), 1)
        before_ref[...] = (earlier < token).astype(BF16)

    sh = _tile_mod(i, shp_ref, shs_ref)
    sc = _tile_mod(i, scp_ref, scs_ref)
    h2 = _rms(x_ref[...], ng_ref[...]) * (1.0 + sc) + sh
    h2_ref[...] = _pack_bf16_pairs(h2)
    lg = lax.dot_general(wr_ref[...], h2.astype(BF16), (((1,), (1,)), ((), ())),
                         preferred_element_type=F32) + br_ref[...]
    row = lax.broadcasted_iota(jnp.int32, (ROUTER_ROWS, TM), 0)
    row_f = row.astype(F32)
    far = float(ROUTER_ROWS)
    neg = -jnp.inf
    is_g = row < N_GROUPS
    gm = jnp.where(is_g, lg, neg)
    gmax = jnp.max(gm, axis=0, keepdims=True)
    gsel = jnp.min(jnp.where(gm == gmax, row_f, far), axis=0, keepdims=True).astype(jnp.int32)
    gsum = jnp.sum(jnp.where(is_g, jnp.exp(lg - gmax), 0.0), axis=0, keepdims=True)
    g_w = 1.0 / gsum
    e_row = row - EXPERT_LANE0
    in_grp = (e_row >= 0) & (e_row < N_EXPERTS) & (
        lax.shift_right_arithmetic(e_row, EXPERTS_PER_GROUP.bit_length() - 1) == gsel)
    em = jnp.where(in_grp, lg, neg)
    m1 = jnp.max(em, axis=0, keepdims=True)
    i1 = jnp.min(jnp.where(em == m1, row_f, far), axis=0, keepdims=True).astype(jnp.int32)
    em2 = jnp.where(row == i1, neg, em)
    m2 = jnp.max(em2, axis=0, keepdims=True)
    i2 = jnp.min(jnp.where(em2 == m2, row_f, far), axis=0, keepdims=True).astype(jnp.int32)
    e2 = jnp.exp(m2 - m1)
    den = 1.0 + e2
    w1 = (1.0 / den) * g_w
    w2 = (e2 / den) * g_w

    hit1 = row == i1
    hit2 = row == i2
    assign = jnp.where(hit1 | hit2, 1.0, 0.0).astype(BF16)
    seen = _dot(assign, before_ref[...]) + carry_ref[...]
    rank1 = jnp.sum(jnp.where(hit1, seen, 0.0), axis=0, keepdims=True)
    rank2 = jnp.sum(jnp.where(hit2, seen, 0.0), axis=0, keepdims=True)
    carry_ref[...] = carry_ref[...] + jnp.sum(assign.astype(F32), axis=1, keepdims=True)
    cnt_ref[...] = carry_ref[...].astype(jnp.int32)

    ex1 = (i1 - EXPERT_LANE0).astype(F32)
    ex2 = (i2 - EXPERT_LANE0).astype(F32)
    field = lax.broadcasted_iota(jnp.int32, (META_ROWS, TM), 0)
    meta_ref[...] = jnp.where(field == 0, ex1, jnp.where(field == 1, ex2,
                              jnp.where(field == 2, rank1, jnp.where(field == 3, rank2, 0.0))))
    slot = lax.broadcasted_iota(jnp.int32, (LANES, TM), 0)
    wts_ref[...] = jnp.where(slot == 0, w1, jnp.where(slot == 1, w2, 0.0)).T


def _router(x_all, mod_p, mod_s, ng, layer, w_r, b_r):
    return pl.pallas_call(
        _router_kernel,
        grid=(N_TILES,),
        in_specs=[pl.BlockSpec((TM, D), lambda i: (i, 0))]
        + _unified_mod_specs(layer, (3, 4))
        + [_full((1, D)), _full((ROUTER_ROWS, D)), _full((ROUTER_ROWS, 1))],
        out_specs=[
            pl.BlockSpec((TM, D // 2), lambda i: (i, 0)),
            pl.BlockSpec((META_ROWS, TM), lambda i: (0, i)),
            pl.BlockSpec((TM, LANES), lambda i: (i, 0)),
            pl.BlockSpec((ROUTER_ROWS, 1), lambda i: (0, 0)),
        ],
        out_shape=[
            jax.ShapeDtypeStruct((T_PAD, D // 2), jnp.int32),
            jax.ShapeDtypeStruct((META_ROWS, T_PAD), F32),
            jax.ShapeDtypeStruct((T_PAD, LANES), F32),
            jax.ShapeDtypeStruct((ROUTER_ROWS, 1), jnp.int32),
        ],
        scratch_shapes=[pltpu.VMEM((ROUTER_ROWS, 1), F32), pltpu.VMEM((TM, TM), BF16)],
        compiler_params=_cparams("arbitrary"),
        name="router",
    )(x_all, mod_p, mod_s, mod_p, mod_s, ng, w_r, b_r)


SC_INDEX_MAX = 128
SC_ROWS_BYTES = 400 * 1024


def _gather_chunk(per_worker, row_bytes, buffers=1):
    top = min(SC_INDEX_MAX, SC_ROWS_BYTES // (row_bytes * buffers)) // SUBLANES * SUBLANES
    for ch in range(top, SUBLANES - 1, -SUBLANES):
        if per_worker % ch == 0:
            return ch
    raise ValueError(per_worker)


def _sc_gather(table, idx):
    nc, nw = SC_CORES, SC_CORES * SC_SUBCORES
    m = idx.shape[0]
    per_w = m // nw
    assert per_w * nw == m
    width = table.shape[1]
    ch = _gather_chunk(per_w, width * table.dtype.itemsize, buffers=2)
    mesh = plsc.VectorSubcoreMesh(core_axis_name="c", subcore_axis_name="s")

    n_chunks = per_w // ch

    @functools.partial(
        pl.kernel,
        out_type=jax.ShapeDtypeStruct((m, width), table.dtype),
        mesh=mesh,
        scratch_types=[pltpu.VMEM((ch,), jnp.int32)] * 2
        + [pltpu.VMEM((ch, width), table.dtype)] * 2
        + [pltpu.SemaphoreType.DMA] * 4,
    )
    def gather_kernel(t_hbm, i_hbm, o_hbm, idx_a, idx_b, rows_a, rows_b, sg_a, sg_b, sw_a, sw_b):
        wid = lax.axis_index("s") * nc + lax.axis_index("c")
        base = wid * per_w
        idx_v, rows_v, sem_g, sem_w = (idx_a, idx_b), (rows_a, rows_b), (sg_a, sg_b), (sw_a, sw_b)

        def out_rows(j):
            return o_hbm.at[pl.ds(pl.multiple_of(base + j * ch, 8), ch)]

        gathers, writes = {}, {}
        for j in range(n_chunks):
            b = j % 2
            if j >= 2:
                writes[j - 2].wait()
            pltpu.sync_copy(i_hbm.at[pl.ds(pl.multiple_of(base + j * ch, 8), ch)], idx_v[b])
            gathers[j] = pltpu.async_copy(t_hbm.at[idx_v[b]], rows_v[b], sem_g[b])
            if j >= 1:
                gathers[j - 1].wait()
                writes[j - 1] = pltpu.async_copy(rows_v[1 - b], out_rows(j - 1), sem_w[1 - b])
        last = n_chunks - 1
        gathers[last].wait()
        writes[last] = pltpu.async_copy(rows_v[last % 2], out_rows(last), sem_w[last % 2])
        for j in range(max(last - 1, 0), n_chunks):
            writes[j].wait()

    return gather_kernel(table, idx)


def _sc_scatter2(rows, dest0, dest1, n_out):
    nc, nw = SC_CORES, SC_CORES * SC_SUBCORES
    n, width = rows.shape
    per_w = n // nw
    assert per_w * nw == n
    ch = _gather_chunk(per_w, width * rows.dtype.itemsize, buffers=2)
    n_chunks = per_w // ch
    mesh = plsc.VectorSubcoreMesh(core_axis_name="c", subcore_axis_name="s")

    @functools.partial(
        pl.kernel,
        out_type=jax.ShapeDtypeStruct((n_out, width), rows.dtype),
        mesh=mesh,
        scratch_types=[pltpu.VMEM((ch,), jnp.int32)] * 4
        + [pltpu.VMEM((ch, width), rows.dtype)] * 2
        + [pltpu.SemaphoreType.DMA] * 4,
    )
    def scatter_kernel(r_hbm, d0_hbm, d1_hbm, o_hbm, i0_a, i0_b, i1_a, i1_b, rows_a, rows_b,
                       s0_a, s0_b, s1_a, s1_b):
        wid = lax.axis_index("s") * nc + lax.axis_index("c")
        base = wid * per_w
        i0_v, i1_v, rows_v = (i0_a, i0_b), (i1_a, i1_b), (rows_a, rows_b)
        sem0, sem1 = (s0_a, s0_b), (s1_a, s1_b)

        scatters = {}
        for j in range(n_chunks):
            b = j % 2
            if j >= 2:
                for cp in scatters[j - 2]:
                    cp.wait()
            off = pl.multiple_of(base + j * ch, 8)
            pltpu.sync_copy(d0_hbm.at[pl.ds(off, ch)], i0_v[b])
            pltpu.sync_copy(d1_hbm.at[pl.ds(off, ch)], i1_v[b])
            pltpu.sync_copy(r_hbm.at[pl.ds(off, ch)], rows_v[b])
            scatters[j] = (
                pltpu.async_copy(rows_v[b], o_hbm.at[i0_v[b]], sem0[b]),
                pltpu.async_copy(rows_v[b], o_hbm.at[i1_v[b]], sem1[b]),
            )
        for j in range(max(n_chunks - 2, 0), n_chunks):
            for cp in scatters[j]:
                cp.wait()

    return scatter_kernel(rows, dest0, dest1)


def _expert_kernel(be_ref, first_ref, nvalid_ref, next_ref, nused_ref, x_ref, wg_hbm, wu_hbm,
                   wd_hbm, y_ref, wgf_ref, wuf_ref, wdf_ref, wgb_ref, wub_ref, wdb_ref, sems,
                   *, layer):
    b = pl.program_id(0)

    def weight_copies(e):
        return (
            pltpu.make_async_copy(wg_hbm.at[layer, e], wgf_ref, sems.at[0]),
            pltpu.make_async_copy(wu_hbm.at[layer, e], wuf_ref, sems.at[1]),
            pltpu.make_async_copy(wd_hbm.at[layer, e], wdf_ref, sems.at[2]),
        )

    @pl.when(b == 0)
    def _():
        for cp in weight_copies(be_ref[0]):
            cp.start()

    @pl.when(b < nused_ref[0])
    def _():
        @pl.when(first_ref[b] == 1)
        def _():
            for cp in weight_copies(be_ref[b]):
                cp.wait()
            wgb_ref[...] = wgf_ref[...].astype(BF16)
            wub_ref[...] = wuf_ref[...].astype(BF16)
            wdb_ref[...] = wdf_ref[...].astype(BF16)

            @pl.when(next_ref[b] >= 0)
            def _():
                for cp in weight_copies(next_ref[b]):
                    cp.start()

        def run(rows):
            live = lax.broadcasted_iota(jnp.int32, (rows, 1), 0) < nvalid_ref[b]
            x = _unpack_bf16_pairs(jnp.where(live, x_ref[0:rows, :], 0), BF16)
            g = _dot(x, wgb_ref[...])
            u = _dot(x, wub_ref[...])
            hmid = (g * jax.nn.sigmoid(g)) * u
            y_ref[0:rows, :] = _pack_bf16_pairs(_dot(hmid.astype(BF16), wdb_ref[...]))

        half = BM // 2

        @pl.when(nvalid_ref[b] > half)
        def _():
            run(BM)

        @pl.when(nvalid_ref[b] <= half)
        def _():
            run(half)
            y_ref[half:BM, :] = jnp.zeros((BM - half, D // 2), jnp.int32)

    @pl.when(b >= nused_ref[0])
    def _():
        y_ref[...] = jnp.zeros((BM, D // 2), jnp.int32)


def _experts(x_sorted, block_expert, first, nvalid, next_expert, nused, layer, w_gate, w_up,
             w_down):
    def row_map(b, be, fi, nv, nx, nu):
        return (jnp.minimum(b, nu[0] - 1), 0)

    def out_map(b, be, fi, nv, nx, nu):
        return (b, 0)

    grid_spec = pltpu.PrefetchScalarGridSpec(
        num_scalar_prefetch=5,
        grid=(N_BLOCKS,),
        in_specs=[
            pl.BlockSpec((BM, D // 2), row_map),
            pl.BlockSpec(memory_space=pl.ANY),
            pl.BlockSpec(memory_space=pl.ANY),
            pl.BlockSpec(memory_space=pl.ANY),
        ],
        out_specs=pl.BlockSpec((BM, D // 2), out_map),
        scratch_shapes=[
            pltpu.VMEM((D, D_EXPERT), F32),
            pltpu.VMEM((D, D_EXPERT), F32),
            pltpu.VMEM((D_EXPERT, D), F32),
            pltpu.VMEM((D, D_EXPERT), BF16),
            pltpu.VMEM((D, D_EXPERT), BF16),
            pltpu.VMEM((D_EXPERT, D), BF16),
            pltpu.SemaphoreType.DMA((3,)),
        ],
    )
    return pl.pallas_call(
        functools.partial(_expert_kernel, layer=layer),
        grid_spec=grid_spec,
        out_shape=jax.ShapeDtypeStruct((P_ROWS, D // 2), jnp.int32),
        compiler_params=_cparams("arbitrary"),
        name="experts",
    )(block_expert, first, nvalid, next_expert, nused, x_sorted, w_gate, w_up, w_down)


def _combine_final_kernel(x_ref, g0_ref, g1_ref, wts_ref, gtp_ref, gts_ref, fg_ref, yp_ref,
                          ys_ref):
    i = pl.program_id(0)
    gate = _tile_mod(i, gtp_ref, gts_ref)
    y = _rms(_plus_moe(x_ref[...], g0_ref, g1_ref, wts_ref, gate), fg_ref[...])

    @pl.when(i < N_PROMPT_TILES)
    def _():
        yp_ref[...] = y

    @pl.when(i == N_PROMPT_TILES)
    def _():
        ys_ref[...] = y[0:DEC_BATCH, :]


def _combine_final(x_all, gathered, wts, mod_p, mod_s, layer, final_g):
    in_specs = [
        pl.BlockSpec((TM, D), lambda i: (i, 0)),
        pl.BlockSpec((TM, D // 2), lambda i: (i, 0)),
        pl.BlockSpec((TM, D // 2), lambda i: (i + N_TILES, 0)),
        pl.BlockSpec((TM, LANES), lambda i: (i, 0)),
    ] + _unified_mod_specs(layer, (5,))
    return pl.pallas_call(
        _combine_final_kernel,
        grid=(N_PROMPT_TILES + 1,),
        in_specs=in_specs + [_full((1, D))],
        out_specs=[
            pl.BlockSpec((TM, D), lambda i: (jnp.minimum(i, N_PROMPT_TILES - 1), 0)),
            pl.BlockSpec((DEC_BATCH, D), lambda i: (0, 0)),
        ],
        out_shape=[
            jax.ShapeDtypeStruct((T_PROMPT, D), F32),
            jax.ShapeDtypeStruct((DEC_BATCH, D), F32),
        ],
        compiler_params=_cparams("arbitrary"),
        name="combine_final",
    )(x_all, gathered, gathered, wts, mod_p, mod_s, final_g)


def _plan_kernel(cnt_ref, meta_ref, d0_ref, d1_ref, be_ref, first_ref, nvalid_ref, next_ref,
                 nused_ref, pstart_ref, nxt_ref):
    shift = BM.bit_length() - 1
    assert 1 << shift == BM

    def count(e):
        return cnt_ref[EXPERT_LANE0 + e, 0]

    def scan_back(k, nxt):
        e = N_EXPERTS - 1 - k
        nxt_ref[e] = nxt
        return jnp.where(count(e) > 0, e, nxt)

    lax.fori_loop(0, N_EXPERTS, scan_back, jnp.int32(-1))

    def fill(e, blk):
        c = count(e)
        pstart_ref[e] = blk << shift

        def one(j, carry):
            be_ref[blk + j] = e
            first_ref[blk + j] = jnp.where(j == 0, 1, 0)
            nvalid_ref[blk + j] = jnp.minimum(c - (j << shift), BM)
            next_ref[blk + j] = nxt_ref[e]
            return carry

        n_blk = (c + (BM - 1)) >> shift
        lax.fori_loop(0, n_blk, one, 0)
        return blk + n_blk

    used = lax.fori_loop(0, N_EXPERTS, fill, jnp.int32(0))
    nused_ref[0] = used

    def tail(b, carry):
        be_ref[b] = 0
        first_ref[b] = 0
        nvalid_ref[b] = 0
        next_ref[b] = -1
        return carry

    lax.fori_loop(used, N_BLOCKS, tail, 0)

    m = meta_ref[...]
    expert = m[0:2, :]
    dest = m[2:4, :]
    for e in range(N_EXPERTS):
        dest = dest + jnp.where(expert == float(e), pstart_ref[e].astype(F32), 0.0)
    dest = dest.astype(jnp.int32)
    d0_ref[...] = dest[0:1, :]
    d1_ref[...] = dest[1:2, :]


def _plan(cnt, meta):
    smem = pl.BlockSpec(memory_space=pltpu.SMEM)
    blocks = jax.ShapeDtypeStruct((N_BLOCKS,), jnp.int32)
    return pl.pallas_call(
        _plan_kernel,
        grid=(1,),
        in_specs=[smem, _full((META_ROWS, T_PAD))],
        out_specs=[_full((1, T_PAD)), _full((1, T_PAD)), smem, smem, smem, smem, smem],
        out_shape=[
            jax.ShapeDtypeStruct((1, T_PAD), jnp.int32),
            jax.ShapeDtypeStruct((1, T_PAD), jnp.int32),
            blocks, blocks, blocks, blocks,
            jax.ShapeDtypeStruct((1,), jnp.int32),
        ],
        scratch_shapes=[pltpu.SMEM((N_EXPERTS,), jnp.int32), pltpu.SMEM((N_EXPERTS,), jnp.int32)],
        compiler_params=_cparams("arbitrary"),
        name="plan",
    )(cnt, meta)


def _moe(x_all, mod_p, mod_s, ng, layer, w_r, b_r, w_gate, w_up, w_down):
    h2, meta, wts, cnt = _router(x_all, mod_p, mod_s, ng, layer, w_r, b_r)
    dest0, dest1, block_expert, first, nvalid, next_expert, nused = _plan(cnt, meta)
    dest0, dest1 = dest0.reshape(T_PAD), dest1.reshape(T_PAD)
    x_sorted = _sc_scatter2(h2, dest0, dest1, P_ROWS)
    y_sorted = _experts(x_sorted, block_expert, first, nvalid, next_expert, nused, layer,
                        w_gate, w_up, w_down)
    gathered = _sc_gather(y_sorted, jnp.concatenate([dest0, dest1]))
    return gathered, wts


def kernel(x_prompt, x_sample, c_prompt, c_sample, state_pool, state_conv, ada_w, ada_b, norm_g, final_g, a_w_in, a_ln_g, a_ln_b, a_w_s, a_b_s, a_w_out, b_w_in, b_w_grp, b_b_grp, b_scale, b_w_out, c_w_in, c_b_in, c_w_dw, c_b_dw, c_ln_g, c_ln_b, c_w_out, c_b_out, moe_w_grp, moe_b_grp, moe_w_exp, moe_b_exp, moe_w_gate, moe_w_up, moe_w_down):
    x_all = (x_prompt.reshape(T_PROMPT, D), x_sample.reshape(DEC_BATCH, D))
    mods = _ada_mods(jnp.concatenate([c_prompt, c_sample], axis=0), ada_w, ada_b)
    mod_p = mods[:, :BATCH].reshape(DEPTH, BATCH, 1, 6 * D)
    mod_s = mods[:, BATCH:]

    a_w_in_b, a_w_out_b = a_w_in.astype(BF16), a_w_out.astype(BF16)
    b_w_in_b, b_w_grp_b, b_w_out_b = b_w_in.astype(BF16), b_w_grp.astype(BF16), b_w_out.astype(BF16)
    c_w_in_b, c_w_out_b = c_w_in.astype(BF16), c_w_out.astype(BF16)
    pad_rows = ROUTER_ROWS - N_GROUPS - N_EXPERTS
    w_r = jnp.pad(jnp.swapaxes(jnp.concatenate([moe_w_grp, moe_w_exp], axis=2), 1, 2),
                  ((0, 0), (0, pad_rows), (0, 0))).astype(BF16)
    b_r = jnp.pad(jnp.concatenate([moe_b_grp, moe_b_exp], axis=1), ((0, 0), (0, pad_rows)))

    new_a_p, new_a_s, new_b_p, new_b_s, new_c_p, new_c_s = [], [], [], [], [], []
    ia = ib = ic = 0
    pending = None
    for layer in range(DEPTH):
        ng1 = norm_g[layer, 0].reshape(1, D)
        ng2 = norm_g[layer, 1].reshape(1, D)
        kind = layer % 3
        if kind == 0:
            x_all, st_p, st_s = _mixer_a(
                x_all, pending, mod_p, mod_s, ng1, layer, ia, a_w_in_b, a_ln_g.reshape(-1, 1, D),
                a_ln_b.reshape(-1, 1, D), a_w_s, a_b_s, a_w_out_b)
            new_a_p.append(st_p)
            new_a_s.append(st_s)
            ia += 1
        elif kind == 1:
            x_all, st_p, st_s = _mixer_b(
                x_all, pending, mod_p, mod_s, ng1, layer, ib, state_pool, b_w_in_b, b_w_grp_b,
                b_b_grp.reshape(-1, len(POOL_WINDOWS), 1, B_GROUP_DIM),
                b_scale.reshape(-1, 1, D), b_w_out_b)
            new_b_p.append(st_p)
            new_b_s.append(st_s)
            ib += 1
        else:
            x_all, st_p, st_s = _mixer_c(
                x_all, pending, mod_p, mod_s, ng1, layer, ic, state_conv, c_w_in_b,
                c_b_in.reshape(-1, 1, 2 * D), c_w_dw, c_b_dw.reshape(-1, 1, D),
                c_ln_g.reshape(-1, 1, D), c_ln_b.reshape(-1, 1, D), c_w_out_b,
                c_b_out.reshape(-1, 1, D))
            new_c_p.append(st_p)
            new_c_s.append(st_s)
            ic += 1
        pending = _moe(x_all, mod_p, mod_s, ng2, layer, w_r[layer],
                       b_r[layer].reshape(ROUTER_ROWS, 1), moe_w_gate, moe_w_up, moe_w_down)

    y_p, y_s = _combine_final(x_all, *pending, mod_p, mod_s, DEPTH - 1, final_g.reshape(1, D))
    return (y_p.reshape(BATCH, SEQ, D), y_s.reshape(DEC_BATCH, 1, D),
            jnp.stack(new_a_p), jnp.stack(new_a_s), jnp.stack(new_b_p), jnp.stack(new_b_s),
            jnp.stack(new_c_p), jnp.stack(new_c_s))
```

```python
import functools

import jax
import jax.numpy as jnp
from jax import lax
from jax.experimental import pallas as pl
from jax.experimental.pallas import tpu as pltpu
from jax.experimental.pallas import tpu_sc as plsc

F32 = jnp.float32
BF16 = jnp.bfloat16

D = 1024
BATCH = 8
SEQ = 2048
DEPTH = 4
DEC_BATCH = 128
PAST_LEN = 16384
CHUNK = 128
A_HEADS = 8
POOL_WINDOWS = (2, 4, 8, 16)
B_GROUP_DIM = D // len(POOL_WINDOWS)
POOL_BUF = max(POOL_WINDOWS) - 1
CONV_WIDTH = 31
CONV_BUF = CONV_WIDTH - 1
N_GROUPS = 4
EXPERTS_PER_GROUP = 8
N_EXPERTS = N_GROUPS * EXPERTS_PER_GROUP
D_EXPERT = D // 2
EPS = 1e-6

LANES = 128
SUBLANES = 8
CONV_ROWS = 128
SC_CORES = 2
SC_SUBCORES = 16
TM = 512
TILES_PER_SEQ = SEQ // TM
T_PROMPT = BATCH * SEQ
T_ALL = T_PROMPT + DEC_BATCH
N_TILES = -(-T_ALL // TM)
T_PAD = N_TILES * TM
N_PROMPT_TILES = T_PROMPT // TM
SAMPLE_ROWS = 32
BM = 512
N_BLOCKS = -(-(2 * T_PAD) // BM) + N_EXPERTS
P_ROWS = N_BLOCKS * BM
META_ROWS = 8
POOL_CARRY = 16
CONV_CARRY = 32
EXPERT_LANE0 = N_GROUPS
ROUTER_ROWS = 48
VMEM_LIMIT = 56 * 1024 * 1024


def _cparams(*sem):
    return pltpu.CompilerParams(dimension_semantics=sem, vmem_limit_bytes=VMEM_LIMIT)


def _rms(x, g):
    return x * lax.rsqrt(jnp.mean(x * x, axis=-1, keepdims=True) + EPS) * g


def _ln(x, g, b):
    mu = jnp.mean(x, axis=-1, keepdims=True)
    xc = x - mu
    var = jnp.mean(xc * xc, axis=-1, keepdims=True)
    return xc * lax.rsqrt(var + EPS) * g + b


def _dot(a, b):
    return jnp.dot(a, b, preferred_element_type=F32)


def _pack_bf16_pairs(v):
    w = v.shape[1] // 2
    lo = lax.bitcast_convert_type(v[:, :w].astype(BF16).astype(F32), jnp.uint32)
    hi = lax.bitcast_convert_type(v[:, w:].astype(BF16).astype(F32), jnp.uint32)
    return lax.bitcast_convert_type(lax.shift_right_logical(lo, jnp.uint32(16)) | hi, jnp.int32)


def _unpack_bf16_pairs(words, dtype):
    u = lax.bitcast_convert_type(words, jnp.uint32)
    lo = lax.bitcast_convert_type(lax.shift_left(u, jnp.uint32(16)), F32)
    hi = lax.bitcast_convert_type(u & jnp.uint32(0xFFFF0000), F32)
    return jnp.concatenate([lo.astype(dtype), hi.astype(dtype)], axis=1)


def _ada_kernel(c_ref, w_ref, b_ref, op_ref, os_ref):
    c = c_ref[...]
    s = (c * jax.nn.sigmoid(c)).astype(BF16)
    mod = _dot(s, w_ref[...].astype(BF16)) + b_ref[...]
    op_ref[...] = mod[0:BATCH, :]
    os_ref[...] = mod[BATCH:, :]


def _ada_mods(c_all, ada_w, ada_b):
    n = c_all.shape[0]
    assert n == BATCH + DEC_BATCH and BATCH % SUBLANES == 0
    tn = 2048
    return pl.pallas_call(
        _ada_kernel,
        grid=(DEPTH, 6 * D // tn),
        in_specs=[
            pl.BlockSpec((n, D), lambda l, j: (0, 0)),
            pl.BlockSpec((None, D, tn), lambda l, j: (l, 0, j)),
            pl.BlockSpec((None, 1, tn), lambda l, j: (l, 0, j)),
        ],
        out_specs=[
            pl.BlockSpec((None, BATCH, tn), lambda l, j: (l, 0, j)),
            pl.BlockSpec((None, DEC_BATCH, tn), lambda l, j: (l, 0, j)),
        ],
        out_shape=[
            jax.ShapeDtypeStruct((DEPTH, BATCH, 6 * D), F32),
            jax.ShapeDtypeStruct((DEPTH, DEC_BATCH, 6 * D), F32),
        ],
        compiler_params=_cparams("parallel", "parallel"),
        name="ada_mods",
    )(c_all, ada_w, ada_b.reshape(DEPTH, 1, 6 * D))


def _mod_specs_prompt(layer, chunks):
    return [
        pl.BlockSpec((None, None, 1, D),
                     lambda i, c=c: (layer, jnp.minimum(i // TILES_PER_SEQ, BATCH - 1), 0, c))
        for c in chunks
    ]


def _mod_specs_sample(layer, chunks, rows, row_block0=0):
    return [
        pl.BlockSpec((None, rows, D), lambda i, c=c: (layer, i + row_block0, c)) for c in chunks
    ]


def _full(shape):
    nd = len(shape)
    return pl.BlockSpec(shape, lambda i: (0,) * nd)


def _layer_block(shape, layer):
    nd = len(shape)
    return pl.BlockSpec((None,) + shape, lambda i: (layer,) + (0,) * nd)


def _plus_moe(x, g0_ref, g1_ref, wts_ref, gate):
    w = wts_ref[...]
    y0 = _unpack_bf16_pairs(g0_ref[...], F32)
    y1 = _unpack_bf16_pairs(g1_ref[...], F32)
    return x + gate * (w[:, 0:1] * y0 + w[:, 1:2] * y1)


def _residual_in(refs, pending):
    if not pending:
        return refs[0][...], refs[1:]
    x_ref, g0_ref, g1_ref, wts_ref, gate_ref = refs[:5]
    return _plus_moe(x_ref[...], g0_ref, g1_ref, wts_ref, gate_ref[...]), refs[5:]


def _pending_prompt(pending, mod_p, layer):
    if pending is None:
        return [], []
    gathered, wts = pending
    specs = [
        pl.BlockSpec((TM, D // 2), lambda i: (i, 0)),
        pl.BlockSpec((TM, D // 2), lambda i: (i + N_TILES, 0)),
        pl.BlockSpec((TM, LANES), lambda i: (i, 0)),
    ] + _mod_specs_prompt(layer - 1, (5,))
    return specs, [gathered, gathered, wts, mod_p]


def _pending_sample(pending, mod_s, layer, rows):
    if pending is None:
        return [], []
    gathered, wts = pending
    first = T_PROMPT // rows
    second = (T_PAD + T_PROMPT) // rows
    specs = [
        pl.BlockSpec((rows, D // 2), lambda i: (first + i, 0)),
        pl.BlockSpec((rows, D // 2), lambda i: (second + i, 0)),
        pl.BlockSpec((rows, LANES), lambda i: (first + i, 0)),
    ] + _mod_specs_sample(layer - 1, (5,), rows)
    return specs, [gathered, gathered, wts, mod_s]


def _mix_a_front(x, sh, sc, ng, win_ref, lng_ref, lnb_ref):
    h = (_rms(x, ng) * (1.0 + sc) + sh).astype(BF16)
    z = jax.nn.gelu(_dot(h, win_ref[...]))
    u = z[:, :D]
    v = _ln(z[:, D:], lng_ref[...], lnb_ref[...])
    return u, v


def _mix_a_prompt_kernel(*refs, pending, first, stack):
    x, refs = _residual_in(refs, pending)
    (sh_ref, sc_ref, gt_ref, ng_ref, win_ref, lng_ref, lnb_ref, ws_ref, bs_ref, wout_ref,
     *cv_prev_ref, x1_ref, cv_ref, mixed_ref) = refs
    i = pl.program_id(0)
    u, v = _mix_a_front(x, sh_ref[...], sc_ref[...], ng_ref[...], win_ref, lng_ref, lnb_ref)
    vb = v.astype(BF16)
    n_chunks = TM // CHUNK
    row = lax.broadcasted_iota(jnp.int32, (CHUNK, CHUNK), 0)
    col = lax.broadcasted_iota(jnp.int32, (CHUNK, CHUNK), 1)
    tril = row >= col
    hd_dim = D // A_HEADS
    for hd in range(A_HEADS):
        ws = jnp.where(tril, ws_ref[hd], 0.0).astype(BF16)
        cols = slice(hd * hd_dim, (hd + 1) * hd_dim)
        vcat = jnp.concatenate([vb[c * CHUNK:(c + 1) * CHUNK, cols] for c in range(n_chunks)], axis=1)
        m = _dot(ws, vcat)
        for c in range(n_chunks):
            mixed_ref[c * CHUNK:(c + 1) * CHUNK, cols] = m[:, c * hd_dim:(c + 1) * hd_dim] + bs_ref[hd]
    y = _dot((u * mixed_ref[...]).astype(BF16), wout_ref[...])
    out = x + gt_ref[...] * y
    if first:
        out = jnp.where(i < N_PROMPT_TILES, out, 0.0)
    x1_ref[...] = out

    @pl.when(i % TILES_PER_SEQ == TILES_PER_SEQ - 1)
    def _():
        if stack:
            cv_ref[0] = cv_prev_ref[0][...]
            cv_ref[1] = v[TM - CHUNK:, :]
        else:
            cv_ref[...] = v[TM - CHUNK:, :]


def _mix_a_sample_kernel(*refs, pending, first):
    if first:
        refs = refs[:1] + refs[2:]
    x, refs = _residual_in(refs, pending)
    (sh_ref, sc_ref, gt_ref, ng_ref, win_ref, lng_ref, lnb_ref, wsd_ref, bsd_ref, wout_ref,
     x1_ref, cv_ref) = refs
    u, v = _mix_a_front(x, sh_ref[...], sc_ref[...], ng_ref[...], win_ref, lng_ref, lnb_ref)
    mixed = wsd_ref[...] * v + bsd_ref[...]
    y = _dot((u * mixed).astype(BF16), wout_ref[...])
    x1_ref[...] = x + gt_ref[...] * y
    cv_ref[...] = v


def _mixer_a(x_in, pending, cv_prev, mod_p, mod_s, ng, layer, la, w_in, ln_g, ln_b, w_s, b_s,
             w_out):
    first = isinstance(x_in, tuple)
    stack = cv_prev is not None
    assert not stack or la == 1
    x_all = x_in[0] if first else x_in
    bs_b = jnp.broadcast_to(b_s[la][:, :, None], (A_HEADS, CHUNK, D // A_HEADS))
    common = [
        _layer_block((D, 2 * D), la), _layer_block((1, D), la), _layer_block((1, D), la),
    ]
    has_pending = pending is not None
    pend_specs, pend_args = _pending_prompt(pending, mod_p, layer)
    last_tile = N_PROMPT_TILES - 1

    def seq_of(i):
        return jnp.minimum(i, last_tile) // TILES_PER_SEQ

    if stack:
        cv_in_specs = [pl.BlockSpec((None, CHUNK, D), lambda i: (seq_of(i), 0, 0))]
        cv_in_args = [cv_prev]
        cv_spec = pl.BlockSpec((2, None, CHUNK, D), lambda i: (0, seq_of(i), 0, 0))
        cv_shape = jax.ShapeDtypeStruct((2, BATCH, CHUNK, D), F32)
    else:
        cv_in_specs, cv_in_args = [], []
        cv_spec = pl.BlockSpec((None, CHUNK, D), lambda i: (seq_of(i), 0, 0))
        cv_shape = jax.ShapeDtypeStruct((BATCH, CHUNK, D), F32)
    x_all, cv_p = pl.pallas_call(
        functools.partial(_mix_a_prompt_kernel, pending=has_pending, first=first, stack=stack),
        grid=(N_TILES if first else N_PROMPT_TILES,),
        in_specs=[pl.BlockSpec((TM, D), lambda i: (jnp.minimum(i, last_tile), 0))] + pend_specs
        + _mod_specs_prompt(layer, (0, 1, 2))
        + [_full((1, D))] + common
        + [_layer_block((A_HEADS, CHUNK, CHUNK), la), _full((A_HEADS, CHUNK, D // A_HEADS)),
           _layer_block((D, D), la)] + cv_in_specs,
        out_specs=[pl.BlockSpec((TM, D), lambda i: (i, 0)), cv_spec],
        out_shape=[jax.ShapeDtypeStruct((T_PAD, D), F32), cv_shape],
        scratch_shapes=[pltpu.VMEM((TM, D), F32)],
        input_output_aliases={} if first else {0: 0},
        compiler_params=_cparams("arbitrary"),
        name="mixer_a_prompt",
    )(x_all, *pend_args, mod_p, mod_p, mod_p, ng, w_in, ln_g, ln_b, w_s, bs_b, w_out,
      *cv_in_args)

    wsd = jnp.repeat(w_s[la, :, 0, 0], D // A_HEADS).reshape(1, D)
    bsd = jnp.repeat(b_s[la, :, 0], D // A_HEADS).reshape(1, D)
    sblk = T_PROMPT // DEC_BATCH
    pend_specs, pend_args = _pending_sample(pending, mod_s, layer, DEC_BATCH)
    if first:
        x_specs = [pl.BlockSpec((DEC_BATCH, D), lambda i: (0, 0)), pl.BlockSpec(memory_space=pl.ANY)]
        x_args = [x_in[1], x_all]
    else:
        x_specs = [pl.BlockSpec((DEC_BATCH, D), lambda i: (sblk, 0))]
        x_args = [x_all]
    x_all, cv_s = pl.pallas_call(
        functools.partial(_mix_a_sample_kernel, pending=has_pending, first=first),
        grid=(1,),
        in_specs=x_specs + pend_specs
        + _mod_specs_sample(layer, (0, 1, 2), DEC_BATCH)
        + [_full((1, D))] + common
        + [_full((1, D)), _full((1, D)), _layer_block((D, D), la)],
        out_specs=[
            pl.BlockSpec((DEC_BATCH, D), lambda i: (sblk, 0)),
            pl.BlockSpec((DEC_BATCH, D), lambda i: (0, 0)),
        ],
        out_shape=[
            jax.ShapeDtypeStruct((T_PAD, D), F32),
            jax.ShapeDtypeStruct((DEC_BATCH, D), F32),
        ],
        input_output_aliases={len(x_args) - 1: 0},
        compiler_params=_cparams("arbitrary"),
        name="mixer_a_sample",
    )(*x_args, *pend_args, mod_s, mod_s, mod_s, ng, w_in, ln_g, ln_b, wsd, bsd, w_out)
    return x_all, cv_p, cv_s.reshape(DEC_BATCH, 1, D)


def _mix_b_tail(pooled_groups, wgrp_ref, bgrp_ref, scale_ref, wout_ref):
    outs = [
        _dot(pg.astype(BF16), wgrp_ref[g]) + bgrp_ref[g]
        for g, pg in enumerate(pooled_groups)
    ]
    mixed = jnp.concatenate(outs, axis=1) * scale_ref[...]
    return _dot(mixed.astype(BF16), wout_ref[...])


def _mix_b_prompt_kernel(*refs, pending):
    x, refs = _residual_in(refs, pending)
    (sh_ref, sc_ref, gt_ref, ng_ref, win_ref, wgrp_ref, bgrp_ref, scale_ref, wout_ref,
     x1_ref, st_ref, full_ref, s2_ref, s4_ref, s8_ref) = refs
    assert POOL_WINDOWS == (2, 4, 8, 16)
    i = pl.program_id(0)
    j = i % TILES_PER_SEQ
    h = (_rms(x, ng_ref[...]) * (1.0 + sc_ref[...]) + sh_ref[...]).astype(BF16)
    p = _dot(h, win_ref[...])
    gd = B_GROUP_DIM
    top = SUBLANES
    cur = top + POOL_CARRY
    rows = TM + POOL_CARRY

    @pl.when(i == 0)
    def _():
        for ref in (full_ref, s2_ref, s4_ref, s8_ref):
            ref[0:top, :] = jnp.zeros((top, ref.shape[1]), F32)

    full_ref[top:cur, :] = jnp.where(j == 0, 0.0, full_ref[top:cur, :])
    full_ref[cur:cur + TM, :] = p
    s2_ref[top:top + rows, :] = full_ref[top:top + rows, :] + full_ref[pl.ds(top - 1, rows), :]
    s4_ref[top:top + rows, :] = s2_ref[top:top + rows, gd:] + s2_ref[pl.ds(top - 2, rows), gd:]
    s8_ref[top:top + rows, :] = s4_ref[top:top + rows, gd:] + s4_ref[pl.ds(top - 4, rows), gd:]
    s16 = s8_ref[cur:cur + TM, gd:] + s8_ref[pl.ds(cur - 8, TM), gd:]
    sums = [s2_ref[cur:cur + TM, 0:gd], s4_ref[cur:cur + TM, 0:gd], s8_ref[cur:cur + TM, 0:gd], s16]
    pos = j * TM + lax.broadcasted_iota(jnp.int32, (TM, 1), 0)
    pooled = []
    for g, w in enumerate(POOL_WINDOWS):
        cnt = jnp.minimum(w, pos + 1).astype(F32)
        pooled.append(sums[g] / cnt - p[:, g * gd:(g + 1) * gd])
    y = _mix_b_tail(pooled, wgrp_ref, bgrp_ref, scale_ref, wout_ref)
    x1_ref[...] = x + gt_ref[...] * y

    @pl.when(j == TILES_PER_SEQ - 1)
    def _():
        st_ref[...] = full_ref[pl.ds(cur + TM - POOL_BUF, POOL_BUF), :]

    full_ref[top:cur, :] = full_ref[pl.ds(TM + top, POOL_CARRY), :]


def _mix_b_sample_kernel(*refs, pending):
    x, refs = _residual_in(refs, pending)
    (sh_ref, sc_ref, gt_ref, ng_ref, st_ref, win_ref, wgrp_ref, bgrp_ref, scale_ref, wout_ref,
     x1_ref, nst_ref) = refs
    h = (_rms(x, ng_ref[...]) * (1.0 + sc_ref[...]) + sh_ref[...]).astype(BF16)
    p = _dot(h, win_ref[...])
    pooled = []
    for g, w in enumerate(POOL_WINDOWS):
        s = p[:, g * B_GROUP_DIM:(g + 1) * B_GROUP_DIM]
        for k in range(1, w):
            s = s + st_ref[:, POOL_BUF - k, g * B_GROUP_DIM:(g + 1) * B_GROUP_DIM]
        cnt = float(min(w, PAST_LEN + 1))
        pooled.append(s / cnt - p[:, g * B_GROUP_DIM:(g + 1) * B_GROUP_DIM])
    y = _mix_b_tail(pooled, wgrp_ref, bgrp_ref, scale_ref, wout_ref)
    x1_ref[...] = x + gt_ref[...] * y
    for k in range(1, POOL_BUF):
        nst_ref[k - 1] = st_ref[:, k, :]
    nst_ref[POOL_BUF - 1] = p


def _mixer_b(x_all, pending, mod_p, mod_s, ng, layer, lb, state, w_in, w_grp, b_grp, scale,
             w_out):
    gd = B_GROUP_DIM
    common = [
        _layer_block((D, D), lb), _layer_block((len(POOL_WINDOWS), gd, gd), lb),
        _layer_block((len(POOL_WINDOWS), 1, gd), lb), _layer_block((1, D), lb),
        _layer_block((D, D), lb),
    ]
    has_pending = pending is not None
    pend_specs, pend_args = _pending_prompt(pending, mod_p, layer)
    x_all, st_p = pl.pallas_call(
        functools.partial(_mix_b_prompt_kernel, pending=has_pending),
        grid=(N_PROMPT_TILES,),
        in_specs=[pl.BlockSpec((TM, D), lambda i: (i, 0))] + pend_specs
        + _mod_specs_prompt(layer, (0, 1, 2)) + [_full((1, D))] + common,
        out_specs=[
            pl.BlockSpec((TM, D), lambda i: (i, 0)),
            pl.BlockSpec((None, POOL_BUF, D), lambda i: (i // TILES_PER_SEQ, 0, 0)),
        ],
        out_shape=[
            jax.ShapeDtypeStruct((T_PAD, D), F32),
            jax.ShapeDtypeStruct((BATCH, POOL_BUF, D), F32),
        ],
        scratch_shapes=[
            pltpu.VMEM((SUBLANES + POOL_CARRY + TM, D - g * gd), F32) for g in (0, 0, 1, 2)],
        input_output_aliases={0: 0},
        compiler_params=_cparams("arbitrary"),
        name="mixer_b_prompt",
    )(x_all, *pend_args, mod_p, mod_p, mod_p, ng, w_in, w_grp, b_grp, scale, w_out)

    r = SAMPLE_ROWS
    sblk = T_PROMPT // r
    pend_specs, pend_args = _pending_sample(pending, mod_s, layer, r)
    x_all, st_s = pl.pallas_call(
        functools.partial(_mix_b_sample_kernel, pending=has_pending),
        grid=(DEC_BATCH // r,),
        in_specs=[pl.BlockSpec((r, D), lambda i: (sblk + i, 0))] + pend_specs
        + _mod_specs_sample(layer, (0, 1, 2), r) + [_full((1, D))]
        + [pl.BlockSpec((None, r, POOL_BUF, D), lambda i: (lb, i, 0, 0))] + common,
        out_specs=[
            pl.BlockSpec((r, D), lambda i: (sblk + i, 0)),
            pl.BlockSpec((POOL_BUF, r, D), lambda i: (0, i, 0)),
        ],
        out_shape=[
            jax.ShapeDtypeStruct((T_PAD, D), F32),
            jax.ShapeDtypeStruct((POOL_BUF, DEC_BATCH, D), F32),
        ],
        input_output_aliases={0: 0},
        compiler_params=_cparams("arbitrary"),
        name="mixer_b_sample",
    )(x_all, *pend_args, mod_s, mod_s, mod_s, ng, state, w_in, w_grp, b_grp, scale, w_out)
    return x_all, st_p, jnp.transpose(st_s, (1, 0, 2))


def _mix_c_glu(x, sh, sc, ng, win_ref, bin_ref):
    h = (_rms(x, ng) * (1.0 + sc) + sh).astype(BF16)
    ag = _dot(h, win_ref[...]) + bin_ref[...]
    return ag[:, :D] * jax.nn.sigmoid(ag[:, D:])


def _mix_c_tail(conv, lng_ref, lnb_ref, wout_ref, bout_ref):
    z = _ln(conv, lng_ref[...], lnb_ref[...])
    z = z * jax.nn.sigmoid(z)
    return _dot(z.astype(BF16), wout_ref[...]) + bout_ref[...]


def _mix_c_prompt_kernel(*refs, pending):
    x, refs = _residual_in(refs, pending)
    (sh_ref, sc_ref, gt_ref, ng_ref, win_ref, bin_ref, wdw_ref, bdw_ref, lng_ref, lnb_ref,
     wout_ref, bout_ref, x1_ref, st_ref, full_ref, conv_ref, *shift_refs) = refs
    i = pl.program_id(0)
    j = i % TILES_PER_SEQ
    glu = _mix_c_glu(x, sh_ref[...], sc_ref[...], ng_ref[...], win_ref, bin_ref)

    @pl.when(j == 0)
    def _():
        full_ref[0:CONV_CARRY, :] = jnp.zeros((CONV_CARRY, D), F32)

    full_ref[CONV_CARRY:CONV_CARRY + TM, :] = glu
    off = CONV_CARRY - CONV_BUF
    span = TM + (off + CONV_BUF) // SUBLANES * SUBLANES - SUBLANES
    for c in range(D // LANES):
        cols = slice(c * LANES, (c + 1) * LANES)
        shift_ref = shift_refs[c % len(shift_refs)]
        for r in range(1, SUBLANES):
            shift_ref[r - 1] = full_ref[pl.ds(r, span), cols]
        for rb in range(TM // CONV_ROWS):
            acc = None
            for k in range(CONV_WIDTH):
                q, r = divmod(off + k, SUBLANES)
                start = q * SUBLANES + rb * CONV_ROWS
                if r == 0:
                    src = full_ref[pl.ds(start, CONV_ROWS), cols]
                else:
                    src = shift_ref[r - 1, pl.ds(start, CONV_ROWS), :]
                term = src * wdw_ref[k:k + 1, cols]
                acc = term if acc is None else acc + term
            conv_ref[pl.ds(rb * CONV_ROWS, CONV_ROWS), cols] = acc + bdw_ref[:, cols]
    y = _mix_c_tail(conv_ref[...], lng_ref, lnb_ref, wout_ref, bout_ref)
    x1_ref[...] = x + gt_ref[...] * y

    @pl.when(j == TILES_PER_SEQ - 1)
    def _():
        st_ref[...] = full_ref[pl.ds(CONV_CARRY + TM - CONV_BUF, CONV_BUF), :]

    full_ref[0:CONV_CARRY, :] = full_ref[pl.ds(TM, CONV_CARRY), :]


def _mix_c_sample_kernel(*refs, pending):
    x, refs = _residual_in(refs, pending)
    (sh_ref, sc_ref, gt_ref, ng_ref, st_ref, win_ref, bin_ref, wdw_ref, bdw_ref, lng_ref,
     lnb_ref, wout_ref, bout_ref, x1_ref, nst_ref) = refs
    glu = _mix_c_glu(x, sh_ref[...], sc_ref[...], ng_ref[...], win_ref, bin_ref)
    acc = glu * wdw_ref[CONV_BUF:CONV_BUF + 1, :]
    for k in range(CONV_BUF):
        acc = acc + st_ref[:, k, :] * wdw_ref[k:k + 1, :]
    y = _mix_c_tail(acc + bdw_ref[...], lng_ref, lnb_ref, wout_ref, bout_ref)
    x1_ref[...] = x + gt_ref[...] * y
    for k in range(1, CONV_BUF):
        nst_ref[k - 1] = st_ref[:, k, :]
    nst_ref[CONV_BUF - 1] = glu


def _mixer_c(x_all, pending, mod_p, mod_s, ng, layer, lc, state, w_in, b_in, w_dw, b_dw, ln_g,
             ln_b, w_out, b_out):
    common = [
        _layer_block((D, 2 * D), lc), _layer_block((1, 2 * D), lc),
        _layer_block((CONV_WIDTH, D), lc), _layer_block((1, D), lc), _layer_block((1, D), lc),
        _layer_block((1, D), lc), _layer_block((D, D), lc), _layer_block((1, D), lc),
    ]
    has_pending = pending is not None
    pend_specs, pend_args = _pending_prompt(pending, mod_p, layer)
    x_all, st_p = pl.pallas_call(
        functools.partial(_mix_c_prompt_kernel, pending=has_pending),
        grid=(N_PROMPT_TILES,),
        in_specs=[pl.BlockSpec((TM, D), lambda i: (i, 0))] + pend_specs
        + _mod_specs_prompt(layer, (0, 1, 2)) + [_full((1, D))] + common,
        out_specs=[
            pl.BlockSpec((TM, D), lambda i: (i, 0)),
            pl.BlockSpec((None, CONV_BUF, D), lambda i: (i // TILES_PER_SEQ, 0, 0)),
        ],
        out_shape=[
            jax.ShapeDtypeStruct((T_PAD, D), F32),
            jax.ShapeDtypeStruct((BATCH, CONV_BUF, D), F32),
        ],
        scratch_shapes=[pltpu.VMEM((TM + CONV_CARRY, D), F32), pltpu.VMEM((TM, D), F32)]
        + [pltpu.VMEM((SUBLANES - 1, TM + CONV_CARRY - SUBLANES, LANES), F32)] * 2,
        input_output_aliases={0: 0},
        compiler_params=_cparams("arbitrary"),
        name="mixer_c_prompt",
    )(x_all, *pend_args, mod_p, mod_p, mod_p, ng, w_in, b_in, w_dw, b_dw, ln_g, ln_b, w_out,
      b_out)

    r = SAMPLE_ROWS
    sblk = T_PROMPT // r
    pend_specs, pend_args = _pending_sample(pending, mod_s, layer, r)
    x_all, st_s = pl.pallas_call(
        functools.partial(_mix_c_sample_kernel, pending=has_pending),
        grid=(DEC_BATCH // r,),
        in_specs=[pl.BlockSpec((r, D), lambda i: (sblk + i, 0))] + pend_specs
        + _mod_specs_sample(layer, (0, 1, 2), r) + [_full((1, D))]
        + [pl.BlockSpec((None, r, CONV_BUF, D), lambda i: (lc, i, 0, 0))] + common,
        out_specs=[
            pl.BlockSpec((r, D), lambda i: (sblk + i, 0)),
            pl.BlockSpec((CONV_BUF, r, D), lambda i: (0, i, 0)),
        ],
        out_shape=[
            jax.ShapeDtypeStruct((T_PAD, D), F32),
            jax.ShapeDtypeStruct((CONV_BUF, DEC_BATCH, D), F32),
        ],
        input_output_aliases={0: 0},
        compiler_params=_cparams("arbitrary"),
        name="mixer_c_sample",
    )(x_all, *pend_args, mod_s, mod_s, mod_s, ng, state, w_in, b_in, w_dw, b_dw, ln_g, ln_b,
      w_out, b_out)
    return x_all, st_p, jnp.transpose(st_s, (1, 0, 2))


def _tile_mod(i, p_ref, s_ref):
    s_rows = jnp.concatenate([s_ref[...], jnp.zeros((TM - DEC_BATCH, D), F32)], axis=0)
    return jnp.where(i >= N_PROMPT_TILES, s_rows, p_ref[...])


def _unified_mod_specs(layer, chunks):
    specs = []
    for c in chunks:
        specs.append(pl.BlockSpec(
            (None, None, 1, D),
            lambda i, c=c: (layer, jnp.minimum(i // TILES_PER_SEQ, BATCH - 1), 0, c)))
        specs.append(pl.BlockSpec((None, DEC_BATCH, D), lambda i, c=c: (layer, 0, c)))
    return specs


def _router_kernel(x_ref, shp_ref, shs_ref, scp_ref, scs_ref, ng_ref, wr_ref, br_ref,
                   h2_ref, meta_ref, wts_ref, cnt_ref, carry_ref, before_ref):
    i = pl.program_id(0)

    @pl.when(i == 0)
    def _():
        carry_ref[...] = jnp.zeros((ROUTER_ROWS, 1), F32)
        earlier = lax.broadcasted_iota(jnp.int32, (TM, TM), 0)
        token = lax.broadcasted_iota(jnp.int32, (TM, TM), 1)
        before_ref[...] = (earlier < token).astype(BF16)

    sh = _tile_mod(i, shp_ref, shs_ref)
    sc = _tile_mod(i, scp_ref, scs_ref)
    h2 = _rms(x_ref[...], ng_ref[...]) * (1.0 + sc) + sh
    h2_ref[...] = _pack_bf16_pairs(h2)
    lg = lax.dot_general(wr_ref[...], h2.astype(BF16), (((1,), (1,)), ((), ())),
                         preferred_element_type=F32) + br_ref[...]
    row = lax.broadcasted_iota(jnp.int32, (ROUTER_ROWS, TM), 0)
    row_f = row.astype(F32)
    far = float(ROUTER_ROWS)
    neg = -jnp.inf
    is_g = row < N_GROUPS
    gm = jnp.where(is_g, lg, neg)
    gmax = jnp.max(gm, axis=0, keepdims=True)
    gsel = jnp.min(jnp.where(gm == gmax, row_f, far), axis=0, keepdims=True).astype(jnp.int32)
    gsum = jnp.sum(jnp.where(is_g, jnp.exp(lg - gmax), 0.0), axis=0, keepdims=True)
    g_w = 1.0 / gsum
    e_row = row - EXPERT_LANE0
    in_grp = (e_row >= 0) & (e_row < N_EXPERTS) & (
        lax.shift_right_arithmetic(e_row, EXPERTS_PER_GROUP.bit_length() - 1) == gsel)
    em = jnp.where(in_grp, lg, neg)
    m1 = jnp.max(em, axis=0, keepdims=True)
    i1 = jnp.min(jnp.where(em == m1, row_f, far), axis=0, keepdims=True).astype(jnp.int32)
    em2 = jnp.where(row == i1, neg, em)
    m2 = jnp.max(em2, axis=0, keepdims=True)
    i2 = jnp.min(jnp.where(em2 == m2, row_f, far), axis=0, keepdims=True).astype(jnp.int32)
    e2 = jnp.exp(m2 - m1)
    den = 1.0 + e2
    w1 = (1.0 / den) * g_w
    w2 = (e2 / den) * g_w

    hit1 = row == i1
    hit2 = row == i2
    assign = jnp.where(hit1 | hit2, 1.0, 0.0).astype(BF16)
    seen = _dot(assign, before_ref[...]) + carry_ref[...]
    rank1 = jnp.sum(jnp.where(hit1, seen, 0.0), axis=0, keepdims=True)
    rank2 = jnp.sum(jnp.where(hit2, seen, 0.0), axis=0, keepdims=True)
    carry_ref[...] = carry_ref[...] + jnp.sum(assign.astype(F32), axis=1, keepdims=True)
    cnt_ref[...] = carry_ref[...].astype(jnp.int32)

    ex1 = (i1 - EXPERT_LANE0).astype(F32)
    ex2 = (i2 - EXPERT_LANE0).astype(F32)
    field = lax.broadcasted_iota(jnp.int32, (META_ROWS, TM), 0)
    meta_ref[...] = jnp.where(field == 0, ex1, jnp.where(field == 1, ex2,
                              jnp.where(field == 2, rank1, jnp.where(field == 3, rank2, 0.0))))
    slot = lax.broadcasted_iota(jnp.int32, (LANES, TM), 0)
    wts_ref[...] = jnp.where(slot == 0, w1, jnp.where(slot == 1, w2, 0.0)).T


def _router(x_all, mod_p, mod_s, ng, layer, w_r, b_r):
    return pl.pallas_call(
        _router_kernel,
        grid=(N_TILES,),
        in_specs=[pl.BlockSpec((TM, D), lambda i: (i, 0))]
        + _unified_mod_specs(layer, (3, 4))
        + [_full((1, D)), _full((ROUTER_ROWS, D)), _full((ROUTER_ROWS, 1))],
        out_specs=[
            pl.BlockSpec((TM, D // 2), lambda i: (i, 0)),
            pl.BlockSpec((META_ROWS, TM), lambda i: (0, i)),
            pl.BlockSpec((TM, LANES), lambda i: (i, 0)),
            pl.BlockSpec((ROUTER_ROWS, 1), lambda i: (0, 0)),
        ],
        out_shape=[
            jax.ShapeDtypeStruct((T_PAD, D // 2), jnp.int32),
            jax.ShapeDtypeStruct((META_ROWS, T_PAD), F32),
            jax.ShapeDtypeStruct((T_PAD, LANES), F32),
            jax.ShapeDtypeStruct((ROUTER_ROWS, 1), jnp.int32),
        ],
        scratch_shapes=[pltpu.VMEM((ROUTER_ROWS, 1), F32), pltpu.VMEM((TM, TM), BF16)],
        compiler_params=_cparams("arbitrary"),
        name="router",
    )(x_all, mod_p, mod_s, mod_p, mod_s, ng, w_r, b_r)


SC_INDEX_MAX = 128
SC_ROWS_BYTES = 400 * 1024


def _gather_chunk(per_worker, row_bytes, buffers=1):
    top = min(SC_INDEX_MAX, SC_ROWS_BYTES // (row_bytes * buffers)) // SUBLANES * SUBLANES
    for ch in range(top, SUBLANES - 1, -SUBLANES):
        if per_worker % ch == 0:
            return ch
    raise ValueError(per_worker)


def _sc_gather(table, idx):
    nc, nw = SC_CORES, SC_CORES * SC_SUBCORES
    m = idx.shape[0]
    per_w = m // nw
    assert per_w * nw == m
    width = table.shape[1]
    ch = _gather_chunk(per_w, width * table.dtype.itemsize, buffers=2)
    mesh = plsc.VectorSubcoreMesh(core_axis_name="c", subcore_axis_name="s")

    n_chunks = per_w // ch

    @functools.partial(
        pl.kernel,
        out_type=jax.ShapeDtypeStruct((m, width), table.dtype),
        mesh=mesh,
        scratch_types=[pltpu.VMEM((ch,), jnp.int32)] * 2
        + [pltpu.VMEM((ch, width), table.dtype)] * 2
        + [pltpu.SemaphoreType.DMA] * 4,
    )
    def gather_kernel(t_hbm, i_hbm, o_hbm, idx_a, idx_b, rows_a, rows_b, sg_a, sg_b, sw_a, sw_b):
        wid = lax.axis_index("s") * nc + lax.axis_index("c")
        base = wid * per_w
        idx_v, rows_v, sem_g, sem_w = (idx_a, idx_b), (rows_a, rows_b), (sg_a, sg_b), (sw_a, sw_b)

        def out_rows(j):
            return o_hbm.at[pl.ds(pl.multiple_of(base + j * ch, 8), ch)]

        gathers, writes = {}, {}
        for j in range(n_chunks):
            b = j % 2
            if j >= 2:
                writes[j - 2].wait()
            pltpu.sync_copy(i_hbm.at[pl.ds(pl.multiple_of(base + j * ch, 8), ch)], idx_v[b])
            gathers[j] = pltpu.async_copy(t_hbm.at[idx_v[b]], rows_v[b], sem_g[b])
            if j >= 1:
                gathers[j - 1].wait()
                writes[j - 1] = pltpu.async_copy(rows_v[1 - b], out_rows(j - 1), sem_w[1 - b])
        last = n_chunks - 1
        gathers[last].wait()
        writes[last] = pltpu.async_copy(rows_v[last % 2], out_rows(last), sem_w[last % 2])
        for j in range(max(last - 1, 0), n_chunks):
            writes[j].wait()

    return gather_kernel(table, idx)


def _sc_scatter2(rows, dest0, dest1, n_out):
    nc, nw = SC_CORES, SC_CORES * SC_SUBCORES
    n, width = rows.shape
    per_w = n // nw
    assert per_w * nw == n
    ch = _gather_chunk(per_w, width * rows.dtype.itemsize)
    mesh = plsc.VectorSubcoreMesh(core_axis_name="c", subcore_axis_name="s")

    @functools.partial(
        pl.kernel,
        out_type=jax.ShapeDtypeStruct((n_out, width), rows.dtype),
        mesh=mesh,
        scratch_types=[
            pltpu.VMEM((ch,), jnp.int32),
            pltpu.VMEM((ch,), jnp.int32),
            pltpu.VMEM((ch, width), rows.dtype),
            pltpu.SemaphoreType.DMA,
            pltpu.SemaphoreType.DMA,
        ],
    )
    def scatter_kernel(r_hbm, d0_hbm, d1_hbm, o_hbm, i0_v, i1_v, rows_v, sem0, sem1):
        wid = lax.axis_index("s") * nc + lax.axis_index("c")
        base = wid * per_w

        @pl.loop(0, per_w // ch)
        def _(j):
            off = pl.multiple_of(base + j * ch, 8)
            pltpu.sync_copy(d0_hbm.at[pl.ds(off, ch)], i0_v)
            pltpu.sync_copy(d1_hbm.at[pl.ds(off, ch)], i1_v)
            pltpu.sync_copy(r_hbm.at[pl.ds(off, ch)], rows_v)
            c0 = pltpu.async_copy(rows_v, o_hbm.at[i0_v], sem0)
            c1 = pltpu.async_copy(rows_v, o_hbm.at[i1_v], sem1)
            c0.wait()
            c1.wait()

    return scatter_kernel(rows, dest0, dest1)


def _expert_kernel(be_ref, first_ref, nvalid_ref, next_ref, nused_ref, x_ref, wg_hbm, wu_hbm,
                   wd_hbm, y_ref, wgf_ref, wuf_ref, wdf_ref, wgb_ref, wub_ref, wdb_ref, sems,
                   *, layer):
    b = pl.program_id(0)

    def weight_copies(e):
        return (
            pltpu.make_async_copy(wg_hbm.at[layer, e], wgf_ref, sems.at[0]),
            pltpu.make_async_copy(wu_hbm.at[layer, e], wuf_ref, sems.at[1]),
            pltpu.make_async_copy(wd_hbm.at[layer, e], wdf_ref, sems.at[2]),
        )

    @pl.when(b == 0)
    def _():
        for cp in weight_copies(be_ref[0]):
            cp.start()

    @pl.when(b < nused_ref[0])
    def _():
        @pl.when(first_ref[b] == 1)
        def _():
            for cp in weight_copies(be_ref[b]):
                cp.wait()
            wgb_ref[...] = wgf_ref[...].astype(BF16)
            wub_ref[...] = wuf_ref[...].astype(BF16)
            wdb_ref[...] = wdf_ref[...].astype(BF16)

            @pl.when(next_ref[b] >= 0)
            def _():
                for cp in weight_copies(next_ref[b]):
                    cp.start()

        def run(rows):
            live = lax.broadcasted_iota(jnp.int32, (rows, 1), 0) < nvalid_ref[b]
            x = _unpack_bf16_pairs(jnp.where(live, x_ref[0:rows, :], 0), BF16)
            g = _dot(x, wgb_ref[...])
            u = _dot(x, wub_ref[...])
            hmid = (g * jax.nn.sigmoid(g)) * u
            y_ref[0:rows, :] = _pack_bf16_pairs(_dot(hmid.astype(BF16), wdb_ref[...]))

        half = BM // 2

        @pl.when(nvalid_ref[b] > half)
        def _():
            run(BM)

        @pl.when(nvalid_ref[b] <= half)
        def _():
            run(half)
            y_ref[half:BM, :] = jnp.zeros((BM - half, D // 2), jnp.int32)

    @pl.when(b >= nused_ref[0])
    def _():
        y_ref[...] = jnp.zeros((BM, D // 2), jnp.int32)


def _experts(x_sorted, block_expert, first, nvalid, next_expert, nused, layer, w_gate, w_up,
             w_down):
    def row_map(b, be, fi, nv, nx, nu):
        return (jnp.minimum(b, nu[0] - 1), 0)

    def out_map(b, be, fi, nv, nx, nu):
        return (b, 0)

    grid_spec = pltpu.PrefetchScalarGridSpec(
        num_scalar_prefetch=5,
        grid=(N_BLOCKS,),
        in_specs=[
            pl.BlockSpec((BM, D // 2), row_map),
            pl.BlockSpec(memory_space=pl.ANY),
            pl.BlockSpec(memory_space=pl.ANY),
            pl.BlockSpec(memory_space=pl.ANY),
        ],
        out_specs=pl.BlockSpec((BM, D // 2), out_map),
        scratch_shapes=[
            pltpu.VMEM((D, D_EXPERT), F32),
            pltpu.VMEM((D, D_EXPERT), F32),
            pltpu.VMEM((D_EXPERT, D), F32),
            pltpu.VMEM((D, D_EXPERT), BF16),
            pltpu.VMEM((D, D_EXPERT), BF16),
            pltpu.VMEM((D_EXPERT, D), BF16),
            pltpu.SemaphoreType.DMA((3,)),
        ],
    )
    return pl.pallas_call(
        functools.partial(_expert_kernel, layer=layer),
        grid_spec=grid_spec,
        out_shape=jax.ShapeDtypeStruct((P_ROWS, D // 2), jnp.int32),
        compiler_params=_cparams("arbitrary"),
        name="experts",
    )(block_expert, first, nvalid, next_expert, nused, x_sorted, w_gate, w_up, w_down)


def _combine_final_kernel(x_ref, g0_ref, g1_ref, wts_ref, gtp_ref, gts_ref, fg_ref, yp_ref,
                          ys_ref):
    i = pl.program_id(0)
    gate = _tile_mod(i, gtp_ref, gts_ref)
    y = _rms(_plus_moe(x_ref[...], g0_ref, g1_ref, wts_ref, gate), fg_ref[...])

    @pl.when(i < N_PROMPT_TILES)
    def _():
        yp_ref[...] = y

    @pl.when(i == N_PROMPT_TILES)
    def _():
        ys_ref[...] = y[0:DEC_BATCH, :]


def _combine_final(x_all, gathered, wts, mod_p, mod_s, layer, final_g):
    in_specs = [
        pl.BlockSpec((TM, D), lambda i: (i, 0)),
        pl.BlockSpec((TM, D // 2), lambda i: (i, 0)),
        pl.BlockSpec((TM, D // 2), lambda i: (i + N_TILES, 0)),
        pl.BlockSpec((TM, LANES), lambda i: (i, 0)),
    ] + _unified_mod_specs(layer, (5,))
    return pl.pallas_call(
        _combine_final_kernel,
        grid=(N_PROMPT_TILES + 1,),
        in_specs=in_specs + [_full((1, D))],
        out_specs=[
            pl.BlockSpec((TM, D), lambda i: (jnp.minimum(i, N_PROMPT_TILES - 1), 0)),
            pl.BlockSpec((DEC_BATCH, D), lambda i: (0, 0)),
        ],
        out_shape=[
            jax.ShapeDtypeStruct((T_PROMPT, D), F32),
            jax.ShapeDtypeStruct((DEC_BATCH, D), F32),
        ],
        compiler_params=_cparams("arbitrary"),
        name="combine_final",
    )(x_all, gathered, gathered, wts, mod_p, mod_s, final_g)


def _plan_kernel(cnt_ref, meta_ref, d0_ref, d1_ref, be_ref, first_ref, nvalid_ref, next_ref,
                 nused_ref, pstart_ref, nxt_ref):
    shift = BM.bit_length() - 1
    assert 1 << shift == BM

    def count(e):
        return cnt_ref[EXPERT_LANE0 + e, 0]

    def scan_back(k, nxt):
        e = N_EXPERTS - 1 - k
        nxt_ref[e] = nxt
        return jnp.where(count(e) > 0, e, nxt)

    lax.fori_loop(0, N_EXPERTS, scan_back, jnp.int32(-1))

    def fill(e, blk):
        c = count(e)
        pstart_ref[e] = blk << shift

        def one(j, carry):
            be_ref[blk + j] = e
            first_ref[blk + j] = jnp.where(j == 0, 1, 0)
            nvalid_ref[blk + j] = jnp.minimum(c - (j << shift), BM)
            next_ref[blk + j] = nxt_ref[e]
            return carry

        n_blk = (c + (BM - 1)) >> shift
        lax.fori_loop(0, n_blk, one, 0)
        return blk + n_blk

    used = lax.fori_loop(0, N_EXPERTS, fill, jnp.int32(0))
    nused_ref[0] = used

    def tail(b, carry):
        be_ref[b] = 0
        first_ref[b] = 0
        nvalid_ref[b] = 0
        next_ref[b] = -1
        return carry

    lax.fori_loop(used, N_BLOCKS, tail, 0)

    m = meta_ref[...]
    expert = m[0:2, :]
    dest = m[2:4, :]
    for e in range(N_EXPERTS):
        dest = dest + jnp.where(expert == float(e), pstart_ref[e].astype(F32), 0.0)
    dest = dest.astype(jnp.int32)
    d0_ref[...] = dest[0:1, :]
    d1_ref[...] = dest[1:2, :]


def _plan(cnt, meta):
    smem = pl.BlockSpec(memory_space=pltpu.SMEM)
    blocks = jax.ShapeDtypeStruct((N_BLOCKS,), jnp.int32)
    return pl.pallas_call(
        _plan_kernel,
        grid=(1,),
        in_specs=[smem, _full((META_ROWS, T_PAD))],
        out_specs=[_full((1, T_PAD)), _full((1, T_PAD)), smem, smem, smem, smem, smem],
        out_shape=[
            jax.ShapeDtypeStruct((1, T_PAD), jnp.int32),
            jax.ShapeDtypeStruct((1, T_PAD), jnp.int32),
            blocks, blocks, blocks, blocks,
            jax.ShapeDtypeStruct((1,), jnp.int32),
        ],
        scratch_shapes=[pltpu.SMEM((N_EXPERTS,), jnp.int32), pltpu.SMEM((N_EXPERTS,), jnp.int32)],
        compiler_params=_cparams("arbitrary"),
        name="plan",
    )(cnt, meta)


def _moe(x_all, mod_p, mod_s, ng, layer, w_r, b_r, w_gate, w_up, w_down):
    h2, meta, wts, cnt = _router(x_all, mod_p, mod_s, ng, layer, w_r, b_r)
    dest0, dest1, block_expert, first, nvalid, next_expert, nused = _plan(cnt, meta)
    dest0, dest1 = dest0.reshape(T_PAD), dest1.reshape(T_PAD)
    x_sorted = _sc_scatter2(h2, dest0, dest1, P_ROWS)
    y_sorted = _experts(x_sorted, block_expert, first, nvalid, next_expert, nused, layer,
                        w_gate, w_up, w_down)
    gathered = _sc_gather(y_sorted, jnp.concatenate([dest0, dest1]))
    return gathered, wts


def kernel(x_prompt, x_sample, c_prompt, c_sample, state_pool, state_conv, ada_w, ada_b, norm_g, final_g, a_w_in, a_ln_g, a_ln_b, a_w_s, a_b_s, a_w_out, b_w_in, b_w_grp, b_b_grp, b_scale, b_w_out, c_w_in, c_b_in, c_w_dw, c_b_dw, c_ln_g, c_ln_b, c_w_out, c_b_out, moe_w_grp, moe_b_grp, moe_w_exp, moe_b_exp, moe_w_gate, moe_w_up, moe_w_down):
    x_all = (x_prompt.reshape(T_PROMPT, D), x_sample.reshape(DEC_BATCH, D))
    mod_p, mod_s = _ada_mods(jnp.concatenate([c_prompt, c_sample], axis=0), ada_w, ada_b)
    mod_p = mod_p.reshape(DEPTH, BATCH, 1, 6 * D)

    a_w_in_b, a_w_out_b = a_w_in.astype(BF16), a_w_out.astype(BF16)
    b_w_in_b, b_w_grp_b, b_w_out_b = b_w_in.astype(BF16), b_w_grp.astype(BF16), b_w_out.astype(BF16)
    c_w_in_b, c_w_out_b = c_w_in.astype(BF16), c_w_out.astype(BF16)
    pad_rows = ROUTER_ROWS - N_GROUPS - N_EXPERTS
    w_r = jnp.pad(jnp.swapaxes(jnp.concatenate([moe_w_grp, moe_w_exp], axis=2), 1, 2),
                  ((0, 0), (0, pad_rows), (0, 0))).astype(BF16)
    b_r = jnp.pad(jnp.concatenate([moe_b_grp, moe_b_exp], axis=1), ((0, 0), (0, pad_rows)))

    new_a_p, new_a_s, new_b_p, new_b_s, new_c_p, new_c_s = [], [], [], [], [], []
    ia = ib = ic = 0
    assert len(range(0, DEPTH, 3)) == 2
    pending = None
    for layer in range(DEPTH):
        ng1 = norm_g[layer, 0].reshape(1, D)
        ng2 = norm_g[layer, 1].reshape(1, D)
        kind = layer % 3
        if kind == 0:
            x_all, st_p, st_s = _mixer_a(
                x_all, pending, new_a_p[0] if new_a_p else None, mod_p, mod_s, ng1, layer, ia,
                a_w_in_b, a_ln_g.reshape(-1, 1, D), a_ln_b.reshape(-1, 1, D), a_w_s, a_b_s,
                a_w_out_b)
            new_a_p = [st_p]
            new_a_s.append(st_s)
            ia += 1
        elif kind == 1:
            x_all, st_p, st_s = _mixer_b(
                x_all, pending, mod_p, mod_s, ng1, layer, ib, state_pool, b_w_in_b, b_w_grp_b,
                b_b_grp.reshape(-1, len(POOL_WINDOWS), 1, B_GROUP_DIM),
                b_scale.reshape(-1, 1, D), b_w_out_b)
            new_b_p.append(st_p)
            new_b_s.append(st_s)
            ib += 1
        else:
            x_all, st_p, st_s = _mixer_c(
                x_all, pending, mod_p, mod_s, ng1, layer, ic, state_conv, c_w_in_b,
                c_b_in.reshape(-1, 1, 2 * D), c_w_dw, c_b_dw.reshape(-1, 1, D),
                c_ln_g.reshape(-1, 1, D), c_ln_b.reshape(-1, 1, D), c_w_out_b,
                c_b_out.reshape(-1, 1, D))
            new_c_p.append(st_p)
            new_c_s.append(st_s)
            ic += 1
        pending = _moe(x_all, mod_p, mod_s, ng2, layer, w_r[layer],
                       b_r[layer].reshape(ROUTER_ROWS, 1), moe_w_gate, moe_w_up, moe_w_down)

    y_p, y_s = _combine_final(x_all, *pending, mod_p, mod_s, DEPTH - 1, final_g.reshape(1, D))
    return (y_p.reshape(BATCH, SEQ, D), y_s.reshape(DEC_BATCH, 1, D),
            new_a_p[0], jnp.stack(new_a_s), jnp.stack(new_b_p), jnp.stack(new_b_s),
            jnp.stack(new_c_p), jnp.stack(new_c_s))
```

```python
import functools

import jax
import jax.numpy as jnp
from jax import lax
from jax.experimental import pallas as pl
from jax.experimental.pallas import tpu as pltpu
from jax.experimental.pallas import tpu_sc as plsc

F32 = jnp.float32
BF16 = jnp.bfloat16

D = 1024
BATCH = 8
SEQ = 2048
DEPTH = 4
DEC_BATCH = 128
PAST_LEN = 16384
CHUNK = 128
A_HEADS = 8
POOL_WINDOWS = (2, 4, 8, 16)
B_GROUP_DIM = D // len(POOL_WINDOWS)
POOL_BUF = max(POOL_WINDOWS) - 1
CONV_WIDTH = 31
CONV_BUF = CONV_WIDTH - 1
N_GROUPS = 4
EXPERTS_PER_GROUP = 8
N_EXPERTS = N_GROUPS * EXPERTS_PER_GROUP
D_EXPERT = D // 2
EPS = 1e-6

LANES = 128
SUBLANES = 8
CONV_ROWS = 128
SC_CORES = 2
SC_SUBCORES = 16
TM = 512
TILES_PER_SEQ = SEQ // TM
T_PROMPT = BATCH * SEQ
T_ALL = T_PROMPT + DEC_BATCH
N_TILES = -(-T_ALL // TM)
T_PAD = N_TILES * TM
N_PROMPT_TILES = T_PROMPT // TM
SAMPLE_ROWS = 32
BM = 512
N_BLOCKS = -(-(2 * T_PAD) // BM) + N_EXPERTS
P_ROWS = N_BLOCKS * BM
META_ROWS = 8
POOL_CARRY = 16
CONV_CARRY = 32
EXPERT_LANE0 = N_GROUPS
ROUTER_ROWS = 48
VMEM_LIMIT = 56 * 1024 * 1024


def _cparams(*sem):
    return pltpu.CompilerParams(dimension_semantics=sem, vmem_limit_bytes=VMEM_LIMIT)


def _rms(x, g):
    return x * lax.rsqrt(jnp.mean(x * x, axis=-1, keepdims=True) + EPS) * g


def _ln(x, g, b):
    mu = jnp.mean(x, axis=-1, keepdims=True)
    xc = x - mu
    var = jnp.mean(xc * xc, axis=-1, keepdims=True)
    return xc * lax.rsqrt(var + EPS) * g + b


def _dot(a, b):
    return jnp.dot(a, b, preferred_element_type=F32)


def _pack_bf16_pairs(v):
    w = v.shape[1] // 2
    lo = lax.bitcast_convert_type(v[:, :w].astype(BF16).astype(F32), jnp.uint32)
    hi = lax.bitcast_convert_type(v[:, w:].astype(BF16).astype(F32), jnp.uint32)
    return lax.bitcast_convert_type(lax.shift_right_logical(lo, jnp.uint32(16)) | hi, jnp.int32)


def _unpack_bf16_pairs(words, dtype):
    u = lax.bitcast_convert_type(words, jnp.uint32)
    lo = lax.bitcast_convert_type(lax.shift_left(u, jnp.uint32(16)), F32)
    hi = lax.bitcast_convert_type(u & jnp.uint32(0xFFFF0000), F32)
    return jnp.concatenate([lo.astype(dtype), hi.astype(dtype)], axis=1)


def _ada_kernel(c_ref, w_ref, b_ref, op_ref, os_ref):
    c = c_ref[...]
    s = (c * jax.nn.sigmoid(c)).astype(BF16)
    mod = _dot(s, w_ref[...].astype(BF16)) + b_ref[...]
    op_ref[...] = mod[0:BATCH, :]
    os_ref[...] = mod[BATCH:, :]


def _ada_mods(c_all, ada_w, ada_b):
    n = c_all.shape[0]
    assert n == BATCH + DEC_BATCH and BATCH % SUBLANES == 0
    tn = 2048
    return pl.pallas_call(
        _ada_kernel,
        grid=(DEPTH, 6 * D // tn),
        in_specs=[
            pl.BlockSpec((n, D), lambda l, j: (0, 0)),
            pl.BlockSpec((None, D, tn), lambda l, j: (l, 0, j)),
            pl.BlockSpec((None, 1, tn), lambda l, j: (l, 0, j)),
        ],
        out_specs=[
            pl.BlockSpec((None, BATCH, tn), lambda l, j: (l, 0, j)),
            pl.BlockSpec((None, DEC_BATCH, tn), lambda l, j: (l, 0, j)),
        ],
        out_shape=[
            jax.ShapeDtypeStruct((DEPTH, BATCH, 6 * D), F32),
            jax.ShapeDtypeStruct((DEPTH, DEC_BATCH, 6 * D), F32),
        ],
        compiler_params=_cparams("parallel", "parallel"),
        name="ada_mods",
    )(c_all, ada_w, ada_b.reshape(DEPTH, 1, 6 * D))


def _mod_specs_prompt(layer, chunks):
    return [
        pl.BlockSpec((None, None, 1, D),
                     lambda i, c=c: (layer, jnp.minimum(i // TILES_PER_SEQ, BATCH - 1), 0, c))
        for c in chunks
    ]


def _mod_specs_sample(layer, chunks, rows, row_block0=0):
    return [
        pl.BlockSpec((None, rows, D), lambda i, c=c: (layer, i + row_block0, c)) for c in chunks
    ]


def _full(shape):
    nd = len(shape)
    return pl.BlockSpec(shape, lambda i: (0,) * nd)


def _layer_block(shape, layer):
    nd = len(shape)
    return pl.BlockSpec((None,) + shape, lambda i: (layer,) + (0,) * nd)


def _plus_moe(x, g0_ref, g1_ref, wts_ref, gate):
    w = wts_ref[...]
    y0 = _unpack_bf16_pairs(g0_ref[...], F32)
    y1 = _unpack_bf16_pairs(g1_ref[...], F32)
    return x + gate * (w[:, 0:1] * y0 + w[:, 1:2] * y1)


def _residual_in(refs, pending):
    if not pending:
        return refs[0][...], refs[1:]
    x_ref, g0_ref, g1_ref, wts_ref, gate_ref = refs[:5]
    return _plus_moe(x_ref[...], g0_ref, g1_ref, wts_ref, gate_ref[...]), refs[5:]


def _pending_prompt(pending, mod_p, layer):
    if pending is None:
        return [], []
    gathered, wts = pending
    specs = [
        pl.BlockSpec((TM, D // 2), lambda i: (i, 0)),
        pl.BlockSpec((TM, D // 2), lambda i: (i + N_TILES, 0)),
        pl.BlockSpec((TM, LANES), lambda i: (i, 0)),
    ] + _mod_specs_prompt(layer - 1, (5,))
    return specs, [gathered, gathered, wts, mod_p]


def _pending_sample(pending, mod_s, layer, rows):
    if pending is None:
        return [], []
    gathered, wts = pending
    first = T_PROMPT // rows
    second = (T_PAD + T_PROMPT) // rows
    specs = [
        pl.BlockSpec((rows, D // 2), lambda i: (first + i, 0)),
        pl.BlockSpec((rows, D // 2), lambda i: (second + i, 0)),
        pl.BlockSpec((rows, LANES), lambda i: (first + i, 0)),
    ] + _mod_specs_sample(layer - 1, (5,), rows)
    return specs, [gathered, gathered, wts, mod_s]


def _mix_a_front(x, sh, sc, ng, win_ref, lng_ref, lnb_ref):
    h = (_rms(x, ng) * (1.0 + sc) + sh).astype(BF16)
    z = jax.nn.gelu(_dot(h, win_ref[...]))
    u = z[:, :D]
    v = _ln(z[:, D:], lng_ref[...], lnb_ref[...])
    return u, v


def _mix_a_prompt_kernel(*refs, pending, first, stack):
    x, refs = _residual_in(refs, pending)
    (sh_ref, sc_ref, gt_ref, ng_ref, win_ref, lng_ref, lnb_ref, ws_ref, bs_ref, wout_ref,
     *cv_prev_ref, x1_ref, cv_ref, mixed_ref) = refs
    i = pl.program_id(0)
    u, v = _mix_a_front(x, sh_ref[...], sc_ref[...], ng_ref[...], win_ref, lng_ref, lnb_ref)
    vb = v.astype(BF16)
    n_chunks = TM // CHUNK
    row = lax.broadcasted_iota(jnp.int32, (CHUNK, CHUNK), 0)
    col = lax.broadcasted_iota(jnp.int32, (CHUNK, CHUNK), 1)
    tril = row >= col
    hd_dim = D // A_HEADS
    for hd in range(A_HEADS):
        ws = jnp.where(tril, ws_ref[hd], 0.0).astype(BF16)
        cols = slice(hd * hd_dim, (hd + 1) * hd_dim)
        vcat = jnp.concatenate([vb[c * CHUNK:(c + 1) * CHUNK, cols] for c in range(n_chunks)], axis=1)
        m = _dot(ws, vcat)
        for c in range(n_chunks):
            mixed_ref[c * CHUNK:(c + 1) * CHUNK, cols] = m[:, c * hd_dim:(c + 1) * hd_dim] + bs_ref[hd]
    y = _dot((u * mixed_ref[...]).astype(BF16), wout_ref[...])
    out = x + gt_ref[...] * y
    if first:
        out = jnp.where(i < N_PROMPT_TILES, out, 0.0)
    x1_ref[...] = out

    @pl.when(i % TILES_PER_SEQ == TILES_PER_SEQ - 1)
    def _():
        if stack:
            cv_ref[0] = cv_prev_ref[0][...]
            cv_ref[1] = v[TM - CHUNK:, :]
        else:
            cv_ref[...] = v[TM - CHUNK:, :]


def _mix_a_sample_kernel(*refs, pending, first):
    if first:
        refs = refs[:1] + refs[2:]
    x, refs = _residual_in(refs, pending)
    (sh_ref, sc_ref, gt_ref, ng_ref, win_ref, lng_ref, lnb_ref, wsd_ref, bsd_ref, wout_ref,
     x1_ref, cv_ref) = refs
    u, v = _mix_a_front(x, sh_ref[...], sc_ref[...], ng_ref[...], win_ref, lng_ref, lnb_ref)
    mixed = wsd_ref[...] * v + bsd_ref[...]
    y = _dot((u * mixed).astype(BF16), wout_ref[...])
    x1_ref[...] = x + gt_ref[...] * y
    cv_ref[...] = v


def _mixer_a(x_in, pending, cv_prev, mod_p, mod_s, ng, layer, la, w_in, ln_g, ln_b, w_s, b_s,
             w_out):
    first = isinstance(x_in, tuple)
    stack = cv_prev is not None
    assert not stack or la == 1
    x_all = x_in[0] if first else x_in
    bs_b = jnp.broadcast_to(b_s[la][:, :, None], (A_HEADS, CHUNK, D // A_HEADS))
    common = [
        _layer_block((D, 2 * D), la), _layer_block((1, D), la), _layer_block((1, D), la),
    ]
    has_pending = pending is not None
    pend_specs, pend_args = _pending_prompt(pending, mod_p, layer)
    last_tile = N_PROMPT_TILES - 1

    def seq_of(i):
        return jnp.minimum(i, last_tile) // TILES_PER_SEQ

    if stack:
        cv_in_specs = [pl.BlockSpec((None, CHUNK, D), lambda i: (seq_of(i), 0, 0))]
        cv_in_args = [cv_prev]
        cv_spec = pl.BlockSpec((2, None, CHUNK, D), lambda i: (0, seq_of(i), 0, 0))
        cv_shape = jax.ShapeDtypeStruct((2, BATCH, CHUNK, D), F32)
    else:
        cv_in_specs, cv_in_args = [], []
        cv_spec = pl.BlockSpec((None, CHUNK, D), lambda i: (seq_of(i), 0, 0))
        cv_shape = jax.ShapeDtypeStruct((BATCH, CHUNK, D), F32)
    x_all, cv_p = pl.pallas_call(
        functools.partial(_mix_a_prompt_kernel, pending=has_pending, first=first, stack=stack),
        grid=(N_TILES if first else N_PROMPT_TILES,),
        in_specs=[pl.BlockSpec((TM, D), lambda i: (jnp.minimum(i, last_tile), 0))] + pend_specs
        + _mod_specs_prompt(layer, (0, 1, 2))
        + [_full((1, D))] + common
        + [_layer_block((A_HEADS, CHUNK, CHUNK), la), _full((A_HEADS, CHUNK, D // A_HEADS)),
           _layer_block((D, D), la)] + cv_in_specs,
        out_specs=[pl.BlockSpec((TM, D), lambda i: (i, 0)), cv_spec],
        out_shape=[jax.ShapeDtypeStruct((T_PAD, D), F32), cv_shape],
        scratch_shapes=[pltpu.VMEM((TM, D), F32)],
        input_output_aliases={} if first else {0: 0},
        compiler_params=_cparams("arbitrary"),
        name="mixer_a_prompt",
    )(x_all, *pend_args, mod_p, mod_p, mod_p, ng, w_in, ln_g, ln_b, w_s, bs_b, w_out,
      *cv_in_args)

    wsd = jnp.repeat(w_s[la, :, 0, 0], D // A_HEADS).reshape(1, D)
    bsd = jnp.repeat(b_s[la, :, 0], D // A_HEADS).reshape(1, D)
    sblk = T_PROMPT // DEC_BATCH
    pend_specs, pend_args = _pending_sample(pending, mod_s, layer, DEC_BATCH)
    if first:
        x_specs = [pl.BlockSpec((DEC_BATCH, D), lambda i: (0, 0)), pl.BlockSpec(memory_space=pl.ANY)]
        x_args = [x_in[1], x_all]
    else:
        x_specs = [pl.BlockSpec((DEC_BATCH, D), lambda i: (sblk, 0))]
        x_args = [x_all]
    x_all, cv_s = pl.pallas_call(
        functools.partial(_mix_a_sample_kernel, pending=has_pending, first=first),
        grid=(1,),
        in_specs=x_specs + pend_specs
        + _mod_specs_sample(layer, (0, 1, 2), DEC_BATCH)
        + [_full((1, D))] + common
        + [_full((1, D)), _full((1, D)), _layer_block((D, D), la)],
        out_specs=[
            pl.BlockSpec((DEC_BATCH, D), lambda i: (sblk, 0)),
            pl.BlockSpec((DEC_BATCH, D), lambda i: (0, 0)),
        ],
        out_shape=[
            jax.ShapeDtypeStruct((T_PAD, D), F32),
            jax.ShapeDtypeStruct((DEC_BATCH, D), F32),
        ],
        input_output_aliases={len(x_args) - 1: 0},
        compiler_params=_cparams("arbitrary"),
        name="mixer_a_sample",
    )(*x_args, *pend_args, mod_s, mod_s, mod_s, ng, w_in, ln_g, ln_b, wsd, bsd, w_out)
    return x_all, cv_p, cv_s.reshape(DEC_BATCH, 1, D)


def _mix_b_tail(pooled_groups, wgrp_ref, bgrp_ref, scale_ref, wout_ref):
    outs = [
        _dot(pg.astype(BF16), wgrp_ref[g]) + bgrp_ref[g]
        for g, pg in enumerate(pooled_groups)
    ]
    mixed = jnp.concatenate(outs, axis=1) * scale_ref[...]
    return _dot(mixed.astype(BF16), wout_ref[...])


def _mix_b_prompt_kernel(*refs, pending):
    x, refs = _residual_in(refs, pending)
    (sh_ref, sc_ref, gt_ref, ng_ref, win_ref, wgrp_ref, bgrp_ref, scale_ref, wout_ref,
     x1_ref, st_ref, full_ref, s2_ref, s4_ref, s8_ref) = refs
    assert POOL_WINDOWS == (2, 4, 8, 16)
    i = pl.program_id(0)
    j = i % TILES_PER_SEQ
    h = (_rms(x, ng_ref[...]) * (1.0 + sc_ref[...]) + sh_ref[...]).astype(BF16)
    p = _dot(h, win_ref[...])
    gd = B_GROUP_DIM
    top = SUBLANES
    cur = top + POOL_CARRY
    rows = TM + POOL_CARRY

    @pl.when(i == 0)
    def _():
        for ref in (full_ref, s2_ref, s4_ref, s8_ref):
            ref[0:top, :] = jnp.zeros((top, ref.shape[1]), F32)

    full_ref[top:cur, :] = jnp.where(j == 0, 0.0, full_ref[top:cur, :])
    full_ref[cur:cur + TM, :] = p
    s2_ref[top:top + rows, :] = full_ref[top:top + rows, :] + full_ref[pl.ds(top - 1, rows), :]
    s4_ref[top:top + rows, :] = s2_ref[top:top + rows, gd:] + s2_ref[pl.ds(top - 2, rows), gd:]
    s8_ref[top:top + rows, :] = s4_ref[top:top + rows, gd:] + s4_ref[pl.ds(top - 4, rows), gd:]
    s16 = s8_ref[cur:cur + TM, gd:] + s8_ref[pl.ds(cur - 8, TM), gd:]
    sums = [s2_ref[cur:cur + TM, 0:gd], s4_ref[cur:cur + TM, 0:gd], s8_ref[cur:cur + TM, 0:gd], s16]
    pos = j * TM + lax.broadcasted_iota(jnp.int32, (TM, 1), 0)
    pooled = []
    for g, w in enumerate(POOL_WINDOWS):
        cnt = jnp.minimum(w, pos + 1).astype(F32)
        pooled.append(sums[g] / cnt - p[:, g * gd:(g + 1) * gd])
    y = _mix_b_tail(pooled, wgrp_ref, bgrp_ref, scale_ref, wout_ref)
    x1_ref[...] = x + gt_ref[...] * y

    @pl.when(j == TILES_PER_SEQ - 1)
    def _():
        st_ref[...] = full_ref[pl.ds(cur + TM - POOL_BUF, POOL_BUF), :]

    full_ref[top:cur, :] = full_ref[pl.ds(TM + top, POOL_CARRY), :]


def _mix_b_sample_kernel(*refs, pending):
    x, refs = _residual_in(refs, pending)
    (sh_ref, sc_ref, gt_ref, ng_ref, st_ref, win_ref, wgrp_ref, bgrp_ref, scale_ref, wout_ref,
     x1_ref, nst_ref) = refs
    h = (_rms(x, ng_ref[...]) * (1.0 + sc_ref[...]) + sh_ref[...]).astype(BF16)
    p = _dot(h, win_ref[...])
    pooled = []
    for g, w in enumerate(POOL_WINDOWS):
        s = p[:, g * B_GROUP_DIM:(g + 1) * B_GROUP_DIM]
        for k in range(1, w):
            s = s + st_ref[POOL_BUF - k, :, g * B_GROUP_DIM:(g + 1) * B_GROUP_DIM]
        cnt = float(min(w, PAST_LEN + 1))
        pooled.append(s / cnt - p[:, g * B_GROUP_DIM:(g + 1) * B_GROUP_DIM])
    y = _mix_b_tail(pooled, wgrp_ref, bgrp_ref, scale_ref, wout_ref)
    x1_ref[...] = x + gt_ref[...] * y
    for k in range(1, POOL_BUF):
        nst_ref[k - 1] = st_ref[k]
    nst_ref[POOL_BUF - 1] = p


def _mixer_b(x_all, pending, mod_p, mod_s, ng, layer, lb, state, w_in, w_grp, b_grp, scale,
             w_out):
    gd = B_GROUP_DIM
    common = [
        _layer_block((D, D), lb), _layer_block((len(POOL_WINDOWS), gd, gd), lb),
        _layer_block((len(POOL_WINDOWS), 1, gd), lb), _layer_block((1, D), lb),
        _layer_block((D, D), lb),
    ]
    has_pending = pending is not None
    pend_specs, pend_args = _pending_prompt(pending, mod_p, layer)
    x_all, st_p = pl.pallas_call(
        functools.partial(_mix_b_prompt_kernel, pending=has_pending),
        grid=(N_PROMPT_TILES,),
        in_specs=[pl.BlockSpec((TM, D), lambda i: (i, 0))] + pend_specs
        + _mod_specs_prompt(layer, (0, 1, 2)) + [_full((1, D))] + common,
        out_specs=[
            pl.BlockSpec((TM, D), lambda i: (i, 0)),
            pl.BlockSpec((None, POOL_BUF, D), lambda i: (i // TILES_PER_SEQ, 0, 0)),
        ],
        out_shape=[
            jax.ShapeDtypeStruct((T_PAD, D), F32),
            jax.ShapeDtypeStruct((BATCH, POOL_BUF, D), F32),
        ],
        scratch_shapes=[
            pltpu.VMEM((SUBLANES + POOL_CARRY + TM, D - g * gd), F32) for g in (0, 0, 1, 2)],
        input_output_aliases={0: 0},
        compiler_params=_cparams("arbitrary"),
        name="mixer_b_prompt",
    )(x_all, *pend_args, mod_p, mod_p, mod_p, ng, w_in, w_grp, b_grp, scale, w_out)

    r = SAMPLE_ROWS
    sblk = T_PROMPT // r
    pend_specs, pend_args = _pending_sample(pending, mod_s, layer, r)
    x_all, st_s = pl.pallas_call(
        functools.partial(_mix_b_sample_kernel, pending=has_pending),
        grid=(DEC_BATCH // r,),
        in_specs=[pl.BlockSpec((r, D), lambda i: (sblk + i, 0))] + pend_specs
        + _mod_specs_sample(layer, (0, 1, 2), r) + [_full((1, D))]
        + [pl.BlockSpec((None, POOL_BUF, r, D), lambda i: (lb, 0, i, 0))] + common,
        out_specs=[
            pl.BlockSpec((r, D), lambda i: (sblk + i, 0)),
            pl.BlockSpec((POOL_BUF, r, D), lambda i: (0, i, 0)),
        ],
        out_shape=[
            jax.ShapeDtypeStruct((T_PAD, D), F32),
            jax.ShapeDtypeStruct((POOL_BUF, DEC_BATCH, D), F32),
        ],
        input_output_aliases={0: 0},
        compiler_params=_cparams("arbitrary"),
        name="mixer_b_sample",
    )(x_all, *pend_args, mod_s, mod_s, mod_s, ng, state, w_in, w_grp, b_grp, scale, w_out)
    return x_all, st_p, jnp.transpose(st_s, (1, 0, 2))


def _mix_c_glu(x, sh, sc, ng, win_ref, bin_ref):
    h = (_rms(x, ng) * (1.0 + sc) + sh).astype(BF16)
    ag = _dot(h, win_ref[...]) + bin_ref[...]
    return ag[:, :D] * jax.nn.sigmoid(ag[:, D:])


def _mix_c_tail(conv, lng_ref, lnb_ref, wout_ref, bout_ref):
    z = _ln(conv, lng_ref[...], lnb_ref[...])
    z = z * jax.nn.sigmoid(z)
    return _dot(z.astype(BF16), wout_ref[...]) + bout_ref[...]


def _mix_c_prompt_kernel(*refs, pending):
    x, refs = _residual_in(refs, pending)
    (sh_ref, sc_ref, gt_ref, ng_ref, win_ref, bin_ref, wdw_ref, bdw_ref, lng_ref, lnb_ref,
     wout_ref, bout_ref, x1_ref, st_ref, full_ref, conv_ref, *shift_refs) = refs
    i = pl.program_id(0)
    j = i % TILES_PER_SEQ
    glu = _mix_c_glu(x, sh_ref[...], sc_ref[...], ng_ref[...], win_ref, bin_ref)

    @pl.when(j == 0)
    def _():
        full_ref[0:CONV_CARRY, :] = jnp.zeros((CONV_CARRY, D), F32)

    full_ref[CONV_CARRY:CONV_CARRY + TM, :] = glu
    off = CONV_CARRY - CONV_BUF
    span = TM + (off + CONV_BUF) // SUBLANES * SUBLANES - SUBLANES
    for c in range(D // LANES):
        cols = slice(c * LANES, (c + 1) * LANES)
        shift_ref = shift_refs[c % len(shift_refs)]
        for r in range(1, SUBLANES):
            shift_ref[r - 1] = full_ref[pl.ds(r, span), cols]
        for rb in range(TM // CONV_ROWS):
            acc = None
            for k in range(CONV_WIDTH):
                q, r = divmod(off + k, SUBLANES)
                start = q * SUBLANES + rb * CONV_ROWS
                if r == 0:
                    src = full_ref[pl.ds(start, CONV_ROWS), cols]
                else:
                    src = shift_ref[r - 1, pl.ds(start, CONV_ROWS), :]
                term = src * wdw_ref[k:k + 1, cols]
                acc = term if acc is None else acc + term
            conv_ref[pl.ds(rb * CONV_ROWS, CONV_ROWS), cols] = acc + bdw_ref[:, cols]
    y = _mix_c_tail(conv_ref[...], lng_ref, lnb_ref, wout_ref, bout_ref)
    x1_ref[...] = x + gt_ref[...] * y

    @pl.when(j == TILES_PER_SEQ - 1)
    def _():
        st_ref[...] = full_ref[pl.ds(CONV_CARRY + TM - CONV_BUF, CONV_BUF), :]

    full_ref[0:CONV_CARRY, :] = full_ref[pl.ds(TM, CONV_CARRY), :]


def _mix_c_sample_kernel(*refs, pending):
    x, refs = _residual_in(refs, pending)
    (sh_ref, sc_ref, gt_ref, ng_ref, st_ref, win_ref, bin_ref, wdw_ref, bdw_ref, lng_ref,
     lnb_ref, wout_ref, bout_ref, x1_ref, nst_ref) = refs
    glu = _mix_c_glu(x, sh_ref[...], sc_ref[...], ng_ref[...], win_ref, bin_ref)
    acc = glu * wdw_ref[CONV_BUF:CONV_BUF + 1, :]
    for k in range(CONV_BUF):
        acc = acc + st_ref[k] * wdw_ref[k:k + 1, :]
    y = _mix_c_tail(acc + bdw_ref[...], lng_ref, lnb_ref, wout_ref, bout_ref)
    x1_ref[...] = x + gt_ref[...] * y
    for k in range(1, CONV_BUF):
        nst_ref[k - 1] = st_ref[k]
    nst_ref[CONV_BUF - 1] = glu


def _mixer_c(x_all, pending, mod_p, mod_s, ng, layer, lc, state, w_in, b_in, w_dw, b_dw, ln_g,
             ln_b, w_out, b_out):
    common = [
        _layer_block((D, 2 * D), lc), _layer_block((1, 2 * D), lc),
        _layer_block((CONV_WIDTH, D), lc), _layer_block((1, D), lc), _layer_block((1, D), lc),
        _layer_block((1, D), lc), _layer_block((D, D), lc), _layer_block((1, D), lc),
    ]
    has_pending = pending is not None
    pend_specs, pend_args = _pending_prompt(pending, mod_p, layer)
    x_all, st_p = pl.pallas_call(
        functools.partial(_mix_c_prompt_kernel, pending=has_pending),
        grid=(N_PROMPT_TILES,),
        in_specs=[pl.BlockSpec((TM, D), lambda i: (i, 0))] + pend_specs
        + _mod_specs_prompt(layer, (0, 1, 2)) + [_full((1, D))] + common,
        out_specs=[
            pl.BlockSpec((TM, D), lambda i: (i, 0)),
            pl.BlockSpec((None, CONV_BUF, D), lambda i: (i // TILES_PER_SEQ, 0, 0)),
        ],
        out_shape=[
            jax.ShapeDtypeStruct((T_PAD, D), F32),
            jax.ShapeDtypeStruct((BATCH, CONV_BUF, D), F32),
        ],
        scratch_shapes=[pltpu.VMEM((TM + CONV_CARRY, D), F32), pltpu.VMEM((TM, D), F32)]
        + [pltpu.VMEM((SUBLANES - 1, TM + CONV_CARRY - SUBLANES, LANES), F32)] * 2,
        input_output_aliases={0: 0},
        compiler_params=_cparams("arbitrary"),
        name="mixer_c_prompt",
    )(x_all, *pend_args, mod_p, mod_p, mod_p, ng, w_in, b_in, w_dw, b_dw, ln_g, ln_b, w_out,
      b_out)

    r = SAMPLE_ROWS
    sblk = T_PROMPT // r
    pend_specs, pend_args = _pending_sample(pending, mod_s, layer, r)
    x_all, st_s = pl.pallas_call(
        functools.partial(_mix_c_sample_kernel, pending=has_pending),
        grid=(DEC_BATCH // r,),
        in_specs=[pl.BlockSpec((r, D), lambda i: (sblk + i, 0))] + pend_specs
        + _mod_specs_sample(layer, (0, 1, 2), r) + [_full((1, D))]
        + [pl.BlockSpec((None, CONV_BUF, r, D), lambda i: (lc, 0, i, 0))] + common,
        out_specs=[
            pl.BlockSpec((r, D), lambda i: (sblk + i, 0)),
            pl.BlockSpec((CONV_BUF, r, D), lambda i: (0, i, 0)),
        ],
        out_shape=[
            jax.ShapeDtypeStruct((T_PAD, D), F32),
            jax.ShapeDtypeStruct((CONV_BUF, DEC_BATCH, D), F32),
        ],
        input_output_aliases={0: 0},
        compiler_params=_cparams("arbitrary"),
        name="mixer_c_sample",
    )(x_all, *pend_args, mod_s, mod_s, mod_s, ng, state, w_in, b_in, w_dw, b_dw, ln_g, ln_b,
      w_out, b_out)
    return x_all, st_p, jnp.transpose(st_s, (1, 0, 2))


def _tile_mod(i, p_ref, s_ref):
    s_rows = jnp.concatenate([s_ref[...], jnp.zeros((TM - DEC_BATCH, D), F32)], axis=0)
    return jnp.where(i >= N_PROMPT_TILES, s_rows, p_ref[...])


def _unified_mod_specs(layer, chunks):
    specs = []
    for c in chunks:
        specs.append(pl.BlockSpec(
            (None, None, 1, D),
            lambda i, c=c: (layer, jnp.minimum(i // TILES_PER_SEQ, BATCH - 1), 0, c)))
        specs.append(pl.BlockSpec((None, DEC_BATCH, D), lambda i, c=c: (layer, 0, c)))
    return specs


def _router_kernel(x_ref, shp_ref, shs_ref, scp_ref, scs_ref, ng_ref, wr_ref, br_ref,
                   h2_ref, meta_ref, wts_ref, cnt_ref, carry_ref, before_ref):
    i = pl.program_id(0)

    @pl.when(i == 0)
    def _():
        carry_ref[...] = jnp.zeros((ROUTER_ROWS, 1), F32)
        earlier = lax.broadcasted_iota(jnp.int32, (TM, TM), 0)
        token = lax.broadcasted_iota(jnp.int32, (TM, TM), 1)
        before_ref[...] = (earlier < token).astype(BF16)

    sh = _tile_mod(i, shp_ref, shs_ref)
    sc = _tile_mod(i, scp_ref, scs_ref)
    h2 = _rms(x_ref[...], ng_ref[...]) * (1.0 + sc) + sh
    h2_ref[...] = _pack_bf16_pairs(h2)
    lg = lax.dot_general(wr_ref[...], h2.astype(BF16), (((1,), (1,)), ((), ())),
                         preferred_element_type=F32) + br_ref[...]
    row = lax.broadcasted_iota(jnp.int32, (ROUTER_ROWS, TM), 0)
    row_f = row.astype(F32)
    far = float(ROUTER_ROWS)
    neg = -jnp.inf
    is_g = row < N_GROUPS
    gm = jnp.where(is_g, lg, neg)
    gmax = jnp.max(gm, axis=0, keepdims=True)
    gsel = jnp.min(jnp.where(gm == gmax, row_f, far), axis=0, keepdims=True).astype(jnp.int32)
    gsum = jnp.sum(jnp.where(is_g, jnp.exp(lg - gmax), 0.0), axis=0, keepdims=True)
    g_w = 1.0 / gsum
    e_row = row - EXPERT_LANE0
    in_grp = (e_row >= 0) & (e_row < N_EXPERTS) & (
        lax.shift_right_arithmetic(e_row, EXPERTS_PER_GROUP.bit_length() - 1) == gsel)
    em = jnp.where(in_grp, lg, neg)
    m1 = jnp.max(em, axis=0, keepdims=True)
    i1 = jnp.min(jnp.where(em == m1, row_f, far), axis=0, keepdims=True).astype(jnp.int32)
    em2 = jnp.where(row == i1, neg, em)
    m2 = jnp.max(em2, axis=0, keepdims=True)
    i2 = jnp.min(jnp.where(em2 == m2, row_f, far), axis=0, keepdims=True).astype(jnp.int32)
    e2 = jnp.exp(m2 - m1)
    den = 1.0 + e2
    w1 = (1.0 / den) * g_w
    w2 = (e2 / den) * g_w

    hit1 = row == i1
    hit2 = row == i2
    assign = jnp.where(hit1 | hit2, 1.0, 0.0).astype(BF16)
    seen = _dot(assign, before_ref[...]) + carry_ref[...]
    rank1 = jnp.sum(jnp.where(hit1, seen, 0.0), axis=0, keepdims=True)
    rank2 = jnp.sum(jnp.where(hit2, seen, 0.0), axis=0, keepdims=True)
    carry_ref[...] = carry_ref[...] + jnp.sum(assign.astype(F32), axis=1, keepdims=True)
    cnt_ref[...] = carry_ref[...].astype(jnp.int32)

    ex1 = (i1 - EXPERT_LANE0).astype(F32)
    ex2 = (i2 - EXPERT_LANE0).astype(F32)
    field = lax.broadcasted_iota(jnp.int32, (META_ROWS, TM), 0)
    meta_ref[...] = jnp.where(field == 0, ex1, jnp.where(field == 1, ex2,
                              jnp.where(field == 2, rank1, jnp.where(field == 3, rank2, 0.0))))
    slot = lax.broadcasted_iota(jnp.int32, (LANES, TM), 0)
    wts_ref[...] = jnp.where(slot == 0, w1, jnp.where(slot == 1, w2, 0.0)).T


def _router(x_all, mod_p, mod_s, ng, layer, w_r, b_r):
    return pl.pallas_call(
        _router_kernel,
        grid=(N_TILES,),
        in_specs=[pl.BlockSpec((TM, D), lambda i: (i, 0))]
        + _unified_mod_specs(layer, (3, 4))
        + [_full((1, D)), _full((ROUTER_ROWS, D)), _full((ROUTER_ROWS, 1))],
        out_specs=[
            pl.BlockSpec((TM, D // 2), lambda i: (i, 0)),
            pl.BlockSpec((META_ROWS, TM), lambda i: (0, i)),
            pl.BlockSpec((TM, LANES), lambda i: (i, 0)),
            pl.BlockSpec((ROUTER_ROWS, 1), lambda i: (0, 0)),
        ],
        out_shape=[
            jax.ShapeDtypeStruct((T_PAD, D // 2), jnp.int32),
            jax.ShapeDtypeStruct((META_ROWS, T_PAD), F32),
            jax.ShapeDtypeStruct((T_PAD, LANES), F32),
            jax.ShapeDtypeStruct((ROUTER_ROWS, 1), jnp.int32),
        ],
        scratch_shapes=[pltpu.VMEM((ROUTER_ROWS, 1), F32), pltpu.VMEM((TM, TM), BF16)],
        compiler_params=_cparams("arbitrary"),
        name="router",
    )(x_all, mod_p, mod_s, mod_p, mod_s, ng, w_r, b_r)


SC_INDEX_MAX = 128
SC_ROWS_BYTES = 400 * 1024


def _gather_chunk(per_worker, row_bytes, buffers=1):
    top = min(SC_INDEX_MAX, SC_ROWS_BYTES // (row_bytes * buffers)) // SUBLANES * SUBLANES
    for ch in range(top, SUBLANES - 1, -SUBLANES):
        if per_worker % ch == 0:
            return ch
    raise ValueError(per_worker)


def _sc_gather(table, idx):
    nc, nw = SC_CORES, SC_CORES * SC_SUBCORES
    m = idx.shape[0]
    per_w = m // nw
    assert per_w * nw == m
    width = table.shape[1]
    ch = _gather_chunk(per_w, width * table.dtype.itemsize, buffers=2)
    mesh = plsc.VectorSubcoreMesh(core_axis_name="c", subcore_axis_name="s")

    n_chunks = per_w // ch

    @functools.partial(
        pl.kernel,
        out_type=jax.ShapeDtypeStruct((m, width), table.dtype),
        mesh=mesh,
        scratch_types=[pltpu.VMEM((ch,), jnp.int32)] * 2
        + [pltpu.VMEM((ch, width), table.dtype)] * 2
        + [pltpu.SemaphoreType.DMA] * 4,
    )
    def gather_kernel(t_hbm, i_hbm, o_hbm, idx_a, idx_b, rows_a, rows_b, sg_a, sg_b, sw_a, sw_b):
        wid = lax.axis_index("s") * nc + lax.axis_index("c")
        base = wid * per_w
        idx_v, rows_v, sem_g, sem_w = (idx_a, idx_b), (rows_a, rows_b), (sg_a, sg_b), (sw_a, sw_b)

        def out_rows(j):
            return o_hbm.at[pl.ds(pl.multiple_of(base + j * ch, 8), ch)]

        gathers, writes = {}, {}
        for j in range(n_chunks):
            b = j % 2
            if j >= 2:
                writes[j - 2].wait()
            pltpu.sync_copy(i_hbm.at[pl.ds(pl.multiple_of(base + j * ch, 8), ch)], idx_v[b])
            gathers[j] = pltpu.async_copy(t_hbm.at[idx_v[b]], rows_v[b], sem_g[b])
            if j >= 1:
                gathers[j - 1].wait()
                writes[j - 1] = pltpu.async_copy(rows_v[1 - b], out_rows(j - 1), sem_w[1 - b])
        last = n_chunks - 1
        gathers[last].wait()
        writes[last] = pltpu.async_copy(rows_v[last % 2], out_rows(last), sem_w[last % 2])
        for j in range(max(last - 1, 0), n_chunks):
            writes[j].wait()

    return gather_kernel(table, idx)


def _sc_scatter2(rows, dest0, dest1, n_out):
    nc, nw = SC_CORES, SC_CORES * SC_SUBCORES
    n, width = rows.shape
    per_w = n // nw
    assert per_w * nw == n
    ch = _gather_chunk(per_w, width * rows.dtype.itemsize)
    mesh = plsc.VectorSubcoreMesh(core_axis_name="c", subcore_axis_name="s")

    @functools.partial(
        pl.kernel,
        out_type=jax.ShapeDtypeStruct((n_out, width), rows.dtype),
        mesh=mesh,
        scratch_types=[
            pltpu.VMEM((ch,), jnp.int32),
            pltpu.VMEM((ch,), jnp.int32),
            pltpu.VMEM((ch, width), rows.dtype),
            pltpu.SemaphoreType.DMA,
            pltpu.SemaphoreType.DMA,
        ],
    )
    def scatter_kernel(r_hbm, d0_hbm, d1_hbm, o_hbm, i0_v, i1_v, rows_v, sem0, sem1):
        wid = lax.axis_index("s") * nc + lax.axis_index("c")
        base = wid * per_w

        @pl.loop(0, per_w // ch)
        def _(j):
            off = pl.multiple_of(base + j * ch, 8)
            pltpu.sync_copy(d0_hbm.at[pl.ds(off, ch)], i0_v)
            pltpu.sync_copy(d1_hbm.at[pl.ds(off, ch)], i1_v)
            pltpu.sync_copy(r_hbm.at[pl.ds(off, ch)], rows_v)
            c0 = pltpu.async_copy(rows_v, o_hbm.at[i0_v], sem0)
            c1 = pltpu.async_copy(rows_v, o_hbm.at[i1_v], sem1)
            c0.wait()
            c1.wait()

    return scatter_kernel(rows, dest0, dest1)


def _expert_kernel(be_ref, first_ref, nvalid_ref, next_ref, nused_ref, x_ref, wg_hbm, wu_hbm,
                   wd_hbm, y_ref, wgf_ref, wuf_ref, wdf_ref, wgb_ref, wub_ref, wdb_ref, sems,
                   *, layer):
    b = pl.program_id(0)

    def weight_copies(e):
        return (
            pltpu.make_async_copy(wg_hbm.at[layer, e], wgf_ref, sems.at[0]),
            pltpu.make_async_copy(wu_hbm.at[layer, e], wuf_ref, sems.at[1]),
            pltpu.make_async_copy(wd_hbm.at[layer, e], wdf_ref, sems.at[2]),
        )

    @pl.when(b == 0)
    def _():
        for cp in weight_copies(be_ref[0]):
            cp.start()

    @pl.when(b < nused_ref[0])
    def _():
        @pl.when(first_ref[b] == 1)
        def _():
            for cp in weight_copies(be_ref[b]):
                cp.wait()
            wgb_ref[...] = wgf_ref[...].astype(BF16)
            wub_ref[...] = wuf_ref[...].astype(BF16)
            wdb_ref[...] = wdf_ref[...].astype(BF16)

            @pl.when(next_ref[b] >= 0)
            def _():
                for cp in weight_copies(next_ref[b]):
                    cp.start()

        def run(rows):
            live = lax.broadcasted_iota(jnp.int32, (rows, 1), 0) < nvalid_ref[b]
            x = _unpack_bf16_pairs(jnp.where(live, x_ref[0:rows, :], 0), BF16)
            g = _dot(x, wgb_ref[...])
            u = _dot(x, wub_ref[...])
            hmid = (g * jax.nn.sigmoid(g)) * u
            y_ref[0:rows, :] = _pack_bf16_pairs(_dot(hmid.astype(BF16), wdb_ref[...]))

        half = BM // 2

        @pl.when(nvalid_ref[b] > half)
        def _():
            run(BM)

        @pl.when(nvalid_ref[b] <= half)
        def _():
            run(half)
            y_ref[half:BM, :] = jnp.zeros((BM - half, D // 2), jnp.int32)

    @pl.when(b >= nused_ref[0])
    def _():
        y_ref[...] = jnp.zeros((BM, D // 2), jnp.int32)


def _experts(x_sorted, block_expert, first, nvalid, next_expert, nused, layer, w_gate, w_up,
             w_down):
    def row_map(b, be, fi, nv, nx, nu):
        return (jnp.minimum(b, nu[0] - 1), 0)

    def out_map(b, be, fi, nv, nx, nu):
        return (b, 0)

    grid_spec = pltpu.PrefetchScalarGridSpec(
        num_scalar_prefetch=5,
        grid=(N_BLOCKS,),
        in_specs=[
            pl.BlockSpec((BM, D // 2), row_map),
            pl.BlockSpec(memory_space=pl.ANY),
            pl.BlockSpec(memory_space=pl.ANY),
            pl.BlockSpec(memory_space=pl.ANY),
        ],
        out_specs=pl.BlockSpec((BM, D // 2), out_map),
        scratch_shapes=[
            pltpu.VMEM((D, D_EXPERT), F32),
            pltpu.VMEM((D, D_EXPERT), F32),
            pltpu.VMEM((D_EXPERT, D), F32),
            pltpu.VMEM((D, D_EXPERT), BF16),
            pltpu.VMEM((D, D_EXPERT), BF16),
            pltpu.VMEM((D_EXPERT, D), BF16),
            pltpu.SemaphoreType.DMA((3,)),
        ],
    )
    return pl.pallas_call(
        functools.partial(_expert_kernel, layer=layer),
        grid_spec=grid_spec,
        out_shape=jax.ShapeDtypeStruct((P_ROWS, D // 2), jnp.int32),
        compiler_params=_cparams("arbitrary"),
        name="experts",
    )(block_expert, first, nvalid, next_expert, nused, x_sorted, w_gate, w_up, w_down)


def _combine_final_kernel(x_ref, g0_ref, g1_ref, wts_ref, gtp_ref, gts_ref, fg_ref, yp_ref,
                          ys_ref):
    i = pl.program_id(0)
    gate = _tile_mod(i, gtp_ref, gts_ref)
    y = _rms(_plus_moe(x_ref[...], g0_ref, g1_ref, wts_ref, gate), fg_ref[...])

    @pl.when(i < N_PROMPT_TILES)
    def _():
        yp_ref[...] = y

    @pl.when(i == N_PROMPT_TILES)
    def _():
        ys_ref[...] = y[0:DEC_BATCH, :]


def _combine_final(x_all, gathered, wts, mod_p, mod_s, layer, final_g):
    in_specs = [
        pl.BlockSpec((TM, D), lambda i: (i, 0)),
        pl.BlockSpec((TM, D // 2), lambda i: (i, 0)),
        pl.BlockSpec((TM, D // 2), lambda i: (i + N_TILES, 0)),
        pl.BlockSpec((TM, LANES), lambda i: (i, 0)),
    ] + _unified_mod_specs(layer, (5,))
    return pl.pallas_call(
        _combine_final_kernel,
        grid=(N_PROMPT_TILES + 1,),
        in_specs=in_specs + [_full((1, D))],
        out_specs=[
            pl.BlockSpec((TM, D), lambda i: (jnp.minimum(i, N_PROMPT_TILES - 1), 0)),
            pl.BlockSpec((DEC_BATCH, D), lambda i: (0, 0)),
        ],
        out_shape=[
            jax.ShapeDtypeStruct((T_PROMPT, D), F32),
            jax.ShapeDtypeStruct((DEC_BATCH, D), F32),
        ],
        compiler_params=_cparams("arbitrary"),
        name="combine_final",
    )(x_all, gathered, gathered, wts, mod_p, mod_s, final_g)


def _plan_kernel(cnt_ref, meta_ref, d0_ref, d1_ref, be_ref, first_ref, nvalid_ref, next_ref,
                 nused_ref, pstart_ref, nxt_ref):
    shift = BM.bit_length() - 1
    assert 1 << shift == BM

    def count(e):
        return cnt_ref[EXPERT_LANE0 + e, 0]

    def scan_back(k, nxt):
        e = N_EXPERTS - 1 - k
        nxt_ref[e] = nxt
        return jnp.where(count(e) > 0, e, nxt)

    lax.fori_loop(0, N_EXPERTS, scan_back, jnp.int32(-1))

    def fill(e, blk):
        c = count(e)
        pstart_ref[e] = blk << shift

        def one(j, carry):
            be_ref[blk + j] = e
            first_ref[blk + j] = jnp.where(j == 0, 1, 0)
            nvalid_ref[blk + j] = jnp.minimum(c - (j << shift), BM)
            next_ref[blk + j] = nxt_ref[e]
            return carry

        n_blk = (c + (BM - 1)) >> shift
        lax.fori_loop(0, n_blk, one, 0)
        return blk + n_blk

    used = lax.fori_loop(0, N_EXPERTS, fill, jnp.int32(0))
    nused_ref[0] = used

    def tail(b, carry):
        be_ref[b] = 0
        first_ref[b] = 0
        nvalid_ref[b] = 0
        next_ref[b] = -1
        return carry

    lax.fori_loop(used, N_BLOCKS, tail, 0)

    m = meta_ref[...]
    expert = m[0:2, :]
    dest = m[2:4, :]
    for e in range(N_EXPERTS):
        dest = dest + jnp.where(expert == float(e), pstart_ref[e].astype(F32), 0.0)
    dest = dest.astype(jnp.int32)
    d0_ref[...] = dest[0:1, :]
    d1_ref[...] = dest[1:2, :]


def _plan(cnt, meta):
    smem = pl.BlockSpec(memory_space=pltpu.SMEM)
    blocks = jax.ShapeDtypeStruct((N_BLOCKS,), jnp.int32)
    return pl.pallas_call(
        _plan_kernel,
        grid=(1,),
        in_specs=[smem, _full((META_ROWS, T_PAD))],
        out_specs=[_full((1, T_PAD)), _full((1, T_PAD)), smem, smem, smem, smem, smem],
        out_shape=[
            jax.ShapeDtypeStruct((1, T_PAD), jnp.int32),
            jax.ShapeDtypeStruct((1, T_PAD), jnp.int32),
            blocks, blocks, blocks, blocks,
            jax.ShapeDtypeStruct((1,), jnp.int32),
        ],
        scratch_shapes=[pltpu.SMEM((N_EXPERTS,), jnp.int32), pltpu.SMEM((N_EXPERTS,), jnp.int32)],
        compiler_params=_cparams("arbitrary"),
        name="plan",
    )(cnt, meta)


def _moe(x_all, mod_p, mod_s, ng, layer, w_r, b_r, w_gate, w_up, w_down):
    h2, meta, wts, cnt = _router(x_all, mod_p, mod_s, ng, layer, w_r, b_r)
    dest0, dest1, block_expert, first, nvalid, next_expert, nused = _plan(cnt, meta)
    dest0, dest1 = dest0.reshape(T_PAD), dest1.reshape(T_PAD)
    x_sorted = _sc_scatter2(h2, dest0, dest1, P_ROWS)
    y_sorted = _experts(x_sorted, block_expert, first, nvalid, next_expert, nused, layer,
                        w_gate, w_up, w_down)
    gathered = _sc_gather(y_sorted, jnp.concatenate([dest0, dest1]))
    return gathered, wts


def kernel(x_prompt, x_sample, c_prompt, c_sample, state_pool, state_conv, ada_w, ada_b, norm_g, final_g, a_w_in, a_ln_g, a_ln_b, a_w_s, a_b_s, a_w_out, b_w_in, b_w_grp, b_b_grp, b_scale, b_w_out, c_w_in, c_b_in, c_w_dw, c_b_dw, c_ln_g, c_ln_b, c_w_out, c_b_out, moe_w_grp, moe_b_grp, moe_w_exp, moe_b_exp, moe_w_gate, moe_w_up, moe_w_down):
    x_all = (x_prompt.reshape(T_PROMPT, D), x_sample.reshape(DEC_BATCH, D))
    mod_p, mod_s = _ada_mods(jnp.concatenate([c_prompt, c_sample], axis=0), ada_w, ada_b)
    mod_p = mod_p.reshape(DEPTH, BATCH, 1, 6 * D)

    a_w_in_b, a_w_out_b = a_w_in.astype(BF16), a_w_out.astype(BF16)
    b_w_in_b, b_w_grp_b, b_w_out_b = b_w_in.astype(BF16), b_w_grp.astype(BF16), b_w_out.astype(BF16)
    c_w_in_b, c_w_out_b = c_w_in.astype(BF16), c_w_out.astype(BF16)
    pad_rows = ROUTER_ROWS - N_GROUPS - N_EXPERTS
    w_r = jnp.pad(jnp.swapaxes(jnp.concatenate([moe_w_grp, moe_w_exp], axis=2), 1, 2),
                  ((0, 0), (0, pad_rows), (0, 0))).astype(BF16)
    b_r = jnp.pad(jnp.concatenate([moe_b_grp, moe_b_exp], axis=1), ((0, 0), (0, pad_rows)))

    new_a_p, new_a_s, new_b_p, new_b_s, new_c_p, new_c_s = [], [], [], [], [], []
    pool_tm = jnp.transpose(state_pool, (0, 2, 1, 3))
    conv_tm = jnp.transpose(state_conv, (0, 2, 1, 3))
    ia = ib = ic = 0
    assert len(range(0, DEPTH, 3)) == 2
    pending = None
    for layer in range(DEPTH):
        ng1 = norm_g[layer, 0].reshape(1, D)
        ng2 = norm_g[layer, 1].reshape(1, D)
        kind = layer % 3
        if kind == 0:
            x_all, st_p, st_s = _mixer_a(
                x_all, pending, new_a_p[0] if new_a_p else None, mod_p, mod_s, ng1, layer, ia,
                a_w_in_b, a_ln_g.reshape(-1, 1, D), a_ln_b.reshape(-1, 1, D), a_w_s, a_b_s,
                a_w_out_b)
            new_a_p = [st_p]
            new_a_s.append(st_s)
            ia += 1
        elif kind == 1:
            x_all, st_p, st_s = _mixer_b(
                x_all, pending, mod_p, mod_s, ng1, layer, ib, pool_tm, b_w_in_b, b_w_grp_b,
                b_b_grp.reshape(-1, len(POOL_WINDOWS), 1, B_GROUP_DIM),
                b_scale.reshape(-1, 1, D), b_w_out_b)
            new_b_p.append(st_p)
            new_b_s.append(st_s)
            ib += 1
        else:
            x_all, st_p, st_s = _mixer_c(
                x_all, pending, mod_p, mod_s, ng1, layer, ic, conv_tm, c_w_in_b,
                c_b_in.reshape(-1, 1, 2 * D), c_w_dw, c_b_dw.reshape(-1, 1, D),
                c_ln_g.reshape(-1, 1, D), c_ln_b.reshape(-1, 1, D), c_w_out_b,
                c_b_out.reshape(-1, 1, D))
            new_c_p.append(st_p)
            new_c_s.append(st_s)
            ic += 1
        pending = _moe(x_all, mod_p, mod_s, ng2, layer, w_r[layer],
                       b_r[layer].reshape(ROUTER_ROWS, 1), moe_w_gate, moe_w_up, moe_w_down)

    y_p, y_s = _combine_final(x_all, *pending, mod_p, mod_s, DEPTH - 1, final_g.reshape(1, D))
    return (y_p.reshape(BATCH, SEQ, D), y_s.reshape(DEC_BATCH, 1, D),
            new_a_p[0], jnp.stack(new_a_s), jnp.stack(new_b_p), jnp.stack(new_b_s),
            jnp.stack(new_c_p), jnp.stack(new_c_s))
```

```python
import functools

import jax
import jax.numpy as jnp
from jax import lax
from jax.experimental import pallas as pl
from jax.experimental.pallas import tpu as pltpu
from jax.experimental.pallas import tpu_sc as plsc

F32 = jnp.float32
BF16 = jnp.bfloat16

D = 1024
BATCH = 8
SEQ = 2048
DEPTH = 4
DEC_BATCH = 128
PAST_LEN = 16384
CHUNK = 128
A_HEADS = 8
POOL_WINDOWS = (2, 4, 8, 16)
B_GROUP_DIM = D // len(POOL_WINDOWS)
POOL_BUF = max(POOL_WINDOWS) - 1
CONV_WIDTH = 31
CONV_BUF = CONV_WIDTH - 1
N_GROUPS = 4
EXPERTS_PER_GROUP = 8
N_EXPERTS = N_GROUPS * EXPERTS_PER_GROUP
D_EXPERT = D // 2
EPS = 1e-6

LANES = 128
SUBLANES = 8
CONV_ROWS = 128
SC_CORES = 2
SC_SUBCORES = 16
TM = 512
TILES_PER_SEQ = SEQ // TM
T_PROMPT = BATCH * SEQ
T_ALL = T_PROMPT + DEC_BATCH
N_TILES = -(-T_ALL // TM)
T_PAD = N_TILES * TM
N_PROMPT_TILES = T_PROMPT // TM
SAMPLE_ROWS = 32
BM = 512
EXPERT_ROW_GROUP = 128
N_BLOCKS = -(-(2 * T_PAD) // BM) + N_EXPERTS
P_ROWS = N_BLOCKS * BM
META_ROWS = 8
POOL_CARRY = 16
CONV_CARRY = 32
EXPERT_LANE0 = N_GROUPS
ROUTER_ROWS = 48
VMEM_LIMIT = 56 * 1024 * 1024


def _cparams(*sem):
    return pltpu.CompilerParams(dimension_semantics=sem, vmem_limit_bytes=VMEM_LIMIT)


def _rms(x, g):
    return x * lax.rsqrt(jnp.mean(x * x, axis=-1, keepdims=True) + EPS) * g


def _ln(x, g, b):
    mu = jnp.mean(x, axis=-1, keepdims=True)
    xc = x - mu
    var = jnp.mean(xc * xc, axis=-1, keepdims=True)
    return xc * lax.rsqrt(var + EPS) * g + b


def _dot(a, b):
    return jnp.dot(a, b, preferred_element_type=F32)


def _pack_bf16_pairs(v):
    w = v.shape[1] // 2
    lo = lax.bitcast_convert_type(v[:, :w].astype(BF16).astype(F32), jnp.uint32)
    hi = lax.bitcast_convert_type(v[:, w:].astype(BF16).astype(F32), jnp.uint32)
    return lax.bitcast_convert_type(lax.shift_right_logical(lo, jnp.uint32(16)) | hi, jnp.int32)


def _unpack_bf16_pairs(words, dtype):
    u = lax.bitcast_convert_type(words, jnp.uint32)
    lo = lax.bitcast_convert_type(lax.shift_left(u, jnp.uint32(16)), F32)
    hi = lax.bitcast_convert_type(u & jnp.uint32(0xFFFF0000), F32)
    return jnp.concatenate([lo.astype(dtype), hi.astype(dtype)], axis=1)


def _ada_kernel(c_ref, w_ref, b_ref, op_ref, os_ref):
    c = c_ref[...]
    s = (c * jax.nn.sigmoid(c)).astype(BF16)
    mod = _dot(s, w_ref[...].astype(BF16)) + b_ref[...]
    op_ref[...] = mod[0:BATCH, :]
    os_ref[...] = mod[BATCH:, :]


def _ada_mods(c_all, ada_w, ada_b):
    n = c_all.shape[0]
    assert n == BATCH + DEC_BATCH and BATCH % SUBLANES == 0
    tn = 2048
    return pl.pallas_call(
        _ada_kernel,
        grid=(DEPTH, 6 * D // tn),
        in_specs=[
            pl.BlockSpec((n, D), lambda l, j: (0, 0)),
            pl.BlockSpec((None, D, tn), lambda l, j: (l, 0, j)),
            pl.BlockSpec((None, 1, tn), lambda l, j: (l, 0, j)),
        ],
        out_specs=[
            pl.BlockSpec((None, BATCH, tn), lambda l, j: (l, 0, j)),
            pl.BlockSpec((None, DEC_BATCH, tn), lambda l, j: (l, 0, j)),
        ],
        out_shape=[
            jax.ShapeDtypeStruct((DEPTH, BATCH, 6 * D), F32),
            jax.ShapeDtypeStruct((DEPTH, DEC_BATCH, 6 * D), F32),
        ],
        compiler_params=_cparams("parallel", "parallel"),
        name="ada_mods",
    )(c_all, ada_w, ada_b.reshape(DEPTH, 1, 6 * D))


def _mod_specs_prompt(layer, chunks):
    return [
        pl.BlockSpec((None, None, 1, D),
                     lambda i, c=c: (layer, jnp.minimum(i // TILES_PER_SEQ, BATCH - 1), 0, c))
        for c in chunks
    ]


def _mod_specs_sample(layer, chunks, rows, row_block0=0):
    return [
        pl.BlockSpec((None, rows, D), lambda i, c=c: (layer, i + row_block0, c)) for c in chunks
    ]


def _full(shape):
    nd = len(shape)
    return pl.BlockSpec(shape, lambda i: (0,) * nd)


def _layer_block(shape, layer):
    nd = len(shape)
    return pl.BlockSpec((None,) + shape, lambda i: (layer,) + (0,) * nd)


def _plus_moe(x, g0_ref, g1_ref, wts_ref, gate):
    w = wts_ref[...]
    y0 = _unpack_bf16_pairs(g0_ref[...], F32)
    y1 = _unpack_bf16_pairs(g1_ref[...], F32)
    return x + gate * (w[:, 0:1] * y0 + w[:, 1:2] * y1)


def _residual_in(refs, pending):
    if not pending:
        return refs[0][...], refs[1:]
    x_ref, g0_ref, g1_ref, wts_ref, gate_ref = refs[:5]
    return _plus_moe(x_ref[...], g0_ref, g1_ref, wts_ref, gate_ref[...]), refs[5:]


def _pending_prompt(pending, mod_p, layer):
    if pending is None:
        return [], []
    gathered, wts = pending
    specs = [
        pl.BlockSpec((TM, D // 2), lambda i: (i, 0)),
        pl.BlockSpec((TM, D // 2), lambda i: (i + N_TILES, 0)),
        pl.BlockSpec((TM, LANES), lambda i: (i, 0)),
    ] + _mod_specs_prompt(layer - 1, (5,))
    return specs, [gathered, gathered, wts, mod_p]


def _pending_sample(pending, mod_s, layer, rows):
    if pending is None:
        return [], []
    gathered, wts = pending
    first = T_PROMPT // rows
    second = (T_PAD + T_PROMPT) // rows
    specs = [
        pl.BlockSpec((rows, D // 2), lambda i: (first + i, 0)),
        pl.BlockSpec((rows, D // 2), lambda i: (second + i, 0)),
        pl.BlockSpec((rows, LANES), lambda i: (first + i, 0)),
    ] + _mod_specs_sample(layer - 1, (5,), rows)
    return specs, [gathered, gathered, wts, mod_s]


def _mix_a_front(x, sh, sc, ng, win_ref, lng_ref, lnb_ref):
    h = (_rms(x, ng) * (1.0 + sc) + sh).astype(BF16)
    z = jax.nn.gelu(_dot(h, win_ref[...]))
    u = z[:, :D]
    v = _ln(z[:, D:], lng_ref[...], lnb_ref[...])
    return u, v


def _mix_a_prompt_kernel(*refs, pending, first, stack):
    x, refs = _residual_in(refs, pending)
    (sh_ref, sc_ref, gt_ref, ng_ref, win_ref, lng_ref, lnb_ref, ws_ref, bs_ref, wout_ref,
     *cv_prev_ref, x1_ref, cv_ref, mixed_ref) = refs
    i = pl.program_id(0)
    u, v = _mix_a_front(x, sh_ref[...], sc_ref[...], ng_ref[...], win_ref, lng_ref, lnb_ref)
    vb = v.astype(BF16)
    n_chunks = TM // CHUNK
    row = lax.broadcasted_iota(jnp.int32, (CHUNK, CHUNK), 0)
    col = lax.broadcasted_iota(jnp.int32, (CHUNK, CHUNK), 1)
    tril = row >= col
    hd_dim = D // A_HEADS
    for hd in range(A_HEADS):
        ws = jnp.where(tril, ws_ref[hd], 0.0).astype(BF16)
        cols = slice(hd * hd_dim, (hd + 1) * hd_dim)
        vcat = jnp.concatenate([vb[c * CHUNK:(c + 1) * CHUNK, cols] for c in range(n_chunks)], axis=1)
        m = _dot(ws, vcat)
        for c in range(n_chunks):
            mixed_ref[c * CHUNK:(c + 1) * CHUNK, cols] = m[:, c * hd_dim:(c + 1) * hd_dim] + bs_ref[hd]
    y = _dot((u * mixed_ref[...]).astype(BF16), wout_ref[...])
    out = x + gt_ref[...] * y
    if first:
        out = jnp.where(i < N_PROMPT_TILES, out, 0.0)
    x1_ref[...] = out

    @pl.when(i % TILES_PER_SEQ == TILES_PER_SEQ - 1)
    def _():
        if stack:
            cv_ref[0] = cv_prev_ref[0][...]
            cv_ref[1] = v[TM - CHUNK:, :]
        else:
            cv_ref[...] = v[TM - CHUNK:, :]


def _mix_a_sample_kernel(*refs, pending, first):
    if first:
        refs = refs[:1] + refs[2:]
    x, refs = _residual_in(refs, pending)
    (sh_ref, sc_ref, gt_ref, ng_ref, win_ref, lng_ref, lnb_ref, wsd_ref, bsd_ref, wout_ref,
     x1_ref, cv_ref) = refs
    u, v = _mix_a_front(x, sh_ref[...], sc_ref[...], ng_ref[...], win_ref, lng_ref, lnb_ref)
    mixed = wsd_ref[...] * v + bsd_ref[...]
    y = _dot((u * mixed).astype(BF16), wout_ref[...])
    x1_ref[...] = x + gt_ref[...] * y
    cv_ref[...] = v


def _mixer_a(x_in, pending, cv_prev, mod_p, mod_s, ng, layer, la, w_in, ln_g, ln_b, w_s, b_s,
             w_out):
    first = isinstance(x_in, tuple)
    stack = cv_prev is not None
    assert not stack or la == 1
    x_all = x_in[0] if first else x_in
    bs_b = jnp.broadcast_to(b_s[la][:, :, None], (A_HEADS, CHUNK, D // A_HEADS))
    common = [
        _layer_block((D, 2 * D), la), _layer_block((1, D), la), _layer_block((1, D), la),
    ]
    has_pending = pending is not None
    pend_specs, pend_args = _pending_prompt(pending, mod_p, layer)
    last_tile = N_PROMPT_TILES - 1

    def seq_of(i):
        return jnp.minimum(i, last_tile) // TILES_PER_SEQ

    if stack:
        cv_in_specs = [pl.BlockSpec((None, CHUNK, D), lambda i: (seq_of(i), 0, 0))]
        cv_in_args = [cv_prev]
        cv_spec = pl.BlockSpec((2, None, CHUNK, D), lambda i: (0, seq_of(i), 0, 0))
        cv_shape = jax.ShapeDtypeStruct((2, BATCH, CHUNK, D), F32)
    else:
        cv_in_specs, cv_in_args = [], []
        cv_spec = pl.BlockSpec((None, CHUNK, D), lambda i: (seq_of(i), 0, 0))
        cv_shape = jax.ShapeDtypeStruct((BATCH, CHUNK, D), F32)
    x_all, cv_p = pl.pallas_call(
        functools.partial(_mix_a_prompt_kernel, pending=has_pending, first=first, stack=stack),
        grid=(N_TILES if first else N_PROMPT_TILES,),
        in_specs=[pl.BlockSpec((TM, D), lambda i: (jnp.minimum(i, last_tile), 0))] + pend_specs
        + _mod_specs_prompt(layer, (0, 1, 2))
        + [_full((1, D))] + common
        + [_layer_block((A_HEADS, CHUNK, CHUNK), la), _full((A_HEADS, CHUNK, D // A_HEADS)),
           _layer_block((D, D), la)] + cv_in_specs,
        out_specs=[pl.BlockSpec((TM, D), lambda i: (i, 0)), cv_spec],
        out_shape=[jax.ShapeDtypeStruct((T_PAD, D), F32), cv_shape],
        scratch_shapes=[pltpu.VMEM((TM, D), F32)],
        input_output_aliases={} if first else {0: 0},
        compiler_params=_cparams("arbitrary"),
        name="mixer_a_prompt",
    )(x_all, *pend_args, mod_p, mod_p, mod_p, ng, w_in, ln_g, ln_b, w_s, bs_b, w_out,
      *cv_in_args)

    wsd = jnp.repeat(w_s[la, :, 0, 0], D // A_HEADS).reshape(1, D)
    bsd = jnp.repeat(b_s[la, :, 0], D // A_HEADS).reshape(1, D)
    sblk = T_PROMPT // DEC_BATCH
    pend_specs, pend_args = _pending_sample(pending, mod_s, layer, DEC_BATCH)
    if first:
        x_specs = [pl.BlockSpec((DEC_BATCH, D), lambda i: (0, 0)), pl.BlockSpec(memory_space=pl.ANY)]
        x_args = [x_in[1], x_all]
    else:
        x_specs = [pl.BlockSpec((DEC_BATCH, D), lambda i: (sblk, 0))]
        x_args = [x_all]
    x_all, cv_s = pl.pallas_call(
        functools.partial(_mix_a_sample_kernel, pending=has_pending, first=first),
        grid=(1,),
        in_specs=x_specs + pend_specs
        + _mod_specs_sample(layer, (0, 1, 2), DEC_BATCH)
        + [_full((1, D))] + common
        + [_full((1, D)), _full((1, D)), _layer_block((D, D), la)],
        out_specs=[
            pl.BlockSpec((DEC_BATCH, D), lambda i: (sblk, 0)),
            pl.BlockSpec((DEC_BATCH, D), lambda i: (0, 0)),
        ],
        out_shape=[
            jax.ShapeDtypeStruct((T_PAD, D), F32),
            jax.ShapeDtypeStruct((DEC_BATCH, D), F32),
        ],
        input_output_aliases={len(x_args) - 1: 0},
        compiler_params=_cparams("arbitrary"),
        name="mixer_a_sample",
    )(*x_args, *pend_args, mod_s, mod_s, mod_s, ng, w_in, ln_g, ln_b, wsd, bsd, w_out)
    return x_all, cv_p, cv_s.reshape(DEC_BATCH, 1, D)


def _mix_b_tail(pooled_groups, wgrp_ref, bgrp_ref, scale_ref, wout_ref):
    outs = [
        _dot(pg.astype(BF16), wgrp_ref[g]) + bgrp_ref[g]
        for g, pg in enumerate(pooled_groups)
    ]
    mixed = jnp.concatenate(outs, axis=1) * scale_ref[...]
    return _dot(mixed.astype(BF16), wout_ref[...])


def _mix_b_prompt_kernel(*refs, pending):
    x, refs = _residual_in(refs, pending)
    (sh_ref, sc_ref, gt_ref, ng_ref, win_ref, wgrp_ref, bgrp_ref, scale_ref, wout_ref,
     x1_ref, st_ref, full_ref, s2_ref, s4_ref, s8_ref) = refs
    assert POOL_WINDOWS == (2, 4, 8, 16)
    i = pl.program_id(0)
    j = i % TILES_PER_SEQ
    h = (_rms(x, ng_ref[...]) * (1.0 + sc_ref[...]) + sh_ref[...]).astype(BF16)
    p = _dot(h, win_ref[...])
    gd = B_GROUP_DIM
    top = SUBLANES
    cur = top + POOL_CARRY
    rows = TM + POOL_CARRY

    @pl.when(i == 0)
    def _():
        for ref in (full_ref, s2_ref, s4_ref, s8_ref):
            ref[0:top, :] = jnp.zeros((top, ref.shape[1]), F32)

    full_ref[top:cur, :] = jnp.where(j == 0, 0.0, full_ref[top:cur, :])
    full_ref[cur:cur + TM, :] = p
    s2_ref[top:top + rows, :] = full_ref[top:top + rows, :] + full_ref[pl.ds(top - 1, rows), :]
    s4_ref[top:top + rows, :] = s2_ref[top:top + rows, gd:] + s2_ref[pl.ds(top - 2, rows), gd:]
    s8_ref[top:top + rows, :] = s4_ref[top:top + rows, gd:] + s4_ref[pl.ds(top - 4, rows), gd:]
    s16 = s8_ref[cur:cur + TM, gd:] + s8_ref[pl.ds(cur - 8, TM), gd:]
    sums = [s2_ref[cur:cur + TM, 0:gd], s4_ref[cur:cur + TM, 0:gd], s8_ref[cur:cur + TM, 0:gd], s16]
    pos = j * TM + lax.broadcasted_iota(jnp.int32, (TM, 1), 0)
    pooled = []
    for g, w in enumerate(POOL_WINDOWS):
        cnt = jnp.minimum(w, pos + 1).astype(F32)
        pooled.append(sums[g] / cnt - p[:, g * gd:(g + 1) * gd])
    y = _mix_b_tail(pooled, wgrp_ref, bgrp_ref, scale_ref, wout_ref)
    x1_ref[...] = x + gt_ref[...] * y

    @pl.when(j == TILES_PER_SEQ - 1)
    def _():
        st_ref[...] = full_ref[pl.ds(cur + TM - POOL_BUF, POOL_BUF), :]

    full_ref[top:cur, :] = full_ref[pl.ds(TM + top, POOL_CARRY), :]


def _mix_b_sample_kernel(*refs, pending):
    x, refs = _residual_in(refs, pending)
    (sh_ref, sc_ref, gt_ref, ng_ref, st_ref, win_ref, wgrp_ref, bgrp_ref, scale_ref, wout_ref,
     x1_ref, nst_ref) = refs
    h = (_rms(x, ng_ref[...]) * (1.0 + sc_ref[...]) + sh_ref[...]).astype(BF16)
    p = _dot(h, win_ref[...])
    pooled = []
    for g, w in enumerate(POOL_WINDOWS):
        s = p[:, g * B_GROUP_DIM:(g + 1) * B_GROUP_DIM]
        for k in range(1, w):
            s = s + st_ref[POOL_BUF - k, :, g * B_GROUP_DIM:(g + 1) * B_GROUP_DIM]
        cnt = float(min(w, PAST_LEN + 1))
        pooled.append(s / cnt - p[:, g * B_GROUP_DIM:(g + 1) * B_GROUP_DIM])
    y = _mix_b_tail(pooled, wgrp_ref, bgrp_ref, scale_ref, wout_ref)
    x1_ref[...] = x + gt_ref[...] * y
    for k in range(1, POOL_BUF):
        nst_ref[k - 1] = st_ref[k]
    nst_ref[POOL_BUF - 1] = p


def _mixer_b(x_all, pending, mod_p, mod_s, ng, layer, lb, state, w_in, w_grp, b_grp, scale,
             w_out):
    gd = B_GROUP_DIM
    common = [
        _layer_block((D, D), lb), _layer_block((len(POOL_WINDOWS), gd, gd), lb),
        _layer_block((len(POOL_WINDOWS), 1, gd), lb), _layer_block((1, D), lb),
        _layer_block((D, D), lb),
    ]
    has_pending = pending is not None
    pend_specs, pend_args = _pending_prompt(pending, mod_p, layer)
    x_all, st_p = pl.pallas_call(
        functools.partial(_mix_b_prompt_kernel, pending=has_pending),
        grid=(N_PROMPT_TILES,),
        in_specs=[pl.BlockSpec((TM, D), lambda i: (i, 0))] + pend_specs
        + _mod_specs_prompt(layer, (0, 1, 2)) + [_full((1, D))] + common,
        out_specs=[
            pl.BlockSpec((TM, D), lambda i: (i, 0)),
            pl.BlockSpec((None, POOL_BUF, D), lambda i: (i // TILES_PER_SEQ, 0, 0)),
        ],
        out_shape=[
            jax.ShapeDtypeStruct((T_PAD, D), F32),
            jax.ShapeDtypeStruct((BATCH, POOL_BUF, D), F32),
        ],
        scratch_shapes=[
            pltpu.VMEM((SUBLANES + POOL_CARRY + TM, D - g * gd), F32) for g in (0, 0, 1, 2)],
        input_output_aliases={0: 0},
        compiler_params=_cparams("arbitrary"),
        name="mixer_b_prompt",
    )(x_all, *pend_args, mod_p, mod_p, mod_p, ng, w_in, w_grp, b_grp, scale, w_out)

    r = SAMPLE_ROWS
    sblk = T_PROMPT // r
    pend_specs, pend_args = _pending_sample(pending, mod_s, layer, r)
    x_all, st_s = pl.pallas_call(
        functools.partial(_mix_b_sample_kernel, pending=has_pending),
        grid=(DEC_BATCH // r,),
        in_specs=[pl.BlockSpec((r, D), lambda i: (sblk + i, 0))] + pend_specs
        + _mod_specs_sample(layer, (0, 1, 2), r) + [_full((1, D))]
        + [pl.BlockSpec((None, POOL_BUF, r, D), lambda i: (lb, 0, i, 0))] + common,
        out_specs=[
            pl.BlockSpec((r, D), lambda i: (sblk + i, 0)),
            pl.BlockSpec((POOL_BUF, r, D), lambda i: (0, i, 0)),
        ],
        out_shape=[
            jax.ShapeDtypeStruct((T_PAD, D), F32),
            jax.ShapeDtypeStruct((POOL_BUF, DEC_BATCH, D), F32),
        ],
        input_output_aliases={0: 0},
        compiler_params=_cparams("arbitrary"),
        name="mixer_b_sample",
    )(x_all, *pend_args, mod_s, mod_s, mod_s, ng, state, w_in, w_grp, b_grp, scale, w_out)
    return x_all, st_p, jnp.transpose(st_s, (1, 0, 2))


def _mix_c_glu(x, sh, sc, ng, win_ref, bin_ref):
    h = (_rms(x, ng) * (1.0 + sc) + sh).astype(BF16)
    ag = _dot(h, win_ref[...]) + bin_ref[...]
    return ag[:, :D] * jax.nn.sigmoid(ag[:, D:])


def _mix_c_tail(conv, lng_ref, lnb_ref, wout_ref, bout_ref):
    z = _ln(conv, lng_ref[...], lnb_ref[...])
    z = z * jax.nn.sigmoid(z)
    return _dot(z.astype(BF16), wout_ref[...]) + bout_ref[...]


def _mix_c_prompt_kernel(*refs, pending):
    x, refs = _residual_in(refs, pending)
    (sh_ref, sc_ref, gt_ref, ng_ref, win_ref, bin_ref, wdw_ref, bdw_ref, lng_ref, lnb_ref,
     wout_ref, bout_ref, x1_ref, st_ref, full_ref, conv_ref, *shift_refs) = refs
    i = pl.program_id(0)
    j = i % TILES_PER_SEQ
    glu = _mix_c_glu(x, sh_ref[...], sc_ref[...], ng_ref[...], win_ref, bin_ref)

    @pl.when(j == 0)
    def _():
        full_ref[0:CONV_CARRY, :] = jnp.zeros((CONV_CARRY, D), F32)

    full_ref[CONV_CARRY:CONV_CARRY + TM, :] = glu
    off = CONV_CARRY - CONV_BUF
    span = TM + (off + CONV_BUF) // SUBLANES * SUBLANES - SUBLANES
    for c in range(D // LANES):
        cols = slice(c * LANES, (c + 1) * LANES)
        shift_ref = shift_refs[c % len(shift_refs)]
        for r in range(1, SUBLANES):
            shift_ref[r - 1] = full_ref[pl.ds(r, span), cols]
        for rb in range(TM // CONV_ROWS):
            acc = None
            for k in range(CONV_WIDTH):
                q, r = divmod(off + k, SUBLANES)
                start = q * SUBLANES + rb * CONV_ROWS
                if r == 0:
                    src = full_ref[pl.ds(start, CONV_ROWS), cols]
                else:
                    src = shift_ref[r - 1, pl.ds(start, CONV_ROWS), :]
                term = src * wdw_ref[k:k + 1, cols]
                acc = term if acc is None else acc + term
            conv_ref[pl.ds(rb * CONV_ROWS, CONV_ROWS), cols] = acc + bdw_ref[:, cols]
    y = _mix_c_tail(conv_ref[...], lng_ref, lnb_ref, wout_ref, bout_ref)
    x1_ref[...] = x + gt_ref[...] * y

    @pl.when(j == TILES_PER_SEQ - 1)
    def _():
        st_ref[...] = full_ref[pl.ds(CONV_CARRY + TM - CONV_BUF, CONV_BUF), :]

    full_ref[0:CONV_CARRY, :] = full_ref[pl.ds(TM, CONV_CARRY), :]


def _mix_c_sample_kernel(*refs, pending):
    x, refs = _residual_in(refs, pending)
    (sh_ref, sc_ref, gt_ref, ng_ref, st_ref, win_ref, bin_ref, wdw_ref, bdw_ref, lng_ref,
     lnb_ref, wout_ref, bout_ref, x1_ref, nst_ref) = refs
    glu = _mix_c_glu(x, sh_ref[...], sc_ref[...], ng_ref[...], win_ref, bin_ref)
    acc = glu * wdw_ref[CONV_BUF:CONV_BUF + 1, :]
    for k in range(CONV_BUF):
        acc = acc + st_ref[k] * wdw_ref[k:k + 1, :]
    y = _mix_c_tail(acc + bdw_ref[...], lng_ref, lnb_ref, wout_ref, bout_ref)
    x1_ref[...] = x + gt_ref[...] * y
    for k in range(1, CONV_BUF):
        nst_ref[k - 1] = st_ref[k]
    nst_ref[CONV_BUF - 1] = glu


def _mixer_c(x_all, pending, mod_p, mod_s, ng, layer, lc, state, w_in, b_in, w_dw, b_dw, ln_g,
             ln_b, w_out, b_out):
    common = [
        _layer_block((D, 2 * D), lc), _layer_block((1, 2 * D), lc),
        _layer_block((CONV_WIDTH, D), lc), _layer_block((1, D), lc), _layer_block((1, D), lc),
        _layer_block((1, D), lc), _layer_block((D, D), lc), _layer_block((1, D), lc),
    ]
    has_pending = pending is not None
    pend_specs, pend_args = _pending_prompt(pending, mod_p, layer)
    x_all, st_p = pl.pallas_call(
        functools.partial(_mix_c_prompt_kernel, pending=has_pending),
        grid=(N_PROMPT_TILES,),
        in_specs=[pl.BlockSpec((TM, D), lambda i: (i, 0))] + pend_specs
        + _mod_specs_prompt(layer, (0, 1, 2)) + [_full((1, D))] + common,
        out_specs=[
            pl.BlockSpec((TM, D), lambda i: (i, 0)),
            pl.BlockSpec((None, CONV_BUF, D), lambda i: (i // TILES_PER_SEQ, 0, 0)),
        ],
        out_shape=[
            jax.ShapeDtypeStruct((T_PAD, D), F32),
            jax.ShapeDtypeStruct((BATCH, CONV_BUF, D), F32),
        ],
        scratch_shapes=[pltpu.VMEM((TM + CONV_CARRY, D), F32), pltpu.VMEM((TM, D), F32)]
        + [pltpu.VMEM((SUBLANES - 1, TM + CONV_CARRY - SUBLANES, LANES), F32)] * 2,
        input_output_aliases={0: 0},
        compiler_params=_cparams("arbitrary"),
        name="mixer_c_prompt",
    )(x_all, *pend_args, mod_p, mod_p, mod_p, ng, w_in, b_in, w_dw, b_dw, ln_g, ln_b, w_out,
      b_out)

    r = SAMPLE_ROWS
    sblk = T_PROMPT // r
    pend_specs, pend_args = _pending_sample(pending, mod_s, layer, r)
    x_all, st_s = pl.pallas_call(
        functools.partial(_mix_c_sample_kernel, pending=has_pending),
        grid=(DEC_BATCH // r,),
        in_specs=[pl.BlockSpec((r, D), lambda i: (sblk + i, 0))] + pend_specs
        + _mod_specs_sample(layer, (0, 1, 2), r) + [_full((1, D))]
        + [pl.BlockSpec((None, CONV_BUF, r, D), lambda i: (lc, 0, i, 0))] + common,
        out_specs=[
            pl.BlockSpec((r, D), lambda i: (sblk + i, 0)),
            pl.BlockSpec((CONV_BUF, r, D), lambda i: (0, i, 0)),
        ],
        out_shape=[
            jax.ShapeDtypeStruct((T_PAD, D), F32),
            jax.ShapeDtypeStruct((CONV_BUF, DEC_BATCH, D), F32),
        ],
        input_output_aliases={0: 0},
        compiler_params=_cparams("arbitrary"),
        name="mixer_c_sample",
    )(x_all, *pend_args, mod_s, mod_s, mod_s, ng, state, w_in, b_in, w_dw, b_dw, ln_g, ln_b,
      w_out, b_out)
    return x_all, st_p, jnp.transpose(st_s, (1, 0, 2))


def _tile_mod(i, p_ref, s_ref):
    s_rows = jnp.concatenate([s_ref[...], jnp.zeros((TM - DEC_BATCH, D), F32)], axis=0)
    return jnp.where(i >= N_PROMPT_TILES, s_rows, p_ref[...])


def _unified_mod_specs(layer, chunks):
    specs = []
    for c in chunks:
        specs.append(pl.BlockSpec(
            (None, None, 1, D),
            lambda i, c=c: (layer, jnp.minimum(i // TILES_PER_SEQ, BATCH - 1), 0, c)))
        specs.append(pl.BlockSpec((None, DEC_BATCH, D), lambda i, c=c: (layer, 0, c)))
    return specs


def _router_kernel(x_ref, shp_ref, shs_ref, scp_ref, scs_ref, ng_ref, wr_ref, br_ref,
                   h2_ref, meta_ref, wts_ref, cnt_ref, carry_ref, before_ref):
    i = pl.program_id(0)

    @pl.when(i == 0)
    def _():
        carry_ref[...] = jnp.zeros((ROUTER_ROWS, 1), F32)
        earlier = lax.broadcasted_iota(jnp.int32, (TM, TM), 0)
        token = lax.broadcasted_iota(jnp.int32, (TM, TM), 1)
        before_ref[...] = (earlier < token).astype(BF16)

    sh = _tile_mod(i, shp_ref, shs_ref)
    sc = _tile_mod(i, scp_ref, scs_ref)
    h2 = _rms(x_ref[...], ng_ref[...]) * (1.0 + sc) + sh
    h2_ref[...] = _pack_bf16_pairs(h2)
    lg = lax.dot_general(wr_ref[...], h2.astype(BF16), (((1,), (1,)), ((), ())),
                         preferred_element_type=F32) + br_ref[...]
    row = lax.broadcasted_iota(jnp.int32, (ROUTER_ROWS, TM), 0)
    row_f = row.astype(F32)
    far = float(ROUTER_ROWS)
    neg = -jnp.inf
    is_g = row < N_GROUPS
    gm = jnp.where(is_g, lg, neg)
    gmax = jnp.max(gm, axis=0, keepdims=True)
    gsel = jnp.min(jnp.where(gm == gmax, row_f, far), axis=0, keepdims=True).astype(jnp.int32)
    gsum = jnp.sum(jnp.where(is_g, jnp.exp(lg - gmax), 0.0), axis=0, keepdims=True)
    g_w = 1.0 / gsum
    e_row = row - EXPERT_LANE0
    in_grp = (e_row >= 0) & (e_row < N_EXPERTS) & (
        lax.shift_right_arithmetic(e_row, EXPERTS_PER_GROUP.bit_length() - 1) == gsel)
    em = jnp.where(in_grp, lg, neg)
    m1 = jnp.max(em, axis=0, keepdims=True)
    i1 = jnp.min(jnp.where(em == m1, row_f, far), axis=0, keepdims=True).astype(jnp.int32)
    em2 = jnp.where(row == i1, neg, em)
    m2 = jnp.max(em2, axis=0, keepdims=True)
    i2 = jnp.min(jnp.where(em2 == m2, row_f, far), axis=0, keepdims=True).astype(jnp.int32)
    e2 = jnp.exp(m2 - m1)
    den = 1.0 + e2
    w1 = (1.0 / den) * g_w
    w2 = (e2 / den) * g_w

    hit1 = row == i1
    hit2 = row == i2
    assign = jnp.where(hit1 | hit2, 1.0, 0.0).astype(BF16)
    seen = _dot(assign, before_ref[...]) + carry_ref[...]
    rank1 = jnp.sum(jnp.where(hit1, seen, 0.0), axis=0, keepdims=True)
    rank2 = jnp.sum(jnp.where(hit2, seen, 0.0), axis=0, keepdims=True)
    carry_ref[...] = carry_ref[...] + jnp.sum(assign.astype(F32), axis=1, keepdims=True)
    cnt_ref[...] = carry_ref[...].astype(jnp.int32)

    ex1 = (i1 - EXPERT_LANE0).astype(F32)
    ex2 = (i2 - EXPERT_LANE0).astype(F32)
    field = lax.broadcasted_iota(jnp.int32, (META_ROWS, TM), 0)
    meta_ref[...] = jnp.where(field == 0, ex1, jnp.where(field == 1, ex2,
                              jnp.where(field == 2, rank1, jnp.where(field == 3, rank2, 0.0))))
    slot = lax.broadcasted_iota(jnp.int32, (LANES, TM), 0)
    wts_ref[...] = jnp.where(slot == 0, w1, jnp.where(slot == 1, w2, 0.0)).T


def _router(x_all, mod_p, mod_s, ng, layer, w_r, b_r):
    return pl.pallas_call(
        _router_kernel,
        grid=(N_TILES,),
        in_specs=[pl.BlockSpec((TM, D), lambda i: (i, 0))]
        + _unified_mod_specs(layer, (3, 4))
        + [_full((1, D)), _full((ROUTER_ROWS, D)), _full((ROUTER_ROWS, 1))],
        out_specs=[
            pl.BlockSpec((TM, D // 2), lambda i: (i, 0)),
            pl.BlockSpec((META_ROWS, TM), lambda i: (0, i)),
            pl.BlockSpec((TM, LANES), lambda i: (i, 0)),
            pl.BlockSpec((ROUTER_ROWS, 1), lambda i: (0, 0)),
        ],
        out_shape=[
            jax.ShapeDtypeStruct((T_PAD, D // 2), jnp.int32),
            jax.ShapeDtypeStruct((META_ROWS, T_PAD), F32),
            jax.ShapeDtypeStruct((T_PAD, LANES), F32),
            jax.ShapeDtypeStruct((ROUTER_ROWS, 1), jnp.int32),
        ],
        scratch_shapes=[pltpu.VMEM((ROUTER_ROWS, 1), F32), pltpu.VMEM((TM, TM), BF16)],
        compiler_params=_cparams("arbitrary"),
        name="router",
    )(x_all, mod_p, mod_s, mod_p, mod_s, ng, w_r, b_r)


SC_INDEX_MAX = 128
SC_ROWS_BYTES = 400 * 1024


def _gather_chunk(per_worker, row_bytes, buffers=1):
    top = min(SC_INDEX_MAX, SC_ROWS_BYTES // (row_bytes * buffers)) // SUBLANES * SUBLANES
    for ch in range(top, SUBLANES - 1, -SUBLANES):
        if per_worker % ch == 0:
            return ch
    raise ValueError(per_worker)


def _sc_gather(table, idx):
    nc, nw = SC_CORES, SC_CORES * SC_SUBCORES
    m = idx.shape[0]
    per_w = m // nw
    assert per_w * nw == m
    width = table.shape[1]
    ch = _gather_chunk(per_w, width * table.dtype.itemsize, buffers=2)
    mesh = plsc.VectorSubcoreMesh(core_axis_name="c", subcore_axis_name="s")

    n_chunks = per_w // ch

    @functools.partial(
        pl.kernel,
        out_type=jax.ShapeDtypeStruct((m, width), table.dtype),
        mesh=mesh,
        scratch_types=[pltpu.VMEM((ch,), jnp.int32)] * 2
        + [pltpu.VMEM((ch, width), table.dtype)] * 2
        + [pltpu.SemaphoreType.DMA] * 4,
    )
    def gather_kernel(t_hbm, i_hbm, o_hbm, idx_a, idx_b, rows_a, rows_b, sg_a, sg_b, sw_a, sw_b):
        wid = lax.axis_index("s") * nc + lax.axis_index("c")
        base = wid * per_w
        idx_v, rows_v, sem_g, sem_w = (idx_a, idx_b), (rows_a, rows_b), (sg_a, sg_b), (sw_a, sw_b)

        def out_rows(j):
            return o_hbm.at[pl.ds(pl.multiple_of(base + j * ch, 8), ch)]

        gathers, writes = {}, {}
        for j in range(n_chunks):
            b = j % 2
            if j >= 2:
                writes[j - 2].wait()
            pltpu.sync_copy(i_hbm.at[pl.ds(pl.multiple_of(base + j * ch, 8), ch)], idx_v[b])
            gathers[j] = pltpu.async_copy(t_hbm.at[idx_v[b]], rows_v[b], sem_g[b])
            if j >= 1:
                gathers[j - 1].wait()
                writes[j - 1] = pltpu.async_copy(rows_v[1 - b], out_rows(j - 1), sem_w[1 - b])
        last = n_chunks - 1
        gathers[last].wait()
        writes[last] = pltpu.async_copy(rows_v[last % 2], out_rows(last), sem_w[last % 2])
        for j in range(max(last - 1, 0), n_chunks):
            writes[j].wait()

    return gather_kernel(table, idx)


def _sc_scatter2(rows, dest0, dest1, n_out):
    nc, nw = SC_CORES, SC_CORES * SC_SUBCORES
    n, width = rows.shape
    per_w = n // nw
    assert per_w * nw == n
    ch = _gather_chunk(per_w, width * rows.dtype.itemsize)
    mesh = plsc.VectorSubcoreMesh(core_axis_name="c", subcore_axis_name="s")

    @functools.partial(
        pl.kernel,
        out_type=jax.ShapeDtypeStruct((n_out, width), rows.dtype),
        mesh=mesh,
        scratch_types=[
            pltpu.VMEM((ch,), jnp.int32),
            pltpu.VMEM((ch,), jnp.int32),
            pltpu.VMEM((ch, width), rows.dtype),
            pltpu.SemaphoreType.DMA,
            pltpu.SemaphoreType.DMA,
        ],
    )
    def scatter_kernel(r_hbm, d0_hbm, d1_hbm, o_hbm, i0_v, i1_v, rows_v, sem0, sem1):
        wid = lax.axis_index("s") * nc + lax.axis_index("c")
        base = wid * per_w

        @pl.loop(0, per_w // ch)
        def _(j):
            off = pl.multiple_of(base + j * ch, 8)
            pltpu.sync_copy(d0_hbm.at[pl.ds(off, ch)], i0_v)
            pltpu.sync_copy(d1_hbm.at[pl.ds(off, ch)], i1_v)
            pltpu.sync_copy(r_hbm.at[pl.ds(off, ch)], rows_v)
            c0 = pltpu.async_copy(rows_v, o_hbm.at[i0_v], sem0)
            c1 = pltpu.async_copy(rows_v, o_hbm.at[i1_v], sem1)
            c0.wait()
            c1.wait()

    return scatter_kernel(rows, dest0, dest1)


def _expert_kernel(be_ref, first_ref, nvalid_ref, next_ref, nused_ref, x_ref, wg_hbm, wu_hbm,
                   wd_hbm, y_ref, wgf_ref, wuf_ref, wdf_ref, wgb_ref, wub_ref, wdb_ref, sems,
                   *, layer):
    b = pl.program_id(0)

    def weight_copies(e):
        return (
            pltpu.make_async_copy(wg_hbm.at[layer, e], wgf_ref, sems.at[0]),
            pltpu.make_async_copy(wu_hbm.at[layer, e], wuf_ref, sems.at[1]),
            pltpu.make_async_copy(wd_hbm.at[layer, e], wdf_ref, sems.at[2]),
        )

    @pl.when(b == 0)
    def _():
        for cp in weight_copies(be_ref[0]):
            cp.start()

    @pl.when(b < nused_ref[0])
    def _():
        @pl.when(first_ref[b] == 1)
        def _():
            for cp in weight_copies(be_ref[b]):
                cp.wait()
            wgb_ref[...] = wgf_ref[...].astype(BF16)
            wub_ref[...] = wuf_ref[...].astype(BF16)
            wdb_ref[...] = wdf_ref[...].astype(BF16)

            @pl.when(next_ref[b] >= 0)
            def _():
                for cp in weight_copies(next_ref[b]):
                    cp.start()

        def run(rows):
            live = lax.broadcasted_iota(jnp.int32, (rows, 1), 0) < nvalid_ref[b]
            x = _unpack_bf16_pairs(jnp.where(live, x_ref[0:rows, :], 0), BF16)
            g = _dot(x, wgb_ref[...])
            u = _dot(x, wub_ref[...])
            hmid = (g * jax.nn.sigmoid(g)) * u
            y_ref[0:rows, :] = _pack_bf16_pairs(_dot(hmid.astype(BF16), wdb_ref[...]))

        for rows in range(EXPERT_ROW_GROUP, BM + 1, EXPERT_ROW_GROUP):
            @pl.when((nvalid_ref[b] > rows - EXPERT_ROW_GROUP) & (nvalid_ref[b] <= rows))
            def _(rows=rows):
                run(rows)
                if rows < BM:
                    y_ref[rows:BM, :] = jnp.zeros((BM - rows, D // 2), jnp.int32)

    @pl.when(b >= nused_ref[0])
    def _():
        y_ref[...] = jnp.zeros((BM, D // 2), jnp.int32)


def _experts(x_sorted, block_expert, first, nvalid, next_expert, nused, layer, w_gate, w_up,
             w_down):
    def row_map(b, be, fi, nv, nx, nu):
        return (jnp.minimum(b, nu[0] - 1), 0)

    def out_map(b, be, fi, nv, nx, nu):
        return (b, 0)

    grid_spec = pltpu.PrefetchScalarGridSpec(
        num_scalar_prefetch=5,
        grid=(N_BLOCKS,),
        in_specs=[
            pl.BlockSpec((BM, D // 2), row_map),
            pl.BlockSpec(memory_space=pl.ANY),
            pl.BlockSpec(memory_space=pl.ANY),
            pl.BlockSpec(memory_space=pl.ANY),
        ],
        out_specs=pl.BlockSpec((BM, D // 2), out_map),
        scratch_shapes=[
            pltpu.VMEM((D, D_EXPERT), F32),
            pltpu.VMEM((D, D_EXPERT), F32),
            pltpu.VMEM((D_EXPERT, D), F32),
            pltpu.VMEM((D, D_EXPERT), BF16),
            pltpu.VMEM((D, D_EXPERT), BF16),
            pltpu.VMEM((D_EXPERT, D), BF16),
            pltpu.SemaphoreType.DMA((3,)),
        ],
    )
    return pl.pallas_call(
        functools.partial(_expert_kernel, layer=layer),
        grid_spec=grid_spec,
        out_shape=jax.ShapeDtypeStruct((P_ROWS, D // 2), jnp.int32),
        compiler_params=_cparams("arbitrary"),
        name="experts",
    )(block_expert, first, nvalid, next_expert, nused, x_sorted, w_gate, w_up, w_down)


def _combine_final_kernel(x_ref, g0_ref, g1_ref, wts_ref, gtp_ref, gts_ref, fg_ref, yp_ref,
                          ys_ref):
    i = pl.program_id(0)
    gate = _tile_mod(i, gtp_ref, gts_ref)
    y = _rms(_plus_moe(x_ref[...], g0_ref, g1_ref, wts_ref, gate), fg_ref[...])

    @pl.when(i < N_PROMPT_TILES)
    def _():
        yp_ref[...] = y

    @pl.when(i == N_PROMPT_TILES)
    def _():
        ys_ref[...] = y[0:DEC_BATCH, :]


def _combine_final(x_all, gathered, wts, mod_p, mod_s, layer, final_g):
    in_specs = [
        pl.BlockSpec((TM, D), lambda i: (i, 0)),
        pl.BlockSpec((TM, D // 2), lambda i: (i, 0)),
        pl.BlockSpec((TM, D // 2), lambda i: (i + N_TILES, 0)),
        pl.BlockSpec((TM, LANES), lambda i: (i, 0)),
    ] + _unified_mod_specs(layer, (5,))
    return pl.pallas_call(
        _combine_final_kernel,
        grid=(N_PROMPT_TILES + 1,),
        in_specs=in_specs + [_full((1, D))],
        out_specs=[
            pl.BlockSpec((TM, D), lambda i: (jnp.minimum(i, N_PROMPT_TILES - 1), 0)),
            pl.BlockSpec((DEC_BATCH, D), lambda i: (0, 0)),
        ],
        out_shape=[
            jax.ShapeDtypeStruct((T_PROMPT, D), F32),
            jax.ShapeDtypeStruct((DEC_BATCH, D), F32),
        ],
        compiler_params=_cparams("arbitrary"),
        name="combine_final",
    )(x_all, gathered, gathered, wts, mod_p, mod_s, final_g)


def _plan_kernel(cnt_ref, meta_ref, d0_ref, d1_ref, be_ref, first_ref, nvalid_ref, next_ref,
                 nused_ref, pstart_ref, nxt_ref):
    shift = BM.bit_length() - 1
    assert 1 << shift == BM

    def count(e):
        return cnt_ref[EXPERT_LANE0 + e, 0]

    def scan_back(k, nxt):
        e = N_EXPERTS - 1 - k
        nxt_ref[e] = nxt
        return jnp.where(count(e) > 0, e, nxt)

    lax.fori_loop(0, N_EXPERTS, scan_back, jnp.int32(-1))

    def fill(e, blk):
        c = count(e)
        pstart_ref[e] = blk << shift

        def one(j, carry):
            be_ref[blk + j] = e
            first_ref[blk + j] = jnp.where(j == 0, 1, 0)
            nvalid_ref[blk + j] = jnp.minimum(c - (j << shift), BM)
            next_ref[blk + j] = nxt_ref[e]
            return carry

        n_blk = (c + (BM - 1)) >> shift
        lax.fori_loop(0, n_blk, one, 0)
        return blk + n_blk

    used = lax.fori_loop(0, N_EXPERTS, fill, jnp.int32(0))
    nused_ref[0] = used

    def tail(b, carry):
        be_ref[b] = 0
        first_ref[b] = 0
        nvalid_ref[b] = 0
        next_ref[b] = -1
        return carry

    lax.fori_loop(used, N_BLOCKS, tail, 0)

    m = meta_ref[...]
    expert = m[0:2, :]
    dest = m[2:4, :]
    for e in range(N_EXPERTS):
        dest = dest + jnp.where(expert == float(e), pstart_ref[e].astype(F32), 0.0)
    dest = dest.astype(jnp.int32)
    d0_ref[...] = dest[0:1, :]
    d1_ref[...] = dest[1:2, :]


def _plan(cnt, meta):
    smem = pl.BlockSpec(memory_space=pltpu.SMEM)
    blocks = jax.ShapeDtypeStruct((N_BLOCKS,), jnp.int32)
    return pl.pallas_call(
        _plan_kernel,
        grid=(1,),
        in_specs=[smem, _full((META_ROWS, T_PAD))],
        out_specs=[_full((1, T_PAD)), _full((1, T_PAD)), smem, smem, smem, smem, smem],
        out_shape=[
            jax.ShapeDtypeStruct((1, T_PAD), jnp.int32),
            jax.ShapeDtypeStruct((1, T_PAD), jnp.int32),
            blocks, blocks, blocks, blocks,
            jax.ShapeDtypeStruct((1,), jnp.int32),
        ],
        scratch_shapes=[pltpu.SMEM((N_EXPERTS,), jnp.int32), pltpu.SMEM((N_EXPERTS,), jnp.int32)],
        compiler_params=_cparams("arbitrary"),
        name="plan",
    )(cnt, meta)


def _moe(x_all, mod_p, mod_s, ng, layer, w_r, b_r, w_gate, w_up, w_down):
    h2, meta, wts, cnt = _router(x_all, mod_p, mod_s, ng, layer, w_r, b_r)
    dest0, dest1, block_expert, first, nvalid, next_expert, nused = _plan(cnt, meta)
    dest0, dest1 = dest0.reshape(T_PAD), dest1.reshape(T_PAD)
    x_sorted = _sc_scatter2(h2, dest0, dest1, P_ROWS)
    y_sorted = _experts(x_sorted, block_expert, first, nvalid, next_expert, nused, layer,
                        w_gate, w_up, w_down)
    gathered = _sc_gather(y_sorted, jnp.concatenate([dest0, dest1]))
    return gathered, wts


def kernel(x_prompt, x_sample, c_prompt, c_sample, state_pool, state_conv, ada_w, ada_b, norm_g, final_g, a_w_in, a_ln_g, a_ln_b, a_w_s, a_b_s, a_w_out, b_w_in, b_w_grp, b_b_grp, b_scale, b_w_out, c_w_in, c_b_in, c_w_dw, c_b_dw, c_ln_g, c_ln_b, c_w_out, c_b_out, moe_w_grp, moe_b_grp, moe_w_exp, moe_b_exp, moe_w_gate, moe_w_up, moe_w_down):
    x_all = (x_prompt.reshape(T_PROMPT, D), x_sample.reshape(DEC_BATCH, D))
    mod_p, mod_s = _ada_mods(jnp.concatenate([c_prompt, c_sample], axis=0), ada_w, ada_b)
    mod_p = mod_p.reshape(DEPTH, BATCH, 1, 6 * D)

    a_w_in_b, a_w_out_b = a_w_in.astype(BF16), a_w_out.astype(BF16)
    b_w_in_b, b_w_grp_b, b_w_out_b = b_w_in.astype(BF16), b_w_grp.astype(BF16), b_w_out.astype(BF16)
    c_w_in_b, c_w_out_b = c_w_in.astype(BF16), c_w_out.astype(BF16)
    pad_rows = ROUTER_ROWS - N_GROUPS - N_EXPERTS
    w_r = jnp.pad(jnp.swapaxes(jnp.concatenate([moe_w_grp, moe_w_exp], axis=2), 1, 2),
                  ((0, 0), (0, pad_rows), (0, 0))).astype(BF16)
    b_r = jnp.pad(jnp.concatenate([moe_b_grp, moe_b_exp], axis=1), ((0, 0), (0, pad_rows)))

    new_a_p, new_a_s, new_b_p, new_b_s, new_c_p, new_c_s = [], [], [], [], [], []
    pool_tm = jnp.transpose(state_pool, (0, 2, 1, 3))
    conv_tm = jnp.transpose(state_conv, (0, 2, 1, 3))
    ia = ib = ic = 0
    assert len(range(0, DEPTH, 3)) == 2
    pending = None
    for layer in range(DEPTH):
        ng1 = norm_g[layer, 0].reshape(1, D)
        ng2 = norm_g[layer, 1].reshape(1, D)
        kind = layer % 3
        if kind == 0:
            x_all, st_p, st_s = _mixer_a(
                x_all, pending, new_a_p[0] if new_a_p else None, mod_p, mod_s, ng1, layer, ia,
                a_w_in_b, a_ln_g.reshape(-1, 1, D), a_ln_b.reshape(-1, 1, D), a_w_s, a_b_s,
                a_w_out_b)
            new_a_p = [st_p]
            new_a_s.append(st_s)
            ia += 1
        elif kind == 1:
            x_all, st_p, st_s = _mixer_b(
                x_all, pending, mod_p, mod_s, ng1, layer, ib, pool_tm, b_w_in_b, b_w_grp_b,
                b_b_grp.reshape(-1, len(POOL_WINDOWS), 1, B_GROUP_DIM),
                b_scale.reshape(-1, 1, D), b_w_out_b)
            new_b_p.append(st_p)
            new_b_s.append(st_s)
            ib += 1
        else:
            x_all, st_p, st_s = _mixer_c(
                x_all, pending, mod_p, mod_s, ng1, layer, ic, conv_tm, c_w_in_b,
                c_b_in.reshape(-1, 1, 2 * D), c_w_dw, c_b_dw.reshape(-1, 1, D),
                c_ln_g.reshape(-1, 1, D), c_ln_b.reshape(-1, 1, D), c_w_out_b,
                c_b_out.reshape(-1, 1, D))
            new_c_p.append(st_p)
            new_c_s.append(st_s)
            ic += 1
        pending = _moe(x_all, mod_p, mod_s, ng2, layer, w_r[layer],
                       b_r[layer].reshape(ROUTER_ROWS, 1), moe_w_gate, moe_w_up, moe_w_down)

    y_p, y_s = _combine_final(x_all, *pending, mod_p, mod_s, DEPTH - 1, final_g.reshape(1, D))
    return (y_p.reshape(BATCH, SEQ, D), y_s.reshape(DEC_BATCH, 1, D),
            new_a_p[0], jnp.stack(new_a_s), jnp.stack(new_b_p), jnp.stack(new_b_s),
            jnp.stack(new_c_p), jnp.stack(new_c_s))
```

```python
import functools

import jax
import jax.numpy as jnp
from jax import lax
from jax.experimental import pallas as pl
from jax.experimental.pallas import tpu as pltpu
from jax.experimental.pallas import tpu_sc as plsc

F32 = jnp.float32
BF16 = jnp.bfloat16

D = 1024
BATCH = 8
SEQ = 2048
DEPTH = 4
DEC_BATCH = 128
PAST_LEN = 16384
CHUNK = 128
A_HEADS = 8
POOL_WINDOWS = (2, 4, 8, 16)
B_GROUP_DIM = D // len(POOL_WINDOWS)
POOL_BUF = max(POOL_WINDOWS) - 1
CONV_WIDTH = 31
CONV_BUF = CONV_WIDTH - 1
N_GROUPS = 4
EXPERTS_PER_GROUP = 8
N_EXPERTS = N_GROUPS * EXPERTS_PER_GROUP
D_EXPERT = D // 2
EPS = 1e-6

LANES = 128
SUBLANES = 8
CONV_ROWS = 128
SC_CORES = 2
SC_SUBCORES = 16
TM = 512
TILES_PER_SEQ = SEQ // TM
T_PROMPT = BATCH * SEQ
T_ALL = T_PROMPT + DEC_BATCH
N_TILES = -(-T_ALL // TM)
T_PAD = N_TILES * TM
N_PROMPT_TILES = T_PROMPT // TM
SAMPLE_ROWS = 32
BM = 512
N_BLOCKS = -(-(2 * T_PAD) // BM) + N_EXPERTS
P_ROWS = N_BLOCKS * BM
META_ROWS = 8
POOL_CARRY = 16
CONV_CARRY = 32
EXPERT_LANE0 = N_GROUPS
ROUTER_ROWS = 48
VMEM_LIMIT = 56 * 1024 * 1024


def _cparams(*sem):
    return pltpu.CompilerParams(dimension_semantics=sem, vmem_limit_bytes=VMEM_LIMIT)


def _rms(x, g):
    return x * lax.rsqrt(jnp.mean(x * x, axis=-1, keepdims=True) + EPS) * g


def _ln(x, g, b):
    mu = jnp.mean(x, axis=-1, keepdims=True)
    xc = x - mu
    var = jnp.mean(xc * xc, axis=-1, keepdims=True)
    return xc * lax.rsqrt(var + EPS) * g + b


def _dot(a, b):
    return jnp.dot(a, b, preferred_element_type=F32)


def _pack_bf16_pairs(v):
    w = v.shape[1] // 2
    lo = lax.bitcast_convert_type(v[:, :w].astype(BF16).astype(F32), jnp.uint32)
    hi = lax.bitcast_convert_type(v[:, w:].astype(BF16).astype(F32), jnp.uint32)
    return lax.bitcast_convert_type(lax.shift_right_logical(lo, jnp.uint32(16)) | hi, jnp.int32)


def _unpack_bf16_pairs(words, dtype):
    u = lax.bitcast_convert_type(words, jnp.uint32)
    lo = lax.bitcast_convert_type(lax.shift_left(u, jnp.uint32(16)), F32)
    hi = lax.bitcast_convert_type(u & jnp.uint32(0xFFFF0000), F32)
    return jnp.concatenate([lo.astype(dtype), hi.astype(dtype)], axis=1)


def _ada_kernel(c_ref, w_ref, b_ref, op_ref, os_ref):
    c = c_ref[...]
    s = (c * jax.nn.sigmoid(c)).astype(BF16)
    mod = _dot(s, w_ref[...].astype(BF16)) + b_ref[...]
    op_ref[...] = mod[0:BATCH, :]
    os_ref[...] = mod[BATCH:, :]


def _ada_mods(c_all, ada_w, ada_b):
    n = c_all.shape[0]
    assert n == BATCH + DEC_BATCH and BATCH % SUBLANES == 0
    tn = 2048
    return pl.pallas_call(
        _ada_kernel,
        grid=(DEPTH, 6 * D // tn),
        in_specs=[
            pl.BlockSpec((n, D), lambda l, j: (0, 0)),
            pl.BlockSpec((None, D, tn), lambda l, j: (l, 0, j)),
            pl.BlockSpec((None, 1, tn), lambda l, j: (l, 0, j)),
        ],
        out_specs=[
            pl.BlockSpec((None, BATCH, tn), lambda l, j: (l, 0, j)),
            pl.BlockSpec((None, DEC_BATCH, tn), lambda l, j: (l, 0, j)),
        ],
        out_shape=[
            jax.ShapeDtypeStruct((DEPTH, BATCH, 6 * D), F32),
            jax.ShapeDtypeStruct((DEPTH, DEC_BATCH, 6 * D), F32),
        ],
        compiler_params=_cparams("parallel", "parallel"),
        name="ada_mods",
    )(c_all, ada_w, ada_b.reshape(DEPTH, 1, 6 * D))


def _mod_specs_prompt(layer, chunks):
    return [
        pl.BlockSpec((None, None, 1, D),
                     lambda i, c=c: (layer, jnp.minimum(i // TILES_PER_SEQ, BATCH - 1), 0, c))
        for c in chunks
    ]


def _mod_specs_sample(layer, chunks, rows, row_block0=0):
    return [
        pl.BlockSpec((None, rows, D), lambda i, c=c: (layer, i + row_block0, c)) for c in chunks
    ]


def _full(shape):
    nd = len(shape)
    return pl.BlockSpec(shape, lambda i: (0,) * nd)


def _layer_block(shape, layer):
    nd = len(shape)
    return pl.BlockSpec((None,) + shape, lambda i: (layer,) + (0,) * nd)


def _plus_moe(x, g0_ref, g1_ref, wts_ref, gate):
    w = wts_ref[...]
    y0 = _unpack_bf16_pairs(g0_ref[...], F32)
    y1 = _unpack_bf16_pairs(g1_ref[...], F32)
    return x + gate * (w[:, 0:1] * y0 + w[:, 1:2] * y1)


def _residual_in(refs, pending):
    if not pending:
        return refs[0][...], refs[1:]
    x_ref, g0_ref, g1_ref, wts_ref, gate_ref = refs[:5]
    return _plus_moe(x_ref[...], g0_ref, g1_ref, wts_ref, gate_ref[...]), refs[5:]


def _pending_prompt(pending, mod_p, layer):
    if pending is None:
        return [], []
    gathered, wts = pending
    specs = [
        pl.BlockSpec((TM, D // 2), lambda i: (i, 0)),
        pl.BlockSpec((TM, D // 2), lambda i: (i + N_TILES, 0)),
        pl.BlockSpec((TM, LANES), lambda i: (i, 0)),
    ] + _mod_specs_prompt(layer - 1, (5,))
    return specs, [gathered, gathered, wts, mod_p]


def _pending_sample(pending, mod_s, layer, rows):
    if pending is None:
        return [], []
    gathered, wts = pending
    first = T_PROMPT // rows
    second = (T_PAD + T_PROMPT) // rows
    specs = [
        pl.BlockSpec((rows, D // 2), lambda i: (first + i, 0)),
        pl.BlockSpec((rows, D // 2), lambda i: (second + i, 0)),
        pl.BlockSpec((rows, LANES), lambda i: (first + i, 0)),
    ] + _mod_specs_sample(layer - 1, (5,), rows)
    return specs, [gathered, gathered, wts, mod_s]


def _mix_a_front(x, sh, sc, ng, win_ref, lng_ref, lnb_ref):
    h = (_rms(x, ng) * (1.0 + sc) + sh).astype(BF16)
    z = jax.nn.gelu(_dot(h, win_ref[...]))
    u = z[:, :D]
    v = _ln(z[:, D:], lng_ref[...], lnb_ref[...])
    return u, v


def _mix_a_prompt_kernel(*refs, pending, first, stack):
    x, refs = _residual_in(refs, pending)
    (sh_ref, sc_ref, gt_ref, ng_ref, win_ref, lng_ref, lnb_ref, ws_ref, bs_ref, wout_ref,
     *cv_prev_ref, x1_ref, cv_ref, mixed_ref) = refs
    i = pl.program_id(0)
    u, v = _mix_a_front(x, sh_ref[...], sc_ref[...], ng_ref[...], win_ref, lng_ref, lnb_ref)
    vb = v.astype(BF16)
    n_chunks = TM // CHUNK
    row = lax.broadcasted_iota(jnp.int32, (CHUNK, CHUNK), 0)
    col = lax.broadcasted_iota(jnp.int32, (CHUNK, CHUNK), 1)
    tril = row >= col
    hd_dim = D // A_HEADS
    for hd in range(A_HEADS):
        ws = jnp.where(tril, ws_ref[hd], 0.0).astype(BF16)
        cols = slice(hd * hd_dim, (hd + 1) * hd_dim)
        vcat = jnp.concatenate([vb[c * CHUNK:(c + 1) * CHUNK, cols] for c in range(n_chunks)], axis=1)
        m = _dot(ws, vcat)
        for c in range(n_chunks):
            mixed_ref[c * CHUNK:(c + 1) * CHUNK, cols] = m[:, c * hd_dim:(c + 1) * hd_dim] + bs_ref[hd]
    y = _dot((u * mixed_ref[...]).astype(BF16), wout_ref[...])
    out = x + gt_ref[...] * y
    if first:
        out = jnp.where(i < N_PROMPT_TILES, out, 0.0)
    x1_ref[...] = out

    @pl.when(i % TILES_PER_SEQ == TILES_PER_SEQ - 1)
    def _():
        if stack:
            cv_ref[0] = cv_prev_ref[0][...]
            cv_ref[1] = v[TM - CHUNK:, :]
        else:
            cv_ref[...] = v[TM - CHUNK:, :]


def _mix_a_sample_kernel(*refs, pending, first):
    if first:
        refs = refs[:1] + refs[2:]
    x, refs = _residual_in(refs, pending)
    (sh_ref, sc_ref, gt_ref, ng_ref, win_ref, lng_ref, lnb_ref, wsd_ref, bsd_ref, wout_ref,
     x1_ref, cv_ref) = refs
    u, v = _mix_a_front(x, sh_ref[...], sc_ref[...], ng_ref[...], win_ref, lng_ref, lnb_ref)
    mixed = wsd_ref[...] * v + bsd_ref[...]
    y = _dot((u * mixed).astype(BF16), wout_ref[...])
    x1_ref[...] = x + gt_ref[...] * y
    cv_ref[...] = v


def _mixer_a(x_in, pending, cv_prev, mod_p, mod_s, ng, layer, la, w_in, ln_g, ln_b, w_s, b_s,
             w_out):
    first = isinstance(x_in, tuple)
    stack = cv_prev is not None
    assert not stack or la == 1
    x_all = x_in[0] if first else x_in
    bs_b = jnp.broadcast_to(b_s[la][:, :, None], (A_HEADS, CHUNK, D // A_HEADS))
    common = [
        _layer_block((D, 2 * D), la), _layer_block((1, D), la), _layer_block((1, D), la),
    ]
    has_pending = pending is not None
    pend_specs, pend_args = _pending_prompt(pending, mod_p, layer)
    last_tile = N_PROMPT_TILES - 1

    def seq_of(i):
        return jnp.minimum(i, last_tile) // TILES_PER_SEQ

    if stack:
        cv_in_specs = [pl.BlockSpec((None, CHUNK, D), lambda i: (seq_of(i), 0, 0))]
        cv_in_args = [cv_prev]
        cv_spec = pl.BlockSpec((2, None, CHUNK, D), lambda i: (0, seq_of(i), 0, 0))
        cv_shape = jax.ShapeDtypeStruct((2, BATCH, CHUNK, D), F32)
    else:
        cv_in_specs, cv_in_args = [], []
        cv_spec = pl.BlockSpec((None, CHUNK, D), lambda i: (seq_of(i), 0, 0))
        cv_shape = jax.ShapeDtypeStruct((BATCH, CHUNK, D), F32)
    x_all, cv_p = pl.pallas_call(
        functools.partial(_mix_a_prompt_kernel, pending=has_pending, first=first, stack=stack),
        grid=(N_TILES if first else N_PROMPT_TILES,),
        in_specs=[pl.BlockSpec((TM, D), lambda i: (jnp.minimum(i, last_tile), 0))] + pend_specs
        + _mod_specs_prompt(layer, (0, 1, 2))
        + [_full((1, D))] + common
        + [_layer_block((A_HEADS, CHUNK, CHUNK), la), _full((A_HEADS, CHUNK, D // A_HEADS)),
           _layer_block((D, D), la)] + cv_in_specs,
        out_specs=[pl.BlockSpec((TM, D), lambda i: (i, 0)), cv_spec],
        out_shape=[jax.ShapeDtypeStruct((T_PAD, D), F32), cv_shape],
        scratch_shapes=[pltpu.VMEM((TM, D), F32)],
        input_output_aliases={} if first else {0: 0},
        compiler_params=_cparams("arbitrary"),
        name="mixer_a_prompt",
    )(x_all, *pend_args, mod_p, mod_p, mod_p, ng, w_in, ln_g, ln_b, w_s, bs_b, w_out,
      *cv_in_args)

    wsd = jnp.repeat(w_s[la, :, 0, 0], D // A_HEADS).reshape(1, D)
    bsd = jnp.repeat(b_s[la, :, 0], D // A_HEADS).reshape(1, D)
    sblk = T_PROMPT // DEC_BATCH
    pend_specs, pend_args = _pending_sample(pending, mod_s, layer, DEC_BATCH)
    if first:
        x_specs = [pl.BlockSpec((DEC_BATCH, D), lambda i: (0, 0)), pl.BlockSpec(memory_space=pl.ANY)]
        x_args = [x_in[1], x_all]
    else:
        x_specs = [pl.BlockSpec((DEC_BATCH, D), lambda i: (sblk, 0))]
        x_args = [x_all]
    x_all, cv_s = pl.pallas_call(
        functools.partial(_mix_a_sample_kernel, pending=has_pending, first=first),
        grid=(1,),
        in_specs=x_specs + pend_specs
        + _mod_specs_sample(layer, (0, 1, 2), DEC_BATCH)
        + [_full((1, D))] + common
        + [_full((1, D)), _full((1, D)), _layer_block((D, D), la)],
        out_specs=[
            pl.BlockSpec((DEC_BATCH, D), lambda i: (sblk, 0)),
            pl.BlockSpec((DEC_BATCH, D), lambda i: (0, 0)),
        ],
        out_shape=[
            jax.ShapeDtypeStruct((T_PAD, D), F32),
            jax.ShapeDtypeStruct((DEC_BATCH, D), F32),
        ],
        input_output_aliases={len(x_args) - 1: 0},
        compiler_params=_cparams("arbitrary"),
        name="mixer_a_sample",
    )(*x_args, *pend_args, mod_s, mod_s, mod_s, ng, w_in, ln_g, ln_b, wsd, bsd, w_out)
    return x_all, cv_p, cv_s.reshape(DEC_BATCH, 1, D)


def _mix_b_tail(pooled_groups, wgrp_ref, bgrp_ref, scale_ref, wout_ref):
    outs = [
        _dot(pg.astype(BF16), wgrp_ref[g]) + bgrp_ref[g]
        for g, pg in enumerate(pooled_groups)
    ]
    mixed = jnp.concatenate(outs, axis=1) * scale_ref[...]
    return _dot(mixed.astype(BF16), wout_ref[...])


def _mix_b_prompt_kernel(*refs, pending):
    x, refs = _residual_in(refs, pending)
    (sh_ref, sc_ref, gt_ref, ng_ref, win_ref, wgrp_ref, bgrp_ref, scale_ref, wout_ref,
     x1_ref, st_ref, full_ref, s2_ref, s4_ref, s8_ref) = refs
    assert POOL_WINDOWS == (2, 4, 8, 16)
    i = pl.program_id(0)
    j = i % TILES_PER_SEQ
    h = (_rms(x, ng_ref[...]) * (1.0 + sc_ref[...]) + sh_ref[...]).astype(BF16)
    p = _dot(h, win_ref[...])
    gd = B_GROUP_DIM
    top = SUBLANES
    cur = top + POOL_CARRY
    rows = TM + POOL_CARRY

    @pl.when(i == 0)
    def _():
        for ref in (full_ref, s2_ref, s4_ref, s8_ref):
            ref[0:top, :] = jnp.zeros((top, ref.shape[1]), F32)

    full_ref[top:cur, :] = jnp.where(j == 0, 0.0, full_ref[top:cur, :])
    full_ref[cur:cur + TM, :] = p
    s2_ref[top:top + rows, :] = full_ref[top:top + rows, :] + full_ref[pl.ds(top - 1, rows), :]
    s4_ref[top:top + rows, :] = s2_ref[top:top + rows, gd:] + s2_ref[pl.ds(top - 2, rows), gd:]
    s8_ref[top:top + rows, :] = s4_ref[top:top + rows, gd:] + s4_ref[pl.ds(top - 4, rows), gd:]
    s16 = s8_ref[cur:cur + TM, gd:] + s8_ref[pl.ds(cur - 8, TM), gd:]
    sums = [s2_ref[cur:cur + TM, 0:gd], s4_ref[cur:cur + TM, 0:gd], s8_ref[cur:cur + TM, 0:gd], s16]
    pos = j * TM + lax.broadcasted_iota(jnp.int32, (TM, 1), 0)
    pooled = []
    for g, w in enumerate(POOL_WINDOWS):
        cnt = jnp.minimum(w, pos + 1).astype(F32)
        pooled.append(sums[g] / cnt - p[:, g * gd:(g + 1) * gd])
    y = _mix_b_tail(pooled, wgrp_ref, bgrp_ref, scale_ref, wout_ref)
    x1_ref[...] = x + gt_ref[...] * y

    @pl.when(j == TILES_PER_SEQ - 1)
    def _():
        st_ref[...] = full_ref[pl.ds(cur + TM - POOL_BUF, POOL_BUF), :]

    full_ref[top:cur, :] = full_ref[pl.ds(TM + top, POOL_CARRY), :]


def _mix_b_sample_kernel(*refs, pending):
    x, refs = _residual_in(refs, pending)
    (sh_ref, sc_ref, gt_ref, ng_ref, st_ref, win_ref, wgrp_ref, bgrp_ref, scale_ref, wout_ref,
     x1_ref, nst_ref) = refs
    h = (_rms(x, ng_ref[...]) * (1.0 + sc_ref[...]) + sh_ref[...]).astype(BF16)
    p = _dot(h, win_ref[...])
    pooled = []
    for g, w in enumerate(POOL_WINDOWS):
        s = p[:, g * B_GROUP_DIM:(g + 1) * B_GROUP_DIM]
        for k in range(1, w):
            s = s + st_ref[POOL_BUF - k, :, g * B_GROUP_DIM:(g + 1) * B_GROUP_DIM]
        cnt = float(min(w, PAST_LEN + 1))
        pooled.append(s / cnt - p[:, g * B_GROUP_DIM:(g + 1) * B_GROUP_DIM])
    y = _mix_b_tail(pooled, wgrp_ref, bgrp_ref, scale_ref, wout_ref)
    x1_ref[...] = x + gt_ref[...] * y
    for k in range(1, POOL_BUF):
        nst_ref[k - 1] = st_ref[k]
    nst_ref[POOL_BUF - 1] = p


def _mixer_b(x_all, pending, mod_p, mod_s, ng, layer, lb, state, w_in, w_grp, b_grp, scale,
             w_out):
    gd = B_GROUP_DIM
    common = [
        _layer_block((D, D), lb), _layer_block((len(POOL_WINDOWS), gd, gd), lb),
        _layer_block((len(POOL_WINDOWS), 1, gd), lb), _layer_block((1, D), lb),
        _layer_block((D, D), lb),
    ]
    has_pending = pending is not None
    pend_specs, pend_args = _pending_prompt(pending, mod_p, layer)
    x_all, st_p = pl.pallas_call(
        functools.partial(_mix_b_prompt_kernel, pending=has_pending),
        grid=(N_PROMPT_TILES,),
        in_specs=[pl.BlockSpec((TM, D), lambda i: (i, 0))] + pend_specs
        + _mod_specs_prompt(layer, (0, 1, 2)) + [_full((1, D))] + common,
        out_specs=[
            pl.BlockSpec((TM, D), lambda i: (i, 0)),
            pl.BlockSpec((None, POOL_BUF, D), lambda i: (i // TILES_PER_SEQ, 0, 0)),
        ],
        out_shape=[
            jax.ShapeDtypeStruct((T_PAD, D), F32),
            jax.ShapeDtypeStruct((BATCH, POOL_BUF, D), F32),
        ],
        scratch_shapes=[
            pltpu.VMEM((SUBLANES + POOL_CARRY + TM, D - g * gd), F32) for g in (0, 0, 1, 2)],
        input_output_aliases={0: 0},
        compiler_params=_cparams("arbitrary"),
        name="mixer_b_prompt",
    )(x_all, *pend_args, mod_p, mod_p, mod_p, ng, w_in, w_grp, b_grp, scale, w_out)

    r = SAMPLE_ROWS
    sblk = T_PROMPT // r
    pend_specs, pend_args = _pending_sample(pending, mod_s, layer, r)
    x_all, st_s = pl.pallas_call(
        functools.partial(_mix_b_sample_kernel, pending=has_pending),
        grid=(DEC_BATCH // r,),
        in_specs=[pl.BlockSpec((r, D), lambda i: (sblk + i, 0))] + pend_specs
        + _mod_specs_sample(layer, (0, 1, 2), r) + [_full((1, D))]
        + [pl.BlockSpec((None, POOL_BUF, r, D), lambda i: (lb, 0, i, 0))] + common,
        out_specs=[
            pl.BlockSpec((r, D), lambda i: (sblk + i, 0)),
            pl.BlockSpec((POOL_BUF, r, D), lambda i: (0, i, 0)),
        ],
        out_shape=[
            jax.ShapeDtypeStruct((T_PAD, D), F32),
            jax.ShapeDtypeStruct((POOL_BUF, DEC_BATCH, D), F32),
        ],
        input_output_aliases={0: 0},
        compiler_params=_cparams("arbitrary"),
        name="mixer_b_sample",
    )(x_all, *pend_args, mod_s, mod_s, mod_s, ng, state, w_in, w_grp, b_grp, scale, w_out)
    return x_all, st_p, jnp.transpose(st_s, (1, 0, 2))


def _mix_c_glu(x, sh, sc, ng, win_ref, bin_ref):
    h = (_rms(x, ng) * (1.0 + sc) + sh).astype(BF16)
    ag = _dot(h, win_ref[...]) + bin_ref[...]
    return ag[:, :D] * jax.nn.sigmoid(ag[:, D:])


def _mix_c_tail(conv, lng_ref, lnb_ref, wout_ref, bout_ref):
    z = _ln(conv, lng_ref[...], lnb_ref[...])
    z = z * jax.nn.sigmoid(z)
    return _dot(z.astype(BF16), wout_ref[...]) + bout_ref[...]


def _mix_c_prompt_kernel(*refs, pending):
    x, refs = _residual_in(refs, pending)
    (sh_ref, sc_ref, gt_ref, ng_ref, win_ref, bin_ref, wdw_ref, bdw_ref, lng_ref, lnb_ref,
     wout_ref, bout_ref, x1_ref, st_ref, full_ref, conv_ref, *shift_refs) = refs
    i = pl.program_id(0)
    j = i % TILES_PER_SEQ
    glu = _mix_c_glu(x, sh_ref[...], sc_ref[...], ng_ref[...], win_ref, bin_ref)

    @pl.when(j == 0)
    def _():
        full_ref[0:CONV_CARRY, :] = jnp.zeros((CONV_CARRY, D), F32)

    full_ref[CONV_CARRY:CONV_CARRY + TM, :] = glu
    off = CONV_CARRY - CONV_BUF
    span = TM + (off + CONV_BUF) // SUBLANES * SUBLANES - SUBLANES
    for c in range(D // LANES):
        cols = slice(c * LANES, (c + 1) * LANES)
        shift_ref = shift_refs[c % len(shift_refs)]
        for r in range(1, SUBLANES):
            shift_ref[r - 1] = full_ref[pl.ds(r, span), cols]
        for rb in range(TM // CONV_ROWS):
            acc = None
            for k in range(CONV_WIDTH):
                q, r = divmod(off + k, SUBLANES)
                start = q * SUBLANES + rb * CONV_ROWS
                if r == 0:
                    src = full_ref[pl.ds(start, CONV_ROWS), cols]
                else:
                    src = shift_ref[r - 1, pl.ds(start, CONV_ROWS), :]
                term = src * wdw_ref[k:k + 1, cols]
                acc = term if acc is None else acc + term
            conv_ref[pl.ds(rb * CONV_ROWS, CONV_ROWS), cols] = acc + bdw_ref[:, cols]
    y = _mix_c_tail(conv_ref[...], lng_ref, lnb_ref, wout_ref, bout_ref)
    x1_ref[...] = x + gt_ref[...] * y

    @pl.when(j == TILES_PER_SEQ - 1)
    def _():
        st_ref[...] = full_ref[pl.ds(CONV_CARRY + TM - CONV_BUF, CONV_BUF), :]

    full_ref[0:CONV_CARRY, :] = full_ref[pl.ds(TM, CONV_CARRY), :]


def _mix_c_sample_kernel(*refs, pending):
    x, refs = _residual_in(refs, pending)
    (sh_ref, sc_ref, gt_ref, ng_ref, st_ref, win_ref, bin_ref, wdw_ref, bdw_ref, lng_ref,
     lnb_ref, wout_ref, bout_ref, x1_ref, nst_ref) = refs
    glu = _mix_c_glu(x, sh_ref[...], sc_ref[...], ng_ref[...], win_ref, bin_ref)
    acc = glu * wdw_ref[CONV_BUF:CONV_BUF + 1, :]
    for k in range(CONV_BUF):
        acc = acc + st_ref[k] * wdw_ref[k:k + 1, :]
    y = _mix_c_tail(acc + bdw_ref[...], lng_ref, lnb_ref, wout_ref, bout_ref)
    x1_ref[...] = x + gt_ref[...] * y
    for k in range(1, CONV_BUF):
        nst_ref[k - 1] = st_ref[k]
    nst_ref[CONV_BUF - 1] = glu


def _mixer_c(x_all, pending, mod_p, mod_s, ng, layer, lc, state, w_in, b_in, w_dw, b_dw, ln_g,
             ln_b, w_out, b_out):
    common = [
        _layer_block((D, 2 * D), lc), _layer_block((1, 2 * D), lc),
        _layer_block((CONV_WIDTH, D), lc), _layer_block((1, D), lc), _layer_block((1, D), lc),
        _layer_block((1, D), lc), _layer_block((D, D), lc), _layer_block((1, D), lc),
    ]
    has_pending = pending is not None
    pend_specs, pend_args = _pending_prompt(pending, mod_p, layer)
    x_all, st_p = pl.pallas_call(
        functools.partial(_mix_c_prompt_kernel, pending=has_pending),
        grid=(N_PROMPT_TILES,),
        in_specs=[pl.BlockSpec((TM, D), lambda i: (i, 0))] + pend_specs
        + _mod_specs_prompt(layer, (0, 1, 2)) + [_full((1, D))] + common,
        out_specs=[
            pl.BlockSpec((TM, D), lambda i: (i, 0)),
            pl.BlockSpec((None, CONV_BUF, D), lambda i: (i // TILES_PER_SEQ, 0, 0)),
        ],
        out_shape=[
            jax.ShapeDtypeStruct((T_PAD, D), F32),
            jax.ShapeDtypeStruct((BATCH, CONV_BUF, D), F32),
        ],
        scratch_shapes=[pltpu.VMEM((TM + CONV_CARRY, D), F32), pltpu.VMEM((TM, D), F32)]
        + [pltpu.VMEM((SUBLANES - 1, TM + CONV_CARRY - SUBLANES, LANES), F32)] * 2,
        input_output_aliases={0: 0},
        compiler_params=_cparams("arbitrary"),
        name="mixer_c_prompt",
    )(x_all, *pend_args, mod_p, mod_p, mod_p, ng, w_in, b_in, w_dw, b_dw, ln_g, ln_b, w_out,
      b_out)

    r = SAMPLE_ROWS
    sblk = T_PROMPT // r
    pend_specs, pend_args = _pending_sample(pending, mod_s, layer, r)
    x_all, st_s = pl.pallas_call(
        functools.partial(_mix_c_sample_kernel, pending=has_pending),
        grid=(DEC_BATCH // r,),
        in_specs=[pl.BlockSpec((r, D), lambda i: (sblk + i, 0))] + pend_specs
        + _mod_specs_sample(layer, (0, 1, 2), r) + [_full((1, D))]
        + [pl.BlockSpec((None, CONV_BUF, r, D), lambda i: (lc, 0, i, 0))] + common,
        out_specs=[
            pl.BlockSpec((r, D), lambda i: (sblk + i, 0)),
            pl.BlockSpec((CONV_BUF, r, D), lambda i: (0, i, 0)),
        ],
        out_shape=[
            jax.ShapeDtypeStruct((T_PAD, D), F32),
            jax.ShapeDtypeStruct((CONV_BUF, DEC_BATCH, D), F32),
        ],
        input_output_aliases={0: 0},
        compiler_params=_cparams("arbitrary"),
        name="mixer_c_sample",
    )(x_all, *pend_args, mod_s, mod_s, mod_s, ng, state, w_in, b_in, w_dw, b_dw, ln_g, ln_b,
      w_out, b_out)
    return x_all, st_p, jnp.transpose(st_s, (1, 0, 2))


def _tile_mod(i, p_ref, s_ref):
    s_rows = jnp.concatenate([s_ref[...], jnp.zeros((TM - DEC_BATCH, D), F32)], axis=0)
    return jnp.where(i >= N_PROMPT_TILES, s_rows, p_ref[...])


def _unified_mod_specs(layer, chunks):
    specs = []
    for c in chunks:
        specs.append(pl.BlockSpec(
            (None, None, 1, D),
            lambda i, c=c: (layer, jnp.minimum(i // TILES_PER_SEQ, BATCH - 1), 0, c)))
        specs.append(pl.BlockSpec((None, DEC_BATCH, D), lambda i, c=c: (layer, 0, c)))
    return specs


def _router_kernel(x_ref, shp_ref, shs_ref, scp_ref, scs_ref, ng_ref, wr_ref, br_ref,
                   h2_ref, meta_ref, wts_ref, cnt_ref, carry_ref, before_ref):
    i = pl.program_id(0)

    @pl.when(i == 0)
    def _():
        carry_ref[...] = jnp.zeros((ROUTER_ROWS, 1), F32)
        earlier = lax.broadcasted_iota(jnp.int32, (TM, TM), 0)
        token = lax.broadcasted_iota(jnp.int32, (TM, TM), 1)
        before_ref[...] = (earlier < token).astype(BF16)

    sh = _tile_mod(i, shp_ref, shs_ref)
    sc = _tile_mod(i, scp_ref, scs_ref)
    h2 = _rms(x_ref[...], ng_ref[...]) * (1.0 + sc) + sh
    h2_ref[...] = _pack_bf16_pairs(h2)
    lg = lax.dot_general(wr_ref[...], h2.astype(BF16), (((1,), (1,)), ((), ())),
                         preferred_element_type=F32) + br_ref[...]
    row = lax.broadcasted_iota(jnp.int32, (ROUTER_ROWS, TM), 0)
    row_f = row.astype(F32)
    far = float(ROUTER_ROWS)
    neg = -jnp.inf
    is_g = row < N_GROUPS
    gm = jnp.where(is_g, lg, neg)
    gmax = jnp.max(gm, axis=0, keepdims=True)
    gsel = jnp.min(jnp.where(gm == gmax, row_f, far), axis=0, keepdims=True).astype(jnp.int32)
    gsum = jnp.sum(jnp.where(is_g, jnp.exp(lg - gmax), 0.0), axis=0, keepdims=True)
    g_w = 1.0 / gsum
    e_row = row - EXPERT_LANE0
    in_grp = (e_row >= 0) & (e_row < N_EXPERTS) & (
        lax.shift_right_arithmetic(e_row, EXPERTS_PER_GROUP.bit_length() - 1) == gsel)
    em = jnp.where(in_grp, lg, neg)
    m1 = jnp.max(em, axis=0, keepdims=True)
    i1 = jnp.min(jnp.where(em == m1, row_f, far), axis=0, keepdims=True).astype(jnp.int32)
    em2 = jnp.where(row == i1, neg, em)
    m2 = jnp.max(em2, axis=0, keepdims=True)
    i2 = jnp.min(jnp.where(em2 == m2, row_f, far), axis=0, keepdims=True).astype(jnp.int32)
    e2 = jnp.exp(m2 - m1)
    den = 1.0 + e2
    w1 = (1.0 / den) * g_w
    w2 = (e2 / den) * g_w

    hit1 = row == i1
    hit2 = row == i2
    assign = jnp.where(hit1 | hit2, 1.0, 0.0).astype(BF16)
    seen = _dot(assign, before_ref[...]) + carry_ref[...]
    rank1 = jnp.sum(jnp.where(hit1, seen, 0.0), axis=0, keepdims=True)
    rank2 = jnp.sum(jnp.where(hit2, seen, 0.0), axis=0, keepdims=True)
    carry_ref[...] = carry_ref[...] + jnp.sum(assign.astype(F32), axis=1, keepdims=True)
    cnt_ref[...] = carry_ref[...].astype(jnp.int32)

    ex1 = (i1 - EXPERT_LANE0).astype(F32)
    ex2 = (i2 - EXPERT_LANE0).astype(F32)
    field = lax.broadcasted_iota(jnp.int32, (META_ROWS, TM), 0)
    meta_ref[...] = jnp.where(field == 0, ex1, jnp.where(field == 1, ex2,
                              jnp.where(field == 2, rank1, jnp.where(field == 3, rank2, 0.0))))
    slot = lax.broadcasted_iota(jnp.int32, (LANES, TM), 0)
    wts_ref[...] = jnp.where(slot == 0, w1, jnp.where(slot == 1, w2, 0.0)).T


def _router(x_all, mod_p, mod_s, ng, layer, w_r, b_r):
    return pl.pallas_call(
        _router_kernel,
        grid=(N_TILES,),
        in_specs=[pl.BlockSpec((TM, D), lambda i: (i, 0))]
        + _unified_mod_specs(layer, (3, 4))
        + [_full((1, D)), _full((ROUTER_ROWS, D)), _full((ROUTER_ROWS, 1))],
        out_specs=[
            pl.BlockSpec((TM, D // 2), lambda i: (i, 0)),
            pl.BlockSpec((META_ROWS, TM), lambda i: (0, i)),
            pl.BlockSpec((TM, LANES), lambda i: (i, 0)),
            pl.BlockSpec((ROUTER_ROWS, 1), lambda i: (0, 0)),
        ],
        out_shape=[
            jax.ShapeDtypeStruct((T_PAD, D // 2), jnp.int32),
            jax.ShapeDtypeStruct((META_ROWS, T_PAD), F32),
            jax.ShapeDtypeStruct((T_PAD, LANES), F32),
            jax.ShapeDtypeStruct((ROUTER_ROWS, 1), jnp.int32),
        ],
        scratch_shapes=[pltpu.VMEM((ROUTER_ROWS, 1), F32), pltpu.VMEM((TM, TM), BF16)],
        compiler_params=_cparams("arbitrary"),
        name="router",
    )(x_all, mod_p, mod_s, mod_p, mod_s, ng, w_r, b_r)


SC_INDEX_MAX = 128
SC_ROWS_BYTES = 400 * 1024


def _gather_chunk(per_worker, row_bytes, buffers=1):
    top = min(SC_INDEX_MAX, SC_ROWS_BYTES // (row_bytes * buffers)) // SUBLANES * SUBLANES
    for ch in range(top, SUBLANES - 1, -SUBLANES):
        if per_worker % ch == 0:
            return ch
    raise ValueError(per_worker)


def _sc_gather(table, idx):
    nc, nw = SC_CORES, SC_CORES * SC_SUBCORES
    m = idx.shape[0]
    per_w = m // nw
    assert per_w * nw == m
    width = table.shape[1]
    ch = _gather_chunk(per_w, width * table.dtype.itemsize, buffers=2)
    mesh = plsc.VectorSubcoreMesh(core_axis_name="c", subcore_axis_name="s")

    n_chunks = per_w // ch

    @functools.partial(
        pl.kernel,
        out_type=jax.ShapeDtypeStruct((m, width), table.dtype),
        mesh=mesh,
        scratch_types=[pltpu.VMEM((ch,), jnp.int32)] * 2
        + [pltpu.VMEM((ch, width), table.dtype)] * 2
        + [pltpu.SemaphoreType.DMA] * 4,
    )
    def gather_kernel(t_hbm, i_hbm, o_hbm, idx_a, idx_b, rows_a, rows_b, sg_a, sg_b, sw_a, sw_b):
        wid = lax.axis_index("s") * nc + lax.axis_index("c")
        base = wid * per_w
        idx_v, rows_v, sem_g, sem_w = (idx_a, idx_b), (rows_a, rows_b), (sg_a, sg_b), (sw_a, sw_b)

        def out_rows(j):
            return o_hbm.at[pl.ds(pl.multiple_of(base + j * ch, 8), ch)]

        gathers, writes = {}, {}
        for j in range(n_chunks):
            b = j % 2
            if j >= 2:
                writes[j - 2].wait()
            pltpu.sync_copy(i_hbm.at[pl.ds(pl.multiple_of(base + j * ch, 8), ch)], idx_v[b])
            gathers[j] = pltpu.async_copy(t_hbm.at[idx_v[b]], rows_v[b], sem_g[b])
            if j >= 1:
                gathers[j - 1].wait()
                writes[j - 1] = pltpu.async_copy(rows_v[1 - b], out_rows(j - 1), sem_w[1 - b])
        last = n_chunks - 1
        gathers[last].wait()
        writes[last] = pltpu.async_copy(rows_v[last % 2], out_rows(last), sem_w[last % 2])
        for j in range(max(last - 1, 0), n_chunks):
            writes[j].wait()

    return gather_kernel(table, idx)


def _sc_scatter2(rows, dest, n_out):
    nc, nw = SC_CORES, SC_CORES * SC_SUBCORES
    n, width = rows.shape
    assert dest.shape == (2 * n,) and n % SUBLANES == 0
    per_w = n // nw
    assert per_w * nw == n
    ch = _gather_chunk(per_w, width * rows.dtype.itemsize)
    mesh = plsc.VectorSubcoreMesh(core_axis_name="c", subcore_axis_name="s")

    @functools.partial(
        pl.kernel,
        out_type=jax.ShapeDtypeStruct((n_out, width), rows.dtype),
        mesh=mesh,
        scratch_types=[
            pltpu.VMEM((ch,), jnp.int32),
            pltpu.VMEM((ch,), jnp.int32),
            pltpu.VMEM((ch, width), rows.dtype),
            pltpu.SemaphoreType.DMA,
            pltpu.SemaphoreType.DMA,
        ],
    )
    def scatter_kernel(r_hbm, d_hbm, o_hbm, i0_v, i1_v, rows_v, sem0, sem1):
        wid = lax.axis_index("s") * nc + lax.axis_index("c")
        base = wid * per_w

        @pl.loop(0, per_w // ch)
        def _(j):
            off = pl.multiple_of(base + j * ch, 8)
            pltpu.sync_copy(d_hbm.at[pl.ds(off, ch)], i0_v)
            pltpu.sync_copy(d_hbm.at[pl.ds(pl.multiple_of(n + off, 8), ch)], i1_v)
            pltpu.sync_copy(r_hbm.at[pl.ds(off, ch)], rows_v)
            c0 = pltpu.async_copy(rows_v, o_hbm.at[i0_v], sem0)
            c1 = pltpu.async_copy(rows_v, o_hbm.at[i1_v], sem1)
            c0.wait()
            c1.wait()

    return scatter_kernel(rows, dest)


def _expert_kernel(be_ref, first_ref, nvalid_ref, next_ref, nused_ref, x_ref, wg_hbm, wu_hbm,
                   wd_hbm, y_ref, wgf_ref, wuf_ref, wdf_ref, wgb_ref, wub_ref, wdb_ref, sems,
                   *, layer):
    b = pl.program_id(0)

    def weight_copies(e):
        return (
            pltpu.make_async_copy(wg_hbm.at[layer, e], wgf_ref, sems.at[0]),
            pltpu.make_async_copy(wu_hbm.at[layer, e], wuf_ref, sems.at[1]),
            pltpu.make_async_copy(wd_hbm.at[layer, e], wdf_ref, sems.at[2]),
        )

    @pl.when(b == 0)
    def _():
        for cp in weight_copies(be_ref[0]):
            cp.start()

    @pl.when(b < nused_ref[0])
    def _():
        @pl.when(first_ref[b] == 1)
        def _():
            for cp in weight_copies(be_ref[b]):
                cp.wait()
            wgb_ref[...] = wgf_ref[...].astype(BF16)
            wub_ref[...] = wuf_ref[...].astype(BF16)
            wdb_ref[...] = wdf_ref[...].astype(BF16)

            @pl.when(next_ref[b] >= 0)
            def _():
                for cp in weight_copies(next_ref[b]):
                    cp.start()

        def run(rows):
            live = lax.broadcasted_iota(jnp.int32, (rows, 1), 0) < nvalid_ref[b]
            x = _unpack_bf16_pairs(jnp.where(live, x_ref[0:rows, :], 0), BF16)
            g = _dot(x, wgb_ref[...])
            u = _dot(x, wub_ref[...])
            hmid = (g * jax.nn.sigmoid(g)) * u
            y_ref[0:rows, :] = _pack_bf16_pairs(_dot(hmid.astype(BF16), wdb_ref[...]))

        half = BM // 2

        @pl.when(nvalid_ref[b] > half)
        def _():
            run(BM)

        @pl.when(nvalid_ref[b] <= half)
        def _():
            run(half)
            y_ref[half:BM, :] = jnp.zeros((BM - half, D // 2), jnp.int32)

    @pl.when(b >= nused_ref[0])
    def _():
        y_ref[...] = jnp.zeros((BM, D // 2), jnp.int32)


def _experts(x_sorted, block_expert, first, nvalid, next_expert, nused, layer, w_gate, w_up,
             w_down):
    def row_map(b, be, fi, nv, nx, nu):
        return (jnp.minimum(b, nu[0] - 1), 0)

    def out_map(b, be, fi, nv, nx, nu):
        return (b, 0)

    grid_spec = pltpu.PrefetchScalarGridSpec(
        num_scalar_prefetch=5,
        grid=(N_BLOCKS,),
        in_specs=[
            pl.BlockSpec((BM, D // 2), row_map),
            pl.BlockSpec(memory_space=pl.ANY),
            pl.BlockSpec(memory_space=pl.ANY),
            pl.BlockSpec(memory_space=pl.ANY),
        ],
        out_specs=pl.BlockSpec((BM, D // 2), out_map),
        scratch_shapes=[
            pltpu.VMEM((D, D_EXPERT), F32),
            pltpu.VMEM((D, D_EXPERT), F32),
            pltpu.VMEM((D_EXPERT, D), F32),
            pltpu.VMEM((D, D_EXPERT), BF16),
            pltpu.VMEM((D, D_EXPERT), BF16),
            pltpu.VMEM((D_EXPERT, D), BF16),
            pltpu.SemaphoreType.DMA((3,)),
        ],
    )
    return pl.pallas_call(
        functools.partial(_expert_kernel, layer=layer),
        grid_spec=grid_spec,
        out_shape=jax.ShapeDtypeStruct((P_ROWS, D // 2), jnp.int32),
        compiler_params=_cparams("arbitrary"),
        name="experts",
    )(block_expert, first, nvalid, next_expert, nused, x_sorted, w_gate, w_up, w_down)


def _combine_final_kernel(x_ref, g0_ref, g1_ref, wts_ref, gtp_ref, gts_ref, fg_ref, yp_ref,
                          ys_ref):
    i = pl.program_id(0)
    gate = _tile_mod(i, gtp_ref, gts_ref)
    y = _rms(_plus_moe(x_ref[...], g0_ref, g1_ref, wts_ref, gate), fg_ref[...])

    @pl.when(i < N_PROMPT_TILES)
    def _():
        yp_ref[...] = y

    @pl.when(i == N_PROMPT_TILES)
    def _():
        ys_ref[...] = y[0:DEC_BATCH, :]


def _combine_final(x_all, gathered, wts, mod_p, mod_s, layer, final_g):
    in_specs = [
        pl.BlockSpec((TM, D), lambda i: (i, 0)),
        pl.BlockSpec((TM, D // 2), lambda i: (i, 0)),
        pl.BlockSpec((TM, D // 2), lambda i: (i + N_TILES, 0)),
        pl.BlockSpec((TM, LANES), lambda i: (i, 0)),
    ] + _unified_mod_specs(layer, (5,))
    return pl.pallas_call(
        _combine_final_kernel,
        grid=(N_PROMPT_TILES + 1,),
        in_specs=in_specs + [_full((1, D))],
        out_specs=[
            pl.BlockSpec((TM, D), lambda i: (jnp.minimum(i, N_PROMPT_TILES - 1), 0)),
            pl.BlockSpec((DEC_BATCH, D), lambda i: (0, 0)),
        ],
        out_shape=[
            jax.ShapeDtypeStruct((T_PROMPT, D), F32),
            jax.ShapeDtypeStruct((DEC_BATCH, D), F32),
        ],
        compiler_params=_cparams("arbitrary"),
        name="combine_final",
    )(x_all, gathered, gathered, wts, mod_p, mod_s, final_g)


def _plan_kernel(cnt_ref, meta_ref, d_ref, be_ref, first_ref, nvalid_ref, next_ref,
                 nused_ref, pstart_ref, nxt_ref):
    shift = BM.bit_length() - 1
    assert 1 << shift == BM

    def count(e):
        return cnt_ref[EXPERT_LANE0 + e, 0]

    def scan_back(k, nxt):
        e = N_EXPERTS - 1 - k
        nxt_ref[e] = nxt
        return jnp.where(count(e) > 0, e, nxt)

    lax.fori_loop(0, N_EXPERTS, scan_back, jnp.int32(-1))

    def fill(e, blk):
        c = count(e)
        pstart_ref[e] = blk << shift

        def one(j, carry):
            be_ref[blk + j] = e
            first_ref[blk + j] = jnp.where(j == 0, 1, 0)
            nvalid_ref[blk + j] = jnp.minimum(c - (j << shift), BM)
            next_ref[blk + j] = nxt_ref[e]
            return carry

        n_blk = (c + (BM - 1)) >> shift
        lax.fori_loop(0, n_blk, one, 0)
        return blk + n_blk

    used = lax.fori_loop(0, N_EXPERTS, fill, jnp.int32(0))
    nused_ref[0] = used

    def tail(b, carry):
        be_ref[b] = 0
        first_ref[b] = 0
        nvalid_ref[b] = 0
        next_ref[b] = -1
        return carry

    lax.fori_loop(used, N_BLOCKS, tail, 0)

    m = meta_ref[...]
    expert = m[0:2, :]
    dest = m[2:4, :]
    for e in range(N_EXPERTS):
        dest = dest + jnp.where(expert == float(e), pstart_ref[e].astype(F32), 0.0)
    dest = dest.astype(jnp.int32)
    d_ref[:, 0:T_PAD] = dest[0:1, :]
    d_ref[:, T_PAD:2 * T_PAD] = dest[1:2, :]


def _plan(cnt, meta):
    smem = pl.BlockSpec(memory_space=pltpu.SMEM)
    blocks = jax.ShapeDtypeStruct((N_BLOCKS,), jnp.int32)
    return pl.pallas_call(
        _plan_kernel,
        grid=(1,),
        in_specs=[smem, _full((META_ROWS, T_PAD))],
        out_specs=[_full((1, 2 * T_PAD)), smem, smem, smem, smem, smem],
        out_shape=[
            jax.ShapeDtypeStruct((1, 2 * T_PAD), jnp.int32),
            blocks, blocks, blocks, blocks,
            jax.ShapeDtypeStruct((1,), jnp.int32),
        ],
        scratch_shapes=[pltpu.SMEM((N_EXPERTS,), jnp.int32), pltpu.SMEM((N_EXPERTS,), jnp.int32)],
        compiler_params=_cparams("arbitrary"),
        name="plan",
    )(cnt, meta)


def _moe(x_all, mod_p, mod_s, ng, layer, w_r, b_r, w_gate, w_up, w_down):
    h2, meta, wts, cnt = _router(x_all, mod_p, mod_s, ng, layer, w_r, b_r)
    dest, block_expert, first, nvalid, next_expert, nused = _plan(cnt, meta)
    dest = dest.reshape(2 * T_PAD)
    x_sorted = _sc_scatter2(h2, dest, P_ROWS)
    y_sorted = _experts(x_sorted, block_expert, first, nvalid, next_expert, nused, layer,
                        w_gate, w_up, w_down)
    gathered = _sc_gather(y_sorted, dest)
    return gathered, wts


def kernel(x_prompt, x_sample, c_prompt, c_sample, state_pool, state_conv, ada_w, ada_b, norm_g, final_g, a_w_in, a_ln_g, a_ln_b, a_w_s, a_b_s, a_w_out, b_w_in, b_w_grp, b_b_grp, b_scale, b_w_out, c_w_in, c_b_in, c_w_dw, c_b_dw, c_ln_g, c_ln_b, c_w_out, c_b_out, moe_w_grp, moe_b_grp, moe_w_exp, moe_b_exp, moe_w_gate, moe_w_up, moe_w_down):
    x_all = (x_prompt.reshape(T_PROMPT, D), x_sample.reshape(DEC_BATCH, D))
    mod_p, mod_s = _ada_mods(jnp.concatenate([c_prompt, c_sample], axis=0), ada_w, ada_b)
    mod_p = mod_p.reshape(DEPTH, BATCH, 1, 6 * D)

    a_w_in_b, a_w_out_b = a_w_in.astype(BF16), a_w_out.astype(BF16)
    b_w_in_b, b_w_grp_b, b_w_out_b = b_w_in.astype(BF16), b_w_grp.astype(BF16), b_w_out.astype(BF16)
    c_w_in_b, c_w_out_b = c_w_in.astype(BF16), c_w_out.astype(BF16)
    pad_rows = ROUTER_ROWS - N_GROUPS - N_EXPERTS
    w_r = jnp.pad(jnp.swapaxes(jnp.concatenate([moe_w_grp, moe_w_exp], axis=2), 1, 2),
                  ((0, 0), (0, pad_rows), (0, 0))).astype(BF16)
    b_r = jnp.pad(jnp.concatenate([moe_b_grp, moe_b_exp], axis=1), ((0, 0), (0, pad_rows)))

    new_a_p, new_a_s, new_b_p, new_b_s, new_c_p, new_c_s = [], [], [], [], [], []
    pool_tm = jnp.transpose(state_pool, (0, 2, 1, 3))
    conv_tm = jnp.transpose(state_conv, (0, 2, 1, 3))
    ia = ib = ic = 0
    assert len(range(0, DEPTH, 3)) == 2
    pending = None
    for layer in range(DEPTH):
        ng1 = norm_g[layer, 0].reshape(1, D)
        ng2 = norm_g[layer, 1].reshape(1, D)
        kind = layer % 3
        if kind == 0:
            x_all, st_p, st_s = _mixer_a(
                x_all, pending, new_a_p[0] if new_a_p else None, mod_p, mod_s, ng1, layer, ia,
                a_w_in_b, a_ln_g.reshape(-1, 1, D), a_ln_b.reshape(-1, 1, D), a_w_s, a_b_s,
                a_w_out_b)
            new_a_p = [st_p]
            new_a_s.append(st_s)
            ia += 1
        elif kind == 1:
            x_all, st_p, st_s = _mixer_b(
                x_all, pending, mod_p, mod_s, ng1, layer, ib, pool_tm, b_w_in_b, b_w_grp_b,
                b_b_grp.reshape(-1, len(POOL_WINDOWS), 1, B_GROUP_DIM),
                b_scale.reshape(-1, 1, D), b_w_out_b)
            new_b_p.append(st_p)
            new_b_s.append(st_s)
            ib += 1
        else:
            x_all, st_p, st_s = _mixer_c(
                x_all, pending, mod_p, mod_s, ng1, layer, ic, conv_tm, c_w_in_b,
                c_b_in.reshape(-1, 1, 2 * D), c_w_dw, c_b_dw.reshape(-1, 1, D),
                c_ln_g.reshape(-1, 1, D), c_ln_b.reshape(-1, 1, D), c_w_out_b,
                c_b_out.reshape(-1, 1, D))
            new_c_p.append(st_p)
            new_c_s.append(st_s)
            ic += 1
        pending = _moe(x_all, mod_p, mod_s, ng2, layer, w_r[layer],
                       b_r[layer].reshape(ROUTER_ROWS, 1), moe_w_gate, moe_w_up, moe_w_down)

    y_p, y_s = _combine_final(x_all, *pending, mod_p, mod_s, DEPTH - 1, final_g.reshape(1, D))
    return (y_p.reshape(BATCH, SEQ, D), y_s.reshape(DEC_BATCH, 1, D),
            new_a_p[0], jnp.stack(new_a_s), jnp.stack(new_b_p), jnp.stack(new_b_s),
            jnp.stack(new_c_p), jnp.stack(new_c_s))
```

```python
import functools

import jax
import jax.numpy as jnp
from jax import lax
from jax.experimental import pallas as pl
from jax.experimental.pallas import tpu as pltpu
from jax.experimental.pallas import tpu_sc as plsc

F32 = jnp.float32
BF16 = jnp.bfloat16

D = 1024
BATCH = 8
SEQ = 2048
DEPTH = 4
DEC_BATCH = 128
PAST_LEN = 16384
CHUNK = 128
A_HEADS = 8
POOL_WINDOWS = (2, 4, 8, 16)
B_GROUP_DIM = D // len(POOL_WINDOWS)
POOL_BUF = max(POOL_WINDOWS) - 1
CONV_WIDTH = 31
CONV_BUF = CONV_WIDTH - 1
N_GROUPS = 4
EXPERTS_PER_GROUP = 8
N_EXPERTS = N_GROUPS * EXPERTS_PER_GROUP
D_EXPERT = D // 2
EPS = 1e-6

LANES = 128
SUBLANES = 8
CONV_ROWS = 128
SC_CORES = 2
SC_SUBCORES = 16
TM = 512
TILES_PER_SEQ = SEQ // TM
T_PROMPT = BATCH * SEQ
T_ALL = T_PROMPT + DEC_BATCH
N_TILES = -(-T_ALL // TM)
T_PAD = N_TILES * TM
N_PROMPT_TILES = T_PROMPT // TM
SAMPLE_ROWS = 64
BM = 512
N_BLOCKS = -(-(2 * T_PAD) // BM) + N_EXPERTS
P_ROWS = N_BLOCKS * BM
META_ROWS = 8
POOL_CARRY = 16
CONV_CARRY = 32
EXPERT_LANE0 = N_GROUPS
ROUTER_ROWS = 48
VMEM_LIMIT = 56 * 1024 * 1024


def _cparams(*sem):
    return pltpu.CompilerParams(dimension_semantics=sem, vmem_limit_bytes=VMEM_LIMIT)


def _rms(x, g):
    return x * lax.rsqrt(jnp.mean(x * x, axis=-1, keepdims=True) + EPS) * g


def _ln(x, g, b):
    mu = jnp.mean(x, axis=-1, keepdims=True)
    xc = x - mu
    var = jnp.mean(xc * xc, axis=-1, keepdims=True)
    return xc * lax.rsqrt(var + EPS) * g + b


def _dot(a, b):
    return jnp.dot(a, b, preferred_element_type=F32)


def _pack_bf16_pairs(v):
    w = v.shape[1] // 2
    lo = lax.bitcast_convert_type(v[:, :w].astype(BF16).astype(F32), jnp.uint32)
    hi = lax.bitcast_convert_type(v[:, w:].astype(BF16).astype(F32), jnp.uint32)
    return lax.bitcast_convert_type(lax.shift_right_logical(lo, jnp.uint32(16)) | hi, jnp.int32)


def _unpack_bf16_pairs(words, dtype):
    u = lax.bitcast_convert_type(words, jnp.uint32)
    lo = lax.bitcast_convert_type(lax.shift_left(u, jnp.uint32(16)), F32)
    hi = lax.bitcast_convert_type(u & jnp.uint32(0xFFFF0000), F32)
    return jnp.concatenate([lo.astype(dtype), hi.astype(dtype)], axis=1)


def _ada_kernel(c_ref, w_ref, b_ref, op_ref, os_ref):
    c = c_ref[...]
    s = (c * jax.nn.sigmoid(c)).astype(BF16)
    mod = _dot(s, w_ref[...].astype(BF16)) + b_ref[...]
    op_ref[...] = mod[0:BATCH, :]
    os_ref[...] = mod[BATCH:, :]


def _ada_mods(c_all, ada_w, ada_b):
    n = c_all.shape[0]
    assert n == BATCH + DEC_BATCH and BATCH % SUBLANES == 0
    tn = 2048
    return pl.pallas_call(
        _ada_kernel,
        grid=(DEPTH, 6 * D // tn),
        in_specs=[
            pl.BlockSpec((n, D), lambda l, j: (0, 0)),
            pl.BlockSpec((None, D, tn), lambda l, j: (l, 0, j)),
            pl.BlockSpec((None, 1, tn), lambda l, j: (l, 0, j)),
        ],
        out_specs=[
            pl.BlockSpec((None, BATCH, tn), lambda l, j: (l, 0, j)),
            pl.BlockSpec((None, DEC_BATCH, tn), lambda l, j: (l, 0, j)),
        ],
        out_shape=[
            jax.ShapeDtypeStruct((DEPTH, BATCH, 6 * D), F32),
            jax.ShapeDtypeStruct((DEPTH, DEC_BATCH, 6 * D), F32),
        ],
        compiler_params=_cparams("parallel", "parallel"),
        name="ada_mods",
    )(c_all, ada_w, ada_b.reshape(DEPTH, 1, 6 * D))


def _mod_specs_prompt(layer, chunks):
    return [
        pl.BlockSpec((None, None, 1, D),
                     lambda i, c=c: (layer, jnp.minimum(i // TILES_PER_SEQ, BATCH - 1), 0, c))
        for c in chunks
    ]


def _mod_specs_sample(layer, chunks, rows):
    return [
        pl.BlockSpec((None, rows, D), lambda i, c=c: (layer, i, c)) for c in chunks
    ]


def _full(shape):
    nd = len(shape)
    return pl.BlockSpec(shape, lambda i: (0,) * nd)


def _layer_block(shape, layer):
    nd = len(shape)
    return pl.BlockSpec((None,) + shape, lambda i: (layer,) + (0,) * nd)


def _plus_moe(x, g0_ref, g1_ref, wts_ref, gate):
    w = wts_ref[...]
    y0 = _unpack_bf16_pairs(g0_ref[...], F32)
    y1 = _unpack_bf16_pairs(g1_ref[...], F32)
    return x + gate * (w[:, 0:1] * y0 + w[:, 1:2] * y1)


def _residual_in(refs, pending):
    if not pending:
        return refs[0][...], refs[1:]
    x_ref, g0_ref, g1_ref, wts_ref, gate_ref = refs[:5]
    return _plus_moe(x_ref[...], g0_ref, g1_ref, wts_ref, gate_ref[...]), refs[5:]


def _pending_prompt(pending, mod_p, layer):
    if pending is None:
        return [], []
    gathered, wts = pending
    specs = [
        pl.BlockSpec((TM, D // 2), lambda i: (i, 0)),
        pl.BlockSpec((TM, D // 2), lambda i: (i + N_TILES, 0)),
        pl.BlockSpec((TM, LANES), lambda i: (i, 0)),
    ] + _mod_specs_prompt(layer - 1, (5,))
    return specs, [gathered, gathered, wts, mod_p]


def _pending_sample(pending, mod_s, layer, rows):
    if pending is None:
        return [], []
    gathered, wts = pending
    first = T_PROMPT // rows
    second = (T_PAD + T_PROMPT) // rows
    specs = [
        pl.BlockSpec((rows, D // 2), lambda i: (first + i, 0)),
        pl.BlockSpec((rows, D // 2), lambda i: (second + i, 0)),
        pl.BlockSpec((rows, LANES), lambda i: (first + i, 0)),
    ] + _mod_specs_sample(layer - 1, (5,), rows)
    return specs, [gathered, gathered, wts, mod_s]


def _mix_a_front(x, sh, sc, ng, win_ref, lng_ref, lnb_ref):
    h = (_rms(x, ng) * (1.0 + sc) + sh).astype(BF16)
    z = jax.nn.gelu(_dot(h, win_ref[...]))
    u = z[:, :D]
    v = _ln(z[:, D:], lng_ref[...], lnb_ref[...])
    return u, v


def _mix_a_prompt_kernel(*refs, pending, first, stack):
    x, refs = _residual_in(refs, pending)
    (sh_ref, sc_ref, gt_ref, ng_ref, win_ref, lng_ref, lnb_ref, ws_ref, bs_ref, wout_ref,
     *cv_prev_ref, x1_ref, cv_ref, mixed_ref) = refs
    i = pl.program_id(0)
    u, v = _mix_a_front(x, sh_ref[...], sc_ref[...], ng_ref[...], win_ref, lng_ref, lnb_ref)
    vb = v.astype(BF16)
    n_chunks = TM // CHUNK
    row = lax.broadcasted_iota(jnp.int32, (CHUNK, CHUNK), 0)
    col = lax.broadcasted_iota(jnp.int32, (CHUNK, CHUNK), 1)
    tril = row >= col
    hd_dim = D // A_HEADS
    for hd in range(A_HEADS):
        ws = jnp.where(tril, ws_ref[hd], 0.0).astype(BF16)
        cols = slice(hd * hd_dim, (hd + 1) * hd_dim)
        vcat = jnp.concatenate([vb[c * CHUNK:(c + 1) * CHUNK, cols] for c in range(n_chunks)], axis=1)
        m = _dot(ws, vcat)
        for c in range(n_chunks):
            mixed_ref[c * CHUNK:(c + 1) * CHUNK, cols] = m[:, c * hd_dim:(c + 1) * hd_dim] + bs_ref[hd]
    y = _dot((u * mixed_ref[...]).astype(BF16), wout_ref[...])
    out = x + gt_ref[...] * y
    if first:
        out = jnp.where(i < N_PROMPT_TILES, out, 0.0)
    x1_ref[...] = out

    @pl.when(i % TILES_PER_SEQ == TILES_PER_SEQ - 1)
    def _():
        if stack:
            cv_ref[0] = cv_prev_ref[0][...]
            cv_ref[1] = v[TM - CHUNK:, :]
        else:
            cv_ref[...] = v[TM - CHUNK:, :]


def _mix_a_sample_kernel(*refs, pending, first):
    if first:
        refs = refs[:1] + refs[2:]
    x, refs = _residual_in(refs, pending)
    (sh_ref, sc_ref, gt_ref, ng_ref, win_ref, lng_ref, lnb_ref, wsd_ref, bsd_ref, wout_ref,
     x1_ref, cv_ref) = refs
    u, v = _mix_a_front(x, sh_ref[...], sc_ref[...], ng_ref[...], win_ref, lng_ref, lnb_ref)
    mixed = wsd_ref[...] * v + bsd_ref[...]
    y = _dot((u * mixed).astype(BF16), wout_ref[...])
    x1_ref[...] = x + gt_ref[...] * y
    cv_ref[...] = v


def _mixer_a(x_in, pending, cv_prev, mod_p, mod_s, ng, layer, la, w_in, ln_g, ln_b, w_s, b_s,
             w_out):
    first = isinstance(x_in, tuple)
    stack = cv_prev is not None
    assert not stack or la == 1
    x_all = x_in[0] if first else x_in
    bs_b = jnp.broadcast_to(b_s[la][:, :, None], (A_HEADS, CHUNK, D // A_HEADS))
    common = [
        _layer_block((D, 2 * D), la), _layer_block((1, D), la), _layer_block((1, D), la),
    ]
    has_pending = pending is not None
    pend_specs, pend_args = _pending_prompt(pending, mod_p, layer)
    last_tile = N_PROMPT_TILES - 1

    def seq_of(i):
        return jnp.minimum(i, last_tile) // TILES_PER_SEQ

    if stack:
        cv_in_specs = [pl.BlockSpec((None, CHUNK, D), lambda i: (seq_of(i), 0, 0))]
        cv_in_args = [cv_prev]
        cv_spec = pl.BlockSpec((2, None, CHUNK, D), lambda i: (0, seq_of(i), 0, 0))
        cv_shape = jax.ShapeDtypeStruct((2, BATCH, CHUNK, D), F32)
    else:
        cv_in_specs, cv_in_args = [], []
        cv_spec = pl.BlockSpec((None, CHUNK, D), lambda i: (seq_of(i), 0, 0))
        cv_shape = jax.ShapeDtypeStruct((BATCH, CHUNK, D), F32)
    x_all, cv_p = pl.pallas_call(
        functools.partial(_mix_a_prompt_kernel, pending=has_pending, first=first, stack=stack),
        grid=(N_TILES if first else N_PROMPT_TILES,),
        in_specs=[pl.BlockSpec((TM, D), lambda i: (jnp.minimum(i, last_tile), 0))] + pend_specs
        + _mod_specs_prompt(layer, (0, 1, 2))
        + [_full((1, D))] + common
        + [_layer_block((A_HEADS, CHUNK, CHUNK), la), _full((A_HEADS, CHUNK, D // A_HEADS)),
           _layer_block((D, D), la)] + cv_in_specs,
        out_specs=[pl.BlockSpec((TM, D), lambda i: (i, 0)), cv_spec],
        out_shape=[jax.ShapeDtypeStruct((T_PAD, D), F32), cv_shape],
        scratch_shapes=[pltpu.VMEM((TM, D), F32)],
        input_output_aliases={} if first else {0: 0},
        compiler_params=_cparams("arbitrary"),
        name="mixer_a_prompt",
    )(x_all, *pend_args, mod_p, mod_p, mod_p, ng, w_in, ln_g, ln_b, w_s, bs_b, w_out,
      *cv_in_args)

    wsd = jnp.repeat(w_s[la, :, 0, 0], D // A_HEADS).reshape(1, D)
    bsd = jnp.repeat(b_s[la, :, 0], D // A_HEADS).reshape(1, D)
    sblk = T_PROMPT // DEC_BATCH
    pend_specs, pend_args = _pending_sample(pending, mod_s, layer, DEC_BATCH)
    if first:
        x_specs = [pl.BlockSpec((DEC_BATCH, D), lambda i: (0, 0)), pl.BlockSpec(memory_space=pl.ANY)]
        x_args = [x_in[1], x_all]
    else:
        x_specs = [pl.BlockSpec((DEC_BATCH, D), lambda i: (sblk, 0))]
        x_args = [x_all]
    x_all, cv_s = pl.pallas_call(
        functools.partial(_mix_a_sample_kernel, pending=has_pending, first=first),
        grid=(1,),
        in_specs=x_specs + pend_specs
        + _mod_specs_sample(layer, (0, 1, 2), DEC_BATCH)
        + [_full((1, D))] + common
        + [_full((1, D)), _full((1, D)), _layer_block((D, D), la)],
        out_specs=[
            pl.BlockSpec((DEC_BATCH, D), lambda i: (sblk, 0)),
            pl.BlockSpec((DEC_BATCH, D), lambda i: (0, 0)),
        ],
        out_shape=[
            jax.ShapeDtypeStruct((T_PAD, D), F32),
            jax.ShapeDtypeStruct((DEC_BATCH, D), F32),
        ],
        input_output_aliases={len(x_args) - 1: 0},
        compiler_params=_cparams("arbitrary"),
        name="mixer_a_sample",
    )(*x_args, *pend_args, mod_s, mod_s, mod_s, ng, w_in, ln_g, ln_b, wsd, bsd, w_out)
    return x_all, cv_p, cv_s.reshape(DEC_BATCH, 1, D)


def _mix_b_tail(pooled_groups, wgrp_ref, bgrp_ref, scale_ref, wout_ref):
    outs = [
        _dot(pg.astype(BF16), wgrp_ref[g]) + bgrp_ref[g]
        for g, pg in enumerate(pooled_groups)
    ]
    mixed = jnp.concatenate(outs, axis=1) * scale_ref[...]
    return _dot(mixed.astype(BF16), wout_ref[...])


def _mix_b_prompt_kernel(*refs, pending):
    x, refs = _residual_in(refs, pending)
    (sh_ref, sc_ref, gt_ref, ng_ref, win_ref, wgrp_ref, bgrp_ref, scale_ref, wout_ref,
     x1_ref, st_ref, full_ref, s2_ref, s4_ref, s8_ref) = refs
    assert POOL_WINDOWS == (2, 4, 8, 16)
    i = pl.program_id(0)
    j = i % TILES_PER_SEQ
    h = (_rms(x, ng_ref[...]) * (1.0 + sc_ref[...]) + sh_ref[...]).astype(BF16)
    p = _dot(h, win_ref[...])
    gd = B_GROUP_DIM
    top = SUBLANES
    cur = top + POOL_CARRY
    rows = TM + POOL_CARRY

    @pl.when(i == 0)
    def _():
        for ref in (full_ref, s2_ref, s4_ref, s8_ref):
            ref[0:top, :] = jnp.zeros((top, ref.shape[1]), F32)

    full_ref[top:cur, :] = jnp.where(j == 0, 0.0, full_ref[top:cur, :])
    full_ref[cur:cur + TM, :] = p
    s2_ref[top:top + rows, :] = full_ref[top:top + rows, :] + full_ref[pl.ds(top - 1, rows), :]
    s4_ref[top:top + rows, :] = s2_ref[top:top + rows, gd:] + s2_ref[pl.ds(top - 2, rows), gd:]
    s8_ref[top:top + rows, :] = s4_ref[top:top + rows, gd:] + s4_ref[pl.ds(top - 4, rows), gd:]
    s16 = s8_ref[cur:cur + TM, gd:] + s8_ref[pl.ds(cur - 8, TM), gd:]
    sums = [s2_ref[cur:cur + TM, 0:gd], s4_ref[cur:cur + TM, 0:gd], s8_ref[cur:cur + TM, 0:gd], s16]
    pos = j * TM + lax.broadcasted_iota(jnp.int32, (TM, 1), 0)
    pooled = []
    for g, w in enumerate(POOL_WINDOWS):
        cnt = jnp.minimum(w, pos + 1).astype(F32)
        pooled.append(sums[g] / cnt - p[:, g * gd:(g + 1) * gd])
    y = _mix_b_tail(pooled, wgrp_ref, bgrp_ref, scale_ref, wout_ref)
    x1_ref[...] = x + gt_ref[...] * y

    @pl.when(j == TILES_PER_SEQ - 1)
    def _():
        st_ref[...] = full_ref[pl.ds(cur + TM - POOL_BUF, POOL_BUF), :]

    full_ref[top:cur, :] = full_ref[pl.ds(TM + top, POOL_CARRY), :]


def _mix_b_sample_kernel(*refs, pending):
    x, refs = _residual_in(refs, pending)
    (sh_ref, sc_ref, gt_ref, ng_ref, st_ref, win_ref, wgrp_ref, bgrp_ref, scale_ref, wout_ref,
     x1_ref, nst_ref) = refs
    h = (_rms(x, ng_ref[...]) * (1.0 + sc_ref[...]) + sh_ref[...]).astype(BF16)
    p = _dot(h, win_ref[...])
    pooled = []
    for g, w in enumerate(POOL_WINDOWS):
        s = p[:, g * B_GROUP_DIM:(g + 1) * B_GROUP_DIM]
        for k in range(1, w):
            s = s + st_ref[POOL_BUF - k, :, g * B_GROUP_DIM:(g + 1) * B_GROUP_DIM]
        cnt = float(min(w, PAST_LEN + 1))
        pooled.append(s / cnt - p[:, g * B_GROUP_DIM:(g + 1) * B_GROUP_DIM])
    y = _mix_b_tail(pooled, wgrp_ref, bgrp_ref, scale_ref, wout_ref)
    x1_ref[...] = x + gt_ref[...] * y
    for k in range(1, POOL_BUF):
        nst_ref[k - 1] = st_ref[k]
    nst_ref[POOL_BUF - 1] = p


def _mixer_b(x_all, pending, mod_p, mod_s, ng, layer, lb, state, w_in, w_grp, b_grp, scale,
             w_out):
    gd = B_GROUP_DIM
    common = [
        _layer_block((D, D), lb), _layer_block((len(POOL_WINDOWS), gd, gd), lb),
        _layer_block((len(POOL_WINDOWS), 1, gd), lb), _layer_block((1, D), lb),
        _layer_block((D, D), lb),
    ]
    has_pending = pending is not None
    pend_specs, pend_args = _pending_prompt(pending, mod_p, layer)
    x_all, st_p = pl.pallas_call(
        functools.partial(_mix_b_prompt_kernel, pending=has_pending),
        grid=(N_PROMPT_TILES,),
        in_specs=[pl.BlockSpec((TM, D), lambda i: (i, 0))] + pend_specs
        + _mod_specs_prompt(layer, (0, 1, 2)) + [_full((1, D))] + common,
        out_specs=[
            pl.BlockSpec((TM, D), lambda i: (i, 0)),
            pl.BlockSpec((None, POOL_BUF, D), lambda i: (i // TILES_PER_SEQ, 0, 0)),
        ],
        out_shape=[
            jax.ShapeDtypeStruct((T_PAD, D), F32),
            jax.ShapeDtypeStruct((BATCH, POOL_BUF, D), F32),
        ],
        scratch_shapes=[
            pltpu.VMEM((SUBLANES + POOL_CARRY + TM, D - g * gd), F32) for g in (0, 0, 1, 2)],
        input_output_aliases={0: 0},
        compiler_params=_cparams("arbitrary"),
        name="mixer_b_prompt",
    )(x_all, *pend_args, mod_p, mod_p, mod_p, ng, w_in, w_grp, b_grp, scale, w_out)

    r = SAMPLE_ROWS
    sblk = T_PROMPT // r
    pend_specs, pend_args = _pending_sample(pending, mod_s, layer, r)
    x_all, st_s = pl.pallas_call(
        functools.partial(_mix_b_sample_kernel, pending=has_pending),
        grid=(DEC_BATCH // r,),
        in_specs=[pl.BlockSpec((r, D), lambda i: (sblk + i, 0))] + pend_specs
        + _mod_specs_sample(layer, (0, 1, 2), r) + [_full((1, D))]
        + [pl.BlockSpec((None, POOL_BUF, r, D), lambda i: (lb, 0, i, 0))] + common,
        out_specs=[
            pl.BlockSpec((r, D), lambda i: (sblk + i, 0)),
            pl.BlockSpec((POOL_BUF, r, D), lambda i: (0, i, 0)),
        ],
        out_shape=[
            jax.ShapeDtypeStruct((T_PAD, D), F32),
            jax.ShapeDtypeStruct((POOL_BUF, DEC_BATCH, D), F32),
        ],
        input_output_aliases={0: 0},
        compiler_params=_cparams("arbitrary"),
        name="mixer_b_sample",
    )(x_all, *pend_args, mod_s, mod_s, mod_s, ng, state, w_in, w_grp, b_grp, scale, w_out)
    return x_all, st_p, jnp.transpose(st_s, (1, 0, 2))


def _mix_c_glu(x, sh, sc, ng, win_ref, bin_ref):
    h = (_rms(x, ng) * (1.0 + sc) + sh).astype(BF16)
    ag = _dot(h, win_ref[...]) + bin_ref[...]
    return ag[:, :D] * jax.nn.sigmoid(ag[:, D:])


def _mix_c_tail(conv, lng_ref, lnb_ref, wout_ref, bout_ref):
    z = _ln(conv, lng_ref[...], lnb_ref[...])
    z = z * jax.nn.sigmoid(z)
    return _dot(z.astype(BF16), wout_ref[...]) + bout_ref[...]


def _mix_c_prompt_kernel(*refs, pending):
    x, refs = _residual_in(refs, pending)
    (sh_ref, sc_ref, gt_ref, ng_ref, win_ref, bin_ref, wdw_ref, bdw_ref, lng_ref, lnb_ref,
     wout_ref, bout_ref, x1_ref, st_ref, full_ref, conv_ref, *shift_refs) = refs
    i = pl.program_id(0)
    j = i % TILES_PER_SEQ
    glu = _mix_c_glu(x, sh_ref[...], sc_ref[...], ng_ref[...], win_ref, bin_ref)

    @pl.when(j == 0)
    def _():
        full_ref[0:CONV_CARRY, :] = jnp.zeros((CONV_CARRY, D), F32)

    full_ref[CONV_CARRY:CONV_CARRY + TM, :] = glu
    off = CONV_CARRY - CONV_BUF
    span = TM + (off + CONV_BUF) // SUBLANES * SUBLANES - SUBLANES
    for c in range(D // LANES):
        cols = slice(c * LANES, (c + 1) * LANES)
        shift_ref = shift_refs[c % len(shift_refs)]
        for r in range(1, SUBLANES):
            shift_ref[r - 1] = full_ref[pl.ds(r, span), cols]
        for rb in range(TM // CONV_ROWS):
            acc = None
            for k in range(CONV_WIDTH):
                q, r = divmod(off + k, SUBLANES)
                start = q * SUBLANES + rb * CONV_ROWS
                if r == 0:
                    src = full_ref[pl.ds(start, CONV_ROWS), cols]
                else:
                    src = shift_ref[r - 1, pl.ds(start, CONV_ROWS), :]
                term = src * wdw_ref[k:k + 1, cols]
                acc = term if acc is None else acc + term
            conv_ref[pl.ds(rb * CONV_ROWS, CONV_ROWS), cols] = acc + bdw_ref[:, cols]
    y = _mix_c_tail(conv_ref[...], lng_ref, lnb_ref, wout_ref, bout_ref)
    x1_ref[...] = x + gt_ref[...] * y

    @pl.when(j == TILES_PER_SEQ - 1)
    def _():
        st_ref[...] = full_ref[pl.ds(CONV_CARRY + TM - CONV_BUF, CONV_BUF), :]

    full_ref[0:CONV_CARRY, :] = full_ref[pl.ds(TM, CONV_CARRY), :]


def _mix_c_sample_kernel(*refs, pending):
    x, refs = _residual_in(refs, pending)
    (sh_ref, sc_ref, gt_ref, ng_ref, st_ref, win_ref, bin_ref, wdw_ref, bdw_ref, lng_ref,
     lnb_ref, wout_ref, bout_ref, x1_ref, nst_ref) = refs
    glu = _mix_c_glu(x, sh_ref[...], sc_ref[...], ng_ref[...], win_ref, bin_ref)
    acc = glu * wdw_ref[CONV_BUF:CONV_BUF + 1, :]
    for k in range(CONV_BUF):
        acc = acc + st_ref[k] * wdw_ref[k:k + 1, :]
    y = _mix_c_tail(acc + bdw_ref[...], lng_ref, lnb_ref, wout_ref, bout_ref)
    x1_ref[...] = x + gt_ref[...] * y
    for k in range(1, CONV_BUF):
        nst_ref[k - 1] = st_ref[k]
    nst_ref[CONV_BUF - 1] = glu


def _mixer_c(x_all, pending, mod_p, mod_s, ng, layer, lc, state, w_in, b_in, w_dw, b_dw, ln_g,
             ln_b, w_out, b_out):
    common = [
        _layer_block((D, 2 * D), lc), _layer_block((1, 2 * D), lc),
        _layer_block((CONV_WIDTH, D), lc), _layer_block((1, D), lc), _layer_block((1, D), lc),
        _layer_block((1, D), lc), _layer_block((D, D), lc), _layer_block((1, D), lc),
    ]
    has_pending = pending is not None
    pend_specs, pend_args = _pending_prompt(pending, mod_p, layer)
    x_all, st_p = pl.pallas_call(
        functools.partial(_mix_c_prompt_kernel, pending=has_pending),
        grid=(N_PROMPT_TILES,),
        in_specs=[pl.BlockSpec((TM, D), lambda i: (i, 0))] + pend_specs
        + _mod_specs_prompt(layer, (0, 1, 2)) + [_full((1, D))] + common,
        out_specs=[
            pl.BlockSpec((TM, D), lambda i: (i, 0)),
            pl.BlockSpec((None, CONV_BUF, D), lambda i: (i // TILES_PER_SEQ, 0, 0)),
        ],
        out_shape=[
            jax.ShapeDtypeStruct((T_PAD, D), F32),
            jax.ShapeDtypeStruct((BATCH, CONV_BUF, D), F32),
        ],
        scratch_shapes=[pltpu.VMEM((TM + CONV_CARRY, D), F32), pltpu.VMEM((TM, D), F32)]
        + [pltpu.VMEM((SUBLANES - 1, TM + CONV_CARRY - SUBLANES, LANES), F32)] * 2,
        input_output_aliases={0: 0},
        compiler_params=_cparams("arbitrary"),
        name="mixer_c_prompt",
    )(x_all, *pend_args, mod_p, mod_p, mod_p, ng, w_in, b_in, w_dw, b_dw, ln_g, ln_b, w_out,
      b_out)

    r = SAMPLE_ROWS
    sblk = T_PROMPT // r
    pend_specs, pend_args = _pending_sample(pending, mod_s, layer, r)
    x_all, st_s = pl.pallas_call(
        functools.partial(_mix_c_sample_kernel, pending=has_pending),
        grid=(DEC_BATCH // r,),
        in_specs=[pl.BlockSpec((r, D), lambda i: (sblk + i, 0))] + pend_specs
        + _mod_specs_sample(layer, (0, 1, 2), r) + [_full((1, D))]
        + [pl.BlockSpec((None, CONV_BUF, r, D), lambda i: (lc, 0, i, 0))] + common,
        out_specs=[
            pl.BlockSpec((r, D), lambda i: (sblk + i, 0)),
            pl.BlockSpec((CONV_BUF, r, D), lambda i: (0, i, 0)),
        ],
        out_shape=[
            jax.ShapeDtypeStruct((T_PAD, D), F32),
            jax.ShapeDtypeStruct((CONV_BUF, DEC_BATCH, D), F32),
        ],
        input_output_aliases={0: 0},
        compiler_params=_cparams("arbitrary"),
        name="mixer_c_sample",
    )(x_all, *pend_args, mod_s, mod_s, mod_s, ng, state, w_in, b_in, w_dw, b_dw, ln_g, ln_b,
      w_out, b_out)
    return x_all, st_p, jnp.transpose(st_s, (1, 0, 2))


def _tile_mod(i, p_ref, s_ref):
    s_rows = jnp.concatenate([s_ref[...], jnp.zeros((TM - DEC_BATCH, D), F32)], axis=0)
    return jnp.where(i >= N_PROMPT_TILES, s_rows, p_ref[...])


def _unified_mod_specs(layer, chunks):
    specs = []
    for c in chunks:
        specs.append(pl.BlockSpec(
            (None, None, 1, D),
            lambda i, c=c: (layer, jnp.minimum(i // TILES_PER_SEQ, BATCH - 1), 0, c)))
        specs.append(pl.BlockSpec((None, DEC_BATCH, D), lambda i, c=c: (layer, 0, c)))
    return specs


def _router_kernel(x_ref, shp_ref, shs_ref, scp_ref, scs_ref, ng_ref, wr_ref, br_ref,
                   h2_ref, meta_ref, wts_ref, cnt_ref, carry_ref, before_ref):
    i = pl.program_id(0)

    @pl.when(i == 0)
    def _():
        carry_ref[...] = jnp.zeros((ROUTER_ROWS, 1), F32)
        earlier = lax.broadcasted_iota(jnp.int32, (TM, TM), 0)
        token = lax.broadcasted_iota(jnp.int32, (TM, TM), 1)
        before_ref[...] = (earlier < token).astype(BF16)

    sh = _tile_mod(i, shp_ref, shs_ref)
    sc = _tile_mod(i, scp_ref, scs_ref)
    h2 = _rms(x_ref[...], ng_ref[...]) * (1.0 + sc) + sh
    h2_ref[...] = _pack_bf16_pairs(h2)
    lg = lax.dot_general(wr_ref[...], h2.astype(BF16), (((1,), (1,)), ((), ())),
                         preferred_element_type=F32) + br_ref[...]
    row = lax.broadcasted_iota(jnp.int32, (ROUTER_ROWS, TM), 0)
    row_f = row.astype(F32)
    far = float(ROUTER_ROWS)
    neg = -jnp.inf
    is_g = row < N_GROUPS
    gm = jnp.where(is_g, lg, neg)
    gmax = jnp.max(gm, axis=0, keepdims=True)
    gsel = jnp.min(jnp.where(gm == gmax, row_f, far), axis=0, keepdims=True).astype(jnp.int32)
    gsum = jnp.sum(jnp.where(is_g, jnp.exp(lg - gmax), 0.0), axis=0, keepdims=True)
    g_w = 1.0 / gsum
    e_row = row - EXPERT_LANE0
    in_grp = (e_row >= 0) & (e_row < N_EXPERTS) & (
        lax.shift_right_arithmetic(e_row, EXPERTS_PER_GROUP.bit_length() - 1) == gsel)
    em = jnp.where(in_grp, lg, neg)
    m1 = jnp.max(em, axis=0, keepdims=True)
    i1 = jnp.min(jnp.where(em == m1, row_f, far), axis=0, keepdims=True).astype(jnp.int32)
    em2 = jnp.where(row == i1, neg, em)
    m2 = jnp.max(em2, axis=0, keepdims=True)
    i2 = jnp.min(jnp.where(em2 == m2, row_f, far), axis=0, keepdims=True).astype(jnp.int32)
    e2 = jnp.exp(m2 - m1)
    den = 1.0 + e2
    w1 = (1.0 / den) * g_w
    w2 = (e2 / den) * g_w

    hit1 = row == i1
    hit2 = row == i2
    assign = jnp.where(hit1 | hit2, 1.0, 0.0).astype(BF16)
    seen = _dot(assign, before_ref[...]) + carry_ref[...]
    rank1 = jnp.sum(jnp.where(hit1, seen, 0.0), axis=0, keepdims=True)
    rank2 = jnp.sum(jnp.where(hit2, seen, 0.0), axis=0, keepdims=True)
    carry_ref[...] = carry_ref[...] + jnp.sum(assign.astype(F32), axis=1, keepdims=True)
    cnt_ref[...] = carry_ref[...].astype(jnp.int32)

    ex1 = (i1 - EXPERT_LANE0).astype(F32)
    ex2 = (i2 - EXPERT_LANE0).astype(F32)
    field = lax.broadcasted_iota(jnp.int32, (META_ROWS, TM), 0)
    meta_ref[...] = jnp.where(field == 0, ex1, jnp.where(field == 1, ex2,
                              jnp.where(field == 2, rank1, jnp.where(field == 3, rank2, 0.0))))
    slot = lax.broadcasted_iota(jnp.int32, (LANES, TM), 0)
    wts_ref[...] = jnp.where(slot == 0, w1, jnp.where(slot == 1, w2, 0.0)).T


def _router(x_all, mod_p, mod_s, ng, layer, w_r, b_r):
    return pl.pallas_call(
        _router_kernel,
        grid=(N_TILES,),
        in_specs=[pl.BlockSpec((TM, D), lambda i: (i, 0))]
        + _unified_mod_specs(layer, (3, 4))
        + [_full((1, D)), _full((ROUTER_ROWS, D)), _full((ROUTER_ROWS, 1))],
        out_specs=[
            pl.BlockSpec((TM, D // 2), lambda i: (i, 0)),
            pl.BlockSpec((META_ROWS, TM), lambda i: (0, i)),
            pl.BlockSpec((TM, LANES), lambda i: (i, 0)),
            pl.BlockSpec((ROUTER_ROWS, 1), lambda i: (0, 0)),
        ],
        out_shape=[
            jax.ShapeDtypeStruct((T_PAD, D // 2), jnp.int32),
            jax.ShapeDtypeStruct((META_ROWS, T_PAD), F32),
            jax.ShapeDtypeStruct((T_PAD, LANES), F32),
            jax.ShapeDtypeStruct((ROUTER_ROWS, 1), jnp.int32),
        ],
        scratch_shapes=[pltpu.VMEM((ROUTER_ROWS, 1), F32), pltpu.VMEM((TM, TM), BF16)],
        compiler_params=_cparams("arbitrary"),
        name="router",
    )(x_all, mod_p, mod_s, mod_p, mod_s, ng, w_r, b_r)


SC_INDEX_MAX = 128
SC_ROWS_BYTES = 400 * 1024


def _gather_chunk(per_worker, row_bytes, buffers=1):
    top = min(SC_INDEX_MAX, SC_ROWS_BYTES // (row_bytes * buffers)) // SUBLANES * SUBLANES
    for ch in range(top, SUBLANES - 1, -SUBLANES):
        if per_worker % ch == 0:
            return ch
    raise ValueError(per_worker)


def _sc_gather(table, idx):
    nc, nw = SC_CORES, SC_CORES * SC_SUBCORES
    m = idx.shape[0]
    per_w = m // nw
    assert per_w * nw == m
    width = table.shape[1]
    ch = _gather_chunk(per_w, width * table.dtype.itemsize, buffers=2)
    mesh = plsc.VectorSubcoreMesh(core_axis_name="c", subcore_axis_name="s")

    n_chunks = per_w // ch

    @functools.partial(
        pl.kernel,
        out_type=jax.ShapeDtypeStruct((m, width), table.dtype),
        mesh=mesh,
        scratch_types=[pltpu.VMEM((ch,), jnp.int32)] * 2
        + [pltpu.VMEM((ch, width), table.dtype)] * 2
        + [pltpu.SemaphoreType.DMA] * 4,
    )
    def gather_kernel(t_hbm, i_hbm, o_hbm, idx_a, idx_b, rows_a, rows_b, sg_a, sg_b, sw_a, sw_b):
        wid = lax.axis_index("s") * nc + lax.axis_index("c")
        base = wid * per_w
        idx_v, rows_v, sem_g, sem_w = (idx_a, idx_b), (rows_a, rows_b), (sg_a, sg_b), (sw_a, sw_b)

        def out_rows(j):
            return o_hbm.at[pl.ds(pl.multiple_of(base + j * ch, 8), ch)]

        gathers, writes = {}, {}
        for j in range(n_chunks):
            b = j % 2
            if j >= 2:
                writes[j - 2].wait()
            pltpu.sync_copy(i_hbm.at[pl.ds(pl.multiple_of(base + j * ch, 8), ch)], idx_v[b])
            gathers[j] = pltpu.async_copy(t_hbm.at[idx_v[b]], rows_v[b], sem_g[b])
            if j >= 1:
                gathers[j - 1].wait()
                writes[j - 1] = pltpu.async_copy(rows_v[1 - b], out_rows(j - 1), sem_w[1 - b])
        last = n_chunks - 1
        gathers[last].wait()
        writes[last] = pltpu.async_copy(rows_v[last % 2], out_rows(last), sem_w[last % 2])
        for j in range(max(last - 1, 0), n_chunks):
            writes[j].wait()

    return gather_kernel(table, idx)


def _sc_scatter2(rows, dest, n_out):
    nc, nw = SC_CORES, SC_CORES * SC_SUBCORES
    n, width = rows.shape
    assert dest.shape == (2 * n,) and n % SUBLANES == 0
    per_w = n // nw
    assert per_w * nw == n
    ch = _gather_chunk(per_w, width * rows.dtype.itemsize)
    mesh = plsc.VectorSubcoreMesh(core_axis_name="c", subcore_axis_name="s")

    @functools.partial(
        pl.kernel,
        out_type=jax.ShapeDtypeStruct((n_out, width), rows.dtype),
        mesh=mesh,
        scratch_types=[
            pltpu.VMEM((ch,), jnp.int32),
            pltpu.VMEM((ch,), jnp.int32),
            pltpu.VMEM((ch, width), rows.dtype),
            pltpu.SemaphoreType.DMA,
            pltpu.SemaphoreType.DMA,
        ],
    )
    def scatter_kernel(r_hbm, d_hbm, o_hbm, i0_v, i1_v, rows_v, sem0, sem1):
        wid = lax.axis_index("s") * nc + lax.axis_index("c")
        base = wid * per_w

        @pl.loop(0, per_w // ch)
        def _(j):
            off = pl.multiple_of(base + j * ch, 8)
            pltpu.sync_copy(d_hbm.at[pl.ds(off, ch)], i0_v)
            pltpu.sync_copy(d_hbm.at[pl.ds(pl.multiple_of(n + off, 8), ch)], i1_v)
            pltpu.sync_copy(r_hbm.at[pl.ds(off, ch)], rows_v)
            c0 = pltpu.async_copy(rows_v, o_hbm.at[i0_v], sem0)
            c1 = pltpu.async_copy(rows_v, o_hbm.at[i1_v], sem1)
            c0.wait()
            c1.wait()

    return scatter_kernel(rows, dest)


def _expert_kernel(be_ref, first_ref, nvalid_ref, next_ref, nused_ref, x_ref, wg_hbm, wu_hbm,
                   wd_hbm, y_ref, wgf_ref, wuf_ref, wdf_ref, wgb_ref, wub_ref, wdb_ref, sems,
                   *, layer):
    b = pl.program_id(0)

    def weight_copies(e):
        return (
            pltpu.make_async_copy(wg_hbm.at[layer, e], wgf_ref, sems.at[0]),
            pltpu.make_async_copy(wu_hbm.at[layer, e], wuf_ref, sems.at[1]),
            pltpu.make_async_copy(wd_hbm.at[layer, e], wdf_ref, sems.at[2]),
        )

    @pl.when(b == 0)
    def _():
        for cp in weight_copies(be_ref[0]):
            cp.start()

    @pl.when(b < nused_ref[0])
    def _():
        @pl.when(first_ref[b] == 1)
        def _():
            for cp in weight_copies(be_ref[b]):
                cp.wait()
            wgb_ref[...] = wgf_ref[...].astype(BF16)
            wub_ref[...] = wuf_ref[...].astype(BF16)
            wdb_ref[...] = wdf_ref[...].astype(BF16)

            @pl.when(next_ref[b] >= 0)
            def _():
                for cp in weight_copies(next_ref[b]):
                    cp.start()

        def run(rows):
            live = lax.broadcasted_iota(jnp.int32, (rows, 1), 0) < nvalid_ref[b]
            x = _unpack_bf16_pairs(jnp.where(live, x_ref[0:rows, :], 0), BF16)
            g = _dot(x, wgb_ref[...])
            u = _dot(x, wub_ref[...])
            hmid = (g * jax.nn.sigmoid(g)) * u
            y_ref[0:rows, :] = _pack_bf16_pairs(_dot(hmid.astype(BF16), wdb_ref[...]))

        half = BM // 2

        @pl.when(nvalid_ref[b] > half)
        def _():
            run(BM)

        @pl.when(nvalid_ref[b] <= half)
        def _():
            run(half)
            y_ref[half:BM, :] = jnp.zeros((BM - half, D // 2), jnp.int32)

    @pl.when(b >= nused_ref[0])
    def _():
        y_ref[...] = jnp.zeros((BM, D // 2), jnp.int32)


def _experts(x_sorted, block_expert, first, nvalid, next_expert, nused, layer, w_gate, w_up,
             w_down):
    def row_map(b, be, fi, nv, nx, nu):
        return (jnp.minimum(b, nu[0] - 1), 0)

    def out_map(b, be, fi, nv, nx, nu):
        return (b, 0)

    grid_spec = pltpu.PrefetchScalarGridSpec(
        num_scalar_prefetch=5,
        grid=(N_BLOCKS,),
        in_specs=[
            pl.BlockSpec((BM, D // 2), row_map),
            pl.BlockSpec(memory_space=pl.ANY),
            pl.BlockSpec(memory_space=pl.ANY),
            pl.BlockSpec(memory_space=pl.ANY),
        ],
        out_specs=pl.BlockSpec((BM, D // 2), out_map),
        scratch_shapes=[
            pltpu.VMEM((D, D_EXPERT), F32),
            pltpu.VMEM((D, D_EXPERT), F32),
            pltpu.VMEM((D_EXPERT, D), F32),
            pltpu.VMEM((D, D_EXPERT), BF16),
            pltpu.VMEM((D, D_EXPERT), BF16),
            pltpu.VMEM((D_EXPERT, D), BF16),
            pltpu.SemaphoreType.DMA((3,)),
        ],
    )
    return pl.pallas_call(
        functools.partial(_expert_kernel, layer=layer),
        grid_spec=grid_spec,
        out_shape=jax.ShapeDtypeStruct((P_ROWS, D // 2), jnp.int32),
        compiler_params=_cparams("arbitrary"),
        name="experts",
    )(block_expert, first, nvalid, next_expert, nused, x_sorted, w_gate, w_up, w_down)


def _combine_final_kernel(x_ref, g0_ref, g1_ref, wts_ref, gtp_ref, gts_ref, fg_ref, yp_ref,
                          ys_ref):
    i = pl.program_id(0)
    gate = _tile_mod(i, gtp_ref, gts_ref)
    y = _rms(_plus_moe(x_ref[...], g0_ref, g1_ref, wts_ref, gate), fg_ref[...])

    @pl.when(i < N_PROMPT_TILES)
    def _():
        yp_ref[...] = y

    @pl.when(i == N_PROMPT_TILES)
    def _():
        ys_ref[...] = y[0:DEC_BATCH, :]


def _combine_final(x_all, gathered, wts, mod_p, mod_s, layer, final_g):
    in_specs = [
        pl.BlockSpec((TM, D), lambda i: (i, 0)),
        pl.BlockSpec((TM, D // 2), lambda i: (i, 0)),
        pl.BlockSpec((TM, D // 2), lambda i: (i + N_TILES, 0)),
        pl.BlockSpec((TM, LANES), lambda i: (i, 0)),
    ] + _unified_mod_specs(layer, (5,))
    return pl.pallas_call(
        _combine_final_kernel,
        grid=(N_PROMPT_TILES + 1,),
        in_specs=in_specs + [_full((1, D))],
        out_specs=[
            pl.BlockSpec((TM, D), lambda i: (jnp.minimum(i, N_PROMPT_TILES - 1), 0)),
            pl.BlockSpec((DEC_BATCH, D), lambda i: (0, 0)),
        ],
        out_shape=[
            jax.ShapeDtypeStruct((T_PROMPT, D), F32),
            jax.ShapeDtypeStruct((DEC_BATCH, D), F32),
        ],
        compiler_params=_cparams("arbitrary"),
        name="combine_final",
    )(x_all, gathered, gathered, wts, mod_p, mod_s, final_g)


def _plan_kernel(cnt_ref, meta_ref, d_ref, be_ref, first_ref, nvalid_ref, next_ref,
                 nused_ref, pstart_ref, nxt_ref):
    shift = BM.bit_length() - 1
    assert 1 << shift == BM

    def count(e):
        return cnt_ref[EXPERT_LANE0 + e, 0]

    def scan_back(k, nxt):
        e = N_EXPERTS - 1 - k
        nxt_ref[e] = nxt
        return jnp.where(count(e) > 0, e, nxt)

    lax.fori_loop(0, N_EXPERTS, scan_back, jnp.int32(-1))

    def fill(e, blk):
        c = count(e)
        pstart_ref[e] = blk << shift

        def one(j, carry):
            be_ref[blk + j] = e
            first_ref[blk + j] = jnp.where(j == 0, 1, 0)
            nvalid_ref[blk + j] = jnp.minimum(c - (j << shift), BM)
            next_ref[blk + j] = nxt_ref[e]
            return carry

        n_blk = (c + (BM - 1)) >> shift
        lax.fori_loop(0, n_blk, one, 0)
        return blk + n_blk

    used = lax.fori_loop(0, N_EXPERTS, fill, jnp.int32(0))
    nused_ref[0] = used

    def tail(b, carry):
        be_ref[b] = 0
        first_ref[b] = 0
        nvalid_ref[b] = 0
        next_ref[b] = -1
        return carry

    lax.fori_loop(used, N_BLOCKS, tail, 0)

    m = meta_ref[...]
    expert = m[0:2, :]
    dest = m[2:4, :]
    for e in range(N_EXPERTS):
        dest = dest + jnp.where(expert == float(e), pstart_ref[e].astype(F32), 0.0)
    dest = dest.astype(jnp.int32)
    d_ref[:, 0:T_PAD] = dest[0:1, :]
    d_ref[:, T_PAD:2 * T_PAD] = dest[1:2, :]


def _plan(cnt, meta):
    smem = pl.BlockSpec(memory_space=pltpu.SMEM)
    blocks = jax.ShapeDtypeStruct((N_BLOCKS,), jnp.int32)
    return pl.pallas_call(
        _plan_kernel,
        grid=(1,),
        in_specs=[smem, _full((META_ROWS, T_PAD))],
        out_specs=[_full((1, 2 * T_PAD)), smem, smem, smem, smem, smem],
        out_shape=[
            jax.ShapeDtypeStruct((1, 2 * T_PAD), jnp.int32),
            blocks, blocks, blocks, blocks,
            jax.ShapeDtypeStruct((1,), jnp.int32),
        ],
        scratch_shapes=[pltpu.SMEM((N_EXPERTS,), jnp.int32), pltpu.SMEM((N_EXPERTS,), jnp.int32)],
        compiler_params=_cparams("arbitrary"),
        name="plan",
    )(cnt, meta)


def _moe(x_all, mod_p, mod_s, ng, layer, w_r, b_r, w_gate, w_up, w_down):
    h2, meta, wts, cnt = _router(x_all, mod_p, mod_s, ng, layer, w_r, b_r)
    dest, block_expert, first, nvalid, next_expert, nused = _plan(cnt, meta)
    dest = dest.reshape(2 * T_PAD)
    x_sorted = _sc_scatter2(h2, dest, P_ROWS)
    y_sorted = _experts(x_sorted, block_expert, first, nvalid, next_expert, nused, layer,
                        w_gate, w_up, w_down)
    gathered = _sc_gather(y_sorted, dest)
    return gathered, wts


def kernel(x_prompt, x_sample, c_prompt, c_sample, state_pool, state_conv, ada_w, ada_b, norm_g, final_g, a_w_in, a_ln_g, a_ln_b, a_w_s, a_b_s, a_w_out, b_w_in, b_w_grp, b_b_grp, b_scale, b_w_out, c_w_in, c_b_in, c_w_dw, c_b_dw, c_ln_g, c_ln_b, c_w_out, c_b_out, moe_w_grp, moe_b_grp, moe_w_exp, moe_b_exp, moe_w_gate, moe_w_up, moe_w_down):
    x_all = (x_prompt.reshape(T_PROMPT, D), x_sample.reshape(DEC_BATCH, D))
    mod_p, mod_s = _ada_mods(jnp.concatenate([c_prompt, c_sample], axis=0), ada_w, ada_b)
    mod_p = mod_p.reshape(DEPTH, BATCH, 1, 6 * D)

    a_w_in_b, a_w_out_b = a_w_in.astype(BF16), a_w_out.astype(BF16)
    b_w_in_b, b_w_grp_b, b_w_out_b = b_w_in.astype(BF16), b_w_grp.astype(BF16), b_w_out.astype(BF16)
    c_w_in_b, c_w_out_b = c_w_in.astype(BF16), c_w_out.astype(BF16)
    pad_rows = ROUTER_ROWS - N_GROUPS - N_EXPERTS
    w_r = jnp.pad(jnp.swapaxes(jnp.concatenate([moe_w_grp, moe_w_exp], axis=2), 1, 2),
                  ((0, 0), (0, pad_rows), (0, 0))).astype(BF16)
    b_r = jnp.pad(jnp.concatenate([moe_b_grp, moe_b_exp], axis=1), ((0, 0), (0, pad_rows)))

    new_a_p, new_a_s, new_b_p, new_b_s, new_c_p, new_c_s = [], [], [], [], [], []
    pool_tm = jnp.transpose(state_pool, (0, 2, 1, 3))
    conv_tm = jnp.transpose(state_conv, (0, 2, 1, 3))
    ia = ib = ic = 0
    assert len(range(0, DEPTH, 3)) == 2
    pending = None
    for layer in range(DEPTH):
        ng1 = norm_g[layer, 0].reshape(1, D)
        ng2 = norm_g[layer, 1].reshape(1, D)
        kind = layer % 3
        if kind == 0:
            x_all, st_p, st_s = _mixer_a(
                x_all, pending, new_a_p[0] if new_a_p else None, mod_p, mod_s, ng1, layer, ia,
                a_w_in_b, a_ln_g.reshape(-1, 1, D), a_ln_b.reshape(-1, 1, D), a_w_s, a_b_s,
                a_w_out_b)
            new_a_p = [st_p]
            new_a_s.append(st_s)
            ia += 1
        elif kind == 1:
            x_all, st_p, st_s = _mixer_b(
                x_all, pending, mod_p, mod_s, ng1, layer, ib, pool_tm, b_w_in_b, b_w_grp_b,
                b_b_grp.reshape(-1, len(POOL_WINDOWS), 1, B_GROUP_DIM),
                b_scale.reshape(-1, 1, D), b_w_out_b)
            new_b_p.append(st_p)
            new_b_s.append(st_s)
            ib += 1
        else:
            x_all, st_p, st_s = _mixer_c(
                x_all, pending, mod_p, mod_s, ng1, layer, ic, conv_tm, c_w_in_b,
                c_b_in.reshape(-1, 1, 2 * D), c_w_dw, c_b_dw.reshape(-1, 1, D),
                c_ln_g.reshape(-1, 1, D), c_ln_b.reshape(-1, 1, D), c_w_out_b,
                c_b_out.reshape(-1, 1, D))
            new_c_p.append(st_p)
            new_c_s.append(st_s)
            ic += 1
        pending = _moe(x_all, mod_p, mod_s, ng2, layer, w_r[layer],
                       b_r[layer].reshape(ROUTER_ROWS, 1), moe_w_gate, moe_w_up, moe_w_down)

    y_p, y_s = _combine_final(x_all, *pending, mod_p, mod_s, DEPTH - 1, final_g.reshape(1, D))
    return (y_p.reshape(BATCH, SEQ, D), y_s.reshape(DEC_BATCH, 1, D),
            new_a_p[0], jnp.stack(new_a_s), jnp.stack(new_b_p), jnp.stack(new_b_s),
            jnp.stack(new_c_p), jnp.stack(new_c_s))
```

```python
import functools

import jax
import jax.numpy as jnp
from jax import lax
from jax.experimental import pallas as pl
from jax.experimental.pallas import tpu as pltpu
from jax.experimental.pallas import tpu_sc as plsc

F32 = jnp.float32
BF16 = jnp.bfloat16

D = 1024
BATCH = 8
SEQ = 2048
DEPTH = 4
DEC_BATCH = 128
PAST_LEN = 16384
CHUNK = 128
A_HEADS = 8
POOL_WINDOWS = (2, 4, 8, 16)
B_GROUP_DIM = D // len(POOL_WINDOWS)
POOL_BUF = max(POOL_WINDOWS) - 1
CONV_WIDTH = 31
CONV_BUF = CONV_WIDTH - 1
N_GROUPS = 4
EXPERTS_PER_GROUP = 8
N_EXPERTS = N_GROUPS * EXPERTS_PER_GROUP
D_EXPERT = D // 2
EPS = 1e-6

LANES = 128
SUBLANES = 8
CONV_ROWS = 128
SC_CORES = 2
SC_SUBCORES = 16
TM = 512
TILES_PER_SEQ = SEQ // TM
T_PROMPT = BATCH * SEQ
T_ALL = T_PROMPT + DEC_BATCH
N_TILES = -(-T_ALL // TM)
T_PAD = N_TILES * TM
N_PROMPT_TILES = T_PROMPT // TM
SAMPLE_ROWS = 32
BM = 512
N_BLOCKS = -(-(2 * T_PAD) // BM) + N_EXPERTS
P_ROWS = N_BLOCKS * BM
META_ROWS = 8
POOL_CARRY = 16
CONV_CARRY = 32
EXPERT_LANE0 = N_GROUPS
ROUTER_ROWS = 48
VMEM_LIMIT = 56 * 1024 * 1024


def _cparams(*sem):
    return pltpu.CompilerParams(dimension_semantics=sem, vmem_limit_bytes=VMEM_LIMIT)


def _rms(x, g):
    return x * lax.rsqrt(jnp.mean(x * x, axis=-1, keepdims=True) + EPS) * g


def _ln(x, g, b):
    mu = jnp.mean(x, axis=-1, keepdims=True)
    xc = x - mu
    var = jnp.mean(xc * xc, axis=-1, keepdims=True)
    return xc * lax.rsqrt(var + EPS) * g + b


def _dot(a, b):
    return jnp.dot(a, b, preferred_element_type=F32)


def _pack_bf16_pairs(v):
    w = v.shape[1] // 2
    lo = lax.bitcast_convert_type(v[:, :w].astype(BF16).astype(F32), jnp.uint32)
    hi = lax.bitcast_convert_type(v[:, w:].astype(BF16).astype(F32), jnp.uint32)
    return lax.bitcast_convert_type(lax.shift_right_logical(lo, jnp.uint32(16)) | hi, jnp.int32)


def _unpack_bf16_pairs(words, dtype):
    u = lax.bitcast_convert_type(words, jnp.uint32)
    lo = lax.bitcast_convert_type(lax.shift_left(u, jnp.uint32(16)), F32)
    hi = lax.bitcast_convert_type(u & jnp.uint32(0xFFFF0000), F32)
    return jnp.concatenate([lo.astype(dtype), hi.astype(dtype)], axis=1)


ADA_K = 256


def _ada_kernel(c_ref, w_ref, b_ref, op_ref, os_ref):
    k = pl.program_id(1)
    c = c_ref[...]
    s = (c * jax.nn.sigmoid(c)).astype(BF16)
    part = _dot(s, w_ref[...].astype(BF16))

    @pl.when(k == 0)
    def _():
        op_ref[...] = part[0:BATCH, :] + b_ref[...]
        os_ref[...] = part[BATCH:, :] + b_ref[...]

    @pl.when(k > 0)
    def _():
        op_ref[...] += part[0:BATCH, :]
        os_ref[...] += part[BATCH:, :]


def _ada_mods(c_all, ada_w, ada_b):
    n = c_all.shape[0]
    assert n == BATCH + DEC_BATCH and BATCH % SUBLANES == 0
    return pl.pallas_call(
        _ada_kernel,
        grid=(DEPTH, D // ADA_K),
        in_specs=[
            pl.BlockSpec((n, ADA_K), lambda l, k: (0, k)),
            pl.BlockSpec((None, ADA_K, 6 * D), lambda l, k: (l, k, 0)),
            pl.BlockSpec((None, 1, 6 * D), lambda l, k: (l, 0, 0)),
        ],
        out_specs=[
            pl.BlockSpec((None, BATCH, 6 * D), lambda l, k: (l, 0, 0)),
            pl.BlockSpec((None, DEC_BATCH, 6 * D), lambda l, k: (l, 0, 0)),
        ],
        out_shape=[
            jax.ShapeDtypeStruct((DEPTH, BATCH, 6 * D), F32),
            jax.ShapeDtypeStruct((DEPTH, DEC_BATCH, 6 * D), F32),
        ],
        compiler_params=_cparams("parallel", "arbitrary"),
        name="ada_mods",
    )(c_all, ada_w, ada_b.reshape(DEPTH, 1, 6 * D))


def _mod_specs_prompt(layer, chunks):
    return [
        pl.BlockSpec((None, None, 1, D),
                     lambda i, c=c: (layer, jnp.minimum(i // TILES_PER_SEQ, BATCH - 1), 0, c))
        for c in chunks
    ]


def _mod_specs_sample(layer, chunks, rows, row_block0=0):
    return [
        pl.BlockSpec((None, rows, D), lambda i, c=c: (layer, i + row_block0, c)) for c in chunks
    ]


def _full(shape):
    nd = len(shape)
    return pl.BlockSpec(shape, lambda i: (0,) * nd)


def _layer_block(shape, layer):
    nd = len(shape)
    return pl.BlockSpec((None,) + shape, lambda i: (layer,) + (0,) * nd)


def _plus_moe(x, g0_ref, g1_ref, wts_ref, gate):
    w = wts_ref[...]
    y0 = _unpack_bf16_pairs(g0_ref[...], F32)
    y1 = _unpack_bf16_pairs(g1_ref[...], F32)
    return x + gate * (w[:, 0:1] * y0 + w[:, 1:2] * y1)


def _residual_in(refs, pending):
    if not pending:
        return refs[0][...], refs[1:]
    x_ref, g0_ref, g1_ref, wts_ref, gate_ref = refs[:5]
    return _plus_moe(x_ref[...], g0_ref, g1_ref, wts_ref, gate_ref[...]), refs[5:]


def _pending_prompt(pending, mod_p, layer):
    if pending is None:
        return [], []
    gathered, wts = pending
    specs = [
        pl.BlockSpec((TM, D // 2), lambda i: (i, 0)),
        pl.BlockSpec((TM, D // 2), lambda i: (i + N_TILES, 0)),
        pl.BlockSpec((TM, LANES), lambda i: (i, 0)),
    ] + _mod_specs_prompt(layer - 1, (5,))
    return specs, [gathered, gathered, wts, mod_p]


def _pending_sample(pending, mod_s, layer, rows):
    if pending is None:
        return [], []
    gathered, wts = pending
    first = T_PROMPT // rows
    second = (T_PAD + T_PROMPT) // rows
    specs = [
        pl.BlockSpec((rows, D // 2), lambda i: (first + i, 0)),
        pl.BlockSpec((rows, D // 2), lambda i: (second + i, 0)),
        pl.BlockSpec((rows, LANES), lambda i: (first + i, 0)),
    ] + _mod_specs_sample(layer - 1, (5,), rows)
    return specs, [gathered, gathered, wts, mod_s]


def _mix_a_front(x, sh, sc, ng, win_ref, lng_ref, lnb_ref):
    h = (_rms(x, ng) * (1.0 + sc) + sh).astype(BF16)
    z = jax.nn.gelu(_dot(h, win_ref[...]))
    u = z[:, :D]
    v = _ln(z[:, D:], lng_ref[...], lnb_ref[...])
    return u, v


def _mix_a_prompt_kernel(*refs, pending, first, stack):
    x, refs = _residual_in(refs, pending)
    (sh_ref, sc_ref, gt_ref, ng_ref, win_ref, lng_ref, lnb_ref, ws_ref, bs_ref, wout_ref,
     *cv_prev_ref, x1_ref, cv_ref, mixed_ref) = refs
    i = pl.program_id(0)
    u, v = _mix_a_front(x, sh_ref[...], sc_ref[...], ng_ref[...], win_ref, lng_ref, lnb_ref)
    vb = v.astype(BF16)
    n_chunks = TM // CHUNK
    row = lax.broadcasted_iota(jnp.int32, (CHUNK, CHUNK), 0)
    col = lax.broadcasted_iota(jnp.int32, (CHUNK, CHUNK), 1)
    tril = row >= col
    hd_dim = D // A_HEADS
    for hd in range(A_HEADS):
        ws = jnp.where(tril, ws_ref[hd], 0.0).astype(BF16)
        cols = slice(hd * hd_dim, (hd + 1) * hd_dim)
        vcat = jnp.concatenate([vb[c * CHUNK:(c + 1) * CHUNK, cols] for c in range(n_chunks)], axis=1)
        m = _dot(ws, vcat)
        for c in range(n_chunks):
            mixed_ref[c * CHUNK:(c + 1) * CHUNK, cols] = m[:, c * hd_dim:(c + 1) * hd_dim] + bs_ref[hd]
    y = _dot((u * mixed_ref[...]).astype(BF16), wout_ref[...])
    out = x + gt_ref[...] * y
    if first:
        out = jnp.where(i < N_PROMPT_TILES, out, 0.0)
    x1_ref[...] = out

    @pl.when(i % TILES_PER_SEQ == TILES_PER_SEQ - 1)
    def _():
        if stack:
            cv_ref[0] = cv_prev_ref[0][...]
            cv_ref[1] = v[TM - CHUNK:, :]
        else:
            cv_ref[...] = v[TM - CHUNK:, :]


def _mix_a_sample_kernel(*refs, pending, first):
    if first:
        refs = refs[:1] + refs[2:]
    x, refs = _residual_in(refs, pending)
    (sh_ref, sc_ref, gt_ref, ng_ref, win_ref, lng_ref, lnb_ref, wsd_ref, bsd_ref, wout_ref,
     x1_ref, cv_ref) = refs
    u, v = _mix_a_front(x, sh_ref[...], sc_ref[...], ng_ref[...], win_ref, lng_ref, lnb_ref)
    mixed = wsd_ref[...] * v + bsd_ref[...]
    y = _dot((u * mixed).astype(BF16), wout_ref[...])
    x1_ref[...] = x + gt_ref[...] * y
    cv_ref[...] = v


def _mixer_a(x_in, pending, cv_prev, mod_p, mod_s, ng, layer, la, w_in, ln_g, ln_b, w_s, b_s,
             w_out):
    first = isinstance(x_in, tuple)
    stack = cv_prev is not None
    assert not stack or la == 1
    x_all = x_in[0] if first else x_in
    bs_b = jnp.broadcast_to(b_s[la][:, :, None], (A_HEADS, CHUNK, D // A_HEADS))
    common = [
        _layer_block((D, 2 * D), la), _layer_block((1, D), la), _layer_block((1, D), la),
    ]
    has_pending = pending is not None
    pend_specs, pend_args = _pending_prompt(pending, mod_p, layer)
    last_tile = N_PROMPT_TILES - 1

    def seq_of(i):
        return jnp.minimum(i, last_tile) // TILES_PER_SEQ

    if stack:
        cv_in_specs = [pl.BlockSpec((None, CHUNK, D), lambda i: (seq_of(i), 0, 0))]
        cv_in_args = [cv_prev]
        cv_spec = pl.BlockSpec((2, None, CHUNK, D), lambda i: (0, seq_of(i), 0, 0))
        cv_shape = jax.ShapeDtypeStruct((2, BATCH, CHUNK, D), F32)
    else:
        cv_in_specs, cv_in_args = [], []
        cv_spec = pl.BlockSpec((None, CHUNK, D), lambda i: (seq_of(i), 0, 0))
        cv_shape = jax.ShapeDtypeStruct((BATCH, CHUNK, D), F32)
    x_all, cv_p = pl.pallas_call(
        functools.partial(_mix_a_prompt_kernel, pending=has_pending, first=first, stack=stack),
        grid=(N_TILES if first else N_PROMPT_TILES,),
        in_specs=[pl.BlockSpec((TM, D), lambda i: (jnp.minimum(i, last_tile), 0))] + pend_specs
        + _mod_specs_prompt(layer, (0, 1, 2))
        + [_full((1, D))] + common
        + [_layer_block((A_HEADS, CHUNK, CHUNK), la), _full((A_HEADS, CHUNK, D // A_HEADS)),
           _layer_block((D, D), la)] + cv_in_specs,
        out_specs=[pl.BlockSpec((TM, D), lambda i: (i, 0)), cv_spec],
        out_shape=[jax.ShapeDtypeStruct((T_PAD, D), F32), cv_shape],
        scratch_shapes=[pltpu.VMEM((TM, D), F32)],
        input_output_aliases={} if first else {0: 0},
        compiler_params=_cparams("arbitrary"),
        name="mixer_a_prompt",
    )(x_all, *pend_args, mod_p, mod_p, mod_p, ng, w_in, ln_g, ln_b, w_s, bs_b, w_out,
      *cv_in_args)

    wsd = jnp.repeat(w_s[la, :, 0, 0], D // A_HEADS).reshape(1, D)
    bsd = jnp.repeat(b_s[la, :, 0], D // A_HEADS).reshape(1, D)
    sblk = T_PROMPT // DEC_BATCH
    pend_specs, pend_args = _pending_sample(pending, mod_s, layer, DEC_BATCH)
    if first:
        x_specs = [pl.BlockSpec((DEC_BATCH, D), lambda i: (0, 0)), pl.BlockSpec(memory_space=pl.ANY)]
        x_args = [x_in[1], x_all]
    else:
        x_specs = [pl.BlockSpec((DEC_BATCH, D), lambda i: (sblk, 0))]
        x_args = [x_all]
    x_all, cv_s = pl.pallas_call(
        functools.partial(_mix_a_sample_kernel, pending=has_pending, first=first),
        grid=(1,),
        in_specs=x_specs + pend_specs
        + _mod_specs_sample(layer, (0, 1, 2), DEC_BATCH)
        + [_full((1, D))] + common
        + [_full((1, D)), _full((1, D)), _layer_block((D, D), la)],
        out_specs=[
            pl.BlockSpec((DEC_BATCH, D), lambda i: (sblk, 0)),
            pl.BlockSpec((DEC_BATCH, D), lambda i: (0, 0)),
        ],
        out_shape=[
            jax.ShapeDtypeStruct((T_PAD, D), F32),
            jax.ShapeDtypeStruct((DEC_BATCH, D), F32),
        ],
        input_output_aliases={len(x_args) - 1: 0},
        compiler_params=_cparams("arbitrary"),
        name="mixer_a_sample",
    )(*x_args, *pend_args, mod_s, mod_s, mod_s, ng, w_in, ln_g, ln_b, wsd, bsd, w_out)
    return x_all, cv_p, cv_s.reshape(DEC_BATCH, 1, D)


def _mix_b_tail(pooled_groups, wgrp_ref, bgrp_ref, scale_ref, wout_ref):
    outs = [
        _dot(pg.astype(BF16), wgrp_ref[g]) + bgrp_ref[g]
        for g, pg in enumerate(pooled_groups)
    ]
    mixed = jnp.concatenate(outs, axis=1) * scale_ref[...]
    return _dot(mixed.astype(BF16), wout_ref[...])


def _mix_b_prompt_kernel(*refs, pending):
    x, refs = _residual_in(refs, pending)
    (sh_ref, sc_ref, gt_ref, ng_ref, win_ref, wgrp_ref, bgrp_ref, scale_ref, wout_ref,
     x1_ref, st_ref, full_ref, s2_ref, s4_ref, s8_ref) = refs
    assert POOL_WINDOWS == (2, 4, 8, 16)
    i = pl.program_id(0)
    j = i % TILES_PER_SEQ
    h = (_rms(x, ng_ref[...]) * (1.0 + sc_ref[...]) + sh_ref[...]).astype(BF16)
    p = _dot(h, win_ref[...])
    gd = B_GROUP_DIM
    top = SUBLANES
    cur = top + POOL_CARRY
    rows = TM + POOL_CARRY

    @pl.when(i == 0)
    def _():
        for ref in (full_ref, s2_ref, s4_ref, s8_ref):
            ref[0:top, :] = jnp.zeros((top, ref.shape[1]), F32)

    full_ref[top:cur, :] = jnp.where(j == 0, 0.0, full_ref[top:cur, :])
    full_ref[cur:cur + TM, :] = p
    s2_ref[top:top + rows, :] = full_ref[top:top + rows, :] + full_ref[pl.ds(top - 1, rows), :]
    s4_ref[top:top + rows, :] = s2_ref[top:top + rows, gd:] + s2_ref[pl.ds(top - 2, rows), gd:]
    s8_ref[top:top + rows, :] = s4_ref[top:top + rows, gd:] + s4_ref[pl.ds(top - 4, rows), gd:]
    s16 = s8_ref[cur:cur + TM, gd:] + s8_ref[pl.ds(cur - 8, TM), gd:]
    sums = [s2_ref[cur:cur + TM, 0:gd], s4_ref[cur:cur + TM, 0:gd], s8_ref[cur:cur + TM, 0:gd], s16]
    pos = j * TM + lax.broadcasted_iota(jnp.int32, (TM, 1), 0)
    pooled = []
    for g, w in enumerate(POOL_WINDOWS):
        cnt = jnp.minimum(w, pos + 1).astype(F32)
        pooled.append(sums[g] / cnt - p[:, g * gd:(g + 1) * gd])
    y = _mix_b_tail(pooled, wgrp_ref, bgrp_ref, scale_ref, wout_ref)
    x1_ref[...] = x + gt_ref[...] * y

    @pl.when(j == TILES_PER_SEQ - 1)
    def _():
        st_ref[...] = full_ref[pl.ds(cur + TM - POOL_BUF, POOL_BUF), :]

    full_ref[top:cur, :] = full_ref[pl.ds(TM + top, POOL_CARRY), :]


def _mix_b_sample_kernel(*refs, pending):
    x, refs = _residual_in(refs, pending)
    (sh_ref, sc_ref, gt_ref, ng_ref, st_ref, win_ref, wgrp_ref, bgrp_ref, scale_ref, wout_ref,
     x1_ref, nst_ref) = refs
    h = (_rms(x, ng_ref[...]) * (1.0 + sc_ref[...]) + sh_ref[...]).astype(BF16)
    p = _dot(h, win_ref[...])
    pooled = []
    for g, w in enumerate(POOL_WINDOWS):
        s = p[:, g * B_GROUP_DIM:(g + 1) * B_GROUP_DIM]
        for k in range(1, w):
            s = s + st_ref[POOL_BUF - k, :, g * B_GROUP_DIM:(g + 1) * B_GROUP_DIM]
        cnt = float(min(w, PAST_LEN + 1))
        pooled.append(s / cnt - p[:, g * B_GROUP_DIM:(g + 1) * B_GROUP_DIM])
    y = _mix_b_tail(pooled, wgrp_ref, bgrp_ref, scale_ref, wout_ref)
    x1_ref[...] = x + gt_ref[...] * y
    for k in range(1, POOL_BUF):
        nst_ref[k - 1] = st_ref[k]
    nst_ref[POOL_BUF - 1] = p


def _mixer_b(x_all, pending, mod_p, mod_s, ng, layer, lb, state, w_in, w_grp, b_grp, scale,
             w_out):
    gd = B_GROUP_DIM
    common = [
        _layer_block((D, D), lb), _layer_block((len(POOL_WINDOWS), gd, gd), lb),
        _layer_block((len(POOL_WINDOWS), 1, gd), lb), _layer_block((1, D), lb),
        _layer_block((D, D), lb),
    ]
    has_pending = pending is not None
    pend_specs, pend_args = _pending_prompt(pending, mod_p, layer)
    x_all, st_p = pl.pallas_call(
        functools.partial(_mix_b_prompt_kernel, pending=has_pending),
        grid=(N_PROMPT_TILES,),
        in_specs=[pl.BlockSpec((TM, D), lambda i: (i, 0))] + pend_specs
        + _mod_specs_prompt(layer, (0, 1, 2)) + [_full((1, D))] + common,
        out_specs=[
            pl.BlockSpec((TM, D), lambda i: (i, 0)),
            pl.BlockSpec((None, POOL_BUF, D), lambda i: (i // TILES_PER_SEQ, 0, 0)),
        ],
        out_shape=[
            jax.ShapeDtypeStruct((T_PAD, D), F32),
            jax.ShapeDtypeStruct((BATCH, POOL_BUF, D), F32),
        ],
        scratch_shapes=[
            pltpu.VMEM((SUBLANES + POOL_CARRY + TM, D - g * gd), F32) for g in (0, 0, 1, 2)],
        input_output_aliases={0: 0},
        compiler_params=_cparams("arbitrary"),
        name="mixer_b_prompt",
    )(x_all, *pend_args, mod_p, mod_p, mod_p, ng, w_in, w_grp, b_grp, scale, w_out)

    r = SAMPLE_ROWS
    sblk = T_PROMPT // r
    pend_specs, pend_args = _pending_sample(pending, mod_s, layer, r)
    x_all, st_s = pl.pallas_call(
        functools.partial(_mix_b_sample_kernel, pending=has_pending),
        grid=(DEC_BATCH // r,),
        in_specs=[pl.BlockSpec((r, D), lambda i: (sblk + i, 0))] + pend_specs
        + _mod_specs_sample(layer, (0, 1, 2), r) + [_full((1, D))]
        + [pl.BlockSpec((None, POOL_BUF, r, D), lambda i: (lb, 0, i, 0))] + common,
        out_specs=[
            pl.BlockSpec((r, D), lambda i: (sblk + i, 0)),
            pl.BlockSpec((POOL_BUF, r, D), lambda i: (0, i, 0)),
        ],
        out_shape=[
            jax.ShapeDtypeStruct((T_PAD, D), F32),
            jax.ShapeDtypeStruct((POOL_BUF, DEC_BATCH, D), F32),
        ],
        input_output_aliases={0: 0},
        compiler_params=_cparams("arbitrary"),
        name="mixer_b_sample",
    )(x_all, *pend_args, mod_s, mod_s, mod_s, ng, state, w_in, w_grp, b_grp, scale, w_out)
    return x_all, st_p, jnp.transpose(st_s, (1, 0, 2))


def _mix_c_glu(x, sh, sc, ng, win_ref, bin_ref):
    h = (_rms(x, ng) * (1.0 + sc) + sh).astype(BF16)
    ag = _dot(h, win_ref[...]) + bin_ref[...]
    return ag[:, :D] * jax.nn.sigmoid(ag[:, D:])


def _mix_c_tail(conv, lng_ref, lnb_ref, wout_ref, bout_ref):
    z = _ln(conv, lng_ref[...], lnb_ref[...])
    z = z * jax.nn.sigmoid(z)
    return _dot(z.astype(BF16), wout_ref[...]) + bout_ref[...]


def _mix_c_prompt_kernel(*refs, pending):
    x, refs = _residual_in(refs, pending)
    (sh_ref, sc_ref, gt_ref, ng_ref, win_ref, bin_ref, wdw_ref, bdw_ref, lng_ref, lnb_ref,
     wout_ref, bout_ref, x1_ref, st_ref, full_ref, conv_ref, *shift_refs) = refs
    i = pl.program_id(0)
    j = i % TILES_PER_SEQ
    glu = _mix_c_glu(x, sh_ref[...], sc_ref[...], ng_ref[...], win_ref, bin_ref)

    @pl.when(j == 0)
    def _():
        full_ref[0:CONV_CARRY, :] = jnp.zeros((CONV_CARRY, D), F32)

    full_ref[CONV_CARRY:CONV_CARRY + TM, :] = glu
    off = CONV_CARRY - CONV_BUF
    span = TM + (off + CONV_BUF) // SUBLANES * SUBLANES - SUBLANES
    for c in range(D // LANES):
        cols = slice(c * LANES, (c + 1) * LANES)
        shift_ref = shift_refs[c % len(shift_refs)]
        for r in range(1, SUBLANES):
            shift_ref[r - 1] = full_ref[pl.ds(r, span), cols]
        for rb in range(TM // CONV_ROWS):
            acc = None
            for k in range(CONV_WIDTH):
                q, r = divmod(off + k, SUBLANES)
                start = q * SUBLANES + rb * CONV_ROWS
                if r == 0:
                    src = full_ref[pl.ds(start, CONV_ROWS), cols]
                else:
                    src = shift_ref[r - 1, pl.ds(start, CONV_ROWS), :]
                term = src * wdw_ref[k:k + 1, cols]
                acc = term if acc is None else acc + term
            conv_ref[pl.ds(rb * CONV_ROWS, CONV_ROWS), cols] = acc + bdw_ref[:, cols]
    y = _mix_c_tail(conv_ref[...], lng_ref, lnb_ref, wout_ref, bout_ref)
    x1_ref[...] = x + gt_ref[...] * y

    @pl.when(j == TILES_PER_SEQ - 1)
    def _():
        st_ref[...] = full_ref[pl.ds(CONV_CARRY + TM - CONV_BUF, CONV_BUF), :]

    full_ref[0:CONV_CARRY, :] = full_ref[pl.ds(TM, CONV_CARRY), :]


def _mix_c_sample_kernel(*refs, pending):
    x, refs = _residual_in(refs, pending)
    (sh_ref, sc_ref, gt_ref, ng_ref, st_ref, win_ref, bin_ref, wdw_ref, bdw_ref, lng_ref,
     lnb_ref, wout_ref, bout_ref, x1_ref, nst_ref) = refs
    glu = _mix_c_glu(x, sh_ref[...], sc_ref[...], ng_ref[...], win_ref, bin_ref)
    acc = glu * wdw_ref[CONV_BUF:CONV_BUF + 1, :]
    for k in range(CONV_BUF):
        acc = acc + st_ref[k] * wdw_ref[k:k + 1, :]
    y = _mix_c_tail(acc + bdw_ref[...], lng_ref, lnb_ref, wout_ref, bout_ref)
    x1_ref[...] = x + gt_ref[...] * y
    for k in range(1, CONV_BUF):
        nst_ref[k - 1] = st_ref[k]
    nst_ref[CONV_BUF - 1] = glu


def _mixer_c(x_all, pending, mod_p, mod_s, ng, layer, lc, state, w_in, b_in, w_dw, b_dw, ln_g,
             ln_b, w_out, b_out):
    common = [
        _layer_block((D, 2 * D), lc), _layer_block((1, 2 * D), lc),
        _layer_block((CONV_WIDTH, D), lc), _layer_block((1, D), lc), _layer_block((1, D), lc),
        _layer_block((1, D), lc), _layer_block((D, D), lc), _layer_block((1, D), lc),
    ]
    has_pending = pending is not None
    pend_specs, pend_args = _pending_prompt(pending, mod_p, layer)
    x_all, st_p = pl.pallas_call(
        functools.partial(_mix_c_prompt_kernel, pending=has_pending),
        grid=(N_PROMPT_TILES,),
        in_specs=[pl.BlockSpec((TM, D), lambda i: (i, 0))] + pend_specs
        + _mod_specs_prompt(layer, (0, 1, 2)) + [_full((1, D))] + common,
        out_specs=[
            pl.BlockSpec((TM, D), lambda i: (i, 0)),
            pl.BlockSpec((None, CONV_BUF, D), lambda i: (i // TILES_PER_SEQ, 0, 0)),
        ],
        out_shape=[
            jax.ShapeDtypeStruct((T_PAD, D), F32),
            jax.ShapeDtypeStruct((BATCH, CONV_BUF, D), F32),
        ],
        scratch_shapes=[pltpu.VMEM((TM + CONV_CARRY, D), F32), pltpu.VMEM((TM, D), F32)]
        + [pltpu.VMEM((SUBLANES - 1, TM + CONV_CARRY - SUBLANES, LANES), F32)] * 2,
        input_output_aliases={0: 0},
        compiler_params=_cparams("arbitrary"),
        name="mixer_c_prompt",
    )(x_all, *pend_args, mod_p, mod_p, mod_p, ng, w_in, b_in, w_dw, b_dw, ln_g, ln_b, w_out,
      b_out)

    r = SAMPLE_ROWS
    sblk = T_PROMPT // r
    pend_specs, pend_args = _pending_sample(pending, mod_s, layer, r)
    x_all, st_s = pl.pallas_call(
        functools.partial(_mix_c_sample_kernel, pending=has_pending),
        grid=(DEC_BATCH // r,),
        in_specs=[pl.BlockSpec((r, D), lambda i: (sblk + i, 0))] + pend_specs
        + _mod_specs_sample(layer, (0, 1, 2), r) + [_full((1, D))]
        + [pl.BlockSpec((None, CONV_BUF, r, D), lambda i: (lc, 0, i, 0))] + common,
        out_specs=[
            pl.BlockSpec((r, D), lambda i: (sblk + i, 0)),
            pl.BlockSpec((CONV_BUF, r, D), lambda i: (0, i, 0)),
        ],
        out_shape=[
            jax.ShapeDtypeStruct((T_PAD, D), F32),
            jax.ShapeDtypeStruct((CONV_BUF, DEC_BATCH, D), F32),
        ],
        input_output_aliases={0: 0},
        compiler_params=_cparams("arbitrary"),
        name="mixer_c_sample",
    )(x_all, *pend_args, mod_s, mod_s, mod_s, ng, state, w_in, b_in, w_dw, b_dw, ln_g, ln_b,
      w_out, b_out)
    return x_all, st_p, jnp.transpose(st_s, (1, 0, 2))


def _tile_mod(i, p_ref, s_ref):
    s_rows = jnp.concatenate([s_ref[...], jnp.zeros((TM - DEC_BATCH, D), F32)], axis=0)
    return jnp.where(i >= N_PROMPT_TILES, s_rows, p_ref[...])


def _unified_mod_specs(layer, chunks):
    specs = []
    for c in chunks:
        specs.append(pl.BlockSpec(
            (None, None, 1, D),
            lambda i, c=c: (layer, jnp.minimum(i // TILES_PER_SEQ, BATCH - 1), 0, c)))
        specs.append(pl.BlockSpec((None, DEC_BATCH, D), lambda i, c=c: (layer, 0, c)))
    return specs


def _router_kernel(x_ref, shp_ref, shs_ref, scp_ref, scs_ref, ng_ref, wr_ref, br_ref,
                   h2_ref, meta_ref, wts_ref, cnt_ref, carry_ref, before_ref):
    i = pl.program_id(0)

    @pl.when(i == 0)
    def _():
        carry_ref[...] = jnp.zeros((ROUTER_ROWS, 1), F32)
        earlier = lax.broadcasted_iota(jnp.int32, (TM, TM), 0)
        token = lax.broadcasted_iota(jnp.int32, (TM, TM), 1)
        before_ref[...] = (earlier < token).astype(BF16)

    sh = _tile_mod(i, shp_ref, shs_ref)
    sc = _tile_mod(i, scp_ref, scs_ref)
    h2 = _rms(x_ref[...], ng_ref[...]) * (1.0 + sc) + sh
    h2_ref[...] = _pack_bf16_pairs(h2)
    lg = lax.dot_general(wr_ref[...], h2.astype(BF16), (((1,), (1,)), ((), ())),
                         preferred_element_type=F32) + br_ref[...]
    row = lax.broadcasted_iota(jnp.int32, (ROUTER_ROWS, TM), 0)
    row_f = row.astype(F32)
    far = float(ROUTER_ROWS)
    neg = -jnp.inf
    is_g = row < N_GROUPS
    gm = jnp.where(is_g, lg, neg)
    gmax = jnp.max(gm, axis=0, keepdims=True)
    gsel = jnp.min(jnp.where(gm == gmax, row_f, far), axis=0, keepdims=True).astype(jnp.int32)
    gsum = jnp.sum(jnp.where(is_g, jnp.exp(lg - gmax), 0.0), axis=0, keepdims=True)
    g_w = 1.0 / gsum
    e_row = row - EXPERT_LANE0
    in_grp = (e_row >= 0) & (e_row < N_EXPERTS) & (
        lax.shift_right_arithmetic(e_row, EXPERTS_PER_GROUP.bit_length() - 1) == gsel)
    em = jnp.where(in_grp, lg, neg)
    m1 = jnp.max(em, axis=0, keepdims=True)
    i1 = jnp.min(jnp.where(em == m1, row_f, far), axis=0, keepdims=True).astype(jnp.int32)
    em2 = jnp.where(row == i1, neg, em)
    m2 = jnp.max(em2, axis=0, keepdims=True)
    i2 = jnp.min(jnp.where(em2 == m2, row_f, far), axis=0, keepdims=True).astype(jnp.int32)
    e2 = jnp.exp(m2 - m1)
    den = 1.0 + e2
    w1 = (1.0 / den) * g_w
    w2 = (e2 / den) * g_w

    hit1 = row == i1
    hit2 = row == i2
    assign = jnp.where(hit1 | hit2, 1.0, 0.0).astype(BF16)
    seen = _dot(assign, before_ref[...]) + carry_ref[...]
    rank1 = jnp.sum(jnp.where(hit1, seen, 0.0), axis=0, keepdims=True)
    rank2 = jnp.sum(jnp.where(hit2, seen, 0.0), axis=0, keepdims=True)
    carry_ref[...] = carry_ref[...] + jnp.sum(assign.astype(F32), axis=1, keepdims=True)
    cnt_ref[...] = carry_ref[...].astype(jnp.int32)

    ex1 = (i1 - EXPERT_LANE0).astype(F32)
    ex2 = (i2 - EXPERT_LANE0).astype(F32)
    field = lax.broadcasted_iota(jnp.int32, (META_ROWS, TM), 0)
    meta_ref[...] = jnp.where(field == 0, ex1, jnp.where(field == 1, ex2,
                              jnp.where(field == 2, rank1, jnp.where(field == 3, rank2, 0.0))))
    slot = lax.broadcasted_iota(jnp.int32, (LANES, TM), 0)
    wts_ref[...] = jnp.where(slot == 0, w1, jnp.where(slot == 1, w2, 0.0)).T


def _router(x_all, mod_p, mod_s, ng, layer, w_r, b_r):
    return pl.pallas_call(
        _router_kernel,
        grid=(N_TILES,),
        in_specs=[pl.BlockSpec((TM, D), lambda i: (i, 0))]
        + _unified_mod_specs(layer, (3, 4))
        + [_full((1, D)), _full((ROUTER_ROWS, D)), _full((ROUTER_ROWS, 1))],
        out_specs=[
            pl.BlockSpec((TM, D // 2), lambda i: (i, 0)),
            pl.BlockSpec((META_ROWS, TM), lambda i: (0, i)),
            pl.BlockSpec((TM, LANES), lambda i: (i, 0)),
            pl.BlockSpec((ROUTER_ROWS, 1), lambda i: (0, 0)),
        ],
        out_shape=[
            jax.ShapeDtypeStruct((T_PAD, D // 2), jnp.int32),
            jax.ShapeDtypeStruct((META_ROWS, T_PAD), F32),
            jax.ShapeDtypeStruct((T_PAD, LANES), F32),
            jax.ShapeDtypeStruct((ROUTER_ROWS, 1), jnp.int32),
        ],
        scratch_shapes=[pltpu.VMEM((ROUTER_ROWS, 1), F32), pltpu.VMEM((TM, TM), BF16)],
        compiler_params=_cparams("arbitrary"),
        name="router",
    )(x_all, mod_p, mod_s, mod_p, mod_s, ng, w_r, b_r)


SC_INDEX_MAX = 128
SC_ROWS_BYTES = 400 * 1024


def _gather_chunk(per_worker, row_bytes, buffers=1):
    top = min(SC_INDEX_MAX, SC_ROWS_BYTES // (row_bytes * buffers)) // SUBLANES * SUBLANES
    for ch in range(top, SUBLANES - 1, -SUBLANES):
        if per_worker % ch == 0:
            return ch
    raise ValueError(per_worker)


def _sc_gather(table, idx):
    nc, nw = SC_CORES, SC_CORES * SC_SUBCORES
    m = idx.shape[0]
    per_w = m // nw
    assert per_w * nw == m
    width = table.shape[1]
    ch = _gather_chunk(per_w, width * table.dtype.itemsize, buffers=2)
    mesh = plsc.VectorSubcoreMesh(core_axis_name="c", subcore_axis_name="s")

    n_chunks = per_w // ch

    @functools.partial(
        pl.kernel,
        out_type=jax.ShapeDtypeStruct((m, width), table.dtype),
        mesh=mesh,
        scratch_types=[pltpu.VMEM((ch,), jnp.int32)] * 2
        + [pltpu.VMEM((ch, width), table.dtype)] * 2
        + [pltpu.SemaphoreType.DMA] * 4,
    )
    def gather_kernel(t_hbm, i_hbm, o_hbm, idx_a, idx_b, rows_a, rows_b, sg_a, sg_b, sw_a, sw_b):
        wid = lax.axis_index("s") * nc + lax.axis_index("c")
        base = wid * per_w
        idx_v, rows_v, sem_g, sem_w = (idx_a, idx_b), (rows_a, rows_b), (sg_a, sg_b), (sw_a, sw_b)

        def out_rows(j):
            return o_hbm.at[pl.ds(pl.multiple_of(base + j * ch, 8), ch)]

        gathers, writes = {}, {}
        for j in range(n_chunks):
            b = j % 2
            if j >= 2:
                writes[j - 2].wait()
            pltpu.sync_copy(i_hbm.at[pl.ds(pl.multiple_of(base + j * ch, 8), ch)], idx_v[b])
            gathers[j] = pltpu.async_copy(t_hbm.at[idx_v[b]], rows_v[b], sem_g[b])
            if j >= 1:
                gathers[j - 1].wait()
                writes[j - 1] = pltpu.async_copy(rows_v[1 - b], out_rows(j - 1), sem_w[1 - b])
        last = n_chunks - 1
        gathers[last].wait()
        writes[last] = pltpu.async_copy(rows_v[last % 2], out_rows(last), sem_w[last % 2])
        for j in range(max(last - 1, 0), n_chunks):
            writes[j].wait()

    return gather_kernel(table, idx)


def _sc_scatter2(rows, dest, n_out):
    nc, nw = SC_CORES, SC_CORES * SC_SUBCORES
    n, width = rows.shape
    assert dest.shape == (2 * n,) and n % SUBLANES == 0
    per_w = n // nw
    assert per_w * nw == n
    ch = _gather_chunk(per_w, width * rows.dtype.itemsize)
    mesh = plsc.VectorSubcoreMesh(core_axis_name="c", subcore_axis_name="s")

    @functools.partial(
        pl.kernel,
        out_type=jax.ShapeDtypeStruct((n_out, width), rows.dtype),
        mesh=mesh,
        scratch_types=[
            pltpu.VMEM((ch,), jnp.int32),
            pltpu.VMEM((ch,), jnp.int32),
            pltpu.VMEM((ch, width), rows.dtype),
            pltpu.SemaphoreType.DMA,
            pltpu.SemaphoreType.DMA,
        ],
    )
    def scatter_kernel(r_hbm, d_hbm, o_hbm, i0_v, i1_v, rows_v, sem0, sem1):
        wid = lax.axis_index("s") * nc + lax.axis_index("c")
        base = wid * per_w

        @pl.loop(0, per_w // ch)
        def _(j):
            off = pl.multiple_of(base + j * ch, 8)
            pltpu.sync_copy(d_hbm.at[pl.ds(off, ch)], i0_v)
            pltpu.sync_copy(d_hbm.at[pl.ds(pl.multiple_of(n + off, 8), ch)], i1_v)
            pltpu.sync_copy(r_hbm.at[pl.ds(off, ch)], rows_v)
            c0 = pltpu.async_copy(rows_v, o_hbm.at[i0_v], sem0)
            c1 = pltpu.async_copy(rows_v, o_hbm.at[i1_v], sem1)
            c0.wait()
            c1.wait()

    return scatter_kernel(rows, dest)


def _expert_kernel(be_ref, first_ref, nvalid_ref, next_ref, nused_ref, x_ref, wg_hbm, wu_hbm,
                   wd_hbm, y_ref, wgf_ref, wuf_ref, wdf_ref, wgb_ref, wub_ref, wdb_ref, sems,
                   *, layer):
    b = pl.program_id(0)

    def weight_copies(e):
        return (
            pltpu.make_async_copy(wg_hbm.at[layer, e], wgf_ref, sems.at[0]),
            pltpu.make_async_copy(wu_hbm.at[layer, e], wuf_ref, sems.at[1]),
            pltpu.make_async_copy(wd_hbm.at[layer, e], wdf_ref, sems.at[2]),
        )

    @pl.when(b == 0)
    def _():
        for cp in weight_copies(be_ref[0]):
            cp.start()

    @pl.when(b < nused_ref[0])
    def _():
        @pl.when(first_ref[b] == 1)
        def _():
            for cp in weight_copies(be_ref[b]):
                cp.wait()
            wgb_ref[...] = wgf_ref[...].astype(BF16)
            wub_ref[...] = wuf_ref[...].astype(BF16)
            wdb_ref[...] = wdf_ref[...].astype(BF16)

            @pl.when(next_ref[b] >= 0)
            def _():
                for cp in weight_copies(next_ref[b]):
                    cp.start()

        def run(rows):
            live = lax.broadcasted_iota(jnp.int32, (rows, 1), 0) < nvalid_ref[b]
            x = _unpack_bf16_pairs(jnp.where(live, x_ref[0:rows, :], 0), BF16)
            g = _dot(x, wgb_ref[...])
            u = _dot(x, wub_ref[...])
            hmid = (g * jax.nn.sigmoid(g)) * u
            y_ref[0:rows, :] = _pack_bf16_pairs(_dot(hmid.astype(BF16), wdb_ref[...]))

        half = BM // 2

        @pl.when(nvalid_ref[b] > half)
        def _():
            run(BM)

        @pl.when(nvalid_ref[b] <= half)
        def _():
            run(half)
            y_ref[half:BM, :] = jnp.zeros((BM - half, D // 2), jnp.int32)

    @pl.when(b >= nused_ref[0])
    def _():
        y_ref[...] = jnp.zeros((BM, D // 2), jnp.int32)


def _experts(x_sorted, block_expert, first, nvalid, next_expert, nused, layer, w_gate, w_up,
             w_down):
    def row_map(b, be, fi, nv, nx, nu):
        return (jnp.minimum(b, nu[0] - 1), 0)

    def out_map(b, be, fi, nv, nx, nu):
        return (b, 0)

    grid_spec = pltpu.PrefetchScalarGridSpec(
        num_scalar_prefetch=5,
        grid=(N_BLOCKS,),
        in_specs=[
            pl.BlockSpec((BM, D // 2), row_map),
            pl.BlockSpec(memory_space=pl.ANY),
            pl.BlockSpec(memory_space=pl.ANY),
            pl.BlockSpec(memory_space=pl.ANY),
        ],
        out_specs=pl.BlockSpec((BM, D // 2), out_map),
        scratch_shapes=[
            pltpu.VMEM((D, D_EXPERT), F32),
            pltpu.VMEM((D, D_EXPERT), F32),
            pltpu.VMEM((D_EXPERT, D), F32),
            pltpu.VMEM((D, D_EXPERT), BF16),
            pltpu.VMEM((D, D_EXPERT), BF16),
            pltpu.VMEM((D_EXPERT, D), BF16),
            pltpu.SemaphoreType.DMA((3,)),
        ],
    )
    return pl.pallas_call(
        functools.partial(_expert_kernel, layer=layer),
        grid_spec=grid_spec,
        out_shape=jax.ShapeDtypeStruct((P_ROWS, D // 2), jnp.int32),
        compiler_params=_cparams("arbitrary"),
        name="experts",
    )(block_expert, first, nvalid, next_expert, nused, x_sorted, w_gate, w_up, w_down)


def _combine_final_kernel(x_ref, g0_ref, g1_ref, wts_ref, gtp_ref, gts_ref, fg_ref, yp_ref,
                          ys_ref):
    i = pl.program_id(0)
    gate = _tile_mod(i, gtp_ref, gts_ref)
    y = _rms(_plus_moe(x_ref[...], g0_ref, g1_ref, wts_ref, gate), fg_ref[...])

    @pl.when(i < N_PROMPT_TILES)
    def _():
        yp_ref[...] = y

    @pl.when(i == N_PROMPT_TILES)
    def _():
        ys_ref[...] = y[0:DEC_BATCH, :]


def _combine_final(x_all, gathered, wts, mod_p, mod_s, layer, final_g):
    in_specs = [
        pl.BlockSpec((TM, D), lambda i: (i, 0)),
        pl.BlockSpec((TM, D // 2), lambda i: (i, 0)),
        pl.BlockSpec((TM, D // 2), lambda i: (i + N_TILES, 0)),
        pl.BlockSpec((TM, LANES), lambda i: (i, 0)),
    ] + _unified_mod_specs(layer, (5,))
    return pl.pallas_call(
        _combine_final_kernel,
        grid=(N_PROMPT_TILES + 1,),
        in_specs=in_specs + [_full((1, D))],
        out_specs=[
            pl.BlockSpec((TM, D), lambda i: (jnp.minimum(i, N_PROMPT_TILES - 1), 0)),
            pl.BlockSpec((DEC_BATCH, D), lambda i: (0, 0)),
        ],
        out_shape=[
            jax.ShapeDtypeStruct((T_PROMPT, D), F32),
            jax.ShapeDtypeStruct((DEC_BATCH, D), F32),
        ],
        compiler_params=_cparams("arbitrary"),
        name="combine_final",
    )(x_all, gathered, gathered, wts, mod_p, mod_s, final_g)


def _plan_kernel(cnt_ref, meta_ref, d_ref, be_ref, first_ref, nvalid_ref, next_ref,
                 nused_ref, pstart_ref, nxt_ref):
    shift = BM.bit_length() - 1
    assert 1 << shift == BM

    def count(e):
        return cnt_ref[EXPERT_LANE0 + e, 0]

    def scan_back(k, nxt):
        e = N_EXPERTS - 1 - k
        nxt_ref[e] = nxt
        return jnp.where(count(e) > 0, e, nxt)

    lax.fori_loop(0, N_EXPERTS, scan_back, jnp.int32(-1))

    def fill(e, blk):
        c = count(e)
        pstart_ref[e] = blk << shift

        def one(j, carry):
            be_ref[blk + j] = e
            first_ref[blk + j] = jnp.where(j == 0, 1, 0)
            nvalid_ref[blk + j] = jnp.minimum(c - (j << shift), BM)
            next_ref[blk + j] = nxt_ref[e]
            return carry

        n_blk = (c + (BM - 1)) >> shift
        lax.fori_loop(0, n_blk, one, 0)
        return blk + n_blk

    used = lax.fori_loop(0, N_EXPERTS, fill, jnp.int32(0))
    nused_ref[0] = used

    def tail(b, carry):
        be_ref[b] = 0
        first_ref[b] = 0
        nvalid_ref[b] = 0
        next_ref[b] = -1
        return carry

    lax.fori_loop(used, N_BLOCKS, tail, 0)

    m = meta_ref[...]
    expert = m[0:2, :]
    dest = m[2:4, :]
    for e in range(N_EXPERTS):
        dest = dest + jnp.where(expert == float(e), pstart_ref[e].astype(F32), 0.0)
    dest = dest.astype(jnp.int32)
    d_ref[:, 0:T_PAD] = dest[0:1, :]
    d_ref[:, T_PAD:2 * T_PAD] = dest[1:2, :]


def _plan(cnt, meta):
    smem = pl.BlockSpec(memory_space=pltpu.SMEM)
    blocks = jax.ShapeDtypeStruct((N_BLOCKS,), jnp.int32)
    return pl.pallas_call(
        _plan_kernel,
        grid=(1,),
        in_specs=[smem, _full((META_ROWS, T_PAD))],
        out_specs=[_full((1, 2 * T_PAD)), smem, smem, smem, smem, smem],
        out_shape=[
            jax.ShapeDtypeStruct((1, 2 * T_PAD), jnp.int32),
            blocks, blocks, blocks, blocks,
            jax.ShapeDtypeStruct((1,), jnp.int32),
        ],
        scratch_shapes=[pltpu.SMEM((N_EXPERTS,), jnp.int32), pltpu.SMEM((N_EXPERTS,), jnp.int32)],
        compiler_params=_cparams("arbitrary"),
        name="plan",
    )(cnt, meta)


def _moe(x_all, mod_p, mod_s, ng, layer, w_r, b_r, w_gate, w_up, w_down):
    h2, meta, wts, cnt = _router(x_all, mod_p, mod_s, ng, layer, w_r, b_r)
    dest, block_expert, first, nvalid, next_expert, nused = _plan(cnt, meta)
    dest = dest.reshape(2 * T_PAD)
    x_sorted = _sc_scatter2(h2, dest, P_ROWS)
    y_sorted = _experts(x_sorted, block_expert, first, nvalid, next_expert, nused, layer,
                        w_gate, w_up, w_down)
    gathered = _sc_gather(y_sorted, dest)
    return gathered, wts


def kernel(x_prompt, x_sample, c_prompt, c_sample, state_pool, state_conv, ada_w, ada_b, norm_g, final_g, a_w_in, a_ln_g, a_ln_b, a_w_s, a_b_s, a_w_out, b_w_in, b_w_grp, b_b_grp, b_scale, b_w_out, c_w_in, c_b_in, c_w_dw, c_b_dw, c_ln_g, c_ln_b, c_w_out, c_b_out, moe_w_grp, moe_b_grp, moe_w_exp, moe_b_exp, moe_w_gate, moe_w_up, moe_w_down):
    x_all = (x_prompt.reshape(T_PROMPT, D), x_sample.reshape(DEC_BATCH, D))
    mod_p, mod_s = _ada_mods(jnp.concatenate([c_prompt, c_sample], axis=0), ada_w, ada_b)
    mod_p = mod_p.reshape(DEPTH, BATCH, 1, 6 * D)

    a_w_in_b, a_w_out_b = a_w_in.astype(BF16), a_w_out.astype(BF16)
    b_w_in_b, b_w_grp_b, b_w_out_b = b_w_in.astype(BF16), b_w_grp.astype(BF16), b_w_out.astype(BF16)
    c_w_in_b, c_w_out_b = c_w_in.astype(BF16), c_w_out.astype(BF16)
    pad_rows = ROUTER_ROWS - N_GROUPS - N_EXPERTS
    w_r = jnp.pad(jnp.swapaxes(jnp.concatenate([moe_w_grp, moe_w_exp], axis=2), 1, 2),
                  ((0, 0), (0, pad_rows), (0, 0))).astype(BF16)
    b_r = jnp.pad(jnp.concatenate([moe_b_grp, moe_b_exp], axis=1), ((0, 0), (0, pad_rows)))

    new_a_p, new_a_s, new_b_p, new_b_s, new_c_p, new_c_s = [], [], [], [], [], []
    pool_tm = jnp.transpose(state_pool, (0, 2, 1, 3))
    conv_tm = jnp.transpose(state_conv, (0, 2, 1, 3))
    ia = ib = ic = 0
    assert len(range(0, DEPTH, 3)) == 2
    pending = None
    for layer in range(DEPTH):
        ng1 = norm_g[layer, 0].reshape(1, D)
        ng2 = norm_g[layer, 1].reshape(1, D)
        kind = layer % 3
        if kind == 0:
            x_all, st_p, st_s = _mixer_a(
                x_all, pending, new_a_p[0] if new_a_p else None, mod_p, mod_s, ng1, layer, ia,
                a_w_in_b, a_ln_g.reshape(-1, 1, D), a_ln_b.reshape(-1, 1, D), a_w_s, a_b_s,
                a_w_out_b)
            new_a_p = [st_p]
            new_a_s.append(st_s)
            ia += 1
        elif kind == 1:
            x_all, st_p, st_s = _mixer_b(
                x_all, pending, mod_p, mod_s, ng1, layer, ib, pool_tm, b_w_in_b, b_w_grp_b,
                b_b_grp.reshape(-1, len(POOL_WINDOWS), 1, B_GROUP_DIM),
                b_scale.reshape(-1, 1, D), b_w_out_b)
            new_b_p.append(st_p)
            new_b_s.append(st_s)
            ib += 1
        else:
            x_all, st_p, st_s = _mixer_c(
                x_all, pending, mod_p, mod_s, ng1, layer, ic, conv_tm, c_w_in_b,
                c_b_in.reshape(-1, 1, 2 * D), c_w_dw, c_b_dw.reshape(-1, 1, D),
                c_ln_g.reshape(-1, 1, D), c_ln_b.reshape(-1, 1, D), c_w_out_b,
                c_b_out.reshape(-1, 1, D))
            new_c_p.append(st_p)
            new_c_s.append(st_s)
            ic += 1
        pending = _moe(x_all, mod_p, mod_s, ng2, layer, w_r[layer],
                       b_r[layer].reshape(ROUTER_ROWS, 1), moe_w_gate, moe_w_up, moe_w_down)

    y_p, y_s = _combine_final(x_all, *pending, mod_p, mod_s, DEPTH - 1, final_g.reshape(1, D))
    return (y_p.reshape(BATCH, SEQ, D), y_s.reshape(DEC_BATCH, 1, D),
            new_a_p[0], jnp.stack(new_a_s), jnp.stack(new_b_p), jnp.stack(new_b_s),
            jnp.stack(new_c_p), jnp.stack(new_c_s))
```

```python
import functools

import jax
import jax.numpy as jnp
from jax import lax
from jax.experimental import pallas as pl
from jax.experimental.pallas import tpu as pltpu
from jax.experimental.pallas import tpu_sc as plsc

F32 = jnp.float32
BF16 = jnp.bfloat16

D = 1024
BATCH = 8
SEQ = 2048
DEPTH = 4
DEC_BATCH = 128
PAST_LEN = 16384
CHUNK = 128
A_HEADS = 8
POOL_WINDOWS = (2, 4, 8, 16)
B_GROUP_DIM = D // len(POOL_WINDOWS)
POOL_BUF = max(POOL_WINDOWS) - 1
CONV_WIDTH = 31
CONV_BUF = CONV_WIDTH - 1
N_GROUPS = 4
EXPERTS_PER_GROUP = 8
N_EXPERTS = N_GROUPS * EXPERTS_PER_GROUP
D_EXPERT = D // 2
EPS = 1e-6

LANES = 128
SUBLANES = 8
CONV_ROWS = 128
SC_CORES = 2
SC_SUBCORES = 16
TM = 512
TILES_PER_SEQ = SEQ // TM
T_PROMPT = BATCH * SEQ
T_ALL = T_PROMPT + DEC_BATCH
N_TILES = -(-T_ALL // TM)
T_PAD = N_TILES * TM
N_PROMPT_TILES = T_PROMPT // TM
SAMPLE_ROWS = 32
BM = 512
N_BLOCKS = -(-(2 * T_PAD) // BM) + N_EXPERTS
P_ROWS = N_BLOCKS * BM
META_ROWS = 8
POOL_CARRY = 16
CONV_CARRY = 32
EXPERT_LANE0 = N_GROUPS
ROUTER_ROWS = 48
VMEM_LIMIT = 56 * 1024 * 1024


def _cparams(*sem):
    return pltpu.CompilerParams(dimension_semantics=sem, vmem_limit_bytes=VMEM_LIMIT)


def _rms(x, g):
    return x * lax.rsqrt(jnp.mean(x * x, axis=-1, keepdims=True) + EPS) * g


def _ln(x, g, b):
    mu = jnp.mean(x, axis=-1, keepdims=True)
    xc = x - mu
    var = jnp.mean(xc * xc, axis=-1, keepdims=True)
    return xc * lax.rsqrt(var + EPS) * g + b


def _dot(a, b):
    return jnp.dot(a, b, preferred_element_type=F32)


def _pack_bf16_pairs(v):
    w = v.shape[1] // 2
    lo = lax.bitcast_convert_type(v[:, :w].astype(BF16).astype(F32), jnp.uint32)
    hi = lax.bitcast_convert_type(v[:, w:].astype(BF16).astype(F32), jnp.uint32)
    return lax.bitcast_convert_type(lax.shift_right_logical(lo, jnp.uint32(16)) | hi, jnp.int32)


def _unpack_bf16_pairs(words, dtype):
    u = lax.bitcast_convert_type(words, jnp.uint32)
    lo = lax.bitcast_convert_type(lax.shift_left(u, jnp.uint32(16)), F32)
    hi = lax.bitcast_convert_type(u & jnp.uint32(0xFFFF0000), F32)
    return jnp.concatenate([lo.astype(dtype), hi.astype(dtype)], axis=1)


def _ada_kernel(c_ref, w_ref, b_ref, op_ref, os_ref):
    c = c_ref[...]
    s = (c * jax.nn.sigmoid(c)).astype(BF16)
    mod = _dot(s, w_ref[...].astype(BF16)) + b_ref[...]
    op_ref[...] = mod[0:BATCH, :]
    os_ref[...] = mod[BATCH:, :]


def _ada_mods(c_all, ada_w, ada_b):
    n = c_all.shape[0]
    assert n == BATCH + DEC_BATCH and BATCH % SUBLANES == 0
    tn = 2048
    return pl.pallas_call(
        _ada_kernel,
        grid=(DEPTH, 6 * D // tn),
        in_specs=[
            pl.BlockSpec((n, D), lambda l, j: (0, 0)),
            pl.BlockSpec((None, D, tn), lambda l, j: (l, 0, j)),
            pl.BlockSpec((None, 1, tn), lambda l, j: (l, 0, j)),
        ],
        out_specs=[
            pl.BlockSpec((None, BATCH, tn), lambda l, j: (l, 0, j)),
            pl.BlockSpec((None, DEC_BATCH, tn), lambda l, j: (l, 0, j)),
        ],
        out_shape=[
            jax.ShapeDtypeStruct((DEPTH, BATCH, 6 * D), F32),
            jax.ShapeDtypeStruct((DEPTH, DEC_BATCH, 6 * D), F32),
        ],
        compiler_params=_cparams("parallel", "parallel"),
        name="ada_mods",
    )(c_all, ada_w, ada_b.reshape(DEPTH, 1, 6 * D))


def _mod_specs_prompt(layer, chunks):
    return [
        pl.BlockSpec((None, None, 1, D),
                     lambda i, c=c: (layer, jnp.minimum(i // TILES_PER_SEQ, BATCH - 1), 0, c))
        for c in chunks
    ]


def _mod_specs_sample(layer, chunks, rows, row_block0=0):
    return [
        pl.BlockSpec((None, rows, D), lambda i, c=c: (layer, i + row_block0, c)) for c in chunks
    ]


def _full(shape):
    nd = len(shape)
    return pl.BlockSpec(shape, lambda i: (0,) * nd)


def _layer_block(shape, layer):
    nd = len(shape)
    return pl.BlockSpec((None,) + shape, lambda i: (layer,) + (0,) * nd)


def _plus_moe(x, g0_ref, g1_ref, wts_ref, gate):
    w = wts_ref[...]
    y0 = _unpack_bf16_pairs(g0_ref[...], F32)
    y1 = _unpack_bf16_pairs(g1_ref[...], F32)
    return x + gate * (w[:, 0:1] * y0 + w[:, 1:2] * y1)


def _residual_in(refs, pending):
    if not pending:
        return refs[0][...], refs[1:]
    x_ref, g0_ref, g1_ref, wts_ref, gate_ref = refs[:5]
    return _plus_moe(x_ref[...], g0_ref, g1_ref, wts_ref, gate_ref[...]), refs[5:]


def _pending_prompt(pending, mod_p, layer):
    if pending is None:
        return [], []
    gathered, wts = pending
    specs = [
        pl.BlockSpec((TM, D // 2), lambda i: (i, 0)),
        pl.BlockSpec((TM, D // 2), lambda i: (i + N_TILES, 0)),
        pl.BlockSpec((TM, LANES), lambda i: (i, 0)),
    ] + _mod_specs_prompt(layer - 1, (5,))
    return specs, [gathered, gathered, wts, mod_p]


def _pending_sample(pending, mod_s, layer, rows):
    if pending is None:
        return [], []
    gathered, wts = pending
    first = T_PROMPT // rows
    second = (T_PAD + T_PROMPT) // rows
    specs = [
        pl.BlockSpec((rows, D // 2), lambda i: (first + i, 0)),
        pl.BlockSpec((rows, D // 2), lambda i: (second + i, 0)),
        pl.BlockSpec((rows, LANES), lambda i: (first + i, 0)),
    ] + _mod_specs_sample(layer - 1, (5,), rows)
    return specs, [gathered, gathered, wts, mod_s]


def _mix_a_front(x, sh, sc, ng, win_ref, lng_ref, lnb_ref):
    h = (_rms(x, ng) * (1.0 + sc) + sh).astype(BF16)
    z = jax.nn.gelu(_dot(h, win_ref[...]))
    u = z[:, :D]
    v = _ln(z[:, D:], lng_ref[...], lnb_ref[...])
    return u, v


def _mix_a_prompt_kernel(*refs, pending, first, stack):
    x, refs = _residual_in(refs, pending)
    (sh_ref, sc_ref, gt_ref, ng_ref, win_ref, lng_ref, lnb_ref, ws_ref, bs_ref, wout_ref,
     *cv_prev_ref, x1_ref, cv_ref, mixed_ref) = refs
    i = pl.program_id(0)
    u, v = _mix_a_front(x, sh_ref[...], sc_ref[...], ng_ref[...], win_ref, lng_ref, lnb_ref)
    vb = v.astype(BF16)
    n_chunks = TM // CHUNK
    row = lax.broadcasted_iota(jnp.int32, (CHUNK, CHUNK), 0)
    col = lax.broadcasted_iota(jnp.int32, (CHUNK, CHUNK), 1)
    tril = row >= col
    hd_dim = D // A_HEADS
    for hd in range(A_HEADS):
        ws = jnp.where(tril, ws_ref[hd], 0.0).astype(BF16)
        cols = slice(hd * hd_dim, (hd + 1) * hd_dim)
        vcat = jnp.concatenate([vb[c * CHUNK:(c + 1) * CHUNK, cols] for c in range(n_chunks)], axis=1)
        m = _dot(ws, vcat)
        for c in range(n_chunks):
            mixed_ref[c * CHUNK:(c + 1) * CHUNK, cols] = m[:, c * hd_dim:(c + 1) * hd_dim] + bs_ref[hd]
    y = _dot((u * mixed_ref[...]).astype(BF16), wout_ref[...])
    out = x + gt_ref[...] * y
    if first:
        out = jnp.where(i < N_PROMPT_TILES, out, 0.0)
    x1_ref[...] = out

    @pl.when(i % TILES_PER_SEQ == TILES_PER_SEQ - 1)
    def _():
        if stack:
            cv_ref[0] = cv_prev_ref[0][...]
            cv_ref[1] = v[TM - CHUNK:, :]
        else:
            cv_ref[...] = v[TM - CHUNK:, :]


def _mix_a_sample_kernel(*refs, pending, first):
    if first:
        refs = refs[:1] + refs[2:]
    x, refs = _residual_in(refs, pending)
    (sh_ref, sc_ref, gt_ref, ng_ref, win_ref, lng_ref, lnb_ref, wsd_ref, bsd_ref, wout_ref,
     x1_ref, cv_ref) = refs
    u, v = _mix_a_front(x, sh_ref[...], sc_ref[...], ng_ref[...], win_ref, lng_ref, lnb_ref)
    mixed = wsd_ref[...] * v + bsd_ref[...]
    y = _dot((u * mixed).astype(BF16), wout_ref[...])
    x1_ref[...] = x + gt_ref[...] * y
    cv_ref[...] = v


def _mixer_a(x_in, pending, cv_prev, mod_p, mod_s, ng, layer, la, w_in, ln_g, ln_b, w_s, b_s,
             w_out):
    first = isinstance(x_in, tuple)
    stack = cv_prev is not None
    assert not stack or la == 1
    x_all = x_in[0] if first else x_in
    bs_b = jnp.broadcast_to(b_s[la][:, :, None], (A_HEADS, CHUNK, D // A_HEADS))
    common = [
        _layer_block((D, 2 * D), la), _layer_block((1, D), la), _layer_block((1, D), la),
    ]
    has_pending = pending is not None
    pend_specs, pend_args = _pending_prompt(pending, mod_p, layer)
    last_tile = N_PROMPT_TILES - 1

    def seq_of(i):
        return jnp.minimum(i, last_tile) // TILES_PER_SEQ

    if stack:
        cv_in_specs = [pl.BlockSpec((None, CHUNK, D), lambda i: (seq_of(i), 0, 0))]
        cv_in_args = [cv_prev]
        cv_spec = pl.BlockSpec((2, None, CHUNK, D), lambda i: (0, seq_of(i), 0, 0))
        cv_shape = jax.ShapeDtypeStruct((2, BATCH, CHUNK, D), F32)
    else:
        cv_in_specs, cv_in_args = [], []
        cv_spec = pl.BlockSpec((None, CHUNK, D), lambda i: (seq_of(i), 0, 0))
        cv_shape = jax.ShapeDtypeStruct((BATCH, CHUNK, D), F32)
    x_all, cv_p = pl.pallas_call(
        functools.partial(_mix_a_prompt_kernel, pending=has_pending, first=first, stack=stack),
        grid=(N_TILES if first else N_PROMPT_TILES,),
        in_specs=[pl.BlockSpec((TM, D), lambda i: (jnp.minimum(i, last_tile), 0))] + pend_specs
        + _mod_specs_prompt(layer, (0, 1, 2))
        + [_full((1, D))] + common
        + [_layer_block((A_HEADS, CHUNK, CHUNK), la), _full((A_HEADS, CHUNK, D // A_HEADS)),
           _layer_block((D, D), la)] + cv_in_specs,
        out_specs=[pl.BlockSpec((TM, D), lambda i: (i, 0)), cv_spec],
        out_shape=[jax.ShapeDtypeStruct((T_PAD, D), F32), cv_shape],
        scratch_shapes=[pltpu.VMEM((TM, D), F32)],
        input_output_aliases={} if first else {0: 0},
        compiler_params=_cparams("arbitrary"),
        name="mixer_a_prompt",
    )(x_all, *pend_args, mod_p, mod_p, mod_p, ng, w_in, ln_g, ln_b, w_s, bs_b, w_out,
      *cv_in_args)

    wsd = jnp.repeat(w_s[la, :, 0, 0], D // A_HEADS).reshape(1, D)
    bsd = jnp.repeat(b_s[la, :, 0], D // A_HEADS).reshape(1, D)
    sblk = T_PROMPT // DEC_BATCH
    pend_specs, pend_args = _pending_sample(pending, mod_s, layer, DEC_BATCH)
    if first:
        x_specs = [pl.BlockSpec((DEC_BATCH, D), lambda i: (0, 0)), pl.BlockSpec(memory_space=pl.ANY)]
        x_args = [x_in[1], x_all]
    else:
        x_specs = [pl.BlockSpec((DEC_BATCH, D), lambda i: (sblk, 0))]
        x_args = [x_all]
    x_all, cv_s = pl.pallas_call(
        functools.partial(_mix_a_sample_kernel, pending=has_pending, first=first),
        grid=(1,),
        in_specs=x_specs + pend_specs
        + _mod_specs_sample(layer, (0, 1, 2), DEC_BATCH)
        + [_full((1, D))] + common
        + [_full((1, D)), _full((1, D)), _layer_block((D, D), la)],
        out_specs=[
            pl.BlockSpec((DEC_BATCH, D), lambda i: (sblk, 0)),
            pl.BlockSpec((DEC_BATCH, D), lambda i: (0, 0)),
        ],
        out_shape=[
            jax.ShapeDtypeStruct((T_PAD, D), F32),
            jax.ShapeDtypeStruct((DEC_BATCH, D), F32),
        ],
        input_output_aliases={len(x_args) - 1: 0},
        compiler_params=_cparams("arbitrary"),
        name="mixer_a_sample",
    )(*x_args, *pend_args, mod_s, mod_s, mod_s, ng, w_in, ln_g, ln_b, wsd, bsd, w_out)
    return x_all, cv_p, cv_s.reshape(DEC_BATCH, 1, D)


def _mix_b_tail(pooled_groups, wgrp_ref, bgrp_ref, scale_ref, wout_ref):
    outs = [
        _dot(pg.astype(BF16), wgrp_ref[g]) + bgrp_ref[g]
        for g, pg in enumerate(pooled_groups)
    ]
    mixed = jnp.concatenate(outs, axis=1) * scale_ref[...]
    return _dot(mixed.astype(BF16), wout_ref[...])


def _mix_b_prompt_kernel(*refs, pending):
    x, refs = _residual_in(refs, pending)
    (sh_ref, sc_ref, gt_ref, ng_ref, win_ref, wgrp_ref, bgrp_ref, scale_ref, wout_ref,
     x1_ref, st_ref, full_ref, s2_ref, s4_ref, s8_ref) = refs
    assert POOL_WINDOWS == (2, 4, 8, 16)
    i = pl.program_id(0)
    j = i % TILES_PER_SEQ
    h = (_rms(x, ng_ref[...]) * (1.0 + sc_ref[...]) + sh_ref[...]).astype(BF16)
    p = _dot(h, win_ref[...])
    gd = B_GROUP_DIM
    top = SUBLANES
    cur = top + POOL_CARRY
    rows = TM + POOL_CARRY

    @pl.when(i == 0)
    def _():
        for ref in (full_ref, s2_ref, s4_ref, s8_ref):
            ref[0:top, :] = jnp.zeros((top, ref.shape[1]), F32)

    full_ref[top:cur, :] = jnp.where(j == 0, 0.0, full_ref[top:cur, :])
    full_ref[cur:cur + TM, :] = p
    s2_ref[top:top + rows, :] = full_ref[top:top + rows, :] + full_ref[pl.ds(top - 1, rows), :]
    s4_ref[top:top + rows, :] = s2_ref[top:top + rows, gd:] + s2_ref[pl.ds(top - 2, rows), gd:]
    s8_ref[top:top + rows, :] = s4_ref[top:top + rows, gd:] + s4_ref[pl.ds(top - 4, rows), gd:]
    s16 = s8_ref[cur:cur + TM, gd:] + s8_ref[pl.ds(cur - 8, TM), gd:]
    sums = [s2_ref[cur:cur + TM, 0:gd], s4_ref[cur:cur + TM, 0:gd], s8_ref[cur:cur + TM, 0:gd], s16]
    pos = j * TM + lax.broadcasted_iota(jnp.int32, (TM, 1), 0)
    pooled = []
    for g, w in enumerate(POOL_WINDOWS):
        cnt = jnp.minimum(w, pos + 1).astype(F32)
        pooled.append(sums[g] / cnt - p[:, g * gd:(g + 1) * gd])
    y = _mix_b_tail(pooled, wgrp_ref, bgrp_ref, scale_ref, wout_ref)
    x1_ref[...] = x + gt_ref[...] * y

    @pl.when(j == TILES_PER_SEQ - 1)
    def _():
        st_ref[...] = full_ref[pl.ds(cur + TM - POOL_BUF, POOL_BUF), :]

    full_ref[top:cur, :] = full_ref[pl.ds(TM + top, POOL_CARRY), :]


def _mix_b_sample_kernel(*refs, pending):
    x, refs = _residual_in(refs, pending)
    (sh_ref, sc_ref, gt_ref, ng_ref, st_ref, win_ref, wgrp_ref, bgrp_ref, scale_ref, wout_ref,
     x1_ref, nst_ref) = refs
    h = (_rms(x, ng_ref[...]) * (1.0 + sc_ref[...]) + sh_ref[...]).astype(BF16)
    p = _dot(h, win_ref[...])
    pooled = []
    for g, w in enumerate(POOL_WINDOWS):
        s = p[:, g * B_GROUP_DIM:(g + 1) * B_GROUP_DIM]
        for k in range(1, w):
            s = s + st_ref[POOL_BUF - k, :, g * B_GROUP_DIM:(g + 1) * B_GROUP_DIM]
        cnt = float(min(w, PAST_LEN + 1))
        pooled.append(s / cnt - p[:, g * B_GROUP_DIM:(g + 1) * B_GROUP_DIM])
    y = _mix_b_tail(pooled, wgrp_ref, bgrp_ref, scale_ref, wout_ref)
    x1_ref[...] = x + gt_ref[...] * y
    for k in range(1, POOL_BUF):
        nst_ref[k - 1] = st_ref[k]
    nst_ref[POOL_BUF - 1] = p


def _mixer_b(x_all, pending, mod_p, mod_s, ng, layer, lb, state, w_in, w_grp, b_grp, scale,
             w_out):
    gd = B_GROUP_DIM
    common = [
        _layer_block((D, D), lb), _layer_block((len(POOL_WINDOWS), gd, gd), lb),
        _layer_block((len(POOL_WINDOWS), 1, gd), lb), _layer_block((1, D), lb),
        _layer_block((D, D), lb),
    ]
    has_pending = pending is not None
    pend_specs, pend_args = _pending_prompt(pending, mod_p, layer)
    x_all, st_p = pl.pallas_call(
        functools.partial(_mix_b_prompt_kernel, pending=has_pending),
        grid=(N_PROMPT_TILES,),
        in_specs=[pl.BlockSpec((TM, D), lambda i: (i, 0))] + pend_specs
        + _mod_specs_prompt(layer, (0, 1, 2)) + [_full((1, D))] + common,
        out_specs=[
            pl.BlockSpec((TM, D), lambda i: (i, 0)),
            pl.BlockSpec((None, POOL_BUF, D), lambda i: (i // TILES_PER_SEQ, 0, 0)),
        ],
        out_shape=[
            jax.ShapeDtypeStruct((T_PAD, D), F32),
            jax.ShapeDtypeStruct((BATCH, POOL_BUF, D), F32),
        ],
        scratch_shapes=[
            pltpu.VMEM((SUBLANES + POOL_CARRY + TM, D - g * gd), F32) for g in (0, 0, 1, 2)],
        input_output_aliases={0: 0},
        compiler_params=_cparams("arbitrary"),
        name="mixer_b_prompt",
    )(x_all, *pend_args, mod_p, mod_p, mod_p, ng, w_in, w_grp, b_grp, scale, w_out)

    r = SAMPLE_ROWS
    sblk = T_PROMPT // r
    pend_specs, pend_args = _pending_sample(pending, mod_s, layer, r)
    x_all, st_s = pl.pallas_call(
        functools.partial(_mix_b_sample_kernel, pending=has_pending),
        grid=(DEC_BATCH // r,),
        in_specs=[pl.BlockSpec((r, D), lambda i: (sblk + i, 0))] + pend_specs
        + _mod_specs_sample(layer, (0, 1, 2), r) + [_full((1, D))]
        + [pl.BlockSpec((None, POOL_BUF, r, D), lambda i: (lb, 0, i, 0))] + common,
        out_specs=[
            pl.BlockSpec((r, D), lambda i: (sblk + i, 0)),
            pl.BlockSpec((POOL_BUF, r, D), lambda i: (0, i, 0)),
        ],
        out_shape=[
            jax.ShapeDtypeStruct((T_PAD, D), F32),
            jax.ShapeDtypeStruct((POOL_BUF, DEC_BATCH, D), F32),
        ],
        input_output_aliases={0: 0},
        compiler_params=_cparams("arbitrary"),
        name="mixer_b_sample",
    )(x_all, *pend_args, mod_s, mod_s, mod_s, ng, state, w_in, w_grp, b_grp, scale, w_out)
    return x_all, st_p, jnp.transpose(st_s, (1, 0, 2))


def _mix_c_glu(x, sh, sc, ng, win_ref, bin_ref):
    h = (_rms(x, ng) * (1.0 + sc) + sh).astype(BF16)
    ag = _dot(h, win_ref[...]) + bin_ref[...]
    return ag[:, :D] * jax.nn.sigmoid(ag[:, D:])


def _mix_c_tail(conv, lng_ref, lnb_ref, wout_ref, bout_ref):
    z = _ln(conv, lng_ref[...], lnb_ref[...])
    z = z * jax.nn.sigmoid(z)
    return _dot(z.astype(BF16), wout_ref[...]) + bout_ref[...]


def _mix_c_prompt_kernel(*refs, pending):
    x, refs = _residual_in(refs, pending)
    (sh_ref, sc_ref, gt_ref, ng_ref, win_ref, bin_ref, wdw_ref, bdw_ref, lng_ref, lnb_ref,
     wout_ref, bout_ref, x1_ref, st_ref, full_ref, conv_ref, *shift_refs) = refs
    i = pl.program_id(0)
    j = i % TILES_PER_SEQ
    glu = _mix_c_glu(x, sh_ref[...], sc_ref[...], ng_ref[...], win_ref, bin_ref)

    @pl.when(j == 0)
    def _():
        full_ref[0:CONV_CARRY, :] = jnp.zeros((CONV_CARRY, D), F32)

    full_ref[CONV_CARRY:CONV_CARRY + TM, :] = glu
    off = CONV_CARRY - CONV_BUF
    span = TM + (off + CONV_BUF) // SUBLANES * SUBLANES - SUBLANES
    for c in range(D // LANES):
        cols = slice(c * LANES, (c + 1) * LANES)
        shift_ref = shift_refs[c % len(shift_refs)]
        for r in range(1, SUBLANES):
            shift_ref[r - 1] = full_ref[pl.ds(r, span), cols]
        for rb in range(TM // CONV_ROWS):
            acc = None
            for k in range(CONV_WIDTH):
                q, r = divmod(off + k, SUBLANES)
                start = q * SUBLANES + rb * CONV_ROWS
                if r == 0:
                    src = full_ref[pl.ds(start, CONV_ROWS), cols]
                else:
                    src = shift_ref[r - 1, pl.ds(start, CONV_ROWS), :]
                term = src * wdw_ref[k:k + 1, cols]
                acc = term if acc is None else acc + term
            conv_ref[pl.ds(rb * CONV_ROWS, CONV_ROWS), cols] = acc + bdw_ref[:, cols]
    y = _mix_c_tail(conv_ref[...], lng_ref, lnb_ref, wout_ref, bout_ref)
    x1_ref[...] = x + gt_ref[...] * y

    @pl.when(j == TILES_PER_SEQ - 1)
    def _():
        st_ref[...] = full_ref[pl.ds(CONV_CARRY + TM - CONV_BUF, CONV_BUF), :]

    full_ref[0:CONV_CARRY, :] = full_ref[pl.ds(TM, CONV_CARRY), :]


def _mix_c_sample_kernel(*refs, pending):
    x, refs = _residual_in(refs, pending)
    (sh_ref, sc_ref, gt_ref, ng_ref, st_ref, win_ref, bin_ref, wdw_ref, bdw_ref, lng_ref,
     lnb_ref, wout_ref, bout_ref, x1_ref, nst_ref) = refs
    glu = _mix_c_glu(x, sh_ref[...], sc_ref[...], ng_ref[...], win_ref, bin_ref)
    acc = glu * wdw_ref[CONV_BUF:CONV_BUF + 1, :]
    for k in range(CONV_BUF):
        acc = acc + st_ref[k] * wdw_ref[k:k + 1, :]
    y = _mix_c_tail(acc + bdw_ref[...], lng_ref, lnb_ref, wout_ref, bout_ref)
    x1_ref[...] = x + gt_ref[...] * y
    for k in range(1, CONV_BUF):
        nst_ref[k - 1] = st_ref[k]
    nst_ref[CONV_BUF - 1] = glu


def _mixer_c(x_all, pending, mod_p, mod_s, ng, layer, lc, state, w_in, b_in, w_dw, b_dw, ln_g,
             ln_b, w_out, b_out):
    common = [
        _layer_block((D, 2 * D), lc), _layer_block((1, 2 * D), lc),
        _layer_block((CONV_WIDTH, D), lc), _layer_block((1, D), lc), _layer_block((1, D), lc),
        _layer_block((1, D), lc), _layer_block((D, D), lc), _layer_block((1, D), lc),
    ]
    has_pending = pending is not None
    pend_specs, pend_args = _pending_prompt(pending, mod_p, layer)
    x_all, st_p = pl.pallas_call(
        functools.partial(_mix_c_prompt_kernel, pending=has_pending),
        grid=(N_PROMPT_TILES,),
        in_specs=[pl.BlockSpec((TM, D), lambda i: (i, 0))] + pend_specs
        + _mod_specs_prompt(layer, (0, 1, 2)) + [_full((1, D))] + common,
        out_specs=[
            pl.BlockSpec((TM, D), lambda i: (i, 0)),
            pl.BlockSpec((None, CONV_BUF, D), lambda i: (i // TILES_PER_SEQ, 0, 0)),
        ],
        out_shape=[
            jax.ShapeDtypeStruct((T_PAD, D), F32),
            jax.ShapeDtypeStruct((BATCH, CONV_BUF, D), F32),
        ],
        scratch_shapes=[pltpu.VMEM((TM + CONV_CARRY, D), F32), pltpu.VMEM((TM, D), F32)]
        + [pltpu.VMEM((SUBLANES - 1, TM + CONV_CARRY - SUBLANES, LANES), F32)] * 2,
        input_output_aliases={0: 0},
        compiler_params=_cparams("arbitrary"),
        name="mixer_c_prompt",
    )(x_all, *pend_args, mod_p, mod_p, mod_p, ng, w_in, b_in, w_dw, b_dw, ln_g, ln_b, w_out,
      b_out)

    r = SAMPLE_ROWS
    sblk = T_PROMPT // r
    pend_specs, pend_args = _pending_sample(pending, mod_s, layer, r)
    x_all, st_s = pl.pallas_call(
        functools.partial(_mix_c_sample_kernel, pending=has_pending),
        grid=(DEC_BATCH // r,),
        in_specs=[pl.BlockSpec((r, D), lambda i: (sblk + i, 0))] + pend_specs
        + _mod_specs_sample(layer, (0, 1, 2), r) + [_full((1, D))]
        + [pl.BlockSpec((None, CONV_BUF, r, D), lambda i: (lc, 0, i, 0))] + common,
        out_specs=[
            pl.BlockSpec((r, D), lambda i: (sblk + i, 0)),
            pl.BlockSpec((CONV_BUF, r, D), lambda i: (0, i, 0)),
        ],
        out_shape=[
            jax.ShapeDtypeStruct((T_PAD, D), F32),
            jax.ShapeDtypeStruct((CONV_BUF, DEC_BATCH, D), F32),
        ],
        input_output_aliases={0: 0},
        compiler_params=_cparams("arbitrary"),
        name="mixer_c_sample",
    )(x_all, *pend_args, mod_s, mod_s, mod_s, ng, state, w_in, b_in, w_dw, b_dw, ln_g, ln_b,
      w_out, b_out)
    return x_all, st_p, jnp.transpose(st_s, (1, 0, 2))


def _tile_mod(i, p_ref, s_ref):
    s_rows = jnp.concatenate([s_ref[...], jnp.zeros((TM - DEC_BATCH, D), F32)], axis=0)
    return jnp.where(i >= N_PROMPT_TILES, s_rows, p_ref[...])


def _unified_mod_specs(layer, chunks):
    specs = []
    for c in chunks:
        specs.append(pl.BlockSpec(
            (None, None, 1, D),
            lambda i, c=c: (layer, jnp.minimum(i // TILES_PER_SEQ, BATCH - 1), 0, c)))
        specs.append(pl.BlockSpec((None, DEC_BATCH, D), lambda i, c=c: (layer, 0, c)))
    return specs


def _router_kernel(x_ref, shp_ref, shs_ref, scp_ref, scs_ref, ng_ref, wr_ref, br_ref,
                   h2_ref, meta_ref, wts_ref, cnt_ref, carry_ref, before_ref):
    i = pl.program_id(0)

    @pl.when(i == 0)
    def _():
        carry_ref[...] = jnp.zeros((ROUTER_ROWS, 1), F32)
        earlier = lax.broadcasted_iota(jnp.int32, (TM, TM), 0)
        token = lax.broadcasted_iota(jnp.int32, (TM, TM), 1)
        before_ref[...] = (earlier < token).astype(BF16)

    sh = _tile_mod(i, shp_ref, shs_ref)
    sc = _tile_mod(i, scp_ref, scs_ref)
    h2 = _rms(x_ref[...], ng_ref[...]) * (1.0 + sc) + sh
    h2_ref[...] = _pack_bf16_pairs(h2)
    lg = lax.dot_general(wr_ref[...], h2.astype(BF16), (((1,), (1,)), ((), ())),
                         preferred_element_type=F32) + br_ref[...]
    row = lax.broadcasted_iota(jnp.int32, (ROUTER_ROWS, TM), 0)
    row_f = row.astype(F32)
    far = float(ROUTER_ROWS)
    neg = -jnp.inf
    is_g = row < N_GROUPS
    gm = jnp.where(is_g, lg, neg)
    gmax = jnp.max(gm, axis=0, keepdims=True)
    gsel = jnp.min(jnp.where(gm == gmax, row_f, far), axis=0, keepdims=True).astype(jnp.int32)
    gsum = jnp.sum(jnp.where(is_g, jnp.exp(lg - gmax), 0.0), axis=0, keepdims=True)
    g_w = 1.0 / gsum
    e_row = row - EXPERT_LANE0
    in_grp = (e_row >= 0) & (e_row < N_EXPERTS) & (
        lax.shift_right_arithmetic(e_row, EXPERTS_PER_GROUP.bit_length() - 1) == gsel)
    em = jnp.where(in_grp, lg, neg)
    m1 = jnp.max(em, axis=0, keepdims=True)
    i1 = jnp.min(jnp.where(em == m1, row_f, far), axis=0, keepdims=True).astype(jnp.int32)
    em2 = jnp.where(row == i1, neg, em)
    m2 = jnp.max(em2, axis=0, keepdims=True)
    i2 = jnp.min(jnp.where(em2 == m2, row_f, far), axis=0, keepdims=True).astype(jnp.int32)
    e2 = jnp.exp(m2 - m1)
    den = 1.0 + e2
    w1 = (1.0 / den) * g_w
    w2 = (e2 / den) * g_w

    hit1 = row == i1
    hit2 = row == i2
    assign = jnp.where(hit1 | hit2, 1.0, 0.0).astype(BF16)
    seen = _dot(assign, before_ref[...]) + carry_ref[...]
    rank1 = jnp.sum(jnp.where(hit1, seen, 0.0), axis=0, keepdims=True)
    rank2 = jnp.sum(jnp.where(hit2, seen, 0.0), axis=0, keepdims=True)
    carry_ref[...] = carry_ref[...] + jnp.sum(assign.astype(F32), axis=1, keepdims=True)
    cnt_ref[...] = carry_ref[...].astype(jnp.int32)

    ex1 = (i1 - EXPERT_LANE0).astype(F32)
    ex2 = (i2 - EXPERT_LANE0).astype(F32)
    field = lax.broadcasted_iota(jnp.int32, (META_ROWS, TM), 0)
    meta_ref[...] = jnp.where(field == 0, ex1, jnp.where(field == 1, ex2,
                              jnp.where(field == 2, rank1, jnp.where(field == 3, rank2, 0.0))))
    slot = lax.broadcasted_iota(jnp.int32, (LANES, TM), 0)
    wts_ref[...] = jnp.where(slot == 0, w1, jnp.where(slot == 1, w2, 0.0)).T


def _router(x_all, mod_p, mod_s, ng, layer, w_r, b_r):
    return pl.pallas_call(
        _router_kernel,
        grid=(N_TILES,),
        in_specs=[pl.BlockSpec((TM, D), lambda i: (i, 0))]
        + _unified_mod_specs(layer, (3, 4))
        + [_full((1, D)), _full((ROUTER_ROWS, D)), _full((ROUTER_ROWS, 1))],
        out_specs=[
            pl.BlockSpec((TM, D // 2), lambda i: (i, 0)),
            pl.BlockSpec((META_ROWS, TM), lambda i: (0, i)),
            pl.BlockSpec((TM, LANES), lambda i: (i, 0)),
            pl.BlockSpec((ROUTER_ROWS, 1), lambda i: (0, 0)),
        ],
        out_shape=[
            jax.ShapeDtypeStruct((T_PAD, D // 2), jnp.int32),
            jax.ShapeDtypeStruct((META_ROWS, T_PAD), F32),
            jax.ShapeDtypeStruct((T_PAD, LANES), F32),
            jax.ShapeDtypeStruct((ROUTER_ROWS, 1), jnp.int32),
        ],
        scratch_shapes=[pltpu.VMEM((ROUTER_ROWS, 1), F32), pltpu.VMEM((TM, TM), BF16)],
        compiler_params=_cparams("arbitrary"),
        name="router",
    )(x_all, mod_p, mod_s, mod_p, mod_s, ng, w_r, b_r)


SC_INDEX_MAX = 128
SC_ROWS_BYTES = 400 * 1024


def _gather_chunk(per_worker, row_bytes, buffers=1):
    top = min(SC_INDEX_MAX, SC_ROWS_BYTES // (row_bytes * buffers)) // SUBLANES * SUBLANES
    for ch in range(top, SUBLANES - 1, -SUBLANES):
        if per_worker % ch == 0:
            return ch
    raise ValueError(per_worker)


def _sc_gather(table, idx):
    nc, nw = SC_CORES, SC_CORES * SC_SUBCORES
    m = idx.shape[0]
    per_w = m // nw
    assert per_w * nw == m
    width = table.shape[1]
    ch = _gather_chunk(per_w, width * table.dtype.itemsize, buffers=2)
    mesh = plsc.VectorSubcoreMesh(core_axis_name="c", subcore_axis_name="s")

    n_chunks = per_w // ch

    @functools.partial(
        pl.kernel,
        out_type=jax.ShapeDtypeStruct((m, width), table.dtype),
        mesh=mesh,
        scratch_types=[pltpu.VMEM((ch,), jnp.int32)] * 2
        + [pltpu.VMEM((ch, width), table.dtype)] * 2
        + [pltpu.SemaphoreType.DMA] * 4,
    )
    def gather_kernel(t_hbm, i_hbm, o_hbm, idx_a, idx_b, rows_a, rows_b, sg_a, sg_b, sw_a, sw_b):
        wid = lax.axis_index("s") * nc + lax.axis_index("c")
        base = wid * per_w
        idx_v, rows_v, sem_g, sem_w = (idx_a, idx_b), (rows_a, rows_b), (sg_a, sg_b), (sw_a, sw_b)

        def out_rows(j):
            return o_hbm.at[pl.ds(pl.multiple_of(base + j * ch, 8), ch)]

        gathers, writes = {}, {}
        for j in range(n_chunks):
            b = j % 2
            if j >= 2:
                writes[j - 2].wait()
            pltpu.sync_copy(i_hbm.at[pl.ds(pl.multiple_of(base + j * ch, 8), ch)], idx_v[b])
            gathers[j] = pltpu.async_copy(t_hbm.at[idx_v[b]], rows_v[b], sem_g[b])
            if j >= 1:
                gathers[j - 1].wait()
                writes[j - 1] = pltpu.async_copy(rows_v[1 - b], out_rows(j - 1), sem_w[1 - b])
        last = n_chunks - 1
        gathers[last].wait()
        writes[last] = pltpu.async_copy(rows_v[last % 2], out_rows(last), sem_w[last % 2])
        for j in range(max(last - 1, 0), n_chunks):
            writes[j].wait()

    return gather_kernel(table, idx)


def _sc_scatter2(rows, dest, n_out):
    nc, nw = SC_CORES, SC_CORES * SC_SUBCORES
    n, width = rows.shape
    assert dest.shape == (2 * n,) and n % SUBLANES == 0
    per_w = n // nw
    assert per_w * nw == n
    ch = _gather_chunk(per_w, width * rows.dtype.itemsize)
    mesh = plsc.VectorSubcoreMesh(core_axis_name="c", subcore_axis_name="s")

    @functools.partial(
        pl.kernel,
        out_type=jax.ShapeDtypeStruct((n_out, width), rows.dtype),
        mesh=mesh,
        scratch_types=[
            pltpu.VMEM((ch,), jnp.int32),
            pltpu.VMEM((ch,), jnp.int32),
            pltpu.VMEM((ch, width), rows.dtype),
            pltpu.SemaphoreType.DMA,
            pltpu.SemaphoreType.DMA,
        ],
    )
    def scatter_kernel(r_hbm, d_hbm, o_hbm, i0_v, i1_v, rows_v, sem0, sem1):
        wid = lax.axis_index("s") * nc + lax.axis_index("c")
        base = wid * per_w

        @pl.loop(0, per_w // ch)
        def _(j):
            off = pl.multiple_of(base + j * ch, 8)
            pltpu.sync_copy(d_hbm.at[pl.ds(off, ch)], i0_v)
            pltpu.sync_copy(d_hbm.at[pl.ds(pl.multiple_of(n + off, 8), ch)], i1_v)
            pltpu.sync_copy(r_hbm.at[pl.ds(off, ch)], rows_v)
            c0 = pltpu.async_copy(rows_v, o_hbm.at[i0_v], sem0)
            c1 = pltpu.async_copy(rows_v, o_hbm.at[i1_v], sem1)
            c0.wait()
            c1.wait()

    return scatter_kernel(rows, dest)


def _expert_kernel(be_ref, first_ref, nvalid_ref, next_ref, nused_ref, x_hbm, wg_hbm, wu_hbm,
                   wd_hbm, y_hbm, *scratch, layer):
    def step(x_ref, y_ref):
        _expert_block(pl.program_id(0), be_ref, first_ref, nvalid_ref, next_ref, nused_ref, x_ref,
                      wg_hbm, wu_hbm, wd_hbm, y_ref, *scratch, layer=layer)

    pltpu.emit_pipeline(
        step,
        grid=(N_BLOCKS,),
        in_specs=[pl.BlockSpec(
            (BM, D // 2), lambda b: (jnp.minimum(b, nused_ref[0] - 1), 0))],
        out_specs=[pl.BlockSpec((BM, D // 2), lambda b: (b, 0))],
    )(x_hbm, y_hbm)


def _expert_block(b, be_ref, first_ref, nvalid_ref, next_ref, nused_ref, x_ref, wg_hbm, wu_hbm,
                  wd_hbm, y_ref, wgf_ref, wuf_ref, wdf_ref, wgb_ref, wub_ref, wdb_ref, sems,
                  *, layer):

    def weight_copies(e):
        return (
            pltpu.make_async_copy(wg_hbm.at[layer, e], wgf_ref, sems.at[0]),
            pltpu.make_async_copy(wu_hbm.at[layer, e], wuf_ref, sems.at[1]),
            pltpu.make_async_copy(wd_hbm.at[layer, e], wdf_ref, sems.at[2]),
        )

    @pl.when(b == 0)
    def _():
        for cp in weight_copies(be_ref[0]):
            cp.start()

    @pl.when(b < nused_ref[0])
    def _():
        @pl.when(first_ref[b] == 1)
        def _():
            for cp in weight_copies(be_ref[b]):
                cp.wait()
            wgb_ref[...] = wgf_ref[...].astype(BF16)
            wub_ref[...] = wuf_ref[...].astype(BF16)
            wdb_ref[...] = wdf_ref[...].astype(BF16)

            @pl.when(next_ref[b] >= 0)
            def _():
                for cp in weight_copies(next_ref[b]):
                    cp.start()

        def run(rows):
            live = lax.broadcasted_iota(jnp.int32, (rows, 1), 0) < nvalid_ref[b]
            x = _unpack_bf16_pairs(jnp.where(live, x_ref[0:rows, :], 0), BF16)
            g = _dot(x, wgb_ref[...])
            u = _dot(x, wub_ref[...])
            hmid = (g * jax.nn.sigmoid(g)) * u
            y_ref[0:rows, :] = _pack_bf16_pairs(_dot(hmid.astype(BF16), wdb_ref[...]))

        half = BM // 2

        @pl.when(nvalid_ref[b] > half)
        def _():
            run(BM)

        @pl.when(nvalid_ref[b] <= half)
        def _():
            run(half)
            y_ref[half:BM, :] = jnp.zeros((BM - half, D // 2), jnp.int32)

    @pl.when(b >= nused_ref[0])
    def _():
        y_ref[...] = jnp.zeros((BM, D // 2), jnp.int32)


def _experts(x_sorted, block_expert, first, nvalid, next_expert, nused, layer, w_gate, w_up,
             w_down):
    grid_spec = pltpu.PrefetchScalarGridSpec(
        num_scalar_prefetch=5,
        grid=(1,),
        in_specs=[pl.BlockSpec(memory_space=pl.ANY)] * 4,
        out_specs=pl.BlockSpec(memory_space=pl.ANY),
        scratch_shapes=[
            pltpu.VMEM((D, D_EXPERT), F32),
            pltpu.VMEM((D, D_EXPERT), F32),
            pltpu.VMEM((D_EXPERT, D), F32),
            pltpu.VMEM((D, D_EXPERT), BF16),
            pltpu.VMEM((D, D_EXPERT), BF16),
            pltpu.VMEM((D_EXPERT, D), BF16),
            pltpu.SemaphoreType.DMA((3,)),
        ],
    )
    return pl.pallas_call(
        functools.partial(_expert_kernel, layer=layer),
        grid_spec=grid_spec,
        out_shape=jax.ShapeDtypeStruct((P_ROWS, D // 2), jnp.int32),
        compiler_params=_cparams("arbitrary"),
        name="experts",
    )(block_expert, first, nvalid, next_expert, nused, x_sorted, w_gate, w_up, w_down)


def _combine_final_kernel(x_ref, g0_ref, g1_ref, wts_ref, gtp_ref, gts_ref, fg_ref, yp_ref,
                          ys_ref):
    i = pl.program_id(0)
    gate = _tile_mod(i, gtp_ref, gts_ref)
    y = _rms(_plus_moe(x_ref[...], g0_ref, g1_ref, wts_ref, gate), fg_ref[...])

    @pl.when(i < N_PROMPT_TILES)
    def _():
        yp_ref[...] = y

    @pl.when(i == N_PROMPT_TILES)
    def _():
        ys_ref[...] = y[0:DEC_BATCH, :]


def _combine_final(x_all, gathered, wts, mod_p, mod_s, layer, final_g):
    in_specs = [
        pl.BlockSpec((TM, D), lambda i: (i, 0)),
        pl.BlockSpec((TM, D // 2), lambda i: (i, 0)),
        pl.BlockSpec((TM, D // 2), lambda i: (i + N_TILES, 0)),
        pl.BlockSpec((TM, LANES), lambda i: (i, 0)),
    ] + _unified_mod_specs(layer, (5,))
    return pl.pallas_call(
        _combine_final_kernel,
        grid=(N_PROMPT_TILES + 1,),
        in_specs=in_specs + [_full((1, D))],
        out_specs=[
            pl.BlockSpec((TM, D), lambda i: (jnp.minimum(i, N_PROMPT_TILES - 1), 0)),
            pl.BlockSpec((DEC_BATCH, D), lambda i: (0, 0)),
        ],
        out_shape=[
            jax.ShapeDtypeStruct((T_PROMPT, D), F32),
            jax.ShapeDtypeStruct((DEC_BATCH, D), F32),
        ],
        compiler_params=_cparams("arbitrary"),
        name="combine_final",
    )(x_all, gathered, gathered, wts, mod_p, mod_s, final_g)


def _plan_kernel(cnt_ref, meta_ref, d_ref, be_ref, first_ref, nvalid_ref, next_ref,
                 nused_ref, pstart_ref, nxt_ref):
    shift = BM.bit_length() - 1
    assert 1 << shift == BM

    def count(e):
        return cnt_ref[EXPERT_LANE0 + e, 0]

    def scan_back(k, nxt):
        e = N_EXPERTS - 1 - k
        nxt_ref[e] = nxt
        return jnp.where(count(e) > 0, e, nxt)

    lax.fori_loop(0, N_EXPERTS, scan_back, jnp.int32(-1))

    def fill(e, blk):
        c = count(e)
        pstart_ref[e] = blk << shift

        def one(j, carry):
            be_ref[blk + j] = e
            first_ref[blk + j] = jnp.where(j == 0, 1, 0)
            nvalid_ref[blk + j] = jnp.minimum(c - (j << shift), BM)
            next_ref[blk + j] = nxt_ref[e]
            return carry

        n_blk = (c + (BM - 1)) >> shift
        lax.fori_loop(0, n_blk, one, 0)
        return blk + n_blk

    used = lax.fori_loop(0, N_EXPERTS, fill, jnp.int32(0))
    nused_ref[0] = used

    def tail(b, carry):
        be_ref[b] = 0
        first_ref[b] = 0
        nvalid_ref[b] = 0
        next_ref[b] = -1
        return carry

    lax.fori_loop(used, N_BLOCKS, tail, 0)

    m = meta_ref[...]
    expert = m[0:2, :]
    dest = m[2:4, :]
    for e in range(N_EXPERTS):
        dest = dest + jnp.where(expert == float(e), pstart_ref[e].astype(F32), 0.0)
    dest = dest.astype(jnp.int32)
    d_ref[:, 0:T_PAD] = dest[0:1, :]
    d_ref[:, T_PAD:2 * T_PAD] = dest[1:2, :]


def _plan(cnt, meta):
    smem = pl.BlockSpec(memory_space=pltpu.SMEM)
    blocks = jax.ShapeDtypeStruct((N_BLOCKS,), jnp.int32)
    return pl.pallas_call(
        _plan_kernel,
        grid=(1,),
        in_specs=[smem, _full((META_ROWS, T_PAD))],
        out_specs=[_full((1, 2 * T_PAD)), smem, smem, smem, smem, smem],
        out_shape=[
            jax.ShapeDtypeStruct((1, 2 * T_PAD), jnp.int32),
            blocks, blocks, blocks, blocks,
            jax.ShapeDtypeStruct((1,), jnp.int32),
        ],
        scratch_shapes=[pltpu.SMEM((N_EXPERTS,), jnp.int32), pltpu.SMEM((N_EXPERTS,), jnp.int32)],
        compiler_params=_cparams("arbitrary"),
        name="plan",
    )(cnt, meta)


def _moe(x_all, mod_p, mod_s, ng, layer, w_r, b_r, w_gate, w_up, w_down):
    h2, meta, wts, cnt = _router(x_all, mod_p, mod_s, ng, layer, w_r, b_r)
    dest, block_expert, first, nvalid, next_expert, nused = _plan(cnt, meta)
    dest = dest.reshape(2 * T_PAD)
    x_sorted = _sc_scatter2(h2, dest, P_ROWS)
    y_sorted = _experts(x_sorted, block_expert, first, nvalid, next_expert, nused, layer,
                        w_gate, w_up, w_down)
    gathered = _sc_gather(y_sorted, dest)
    return gathered, wts


def kernel(x_prompt, x_sample, c_prompt, c_sample, state_pool, state_conv, ada_w, ada_b, norm_g, final_g, a_w_in, a_ln_g, a_ln_b, a_w_s, a_b_s, a_w_out, b_w_in, b_w_grp, b_b_grp, b_scale, b_w_out, c_w_in, c_b_in, c_w_dw, c_b_dw, c_ln_g, c_ln_b, c_w_out, c_b_out, moe_w_grp, moe_b_grp, moe_w_exp, moe_b_exp, moe_w_gate, moe_w_up, moe_w_down):
    x_all = (x_prompt.reshape(T_PROMPT, D), x_sample.reshape(DEC_BATCH, D))
    mod_p, mod_s = _ada_mods(jnp.concatenate([c_prompt, c_sample], axis=0), ada_w, ada_b)
    mod_p = mod_p.reshape(DEPTH, BATCH, 1, 6 * D)

    a_w_in_b, a_w_out_b = a_w_in.astype(BF16), a_w_out.astype(BF16)
    b_w_in_b, b_w_grp_b, b_w_out_b = b_w_in.astype(BF16), b_w_grp.astype(BF16), b_w_out.astype(BF16)
    c_w_in_b, c_w_out_b = c_w_in.astype(BF16), c_w_out.astype(BF16)
    pad_rows = ROUTER_ROWS - N_GROUPS - N_EXPERTS
    w_r = jnp.pad(jnp.swapaxes(jnp.concatenate([moe_w_grp, moe_w_exp], axis=2), 1, 2),
                  ((0, 0), (0, pad_rows), (0, 0))).astype(BF16)
    b_r = jnp.pad(jnp.concatenate([moe_b_grp, moe_b_exp], axis=1), ((0, 0), (0, pad_rows)))

    new_a_p, new_a_s, new_b_p, new_b_s, new_c_p, new_c_s = [], [], [], [], [], []
    pool_tm = jnp.transpose(state_pool, (0, 2, 1, 3))
    conv_tm = jnp.transpose(state_conv, (0, 2, 1, 3))
    ia = ib = ic = 0
    assert len(range(0, DEPTH, 3)) == 2
    pending = None
    for layer in range(DEPTH):
        ng1 = norm_g[layer, 0].reshape(1, D)
        ng2 = norm_g[layer, 1].reshape(1, D)
        kind = layer % 3
        if kind == 0:
            x_all, st_p, st_s = _mixer_a(
                x_all, pending, new_a_p[0] if new_a_p else None, mod_p, mod_s, ng1, layer, ia,
                a_w_in_b, a_ln_g.reshape(-1, 1, D), a_ln_b.reshape(-1, 1, D), a_w_s, a_b_s,
                a_w_out_b)
            new_a_p = [st_p]
            new_a_s.append(st_s)
            ia += 1
        elif kind == 1:
            x_all, st_p, st_s = _mixer_b(
                x_all, pending, mod_p, mod_s, ng1, layer, ib, pool_tm, b_w_in_b, b_w_grp_b,
                b_b_grp.reshape(-1, len(POOL_WINDOWS), 1, B_GROUP_DIM),
                b_scale.reshape(-1, 1, D), b_w_out_b)
            new_b_p.append(st_p)
            new_b_s.append(st_s)
            ib += 1
        else:
            x_all, st_p, st_s = _mixer_c(
                x_all, pending, mod_p, mod_s, ng1, layer, ic, conv_tm, c_w_in_b,
                c_b_in.reshape(-1, 1, 2 * D), c_w_dw, c_b_dw.reshape(-1, 1, D),
                c_ln_g.reshape(-1, 1, D), c_ln_b.reshape(-1, 1, D), c_w_out_b,
                c_b_out.reshape(-1, 1, D))
            new_c_p.append(st_p)
            new_c_s.append(st_s)
            ic += 1
        pending = _moe(x_all, mod_p, mod_s, ng2, layer, w_r[layer],
                       b_r[layer].reshape(ROUTER_ROWS, 1), moe_w_gate, moe_w_up, moe_w_down)

    y_p, y_s = _combine_final(x_all, *pending, mod_p, mod_s, DEPTH - 1, final_g.reshape(1, D))
    return (y_p.reshape(BATCH, SEQ, D), y_s.reshape(DEC_BATCH, 1, D),
            new_a_p[0], jnp.stack(new_a_s), jnp.stack(new_b_p), jnp.stack(new_b_s),
            jnp.stack(new_c_p), jnp.stack(new_c_s))
```
